```python
import math
import jax
import jax.numpy as jnp
from jax import lax
import numpy as np

D_MODEL = 2048
BATCH = 8
SEQ = 4096
DEPTH = 4

CTX_LEN = 256
GRID_W = 64
D_MIX = D_MODEL
ATTN_W = D_MIX // 2
SSM_W = D_MIX // 4
GMLP_W = D_MIX - ATTN_W - SSM_W
HEAD_DIM = 64
N_HEADS = ATTN_W // HEAD_DIM
GQA_RATIO = 8
N_KV_HEADS = N_HEADS // GQA_RATIO
KV_W = N_KV_HEADS * HEAD_DIM
WINDOW = 128
ATTN_BLOCK = 128
ROPE_BASE = 10000.0
SSM_GROUP = 16
SSM_GROUPS = SSM_W // SSM_GROUP
SSM_STATE = 64
DT_MIN = 0.001
DT_MAX = 0.1
GMLP_CHUNK = 128
GMLP_GROUP_W = 128
GMLP_GROUPS = GMLP_W // GMLP_GROUP_W
D_FF = ((8 * D_MODEL // 3 + 255) // 256) * 256
CONV_W = 3
N_MOD = 6
NORM_EPS = 1e-6
OFF_K = ATTN_W
OFF_V = OFF_K + KV_W
OFF_S = OFF_V + KV_W
OFF_GU = OFF_S + SSM_W
OFF_GV = OFF_GU + GMLP_W
N_IN = OFF_GV + GMLP_W

kernel_name = 'hymba_style_s5_swa_gmlp_convffn_dit'


def rms_norm(x, g):
    xf = x.astype(jnp.float32)
    y = xf * lax.rsqrt(jnp.mean(xf * xf, axis=-1, keepdims=True) + NORM_EPS)
    return (y * g.astype(jnp.float32)).astype(x.dtype)


def layer_norm(x, g, b):
    xf = x.astype(jnp.float32)
    mu = jnp.mean(xf, axis=-1, keepdims=True)
    var = jnp.mean(jnp.square(xf - mu), axis=-1, keepdims=True)
    return ((xf - mu) * lax.rsqrt(var + NORM_EPS) * g.astype(jnp.float32) + b.astype(jnp.float32)).astype(x.dtype)


def modulate(h, shift, scale):
    return h * (1 + scale) + shift


def axial_rope_angles(rows):
    row = jnp.repeat(jnp.arange(rows), GRID_W)
    col = jnp.tile(jnp.arange(GRID_W), rows)
    n_freq = HEAD_DIM // 4
    inv_freq = ROPE_BASE ** (-jnp.arange(n_freq, dtype=jnp.float32) / n_freq)
    ang = jnp.stack([row, col], axis=-1).astype(jnp.float32)[:, :, None] * inv_freq
    return jnp.cos(ang), jnp.sin(ang)


def apply_rope(x, cos, sin):
    b, l, h, _ = x.shape
    xr = x.astype(jnp.float32).reshape(b, l, h, 2, 2, HEAD_DIM // 4)
    x1, x2 = xr[..., 0, :], xr[..., 1, :]
    cs, sn = cos[None, :, None], sin[None, :, None]
    out = jnp.stack([x1 * cs - x2 * sn, x1 * sn + x2 * cs], axis=-2)
    return out.reshape(b, l, h, HEAD_DIM).astype(x.dtype)


def window_attention(q, k, v, kc, vc, sink):
    b, l, _, _ = q.shape
    nb = l // ATTN_BLOCK
    n_c = kc.shape[1]
    scale = HEAD_DIM ** -0.5
    qb = q.reshape(b, nb, ATTN_BLOCK, N_KV_HEADS, GQA_RATIO, HEAD_DIM)
    pad = ((0, 0), (ATTN_BLOCK, ATTN_BLOCK), (0, 0), (0, 0))

    def band(t):
        tb = jnp.pad(t, pad).reshape(b, nb + 2, ATTN_BLOCK, N_KV_HEADS, HEAD_DIM)
        return jnp.concatenate([tb[:, :-2], tb[:, 1:-1], tb[:, 2:]], axis=2)

    kw, vw = band(k), band(v)
    s_win = jnp.einsum('bnqkgd,bnjkd->bnkgqj', qb, kw).astype(jnp.float32) * scale
    blk = jnp.arange(nb)[:, None, None]
    qpos = blk * ATTN_BLOCK + jnp.arange(ATTN_BLOCK)[None, :, None]
    kpos = (blk - 1) * ATTN_BLOCK + jnp.arange(3 * ATTN_BLOCK)[None, None, :]
    mask = (jnp.abs(kpos - qpos) <= WINDOW) & (kpos >= 0) & (kpos < l)
    s_win = jnp.where(mask[None, :, None, None], s_win, -jnp.inf)
    s_ctx = jnp.einsum('bnqkgd,bckd->bnkgqc', qb, kc).astype(jnp.float32) * scale
    s_sink = jnp.broadcast_to(sink.astype(jnp.float32).reshape(1, 1, N_KV_HEADS, GQA_RATIO, 1, 1),
                              s_win.shape[:-1] + (1,))
    p = jax.nn.softmax(jnp.concatenate([s_win, s_ctx, s_sink], axis=-1), axis=-1).astype(v.dtype)
    nw = 3 * ATTN_BLOCK
    o = (jnp.einsum('bnkgqj,bnjkd->bnqkgd', p[..., :nw], vw)
         + jnp.einsum('bnkgqc,bckd->bnqkgd', p[..., nw:nw + n_c], vc))
    return o.reshape(b, l, N_HEADS * HEAD_DIM)


def context_attention(qc, kc, vc, sink):
    b, n, _, _ = qc.shape
    qg = qc.reshape(b, n, N_KV_HEADS, GQA_RATIO, HEAD_DIM)
    s = jnp.einsum('bqkgd,bjkd->bkgqj', qg, kc).astype(jnp.float32) * HEAD_DIM ** -0.5
    s_sink = jnp.broadcast_to(sink.astype(jnp.float32).reshape(1, N_KV_HEADS, GQA_RATIO, 1, 1), s.shape[:-1] + (1,))
    p = jax.nn.softmax(jnp.concatenate([s, s_sink], axis=-1), axis=-1).astype(vc.dtype)
    o = jnp.einsum('bkgqj,bjkd->bqkgd', p[..., :n], vc)
    return o.reshape(b, n, N_HEADS * HEAD_DIM)


def s5_discretize(lam_re, lam_im, log_dt, b_re, b_im):
    lam_re = lam_re.astype(jnp.float32)
    lam_im = lam_im.astype(jnp.float32)
    dt = jnp.exp(log_dt.astype(jnp.float32))[:, None]
    mag = jnp.exp(lam_re * dt)
    a_re = mag * jnp.cos(lam_im * dt)
    a_im = mag * jnp.sin(lam_im * dt)
    den = lam_re * lam_re + lam_im * lam_im
    n_re = a_re - 1.0
    f_re = (n_re * lam_re + a_im * lam_im) / den
    f_im = (a_im * lam_re - n_re * lam_im) / den
    b_re = b_re.astype(jnp.float32)
    b_im = b_im.astype(jnp.float32)
    bb_re = f_re[..., None] * b_re - f_im[..., None] * b_im
    bb_im = f_re[..., None] * b_im + f_im[..., None] * b_re
    return a_re, a_im, bb_re, bb_im


def complex_affine_combine(e1, e2):
    a1r, a1i, b1r, b1i = e1
    a2r, a2i, b2r, b2i = e2
    return (a2r * a1r - a2i * a1i, a2r * a1i + a2i * a1r,
            a2r * b1r - a2i * b1i + b2r, a2r * b1i + a2i * b1r + b2i)


def s5_states(u, disc, h0, reverse):
    a_re, a_im, bb_re, bb_im = disc
    bu_re = jnp.einsum('gph,bngh->bngp', bb_re, u)
    bu_im = jnp.einsum('gph,bngh->bngp', bb_im, u)
    if h0 is not None:
        h_re, h_im = h0
        first = -1 if reverse else 0
        bu_re = bu_re.at[:, first].add(a_re * h_re - a_im * h_im)
        bu_im = bu_im.at[:, first].add(a_re * h_im + a_im * h_re)
    shape = bu_re.shape
    elems = (jnp.broadcast_to(a_re, shape), jnp.broadcast_to(a_im, shape), bu_re, bu_im)
    _, _, s_re, s_im = lax.associative_scan(complex_affine_combine, elems, reverse=reverse, axis=1)
    return s_re, s_im


def s5_readout(s_re, s_im, c_re, c_im):
    return jnp.einsum('ghp,bngp->bngh', c_re, s_re) - jnp.einsum('ghp,bngp->bngh', c_im, s_im)


def s5_glu(y, w_glu, b_glu):
    g = jax.nn.gelu(y)
    return g * jax.nn.sigmoid(g @ w_glu + b_glu)


def s5_mixer(u, uc, lam_re, lam_im, log_dt, b_re, b_im, c_re, c_im, d_skip, w_glu, b_glu, ctx_out):
    dtype = u.dtype
    bsz, n, _ = u.shape
    n_c = uc.shape[1]
    u4 = u.astype(jnp.float32).reshape(bsz, n, SSM_GROUPS, SSM_GROUP)
    uc4 = uc.astype(jnp.float32).reshape(bsz, n_c, SSM_GROUPS, SSM_GROUP)
    d4 = d_skip.astype(jnp.float32).reshape(SSM_GROUPS, SSM_GROUP)
    y = d4 * u4
    yc = d4 * uc4 if ctx_out else None
    for direction, rev in enumerate((False, True)):
        disc = s5_discretize(lam_re[direction], lam_im[direction], log_dt[direction], b_re[direction], b_im[direction])
        cr = c_re[direction].astype(jnp.float32)
        ci = c_im[direction].astype(jnp.float32)
        sc_re, sc_im = s5_states(uc4, disc, None, rev)
        end = 0 if rev else -1
        s_re, s_im = s5_states(u4, disc, (sc_re[:, end], sc_im[:, end]), rev)
        y = y + s5_readout(s_re, s_im, cr, ci)
        if ctx_out:
            yc = yc + s5_readout(sc_re, sc_im, cr, ci)
    out = s5_glu(y.reshape(bsz, n, SSM_W).astype(dtype), w_glu, b_glu)
    out_c = s5_glu(yc.reshape(bsz, n_c, SSM_W).astype(dtype), w_glu, b_glu) if ctx_out else None
    return out, out_c


def gmlp_spatial_gate(gu, gv, ln_g, ln_b, w_s, b_s):
    bsz, n, _ = gu.shape
    u = jax.nn.gelu(gu)
    v = layer_norm(jax.nn.gelu(gv), ln_g, ln_b)
    vch = v.reshape(bsz, n // GMLP_CHUNK, GMLP_CHUNK, GMLP_GROUPS, GMLP_GROUP_W)
    mixed = jnp.einsum('gij,bnjgc->bnigc', w_s, vch) + b_s.T[None, None, :, :, None]
    return u * mixed.reshape(bsz, n, GMLP_W)


def conv_ffn(h, w_up, conv_w, conv_b, w_down):
    n = h.shape[1]
    up = h @ w_up
    gate, val = up[..., :D_FF], up[..., D_FF:]
    half = CONV_W // 2
    gp = jnp.pad(gate, ((0, 0), (half, half), (0, 0)))
    gate = sum(gp[:, j:j + n] * conv_w[j] for j in range(CONV_W)) + conv_b
    return (jax.nn.silu(gate) * val) @ w_down


def _fwd_setup_inputs(seed: int = 0) -> dict:
    key = jax.random.key(seed)
    ks = iter(jax.random.split(key, 40))

    def nrm(shape, std):
        return std * jax.random.normal(next(ks), shape, jnp.float32)

    x = nrm((BATCH, SEQ, D_MODEL), 1.0)
    c = nrm((BATCH, D_MODEL), 1.0)
    ctx = nrm((BATCH, CTX_LEN, D_MODEL), 1.0)
    c_ctx = nrm((D_MODEL,), 1.0)
    w_ada = nrm((DEPTH, D_MODEL, N_MOD * D_MODEL), 0.5 * D_MODEL ** -0.5)
    b_ada = nrm((DEPTH, N_MOD * D_MODEL), 0.02)
    g_mix = 1.0 + nrm((DEPTH, D_MODEL), 0.02)
    g_ffn = 1.0 + nrm((DEPTH, D_MODEL), 0.02)
    w_in = nrm((DEPTH, D_MODEL, N_IN), D_MODEL ** -0.5)
    w_out = nrm((DEPTH, D_MIX, D_MODEL), D_MIX ** -0.5)
    attn_sink = nrm((DEPTH, N_HEADS), 0.5)
    ssm_shape = (DEPTH, 2, SSM_GROUPS, SSM_STATE)
    ssm_lambda_re = -0.5 + nrm(ssm_shape, 0.01)
    ssm_lambda_im = math.pi * jnp.arange(SSM_STATE, dtype=jnp.float32) + nrm(ssm_shape, 0.01)
    ssm_log_dt = jax.random.uniform(next(ks), (DEPTH, 2, SSM_GROUPS), jnp.float32,
                                    minval=math.log(DT_MIN), maxval=math.log(DT_MAX))
    ssm_b_re = nrm((DEPTH, 2, SSM_GROUPS, SSM_STATE, SSM_GROUP), SSM_GROUP ** -0.5)
    ssm_b_im = nrm((DEPTH, 2, SSM_GROUPS, SSM_STATE, SSM_GROUP), SSM_GROUP ** -0.5)
    ssm_c_re = nrm((DEPTH, 2, SSM_GROUPS, SSM_GROUP, SSM_STATE), SSM_STATE ** -0.5)
    ssm_c_im = nrm((DEPTH, 2, SSM_GROUPS, SSM_GROUP, SSM_STATE), SSM_STATE ** -0.5)
    ssm_d = nrm((DEPTH, SSM_W), 1.0)
    ssm_w_glu = nrm((DEPTH, SSM_W, SSM_W), SSM_W ** -0.5)
    ssm_b_glu = nrm((DEPTH, SSM_W), 0.02)
    gmlp_ln_g = 1.0 + nrm((DEPTH, GMLP_W), 0.02)
    gmlp_ln_b = nrm((DEPTH, GMLP_W), 0.02)
    gmlp_w_s = nrm((DEPTH, GMLP_GROUPS, GMLP_CHUNK, GMLP_CHUNK), 0.5 * GMLP_CHUNK ** -0.5)
    gmlp_b_s = 1.0 + nrm((DEPTH, GMLP_GROUPS, GMLP_CHUNK), 0.02)
    ffn_w_up = nrm((DEPTH, D_MODEL, 2 * D_FF), D_MODEL ** -0.5)
    ffn_conv_w = nrm((DEPTH, CONV_W, D_FF), CONV_W ** -0.5)
    ffn_conv_b = nrm((DEPTH, D_FF), 0.02)
    ffn_w_down = nrm((DEPTH, D_FF, D_MODEL), D_FF ** -0.5)
    g_final = 1.0 + nrm((D_MODEL,), 0.02)
    return {'x': x, 'c': c, 'ctx': ctx, 'c_ctx': c_ctx, 'w_ada': w_ada, 'b_ada': b_ada,
            'g_mix': g_mix, 'g_ffn': g_ffn, 'w_in': w_in, 'w_out': w_out, 'attn_sink': attn_sink,
            'ssm_lambda_re': ssm_lambda_re, 'ssm_lambda_im': ssm_lambda_im, 'ssm_log_dt': ssm_log_dt,
            'ssm_b_re': ssm_b_re, 'ssm_b_im': ssm_b_im, 'ssm_c_re': ssm_c_re, 'ssm_c_im': ssm_c_im,
            'ssm_d': ssm_d, 'ssm_w_glu': ssm_w_glu, 'ssm_b_glu': ssm_b_glu,
            'gmlp_ln_g': gmlp_ln_g, 'gmlp_ln_b': gmlp_ln_b, 'gmlp_w_s': gmlp_w_s, 'gmlp_b_s': gmlp_b_s,
            'ffn_w_up': ffn_w_up, 'ffn_conv_w': ffn_conv_w, 'ffn_conv_b': ffn_conv_b, 'ffn_w_down': ffn_w_down,
            'g_final': g_final}


def _fwd_reference(x, c, ctx, c_ctx, w_ada, b_ada, g_mix, g_ffn, w_in, w_out, attn_sink,
              ssm_lambda_re, ssm_lambda_im, ssm_log_dt, ssm_b_re, ssm_b_im, ssm_c_re, ssm_c_im,
              ssm_d, ssm_w_glu, ssm_b_glu, gmlp_ln_g, gmlp_ln_b, gmlp_w_s, gmlp_b_s,
              ffn_w_up, ffn_conv_w, ffn_conv_b, ffn_w_down, g_final):
    b, l, _ = x.shape
    n_c = ctx.shape[1]
    rows = l // GRID_W
    cos, sin = axial_rope_angles(rows)
    xc = ctx
    act_c = jax.nn.silu(c)
    act_cc = jax.nn.silu(c_ctx)
    for i in range(DEPTH):
        last = i == DEPTH - 1
        mod = (act_c @ w_ada[i] + b_ada[i])[:, None, :]
        mod_c = (act_cc @ w_ada[i] + b_ada[i])[None, None, :]
        shift_a, scale_a, gate_a, shift_f, scale_f, gate_f = jnp.split(mod, N_MOD, axis=-1)
        shift_ac, scale_ac, gate_ac, shift_fc, scale_fc, gate_fc = jnp.split(mod_c, N_MOD, axis=-1)

        h = modulate(rms_norm(x, g_mix[i]), shift_a, scale_a)
        hc = modulate(rms_norm(xc, g_mix[i]), shift_ac, scale_ac)
        z = h @ w_in[i]
        base = OFF_K if last else 0
        zc = hc @ (w_in[i][:, OFF_K:OFF_GU] if last else w_in[i])

        q = apply_rope(z[..., :OFF_K].reshape(b, l, N_HEADS, HEAD_DIM), cos, sin)
        k = apply_rope(z[..., OFF_K:OFF_V].reshape(b, l, N_KV_HEADS, HEAD_DIM), cos, sin)
        v = z[..., OFF_V:OFF_S].reshape(b, l, N_KV_HEADS, HEAD_DIM)
        kc = zc[..., OFF_K - base:OFF_V - base].reshape(b, n_c, N_KV_HEADS, HEAD_DIM)
        vc = zc[..., OFF_V - base:OFF_S - base].reshape(b, n_c, N_KV_HEADS, HEAD_DIM)
        o_attn = window_attention(q, k, v, kc, vc, attn_sink[i])

        o_ssm, o_ssm_c = s5_mixer(z[..., OFF_S:OFF_GU], zc[..., OFF_S - base:OFF_GU - base],
                                  ssm_lambda_re[i], ssm_lambda_im[i], ssm_log_dt[i], ssm_b_re[i], ssm_b_im[i],
                                  ssm_c_re[i], ssm_c_im[i], ssm_d[i], ssm_w_glu[i], ssm_b_glu[i], not last)

        o_gmlp = gmlp_spatial_gate(z[..., OFF_GU:OFF_GV], z[..., OFF_GV:], gmlp_ln_g[i], gmlp_ln_b[i],
                                   gmlp_w_s[i], gmlp_b_s[i])

        x = x + gate_a * (jnp.concatenate([o_attn, o_ssm, o_gmlp], axis=-1) @ w_out[i])
        x = x + gate_f * conv_ffn(modulate(rms_norm(x, g_ffn[i]), shift_f, scale_f),
                                  ffn_w_up[i], ffn_conv_w[i], ffn_conv_b[i], ffn_w_down[i])

        if not last:
            qc = zc[..., :OFF_K].reshape(b, n_c, N_HEADS, HEAD_DIM)
            o_attn_c = context_attention(qc, kc, vc, attn_sink[i])
            o_gmlp_c = gmlp_spatial_gate(zc[..., OFF_GU:OFF_GV], zc[..., OFF_GV:], gmlp_ln_g[i], gmlp_ln_b[i],
                                         gmlp_w_s[i], gmlp_b_s[i])
            xc = xc + gate_ac * (jnp.concatenate([o_attn_c, o_ssm_c, o_gmlp_c], axis=-1) @ w_out[i])
            xc = xc + gate_fc * conv_ffn(modulate(rms_norm(xc, g_ffn[i]), shift_fc, scale_fc),
                                         ffn_w_up[i], ffn_conv_w[i], ffn_conv_b[i], ffn_w_down[i])
    return rms_norm(x, g_final)


import jax as _jax
import jax.numpy as _jnp

TWIN_FORMAT = 'train_step'
FWD_PARAMS = ['x', 'c', 'ctx', 'c_ctx', 'w_ada', 'b_ada', 'g_mix', 'g_ffn', 'w_in', 'w_out', 'attn_sink', 'ssm_lambda_re', 'ssm_lambda_im', 'ssm_log_dt', 'ssm_b_re', 'ssm_b_im', 'ssm_c_re', 'ssm_c_im', 'ssm_d', 'ssm_w_glu', 'ssm_b_glu', 'gmlp_ln_g', 'gmlp_ln_b', 'gmlp_w_s', 'gmlp_b_s', 'ffn_w_up', 'ffn_conv_w', 'ffn_conv_b', 'ffn_w_down', 'g_final']
TWIN_WEIGHTS = ['c_ctx', 'w_ada', 'b_ada', 'g_mix', 'g_ffn', 'w_in', 'w_out', 'attn_sink', 'ssm_lambda_re', 'ssm_lambda_im', 'ssm_log_dt', 'ssm_b_re', 'ssm_b_im', 'ssm_c_re', 'ssm_c_im', 'ssm_d', 'ssm_w_glu', 'ssm_b_glu', 'gmlp_ln_g', 'gmlp_ln_b', 'gmlp_w_s', 'gmlp_b_s', 'ffn_w_up', 'ffn_conv_w', 'ffn_conv_b', 'ffn_w_down', 'g_final']
TWIN_DIFF_INPUT = 'x'
TWIN_INPUTS = ['x', 'c', 'ctx', 'c_ctx', 'w_ada', 'b_ada', 'g_mix', 'g_ffn', 'w_in', 'w_out', 'attn_sink', 'ssm_lambda_re', 'ssm_lambda_im', 'ssm_log_dt', 'ssm_b_re', 'ssm_b_im', 'ssm_c_re', 'ssm_c_im', 'ssm_d', 'ssm_w_glu', 'ssm_b_glu', 'gmlp_ln_g', 'gmlp_ln_b', 'gmlp_w_s', 'gmlp_b_s', 'ffn_w_up', 'ffn_conv_w', 'ffn_conv_b', 'ffn_w_down', 'g_final', 'loss_target', 'm_c_ctx', 'm_w_ada', 'm_b_ada', 'm_g_mix', 'm_g_ffn', 'm_w_in', 'm_w_out', 'm_attn_sink', 'm_ssm_lambda_re', 'm_ssm_lambda_im', 'm_ssm_log_dt', 'm_ssm_b_re', 'm_ssm_b_im', 'm_ssm_c_re', 'm_ssm_c_im', 'm_ssm_d', 'm_ssm_w_glu', 'm_ssm_b_glu', 'm_gmlp_ln_g', 'm_gmlp_ln_b', 'm_gmlp_w_s', 'm_gmlp_b_s', 'm_ffn_w_up', 'm_ffn_conv_w', 'm_ffn_conv_b', 'm_ffn_w_down', 'm_g_final', 'v_c_ctx', 'v_w_ada', 'v_b_ada', 'v_g_mix', 'v_g_ffn', 'v_w_in', 'v_w_out', 'v_attn_sink', 'v_ssm_lambda_re', 'v_ssm_lambda_im', 'v_ssm_log_dt', 'v_ssm_b_re', 'v_ssm_b_im', 'v_ssm_c_re', 'v_ssm_c_im', 'v_ssm_d', 'v_ssm_w_glu', 'v_ssm_b_glu', 'v_gmlp_ln_g', 'v_gmlp_ln_b', 'v_gmlp_w_s', 'v_gmlp_b_s', 'v_ffn_w_up', 'v_ffn_conv_w', 'v_ffn_conv_b', 'v_ffn_w_down', 'v_g_final']
TWIN_OUTPUTS = ['loss', 'grad_x', 'grad_c_ctx', 'grad_w_ada', 'grad_b_ada', 'grad_g_mix', 'grad_g_ffn', 'grad_w_in', 'grad_w_out', 'grad_attn_sink', 'grad_ssm_lambda_re', 'grad_ssm_lambda_im', 'grad_ssm_log_dt', 'grad_ssm_b_re', 'grad_ssm_b_im', 'grad_ssm_c_re', 'grad_ssm_c_im', 'grad_ssm_d', 'grad_ssm_w_glu', 'grad_ssm_b_glu', 'grad_gmlp_ln_g', 'grad_gmlp_ln_b', 'grad_gmlp_w_s', 'grad_gmlp_b_s', 'grad_ffn_w_up', 'grad_ffn_conv_w', 'grad_ffn_conv_b', 'grad_ffn_w_down', 'grad_g_final', 'delta_c_ctx', 'delta_w_ada', 'delta_b_ada', 'delta_g_mix', 'delta_g_ffn', 'delta_w_in', 'delta_w_out', 'delta_attn_sink', 'delta_ssm_lambda_re', 'delta_ssm_lambda_im', 'delta_ssm_log_dt', 'delta_ssm_b_re', 'delta_ssm_b_im', 'delta_ssm_c_re', 'delta_ssm_c_im', 'delta_ssm_d', 'delta_ssm_w_glu', 'delta_ssm_b_glu', 'delta_gmlp_ln_g', 'delta_gmlp_ln_b', 'delta_gmlp_w_s', 'delta_gmlp_b_s', 'delta_ffn_w_up', 'delta_ffn_conv_w', 'delta_ffn_conv_b', 'delta_ffn_w_down', 'delta_g_final', 'new_m_c_ctx', 'new_m_w_ada', 'new_m_b_ada', 'new_m_g_mix', 'new_m_g_ffn', 'new_m_w_in', 'new_m_w_out', 'new_m_attn_sink', 'new_m_ssm_lambda_re', 'new_m_ssm_lambda_im', 'new_m_ssm_log_dt', 'new_m_ssm_b_re', 'new_m_ssm_b_im', 'new_m_ssm_c_re', 'new_m_ssm_c_im', 'new_m_ssm_d', 'new_m_ssm_w_glu', 'new_m_ssm_b_glu', 'new_m_gmlp_ln_g', 'new_m_gmlp_ln_b', 'new_m_gmlp_w_s', 'new_m_gmlp_b_s', 'new_m_ffn_w_up', 'new_m_ffn_conv_w', 'new_m_ffn_conv_b', 'new_m_ffn_w_down', 'new_m_g_final', 'new_v_c_ctx', 'new_v_w_ada', 'new_v_b_ada', 'new_v_g_mix', 'new_v_g_ffn', 'new_v_w_in', 'new_v_w_out', 'new_v_attn_sink', 'new_v_ssm_lambda_re', 'new_v_ssm_lambda_im', 'new_v_ssm_log_dt', 'new_v_ssm_b_re', 'new_v_ssm_b_im', 'new_v_ssm_c_re', 'new_v_ssm_c_im', 'new_v_ssm_d', 'new_v_ssm_w_glu', 'new_v_ssm_b_glu', 'new_v_gmlp_ln_g', 'new_v_gmlp_ln_b', 'new_v_gmlp_w_s', 'new_v_gmlp_b_s', 'new_v_ffn_w_up', 'new_v_ffn_conv_w', 'new_v_ffn_conv_b', 'new_v_ffn_w_down', 'new_v_g_final']
TWIN_LEAF_KINDS = {'loss': 'loss', 'grad_x': 'grad_x', 'grad_c_ctx': 'grad_w', 'grad_w_ada': 'grad_w', 'grad_b_ada': 'grad_w', 'grad_g_mix': 'grad_w', 'grad_g_ffn': 'grad_w', 'grad_w_in': 'grad_w', 'grad_w_out': 'grad_w', 'grad_attn_sink': 'grad_w', 'grad_ssm_lambda_re': 'grad_w', 'grad_ssm_lambda_im': 'grad_w', 'grad_ssm_log_dt': 'grad_w', 'grad_ssm_b_re': 'grad_w', 'grad_ssm_b_im': 'grad_w', 'grad_ssm_c_re': 'grad_w', 'grad_ssm_c_im': 'grad_w', 'grad_ssm_d': 'grad_w', 'grad_ssm_w_glu': 'grad_w', 'grad_ssm_b_glu': 'grad_w', 'grad_gmlp_ln_g': 'grad_w', 'grad_gmlp_ln_b': 'grad_w', 'grad_gmlp_w_s': 'grad_w', 'grad_gmlp_b_s': 'grad_w', 'grad_ffn_w_up': 'grad_w', 'grad_ffn_conv_w': 'grad_w', 'grad_ffn_conv_b': 'grad_w', 'grad_ffn_w_down': 'grad_w', 'grad_g_final': 'grad_w', 'delta_c_ctx': 'delta_w', 'delta_w_ada': 'delta_w', 'delta_b_ada': 'delta_w', 'delta_g_mix': 'delta_w', 'delta_g_ffn': 'delta_w', 'delta_w_in': 'delta_w', 'delta_w_out': 'delta_w', 'delta_attn_sink': 'delta_w', 'delta_ssm_lambda_re': 'delta_w', 'delta_ssm_lambda_im': 'delta_w', 'delta_ssm_log_dt': 'delta_w', 'delta_ssm_b_re': 'delta_w', 'delta_ssm_b_im': 'delta_w', 'delta_ssm_c_re': 'delta_w', 'delta_ssm_c_im': 'delta_w', 'delta_ssm_d': 'delta_w', 'delta_ssm_w_glu': 'delta_w', 'delta_ssm_b_glu': 'delta_w', 'delta_gmlp_ln_g': 'delta_w', 'delta_gmlp_ln_b': 'delta_w', 'delta_gmlp_w_s': 'delta_w', 'delta_gmlp_b_s': 'delta_w', 'delta_ffn_w_up': 'delta_w', 'delta_ffn_conv_w': 'delta_w', 'delta_ffn_conv_b': 'delta_w', 'delta_ffn_w_down': 'delta_w', 'delta_g_final': 'delta_w', 'new_m_c_ctx': 'new_m', 'new_m_w_ada': 'new_m', 'new_m_b_ada': 'new_m', 'new_m_g_mix': 'new_m', 'new_m_g_ffn': 'new_m', 'new_m_w_in': 'new_m', 'new_m_w_out': 'new_m', 'new_m_attn_sink': 'new_m', 'new_m_ssm_lambda_re': 'new_m', 'new_m_ssm_lambda_im': 'new_m', 'new_m_ssm_log_dt': 'new_m', 'new_m_ssm_b_re': 'new_m', 'new_m_ssm_b_im': 'new_m', 'new_m_ssm_c_re': 'new_m', 'new_m_ssm_c_im': 'new_m', 'new_m_ssm_d': 'new_m', 'new_m_ssm_w_glu': 'new_m', 'new_m_ssm_b_glu': 'new_m', 'new_m_gmlp_ln_g': 'new_m', 'new_m_gmlp_ln_b': 'new_m', 'new_m_gmlp_w_s': 'new_m', 'new_m_gmlp_b_s': 'new_m', 'new_m_ffn_w_up': 'new_m', 'new_m_ffn_conv_w': 'new_m', 'new_m_ffn_conv_b': 'new_m', 'new_m_ffn_w_down': 'new_m', 'new_m_g_final': 'new_m', 'new_v_c_ctx': 'new_v', 'new_v_w_ada': 'new_v', 'new_v_b_ada': 'new_v', 'new_v_g_mix': 'new_v', 'new_v_g_ffn': 'new_v', 'new_v_w_in': 'new_v', 'new_v_w_out': 'new_v', 'new_v_attn_sink': 'new_v', 'new_v_ssm_lambda_re': 'new_v', 'new_v_ssm_lambda_im': 'new_v', 'new_v_ssm_log_dt': 'new_v', 'new_v_ssm_b_re': 'new_v', 'new_v_ssm_b_im': 'new_v', 'new_v_ssm_c_re': 'new_v', 'new_v_ssm_c_im': 'new_v', 'new_v_ssm_d': 'new_v', 'new_v_ssm_w_glu': 'new_v', 'new_v_ssm_b_glu': 'new_v', 'new_v_gmlp_ln_g': 'new_v', 'new_v_gmlp_ln_b': 'new_v', 'new_v_gmlp_w_s': 'new_v', 'new_v_gmlp_b_s': 'new_v', 'new_v_ffn_w_up': 'new_v', 'new_v_ffn_conv_w': 'new_v', 'new_v_ffn_conv_b': 'new_v', 'new_v_ffn_w_down': 'new_v', 'new_v_g_final': 'new_v'}


def _forward(args):
    return _fwd_reference(*[args[k] for k in FWD_PARAMS])


def _output_shape():
    def fwd():
        inp = _fwd_setup_inputs(0)
        return _fwd_reference(*[inp[k] for k in FWD_PARAMS])
    out = _jax.eval_shape(fwd)
    return out.shape, out.dtype

N_MICROBATCH = 1
ADAM_LR = 0.001
ADAM_B1 = 0.9
ADAM_B2 = 0.999
ADAM_EPS = 1e-08
ADAM_WD = 0.01
ADAM_STEP = 10
PER_EXAMPLE_BATCH_AXIS = {'x': 0, 'c': 0, 'ctx': 0, 'loss_target': 0}
SHARED_INPUTS = []
_WEIGHT_DTYPES = {'c_ctx': _jnp.float32, 'w_ada': _jnp.float32, 'b_ada': _jnp.float32, 'g_mix': _jnp.float32, 'g_ffn': _jnp.float32, 'w_in': _jnp.float32, 'w_out': _jnp.float32, 'attn_sink': _jnp.float32, 'ssm_lambda_re': _jnp.float32, 'ssm_lambda_im': _jnp.float32, 'ssm_log_dt': _jnp.float32, 'ssm_b_re': _jnp.float32, 'ssm_b_im': _jnp.float32, 'ssm_c_re': _jnp.float32, 'ssm_c_im': _jnp.float32, 'ssm_d': _jnp.float32, 'ssm_w_glu': _jnp.float32, 'ssm_b_glu': _jnp.float32, 'gmlp_ln_g': _jnp.float32, 'gmlp_ln_b': _jnp.float32, 'gmlp_w_s': _jnp.float32, 'gmlp_b_s': _jnp.float32, 'ffn_w_up': _jnp.float32, 'ffn_conv_w': _jnp.float32, 'ffn_conv_b': _jnp.float32, 'ffn_w_down': _jnp.float32, 'g_final': _jnp.float32}
MOMENT_SCALE = {'c_ctx': 7.846319e-03, 'w_ada': 2.168720e-02, 'b_ada': 3.729867e-02, 'g_mix': 1.350787e-02, 'g_ffn': 2.575202e-02, 'w_in': 1.249638e-02, 'w_out': 1.292865e-02, 'attn_sink': 1.049447e-04, 'ssm_lambda_re': 1.455016e-03, 'ssm_lambda_im': 1.296774e-03, 'ssm_log_dt': 5.378891e-01, 'ssm_b_re': 5.901204e-04, 'ssm_b_im': 6.107548e-04, 'ssm_c_re': 1.167183e-03, 'ssm_c_im': 1.202247e-03, 'ssm_d': 1.198906e-02, 'ssm_w_glu': 3.504163e-03, 'ssm_b_glu': 4.999333e-03, 'gmlp_ln_g': 9.541590e-03, 'gmlp_ln_b': 9.489798e-03, 'gmlp_w_s': 1.867897e-02, 'gmlp_b_s': 1.891073e-02, 'ffn_w_up': 1.134374e-02, 'ffn_conv_w': 1.152894e-02, 'ffn_conv_b': 1.012937e-02, 'ffn_w_down': 1.850478e-02, 'g_final': 1.600640e+01}


def _to_microbatches(a, axis):
    t = _jnp.moveaxis(a, axis, 0)
    t = t.reshape((N_MICROBATCH, t.shape[0] // N_MICROBATCH) + t.shape[1:])
    return _jnp.moveaxis(t, 1, axis + 1)


def setup_inputs(seed: int = 0) -> dict:
    inp = _fwd_setup_inputs(seed)
    key = _jax.random.fold_in(_jax.random.key(seed), 7919)
    shape, _ = _output_shape()
    out = dict(inp)
    out["loss_target"] = _jax.random.normal(_jax.random.fold_in(key, 0), shape, _jnp.float32)
    for i, name in enumerate(TWIN_WEIGHTS):
        w = inp[name].astype(_jnp.float32)
        if MOMENT_SCALE is None:
            s = _jnp.sqrt(_jnp.mean(_jnp.square(w)) + 1e-30)
        else:
            s = MOMENT_SCALE[name]
        km, kv = _jax.random.split(_jax.random.fold_in(key, i + 1))
        out[name] = w
        out["m_" + name] = s * _jax.random.normal(km, w.shape, _jnp.float32)
        out["v_" + name] = (s * s) * _jax.random.uniform(kv, w.shape, _jnp.float32, 0.5, 1.5)
    if N_MICROBATCH > 1:
        for name, axis in PER_EXAMPLE_BATCH_AXIS.items():
            out[name] = _to_microbatches(out[name], axis)
    return {'x': out['x'], 'c': out['c'], 'ctx': out['ctx'], 'c_ctx': out['c_ctx'], 'w_ada': out['w_ada'], 'b_ada': out['b_ada'], 'g_mix': out['g_mix'], 'g_ffn': out['g_ffn'], 'w_in': out['w_in'], 'w_out': out['w_out'], 'attn_sink': out['attn_sink'], 'ssm_lambda_re': out['ssm_lambda_re'], 'ssm_lambda_im': out['ssm_lambda_im'], 'ssm_log_dt': out['ssm_log_dt'], 'ssm_b_re': out['ssm_b_re'], 'ssm_b_im': out['ssm_b_im'], 'ssm_c_re': out['ssm_c_re'], 'ssm_c_im': out['ssm_c_im'], 'ssm_d': out['ssm_d'], 'ssm_w_glu': out['ssm_w_glu'], 'ssm_b_glu': out['ssm_b_glu'], 'gmlp_ln_g': out['gmlp_ln_g'], 'gmlp_ln_b': out['gmlp_ln_b'], 'gmlp_w_s': out['gmlp_w_s'], 'gmlp_b_s': out['gmlp_b_s'], 'ffn_w_up': out['ffn_w_up'], 'ffn_conv_w': out['ffn_conv_w'], 'ffn_conv_b': out['ffn_conv_b'], 'ffn_w_down': out['ffn_w_down'], 'g_final': out['g_final'], 'loss_target': out['loss_target'], 'm_c_ctx': out['m_c_ctx'], 'm_w_ada': out['m_w_ada'], 'm_b_ada': out['m_b_ada'], 'm_g_mix': out['m_g_mix'], 'm_g_ffn': out['m_g_ffn'], 'm_w_in': out['m_w_in'], 'm_w_out': out['m_w_out'], 'm_attn_sink': out['m_attn_sink'], 'm_ssm_lambda_re': out['m_ssm_lambda_re'], 'm_ssm_lambda_im': out['m_ssm_lambda_im'], 'm_ssm_log_dt': out['m_ssm_log_dt'], 'm_ssm_b_re': out['m_ssm_b_re'], 'm_ssm_b_im': out['m_ssm_b_im'], 'm_ssm_c_re': out['m_ssm_c_re'], 'm_ssm_c_im': out['m_ssm_c_im'], 'm_ssm_d': out['m_ssm_d'], 'm_ssm_w_glu': out['m_ssm_w_glu'], 'm_ssm_b_glu': out['m_ssm_b_glu'], 'm_gmlp_ln_g': out['m_gmlp_ln_g'], 'm_gmlp_ln_b': out['m_gmlp_ln_b'], 'm_gmlp_w_s': out['m_gmlp_w_s'], 'm_gmlp_b_s': out['m_gmlp_b_s'], 'm_ffn_w_up': out['m_ffn_w_up'], 'm_ffn_conv_w': out['m_ffn_conv_w'], 'm_ffn_conv_b': out['m_ffn_conv_b'], 'm_ffn_w_down': out['m_ffn_w_down'], 'm_g_final': out['m_g_final'], 'v_c_ctx': out['v_c_ctx'], 'v_w_ada': out['v_w_ada'], 'v_b_ada': out['v_b_ada'], 'v_g_mix': out['v_g_mix'], 'v_g_ffn': out['v_g_ffn'], 'v_w_in': out['v_w_in'], 'v_w_out': out['v_w_out'], 'v_attn_sink': out['v_attn_sink'], 'v_ssm_lambda_re': out['v_ssm_lambda_re'], 'v_ssm_lambda_im': out['v_ssm_lambda_im'], 'v_ssm_log_dt': out['v_ssm_log_dt'], 'v_ssm_b_re': out['v_ssm_b_re'], 'v_ssm_b_im': out['v_ssm_b_im'], 'v_ssm_c_re': out['v_ssm_c_re'], 'v_ssm_c_im': out['v_ssm_c_im'], 'v_ssm_d': out['v_ssm_d'], 'v_ssm_w_glu': out['v_ssm_w_glu'], 'v_ssm_b_glu': out['v_ssm_b_glu'], 'v_gmlp_ln_g': out['v_gmlp_ln_g'], 'v_gmlp_ln_b': out['v_gmlp_ln_b'], 'v_gmlp_w_s': out['v_gmlp_w_s'], 'v_gmlp_b_s': out['v_gmlp_b_s'], 'v_ffn_w_up': out['v_ffn_w_up'], 'v_ffn_conv_w': out['v_ffn_conv_w'], 'v_ffn_conv_b': out['v_ffn_conv_b'], 'v_ffn_w_down': out['v_ffn_w_down'], 'v_g_final': out['v_g_final']}


def _loss(weights, diff, rest, loss_target):
    with _jax.named_scope("forward"):
        args = {**rest, TWIN_DIFF_INPUT: diff, **{k: w.astype(_WEIGHT_DTYPES[k]) for k, w in weights.items()}}
        y = _forward(args)
    with _jax.named_scope("loss_head"):
        err = _jnp.square(y.astype(_jnp.float32) - loss_target)
        return 0.5 * _jnp.sum(_jnp.mean(err, axis=-1)) if err.ndim else 0.5 * err


def _adamw(w, g, m, v):
    m = ADAM_B1 * m + (1.0 - ADAM_B1) * g
    v = ADAM_B2 * v + (1.0 - ADAM_B2) * _jnp.square(g)
    m_hat = m / (1.0 - ADAM_B1 ** ADAM_STEP)
    v_hat = v / (1.0 - ADAM_B2 ** ADAM_STEP)
    delta = -ADAM_LR * (m_hat / (_jnp.sqrt(v_hat) + ADAM_EPS) + ADAM_WD * w)
    return delta, m, v


def reference(x, c, ctx, c_ctx, w_ada, b_ada, g_mix, g_ffn, w_in, w_out, attn_sink, ssm_lambda_re, ssm_lambda_im, ssm_log_dt, ssm_b_re, ssm_b_im, ssm_c_re, ssm_c_im, ssm_d, ssm_w_glu, ssm_b_glu, gmlp_ln_g, gmlp_ln_b, gmlp_w_s, gmlp_b_s, ffn_w_up, ffn_conv_w, ffn_conv_b, ffn_w_down, g_final, loss_target, m_c_ctx, m_w_ada, m_b_ada, m_g_mix, m_g_ffn, m_w_in, m_w_out, m_attn_sink, m_ssm_lambda_re, m_ssm_lambda_im, m_ssm_log_dt, m_ssm_b_re, m_ssm_b_im, m_ssm_c_re, m_ssm_c_im, m_ssm_d, m_ssm_w_glu, m_ssm_b_glu, m_gmlp_ln_g, m_gmlp_ln_b, m_gmlp_w_s, m_gmlp_b_s, m_ffn_w_up, m_ffn_conv_w, m_ffn_conv_b, m_ffn_w_down, m_g_final, v_c_ctx, v_w_ada, v_b_ada, v_g_mix, v_g_ffn, v_w_in, v_w_out, v_attn_sink, v_ssm_lambda_re, v_ssm_lambda_im, v_ssm_log_dt, v_ssm_b_re, v_ssm_b_im, v_ssm_c_re, v_ssm_c_im, v_ssm_d, v_ssm_w_glu, v_ssm_b_glu, v_gmlp_ln_g, v_gmlp_ln_b, v_gmlp_w_s, v_gmlp_b_s, v_ffn_w_up, v_ffn_conv_w, v_ffn_conv_b, v_ffn_w_down, v_g_final):
    given = dict(x=x, c=c, ctx=ctx, c_ctx=c_ctx, w_ada=w_ada, b_ada=b_ada, g_mix=g_mix, g_ffn=g_ffn, w_in=w_in, w_out=w_out, attn_sink=attn_sink, ssm_lambda_re=ssm_lambda_re, ssm_lambda_im=ssm_lambda_im, ssm_log_dt=ssm_log_dt, ssm_b_re=ssm_b_re, ssm_b_im=ssm_b_im, ssm_c_re=ssm_c_re, ssm_c_im=ssm_c_im, ssm_d=ssm_d, ssm_w_glu=ssm_w_glu, ssm_b_glu=ssm_b_glu, gmlp_ln_g=gmlp_ln_g, gmlp_ln_b=gmlp_ln_b, gmlp_w_s=gmlp_w_s, gmlp_b_s=gmlp_b_s, ffn_w_up=ffn_w_up, ffn_conv_w=ffn_conv_w, ffn_conv_b=ffn_conv_b, ffn_w_down=ffn_w_down, g_final=g_final, loss_target=loss_target, m_c_ctx=m_c_ctx, m_w_ada=m_w_ada, m_b_ada=m_b_ada, m_g_mix=m_g_mix, m_g_ffn=m_g_ffn, m_w_in=m_w_in, m_w_out=m_w_out, m_attn_sink=m_attn_sink, m_ssm_lambda_re=m_ssm_lambda_re, m_ssm_lambda_im=m_ssm_lambda_im, m_ssm_log_dt=m_ssm_log_dt, m_ssm_b_re=m_ssm_b_re, m_ssm_b_im=m_ssm_b_im, m_ssm_c_re=m_ssm_c_re, m_ssm_c_im=m_ssm_c_im, m_ssm_d=m_ssm_d, m_ssm_w_glu=m_ssm_w_glu, m_ssm_b_glu=m_ssm_b_glu, m_gmlp_ln_g=m_gmlp_ln_g, m_gmlp_ln_b=m_gmlp_ln_b, m_gmlp_w_s=m_gmlp_w_s, m_gmlp_b_s=m_gmlp_b_s, m_ffn_w_up=m_ffn_w_up, m_ffn_conv_w=m_ffn_conv_w, m_ffn_conv_b=m_ffn_conv_b, m_ffn_w_down=m_ffn_w_down, m_g_final=m_g_final, v_c_ctx=v_c_ctx, v_w_ada=v_w_ada, v_b_ada=v_b_ada, v_g_mix=v_g_mix, v_g_ffn=v_g_ffn, v_w_in=v_w_in, v_w_out=v_w_out, v_attn_sink=v_attn_sink, v_ssm_lambda_re=v_ssm_lambda_re, v_ssm_lambda_im=v_ssm_lambda_im, v_ssm_log_dt=v_ssm_log_dt, v_ssm_b_re=v_ssm_b_re, v_ssm_b_im=v_ssm_b_im, v_ssm_c_re=v_ssm_c_re, v_ssm_c_im=v_ssm_c_im, v_ssm_d=v_ssm_d, v_ssm_w_glu=v_ssm_w_glu, v_ssm_b_glu=v_ssm_b_glu, v_gmlp_ln_g=v_gmlp_ln_g, v_gmlp_ln_b=v_gmlp_ln_b, v_gmlp_w_s=v_gmlp_w_s, v_gmlp_b_s=v_gmlp_b_s, v_ffn_w_up=v_ffn_w_up, v_ffn_conv_w=v_ffn_conv_w, v_ffn_conv_b=v_ffn_conv_b, v_ffn_w_down=v_ffn_w_down, v_g_final=v_g_final)
    weights = {n: given[n] for n in TWIN_WEIGHTS}
    shared = {n: given[n] for n in SHARED_INPUTS}
    per_example = {n: given[n] for n in ['x', 'c', 'ctx']}
    grad_fn = _jax.value_and_grad(_loss, argnums=(0, 1))

    def one_microbatch(ex, loss_target):
        ex = dict(ex)
        diff = ex.pop(TWIN_DIFF_INPUT)
        return grad_fn(weights, diff, {**shared, **ex}, loss_target)

    if N_MICROBATCH == 1:
        loss, (grad_w, grad_x) = one_microbatch(per_example, given["loss_target"])
    else:
        def body(carry, xs):
            loss_sum, grad_sum = carry
            l_k, (gw_k, gx_k) = one_microbatch(xs[0], xs[1])
            with _jax.named_scope("update"):
                return (loss_sum + l_k, _jax.tree.map(_jnp.add, grad_sum, gw_k)), gx_k

        init = (_jnp.zeros((), _jnp.float32), _jax.tree.map(_jnp.zeros_like, weights))
        (loss, grad_w), grad_x = _jax.lax.scan(body, init, (per_example, given["loss_target"]))
    with _jax.named_scope("update"):
        delta_w, new_m, new_v = {}, {}, {}
        for n in TWIN_WEIGHTS:
            delta_w[n], new_m[n], new_v[n] = _adamw(weights[n], grad_w[n], given["m_" + n], given["v_" + n])
    return (loss, grad_x, *[grad_w[n] for n in TWIN_WEIGHTS], *[delta_w[n] for n in TWIN_WEIGHTS],
            *[new_m[n] for n in TWIN_WEIGHTS], *[new_v[n] for n in TWIN_WEIGHTS])
```

```python
import functools
import math

import jax
import jax.numpy as jnp
from jax import lax
from jax.experimental import pallas as pl
from jax.experimental.pallas import tpu as pltpu

F32 = jnp.float32
BF16 = jnp.bfloat16

D_MODEL = 2048
SEQ = 4096
DEPTH = 4
CTX_LEN = 256
GRID_W = 64
HEAD_DIM = 64
GQA_RATIO = 8
WINDOW = 128
ATTN_BLOCK = 128
ROPE_BASE = 10000.0
SSM_GROUP = 16
SSM_STATE = 64
GMLP_CHUNK = 128
GMLP_GROUP_W = 128
N_MOD = 6
NORM_EPS = 1e-6
ADAM_LR = 0.001
ADAM_B1 = 0.9
ADAM_B2 = 0.999
ADAM_EPS = 1e-08
ADAM_WD = 0.01
ADAM_STEP = 10

N_CHIPS = 4
N_DEV = 8
SCAN_SEGMENTS = 8
STATE_BLOCK = 256
LANES = 128
PACK_W = 1024
VMEM_LIMIT_MB = 56
_GELU_K = math.sqrt(2.0 / math.pi)
_GELU_C = 0.044715


class _Dims:
    def __init__(self):
        self.D = D_MODEL
        self.L = SEQ
        self.C = CTX_LEN
        self.T = SEQ + CTX_LEN
        self.ATTN_W = D_MODEL // 2
        self.SSM_W = D_MODEL // 4
        self.GMLP_W = D_MODEL - self.ATTN_W - self.SSM_W
        self.NH = self.ATTN_W // HEAD_DIM
        self.NKV = self.NH // GQA_RATIO
        self.KV_W = self.NKV * HEAD_DIM
        self.SG = self.SSM_W // SSM_GROUP
        self.GP = self.SG * SSM_STATE
        self.GG = self.GMLP_W // GMLP_GROUP_W
        self.DFF = ((8 * D_MODEL // 3 + 255) // 256) * 256
        self.OFF_K = self.ATTN_W
        self.OFF_V = self.OFF_K + self.KV_W
        self.OFF_S = self.OFF_V + self.KV_W
        self.OFF_GU = self.OFF_S + self.SSM_W
        self.OFF_GV = self.OFF_GU + self.GMLP_W
        self.N_IN = self.OFF_GV + self.GMLP_W
        assert self.NKV == 2 and self.KV_W == LANES
        assert self.C % (SCAN_SEGMENTS * 8) == 0 and self.L % (SCAN_SEGMENTS * 8) == 0
        assert self.C % ATTN_BLOCK == 0 and self.L % ATTN_BLOCK == 0


def _tile(n, cap, mult):
    best = None
    for t in range(mult, min(n, cap) + 1, mult):
        if n % t == 0:
            best = t
    return best if best is not None else n


def _cparams(sem=None):
    kw = dict(vmem_limit_bytes=VMEM_LIMIT_MB << 20)
    if sem is not None:
        kw["dimension_semantics"] = sem
    return pltpu.CompilerParams(**kw)


def _gelu(x):
    return 0.5 * x * (1.0 + jnp.tanh(_GELU_K * (x + _GELU_C * x * x * x)))


def _gelu_grad(x):
    t = jnp.tanh(_GELU_K * (x + _GELU_C * x * x * x))
    return 0.5 * (1.0 + t) + 0.5 * x * (1.0 - t * t) * _GELU_K * (1.0 + 3.0 * _GELU_C * x * x)


def _sigmoid(x):
    return 1.0 / (1.0 + jnp.exp(-x))


def _dot(a, b, dims):
    return lax.dot_general(a.astype(BF16), b.astype(BF16), (dims, ((), ())), preferred_element_type=F32)


_NN = ((1,), (0,))
_NT = ((1,), (1,))
_TN = ((0,), (0,))


def _comm(x, *, group, by_peer, name):
    n = 2 if group == "sib" else N_CHIPS
    blk = x.shape[1:] if by_peer else x.shape
    npeer = n - 1

    def body(x_ref, o_ref, send_sems, recv_sems, local_sem):
        mx, my, mc = lax.axis_index("x"), lax.axis_index("y"), lax.axis_index("c")
        if group == "sib":
            me = mc
            peers = [((mx, my, 1 - mc), 1 - mc)]
        else:
            me = 2 * mx + my
            peers = [((1 - mx, my, mc), 2 * (1 - mx) + my), ((mx, 1 - my, mc), 2 * mx + (1 - my)),
                     ((1 - mx, 1 - my, mc), 2 * (1 - mx) + (1 - my))]
        own = pltpu.make_async_copy(x_ref.at[me] if by_peer else x_ref, o_ref.at[me], local_sem)
        own.start()
        copies = []
        for k, (dev, pid) in enumerate(peers):
            cp = pltpu.make_async_remote_copy(
                src_ref=x_ref.at[pid] if by_peer else x_ref, dst_ref=o_ref.at[me],
                send_sem=send_sems.at[k], recv_sem=recv_sems.at[k],
                device_id=dev, device_id_type=pl.DeviceIdType.MESH)
            cp.start()
            copies.append(cp)
        for cp in copies:
            cp.wait_recv()
        for cp in copies:
            cp.wait_send()
        own.wait()

    return pl.pallas_call(
        body, name=name,
        out_shape=jax.ShapeDtypeStruct((n,) + tuple(blk), x.dtype),
        in_specs=[pl.BlockSpec(memory_space=pl.ANY)],
        out_specs=pl.BlockSpec(memory_space=pl.ANY),
        scratch_shapes=[pltpu.SemaphoreType.DMA((npeer,)), pltpu.SemaphoreType.DMA((npeer,)),
                        pltpu.SemaphoreType.DMA],
    )(x)


def _allgather8(x, name):
    a = _comm(x, group="chips", by_peer=False, name=name + "_chips")
    b = _comm(a, group="sib", by_peer=False, name=name + "_sib")
    return jnp.swapaxes(b, 0, 1).reshape((N_DEV,) + x.shape)


def _sum_slots(x, out_dtype, name):
    n, R, W = x.shape
    tr = _tile(R, 512, 16)

    def body(x_ref, o_ref):
        acc = x_ref[0].astype(F32)
        for s in range(1, n):
            acc = acc + x_ref[s].astype(F32)
        o_ref[...] = acc.astype(o_ref.dtype)

    return pl.pallas_call(
        body, name=name, grid=(R // tr,),
        in_specs=[pl.BlockSpec((n, tr, W), lambda i: (0, i, 0))],
        out_specs=pl.BlockSpec((tr, W), lambda i: (i, 0)),
        out_shape=jax.ShapeDtypeStruct((R, W), out_dtype),
        compiler_params=_cparams(("parallel",)),
    )(x)


def _pack(arrs, dtype, row_mult):
    flat = jnp.concatenate([a.reshape(-1).astype(dtype) for a in arrs])
    n = flat.shape[0]
    quant = row_mult * PACK_W
    total = -(-n // quant) * quant
    if total != n:
        flat = jnp.concatenate([flat, jnp.zeros((total - n,), dtype)])
    return flat.reshape(total // PACK_W, PACK_W)


def _unpack(buf, shapes):
    flat = buf.reshape(-1)
    out, off = [], 0
    for s in shapes:
        n = math.prod(s)
        out.append(flat[off:off + n].reshape(s))
        off += n
    return out


def _matmul(a, b, *, mode, name, out_dtype=F32, add=None, tm_cap=1088, tn_cap=1536, tk_cap=2048):
    if mode == "nn":
        (M, K), (_, N) = a.shape, b.shape
    elif mode == "nt":
        (M, K), (N, _) = a.shape, b.shape
    else:
        (K, M), (_, N) = a.shape, b.shape
    tm = _tile(M, tm_cap, 16 if mode != "tn" else LANES)
    tn = _tile(N, tn_cap, LANES)
    tk = _tile(K, tk_cap, LANES if mode != "tn" else 16)
    nk = K // tk
    dims = {"nn": _NN, "nt": _NT, "tn": _TN}[mode]

    def body(*refs):
        if add is None:
            a_ref, b_ref, o_ref, acc_ref = refs
            add_ref = None
        else:
            a_ref, b_ref, add_ref, o_ref, acc_ref = refs
        k = pl.program_id(2)

        @pl.when(k == 0)
        def _():
            acc_ref[...] = jnp.zeros_like(acc_ref)

        acc_ref[...] += _dot(a_ref[...], b_ref[...], dims)

        @pl.when(k == nk - 1)
        def _():
            r = acc_ref[...]
            if add_ref is not None:
                r = r + add_ref[...].astype(F32)
            o_ref[...] = r.astype(o_ref.dtype)

    if mode == "nn":
        a_spec = pl.BlockSpec((tm, tk), lambda i, j, k: (i, k))
        b_spec = pl.BlockSpec((tk, tn), lambda i, j, k: (k, j))
    elif mode == "nt":
        a_spec = pl.BlockSpec((tm, tk), lambda i, j, k: (i, k))
        b_spec = pl.BlockSpec((tn, tk), lambda i, j, k: (j, k))
    else:
        a_spec = pl.BlockSpec((tk, tm), lambda i, j, k: (k, i))
        b_spec = pl.BlockSpec((tk, tn), lambda i, j, k: (k, j))
    o_spec = pl.BlockSpec((tm, tn), lambda i, j, k: (i, j))
    in_specs = [a_spec, b_spec] + ([o_spec] if add is not None else [])
    args = (a, b) + ((add,) if add is not None else ())
    return pl.pallas_call(
        body, name=name, grid=(M // tm, N // tn, nk),
        in_specs=in_specs, out_specs=o_spec,
        out_shape=jax.ShapeDtypeStruct((M, N), out_dtype),
        scratch_shapes=[pltpu.VMEM((tm, tn), F32)],
        compiler_params=_cparams(("parallel", "parallel", "arbitrary")),
    )(*args)


def _matmul_residual(a, w, res, gate, C, name):
    M, K = a.shape
    N = w.shape[1]
    tm = _tile(M, 1088, 16)
    tn = _tile(N, 1024, LANES)
    tk = _tile(K, 2048, LANES)
    nk = K // tk

    def body(a_ref, w_ref, res_ref, gate_ref, mix_ref, o_ref, acc_ref):
        i, k = pl.program_id(0), pl.program_id(2)

        @pl.when(k == 0)
        def _():
            acc_ref[...] = jnp.zeros_like(acc_ref)

        acc_ref[...] += _dot(a_ref[...], w_ref[...], _NN)

        @pl.when(k == nk - 1)
        def _():
            r = acc_ref[...]
            row = i * tm + lax.broadcasted_iota(jnp.int32, (tm, 1), 0)
            g = jnp.where(row < C, gate_ref[0:1, :], gate_ref[1:2, :])
            mix_ref[...] = r
            o_ref[...] = res_ref[...] + g * r

    o_spec = pl.BlockSpec((tm, tn), lambda i, j, k: (i, j))
    return pl.pallas_call(
        body, name=name, grid=(M // tm, N // tn, nk),
        in_specs=[pl.BlockSpec((tm, tk), lambda i, j, k: (i, k)),
                  pl.BlockSpec((tk, tn), lambda i, j, k: (k, j)),
                  o_spec,
                  pl.BlockSpec((2, tn), lambda i, j, k: (0, j))],
        out_specs=[o_spec, o_spec],
        out_shape=[jax.ShapeDtypeStruct((M, N), F32), jax.ShapeDtypeStruct((M, N), F32)],
        scratch_shapes=[pltpu.VMEM((tm, tn), F32)],
        compiler_params=_cparams(("parallel", "parallel", "arbitrary")),
    )(a, w, res, gate)


def _seg_select(i, tm, C, ref):
    row = i * tm + lax.broadcasted_iota(jnp.int32, (tm, 1), 0)
    return row < C, jnp.where(row < C, ref[0:1, :], ref[1:2, :])


def _normmod(X, g, shift, scale, C, name):
    T, D = X.shape
    tm = _tile(T, 512, 16)

    def body(x_ref, g_ref, sh_ref, sc_ref, h_ref):
        i = pl.program_id(0)
        x = x_ref[...]
        r = lax.rsqrt(jnp.mean(x * x, axis=-1, keepdims=True) + NORM_EPS)
        _, sh = _seg_select(i, tm, C, sh_ref)
        _, sc = _seg_select(i, tm, C, sc_ref)
        h_ref[...] = ((x * r) * g_ref[...] * (1.0 + sc) + sh).astype(BF16)

    row = pl.BlockSpec((tm, D), lambda i: (i, 0))
    vec = pl.BlockSpec((1, D), lambda i: (0, 0))
    two = pl.BlockSpec((2, D), lambda i: (0, 0))
    return pl.pallas_call(
        body, name=name, grid=(T // tm,),
        in_specs=[row, vec, two, two], out_specs=row,
        out_shape=jax.ShapeDtypeStruct((T, D), BF16),
        compiler_params=_cparams(("parallel",)),
    )(X, g, shift, scale)


def _normmod_bwd(X, g, scale, dh, dres, C, name):
    T, D = X.shape
    tm = _tile(T, 512, 16)

    def body(x_ref, g_ref, sc_ref, dh_ref, dres_ref, dx_ref, dg_ref, dsh_ref, dsc_ref):
        i = pl.program_id(0)

        @pl.when(i == 0)
        def _():
            dg_ref[...] = jnp.zeros_like(dg_ref)
            dsh_ref[...] = jnp.zeros_like(dsh_ref)
            dsc_ref[...] = jnp.zeros_like(dsc_ref)

        x = x_ref[...]
        dh = dh_ref[...]
        gv = g_ref[...]
        r = lax.rsqrt(jnp.mean(x * x, axis=-1, keepdims=True) + NORM_EPS)
        xh = x * r
        isc, sc = _seg_select(i, tm, C, sc_ref)
        n = xh * gv
        dhn = dh * n
        zero = jnp.zeros_like(dh)
        dsh_ref[0:1, :] += jnp.sum(jnp.where(isc, dh, zero), axis=0, keepdims=True)
        dsh_ref[1:2, :] += jnp.sum(jnp.where(isc, zero, dh), axis=0, keepdims=True)
        dsc_ref[0:1, :] += jnp.sum(jnp.where(isc, dhn, zero), axis=0, keepdims=True)
        dsc_ref[1:2, :] += jnp.sum(jnp.where(isc, zero, dhn), axis=0, keepdims=True)
        dn = dh * (1.0 + sc)
        dg_ref[...] += jnp.sum(dn * xh, axis=0, keepdims=True)
        dxh = dn * gv
        dx_ref[...] = dres_ref[...] + r * (dxh - xh * jnp.mean(dxh * xh, axis=-1, keepdims=True))

    row = pl.BlockSpec((tm, D), lambda i: (i, 0))
    vec = pl.BlockSpec((1, D), lambda i: (0, 0))
    two = pl.BlockSpec((2, D), lambda i: (0, 0))
    return pl.pallas_call(
        body, name=name, grid=(T // tm,),
        in_specs=[row, vec, two, row, row], out_specs=[row, vec, two, two],
        out_shape=[jax.ShapeDtypeStruct((T, D), F32), jax.ShapeDtypeStruct((1, D), F32),
                   jax.ShapeDtypeStruct((2, D), F32), jax.ShapeDtypeStruct((2, D), F32)],
        compiler_params=_cparams(("arbitrary",)),
    )(X, g, scale, dh, dres)


def _gate_bwd(dxo, mix, gate, C, name):
    T, D = dxo.shape
    tm = _tile(T, 512, 16)

    def body(dx_ref, mix_ref, gate_ref, dmix_ref, dgate_ref):
        i = pl.program_id(0)

        @pl.when(i == 0)
        def _():
            dgate_ref[...] = jnp.zeros_like(dgate_ref)

        dx = dx_ref[...]
        isc, gt = _seg_select(i, tm, C, gate_ref)
        dmix_ref[...] = (dx * gt).astype(BF16)
        p = dx * mix_ref[...]
        zero = jnp.zeros_like(p)
        dgate_ref[0:1, :] += jnp.sum(jnp.where(isc, p, zero), axis=0, keepdims=True)
        dgate_ref[1:2, :] += jnp.sum(jnp.where(isc, zero, p), axis=0, keepdims=True)

    row = pl.BlockSpec((tm, D), lambda i: (i, 0))
    two = pl.BlockSpec((2, D), lambda i: (0, 0))
    return pl.pallas_call(
        body, name=name, grid=(T // tm,),
        in_specs=[row, row, two], out_specs=[row, two],
        out_shape=[jax.ShapeDtypeStruct((T, D), BF16), jax.ShapeDtypeStruct((2, D), F32)],
        compiler_params=_cparams(("arbitrary",)),
    )(dxo, mix, gate)


def _conv_masks(T, C):
    row = lax.broadcasted_iota(jnp.int32, (T, 1), 0)
    first = (row == 0) | (row == C)
    last = (row == C - 1) | (row == T - 1)
    return first, last


def _shift_rows(x, first, last):
    T = x.shape[0]
    prev = jnp.where(first, 0.0, pltpu.roll(x, 1, 0))
    nxt = jnp.where(last, 0.0, pltpu.roll(x, T - 1, 0))
    return prev, nxt


def _convact(upg, upv, cw, cb, C, name):
    T, F = upg.shape
    tc = LANES

    def body(g_ref, v_ref, w_ref, b_ref, o_ref):
        gp = g_ref[...]
        first, last = _conv_masks(T, C)
        prev, nxt = _shift_rows(gp, first, last)
        gc = prev * w_ref[0:1, :] + gp * w_ref[1:2, :] + nxt * w_ref[2:3, :] + b_ref[...]
        o_ref[...] = (gc * _sigmoid(gc) * v_ref[...]).astype(BF16)

    col = pl.BlockSpec((T, tc), lambda j: (0, j))
    return pl.pallas_call(
        body, name=name, grid=(F // tc,),
        in_specs=[col, col, pl.BlockSpec((3, tc), lambda j: (0, j)), pl.BlockSpec((1, tc), lambda j: (0, j))],
        out_specs=col, out_shape=jax.ShapeDtypeStruct((T, F), BF16),
        compiler_params=_cparams(("parallel",)),
    )(upg, upv, cw, cb)


def _convact_bwd(upg, upv, cw, cb, dact, C, name):
    T, F = upg.shape
    tc = LANES

    def body(g_ref, v_ref, w_ref, b_ref, da_ref, dg_ref, dv_ref, dw_ref, db_ref):
        gp = g_ref[...]
        val = v_ref[...]
        da = da_ref[...]
        first, last = _conv_masks(T, C)
        prev, nxt = _shift_rows(gp, first, last)
        w0, w1, w2 = w_ref[0:1, :], w_ref[1:2, :], w_ref[2:3, :]
        gc = prev * w0 + gp * w1 + nxt * w2 + b_ref[...]
        s = _sigmoid(gc)
        dv_ref[...] = (da * gc * s).astype(BF16)
        dgc = da * val * (s + gc * s * (1.0 - s))
        db_ref[...] = jnp.sum(dgc, axis=0, keepdims=True)
        dw_ref[0:1, :] = jnp.sum(dgc * prev, axis=0, keepdims=True)
        dw_ref[1:2, :] = jnp.sum(dgc * gp, axis=0, keepdims=True)
        dw_ref[2:3, :] = jnp.sum(dgc * nxt, axis=0, keepdims=True)
        dprev, dnxt = _shift_rows(dgc, first, last)
        dg_ref[...] = (dnxt * w0 + dgc * w1 + dprev * w2).astype(BF16)

    col = pl.BlockSpec((T, tc), lambda j: (0, j))
    w3 = pl.BlockSpec((3, tc), lambda j: (0, j))
    w1 = pl.BlockSpec((1, tc), lambda j: (0, j))
    return pl.pallas_call(
        body, name=name, grid=(F // tc,),
        in_specs=[col, col, w3, w1, col], out_specs=[col, col, w3, w1],
        out_shape=[jax.ShapeDtypeStruct((T, F), BF16), jax.ShapeDtypeStruct((T, F), BF16),
                   jax.ShapeDtypeStruct((3, F), F32), jax.ShapeDtypeStruct((1, F), F32)],
        compiler_params=_cparams(("parallel",)),
    )(upg, upv, cw, cb, dact)


def _rot_half(x):
    W = x.shape[-1]
    lane = lax.broadcasted_iota(jnp.int32, x.shape, x.ndim - 1)
    lo = (lane % 32) < 16
    return jnp.where(lo, -pltpu.roll(x, W - 16, x.ndim - 1), pltpu.roll(x, 16, x.ndim - 1))


def _swap_halves(x):
    return pltpu.roll(x, 64, x.ndim - 1)


def _rope_tables(dm):
    t = jnp.arange(dm.L)
    n_freq = HEAD_DIM // 4
    inv_freq = ROPE_BASE ** (-jnp.arange(n_freq, dtype=F32) / n_freq)
    ang_r = (t // GRID_W).astype(F32)[:, None] * inv_freq
    ang_c = (t % GRID_W).astype(F32)[:, None] * inv_freq
    ang = jnp.concatenate([ang_r, ang_r, ang_c, ang_c], axis=-1)
    cos = jnp.concatenate([jnp.ones((dm.C, HEAD_DIM), F32), jnp.cos(ang)], axis=0)
    sin = jnp.concatenate([jnp.zeros((dm.C, HEAD_DIM), F32), jnp.sin(ang)], axis=0)
    return jnp.tile(cos, (1, 2)), jnp.tile(sin, (1, 2))


def _attn_prep(z, cos, sin, dm, name):
    T = dm.T
    AW = dm.ATTN_W
    tm = _tile(T, 256, 16)
    rep = AW // LANES

    def body(z_ref, cos_ref, sin_ref, q_ref, k_ref, v_ref):
        cs, sn = cos_ref[...], sin_ref[...]
        q = z_ref[:, 0:AW]
        csq, snq = jnp.tile(cs, (1, rep)), jnp.tile(sn, (1, rep))
        q_ref[...] = (q * csq + _rot_half(q) * snq).astype(BF16)
        k = z_ref[:, dm.OFF_K:dm.OFF_V]
        k = k * cs + _rot_half(k) * sn
        v = z_ref[:, dm.OFF_V:dm.OFF_S]
        lo = lax.broadcasted_iota(jnp.int32, k.shape, 1) < HEAD_DIM
        ks, vs = _swap_halves(k), _swap_halves(v)
        k_ref[0] = jnp.where(lo, k, ks).astype(BF16)
        k_ref[1] = jnp.where(lo, ks, k).astype(BF16)
        v_ref[0] = jnp.where(lo, v, vs).astype(BF16)
        v_ref[1] = jnp.where(lo, vs, v).astype(BF16)

    tab = pl.BlockSpec((tm, LANES), lambda i: (i, 0))
    kv = pl.BlockSpec((2, tm, LANES), lambda i: (0, i, 0))
    return pl.pallas_call(
        body, name=name, grid=(T // tm,),
        in_specs=[pl.BlockSpec((tm, dm.OFF_S), lambda i: (i, 0)), tab, tab],
        out_specs=[pl.BlockSpec((tm, AW), lambda i: (i, 0)), kv, kv],
        out_shape=[jax.ShapeDtypeStruct((T, AW), BF16), jax.ShapeDtypeStruct((2, T, LANES), BF16),
                   jax.ShapeDtypeStruct((2, T, LANES), BF16)],
        compiler_params=_cparams(("parallel",)),
    )(z, cos, sin)


def _attn_prep_bwd(dq, dk2, dv2, cos, sin, dm, name):
    T = dm.T
    AW = dm.ATTN_W
    tm = _tile(T, 256, 16)
    rep = AW // LANES

    def body(dq_ref, dk_ref, dv_ref, cos_ref, sin_ref, o_ref):
        cs, sn = cos_ref[...], sin_ref[...]
        csq, snq = jnp.tile(cs, (1, rep)), jnp.tile(sn, (1, rep))
        dq = dq_ref[...]
        o_ref[:, 0:AW] = dq * csq - _rot_half(dq * snq)
        lo = lax.broadcasted_iota(jnp.int32, (tm, LANES), 1) < HEAD_DIM
        dk = jnp.where(lo, dk_ref[0], dk_ref[1])
        o_ref[:, dm.OFF_K:dm.OFF_V] = dk * cs - _rot_half(dk * sn)
        o_ref[:, dm.OFF_V:dm.OFF_S] = jnp.where(lo, dv_ref[0], dv_ref[1])

    tab = pl.BlockSpec((tm, LANES), lambda i: (i, 0))
    kv = pl.BlockSpec((2, tm, LANES), lambda i: (0, i, 0))
    return pl.pallas_call(
        body, name=name, grid=(T // tm,),
        in_specs=[pl.BlockSpec((tm, AW), lambda i: (i, 0)), kv, kv, tab, tab],
        out_specs=pl.BlockSpec((tm, dm.OFF_S), lambda i: (i, 0)),
        out_shape=jax.ShapeDtypeStruct((T, dm.OFF_S), F32),
        compiler_params=_cparams(("parallel",)),
    )(dq, dk2, dv2, cos, sin)


def _attn_specs(dm):
    B = ATTN_BLOCK
    nblk = dm.T // B
    q_spec = pl.BlockSpec((B, dm.ATTN_W), lambda i: (i, 0))
    prev = pl.BlockSpec((2, B, LANES), lambda i: (0, jnp.maximum(i - 1, 0), 0))
    own = pl.BlockSpec((2, B, LANES), lambda i: (0, i, 0))
    nxt = pl.BlockSpec((2, B, LANES), lambda i: (0, jnp.minimum(i + 1, nblk - 1), 0))
    ctx = pl.BlockSpec((2, dm.C, LANES), lambda i: (0, 0, 0))
    sink = pl.BlockSpec((2, GQA_RATIO * B, 1), lambda i: (0, 0, 0))
    return nblk, q_spec, prev, own, nxt, ctx, sink


def _attn_scores(i, q_ref, kp_ref, ko_ref, kn_ref, kc_ref, sink_ref, h, dm):
    B = ATTN_BLOCK
    nctx = dm.C // B
    nb = dm.L // B
    npair = GQA_RATIO // 2
    lo = lax.broadcasted_iota(jnp.int32, (B, LANES), 1) < HEAD_DIM
    zero = jnp.zeros((B, LANES), BF16)
    parts = []
    for p in range(npair):
        q2 = q_ref[:, (h * npair + p) * LANES:(h * npair + p + 1) * LANES]
        parts.append(jnp.where(lo, q2, zero))
        parts.append(jnp.where(lo, zero, q2))
    q8 = jnp.concatenate(parts, axis=0)
    kcat = jnp.concatenate([kp_ref[h], ko_ref[h], kn_ref[h], kc_ref[h]], axis=0)
    s = _dot(q8, kcat, _NT) * (HEAD_DIM ** -0.5)
    R, NK = s.shape
    iq = lax.broadcasted_iota(jnp.int32, (R, NK), 0) % B
    col = lax.broadcasted_iota(jnp.int32, (R, NK), 1)
    jk = col % B
    qb = i - nctx
    lat = qb >= 0
    m_prev = (col < B) & lat & (qb >= 1) & (jk >= iq)
    m_own = (col >= B) & (col < 2 * B) & lat
    m_next = (col >= 2 * B) & (col < 3 * B) & lat & (qb <= nb - 2) & (jk <= iq)
    mask = m_prev | m_own | m_next | (col >= 3 * B)
    s = jnp.where(mask, s, -jnp.inf)
    sk = sink_ref[h]
    m = jnp.maximum(jnp.max(s, axis=-1, keepdims=True), sk)
    e = jnp.exp(s - m)
    es = jnp.exp(sk - m)
    inv = 1.0 / (jnp.sum(e, axis=-1, keepdims=True) + es)
    return q8, kcat, e * inv, es * inv, lo


def _unstack_heads(x8, lo):
    B = ATTN_BLOCK
    out = []
    for p in range(GQA_RATIO // 2):
        a = x8[(2 * p) * B:(2 * p + 1) * B]
        b = x8[(2 * p + 1) * B:(2 * p + 2) * B]
        out.append(jnp.where(lo, a, b))
    return out


def _attention(q, k2, v2, sinkcol, dm, name):
    B = ATTN_BLOCK
    nblk, q_spec, prev, own, nxt, ctx, sink = _attn_specs(dm)
    npair = GQA_RATIO // 2

    def body(q_ref, kp_ref, ko_ref, kn_ref, kc_ref, vp_ref, vo_ref, vn_ref, vc_ref, sink_ref, o_ref):
        i = pl.program_id(0)
        for h in range(2):
            _, _, p, _, lo = _attn_scores(i, q_ref, kp_ref, ko_ref, kn_ref, kc_ref, sink_ref, h, dm)
            vcat = jnp.concatenate([vp_ref[h], vo_ref[h], vn_ref[h], vc_ref[h]], axis=0)
            o8 = _dot(p, vcat, _NN)
            for pi, o2 in enumerate(_unstack_heads(o8, lo)):
                c0 = (h * npair + pi) * LANES
                o_ref[:, c0:c0 + LANES] = o2.astype(BF16)

    return pl.pallas_call(
        body, name=name, grid=(nblk,),
        in_specs=[q_spec, prev, own, nxt, ctx, prev, own, nxt, ctx, sink],
        out_specs=q_spec, out_shape=jax.ShapeDtypeStruct((dm.T, dm.ATTN_W), BF16),
        compiler_params=_cparams(("parallel",)),
    )(q, k2, k2, k2, k2, v2, v2, v2, v2, sinkcol)


def _attention_bwd(q, k2, v2, sinkcol, do, dm, name):
    B = ATTN_BLOCK
    nblk, q_spec, prev, own, nxt, ctx, sink = _attn_specs(dm)
    npair = GQA_RATIO // 2

    def body(q_ref, kp_ref, ko_ref, kn_ref, kc_ref, vp_ref, vo_ref, vn_ref, vc_ref, sink_ref, do_ref,
             dq_ref, pk_ref, pv_ref, dkc_ref, dvc_ref, dsk_ref):
        i = pl.program_id(0)

        @pl.when(i == 0)
        def _():
            dkc_ref[...] = jnp.zeros_like(dkc_ref)
            dvc_ref[...] = jnp.zeros_like(dvc_ref)
            dsk_ref[...] = jnp.zeros_like(dsk_ref)

        for h in range(2):
            q8, kcat, p, ps, lo = _attn_scores(i, q_ref, kp_ref, ko_ref, kn_ref, kc_ref, sink_ref, h, dm)
            vcat = jnp.concatenate([vp_ref[h], vo_ref[h], vn_ref[h], vc_ref[h]], axis=0)
            zero = jnp.zeros((B, LANES), F32)
            parts = []
            for pi in range(npair):
                d2 = do_ref[:, (h * npair + pi) * LANES:(h * npair + pi + 1) * LANES]
                parts.append(jnp.where(lo, d2, zero))
                parts.append(jnp.where(lo, zero, d2))
            do8 = jnp.concatenate(parts, axis=0)
            dp = _dot(do8, vcat, _NT)
            delta = jnp.sum(p * dp, axis=-1, keepdims=True)
            ds = p * (dp - delta) * (HEAD_DIM ** -0.5)
            dsr = -ps * delta
            rows = [jnp.broadcast_to(jnp.sum(dsr[r * B:(r + 1) * B], axis=0, keepdims=True), (1, LANES))
                    for r in range(GQA_RATIO)]
            dsk_ref[h] += jnp.concatenate(rows, axis=0)
            dq8 = _dot(ds, kcat, _NN)
            for pi, d2 in enumerate(_unstack_heads(dq8, lo)):
                c0 = (h * npair + pi) * LANES
                dq_ref[:, c0:c0 + LANES] = d2
            dk = _dot(ds, q8, _TN)
            dv = _dot(p, do8, _TN)
            dk = dk + _swap_halves(dk)
            dv = dv + _swap_halves(dv)
            for w in range(3):
                pk_ref[0, h, w] = dk[w * B:(w + 1) * B]
                pv_ref[0, h, w] = dv[w * B:(w + 1) * B]
            dkc_ref[h] += dk[3 * B:]
            dvc_ref[h] += dv[3 * B:]

    part = pl.BlockSpec((1, 2, 3, B, LANES), lambda i: (i, 0, 0, 0, 0))
    acc_c = pl.BlockSpec((2, dm.C, LANES), lambda i: (0, 0, 0))
    acc_s = pl.BlockSpec((2, GQA_RATIO, LANES), lambda i: (0, 0, 0))
    return pl.pallas_call(
        body, name=name, grid=(nblk,),
        in_specs=[q_spec, prev, own, nxt, ctx, prev, own, nxt, ctx, sink, q_spec],
        out_specs=[q_spec, part, part, acc_c, acc_c, acc_s],
        out_shape=[jax.ShapeDtypeStruct((dm.T, dm.ATTN_W), F32),
                   jax.ShapeDtypeStruct((nblk, 2, 3, B, LANES), F32),
                   jax.ShapeDtypeStruct((nblk, 2, 3, B, LANES), F32),
                   jax.ShapeDtypeStruct((2, dm.C, LANES), F32), jax.ShapeDtypeStruct((2, dm.C, LANES), F32),
                   jax.ShapeDtypeStruct((2, GQA_RATIO, LANES), F32)],
        compiler_params=_cparams(("arbitrary",)),
    )(q, k2, k2, k2, k2, v2, v2, v2, v2, sinkcol, do)


def _kv_reduce(part, ctxacc, dm, name):
    B = ATTN_BLOCK
    nblk = dm.T // B
    nctx = dm.C // B

    def body(own_ref, nxt_ref, prv_ref, c_ref, o_ref):
        j = pl.program_id(0)
        acc = own_ref[0, :, 0]
        acc = acc + jnp.where(j < nblk - 1, nxt_ref[0, :, 0], 0.0)
        acc = acc + jnp.where(j > 0, prv_ref[0, :, 0], 0.0)
        acc = acc + jnp.where(j < nctx, c_ref[...], 0.0)
        o_ref[...] = acc

    def spec(w, f):
        return pl.BlockSpec((1, 2, 1, B, LANES), lambda j: (f(j), 0, w, 0, 0))

    return pl.pallas_call(
        body, name=name, grid=(nblk,),
        in_specs=[spec(1, lambda j: j), spec(0, lambda j: jnp.minimum(j + 1, nblk - 1)),
                  spec(2, lambda j: jnp.maximum(j - 1, 0)),
                  pl.BlockSpec((2, B, LANES), lambda j: (0, jnp.minimum(j, nctx - 1), 0))],
        out_specs=pl.BlockSpec((2, B, LANES), lambda j: (0, j, 0)),
        out_shape=jax.ShapeDtypeStruct((2, dm.T, LANES), F32),
        compiler_params=_cparams(("parallel",)),
    )(part, part, part, ctxacc)


def _gmlp_core(z_ref, lg_ref, lb_ref, ws_ref, bs_ref, dm):
    GW = dm.GMLP_W
    gu = z_ref[:, dm.OFF_GU:dm.OFF_GV]
    gv = z_ref[:, dm.OFF_GV:dm.N_IN]
    u = _gelu(gu)
    ge = _gelu(gv)
    mu = jnp.mean(ge, axis=-1, keepdims=True)
    xc = ge - mu
    r = lax.rsqrt(jnp.mean(xc * xc, axis=-1, keepdims=True) + NORM_EPS)
    xh = xc * r
    v = xh * lg_ref[...] + lb_ref[...]
    mixed = []
    for g in range(dm.GG):
        vg = v[:, g * GMLP_GROUP_W:(g + 1) * GMLP_GROUP_W]
        mixed.append(_dot(ws_ref[g], vg, _NN) + bs_ref[g])
    return gu, gv, u, xh, r, v, mixed


def _gmlp(z, ln_g, ln_b, w_s, b_s, dm, name):
    T, GW, CH = dm.T, dm.GMLP_W, GMLP_CHUNK

    def body(z_ref, lg_ref, lb_ref, ws_ref, bs_ref, o_ref):
        _, _, u, _, _, _, mixed = _gmlp_core(z_ref, lg_ref, lb_ref, ws_ref, bs_ref, dm)
        for g in range(dm.GG):
            sl = slice(g * GMLP_GROUP_W, (g + 1) * GMLP_GROUP_W)
            o_ref[:, sl] = (u[:, sl] * mixed[g]).astype(BF16)

    vec = pl.BlockSpec((1, GW), lambda i: (0, 0))
    return pl.pallas_call(
        body, name=name, grid=(T // CH,),
        in_specs=[pl.BlockSpec((CH, dm.N_IN), lambda i: (i, 0)), vec, vec,
                  pl.BlockSpec((dm.GG, CH, CH), lambda i: (0, 0, 0)),
                  pl.BlockSpec((dm.GG, CH, 1), lambda i: (0, 0, 0))],
        out_specs=pl.BlockSpec((CH, GW), lambda i: (i, 0)),
        out_shape=jax.ShapeDtypeStruct((T, GW), BF16),
        compiler_params=_cparams(("parallel",)),
    )(z, ln_g, ln_b, w_s, b_s)


def _gmlp_bwd(z, ln_g, ln_b, w_s, b_s, do, dm, name):
    T, GW, CH = dm.T, dm.GMLP_W, GMLP_CHUNK

    def body(z_ref, lg_ref, lb_ref, ws_ref, bs_ref, do_ref, dz_ref, dlg_ref, dlb_ref, dws_ref, dbs_ref):
        i = pl.program_id(0)

        @pl.when(i == 0)
        def _():
            dlg_ref[...] = jnp.zeros_like(dlg_ref)
            dlb_ref[...] = jnp.zeros_like(dlb_ref)
            dws_ref[...] = jnp.zeros_like(dws_ref)
            dbs_ref[...] = jnp.zeros_like(dbs_ref)

        gu, gv, u, xh, r, v, mixed = _gmlp_core(z_ref, lg_ref, lb_ref, ws_ref, bs_ref, dm)
        do = do_ref[...]
        dv_parts = []
        for g in range(dm.GG):
            sl = slice(g * GMLP_GROUP_W, (g + 1) * GMLP_GROUP_W)
            dog = do[:, sl]
            dz_ref[:, sl] = dog * mixed[g] * _gelu_grad(gu[:, sl])
            dmx = dog * u[:, sl]
            dbs_ref[g] += jnp.sum(dmx, axis=-1, keepdims=True)
            dws_ref[g] += _dot(dmx, v[:, sl], _NT)
            dv_parts.append(_dot(ws_ref[g], dmx, _TN))
        dv = jnp.concatenate(dv_parts, axis=-1) if dm.GG > 1 else dv_parts[0]
        dlg_ref[...] += jnp.sum(dv * xh, axis=0, keepdims=True)
        dlb_ref[...] += jnp.sum(dv, axis=0, keepdims=True)
        dxh = dv * lg_ref[...]
        dge = r * (dxh - jnp.mean(dxh, axis=-1, keepdims=True) - xh * jnp.mean(dxh * xh, axis=-1, keepdims=True))
        dz_ref[:, GW:2 * GW] = dge * _gelu_grad(gv)

    vec = pl.BlockSpec((1, GW), lambda i: (0, 0))
    wsp = pl.BlockSpec((dm.GG, CH, CH), lambda i: (0, 0, 0))
    bsp = pl.BlockSpec((dm.GG, CH, 1), lambda i: (0, 0, 0))
    return pl.pallas_call(
        body, name=name, grid=(T // CH,),
        in_specs=[pl.BlockSpec((CH, dm.N_IN), lambda i: (i, 0)), vec, vec, wsp, bsp,
                  pl.BlockSpec((CH, GW), lambda i: (i, 0))],
        out_specs=[pl.BlockSpec((CH, 2 * GW), lambda i: (i, 0)), vec, vec, wsp, bsp],
        out_shape=[jax.ShapeDtypeStruct((T, 2 * GW), F32), jax.ShapeDtypeStruct((1, GW), F32),
                   jax.ShapeDtypeStruct((1, GW), F32), jax.ShapeDtypeStruct((dm.GG, CH, CH), F32),
                   jax.ShapeDtypeStruct((dm.GG, CH, 1), F32)],
        compiler_params=_cparams(("arbitrary",)),
    )(z, ln_g, ln_b, w_s, b_s, do)


def _disc_fn(lam_re, lam_im, ldt, b_re, b_im):
    dt = jnp.exp(ldt)
    mag = jnp.exp(lam_re * dt)
    a_re = mag * jnp.cos(lam_im * dt)
    a_im = mag * jnp.sin(lam_im * dt)
    den = lam_re * lam_re + lam_im * lam_im
    n_re = a_re - 1.0
    f_re = (n_re * lam_re + a_im * lam_im) / den
    f_im = (a_im * lam_re - n_re * lam_im) / den
    return a_re, a_im, f_re * b_re - f_im * b_im, f_re * b_im + f_im * b_re


def _s5_disc(lam_re, lam_im, ldt, bT_re, bT_im, name):
    nd, H, GP = bT_re.shape

    def body(lr_ref, li_ref, dt_ref, br_ref, bi_ref, ar_ref, ai_ref, bbr_ref, bbi_ref):
        for d in range(nd):
            a_re, a_im, bb_re, bb_im = _disc_fn(lr_ref[d], li_ref[d], dt_ref[d], br_ref[d], bi_ref[d])
            ar_ref[d] = a_re
            ai_ref[d] = a_im
            bbr_ref[d] = bb_re
            bbi_ref[d] = bb_im

    vs = jax.ShapeDtypeStruct((nd, 1, GP), F32)
    ms = jax.ShapeDtypeStruct((nd, H, GP), F32)
    return pl.pallas_call(body, name=name, out_shape=[vs, vs, ms, ms], compiler_params=_cparams())(
        lam_re, lam_im, ldt, bT_re, bT_im)


def _s5_disc_bwd(lam_re, lam_im, ldt, bT_re, bT_im, da_re, da_im, dbb_re, dbb_im, name):
    nd, H, GP = bT_re.shape

    def body(lr_ref, li_ref, dt_ref, br_ref, bi_ref, dar_ref, dai_ref, dbr_ref, dbi_ref,
             olr_ref, oli_ref, odt_ref, obr_ref, obi_ref):
        for d in range(nd):
            _, vjp = jax.vjp(_disc_fn, lr_ref[d], li_ref[d], dt_ref[d], br_ref[d], bi_ref[d])
            g = vjp((dar_ref[d], dai_ref[d], dbr_ref[d], dbi_ref[d]))
            olr_ref[d], oli_ref[d], odt_ref[d], obr_ref[d], obi_ref[d] = g

    vs = jax.ShapeDtypeStruct((nd, 1, GP), F32)
    ms = jax.ShapeDtypeStruct((nd, H, GP), F32)
    return pl.pallas_call(body, name=name, out_shape=[vs, vs, vs, ms, ms], compiler_params=_cparams())(
        lam_re, lam_im, ldt, bT_re, bT_im, da_re, da_im, dbb_re, dbb_im)


def _scan_pass(re_ref, im_ref, nsteps, ar, ai, ir, ii, reverse, store):
    def step(n, carry):
        sr, si = carry
        t = (nsteps - 1 - n) if reverse else n
        off = pl.multiple_of(t * 8, 8)
        nr = ar * sr - ai * si + re_ref[pl.ds(off, 8), :]
        ni = ar * si + ai * sr + im_ref[pl.ds(off, 8), :]
        if store:
            re_ref[pl.ds(off, 8), :] = nr
            im_ref[pl.ds(off, 8), :] = ni
        return nr, ni

    return lax.fori_loop(0, nsteps, step, (ir, ii), unroll=4)


def _cpow(ar, ai, n):
    rr = ri = None
    br, bi = ar, ai
    while n:
        if n & 1:
            rr, ri = (br, bi) if rr is None else (rr * br - ri * bi, rr * bi + ri * br)
        n >>= 1
        if n:
            br, bi = br * br - bi * bi, 2.0 * br * bi
    return rr, ri


def _row_of(x, k):
    rows = lax.broadcasted_iota(jnp.int32, x.shape, 0)
    return jnp.sum(jnp.where(rows == k, x, 0.0), axis=0, keepdims=True)


def _full_scan(re_ref, im_ref, nsteps, ar1, ai1, h0r, h0i, reverse):
    P = ar1.shape[1]
    ar, ai = jnp.broadcast_to(ar1, (8, P)), jnp.broadcast_to(ai1, (8, P))
    z = jnp.zeros((8, P), F32)
    er, ei = _scan_pass(re_ref, im_ref, nsteps, ar, ai, z, z, reverse, False)
    pr, pi = _cpow(ar1, ai1, nsteps)
    rows = lax.broadcasted_iota(jnp.int32, (8, P), 0)
    initr, initi = z, z
    cr, ci = h0r, h0i
    for k in (range(SCAN_SEGMENTS - 1, -1, -1) if reverse else range(SCAN_SEGMENTS)):
        initr = jnp.where(rows == k, cr, initr)
        initi = jnp.where(rows == k, ci, initi)
        ekr, eki = _row_of(er, k), _row_of(ei, k)
        cr, ci = ekr + pr * cr - pi * ci, eki + pr * ci + pi * cr
    _scan_pass(re_ref, im_ref, nsteps, ar, ai, initr, initi, reverse, True)
    return initr, initi, cr, ci


def _rows_loop(n, chunk, fn):
    def step(c, carry):
        fn(pl.multiple_of(c * chunk, chunk))
        return carry
    lax.fori_loop(0, n // chunk, step, 0)


def _s5_specs(dm):
    PB = STATE_BLOCK
    nsb = dm.GP // PB
    per = (LANES * SSM_STATE // SSM_GROUP) // PB
    ul = pl.BlockSpec((dm.L, LANES), lambda j: (0, j // per))
    uc = pl.BlockSpec((dm.C, LANES), lambda j: (0, j // per))
    av = pl.BlockSpec((2, 1, PB), lambda j: (0, 0, j))
    bd = pl.BlockSpec((2, LANES, PB), lambda j: (0, j // per, j))
    cd = pl.BlockSpec((2, PB, LANES), lambda j: (0, j, j // per))
    dv = pl.BlockSpec((1, LANES), lambda j: (0, j // per))
    return PB, nsb, per, ul, uc, av, bd, cd, dv


def _s5_scan(ul, uc, a_re, a_im, bd_re, bd_im, cd_re, cd_im, dskip, dm, name):
    PB, nsb, per, ul_s, uc_s, av, bd, cd, dv = _s5_specs(dm)
    L, C = dm.L, dm.C
    RC = _tile(L, 512, 8)
    RCC = _tile(C, 512, 8)

    def body(ul_ref, uc_ref, ar_ref, ai_ref, br_ref, bi_ref, cr_ref, ci_ref, d_ref, yl_ref, yc_ref,
             lre, lim, cre, cim):
        j = pl.program_id(0)

        @pl.when(j % per == 0)
        def _():
            yl_ref[...] = ul_ref[...] * d_ref[...]
            yc_ref[...] = uc_ref[...] * d_ref[...]

        for d in range(2):
            rev = d == 1
            ar1, ai1 = ar_ref[d], ai_ref[d]

            def proj(u_ref, re, im, chunk, n):
                def f(r0):
                    u = u_ref[pl.ds(r0, chunk), :]
                    re[pl.ds(r0, chunk), :] = _dot(u, br_ref[d], _NN)
                    im[pl.ds(r0, chunk), :] = _dot(u, bi_ref[d], _NN)
                _rows_loop(n, chunk, f)

            def readout(y_ref, re, im, chunk, n):
                def f(r0):
                    y_ref[pl.ds(r0, chunk), :] += (_dot(re[pl.ds(r0, chunk), :], cr_ref[d], _NN)
                                                   - _dot(im[pl.ds(r0, chunk), :], ci_ref[d], _NN))
                _rows_loop(n, chunk, f)

            z1 = jnp.zeros((1, PB), F32)
            proj(uc_ref, cre, cim, RCC, C)
            _, _, fr, fi = _full_scan(cre, cim, C // 8, ar1, ai1, z1, z1, rev)
            readout(yc_ref, cre, cim, RCC, C)
            proj(ul_ref, lre, lim, RC, L)
            _full_scan(lre, lim, L // 8, ar1, ai1, fr, fi, rev)
            readout(yl_ref, lre, lim, RC, L)

    return pl.pallas_call(
        body, name=name, grid=(nsb,),
        in_specs=[ul_s, uc_s, av, av, bd, bd, cd, cd, dv],
        out_specs=[ul_s, uc_s],
        out_shape=[jax.ShapeDtypeStruct((L, dm.SSM_W), F32), jax.ShapeDtypeStruct((C, dm.SSM_W), F32)],
        scratch_shapes=[pltpu.VMEM((L, PB), F32), pltpu.VMEM((L, PB), F32),
                        pltpu.VMEM((C, PB), F32), pltpu.VMEM((C, PB), F32)],
        compiler_params=_cparams(("arbitrary",)),
    )(ul, uc, a_re, a_im, bd_re, bd_im, cd_re, cd_im, dskip)


def _s5_scan_bwd(ul, uc, dyl, dyc, a_re, a_im, bd_re, bd_im, cd_re, cd_im, bdT_re, bdT_im, cdT_re, cdT_im,
                 dskip, dm, name):
    PB, nsb, per, ul_s, uc_s, av, bd, cd, dv = _s5_specs(dm)
    L, C = dm.L, dm.C
    RC = _tile(L, 512, 8)
    RCC = _tile(C, 512, 8)
    bdT = pl.BlockSpec((2, PB, LANES), lambda j: (0, j, j // per))
    cdT = pl.BlockSpec((2, LANES, PB), lambda j: (0, j // per, j))

    def body(ul_ref, uc_ref, dyl_ref, dyc_ref, ar_ref, ai_ref, br_ref, bi_ref, cr_ref, ci_ref,
             brT_ref, biT_ref, crT_ref, ciT_ref, d_ref,
             dul_ref, duc_ref, dar_ref, dai_ref, dbr_ref, dbi_ref, dcr_ref, dci_ref, dd_ref,
             lre, lim, cre, cim, gre, gim, hre, him):
        j = pl.program_id(0)

        @pl.when(j % per == 0)
        def _():
            dul_ref[...] = dyl_ref[...] * d_ref[...]
            duc_ref[...] = dyc_ref[...] * d_ref[...]
            dd_ref[...] = (jnp.sum(dyl_ref[...] * ul_ref[...], axis=0, keepdims=True)
                           + jnp.sum(dyc_ref[...] * uc_ref[...], axis=0, keepdims=True))

        for d in range(2):
            rev = d == 1
            ar1, ai1 = ar_ref[d], ai_ref[d]
            z1 = jnp.zeros((1, PB), F32)

            def proj(u_ref, w_re, w_im, sign, re, im, chunk, n):
                def f(r0):
                    u = u_ref[pl.ds(r0, chunk), :]
                    re[pl.ds(r0, chunk), :] = _dot(u, w_re, _NN)
                    im[pl.ds(r0, chunk), :] = sign * _dot(u, w_im, _NN)
                _rows_loop(n, chunk, f)

            proj(uc_ref, br_ref[d], bi_ref[d], 1.0, cre, cim, RCC, C)
            icr, ici, fr, fi = _full_scan(cre, cim, C // 8, ar1, ai1, z1, z1, rev)
            proj(ul_ref, br_ref[d], bi_ref[d], 1.0, lre, lim, RC, L)
            ilr, ili, _, _ = _full_scan(lre, lim, L // 8, ar1, ai1, fr, fi, rev)
            proj(dyl_ref, crT_ref[d], ciT_ref[d], -1.0, gre, gim, RC, L)
            _, _, gfr, gfi = _full_scan(gre, gim, L // 8, ar1, -ai1, z1, z1, not rev)
            proj(dyc_ref, crT_ref[d], ciT_ref[d], -1.0, hre, him, RCC, C)
            _full_scan(hre, him, C // 8, ar1, -ai1, gfr, gfi, not rev)

            dar = jnp.zeros((8, PB), F32)
            dai = jnp.zeros((8, PB), F32)
            dbr = jnp.zeros((LANES, PB), F32)
            dbi = jnp.zeros((LANES, PB), F32)
            dcr = jnp.zeros((PB, LANES), F32)
            dci = jnp.zeros((PB, LANES), F32)
            for (u_ref, dy_ref, du_ref, sre, sim, are, aim, inr, ini, n, chunk) in (
                    (ul_ref, dyl_ref, dul_ref, lre, lim, gre, gim, ilr, ili, L, RC),
                    (uc_ref, dyc_ref, duc_ref, cre, cim, hre, him, icr, ici, C, RCC)):
                nst = n // 8

                def da_step(t, carry, sre=sre, sim=sim, are=are, aim=aim, nst=nst):
                    xr, xi = carry
                    tp = (t + 1) if rev else (t - 1)
                    o = pl.multiple_of(t * 8, 8)
                    op = pl.multiple_of(tp * 8, 8)
                    g_r, g_i = are[pl.ds(o, 8), :], aim[pl.ds(o, 8), :]
                    p_r, p_i = sre[pl.ds(op, 8), :], sim[pl.ds(op, 8), :]
                    return xr + g_r * p_r + g_i * p_i, xi + g_i * p_r - g_r * p_i

                lo_t, hi_t = (0, nst - 1) if rev else (1, nst)
                dar, dai = lax.fori_loop(lo_t, hi_t, da_step, (dar, dai))
                e0 = (nst - 1) * 8 if rev else 0
                g_r, g_i = are[pl.ds(e0, 8), :], aim[pl.ds(e0, 8), :]
                dar = dar + g_r * inr + g_i * ini
                dai = dai + g_i * inr - g_r * ini

                def mats(c, carry, u_ref=u_ref, dy_ref=dy_ref, du_ref=du_ref, sre=sre, sim=sim, are=are, aim=aim,
                         chunk=chunk):
                    xbr, xbi, xcr, xci = carry
                    r0 = pl.multiple_of(c * chunk, chunk)
                    u = u_ref[pl.ds(r0, chunk), :]
                    dy = dy_ref[pl.ds(r0, chunk), :]
                    g_r, g_i = are[pl.ds(r0, chunk), :], aim[pl.ds(r0, chunk), :]
                    du_ref[pl.ds(r0, chunk), :] += _dot(g_r, brT_ref[d], _NN) + _dot(g_i, biT_ref[d], _NN)
                    xbr = xbr + _dot(u, g_r, _TN)
                    xbi = xbi + _dot(u, g_i, _TN)
                    xcr = xcr + _dot(sre[pl.ds(r0, chunk), :], dy, _TN)
                    xci = xci - _dot(sim[pl.ds(r0, chunk), :], dy, _TN)
                    return xbr, xbi, xcr, xci

                dbr, dbi, dcr, dci = lax.fori_loop(0, n // chunk, mats, (dbr, dbi, dcr, dci))
            dar_ref[d] = jnp.sum(dar, axis=0, keepdims=True)
            dai_ref[d] = jnp.sum(dai, axis=0, keepdims=True)
            dbr_ref[d] = dbr
            dbi_ref[d] = dbi
            dcr_ref[d] = dcr
            dci_ref[d] = dci

    W, GP = dm.SSM_W, dm.GP
    return pl.pallas_call(
        body, name=name, grid=(nsb,),
        in_specs=[ul_s, uc_s, ul_s, uc_s, av, av, bd, bd, cd, cd, bdT, bdT, cdT, cdT, dv],
        out_specs=[ul_s, uc_s, av, av, bd, bd, cd, cd, dv],
        out_shape=[jax.ShapeDtypeStruct((L, W), F32), jax.ShapeDtypeStruct((C, W), F32),
                   jax.ShapeDtypeStruct((2, 1, GP), F32), jax.ShapeDtypeStruct((2, 1, GP), F32),
                   jax.ShapeDtypeStruct((2, W, GP), F32), jax.ShapeDtypeStruct((2, W, GP), F32),
                   jax.ShapeDtypeStruct((2, GP, W), F32), jax.ShapeDtypeStruct((2, GP, W), F32),
                   jax.ShapeDtypeStruct((1, W), F32)],
        scratch_shapes=[pltpu.VMEM((L, PB), F32), pltpu.VMEM((L, PB), F32),
                        pltpu.VMEM((C, PB), F32), pltpu.VMEM((C, PB), F32),
                        pltpu.VMEM((L, PB), F32), pltpu.VMEM((L, PB), F32),
                        pltpu.VMEM((C, PB), F32), pltpu.VMEM((C, PB), F32)],
        compiler_params=_cparams(("arbitrary",)),
    )(ul, uc, dyl, dyc, a_re, a_im, bd_re, bd_im, cd_re, cd_im, bdT_re, bdT_im, cdT_re, cdT_im, dskip)


def _glu(y, w, b, name):
    T, W = y.shape
    tm = _tile(T, 544, 16)

    def body(y_ref, w_ref, b_ref, o_ref):
        g = _gelu(y_ref[...])
        o_ref[...] = (g * _sigmoid(_dot(g, w_ref[...], _NN) + b_ref[...])).astype(BF16)

    return pl.pallas_call(
        body, name=name, grid=(T // tm,),
        in_specs=[pl.BlockSpec((tm, W), lambda i: (i, 0)), pl.BlockSpec((W, W), lambda i: (0, 0)),
                  pl.BlockSpec((1, W), lambda i: (0, 0))],
        out_specs=pl.BlockSpec((tm, W), lambda i: (i, 0)),
        out_shape=jax.ShapeDtypeStruct((T, W), BF16),
        compiler_params=_cparams(("parallel",)),
    )(y, w, b)


def _glu_bwd(y, w, b, do, name):
    T, W = y.shape
    tm = _tile(T, 544, 16)

    def body(y_ref, w_ref, b_ref, do_ref, dy_ref, dw_ref, db_ref):
        i = pl.program_id(0)

        @pl.when(i == 0)
        def _():
            dw_ref[...] = jnp.zeros_like(dw_ref)
            db_ref[...] = jnp.zeros_like(db_ref)

        y = y_ref[...]
        do = do_ref[...]
        g = _gelu(y)
        s = _sigmoid(_dot(g, w_ref[...], _NN) + b_ref[...])
        dpre = do * g * s * (1.0 - s)
        db_ref[...] += jnp.sum(dpre, axis=0, keepdims=True)
        dw_ref[...] += _dot(g, dpre, _TN)
        dg = do * s + _dot(dpre, w_ref[...], _NT)
        dy_ref[...] = dg * _gelu_grad(y)

    row = pl.BlockSpec((tm, W), lambda i: (i, 0))
    wsp = pl.BlockSpec((W, W), lambda i: (0, 0))
    vec = pl.BlockSpec((1, W), lambda i: (0, 0))
    return pl.pallas_call(
        body, name=name, grid=(T // tm,),
        in_specs=[row, wsp, vec, row], out_specs=[row, wsp, vec],
        out_shape=[jax.ShapeDtypeStruct((T, W), F32), jax.ShapeDtypeStruct((W, W), F32),
                   jax.ShapeDtypeStruct((1, W), F32)],
        compiler_params=_cparams(("arbitrary",)),
    )(y, w, b, do)


def _loss_head(X, g, target, C, name):
    T, D = X.shape
    tm = _tile(math.gcd(C, T - C), 512, 16)
    nc = C // tm

    def body(x_ref, g_ref, t_ref, loss_ref, dx_ref, dg_ref):
        i = pl.program_id(0)

        @pl.when(i == 0)
        def _():
            loss_ref[...] = jnp.zeros_like(loss_ref)
            dg_ref[...] = jnp.zeros_like(dg_ref)

        @pl.when(i < nc)
        def _():
            dx_ref[...] = jnp.zeros_like(dx_ref)

        @pl.when(i >= nc)
        def _():
            x = x_ref[...]
            gv = g_ref[...]
            r = lax.rsqrt(jnp.mean(x * x, axis=-1, keepdims=True) + NORM_EPS)
            xh = x * r
            err = xh * gv - t_ref[...]
            loss_ref[...] += 0.5 * jnp.sum(jnp.mean(err * err, axis=-1, keepdims=True), axis=0, keepdims=True)
            dy = err * (1.0 / D)
            dg_ref[...] += jnp.sum(dy * xh, axis=0, keepdims=True)
            dxh = dy * gv
            dx_ref[...] = r * (dxh - xh * jnp.mean(dxh * xh, axis=-1, keepdims=True))

    row = pl.BlockSpec((tm, D), lambda i: (i, 0))
    return pl.pallas_call(
        body, name=name, grid=(T // tm,),
        in_specs=[row, pl.BlockSpec((1, D), lambda i: (0, 0)),
                  pl.BlockSpec((tm, D), lambda i: (jnp.maximum(i - nc, 0), 0))],
        out_specs=[pl.BlockSpec((1, 1), lambda i: (0, 0)), row, pl.BlockSpec((1, D), lambda i: (0, 0))],
        out_shape=[jax.ShapeDtypeStruct((1, 1), F32), jax.ShapeDtypeStruct((T, D), F32),
                   jax.ShapeDtypeStruct((1, D), F32)],
        compiler_params=_cparams(("arbitrary",)),
    )(X, g, target)


def _ada_fwd(cond, w, b, name):
    nl, D, Ns = w.shape
    tn = _tile(Ns, 1024, LANES)

    def body(c_ref, w_ref, b_ref, o_ref):
        c = c_ref[...]
        o_ref[0] = _dot(c * _sigmoid(c), w_ref[0], _NN) + b_ref[0]

    return pl.pallas_call(
        body, name=name, grid=(nl, Ns // tn),
        in_specs=[pl.BlockSpec((16, D), lambda l, j: (0, 0)), pl.BlockSpec((1, D, tn), lambda l, j: (l, 0, j)),
                  pl.BlockSpec((1, 1, tn), lambda l, j: (l, 0, j))],
        out_specs=pl.BlockSpec((1, 16, tn), lambda l, j: (l, 0, j)),
        out_shape=jax.ShapeDtypeStruct((nl, 16, Ns), F32),
        compiler_params=_cparams(("parallel", "parallel")),
    )(cond, w, b)


def _ada_bwd(cond, w, dmod, name):
    nl, D, Ns = w.shape
    tn = _tile(Ns, 1024, LANES)

    def body(c_ref, w_ref, dm_ref, dw_ref, da_ref):
        j = pl.program_id(1)

        @pl.when(j == 0)
        def _():
            da_ref[...] = jnp.zeros_like(da_ref)

        c = c_ref[...]
        dw_ref[0] = _dot(c * _sigmoid(c), dm_ref[0], _TN)
        da_ref[0] += _dot(dm_ref[0], w_ref[0], _NT)

    return pl.pallas_call(
        body, name=name, grid=(nl, Ns // tn),
        in_specs=[pl.BlockSpec((16, D), lambda l, j: (0, 0)), pl.BlockSpec((1, D, tn), lambda l, j: (l, 0, j)),
                  pl.BlockSpec((1, 16, tn), lambda l, j: (l, 0, j))],
        out_specs=[pl.BlockSpec((1, D, tn), lambda l, j: (l, 0, j)), pl.BlockSpec((1, 16, D), lambda l, j: (l, 0, 0))],
        out_shape=[jax.ShapeDtypeStruct((nl, D, Ns), F32), jax.ShapeDtypeStruct((nl, 16, D), F32)],
        compiler_params=_cparams(("parallel", "arbitrary")),
    )(cond, w, dmod)


def _cctx_grad(c_ctx, dact4, name):
    nch, nl, _, D = dact4.shape

    def body(c_ref, da_ref, o_ref):
        acc = jnp.zeros((1, D), F32)
        for ch in range(nch):
            for l in range(nl):
                acc = acc + da_ref[ch, l, 8:9, :]
        c = c_ref[...]
        s = _sigmoid(c)
        o_ref[...] = acc * (s + c * s * (1.0 - s))

    return pl.pallas_call(body, name=name, out_shape=jax.ShapeDtypeStruct((1, D), F32),
                          compiler_params=_cparams())(c_ctx, dact4)


def _adamw(w, g, m, v, name):
    R, W = w.shape
    tr = _tile(R, max(8, (1 << 19) // W // 8 * 8), 8)
    c1 = 1.0 - ADAM_B1 ** ADAM_STEP
    c2 = 1.0 - ADAM_B2 ** ADAM_STEP

    def body(w_ref, g_ref, m_ref, v_ref, d_ref, nm_ref, nv_ref):
        g = g_ref[...]
        m = ADAM_B1 * m_ref[...] + (1.0 - ADAM_B1) * g
        v = ADAM_B2 * v_ref[...] + (1.0 - ADAM_B2) * (g * g)
        nm_ref[...] = m
        nv_ref[...] = v
        d_ref[...] = -ADAM_LR * ((m / c1) / (jnp.sqrt(v / c2) + ADAM_EPS) + ADAM_WD * w_ref[...])

    blk = pl.BlockSpec((tr, W), lambda i: (i, 0))
    s = jax.ShapeDtypeStruct((R, W), F32)
    return pl.pallas_call(
        body, name=name, grid=(R // tr,), in_specs=[blk] * 4, out_specs=[blk] * 3, out_shape=[s, s, s],
        compiler_params=_cparams(("parallel",)),
    )(w, g, m, v)


def _adamw_nd(w, g, m, v, name):
    shp = w.shape
    two = (math.prod(shp[:-1]), shp[-1])
    d, nm, nv = _adamw(w.reshape(two), g.reshape(two), m.reshape(two), v.reshape(two), name)
    return d.reshape(shp), nm.reshape(shp), nv.reshape(shp)


def _interleave(a, n):
    return a.reshape(SCAN_SEGMENTS, n // SCAN_SEGMENTS, -1).transpose(1, 0, 2).reshape(n, -1)


def _deinterleave(a, n):
    return a.reshape(n // SCAN_SEGMENTS, SCAN_SEGMENTS, -1).transpose(1, 0, 2).reshape(n, -1)


def _blockdiag_in(bbT, dm):
    eye = jnp.eye(dm.SG, dtype=F32)
    b4 = bbT.reshape(2, SSM_GROUP, dm.SG, SSM_STATE)
    return jnp.einsum("dhgp,gk->dghkp", b4, eye).reshape(2, dm.SSM_W, dm.GP)


def _blockdiag_in_T(dbd, dm):
    d5 = dbd.reshape(2, dm.SG, SSM_GROUP, dm.SG, SSM_STATE)
    return jnp.einsum("dghgp->dhgp", d5).reshape(2, SSM_GROUP, dm.GP)


def _blockdiag_out(c, dm):
    eye = jnp.eye(dm.SG, dtype=F32)
    return jnp.einsum("dghp,gk->dgpkh", c, eye).reshape(2, dm.GP, dm.SSM_W)


def _blockdiag_out_T(dcd, dm):
    d5 = dcd.reshape(2, dm.SG, SSM_STATE, dm.SG, SSM_GROUP)
    return jnp.einsum("dgpgh->dghp", d5)


_BIG = ("w_in", "w_out", "ssm_w_glu", "ffn_w_up", "ffn_w_down")
_SHARD_AXIS = {"w_in": 2, "w_out": 1, "ssm_w_glu": 1, "ffn_w_up": 2, "ffn_w_down": 1}
_SMALL = ("g_mix", "g_ffn", "attn_sink", "ssm_lambda_re", "ssm_lambda_im", "ssm_log_dt", "ssm_b_re", "ssm_b_im",
          "ssm_c_re", "ssm_c_im", "ssm_d", "ssm_b_glu", "gmlp_ln_g", "gmlp_ln_b", "gmlp_w_s", "gmlp_b_s",
          "ffn_conv_b", "g_final", "ffn_conv_w")
_WEIGHTS = ("c_ctx", "w_ada", "b_ada", "g_mix", "g_ffn", "w_in", "w_out", "attn_sink", "ssm_lambda_re",
            "ssm_lambda_im", "ssm_log_dt", "ssm_b_re", "ssm_b_im", "ssm_c_re", "ssm_c_im", "ssm_d", "ssm_w_glu",
            "ssm_b_glu", "gmlp_ln_g", "gmlp_ln_b", "gmlp_w_s", "gmlp_b_s", "ffn_w_up", "ffn_conv_w", "ffn_conv_b",
            "ffn_w_down", "g_final")


def _step(P, M, V, x, c, ctx, loss_target):
    dm = _Dims()
    D, T, C, L = dm.D, dm.T, dm.C, dm.L
    mx, my, mc = lax.axis_index("x"), lax.axis_index("y"), lax.axis_index("c")
    chip = 2 * mx + my
    dev = 2 * chip + mc

    conv_sh = P["ffn_conv_w"].shape
    small_f = _pack([c, P["ffn_conv_w"]], F32, 8)
    small_all = _allgather8(small_f, "ag_small_fwd")
    c_all, conv_all = [], []
    for d in range(N_DEV):
        cd_, cw_ = _unpack(small_all[d], [(1, D), conv_sh])
        c_all.append(cd_)
        conv_all.append(cw_)
    conv_w = jnp.concatenate([conv_all[2 * j] for j in range(N_CHIPS)], axis=2)
    cond = jnp.concatenate(c_all + [P["c_ctx"].reshape(1, D), jnp.zeros((16 - N_DEV - 1, D), F32)], axis=0)

    n_sh = P["w_ada"].shape[2]
    b_sh = lax.dynamic_slice_in_dim(P["b_ada"], chip * n_sh, n_sh, axis=1)[:, None, :]
    mods_sh = _ada_fwd(cond, P["w_ada"], b_sh, "ada_fwd")
    mods4 = _comm(mods_sh, group="chips", by_peer=False, name="ag_mods")
    mods = jnp.concatenate([mods4[j] for j in range(N_CHIPS)], axis=-1)
    mod_lat = lax.dynamic_index_in_dim(mods, dev, axis=1, keepdims=False)
    mod_ctx = mods[:, N_DEV]
    modp = jnp.stack([mod_ctx, mod_lat], axis=1).reshape(DEPTH, 2, N_MOD, D)

    big_shapes = [P[n].shape for n in _BIG]
    wpack = _pack([P[n] for n in _BIG], BF16, 2 * 16)
    R = wpack.shape[0]
    w_half = lax.dynamic_index_in_dim(wpack.reshape(2, R // 2, PACK_W), mc, axis=0, keepdims=False)
    w4 = _comm(w_half, group="chips", by_peer=False, name="ag_w_chips")
    w24 = _comm(w4, group="sib", by_peer=False, name="ag_w_sib")
    Wfull = {n: [] for n in _BIG}
    for j in range(N_CHIPS):
        parts = _unpack(jnp.concatenate([w24[0, j], w24[1, j]], axis=0), big_shapes)
        for n, a in zip(_BIG, parts):
            Wfull[n].append(a)
    Wfull = {n: jnp.concatenate(v, axis=_SHARD_AXIS[n]) for n, v in Wfull.items()}

    cos, sin = _rope_tables(dm)
    eye_dummy = None
    del eye_dummy

    X = jnp.concatenate([ctx.reshape(C, D), x.reshape(L, D)], axis=0)
    saved = []
    for l in range(DEPTH):
        mp = modp[l]
        sh_a, sc_a, gt_a, sh_f, sc_f, gt_f = [mp[:, k] for k in range(N_MOD)]
        w_in, w_out = Wfull["w_in"][l], Wfull["w_out"][l]
        w_up, w_dn, w_glu = Wfull["ffn_w_up"][l], Wfull["ffn_w_down"][l], Wfull["ssm_w_glu"][l]
        w_g, w_v = w_up[:, :dm.DFF], w_up[:, dm.DFF:]
        g_mix, g_ffn = P["g_mix"][l][None], P["g_ffn"][l][None]

        h = _normmod(X, g_mix, sh_a, sc_a, C, "normmod_a")
        z = _matmul(h, w_in, mode="nn", name="mm_in")
        q, k2, v2 = _attn_prep(z, cos, sin, dm, "attn_prep")
        sinkcol = jnp.repeat(P["attn_sink"][l].reshape(2, GQA_RATIO), ATTN_BLOCK, axis=1)[..., None]
        o_attn = _attention(q, k2, v2, sinkcol, dm, "attention")
        u = z[:, dm.OFF_S:dm.OFF_GU]
        ul, uc = _interleave(u[C:], L), _interleave(u[:C], C)
        lam_re = P["ssm_lambda_re"][l].reshape(2, 1, dm.GP)
        lam_im = P["ssm_lambda_im"][l].reshape(2, 1, dm.GP)
        ldt = jnp.repeat(P["ssm_log_dt"][l], SSM_STATE, axis=-1).reshape(2, 1, dm.GP)
        bT_re = P["ssm_b_re"][l].reshape(2, dm.GP, SSM_GROUP).transpose(0, 2, 1)
        bT_im = P["ssm_b_im"][l].reshape(2, dm.GP, SSM_GROUP).transpose(0, 2, 1)
        a_re, a_im, bbT_re, bbT_im = _s5_disc(lam_re, lam_im, ldt, bT_re, bT_im, "s5_disc")
        bd_re, bd_im = _blockdiag_in(bbT_re, dm).astype(BF16), _blockdiag_in(bbT_im, dm).astype(BF16)
        cd_re = _blockdiag_out(P["ssm_c_re"][l], dm).astype(BF16)
        cd_im = _blockdiag_out(P["ssm_c_im"][l], dm).astype(BF16)
        dskip = P["ssm_d"][l][None]
        yl, yc = _s5_scan(ul, uc, a_re, a_im, bd_re, bd_im, cd_re, cd_im, dskip, dm, "s5_scan")
        y = jnp.concatenate([_deinterleave(yc, C), _deinterleave(yl, L)], axis=0)
        b_glu = P["ssm_b_glu"][l][None]
        o_ssm = _glu(y, w_glu, b_glu, "glu")
        ln_g, ln_b = P["gmlp_ln_g"][l][None], P["gmlp_ln_b"][l][None]
        w_s = P["gmlp_w_s"][l].astype(BF16)
        b_s = P["gmlp_b_s"][l][..., None]
        o_gmlp = _gmlp(z, ln_g, ln_b, w_s, b_s, dm, "gmlp")
        o_cat = jnp.concatenate([o_attn, o_ssm, o_gmlp], axis=-1)
        mix, X1 = _matmul_residual(o_cat, w_out, X, gt_a, C, "mm_out")
        h2 = _normmod(X1, g_ffn, sh_f, sc_f, C, "normmod_f")
        upg = _matmul(h2, w_g, mode="nn", name="mm_up")
        upv = _matmul(h2, w_v, mode="nn", name="mm_up")
        cw, cb = conv_w[l], P["ffn_conv_b"][l][None]
        act = _convact(upg, upv, cw, cb, C, "convact")
        ffn, X2 = _matmul_residual(act, w_dn, X1, gt_f, C, "mm_down")
        saved.append(dict(X=X, h=h, z=z, q=q, k2=k2, v2=v2, sinkcol=sinkcol, ul=ul, uc=uc, a_re=a_re, a_im=a_im,
                          bd_re=bd_re, bd_im=bd_im, cd_re=cd_re, cd_im=cd_im, y=y, o_cat=o_cat, mix=mix, X1=X1, h2=h2,
                          upg=upg, upv=upv, act=act, ffn=ffn, lam_re=lam_re, lam_im=lam_im, ldt=ldt, bT_re=bT_re,
                          bT_im=bT_im))
        X = X2

    loss_l, dX, dg_final = _loss_head(X, P["g_final"][None], loss_target.reshape(L, D), C, "loss_head")
    loss = lax.psum(loss_l[0, 0], ("x", "y", "c"))

    G = {n: [None] * DEPTH for n in _WEIGHTS if n not in ("c_ctx", "w_ada", "b_ada", "g_final")}
    dmod = [None] * DEPTH
    for l in reversed(range(DEPTH)):
        S = saved[l]
        mp = modp[l]
        sh_a, sc_a, gt_a, sh_f, sc_f, gt_f = [mp[:, k] for k in range(N_MOD)]
        w_in, w_out = Wfull["w_in"][l], Wfull["w_out"][l]
        w_up, w_dn, w_glu = Wfull["ffn_w_up"][l], Wfull["ffn_w_down"][l], Wfull["ssm_w_glu"][l]
        w_g, w_v = w_up[:, :dm.DFF], w_up[:, dm.DFF:]
        g_mix, g_ffn = P["g_mix"][l][None], P["g_ffn"][l][None]
        cw, cb = conv_w[l], P["ffn_conv_b"][l][None]

        dffn, dgt_f = _gate_bwd(dX, S["ffn"], gt_f, C, "gate_bwd")
        dact = _matmul(dffn, w_dn, mode="nt", name="mm_down_dx")
        G["ffn_w_down"][l] = _matmul(S["act"], dffn, mode="tn", name="mm_down_dw")
        dupg, dupv, dcw, dcb = _convact_bwd(S["upg"], S["upv"], cw, cb, dact, C, "convact_bwd")
        G["ffn_conv_w"][l], G["ffn_conv_b"][l] = dcw, dcb[0]
        dh2 = _matmul(dupg, w_g, mode="nt", name="mm_up_dx")
        dh2 = _matmul(dupv, w_v, mode="nt", name="mm_up_dx2", add=dh2)
        G["ffn_w_up"][l] = jnp.concatenate([_matmul(S["h2"], dupg, mode="tn", name="mm_up_dw"),
                                            _matmul(S["h2"], dupv, mode="tn", name="mm_up_dw")], axis=1)
        dX1, dg_f, dsh_f, dsc_f = _normmod_bwd(S["X1"], g_ffn, sc_f, dh2, dX, C, "normmod_bwd")
        G["g_ffn"][l] = dg_f[0]
        dmix, dgt_a = _gate_bwd(dX1, S["mix"], gt_a, C, "gate_bwd")
        do = _matmul(dmix, w_out, mode="nt", name="mm_out_dx")
        G["w_out"][l] = _matmul(S["o_cat"], dmix, mode="tn", name="mm_out_dw")
        do_attn, do_ssm, do_gmlp = do[:, :dm.ATTN_W], do[:, dm.ATTN_W:dm.ATTN_W + dm.SSM_W], do[:, dm.ATTN_W + dm.SSM_W:]
        ln_g, ln_b = P["gmlp_ln_g"][l][None], P["gmlp_ln_b"][l][None]
        w_s = P["gmlp_w_s"][l].astype(BF16)
        b_s = P["gmlp_b_s"][l][..., None]
        dz_g, dlg, dlb, dws, dbs = _gmlp_bwd(S["z"], ln_g, ln_b, w_s, b_s, do_gmlp, dm, "gmlp_bwd")
        G["gmlp_ln_g"][l], G["gmlp_ln_b"][l], G["gmlp_w_s"][l], G["gmlp_b_s"][l] = dlg[0], dlb[0], dws, dbs[..., 0]
        b_glu = P["ssm_b_glu"][l][None]
        dy, dwglu, dbglu = _glu_bwd(S["y"], w_glu, b_glu, do_ssm, "glu_bwd")
        G["ssm_w_glu"][l], G["ssm_b_glu"][l] = dwglu, dbglu[0]
        dyl, dyc = _interleave(dy[C:], L), _interleave(dy[:C], C)
        tr = lambda a: jnp.swapaxes(a, 1, 2)
        dskip = P["ssm_d"][l][None]
        (dul, duc, da_re, da_im, dbd_re, dbd_im, dcd_re, dcd_im, dd) = _s5_scan_bwd(
            S["ul"], S["uc"], dyl, dyc, S["a_re"], S["a_im"], S["bd_re"], S["bd_im"], S["cd_re"], S["cd_im"],
            tr(S["bd_re"]), tr(S["bd_im"]), tr(S["cd_re"]), tr(S["cd_im"]), dskip, dm, "s5_scan_bwd")
        dz_s = jnp.concatenate([_deinterleave(duc, C), _deinterleave(dul, L)], axis=0)
        dlr, dli, dldt, dbTr, dbTi = _s5_disc_bwd(S["lam_re"], S["lam_im"], S["ldt"], S["bT_re"], S["bT_im"],
                                                  da_re, da_im, _blockdiag_in_T(dbd_re, dm),
                                                  _blockdiag_in_T(dbd_im, dm), "s5_disc_bwd")
        G["ssm_lambda_re"][l] = dlr.reshape(2, dm.SG, SSM_STATE)
        G["ssm_lambda_im"][l] = dli.reshape(2, dm.SG, SSM_STATE)
        G["ssm_log_dt"][l] = jnp.sum(dldt.reshape(2, dm.SG, SSM_STATE), axis=-1)
        G["ssm_b_re"][l] = dbTr.transpose(0, 2, 1).reshape(2, dm.SG, SSM_STATE, SSM_GROUP)
        G["ssm_b_im"][l] = dbTi.transpose(0, 2, 1).reshape(2, dm.SG, SSM_STATE, SSM_GROUP)
        G["ssm_c_re"][l] = _blockdiag_out_T(dcd_re, dm)
        G["ssm_c_im"][l] = _blockdiag_out_T(dcd_im, dm)
        G["ssm_d"][l] = dd[0]
        dq, pk, pv, dkc, dvc, dsk = _attention_bwd(S["q"], S["k2"], S["v2"], S["sinkcol"], do_attn, dm, "attention_bwd")
        G["attn_sink"][l] = dsk[:, :, 0].reshape(dm.NH)
        dk2 = _kv_reduce(pk, dkc, dm, "kv_reduce")
        dv2 = _kv_reduce(pv, dvc, dm, "kv_reduce")
        dz_a = _attn_prep_bwd(dq, dk2, dv2, cos, sin, dm, "attn_prep_bwd")
        dz = jnp.concatenate([dz_a, dz_s, dz_g], axis=-1).astype(BF16)
        dh = _matmul(dz, w_in, mode="nt", name="mm_in_dx")
        G["w_in"][l] = _matmul(S["h"], dz, mode="tn", name="mm_in_dw")
        dX, dg_a, dsh_a, dsc_a = _normmod_bwd(S["X"], g_mix, sc_a, dh, dX1, C, "normmod_bwd")
        G["g_mix"][l] = dg_a[0]
        dmod[l] = jnp.stack([dsh_a, dsc_a, dgt_a, dsh_f, dsc_f, dgt_f], axis=1)

    grad_x = dX[C:].reshape(1, L, D)
    G = {n: jnp.stack(v) for n, v in G.items()}
    G["g_final"] = dg_final[0]
    dmod = jnp.stack(dmod)

    small_shapes = [G[n].shape for n in _SMALL]
    n_small = sum(math.prod(s) for s in small_shapes)
    r_small = -(-n_small // (8 * PACK_W)) * 8
    gs_pack = _pack([G[n] for n in _SMALL], F32, 8)
    dm_pack = _pack([dmod], F32, 8)
    r_dm = dm_pack.shape[0]
    allp = _allgather8(jnp.concatenate([gs_pack, dm_pack], axis=0), "ag_small_bwd")
    gs_sum = _sum_slots(allp[:, :r_small], F32, "sum_small")
    for n, a in zip(_SMALL, _unpack(gs_sum, small_shapes)):
        G[n] = a
    dmod_all = allp[:, r_small:r_small + r_dm].reshape(N_DEV, -1)[:, :DEPTH * 2 * N_MOD * D]
    dmod_all = dmod_all.reshape(N_DEV, DEPTH, 2, N_MOD * D)
    dmod_ctx = _sum_slots(dmod_all[:, :, 0].reshape(N_DEV, DEPTH, N_MOD * D), F32, "sum_dmod_ctx")
    dmod16 = jnp.concatenate([jnp.swapaxes(dmod_all[:, :, 1], 0, 1), dmod_ctx[:, None],
                              jnp.zeros((DEPTH, 16 - N_DEV - 1, N_MOD * D), F32)], axis=1)
    G["b_ada"] = _sum_slots(jnp.swapaxes(dmod16, 0, 1)[:N_DEV + 1], F32, "sum_b_ada")
    dmod16_sh = lax.dynamic_slice_in_dim(dmod16, chip * n_sh, n_sh, axis=2)
    G["w_ada"], dact = _ada_bwd(cond, P["w_ada"], dmod16_sh, "ada_bwd")
    dact4 = _comm(dact, group="chips", by_peer=False, name="ag_dact")
    G["c_ctx"] = _cctx_grad(P["c_ctx"].reshape(1, D), dact4, "cctx_grad")[0]
    conv_cols = conv_sh[2]
    G["ffn_conv_w"] = lax.dynamic_slice_in_dim(G["ffn_conv_w"], chip * conv_cols, conv_cols, axis=2)

    def shard_of(n, j):
        a = G[n]
        ax = _SHARD_AXIS[n]
        k = a.shape[ax] // N_CHIPS
        return lax.slice_in_dim(a, j * k, (j + 1) * k, axis=ax)

    gp = jnp.stack([_pack([shard_of(n, j) for n in _BIG], BF16, 2 * 16) for j in range(N_CHIPS)])
    gp = jnp.swapaxes(gp.reshape(N_CHIPS, 2, R // 2, PACK_W), 0, 1)
    pair = _comm(gp.reshape(2, N_CHIPS * (R // 2), PACK_W), group="sib", by_peer=True, name="rs_pair")
    psum = _sum_slots(pair, BF16, "sum_pair").reshape(N_CHIPS, R // 2, PACK_W)
    quad = _comm(psum, group="chips", by_peer=True, name="rs_chips")
    red = _sum_slots(quad, F32, "sum_chips")
    both = _comm(red, group="sib", by_peer=False, name="rs_share")
    for n, a in zip(_BIG, _unpack(both.reshape(R, PACK_W), big_shapes)):
        G[n] = a

    delta, new_m, new_v = {}, {}, {}
    for n in _BIG + ("w_ada",):
        delta[n], new_m[n], new_v[n] = _adamw_nd(P[n], G[n], M[n], V[n], "adamw_" + n)
    rest = [n for n in _WEIGHTS if n not in _BIG + ("w_ada",)]
    shapes = [P[n].shape for n in rest]
    packed = [_pack([d[n] for n in rest], F32, 8) for d in (P, G, M, V)]
    outs = _adamw(*packed, "adamw_small")
    for dst, buf in zip((delta, new_m, new_v), outs):
        for n, a in zip(rest, _unpack(buf, shapes)):
            dst[n] = a
    return loss, grad_x, G, delta, new_m, new_v


def kernel(x, c, ctx, c_ctx, w_ada, b_ada, g_mix, g_ffn, w_in, w_out, attn_sink, ssm_lambda_re, ssm_lambda_im, ssm_log_dt, ssm_b_re, ssm_b_im, ssm_c_re, ssm_c_im, ssm_d, ssm_w_glu, ssm_b_glu, gmlp_ln_g, gmlp_ln_b, gmlp_w_s, gmlp_b_s, ffn_w_up, ffn_conv_w, ffn_conv_b, ffn_w_down, g_final, loss_target, m_c_ctx, m_w_ada, m_b_ada, m_g_mix, m_g_ffn, m_w_in, m_w_out, m_attn_sink, m_ssm_lambda_re, m_ssm_lambda_im, m_ssm_log_dt, m_ssm_b_re, m_ssm_b_im, m_ssm_c_re, m_ssm_c_im, m_ssm_d, m_ssm_w_glu, m_ssm_b_glu, m_gmlp_ln_g, m_gmlp_ln_b, m_gmlp_w_s, m_gmlp_b_s, m_ffn_w_up, m_ffn_conv_w, m_ffn_conv_b, m_ffn_w_down, m_g_final, v_c_ctx, v_w_ada, v_b_ada, v_g_mix, v_g_ffn, v_w_in, v_w_out, v_attn_sink, v_ssm_lambda_re, v_ssm_lambda_im, v_ssm_log_dt, v_ssm_b_re, v_ssm_b_im, v_ssm_c_re, v_ssm_c_im, v_ssm_d, v_ssm_w_glu, v_ssm_b_glu, v_gmlp_ln_g, v_gmlp_ln_b, v_gmlp_w_s, v_gmlp_b_s, v_ffn_w_up, v_ffn_conv_w, v_ffn_conv_b, v_ffn_w_down, v_g_final):
    env = locals()
    P = {n: env[n] for n in _WEIGHTS}
    M = {n: env["m_" + n] for n in _WEIGHTS}
    V = {n: env["v_" + n] for n in _WEIGHTS}
    loss, grad_x, G, delta, new_m, new_v = _step(P, M, V, x, c, ctx, loss_target)
    return (loss, grad_x, *[G[n] for n in _WEIGHTS], *[delta[n] for n in _WEIGHTS],
            *[new_m[n] for n in _WEIGHTS], *[new_v[n] for n in _WEIGHTS])
```

```python
import functools
import math

import jax
import jax.numpy as jnp
from jax import lax
from jax.experimental import pallas as pl
from jax.experimental.pallas import tpu as pltpu

F32 = jnp.float32
BF16 = jnp.bfloat16

D_MODEL = 2048
SEQ = 4096
DEPTH = 4
CTX_LEN = 256
GRID_W = 64
HEAD_DIM = 64
GQA_RATIO = 8
WINDOW = 128
ATTN_BLOCK = 128
ROPE_BASE = 10000.0
SSM_GROUP = 16
SSM_STATE = 64
GMLP_CHUNK = 128
GMLP_GROUP_W = 128
N_MOD = 6
NORM_EPS = 1e-6
ADAM_LR = 0.001
ADAM_B1 = 0.9
ADAM_B2 = 0.999
ADAM_EPS = 1e-08
ADAM_WD = 0.01
ADAM_STEP = 10

N_CHIPS = 4
N_DEV = 8
SCAN_SEGMENTS = 8
STATE_BLOCK = 256
LANES = 128
PACK_W = 1024
COMM_CHUNKS = (16, 8, 4, 2)
VMEM_LIMIT_MB = 56
_GELU_K = math.sqrt(2.0 / math.pi)
_GELU_C = 0.044715


class _Dims:
    def __init__(self):
        self.D = D_MODEL
        self.L = SEQ
        self.C = CTX_LEN
        self.T = SEQ + CTX_LEN
        self.ATTN_W = D_MODEL // 2
        self.SSM_W = D_MODEL // 4
        self.GMLP_W = D_MODEL - self.ATTN_W - self.SSM_W
        self.NH = self.ATTN_W // HEAD_DIM
        self.NKV = self.NH // GQA_RATIO
        self.KV_W = self.NKV * HEAD_DIM
        self.SG = self.SSM_W // SSM_GROUP
        self.GP = self.SG * SSM_STATE
        self.GG = self.GMLP_W // GMLP_GROUP_W
        self.DFF = ((8 * D_MODEL // 3 + 255) // 256) * 256
        self.OFF_K = self.ATTN_W
        self.OFF_V = self.OFF_K + self.KV_W
        self.OFF_S = self.OFF_V + self.KV_W
        self.OFF_GU = self.OFF_S + self.SSM_W
        self.OFF_GV = self.OFF_GU + self.GMLP_W
        self.N_IN = self.OFF_GV + self.GMLP_W
        assert self.NKV == 2 and self.KV_W == LANES
        assert self.C % (SCAN_SEGMENTS * 8) == 0 and self.L % (SCAN_SEGMENTS * 8) == 0
        assert self.C % ATTN_BLOCK == 0 and self.L % ATTN_BLOCK == 0


def _tile(n, cap, mult):
    best = None
    for t in range(mult, min(n, cap) + 1, mult):
        if n % t == 0:
            best = t
    return best if best is not None else n


def _cparams(sem=None):
    kw = dict(vmem_limit_bytes=VMEM_LIMIT_MB << 20)
    if sem is not None:
        kw["dimension_semantics"] = sem
    return pltpu.CompilerParams(**kw)


def _gelu(x):
    return 0.5 * x * (1.0 + jnp.tanh(_GELU_K * (x + _GELU_C * x * x * x)))


def _gelu_grad(x):
    t = jnp.tanh(_GELU_K * (x + _GELU_C * x * x * x))
    return 0.5 * (1.0 + t) + 0.5 * x * (1.0 - t * t) * _GELU_K * (1.0 + 3.0 * _GELU_C * x * x)


def _sigmoid(x):
    return 1.0 / (1.0 + jnp.exp(-x))


def _dot(a, b, dims):
    return lax.dot_general(a.astype(BF16), b.astype(BF16), (dims, ((), ())), preferred_element_type=F32)


_NN = ((1,), (0,))
_NT = ((1,), (1,))
_TN = ((0,), (0,))


def _comm(x, *, group, by_peer, name):
    n = 2 if group == "sib" else N_CHIPS
    blk = x.shape[1:] if by_peer else x.shape
    npeer = n - 1
    W = blk[-1]
    rows = math.prod(blk[:-1])
    x2 = x.reshape((n, rows, W) if by_peer else (rows, W))
    nchunk = next((k for k in COMM_CHUNKS if rows % (k * 16) == 0), 1)
    cr = rows // nchunk

    def body(x_ref, o_ref, send_sems, recv_sems, local_sem):
        mx, my, mc = lax.axis_index("x"), lax.axis_index("y"), lax.axis_index("c")
        if group == "sib":
            me = mc
            peers = [((mx, my, 1 - mc), 1 - mc)]
        else:
            me = 2 * mx + my
            peers = [((1 - mx, my, mc), 2 * (1 - mx) + my), ((mx, 1 - my, mc), 2 * mx + (1 - my)),
                     ((1 - mx, 1 - my, mc), 2 * (1 - mx) + (1 - my))]

        def src_of(pid):
            return x_ref.at[pid] if by_peer else x_ref

        def chunk(ref, ch):
            return ref.at[pl.ds(ch * cr, cr), :]

        def remote(k, dev, pid, ch=None):
            s, d = src_of(pid), o_ref.at[me]
            if ch is not None:
                s, d = chunk(s, ch), chunk(d, ch)
            return pltpu.make_async_remote_copy(src_ref=s, dst_ref=d, send_sem=send_sems.at[k],
                                                recv_sem=recv_sems.at[k], device_id=dev,
                                                device_id_type=pl.DeviceIdType.MESH)

        for ch in range(nchunk):
            pltpu.make_async_copy(chunk(src_of(me), ch), chunk(o_ref.at[me], ch), local_sem).start()
            for k, (dev, pid) in enumerate(peers):
                remote(k, dev, pid, ch).start()
        for k, (dev, pid) in enumerate(peers):
            remote(k, dev, pid).wait_recv()
        for k, (dev, pid) in enumerate(peers):
            remote(k, dev, pid).wait_send()
        pltpu.make_async_copy(src_of(me), o_ref.at[me], local_sem).wait()

    out = pl.pallas_call(
        body, name=name,
        out_shape=jax.ShapeDtypeStruct((n, rows, W), x.dtype),
        in_specs=[pl.BlockSpec(memory_space=pl.ANY)],
        out_specs=pl.BlockSpec(memory_space=pl.ANY),
        scratch_shapes=[pltpu.SemaphoreType.DMA((npeer,)), pltpu.SemaphoreType.DMA((npeer,)),
                        pltpu.SemaphoreType.DMA],
    )(x2)
    return out.reshape((n,) + tuple(blk))


def _allgather8(x, name):
    a = _comm(x, group="chips", by_peer=False, name=name + "_chips")
    b = _comm(a, group="sib", by_peer=False, name=name + "_sib")
    return jnp.swapaxes(b, 0, 1).reshape((N_DEV,) + x.shape)


def _sum_slots(x, out_dtype, name):
    n, R, W = x.shape
    tr = _tile(R, 512, 16)

    def body(x_ref, o_ref):
        acc = x_ref[0].astype(F32)
        for s in range(1, n):
            acc = acc + x_ref[s].astype(F32)
        o_ref[...] = acc.astype(o_ref.dtype)

    return pl.pallas_call(
        body, name=name, grid=(R // tr,),
        in_specs=[pl.BlockSpec((n, tr, W), lambda i: (0, i, 0))],
        out_specs=pl.BlockSpec((tr, W), lambda i: (i, 0)),
        out_shape=jax.ShapeDtypeStruct((R, W), out_dtype),
        compiler_params=_cparams(("parallel",)),
    )(x)


def _pack(arrs, dtype, row_mult):
    flat = jnp.concatenate([a.reshape(-1).astype(dtype) for a in arrs])
    n = flat.shape[0]
    quant = row_mult * PACK_W
    total = -(-n // quant) * quant
    if total != n:
        flat = jnp.concatenate([flat, jnp.zeros((total - n,), dtype)])
    return flat.reshape(total // PACK_W, PACK_W)


def _unpack(buf, shapes):
    flat = buf.reshape(-1)
    out, off = [], 0
    for s in shapes:
        n = math.prod(s)
        out.append(flat[off:off + n].reshape(s))
        off += n
    return out


def _matmul(a, b, *, mode, name, out_dtype=F32, add=None, tm_cap=1088, tn_cap=1536, tk_cap=2048):
    if mode == "nn":
        (M, K), (_, N) = a.shape, b.shape
    elif mode == "nt":
        (M, K), (N, _) = a.shape, b.shape
    else:
        (K, M), (_, N) = a.shape, b.shape
    tm = _tile(M, tm_cap, 16 if mode != "tn" else LANES)
    tn = _tile(N, tn_cap, LANES)
    tk = _tile(K, tk_cap, LANES if mode != "tn" else 16)
    nk = K // tk
    dims = {"nn": _NN, "nt": _NT, "tn": _TN}[mode]

    def body(*refs):
        if add is None:
            a_ref, b_ref, o_ref, acc_ref = refs
            add_ref = None
        else:
            a_ref, b_ref, add_ref, o_ref, acc_ref = refs
        k = pl.program_id(2)

        @pl.when(k == 0)
        def _():
            acc_ref[...] = jnp.zeros_like(acc_ref)

        acc_ref[...] += _dot(a_ref[...], b_ref[...], dims)

        @pl.when(k == nk - 1)
        def _():
            r = acc_ref[...]
            if add_ref is not None:
                r = r + add_ref[...].astype(F32)
            o_ref[...] = r.astype(o_ref.dtype)

    if mode == "nn":
        a_spec = pl.BlockSpec((tm, tk), lambda i, j, k: (i, k))
        b_spec = pl.BlockSpec((tk, tn), lambda i, j, k: (k, j))
    elif mode == "nt":
        a_spec = pl.BlockSpec((tm, tk), lambda i, j, k: (i, k))
        b_spec = pl.BlockSpec((tn, tk), lambda i, j, k: (j, k))
    else:
        a_spec = pl.BlockSpec((tk, tm), lambda i, j, k: (k, i))
        b_spec = pl.BlockSpec((tk, tn), lambda i, j, k: (k, j))
    o_spec = pl.BlockSpec((tm, tn), lambda i, j, k: (i, j))
    in_specs = [a_spec, b_spec] + ([o_spec] if add is not None else [])
    args = (a, b) + ((add,) if add is not None else ())
    return pl.pallas_call(
        body, name=name, grid=(M // tm, N // tn, nk),
        in_specs=in_specs, out_specs=o_spec,
        out_shape=jax.ShapeDtypeStruct((M, N), out_dtype),
        scratch_shapes=[pltpu.VMEM((tm, tn), F32)],
        compiler_params=_cparams(("parallel", "parallel", "arbitrary")),
    )(*args)


def _matmul_residual(a, w, res, gate, C, name):
    M, K = a.shape
    N = w.shape[1]
    tm = _tile(M, 1088, 16)
    tn = _tile(N, 1024, LANES)
    tk = _tile(K, 2048, LANES)
    nk = K // tk

    def body(a_ref, w_ref, res_ref, gate_ref, mix_ref, o_ref, acc_ref):
        i, k = pl.program_id(0), pl.program_id(2)

        @pl.when(k == 0)
        def _():
            acc_ref[...] = jnp.zeros_like(acc_ref)

        acc_ref[...] += _dot(a_ref[...], w_ref[...], _NN)

        @pl.when(k == nk - 1)
        def _():
            r = acc_ref[...]
            row = i * tm + lax.broadcasted_iota(jnp.int32, (tm, 1), 0)
            g = jnp.where(row < C, gate_ref[0:1, :], gate_ref[1:2, :])
            mix_ref[...] = r
            o_ref[...] = res_ref[...] + g * r

    o_spec = pl.BlockSpec((tm, tn), lambda i, j, k: (i, j))
    return pl.pallas_call(
        body, name=name, grid=(M // tm, N // tn, nk),
        in_specs=[pl.BlockSpec((tm, tk), lambda i, j, k: (i, k)),
                  pl.BlockSpec((tk, tn), lambda i, j, k: (k, j)),
                  o_spec,
                  pl.BlockSpec((2, tn), lambda i, j, k: (0, j))],
        out_specs=[o_spec, o_spec],
        out_shape=[jax.ShapeDtypeStruct((M, N), F32), jax.ShapeDtypeStruct((M, N), F32)],
        scratch_shapes=[pltpu.VMEM((tm, tn), F32)],
        compiler_params=_cparams(("parallel", "parallel", "arbitrary")),
    )(a, w, res, gate)


def _seg_select(i, tm, C, ref):
    row = i * tm + lax.broadcasted_iota(jnp.int32, (tm, 1), 0)
    return row < C, jnp.where(row < C, ref[0:1, :], ref[1:2, :])


def _normmod(X, g, shift, scale, C, name):
    T, D = X.shape
    tm = _tile(T, 512, 16)

    def body(x_ref, g_ref, sh_ref, sc_ref, h_ref):
        i = pl.program_id(0)
        x = x_ref[...]
        r = lax.rsqrt(jnp.mean(x * x, axis=-1, keepdims=True) + NORM_EPS)
        _, sh = _seg_select(i, tm, C, sh_ref)
        _, sc = _seg_select(i, tm, C, sc_ref)
        h_ref[...] = ((x * r) * g_ref[...] * (1.0 + sc) + sh).astype(BF16)

    row = pl.BlockSpec((tm, D), lambda i: (i, 0))
    vec = pl.BlockSpec((1, D), lambda i: (0, 0))
    two = pl.BlockSpec((2, D), lambda i: (0, 0))
    return pl.pallas_call(
        body, name=name, grid=(T // tm,),
        in_specs=[row, vec, two, two], out_specs=row,
        out_shape=jax.ShapeDtypeStruct((T, D), BF16),
        compiler_params=_cparams(("parallel",)),
    )(X, g, shift, scale)


def _normmod_bwd(X, g, scale, dh, dres, C, name):
    T, D = X.shape
    tm = _tile(T, 512, 16)

    def body(x_ref, g_ref, sc_ref, dh_ref, dres_ref, dx_ref, dg_ref, dsh_ref, dsc_ref):
        i = pl.program_id(0)

        @pl.when(i == 0)
        def _():
            dg_ref[...] = jnp.zeros_like(dg_ref)
            dsh_ref[...] = jnp.zeros_like(dsh_ref)
            dsc_ref[...] = jnp.zeros_like(dsc_ref)

        x = x_ref[...]
        dh = dh_ref[...]
        gv = g_ref[...]
        r = lax.rsqrt(jnp.mean(x * x, axis=-1, keepdims=True) + NORM_EPS)
        xh = x * r
        isc, sc = _seg_select(i, tm, C, sc_ref)
        n = xh * gv
        dhn = dh * n
        zero = jnp.zeros_like(dh)
        dsh_ref[0:1, :] += jnp.sum(jnp.where(isc, dh, zero), axis=0, keepdims=True)
        dsh_ref[1:2, :] += jnp.sum(jnp.where(isc, zero, dh), axis=0, keepdims=True)
        dsc_ref[0:1, :] += jnp.sum(jnp.where(isc, dhn, zero), axis=0, keepdims=True)
        dsc_ref[1:2, :] += jnp.sum(jnp.where(isc, zero, dhn), axis=0, keepdims=True)
        dn = dh * (1.0 + sc)
        dg_ref[...] += jnp.sum(dn * xh, axis=0, keepdims=True)
        dxh = dn * gv
        dx_ref[...] = dres_ref[...] + r * (dxh - xh * jnp.mean(dxh * xh, axis=-1, keepdims=True))

    row = pl.BlockSpec((tm, D), lambda i: (i, 0))
    vec = pl.BlockSpec((1, D), lambda i: (0, 0))
    two = pl.BlockSpec((2, D), lambda i: (0, 0))
    return pl.pallas_call(
        body, name=name, grid=(T // tm,),
        in_specs=[row, vec, two, row, row], out_specs=[row, vec, two, two],
        out_shape=[jax.ShapeDtypeStruct((T, D), F32), jax.ShapeDtypeStruct((1, D), F32),
                   jax.ShapeDtypeStruct((2, D), F32), jax.ShapeDtypeStruct((2, D), F32)],
        compiler_params=_cparams(("arbitrary",)),
    )(X, g, scale, dh, dres)


def _gate_bwd(dxo, mix, gate, C, name):
    T, D = dxo.shape
    tm = _tile(T, 512, 16)

    def body(dx_ref, mix_ref, gate_ref, dmix_ref, dgate_ref):
        i = pl.program_id(0)

        @pl.when(i == 0)
        def _():
            dgate_ref[...] = jnp.zeros_like(dgate_ref)

        dx = dx_ref[...]
        isc, gt = _seg_select(i, tm, C, gate_ref)
        dmix_ref[...] = (dx * gt).astype(BF16)
        p = dx * mix_ref[...]
        zero = jnp.zeros_like(p)
        dgate_ref[0:1, :] += jnp.sum(jnp.where(isc, p, zero), axis=0, keepdims=True)
        dgate_ref[1:2, :] += jnp.sum(jnp.where(isc, zero, p), axis=0, keepdims=True)

    row = pl.BlockSpec((tm, D), lambda i: (i, 0))
    two = pl.BlockSpec((2, D), lambda i: (0, 0))
    return pl.pallas_call(
        body, name=name, grid=(T // tm,),
        in_specs=[row, row, two], out_specs=[row, two],
        out_shape=[jax.ShapeDtypeStruct((T, D), BF16), jax.ShapeDtypeStruct((2, D), F32)],
        compiler_params=_cparams(("arbitrary",)),
    )(dxo, mix, gate)


def _conv_masks(T, C):
    row = lax.broadcasted_iota(jnp.int32, (T, 1), 0)
    first = (row == 0) | (row == C)
    last = (row == C - 1) | (row == T - 1)
    return first, last


def _shift_rows(x, first, last):
    T = x.shape[0]
    prev = jnp.where(first, 0.0, pltpu.roll(x, 1, 0))
    nxt = jnp.where(last, 0.0, pltpu.roll(x, T - 1, 0))
    return prev, nxt


def _convact(upg, upv, cw, cb, C, name):
    T, F = upg.shape
    tc = LANES

    def body(g_ref, v_ref, w_ref, b_ref, o_ref):
        gp = g_ref[...]
        first, last = _conv_masks(T, C)
        prev, nxt = _shift_rows(gp, first, last)
        gc = prev * w_ref[0:1, :] + gp * w_ref[1:2, :] + nxt * w_ref[2:3, :] + b_ref[...]
        o_ref[...] = (gc * _sigmoid(gc) * v_ref[...]).astype(BF16)

    col = pl.BlockSpec((T, tc), lambda j: (0, j))
    return pl.pallas_call(
        body, name=name, grid=(F // tc,),
        in_specs=[col, col, pl.BlockSpec((3, tc), lambda j: (0, j)), pl.BlockSpec((1, tc), lambda j: (0, j))],
        out_specs=col, out_shape=jax.ShapeDtypeStruct((T, F), BF16),
        compiler_params=_cparams(("parallel",)),
    )(upg, upv, cw, cb)


def _convact_bwd(upg, upv, cw, cb, dact, C, name):
    T, F = upg.shape
    tc = LANES

    def body(g_ref, v_ref, w_ref, b_ref, da_ref, dg_ref, dv_ref, dw_ref, db_ref):
        gp = g_ref[...]
        val = v_ref[...]
        da = da_ref[...]
        first, last = _conv_masks(T, C)
        prev, nxt = _shift_rows(gp, first, last)
        w0, w1, w2 = w_ref[0:1, :], w_ref[1:2, :], w_ref[2:3, :]
        gc = prev * w0 + gp * w1 + nxt * w2 + b_ref[...]
        s = _sigmoid(gc)
        dv_ref[...] = (da * gc * s).astype(BF16)
        dgc = da * val * (s + gc * s * (1.0 - s))
        db_ref[...] = jnp.sum(dgc, axis=0, keepdims=True)
        dw_ref[0:1, :] = jnp.sum(dgc * prev, axis=0, keepdims=True)
        dw_ref[1:2, :] = jnp.sum(dgc * gp, axis=0, keepdims=True)
        dw_ref[2:3, :] = jnp.sum(dgc * nxt, axis=0, keepdims=True)
        dprev, dnxt = _shift_rows(dgc, first, last)
        dg_ref[...] = (dnxt * w0 + dgc * w1 + dprev * w2).astype(BF16)

    col = pl.BlockSpec((T, tc), lambda j: (0, j))
    w3 = pl.BlockSpec((3, tc), lambda j: (0, j))
    w1 = pl.BlockSpec((1, tc), lambda j: (0, j))
    return pl.pallas_call(
        body, name=name, grid=(F // tc,),
        in_specs=[col, col, w3, w1, col], out_specs=[col, col, w3, w1],
        out_shape=[jax.ShapeDtypeStruct((T, F), BF16), jax.ShapeDtypeStruct((T, F), BF16),
                   jax.ShapeDtypeStruct((3, F), F32), jax.ShapeDtypeStruct((1, F), F32)],
        compiler_params=_cparams(("parallel",)),
    )(upg, upv, cw, cb, dact)


def _rot_half(x):
    W = x.shape[-1]
    lane = lax.broadcasted_iota(jnp.int32, x.shape, x.ndim - 1)
    lo = (lane % 32) < 16
    return jnp.where(lo, -pltpu.roll(x, W - 16, x.ndim - 1), pltpu.roll(x, 16, x.ndim - 1))


def _swap_halves(x):
    return pltpu.roll(x, 64, x.ndim - 1)


def _rope_tables(dm):
    t = jnp.arange(dm.L)
    n_freq = HEAD_DIM // 4
    inv_freq = ROPE_BASE ** (-jnp.arange(n_freq, dtype=F32) / n_freq)
    ang_r = (t // GRID_W).astype(F32)[:, None] * inv_freq
    ang_c = (t % GRID_W).astype(F32)[:, None] * inv_freq
    ang = jnp.concatenate([ang_r, ang_r, ang_c, ang_c], axis=-1)
    cos = jnp.concatenate([jnp.ones((dm.C, HEAD_DIM), F32), jnp.cos(ang)], axis=0)
    sin = jnp.concatenate([jnp.zeros((dm.C, HEAD_DIM), F32), jnp.sin(ang)], axis=0)
    return jnp.tile(cos, (1, 2)), jnp.tile(sin, (1, 2))


def _attn_prep(z, cos, sin, dm, name):
    T = dm.T
    AW = dm.ATTN_W
    tm = _tile(T, 256, 16)
    rep = AW // LANES

    def body(z_ref, cos_ref, sin_ref, q_ref, k_ref, v_ref):
        cs, sn = cos_ref[...], sin_ref[...]
        q = z_ref[:, 0:AW]
        csq, snq = jnp.tile(cs, (1, rep)), jnp.tile(sn, (1, rep))
        q_ref[...] = (q * csq + _rot_half(q) * snq).astype(BF16)
        k = z_ref[:, dm.OFF_K:dm.OFF_V]
        k = k * cs + _rot_half(k) * sn
        v = z_ref[:, dm.OFF_V:dm.OFF_S]
        lo = lax.broadcasted_iota(jnp.int32, k.shape, 1) < HEAD_DIM
        ks, vs = _swap_halves(k), _swap_halves(v)
        k_ref[0] = jnp.where(lo, k, ks).astype(BF16)
        k_ref[1] = jnp.where(lo, ks, k).astype(BF16)
        v_ref[0] = jnp.where(lo, v, vs).astype(BF16)
        v_ref[1] = jnp.where(lo, vs, v).astype(BF16)

    tab = pl.BlockSpec((tm, LANES), lambda i: (i, 0))
    kv = pl.BlockSpec((2, tm, LANES), lambda i: (0, i, 0))
    return pl.pallas_call(
        body, name=name, grid=(T // tm,),
        in_specs=[pl.BlockSpec((tm, dm.OFF_S), lambda i: (i, 0)), tab, tab],
        out_specs=[pl.BlockSpec((tm, AW), lambda i: (i, 0)), kv, kv],
        out_shape=[jax.ShapeDtypeStruct((T, AW), BF16), jax.ShapeDtypeStruct((2, T, LANES), BF16),
                   jax.ShapeDtypeStruct((2, T, LANES), BF16)],
        compiler_params=_cparams(("parallel",)),
    )(z, cos, sin)


def _attn_prep_bwd(dq, dk2, dv2, cos, sin, dm, name):
    T = dm.T
    AW = dm.ATTN_W
    tm = _tile(T, 256, 16)
    rep = AW // LANES

    def body(dq_ref, dk_ref, dv_ref, cos_ref, sin_ref, o_ref):
        cs, sn = cos_ref[...], sin_ref[...]
        csq, snq = jnp.tile(cs, (1, rep)), jnp.tile(sn, (1, rep))
        dq = dq_ref[...]
        o_ref[:, 0:AW] = dq * csq - _rot_half(dq * snq)
        lo = lax.broadcasted_iota(jnp.int32, (tm, LANES), 1) < HEAD_DIM
        dk = jnp.where(lo, dk_ref[0], dk_ref[1])
        o_ref[:, dm.OFF_K:dm.OFF_V] = dk * cs - _rot_half(dk * sn)
        o_ref[:, dm.OFF_V:dm.OFF_S] = jnp.where(lo, dv_ref[0], dv_ref[1])

    tab = pl.BlockSpec((tm, LANES), lambda i: (i, 0))
    kv = pl.BlockSpec((2, tm, LANES), lambda i: (0, i, 0))
    return pl.pallas_call(
        body, name=name, grid=(T // tm,),
        in_specs=[pl.BlockSpec((tm, AW), lambda i: (i, 0)), kv, kv, tab, tab],
        out_specs=pl.BlockSpec((tm, dm.OFF_S), lambda i: (i, 0)),
        out_shape=jax.ShapeDtypeStruct((T, dm.OFF_S), F32),
        compiler_params=_cparams(("parallel",)),
    )(dq, dk2, dv2, cos, sin)


def _attn_specs(dm):
    B = ATTN_BLOCK
    nblk = dm.T // B
    q_spec = pl.BlockSpec((B, dm.ATTN_W), lambda i: (i, 0))
    prev = pl.BlockSpec((2, B, LANES), lambda i: (0, jnp.maximum(i - 1, 0), 0))
    own = pl.BlockSpec((2, B, LANES), lambda i: (0, i, 0))
    nxt = pl.BlockSpec((2, B, LANES), lambda i: (0, jnp.minimum(i + 1, nblk - 1), 0))
    ctx = pl.BlockSpec((2, dm.C, LANES), lambda i: (0, 0, 0))
    sink = pl.BlockSpec((2, GQA_RATIO * B, 1), lambda i: (0, 0, 0))
    return nblk, q_spec, prev, own, nxt, ctx, sink


def _attn_scores(i, q_ref, kp_ref, ko_ref, kn_ref, kc_ref, sink_ref, h, dm):
    B = ATTN_BLOCK
    nctx = dm.C // B
    nb = dm.L // B
    npair = GQA_RATIO // 2
    lo = lax.broadcasted_iota(jnp.int32, (B, LANES), 1) < HEAD_DIM
    zero = jnp.zeros((B, LANES), BF16)
    parts = []
    for p in range(npair):
        q2 = q_ref[:, (h * npair + p) * LANES:(h * npair + p + 1) * LANES]
        parts.append(jnp.where(lo, q2, zero))
        parts.append(jnp.where(lo, zero, q2))
    q8 = jnp.concatenate(parts, axis=0)
    kcat = jnp.concatenate([kp_ref[h], ko_ref[h], kn_ref[h], kc_ref[h]], axis=0)
    s = _dot(q8, kcat, _NT) * (HEAD_DIM ** -0.5)
    R, NK = s.shape
    iq = lax.broadcasted_iota(jnp.int32, (R, NK), 0) % B
    col = lax.broadcasted_iota(jnp.int32, (R, NK), 1)
    jk = col % B
    qb = i - nctx
    lat = qb >= 0
    m_prev = (col < B) & lat & (qb >= 1) & (jk >= iq)
    m_own = (col >= B) & (col < 2 * B) & lat
    m_next = (col >= 2 * B) & (col < 3 * B) & lat & (qb <= nb - 2) & (jk <= iq)
    mask = m_prev | m_own | m_next | (col >= 3 * B)
    s = jnp.where(mask, s, -jnp.inf)
    sk = sink_ref[h]
    m = jnp.maximum(jnp.max(s, axis=-1, keepdims=True), sk)
    e = jnp.exp(s - m)
    es = jnp.exp(sk - m)
    inv = 1.0 / (jnp.sum(e, axis=-1, keepdims=True) + es)
    return q8, kcat, e * inv, es * inv, lo


def _unstack_heads(x8, lo):
    B = ATTN_BLOCK
    out = []
    for p in range(GQA_RATIO // 2):
        a = x8[(2 * p) * B:(2 * p + 1) * B]
        b = x8[(2 * p + 1) * B:(2 * p + 2) * B]
        out.append(jnp.where(lo, a, b))
    return out


def _attention(q, k2, v2, sinkcol, dm, name):
    B = ATTN_BLOCK
    nblk, q_spec, prev, own, nxt, ctx, sink = _attn_specs(dm)
    npair = GQA_RATIO // 2

    def body(q_ref, kp_ref, ko_ref, kn_ref, kc_ref, vp_ref, vo_ref, vn_ref, vc_ref, sink_ref, o_ref):
        i = pl.program_id(0)
        for h in range(2):
            _, _, p, _, lo = _attn_scores(i, q_ref, kp_ref, ko_ref, kn_ref, kc_ref, sink_ref, h, dm)
            vcat = jnp.concatenate([vp_ref[h], vo_ref[h], vn_ref[h], vc_ref[h]], axis=0)
            o8 = _dot(p, vcat, _NN)
            for pi, o2 in enumerate(_unstack_heads(o8, lo)):
                c0 = (h * npair + pi) * LANES
                o_ref[:, c0:c0 + LANES] = o2.astype(BF16)

    return pl.pallas_call(
        body, name=name, grid=(nblk,),
        in_specs=[q_spec, prev, own, nxt, ctx, prev, own, nxt, ctx, sink],
        out_specs=q_spec, out_shape=jax.ShapeDtypeStruct((dm.T, dm.ATTN_W), BF16),
        compiler_params=_cparams(("parallel",)),
    )(q, k2, k2, k2, k2, v2, v2, v2, v2, sinkcol)


def _attention_bwd(q, k2, v2, sinkcol, do, dm, name):
    B = ATTN_BLOCK
    nblk, q_spec, prev, own, nxt, ctx, sink = _attn_specs(dm)
    npair = GQA_RATIO // 2

    def body(q_ref, kp_ref, ko_ref, kn_ref, kc_ref, vp_ref, vo_ref, vn_ref, vc_ref, sink_ref, do_ref,
             dq_ref, pk_ref, pv_ref, dkc_ref, dvc_ref, dsk_ref):
        i = pl.program_id(0)

        @pl.when(i == 0)
        def _():
            dkc_ref[...] = jnp.zeros_like(dkc_ref)
            dvc_ref[...] = jnp.zeros_like(dvc_ref)
            dsk_ref[...] = jnp.zeros_like(dsk_ref)

        for h in range(2):
            q8, kcat, p, ps, lo = _attn_scores(i, q_ref, kp_ref, ko_ref, kn_ref, kc_ref, sink_ref, h, dm)
            vcat = jnp.concatenate([vp_ref[h], vo_ref[h], vn_ref[h], vc_ref[h]], axis=0)
            zero = jnp.zeros((B, LANES), F32)
            parts = []
            for pi in range(npair):
                d2 = do_ref[:, (h * npair + pi) * LANES:(h * npair + pi + 1) * LANES]
                parts.append(jnp.where(lo, d2, zero))
                parts.append(jnp.where(lo, zero, d2))
            do8 = jnp.concatenate(parts, axis=0)
            dp = _dot(do8, vcat, _NT)
            delta = jnp.sum(p * dp, axis=-1, keepdims=True)
            ds = p * (dp - delta) * (HEAD_DIM ** -0.5)
            dsr = -ps * delta
            rows = [jnp.broadcast_to(jnp.sum(dsr[r * B:(r + 1) * B], axis=0, keepdims=True), (1, LANES))
                    for r in range(GQA_RATIO)]
            dsk_ref[h] += jnp.concatenate(rows, axis=0)
            dq8 = _dot(ds, kcat, _NN)
            for pi, d2 in enumerate(_unstack_heads(dq8, lo)):
                c0 = (h * npair + pi) * LANES
                dq_ref[:, c0:c0 + LANES] = d2
            dk = _dot(ds, q8, _TN)
            dv = _dot(p, do8, _TN)
            dk = dk + _swap_halves(dk)
            dv = dv + _swap_halves(dv)
            for w in range(3):
                pk_ref[0, h, w] = dk[w * B:(w + 1) * B]
                pv_ref[0, h, w] = dv[w * B:(w + 1) * B]
            dkc_ref[h] += dk[3 * B:]
            dvc_ref[h] += dv[3 * B:]

    part = pl.BlockSpec((1, 2, 3, B, LANES), lambda i: (i, 0, 0, 0, 0))
    acc_c = pl.BlockSpec((2, dm.C, LANES), lambda i: (0, 0, 0))
    acc_s = pl.BlockSpec((2, GQA_RATIO, LANES), lambda i: (0, 0, 0))
    return pl.pallas_call(
        body, name=name, grid=(nblk,),
        in_specs=[q_spec, prev, own, nxt, ctx, prev, own, nxt, ctx, sink, q_spec],
        out_specs=[q_spec, part, part, acc_c, acc_c, acc_s],
        out_shape=[jax.ShapeDtypeStruct((dm.T, dm.ATTN_W), F32),
                   jax.ShapeDtypeStruct((nblk, 2, 3, B, LANES), F32),
                   jax.ShapeDtypeStruct((nblk, 2, 3, B, LANES), F32),
                   jax.ShapeDtypeStruct((2, dm.C, LANES), F32), jax.ShapeDtypeStruct((2, dm.C, LANES), F32),
                   jax.ShapeDtypeStruct((2, GQA_RATIO, LANES), F32)],
        compiler_params=_cparams(("arbitrary",)),
    )(q, k2, k2, k2, k2, v2, v2, v2, v2, sinkcol, do)


def _kv_reduce(part, ctxacc, dm, name):
    B = ATTN_BLOCK
    nblk = dm.T // B
    nctx = dm.C // B

    def body(own_ref, nxt_ref, prv_ref, c_ref, o_ref):
        j = pl.program_id(0)
        acc = own_ref[0, :, 0]
        acc = acc + jnp.where(j < nblk - 1, nxt_ref[0, :, 0], 0.0)
        acc = acc + jnp.where(j > 0, prv_ref[0, :, 0], 0.0)
        acc = acc + jnp.where(j < nctx, c_ref[...], 0.0)
        o_ref[...] = acc

    def spec(w, f):
        return pl.BlockSpec((1, 2, 1, B, LANES), lambda j: (f(j), 0, w, 0, 0))

    return pl.pallas_call(
        body, name=name, grid=(nblk,),
        in_specs=[spec(1, lambda j: j), spec(0, lambda j: jnp.minimum(j + 1, nblk - 1)),
                  spec(2, lambda j: jnp.maximum(j - 1, 0)),
                  pl.BlockSpec((2, B, LANES), lambda j: (0, jnp.minimum(j, nctx - 1), 0))],
        out_specs=pl.BlockSpec((2, B, LANES), lambda j: (0, j, 0)),
        out_shape=jax.ShapeDtypeStruct((2, dm.T, LANES), F32),
        compiler_params=_cparams(("parallel",)),
    )(part, part, part, ctxacc)


def _gmlp_core(z_ref, lg_ref, lb_ref, ws_ref, bs_ref, dm):
    GW = dm.GMLP_W
    gu = z_ref[:, dm.OFF_GU:dm.OFF_GV]
    gv = z_ref[:, dm.OFF_GV:dm.N_IN]
    u = _gelu(gu)
    ge = _gelu(gv)
    mu = jnp.mean(ge, axis=-1, keepdims=True)
    xc = ge - mu
    r = lax.rsqrt(jnp.mean(xc * xc, axis=-1, keepdims=True) + NORM_EPS)
    xh = xc * r
    v = xh * lg_ref[...] + lb_ref[...]
    mixed = []
    for g in range(dm.GG):
        vg = v[:, g * GMLP_GROUP_W:(g + 1) * GMLP_GROUP_W]
        mixed.append(_dot(ws_ref[g], vg, _NN) + bs_ref[g])
    return gu, gv, u, xh, r, v, mixed


def _gmlp(z, ln_g, ln_b, w_s, b_s, dm, name):
    T, GW, CH = dm.T, dm.GMLP_W, GMLP_CHUNK

    def body(z_ref, lg_ref, lb_ref, ws_ref, bs_ref, o_ref):
        _, _, u, _, _, _, mixed = _gmlp_core(z_ref, lg_ref, lb_ref, ws_ref, bs_ref, dm)
        for g in range(dm.GG):
            sl = slice(g * GMLP_GROUP_W, (g + 1) * GMLP_GROUP_W)
            o_ref[:, sl] = (u[:, sl] * mixed[g]).astype(BF16)

    vec = pl.BlockSpec((1, GW), lambda i: (0, 0))
    return pl.pallas_call(
        body, name=name, grid=(T // CH,),
        in_specs=[pl.BlockSpec((CH, dm.N_IN), lambda i: (i, 0)), vec, vec,
                  pl.BlockSpec((dm.GG, CH, CH), lambda i: (0, 0, 0)),
                  pl.BlockSpec((dm.GG, CH, 1), lambda i: (0, 0, 0))],
        out_specs=pl.BlockSpec((CH, GW), lambda i: (i, 0)),
        out_shape=jax.ShapeDtypeStruct((T, GW), BF16),
        compiler_params=_cparams(("parallel",)),
    )(z, ln_g, ln_b, w_s, b_s)


def _gmlp_bwd(z, ln_g, ln_b, w_s, b_s, do, dm, name):
    T, GW, CH = dm.T, dm.GMLP_W, GMLP_CHUNK

    def body(z_ref, lg_ref, lb_ref, ws_ref, bs_ref, do_ref, dz_ref, dlg_ref, dlb_ref, dws_ref, dbs_ref):
        i = pl.program_id(0)

        @pl.when(i == 0)
        def _():
            dlg_ref[...] = jnp.zeros_like(dlg_ref)
            dlb_ref[...] = jnp.zeros_like(dlb_ref)
            dws_ref[...] = jnp.zeros_like(dws_ref)
            dbs_ref[...] = jnp.zeros_like(dbs_ref)

        gu, gv, u, xh, r, v, mixed = _gmlp_core(z_ref, lg_ref, lb_ref, ws_ref, bs_ref, dm)
        do = do_ref[...]
        dv_parts = []
        for g in range(dm.GG):
            sl = slice(g * GMLP_GROUP_W, (g + 1) * GMLP_GROUP_W)
            dog = do[:, sl]
            dz_ref[:, sl] = dog * mixed[g] * _gelu_grad(gu[:, sl])
            dmx = dog * u[:, sl]
            dbs_ref[g] += jnp.sum(dmx, axis=-1, keepdims=True)
            dws_ref[g] += _dot(dmx, v[:, sl], _NT)
            dv_parts.append(_dot(ws_ref[g], dmx, _TN))
        dv = jnp.concatenate(dv_parts, axis=-1) if dm.GG > 1 else dv_parts[0]
        dlg_ref[...] += jnp.sum(dv * xh, axis=0, keepdims=True)
        dlb_ref[...] += jnp.sum(dv, axis=0, keepdims=True)
        dxh = dv * lg_ref[...]
        dge = r * (dxh - jnp.mean(dxh, axis=-1, keepdims=True) - xh * jnp.mean(dxh * xh, axis=-1, keepdims=True))
        dz_ref[:, GW:2 * GW] = dge * _gelu_grad(gv)

    vec = pl.BlockSpec((1, GW), lambda i: (0, 0))
    wsp = pl.BlockSpec((dm.GG, CH, CH), lambda i: (0, 0, 0))
    bsp = pl.BlockSpec((dm.GG, CH, 1), lambda i: (0, 0, 0))
    return pl.pallas_call(
        body, name=name, grid=(T // CH,),
        in_specs=[pl.BlockSpec((CH, dm.N_IN), lambda i: (i, 0)), vec, vec, wsp, bsp,
                  pl.BlockSpec((CH, GW), lambda i: (i, (dm.ATTN_W + dm.SSM_W) // GW))],
        out_specs=[pl.BlockSpec((CH, 2 * GW), lambda i: (i, 0)), vec, vec, wsp, bsp],
        out_shape=[jax.ShapeDtypeStruct((T, 2 * GW), F32), jax.ShapeDtypeStruct((1, GW), F32),
                   jax.ShapeDtypeStruct((1, GW), F32), jax.ShapeDtypeStruct((dm.GG, CH, CH), F32),
                   jax.ShapeDtypeStruct((dm.GG, CH, 1), F32)],
        compiler_params=_cparams(("arbitrary",)),
    )(z, ln_g, ln_b, w_s, b_s, do)


def _disc_fn(lam_re, lam_im, ldt, b_re, b_im):
    dt = jnp.exp(ldt)
    mag = jnp.exp(lam_re * dt)
    a_re = mag * jnp.cos(lam_im * dt)
    a_im = mag * jnp.sin(lam_im * dt)
    den = lam_re * lam_re + lam_im * lam_im
    n_re = a_re - 1.0
    f_re = (n_re * lam_re + a_im * lam_im) / den
    f_im = (a_im * lam_re - n_re * lam_im) / den
    return a_re, a_im, f_re * b_re - f_im * b_im, f_re * b_im + f_im * b_re


def _s5_disc(lam_re, lam_im, ldt, bT_re, bT_im, name):
    nd, H, GP = bT_re.shape

    def body(lr_ref, li_ref, dt_ref, br_ref, bi_ref, ar_ref, ai_ref, bbr_ref, bbi_ref):
        for d in range(nd):
            a_re, a_im, bb_re, bb_im = _disc_fn(lr_ref[d], li_ref[d], dt_ref[d], br_ref[d], bi_ref[d])
            ar_ref[d] = a_re
            ai_ref[d] = a_im
            bbr_ref[d] = bb_re
            bbi_ref[d] = bb_im

    vs = jax.ShapeDtypeStruct((nd, 1, GP), F32)
    ms = jax.ShapeDtypeStruct((nd, H, GP), F32)
    return pl.pallas_call(body, name=name, out_shape=[vs, vs, ms, ms], compiler_params=_cparams())(
        lam_re, lam_im, ldt, bT_re, bT_im)


def _s5_disc_bwd(lam_re, lam_im, ldt, bT_re, bT_im, da_re, da_im, dbb_re, dbb_im, name):
    nd, H, GP = bT_re.shape

    def body(lr_ref, li_ref, dt_ref, br_ref, bi_ref, dar_ref, dai_ref, dbr_ref, dbi_ref,
             olr_ref, oli_ref, odt_ref, obr_ref, obi_ref):
        for d in range(nd):
            _, vjp = jax.vjp(_disc_fn, lr_ref[d], li_ref[d], dt_ref[d], br_ref[d], bi_ref[d])
            g = vjp((dar_ref[d], dai_ref[d], dbr_ref[d], dbi_ref[d]))
            olr_ref[d], oli_ref[d], odt_ref[d], obr_ref[d], obi_ref[d] = g

    vs = jax.ShapeDtypeStruct((nd, 1, GP), F32)
    ms = jax.ShapeDtypeStruct((nd, H, GP), F32)
    return pl.pallas_call(body, name=name, out_shape=[vs, vs, vs, ms, ms], compiler_params=_cparams())(
        lam_re, lam_im, ldt, bT_re, bT_im, da_re, da_im, dbb_re, dbb_im)


def _scan_pass(re_ref, im_ref, nsteps, ar, ai, ir, ii, reverse, store):
    def step(n, carry):
        sr, si = carry
        t = (nsteps - 1 - n) if reverse else n
        off = pl.multiple_of(t * 8, 8)
        nr = ar * sr - ai * si + re_ref[pl.ds(off, 8), :]
        ni = ar * si + ai * sr + im_ref[pl.ds(off, 8), :]
        if store:
            re_ref[pl.ds(off, 8), :] = nr
            im_ref[pl.ds(off, 8), :] = ni
        return nr, ni

    return lax.fori_loop(0, nsteps, step, (ir, ii), unroll=4)


def _cpow(ar, ai, n):
    rr = ri = None
    br, bi = ar, ai
    while n:
        if n & 1:
            rr, ri = (br, bi) if rr is None else (rr * br - ri * bi, rr * bi + ri * br)
        n >>= 1
        if n:
            br, bi = br * br - bi * bi, 2.0 * br * bi
    return rr, ri


def _row_of(x, k):
    rows = lax.broadcasted_iota(jnp.int32, x.shape, 0)
    return jnp.sum(jnp.where(rows == k, x, 0.0), axis=0, keepdims=True)


def _full_scan(re_ref, im_ref, nsteps, ar1, ai1, h0r, h0i, reverse):
    P = ar1.shape[1]
    ar, ai = jnp.broadcast_to(ar1, (8, P)), jnp.broadcast_to(ai1, (8, P))
    z = jnp.zeros((8, P), F32)
    er, ei = _scan_pass(re_ref, im_ref, nsteps, ar, ai, z, z, reverse, False)
    pr, pi = _cpow(ar1, ai1, nsteps)
    rows = lax.broadcasted_iota(jnp.int32, (8, P), 0)
    initr, initi = z, z
    cr, ci = h0r, h0i
    for k in (range(SCAN_SEGMENTS - 1, -1, -1) if reverse else range(SCAN_SEGMENTS)):
        initr = jnp.where(rows == k, cr, initr)
        initi = jnp.where(rows == k, ci, initi)
        ekr, eki = _row_of(er, k), _row_of(ei, k)
        cr, ci = ekr + pr * cr - pi * ci, eki + pr * ci + pi * cr
    _scan_pass(re_ref, im_ref, nsteps, ar, ai, initr, initi, reverse, True)
    return initr, initi, cr, ci


def _rows_loop(n, chunk, fn):
    def step(c, carry):
        fn(pl.multiple_of(c * chunk, chunk))
        return carry
    lax.fori_loop(0, n // chunk, step, 0)


def _s5_specs(dm):
    PB = STATE_BLOCK
    nsb = dm.GP // PB
    per = (LANES * SSM_STATE // SSM_GROUP) // PB
    ul = pl.BlockSpec((dm.L, LANES), lambda j: (0, j // per))
    uc = pl.BlockSpec((dm.C, LANES), lambda j: (0, j // per))
    av = pl.BlockSpec((2, 1, PB), lambda j: (0, 0, j))
    bd = pl.BlockSpec((2, LANES, PB), lambda j: (0, j // per, j))
    cd = pl.BlockSpec((2, PB, LANES), lambda j: (0, j, j // per))
    dv = pl.BlockSpec((1, LANES), lambda j: (0, j // per))
    return PB, nsb, per, ul, uc, av, bd, cd, dv


def _s5_scan(ul, uc, a_re, a_im, bd_re, bd_im, cd_re, cd_im, dskip, dm, name):
    PB, nsb, per, ul_s, uc_s, av, bd, cd, dv = _s5_specs(dm)
    L, C = dm.L, dm.C
    RC = _tile(L, 512, 8)
    RCC = _tile(C, 512, 8)

    def body(ul_ref, uc_ref, ar_ref, ai_ref, br_ref, bi_ref, cr_ref, ci_ref, d_ref, yl_ref, yc_ref,
             lre, lim, cre, cim):
        j = pl.program_id(0)

        @pl.when(j % per == 0)
        def _():
            yl_ref[...] = ul_ref[...] * d_ref[...]
            yc_ref[...] = uc_ref[...] * d_ref[...]

        for d in range(2):
            rev = d == 1
            ar1, ai1 = ar_ref[d], ai_ref[d]

            def proj(u_ref, re, im, chunk, n):
                def f(r0):
                    u = u_ref[pl.ds(r0, chunk), :]
                    re[pl.ds(r0, chunk), :] = _dot(u, br_ref[d], _NN)
                    im[pl.ds(r0, chunk), :] = _dot(u, bi_ref[d], _NN)
                _rows_loop(n, chunk, f)

            def readout(y_ref, re, im, chunk, n):
                def f(r0):
                    y_ref[pl.ds(r0, chunk), :] += (_dot(re[pl.ds(r0, chunk), :], cr_ref[d], _NN)
                                                   - _dot(im[pl.ds(r0, chunk), :], ci_ref[d], _NN))
                _rows_loop(n, chunk, f)

            z1 = jnp.zeros((1, PB), F32)
            proj(uc_ref, cre, cim, RCC, C)
            _, _, fr, fi = _full_scan(cre, cim, C // 8, ar1, ai1, z1, z1, rev)
            readout(yc_ref, cre, cim, RCC, C)
            proj(ul_ref, lre, lim, RC, L)
            _full_scan(lre, lim, L // 8, ar1, ai1, fr, fi, rev)
            readout(yl_ref, lre, lim, RC, L)

    return pl.pallas_call(
        body, name=name, grid=(nsb,),
        in_specs=[ul_s, uc_s, av, av, bd, bd, cd, cd, dv],
        out_specs=[ul_s, uc_s],
        out_shape=[jax.ShapeDtypeStruct((L, dm.SSM_W), F32), jax.ShapeDtypeStruct((C, dm.SSM_W), F32)],
        scratch_shapes=[pltpu.VMEM((L, PB), F32), pltpu.VMEM((L, PB), F32),
                        pltpu.VMEM((C, PB), F32), pltpu.VMEM((C, PB), F32)],
        compiler_params=_cparams(("arbitrary",)),
    )(ul, uc, a_re, a_im, bd_re, bd_im, cd_re, cd_im, dskip)


def _s5_scan_bwd(ul, uc, dyl, dyc, a_re, a_im, bd_re, bd_im, cd_re, cd_im, bdT_re, bdT_im, cdT_re, cdT_im,
                 dskip, dm, name):
    PB, nsb, per, ul_s, uc_s, av, bd, cd, dv = _s5_specs(dm)
    L, C = dm.L, dm.C
    RC = _tile(L, 512, 8)
    RCC = _tile(C, 512, 8)
    bdT = pl.BlockSpec((2, PB, LANES), lambda j: (0, j, j // per))
    cdT = pl.BlockSpec((2, LANES, PB), lambda j: (0, j // per, j))
    dbd = pl.BlockSpec((2, 1, LANES, PB), lambda j: (0, j, 0, 0))
    dcd = pl.BlockSpec((2, 1, PB, LANES), lambda j: (0, j, 0, 0))

    def body(ul_ref, uc_ref, dyl_ref, dyc_ref, ar_ref, ai_ref, br_ref, bi_ref, cr_ref, ci_ref,
             brT_ref, biT_ref, crT_ref, ciT_ref, d_ref,
             dul_ref, duc_ref, dar_ref, dai_ref, dbr_ref, dbi_ref, dcr_ref, dci_ref, dd_ref,
             lre, lim, cre, cim, gre, gim, hre, him):
        j = pl.program_id(0)

        @pl.when(j % per == 0)
        def _():
            dul_ref[...] = dyl_ref[...] * d_ref[...]
            duc_ref[...] = dyc_ref[...] * d_ref[...]
            dd_ref[...] = (jnp.sum(dyl_ref[...] * ul_ref[...], axis=0, keepdims=True)
                           + jnp.sum(dyc_ref[...] * uc_ref[...], axis=0, keepdims=True))

        for d in range(2):
            rev = d == 1
            ar1, ai1 = ar_ref[d], ai_ref[d]
            z1 = jnp.zeros((1, PB), F32)

            def proj(u_ref, w_re, w_im, sign, re, im, chunk, n):
                def f(r0):
                    u = u_ref[pl.ds(r0, chunk), :]
                    re[pl.ds(r0, chunk), :] = _dot(u, w_re, _NN)
                    im[pl.ds(r0, chunk), :] = sign * _dot(u, w_im, _NN)
                _rows_loop(n, chunk, f)

            proj(uc_ref, br_ref[d], bi_ref[d], 1.0, cre, cim, RCC, C)
            icr, ici, fr, fi = _full_scan(cre, cim, C // 8, ar1, ai1, z1, z1, rev)
            proj(ul_ref, br_ref[d], bi_ref[d], 1.0, lre, lim, RC, L)
            ilr, ili, _, _ = _full_scan(lre, lim, L // 8, ar1, ai1, fr, fi, rev)
            proj(dyl_ref, crT_ref[d], ciT_ref[d], -1.0, gre, gim, RC, L)
            _, _, gfr, gfi = _full_scan(gre, gim, L // 8, ar1, -ai1, z1, z1, not rev)
            proj(dyc_ref, crT_ref[d], ciT_ref[d], -1.0, hre, him, RCC, C)
            _full_scan(hre, him, C // 8, ar1, -ai1, gfr, gfi, not rev)

            dar = jnp.zeros((8, PB), F32)
            dai = jnp.zeros((8, PB), F32)
            dbr = jnp.zeros((LANES, PB), F32)
            dbi = jnp.zeros((LANES, PB), F32)
            dcr = jnp.zeros((PB, LANES), F32)
            dci = jnp.zeros((PB, LANES), F32)
            for (u_ref, dy_ref, du_ref, sre, sim, are, aim, inr, ini, n, chunk) in (
                    (ul_ref, dyl_ref, dul_ref, lre, lim, gre, gim, ilr, ili, L, RC),
                    (uc_ref, dyc_ref, duc_ref, cre, cim, hre, him, icr, ici, C, RCC)):
                nst = n // 8

                def da_step(t, carry, sre=sre, sim=sim, are=are, aim=aim, nst=nst):
                    xr, xi = carry
                    tp = (t + 1) if rev else (t - 1)
                    o = pl.multiple_of(t * 8, 8)
                    op = pl.multiple_of(tp * 8, 8)
                    g_r, g_i = are[pl.ds(o, 8), :], aim[pl.ds(o, 8), :]
                    p_r, p_i = sre[pl.ds(op, 8), :], sim[pl.ds(op, 8), :]
                    return xr + g_r * p_r + g_i * p_i, xi + g_i * p_r - g_r * p_i

                lo_t, hi_t = (0, nst - 1) if rev else (1, nst)
                dar, dai = lax.fori_loop(lo_t, hi_t, da_step, (dar, dai))
                e0 = (nst - 1) * 8 if rev else 0
                g_r, g_i = are[pl.ds(e0, 8), :], aim[pl.ds(e0, 8), :]
                dar = dar + g_r * inr + g_i * ini
                dai = dai + g_i * inr - g_r * ini

                def mats(c, carry, u_ref=u_ref, dy_ref=dy_ref, du_ref=du_ref, sre=sre, sim=sim, are=are, aim=aim,
                         chunk=chunk):
                    xbr, xbi, xcr, xci = carry
                    r0 = pl.multiple_of(c * chunk, chunk)
                    u = u_ref[pl.ds(r0, chunk), :]
                    dy = dy_ref[pl.ds(r0, chunk), :]
                    g_r, g_i = are[pl.ds(r0, chunk), :], aim[pl.ds(r0, chunk), :]
                    du_ref[pl.ds(r0, chunk), :] += _dot(g_r, brT_ref[d], _NN) + _dot(g_i, biT_ref[d], _NN)
                    xbr = xbr + _dot(u, g_r, _TN)
                    xbi = xbi + _dot(u, g_i, _TN)
                    xcr = xcr + _dot(sre[pl.ds(r0, chunk), :], dy, _TN)
                    xci = xci - _dot(sim[pl.ds(r0, chunk), :], dy, _TN)
                    return xbr, xbi, xcr, xci

                dbr, dbi, dcr, dci = lax.fori_loop(0, n // chunk, mats, (dbr, dbi, dcr, dci))
            dar_ref[d] = jnp.sum(dar, axis=0, keepdims=True)
            dai_ref[d] = jnp.sum(dai, axis=0, keepdims=True)
            dbr_ref[d, 0] = dbr
            dbi_ref[d, 0] = dbi
            dcr_ref[d, 0] = dcr
            dci_ref[d, 0] = dci

    W, GP = dm.SSM_W, dm.GP
    return pl.pallas_call(
        body, name=name, grid=(nsb,),
        in_specs=[ul_s, uc_s, ul_s, uc_s, av, av, bd, bd, cd, cd, bdT, bdT, cdT, cdT, dv],
        out_specs=[ul_s, uc_s, av, av, dbd, dbd, dcd, dcd, dv],
        out_shape=[jax.ShapeDtypeStruct((L, W), F32), jax.ShapeDtypeStruct((C, W), F32),
                   jax.ShapeDtypeStruct((2, 1, GP), F32), jax.ShapeDtypeStruct((2, 1, GP), F32),
                   jax.ShapeDtypeStruct((2, nsb, LANES, PB), F32), jax.ShapeDtypeStruct((2, nsb, LANES, PB), F32),
                   jax.ShapeDtypeStruct((2, nsb, PB, LANES), F32), jax.ShapeDtypeStruct((2, nsb, PB, LANES), F32),
                   jax.ShapeDtypeStruct((1, W), F32)],
        scratch_shapes=[pltpu.VMEM((L, PB), F32), pltpu.VMEM((L, PB), F32),
                        pltpu.VMEM((C, PB), F32), pltpu.VMEM((C, PB), F32),
                        pltpu.VMEM((L, PB), F32), pltpu.VMEM((L, PB), F32),
                        pltpu.VMEM((C, PB), F32), pltpu.VMEM((C, PB), F32)],
        compiler_params=_cparams(("arbitrary",)),
    )(ul, uc, dyl, dyc, a_re, a_im, bd_re, bd_im, cd_re, cd_im, bdT_re, bdT_im, cdT_re, cdT_im, dskip)


def _glu(y, w, b, name):
    T, W = y.shape
    tm = _tile(T, 544, 16)

    def body(y_ref, w_ref, b_ref, o_ref):
        g = _gelu(y_ref[...])
        o_ref[...] = (g * _sigmoid(_dot(g, w_ref[...], _NN) + b_ref[...])).astype(BF16)

    return pl.pallas_call(
        body, name=name, grid=(T // tm,),
        in_specs=[pl.BlockSpec((tm, W), lambda i: (i, 0)), pl.BlockSpec((W, W), lambda i: (0, 0)),
                  pl.BlockSpec((1, W), lambda i: (0, 0))],
        out_specs=pl.BlockSpec((tm, W), lambda i: (i, 0)),
        out_shape=jax.ShapeDtypeStruct((T, W), BF16),
        compiler_params=_cparams(("parallel",)),
    )(y, w, b)


def _glu_bwd(y, w, b, do, do_col, name):
    T, W = y.shape
    tm = _tile(T, 544, 16)

    def body(y_ref, w_ref, b_ref, do_ref, dy_ref, dw_ref, db_ref):
        i = pl.program_id(0)

        @pl.when(i == 0)
        def _():
            dw_ref[...] = jnp.zeros_like(dw_ref)
            db_ref[...] = jnp.zeros_like(db_ref)

        y = y_ref[...]
        do = do_ref[...]
        g = _gelu(y)
        s = _sigmoid(_dot(g, w_ref[...], _NN) + b_ref[...])
        dpre = do * g * s * (1.0 - s)
        db_ref[...] += jnp.sum(dpre, axis=0, keepdims=True)
        dw_ref[...] += _dot(g, dpre, _TN)
        dg = do * s + _dot(dpre, w_ref[...], _NT)
        dy_ref[...] = dg * _gelu_grad(y)

    row = pl.BlockSpec((tm, W), lambda i: (i, 0))
    wsp = pl.BlockSpec((W, W), lambda i: (0, 0))
    vec = pl.BlockSpec((1, W), lambda i: (0, 0))
    return pl.pallas_call(
        body, name=name, grid=(T // tm,),
        in_specs=[row, wsp, vec, pl.BlockSpec((tm, W), lambda i: (i, do_col))], out_specs=[row, wsp, vec],
        out_shape=[jax.ShapeDtypeStruct((T, W), F32), jax.ShapeDtypeStruct((W, W), F32),
                   jax.ShapeDtypeStruct((1, W), F32)],
        compiler_params=_cparams(("arbitrary",)),
    )(y, w, b, do)


def _loss_head(X, g, target, C, name):
    T, D = X.shape
    tm = _tile(math.gcd(C, T - C), 512, 16)
    nc = C // tm

    def body(x_ref, g_ref, t_ref, loss_ref, dx_ref, dg_ref):
        i = pl.program_id(0)

        @pl.when(i == 0)
        def _():
            loss_ref[...] = jnp.zeros_like(loss_ref)
            dg_ref[...] = jnp.zeros_like(dg_ref)

        @pl.when(i < nc)
        def _():
            dx_ref[...] = jnp.zeros_like(dx_ref)

        @pl.when(i >= nc)
        def _():
            x = x_ref[...]
            gv = g_ref[...]
            r = lax.rsqrt(jnp.mean(x * x, axis=-1, keepdims=True) + NORM_EPS)
            xh = x * r
            err = xh * gv - t_ref[...]
            loss_ref[...] += 0.5 * jnp.sum(jnp.mean(err * err, axis=-1, keepdims=True), axis=0, keepdims=True)
            dy = err * (1.0 / D)
            dg_ref[...] += jnp.sum(dy * xh, axis=0, keepdims=True)
            dxh = dy * gv
            dx_ref[...] = r * (dxh - xh * jnp.mean(dxh * xh, axis=-1, keepdims=True))

    row = pl.BlockSpec((tm, D), lambda i: (i, 0))
    return pl.pallas_call(
        body, name=name, grid=(T // tm,),
        in_specs=[row, pl.BlockSpec((1, D), lambda i: (0, 0)),
                  pl.BlockSpec((tm, D), lambda i: (jnp.maximum(i - nc, 0), 0))],
        out_specs=[pl.BlockSpec((1, 1), lambda i: (0, 0)), row, pl.BlockSpec((1, D), lambda i: (0, 0))],
        out_shape=[jax.ShapeDtypeStruct((1, 1), F32), jax.ShapeDtypeStruct((T, D), F32),
                   jax.ShapeDtypeStruct((1, D), F32)],
        compiler_params=_cparams(("arbitrary",)),
    )(X, g, target)


def _ada_fwd(cond, w, b, name):
    nl, D, Ns = w.shape
    tn = _tile(Ns, 1024, LANES)

    def body(c_ref, w_ref, b_ref, o_ref):
        c = c_ref[...]
        o_ref[0] = _dot(c * _sigmoid(c), w_ref[0], _NN) + b_ref[0]

    return pl.pallas_call(
        body, name=name, grid=(nl, Ns // tn),
        in_specs=[pl.BlockSpec((16, D), lambda l, j: (0, 0)), pl.BlockSpec((1, D, tn), lambda l, j: (l, 0, j)),
                  pl.BlockSpec((1, 1, tn), lambda l, j: (l, 0, j))],
        out_specs=pl.BlockSpec((1, 16, tn), lambda l, j: (l, 0, j)),
        out_shape=jax.ShapeDtypeStruct((nl, 16, Ns), F32),
        compiler_params=_cparams(("parallel", "parallel")),
    )(cond, w, b)


def _ada_bwd(cond, w, dmod, name):
    nl, D, Ns = w.shape
    tn = _tile(Ns, 1024, LANES)

    def body(c_ref, w_ref, dm_ref, dw_ref, da_ref):
        j = pl.program_id(1)

        @pl.when(j == 0)
        def _():
            da_ref[...] = jnp.zeros_like(da_ref)

        c = c_ref[...]
        dw_ref[0] = _dot(c * _sigmoid(c), dm_ref[0], _TN)
        da_ref[0] += _dot(dm_ref[0], w_ref[0], _NT)

    return pl.pallas_call(
        body, name=name, grid=(nl, Ns // tn),
        in_specs=[pl.BlockSpec((16, D), lambda l, j: (0, 0)), pl.BlockSpec((1, D, tn), lambda l, j: (l, 0, j)),
                  pl.BlockSpec((1, 16, tn), lambda l, j: (l, 0, j))],
        out_specs=[pl.BlockSpec((1, D, tn), lambda l, j: (l, 0, j)), pl.BlockSpec((1, 16, D), lambda l, j: (l, 0, 0))],
        out_shape=[jax.ShapeDtypeStruct((nl, D, Ns), F32), jax.ShapeDtypeStruct((nl, 16, D), F32)],
        compiler_params=_cparams(("parallel", "arbitrary")),
    )(cond, w, dmod)


def _cctx_grad(c_ctx, dact4, name):
    nch, nl, _, D = dact4.shape

    def body(c_ref, da_ref, o_ref):
        acc = jnp.zeros((1, D), F32)
        for ch in range(nch):
            for l in range(nl):
                acc = acc + da_ref[ch, l, 8:9, :]
        c = c_ref[...]
        s = _sigmoid(c)
        o_ref[...] = acc * (s + c * s * (1.0 - s))

    return pl.pallas_call(body, name=name, out_shape=jax.ShapeDtypeStruct((1, D), F32),
                          compiler_params=_cparams())(c_ctx, dact4)


def _adamw(w, g, m, v, name):
    R, W = w.shape
    tr = _tile(R, max(8, (1 << 19) // W // 8 * 8), 8)
    c1 = 1.0 - ADAM_B1 ** ADAM_STEP
    c2 = 1.0 - ADAM_B2 ** ADAM_STEP

    def body(w_ref, g_ref, m_ref, v_ref, d_ref, nm_ref, nv_ref):
        g = g_ref[...]
        m = ADAM_B1 * m_ref[...] + (1.0 - ADAM_B1) * g
        v = ADAM_B2 * v_ref[...] + (1.0 - ADAM_B2) * (g * g)
        nm_ref[...] = m
        nv_ref[...] = v
        d_ref[...] = -ADAM_LR * ((m / c1) / (jnp.sqrt(v / c2) + ADAM_EPS) + ADAM_WD * w_ref[...])

    blk = pl.BlockSpec((tr, W), lambda i: (i, 0))
    s = jax.ShapeDtypeStruct((R, W), F32)
    return pl.pallas_call(
        body, name=name, grid=(R // tr,), in_specs=[blk] * 4, out_specs=[blk] * 3, out_shape=[s, s, s],
        compiler_params=_cparams(("parallel",)),
    )(w, g, m, v)


def _adamw_nd(w, g, m, v, name):
    shp = w.shape
    two = (math.prod(shp[:-1]), shp[-1])
    d, nm, nv = _adamw(w.reshape(two), g.reshape(two), m.reshape(two), v.reshape(two), name)
    return d.reshape(shp), nm.reshape(shp), nv.reshape(shp)


def _interleave(a, n):
    return a.reshape(SCAN_SEGMENTS, n // SCAN_SEGMENTS, -1).transpose(1, 0, 2).reshape(n, -1)


def _deinterleave(a, n):
    return a.reshape(n // SCAN_SEGMENTS, SCAN_SEGMENTS, -1).transpose(1, 0, 2).reshape(n, -1)


def _blockdiag_in(bbT, dm):
    eye = jnp.eye(dm.SG, dtype=F32)
    b4 = bbT.reshape(2, SSM_GROUP, dm.SG, SSM_STATE)
    return jnp.einsum("dhgp,gk->dghkp", b4, eye).reshape(2, dm.SSM_W, dm.GP)


def _blockdiag_in_T(dbd, dm):
    gpb = STATE_BLOCK // SSM_STATE
    per = (LANES // SSM_GROUP) // gpb
    d8 = dbd.reshape(2, dm.SSM_W // LANES, per, per, gpb, SSM_GROUP, gpb, SSM_STATE)
    x = jnp.einsum("duaabhbp->duabhp", d8).reshape(2, dm.SG, SSM_GROUP, SSM_STATE)
    return x.transpose(0, 2, 1, 3).reshape(2, SSM_GROUP, dm.GP)


def _blockdiag_out(c, dm):
    eye = jnp.eye(dm.SG, dtype=F32)
    return jnp.einsum("dghp,gk->dgpkh", c, eye).reshape(2, dm.GP, dm.SSM_W)


def _blockdiag_out_T(dcd, dm):
    gpb = STATE_BLOCK // SSM_STATE
    per = (LANES // SSM_GROUP) // gpb
    d8 = dcd.reshape(2, dm.SSM_W // LANES, per, gpb, SSM_STATE, per, gpb, SSM_GROUP)
    return jnp.einsum("duabpabh->duabhp", d8).reshape(2, dm.SG, SSM_GROUP, SSM_STATE)


_BIG = ("w_in", "w_out", "ssm_w_glu", "ffn_w_up", "ffn_w_down")
_SHARD_AXIS = {"w_in": 2, "w_out": 1, "ssm_w_glu": 1, "ffn_w_up": 2, "ffn_w_down": 1}
_SMALL = ("g_mix", "g_ffn", "attn_sink", "ssm_lambda_re", "ssm_lambda_im", "ssm_log_dt", "ssm_b_re", "ssm_b_im",
          "ssm_c_re", "ssm_c_im", "ssm_d", "ssm_b_glu", "gmlp_ln_g", "gmlp_ln_b", "gmlp_w_s", "gmlp_b_s",
          "ffn_conv_b", "g_final", "ffn_conv_w")
_WEIGHTS = ("c_ctx", "w_ada", "b_ada", "g_mix", "g_ffn", "w_in", "w_out", "attn_sink", "ssm_lambda_re",
            "ssm_lambda_im", "ssm_log_dt", "ssm_b_re", "ssm_b_im", "ssm_c_re", "ssm_c_im", "ssm_d", "ssm_w_glu",
            "ssm_b_glu", "gmlp_ln_g", "gmlp_ln_b", "gmlp_w_s", "gmlp_b_s", "ffn_w_up", "ffn_conv_w", "ffn_conv_b",
            "ffn_w_down", "g_final")


def _step(P, M, V, x, c, ctx, loss_target):
    dm = _Dims()
    D, T, C, L = dm.D, dm.T, dm.C, dm.L
    mx, my, mc = lax.axis_index("x"), lax.axis_index("y"), lax.axis_index("c")
    chip = 2 * mx + my
    dev = 2 * chip + mc

    conv_sh = P["ffn_conv_w"].shape
    small_f = _pack([c, P["ffn_conv_w"]], F32, 8)
    small_all = _allgather8(small_f, "ag_small_fwd")
    c_all, conv_all = [], []
    for d in range(N_DEV):
        cd_, cw_ = _unpack(small_all[d], [(1, D), conv_sh])
        c_all.append(cd_)
        conv_all.append(cw_)
    conv_w = jnp.concatenate([conv_all[2 * j] for j in range(N_CHIPS)], axis=2)
    cond = jnp.concatenate(c_all + [P["c_ctx"].reshape(1, D), jnp.zeros((16 - N_DEV - 1, D), F32)], axis=0)

    n_sh = P["w_ada"].shape[2]
    b_sh = lax.dynamic_slice_in_dim(P["b_ada"], chip * n_sh, n_sh, axis=1)[:, None, :]
    mods_sh = _ada_fwd(cond, P["w_ada"], b_sh, "ada_fwd")
    mods4 = _comm(mods_sh, group="chips", by_peer=False, name="ag_mods")
    mods = jnp.concatenate([mods4[j] for j in range(N_CHIPS)], axis=-1)
    mod_lat = lax.dynamic_index_in_dim(mods, dev, axis=1, keepdims=False)
    mod_ctx = mods[:, N_DEV]
    modp = jnp.stack([mod_ctx, mod_lat], axis=1).reshape(DEPTH, 2, N_MOD, D)

    LH = DEPTH // 2
    half_shapes = [(LH,) + P[n].shape[1:] for n in _BIG]
    w_half = _pack([lax.dynamic_slice_in_dim(P[n], mc * LH, LH, axis=0) for n in _BIG], BF16, 16)
    RH = w_half.shape[0]
    w4 = _comm(w_half, group="chips", by_peer=False, name="ag_w_chips")
    w24 = _comm(w4, group="sib", by_peer=False, name="ag_w_sib")
    pieces = [[_unpack(w24[hh, j], half_shapes) for j in range(N_CHIPS)] for hh in range(2)]

    def full_weight(n, l, chips=range(N_CHIPS)):
        t = _BIG.index(n)
        return jnp.concatenate([pieces[l // LH][j][t][l % LH] for j in chips], axis=_SHARD_AXIS[n] - 1)

    cos, sin = _rope_tables(dm)
    Wl = [dict(w_in=full_weight("w_in", l), w_out=full_weight("w_out", l), w_glu=full_weight("ssm_w_glu", l),
               w_g=full_weight("ffn_w_up", l, (0, 1)), w_v=full_weight("ffn_w_up", l, (2, 3)),
               w_dn=full_weight("ffn_w_down", l)) for l in range(DEPTH)]

    X = jnp.concatenate([ctx.reshape(C, D), x.reshape(L, D)], axis=0)
    saved = []
    for l in range(DEPTH):
        mp = modp[l]
        sh_a, sc_a, gt_a, sh_f, sc_f, gt_f = [mp[:, k] for k in range(N_MOD)]
        w_in, w_out, w_glu = Wl[l]["w_in"], Wl[l]["w_out"], Wl[l]["w_glu"]
        w_g, w_v, w_dn = Wl[l]["w_g"], Wl[l]["w_v"], Wl[l]["w_dn"]
        g_mix, g_ffn = P["g_mix"][l][None], P["g_ffn"][l][None]

        h = _normmod(X, g_mix, sh_a, sc_a, C, "normmod_a")
        z = _matmul(h, w_in, mode="nn", name="mm_in")
        q, k2, v2 = _attn_prep(z, cos, sin, dm, "attn_prep")
        sinkcol = jnp.repeat(P["attn_sink"][l].reshape(2, GQA_RATIO), ATTN_BLOCK, axis=1)[..., None]
        o_attn = _attention(q, k2, v2, sinkcol, dm, "attention")
        u = z[:, dm.OFF_S:dm.OFF_GU]
        ul, uc = _interleave(u[C:], L), _interleave(u[:C], C)
        lam_re = P["ssm_lambda_re"][l].reshape(2, 1, dm.GP)
        lam_im = P["ssm_lambda_im"][l].reshape(2, 1, dm.GP)
        ldt = jnp.repeat(P["ssm_log_dt"][l], SSM_STATE, axis=-1).reshape(2, 1, dm.GP)
        bT_re = P["ssm_b_re"][l].reshape(2, dm.GP, SSM_GROUP).transpose(0, 2, 1)
        bT_im = P["ssm_b_im"][l].reshape(2, dm.GP, SSM_GROUP).transpose(0, 2, 1)
        a_re, a_im, bbT_re, bbT_im = _s5_disc(lam_re, lam_im, ldt, bT_re, bT_im, "s5_disc")
        bd_re, bd_im = _blockdiag_in(bbT_re, dm).astype(BF16), _blockdiag_in(bbT_im, dm).astype(BF16)
        cd_re = _blockdiag_out(P["ssm_c_re"][l], dm).astype(BF16)
        cd_im = _blockdiag_out(P["ssm_c_im"][l], dm).astype(BF16)
        dskip = P["ssm_d"][l][None]
        yl, yc = _s5_scan(ul, uc, a_re, a_im, bd_re, bd_im, cd_re, cd_im, dskip, dm, "s5_scan")
        y = jnp.concatenate([_deinterleave(yc, C), _deinterleave(yl, L)], axis=0)
        b_glu = P["ssm_b_glu"][l][None]
        o_ssm = _glu(y, w_glu, b_glu, "glu")
        ln_g, ln_b = P["gmlp_ln_g"][l][None], P["gmlp_ln_b"][l][None]
        w_s = P["gmlp_w_s"][l].astype(BF16)
        b_s = P["gmlp_b_s"][l][..., None]
        o_gmlp = _gmlp(z, ln_g, ln_b, w_s, b_s, dm, "gmlp")
        o_cat = jnp.concatenate([o_attn, o_ssm, o_gmlp], axis=-1)
        mix, X1 = _matmul_residual(o_cat, w_out, X, gt_a, C, "mm_out")
        h2 = _normmod(X1, g_ffn, sh_f, sc_f, C, "normmod_f")
        upg = _matmul(h2, w_g, mode="nn", name="mm_up")
        upv = _matmul(h2, w_v, mode="nn", name="mm_up")
        cw, cb = conv_w[l], P["ffn_conv_b"][l][None]
        act = _convact(upg, upv, cw, cb, C, "convact")
        ffn, X2 = _matmul_residual(act, w_dn, X1, gt_f, C, "mm_down")
        saved.append(dict(X=X, h=h, z=z, q=q, k2=k2, v2=v2, sinkcol=sinkcol, ul=ul, uc=uc, a_re=a_re, a_im=a_im,
                          bd_re=bd_re, bd_im=bd_im, cd_re=cd_re, cd_im=cd_im, y=y, o_cat=o_cat, mix=mix, X1=X1, h2=h2,
                          upg=upg, upv=upv, act=act, ffn=ffn, lam_re=lam_re, lam_im=lam_im, ldt=ldt, bT_re=bT_re,
                          bT_im=bT_im))
        X = X2

    loss_l, dX, dg_final = _loss_head(X, P["g_final"][None], loss_target.reshape(L, D), C, "loss_head")
    loss = lax.psum(loss_l[0, 0], ("x", "y", "c"))

    G = {n: [None] * DEPTH for n in _WEIGHTS if n not in ("c_ctx", "w_ada", "b_ada", "g_final")}
    dmod = [None] * DEPTH
    for l in reversed(range(DEPTH)):
        S = saved[l]
        mp = modp[l]
        sh_a, sc_a, gt_a, sh_f, sc_f, gt_f = [mp[:, k] for k in range(N_MOD)]
        w_in, w_out, w_glu = Wl[l]["w_in"], Wl[l]["w_out"], Wl[l]["w_glu"]
        w_g, w_v, w_dn = Wl[l]["w_g"], Wl[l]["w_v"], Wl[l]["w_dn"]
        g_mix, g_ffn = P["g_mix"][l][None], P["g_ffn"][l][None]
        cw, cb = conv_w[l], P["ffn_conv_b"][l][None]

        dffn, dgt_f = _gate_bwd(dX, S["ffn"], gt_f, C, "gate_bwd")
        dact = _matmul(dffn, w_dn, mode="nt", name="mm_down_dx")
        G["ffn_w_down"][l] = _matmul(S["act"], dffn, mode="tn", name="mm_down_dw")
        dupg, dupv, dcw, dcb = _convact_bwd(S["upg"], S["upv"], cw, cb, dact, C, "convact_bwd")
        G["ffn_conv_w"][l], G["ffn_conv_b"][l] = dcw, dcb[0]
        dh2 = _matmul(dupg, w_g, mode="nt", name="mm_up_dx")
        dh2 = _matmul(dupv, w_v, mode="nt", name="mm_up_dx2", add=dh2)
        G["ffn_w_up"][l] = (_matmul(S["h2"], dupg, mode="tn", name="mm_up_dw"),
                            _matmul(S["h2"], dupv, mode="tn", name="mm_up_dw"))
        dX1, dg_f, dsh_f, dsc_f = _normmod_bwd(S["X1"], g_ffn, sc_f, dh2, dX, C, "normmod_bwd")
        G["g_ffn"][l] = dg_f[0]
        dmix, dgt_a = _gate_bwd(dX1, S["mix"], gt_a, C, "gate_bwd")
        do = _matmul(dmix, w_out, mode="nt", name="mm_out_dx")
        G["w_out"][l] = _matmul(S["o_cat"], dmix, mode="tn", name="mm_out_dw")
        do_attn = do_ssm = do_gmlp = do
        ln_g, ln_b = P["gmlp_ln_g"][l][None], P["gmlp_ln_b"][l][None]
        w_s = P["gmlp_w_s"][l].astype(BF16)
        b_s = P["gmlp_b_s"][l][..., None]
        dz_g, dlg, dlb, dws, dbs = _gmlp_bwd(S["z"], ln_g, ln_b, w_s, b_s, do_gmlp, dm, "gmlp_bwd")
        G["gmlp_ln_g"][l], G["gmlp_ln_b"][l], G["gmlp_w_s"][l], G["gmlp_b_s"][l] = dlg[0], dlb[0], dws, dbs[..., 0]
        b_glu = P["ssm_b_glu"][l][None]
        dy, dwglu, dbglu = _glu_bwd(S["y"], w_glu, b_glu, do_ssm, dm.ATTN_W // dm.SSM_W, "glu_bwd")
        G["ssm_w_glu"][l], G["ssm_b_glu"][l] = dwglu, dbglu[0]
        dyl, dyc = _interleave(dy[C:], L), _interleave(dy[:C], C)
        tr = lambda a: jnp.swapaxes(a, 1, 2)
        dskip = P["ssm_d"][l][None]
        (dul, duc, da_re, da_im, dbd_re, dbd_im, dcd_re, dcd_im, dd) = _s5_scan_bwd(
            S["ul"], S["uc"], dyl, dyc, S["a_re"], S["a_im"], S["bd_re"], S["bd_im"], S["cd_re"], S["cd_im"],
            tr(S["bd_re"]), tr(S["bd_im"]), tr(S["cd_re"]), tr(S["cd_im"]), dskip, dm, "s5_scan_bwd")
        dz_s = jnp.concatenate([_deinterleave(duc, C), _deinterleave(dul, L)], axis=0)
        dlr, dli, dldt, dbTr, dbTi = _s5_disc_bwd(S["lam_re"], S["lam_im"], S["ldt"], S["bT_re"], S["bT_im"],
                                                  da_re, da_im, _blockdiag_in_T(dbd_re, dm),
                                                  _blockdiag_in_T(dbd_im, dm), "s5_disc_bwd")
        G["ssm_lambda_re"][l] = dlr.reshape(2, dm.SG, SSM_STATE)
        G["ssm_lambda_im"][l] = dli.reshape(2, dm.SG, SSM_STATE)
        G["ssm_log_dt"][l] = jnp.sum(dldt.reshape(2, dm.SG, SSM_STATE), axis=-1)
        G["ssm_b_re"][l] = dbTr.transpose(0, 2, 1).reshape(2, dm.SG, SSM_STATE, SSM_GROUP)
        G["ssm_b_im"][l] = dbTi.transpose(0, 2, 1).reshape(2, dm.SG, SSM_STATE, SSM_GROUP)
        G["ssm_c_re"][l] = _blockdiag_out_T(dcd_re, dm)
        G["ssm_c_im"][l] = _blockdiag_out_T(dcd_im, dm)
        G["ssm_d"][l] = dd[0]
        dq, pk, pv, dkc, dvc, dsk = _attention_bwd(S["q"], S["k2"], S["v2"], S["sinkcol"], do_attn, dm, "attention_bwd")
        G["attn_sink"][l] = dsk[:, :, 0].reshape(dm.NH)
        dk2 = _kv_reduce(pk, dkc, dm, "kv_reduce")
        dv2 = _kv_reduce(pv, dvc, dm, "kv_reduce")
        dz_a = _attn_prep_bwd(dq, dk2, dv2, cos, sin, dm, "attn_prep_bwd")
        dz = jnp.concatenate([dz_a, dz_s, dz_g], axis=-1).astype(BF16)
        dh = _matmul(dz, w_in, mode="nt", name="mm_in_dx")
        G["w_in"][l] = _matmul(S["h"], dz, mode="tn", name="mm_in_dw")
        dX, dg_a, dsh_a, dsc_a = _normmod_bwd(S["X"], g_mix, sc_a, dh, dX1, C, "normmod_bwd")
        G["g_mix"][l] = dg_a[0]
        dmod[l] = jnp.stack([dsh_a, dsc_a, dgt_a, dsh_f, dsc_f, dgt_f], axis=1)

    grad_x = dX[C:].reshape(1, L, D)
    Gbig = {n: G.pop(n) for n in _BIG}
    G = {n: jnp.stack(v) for n, v in G.items()}
    G["g_final"] = dg_final[0]
    dmod = jnp.stack(dmod)

    small_shapes = [G[n].shape for n in _SMALL]
    n_small = sum(math.prod(s) for s in small_shapes)
    r_small = -(-n_small // (8 * PACK_W)) * 8
    gs_pack = _pack([G[n] for n in _SMALL], F32, 8)
    dm_pack = _pack([dmod], F32, 8)
    r_dm = dm_pack.shape[0]
    allp = _allgather8(jnp.concatenate([gs_pack, dm_pack], axis=0), "ag_small_bwd")
    gs_sum = _sum_slots(allp[:, :r_small], F32, "sum_small")
    for n, a in zip(_SMALL, _unpack(gs_sum, small_shapes)):
        G[n] = a
    dmod_all = allp[:, r_small:r_small + r_dm].reshape(N_DEV, -1)[:, :DEPTH * 2 * N_MOD * D]
    dmod_all = dmod_all.reshape(N_DEV, DEPTH, 2, N_MOD * D)
    dmod_ctx = _sum_slots(dmod_all[:, :, 0].reshape(N_DEV, DEPTH, N_MOD * D), F32, "sum_dmod_ctx")
    dmod16 = jnp.concatenate([jnp.swapaxes(dmod_all[:, :, 1], 0, 1), dmod_ctx[:, None],
                              jnp.zeros((DEPTH, 16 - N_DEV - 1, N_MOD * D), F32)], axis=1)
    G["b_ada"] = _sum_slots(jnp.swapaxes(dmod16, 0, 1)[:N_DEV + 1], F32, "sum_b_ada")
    dmod16_sh = lax.dynamic_slice_in_dim(dmod16, chip * n_sh, n_sh, axis=2)
    G["w_ada"], dact = _ada_bwd(cond, P["w_ada"], dmod16_sh, "ada_bwd")
    dact4 = _comm(dact, group="chips", by_peer=False, name="ag_dact")
    G["c_ctx"] = _cctx_grad(P["c_ctx"].reshape(1, D), dact4, "cctx_grad")[0]
    conv_cols = conv_sh[2]
    G["ffn_conv_w"] = lax.dynamic_slice_in_dim(G["ffn_conv_w"], chip * conv_cols, conv_cols, axis=2)

    def shard_of(n, l, j):
        a = Gbig[n][l]
        if n == "ffn_w_up":
            a, j = a[j // 2], j % 2
            k = a.shape[1] // 2
            return lax.slice_in_dim(a, j * k, (j + 1) * k, axis=1)
        ax = _SHARD_AXIS[n] - 1
        k = a.shape[ax] // N_CHIPS
        return lax.slice_in_dim(a, j * k, (j + 1) * k, axis=ax)

    gp = jnp.stack([jnp.stack([_pack([shard_of(n, hh * LH + li, j) for n in _BIG for li in range(LH)], BF16, 16)
                               for j in range(N_CHIPS)]) for hh in range(2)])
    pair = _comm(gp.reshape(2, N_CHIPS * RH, PACK_W), group="sib", by_peer=True, name="rs_pair")
    psum = _sum_slots(pair, BF16, "sum_pair").reshape(N_CHIPS, RH, PACK_W)
    quad = _comm(psum, group="chips", by_peer=True, name="rs_chips")
    red = _sum_slots(quad, F32, "sum_chips")
    both = _comm(red, group="sib", by_peer=False, name="rs_share")
    halves = [_unpack(both[hh], half_shapes) for hh in range(2)]
    for t, n in enumerate(_BIG):
        G[n] = jnp.concatenate([halves[0][t], halves[1][t]], axis=0)

    delta, new_m, new_v = {}, {}, {}
    for n in _BIG + ("w_ada",):
        delta[n], new_m[n], new_v[n] = _adamw_nd(P[n], G[n], M[n], V[n], "adamw_" + n)
    rest = [n for n in _WEIGHTS if n not in _BIG + ("w_ada",)]
    shapes = [P[n].shape for n in rest]
    packed = [_pack([d[n] for n in rest], F32, 8) for d in (P, G, M, V)]
    outs = _adamw(*packed, "adamw_small")
    for dst, buf in zip((delta, new_m, new_v), outs):
        for n, a in zip(rest, _unpack(buf, shapes)):
            dst[n] = a
    return loss, grad_x, G, delta, new_m, new_v


def kernel(x, c, ctx, c_ctx, w_ada, b_ada, g_mix, g_ffn, w_in, w_out, attn_sink, ssm_lambda_re, ssm_lambda_im, ssm_log_dt, ssm_b_re, ssm_b_im, ssm_c_re, ssm_c_im, ssm_d, ssm_w_glu, ssm_b_glu, gmlp_ln_g, gmlp_ln_b, gmlp_w_s, gmlp_b_s, ffn_w_up, ffn_conv_w, ffn_conv_b, ffn_w_down, g_final, loss_target, m_c_ctx, m_w_ada, m_b_ada, m_g_mix, m_g_ffn, m_w_in, m_w_out, m_attn_sink, m_ssm_lambda_re, m_ssm_lambda_im, m_ssm_log_dt, m_ssm_b_re, m_ssm_b_im, m_ssm_c_re, m_ssm_c_im, m_ssm_d, m_ssm_w_glu, m_ssm_b_glu, m_gmlp_ln_g, m_gmlp_ln_b, m_gmlp_w_s, m_gmlp_b_s, m_ffn_w_up, m_ffn_conv_w, m_ffn_conv_b, m_ffn_w_down, m_g_final, v_c_ctx, v_w_ada, v_b_ada, v_g_mix, v_g_ffn, v_w_in, v_w_out, v_attn_sink, v_ssm_lambda_re, v_ssm_lambda_im, v_ssm_log_dt, v_ssm_b_re, v_ssm_b_im, v_ssm_c_re, v_ssm_c_im, v_ssm_d, v_ssm_w_glu, v_ssm_b_glu, v_gmlp_ln_g, v_gmlp_ln_b, v_gmlp_w_s, v_gmlp_b_s, v_ffn_w_up, v_ffn_conv_w, v_ffn_conv_b, v_ffn_w_down, v_g_final):
    env = locals()
    P = {n: env[n] for n in _WEIGHTS}
    M = {n: env["m_" + n] for n in _WEIGHTS}
    V = {n: env["v_" + n] for n in _WEIGHTS}
    loss, grad_x, G, delta, new_m, new_v = _step(P, M, V, x, c, ctx, loss_target)
    return (loss, grad_x, *[G[n] for n in _WEIGHTS], *[delta[n] for n in _WEIGHTS],
            *[new_m[n] for n in _WEIGHTS], *[new_v[n] for n in _WEIGHTS])
```

```python
import functools
import math

import jax
import jax.numpy as jnp
from jax import lax
from jax.experimental import pallas as pl
from jax.experimental.pallas import tpu as pltpu

F32 = jnp.float32
BF16 = jnp.bfloat16

D_MODEL = 2048
SEQ = 4096
DEPTH = 4
CTX_LEN = 256
GRID_W = 64
HEAD_DIM = 64
GQA_RATIO = 8
WINDOW = 128
ATTN_BLOCK = 128
ROPE_BASE = 10000.0
SSM_GROUP = 16
SSM_STATE = 64
GMLP_CHUNK = 128
GMLP_GROUP_W = 128
N_MOD = 6
NORM_EPS = 1e-6
ADAM_LR = 0.001
ADAM_B1 = 0.9
ADAM_B2 = 0.999
ADAM_EPS = 1e-08
ADAM_WD = 0.01
ADAM_STEP = 10

N_CHIPS = 4
N_DEV = 8
SCAN_SEGMENTS = 8
STATE_BLOCK = 256
LANES = 128
PACK_W = 1024
COMM_CHUNKS = (16, 8, 4, 2)
STAGE_BYTES = 4 << 20
VMEM_LIMIT_MB = 56
_GELU_K = math.sqrt(2.0 / math.pi)
_GELU_C = 0.044715


class _Dims:
    def __init__(self):
        self.D = D_MODEL
        self.L = SEQ
        self.C = CTX_LEN
        self.T = SEQ + CTX_LEN
        self.ATTN_W = D_MODEL // 2
        self.SSM_W = D_MODEL // 4
        self.GMLP_W = D_MODEL - self.ATTN_W - self.SSM_W
        self.NH = self.ATTN_W // HEAD_DIM
        self.NKV = self.NH // GQA_RATIO
        self.KV_W = self.NKV * HEAD_DIM
        self.SG = self.SSM_W // SSM_GROUP
        self.GP = self.SG * SSM_STATE
        self.GG = self.GMLP_W // GMLP_GROUP_W
        self.DFF = ((8 * D_MODEL // 3 + 255) // 256) * 256
        self.OFF_K = self.ATTN_W
        self.OFF_V = self.OFF_K + self.KV_W
        self.OFF_S = self.OFF_V + self.KV_W
        self.OFF_GU = self.OFF_S + self.SSM_W
        self.OFF_GV = self.OFF_GU + self.GMLP_W
        self.N_IN = self.OFF_GV + self.GMLP_W
        assert self.NKV == 2 and self.KV_W == LANES
        assert self.C % (SCAN_SEGMENTS * 8) == 0 and self.L % (SCAN_SEGMENTS * 8) == 0
        assert self.C % ATTN_BLOCK == 0 and self.L % ATTN_BLOCK == 0


def _tile(n, cap, mult):
    best = None
    for t in range(mult, min(n, cap) + 1, mult):
        if n % t == 0:
            best = t
    return best if best is not None else n


def _cparams(sem=None):
    kw = dict(vmem_limit_bytes=VMEM_LIMIT_MB << 20)
    if sem is not None:
        kw["dimension_semantics"] = sem
    return pltpu.CompilerParams(**kw)


def _gelu(x):
    return 0.5 * x * (1.0 + jnp.tanh(_GELU_K * (x + _GELU_C * x * x * x)))


def _gelu_grad(x):
    t = jnp.tanh(_GELU_K * (x + _GELU_C * x * x * x))
    return 0.5 * (1.0 + t) + 0.5 * x * (1.0 - t * t) * _GELU_K * (1.0 + 3.0 * _GELU_C * x * x)


def _sigmoid(x):
    return 1.0 / (1.0 + jnp.exp(-x))


def _dot(a, b, dims):
    return lax.dot_general(a.astype(BF16), b.astype(BF16), (dims, ((), ())), preferred_element_type=F32)


_NN = ((1,), (0,))
_NT = ((1,), (1,))
_TN = ((0,), (0,))


def _sib_staged(x, by_peer, name):
    rows, W = x.shape[-2:]
    cr = _tile(rows, max(16, STAGE_BYTES // (W * x.dtype.itemsize) // 16 * 16), 16)
    nch = rows // cr

    def body(x_ref, o_ref, sbuf, rbuf, send_sems, recv_sems, ld_sem, st_sem, own_sem, credit_sem):
        mx, my, mc = lax.axis_index("x"), lax.axis_index("y"), lax.axis_index("c")
        sib = (mx, my, 1 - mc)
        mine = x_ref.at[mc] if by_peer else x_ref
        theirs = x_ref.at[1 - mc] if by_peer else x_ref
        own = pltpu.make_async_copy(mine, o_ref.at[mc], own_sem)
        own.start()

        def step(i, carry):
            s = i % 2
            r0 = pl.multiple_of(i * cr, 16)
            ld = pltpu.make_async_copy(theirs.at[pl.ds(r0, cr), :], sbuf.at[s], ld_sem)
            ld.start()
            ld.wait()

            @pl.when(i >= 2)
            def _():
                pl.semaphore_wait(credit_sem, 1)

            cp = pltpu.make_async_remote_copy(src_ref=sbuf.at[s], dst_ref=rbuf.at[s], send_sem=send_sems.at[s],
                                              recv_sem=recv_sems.at[s], device_id=sib,
                                              device_id_type=pl.DeviceIdType.MESH)
            cp.start()
            cp.wait_recv()
            st = pltpu.make_async_copy(rbuf.at[s], o_ref.at[1 - mc, pl.ds(r0, cr), :], st_sem)
            st.start()
            st.wait()

            @pl.when(i < nch - 2)
            def _():
                pl.semaphore_signal(credit_sem, inc=1, device_id=sib, device_id_type=pl.DeviceIdType.MESH)

            cp.wait_send()
            return carry

        lax.fori_loop(0, nch, step, 0)
        own.wait()

    return pl.pallas_call(
        body, name=name,
        out_shape=jax.ShapeDtypeStruct((2, rows, W), x.dtype),
        in_specs=[pl.BlockSpec(memory_space=pl.ANY)],
        out_specs=pl.BlockSpec(memory_space=pl.ANY),
        scratch_shapes=[pltpu.VMEM((2, cr, W), x.dtype), pltpu.VMEM((2, cr, W), x.dtype),
                        pltpu.SemaphoreType.DMA((2,)), pltpu.SemaphoreType.DMA((2,)),
                        pltpu.SemaphoreType.DMA, pltpu.SemaphoreType.DMA, pltpu.SemaphoreType.DMA,
                        pltpu.SemaphoreType.REGULAR],
        compiler_params=_cparams(),
    )(x)


def _comm(x, *, group, by_peer, name):
    n = 2 if group == "sib" else N_CHIPS
    blk = x.shape[1:] if by_peer else x.shape
    npeer = n - 1
    W = blk[-1]
    rows = math.prod(blk[:-1])
    x2 = x.reshape((n, rows, W) if by_peer else (rows, W))
    if group == "sib":
        return _sib_staged(x2, by_peer, name).reshape((n,) + tuple(blk))
    nchunk = next((k for k in COMM_CHUNKS if rows % (k * 16) == 0), 1)
    cr = rows // nchunk

    def body(x_ref, o_ref, send_sems, recv_sems, local_sem):
        mx, my, mc = lax.axis_index("x"), lax.axis_index("y"), lax.axis_index("c")
        if group == "sib":
            me = mc
            peers = [((mx, my, 1 - mc), 1 - mc)]
        else:
            me = 2 * mx + my
            peers = [((1 - mx, my, mc), 2 * (1 - mx) + my), ((mx, 1 - my, mc), 2 * mx + (1 - my)),
                     ((1 - mx, 1 - my, mc), 2 * (1 - mx) + (1 - my))]

        def src_of(pid):
            return x_ref.at[pid] if by_peer else x_ref

        def chunk(ref, ch):
            return ref.at[pl.ds(ch * cr, cr), :]

        def remote(k, dev, pid, ch=None):
            s, d = src_of(pid), o_ref.at[me]
            if ch is not None:
                s, d = chunk(s, ch), chunk(d, ch)
            return pltpu.make_async_remote_copy(src_ref=s, dst_ref=d, send_sem=send_sems.at[k],
                                                recv_sem=recv_sems.at[k], device_id=dev,
                                                device_id_type=pl.DeviceIdType.MESH)

        for ch in range(nchunk):
            pltpu.make_async_copy(chunk(src_of(me), ch), chunk(o_ref.at[me], ch), local_sem).start()
            for k, (dev, pid) in enumerate(peers):
                remote(k, dev, pid, ch).start()
        for k, (dev, pid) in enumerate(peers):
            remote(k, dev, pid).wait_recv()
        for k, (dev, pid) in enumerate(peers):
            remote(k, dev, pid).wait_send()
        pltpu.make_async_copy(src_of(me), o_ref.at[me], local_sem).wait()

    out = pl.pallas_call(
        body, name=name,
        out_shape=jax.ShapeDtypeStruct((n, rows, W), x.dtype),
        in_specs=[pl.BlockSpec(memory_space=pl.ANY)],
        out_specs=pl.BlockSpec(memory_space=pl.ANY),
        scratch_shapes=[pltpu.SemaphoreType.DMA((npeer,)), pltpu.SemaphoreType.DMA((npeer,)),
                        pltpu.SemaphoreType.DMA],
    )(x2)
    return out.reshape((n,) + tuple(blk))


def _allgather8(x, name):
    a = _comm(x, group="chips", by_peer=False, name=name + "_chips")
    b = _comm(a, group="sib", by_peer=False, name=name + "_sib")
    return jnp.swapaxes(b, 0, 1).reshape((N_DEV,) + x.shape)


def _sum_slots(x, out_dtype, name):
    n, R, W = x.shape
    tr = _tile(R, 512, 16)

    def body(x_ref, o_ref):
        acc = x_ref[0].astype(F32)
        for s in range(1, n):
            acc = acc + x_ref[s].astype(F32)
        o_ref[...] = acc.astype(o_ref.dtype)

    return pl.pallas_call(
        body, name=name, grid=(R // tr,),
        in_specs=[pl.BlockSpec((n, tr, W), lambda i: (0, i, 0))],
        out_specs=pl.BlockSpec((tr, W), lambda i: (i, 0)),
        out_shape=jax.ShapeDtypeStruct((R, W), out_dtype),
        compiler_params=_cparams(("parallel",)),
    )(x)


def _pack(arrs, dtype, row_mult):
    flat = jnp.concatenate([a.reshape(-1).astype(dtype) for a in arrs])
    n = flat.shape[0]
    quant = row_mult * PACK_W
    total = -(-n // quant) * quant
    if total != n:
        flat = jnp.concatenate([flat, jnp.zeros((total - n,), dtype)])
    return flat.reshape(total // PACK_W, PACK_W)


def _unpack(buf, shapes):
    flat = buf.reshape(-1)
    out, off = [], 0
    for s in shapes:
        n = math.prod(s)
        out.append(flat[off:off + n].reshape(s))
        off += n
    return out


def _matmul(a, b, *, mode, name, out_dtype=F32, add=None, tm_cap=1088, tn_cap=1536, tk_cap=2048):
    if mode == "nn":
        (M, K), (_, N) = a.shape, b.shape
    elif mode == "nt":
        (M, K), (N, _) = a.shape, b.shape
    else:
        (K, M), (_, N) = a.shape, b.shape
    tm = _tile(M, tm_cap, 16 if mode != "tn" else LANES)
    tn = _tile(N, tn_cap, LANES)
    tk = _tile(K, tk_cap, LANES if mode != "tn" else 16)
    nk = K // tk
    dims = {"nn": _NN, "nt": _NT, "tn": _TN}[mode]

    def body(*refs):
        if add is None:
            a_ref, b_ref, o_ref, acc_ref = refs
            add_ref = None
        else:
            a_ref, b_ref, add_ref, o_ref, acc_ref = refs
        k = pl.program_id(2)

        @pl.when(k == 0)
        def _():
            acc_ref[...] = jnp.zeros_like(acc_ref)

        acc_ref[...] += _dot(a_ref[...], b_ref[...], dims)

        @pl.when(k == nk - 1)
        def _():
            r = acc_ref[...]
            if add_ref is not None:
                r = r + add_ref[...].astype(F32)
            o_ref[...] = r.astype(o_ref.dtype)

    if mode == "nn":
        a_spec = pl.BlockSpec((tm, tk), lambda i, j, k: (i, k))
        b_spec = pl.BlockSpec((tk, tn), lambda i, j, k: (k, j))
    elif mode == "nt":
        a_spec = pl.BlockSpec((tm, tk), lambda i, j, k: (i, k))
        b_spec = pl.BlockSpec((tn, tk), lambda i, j, k: (j, k))
    else:
        a_spec = pl.BlockSpec((tk, tm), lambda i, j, k: (k, i))
        b_spec = pl.BlockSpec((tk, tn), lambda i, j, k: (k, j))
    o_spec = pl.BlockSpec((tm, tn), lambda i, j, k: (i, j))
    in_specs = [a_spec, b_spec] + ([o_spec] if add is not None else [])
    args = (a, b) + ((add,) if add is not None else ())
    return pl.pallas_call(
        body, name=name, grid=(M // tm, N // tn, nk),
        in_specs=in_specs, out_specs=o_spec,
        out_shape=jax.ShapeDtypeStruct((M, N), out_dtype),
        scratch_shapes=[pltpu.VMEM((tm, tn), F32)],
        compiler_params=_cparams(("parallel", "parallel", "arbitrary")),
    )(*args)


def _matmul_residual(a, w, res, gate, C, name):
    M, K = a.shape
    N = w.shape[1]
    tm = _tile(M, 1088, 16)
    tn = _tile(N, 1024, LANES)
    tk = _tile(K, 2048, LANES)
    nk = K // tk

    def body(a_ref, w_ref, res_ref, gate_ref, mix_ref, o_ref, acc_ref):
        i, k = pl.program_id(0), pl.program_id(2)

        @pl.when(k == 0)
        def _():
            acc_ref[...] = jnp.zeros_like(acc_ref)

        acc_ref[...] += _dot(a_ref[...], w_ref[...], _NN)

        @pl.when(k == nk - 1)
        def _():
            r = acc_ref[...]
            row = i * tm + lax.broadcasted_iota(jnp.int32, (tm, 1), 0)
            g = jnp.where(row < C, gate_ref[0:1, :], gate_ref[1:2, :])
            mix_ref[...] = r
            o_ref[...] = res_ref[...] + g * r

    o_spec = pl.BlockSpec((tm, tn), lambda i, j, k: (i, j))
    return pl.pallas_call(
        body, name=name, grid=(M // tm, N // tn, nk),
        in_specs=[pl.BlockSpec((tm, tk), lambda i, j, k: (i, k)),
                  pl.BlockSpec((tk, tn), lambda i, j, k: (k, j)),
                  o_spec,
                  pl.BlockSpec((2, tn), lambda i, j, k: (0, j))],
        out_specs=[o_spec, o_spec],
        out_shape=[jax.ShapeDtypeStruct((M, N), F32), jax.ShapeDtypeStruct((M, N), F32)],
        scratch_shapes=[pltpu.VMEM((tm, tn), F32)],
        compiler_params=_cparams(("parallel", "parallel", "arbitrary")),
    )(a, w, res, gate)


def _seg_select(i, tm, C, ref):
    row = i * tm + lax.broadcasted_iota(jnp.int32, (tm, 1), 0)
    return row < C, jnp.where(row < C, ref[0:1, :], ref[1:2, :])


def _normmod(X, g, shift, scale, C, name):
    T, D = X.shape
    tm = _tile(T, 512, 16)

    def body(x_ref, g_ref, sh_ref, sc_ref, h_ref):
        i = pl.program_id(0)
        x = x_ref[...]
        r = lax.rsqrt(jnp.mean(x * x, axis=-1, keepdims=True) + NORM_EPS)
        _, sh = _seg_select(i, tm, C, sh_ref)
        _, sc = _seg_select(i, tm, C, sc_ref)
        h_ref[...] = ((x * r) * g_ref[...] * (1.0 + sc) + sh).astype(BF16)

    row = pl.BlockSpec((tm, D), lambda i: (i, 0))
    vec = pl.BlockSpec((1, D), lambda i: (0, 0))
    two = pl.BlockSpec((2, D), lambda i: (0, 0))
    return pl.pallas_call(
        body, name=name, grid=(T // tm,),
        in_specs=[row, vec, two, two], out_specs=row,
        out_shape=jax.ShapeDtypeStruct((T, D), BF16),
        compiler_params=_cparams(("parallel",)),
    )(X, g, shift, scale)


def _normmod_bwd(X, g, scale, dh, dres, C, name):
    T, D = X.shape
    tm = _tile(T, 512, 16)

    def body(x_ref, g_ref, sc_ref, dh_ref, dres_ref, dx_ref, dg_ref, dsh_ref, dsc_ref):
        i = pl.program_id(0)

        @pl.when(i == 0)
        def _():
            dg_ref[...] = jnp.zeros_like(dg_ref)
            dsh_ref[...] = jnp.zeros_like(dsh_ref)
            dsc_ref[...] = jnp.zeros_like(dsc_ref)

        x = x_ref[...]
        dh = dh_ref[...]
        gv = g_ref[...]
        r = lax.rsqrt(jnp.mean(x * x, axis=-1, keepdims=True) + NORM_EPS)
        xh = x * r
        isc, sc = _seg_select(i, tm, C, sc_ref)
        n = xh * gv
        dhn = dh * n
        zero = jnp.zeros_like(dh)
        dsh_ref[0:1, :] += jnp.sum(jnp.where(isc, dh, zero), axis=0, keepdims=True)
        dsh_ref[1:2, :] += jnp.sum(jnp.where(isc, zero, dh), axis=0, keepdims=True)
        dsc_ref[0:1, :] += jnp.sum(jnp.where(isc, dhn, zero), axis=0, keepdims=True)
        dsc_ref[1:2, :] += jnp.sum(jnp.where(isc, zero, dhn), axis=0, keepdims=True)
        dn = dh * (1.0 + sc)
        dg_ref[...] += jnp.sum(dn * xh, axis=0, keepdims=True)
        dxh = dn * gv
        dx_ref[...] = dres_ref[...] + r * (dxh - xh * jnp.mean(dxh * xh, axis=-1, keepdims=True))

    row = pl.BlockSpec((tm, D), lambda i: (i, 0))
    vec = pl.BlockSpec((1, D), lambda i: (0, 0))
    two = pl.BlockSpec((2, D), lambda i: (0, 0))
    return pl.pallas_call(
        body, name=name, grid=(T // tm,),
        in_specs=[row, vec, two, row, row], out_specs=[row, vec, two, two],
        out_shape=[jax.ShapeDtypeStruct((T, D), F32), jax.ShapeDtypeStruct((1, D), F32),
                   jax.ShapeDtypeStruct((2, D), F32), jax.ShapeDtypeStruct((2, D), F32)],
        compiler_params=_cparams(("arbitrary",)),
    )(X, g, scale, dh, dres)


def _gate_bwd(dxo, mix, gate, C, name):
    T, D = dxo.shape
    tm = _tile(T, 512, 16)

    def body(dx_ref, mix_ref, gate_ref, dmix_ref, dgate_ref):
        i = pl.program_id(0)

        @pl.when(i == 0)
        def _():
            dgate_ref[...] = jnp.zeros_like(dgate_ref)

        dx = dx_ref[...]
        isc, gt = _seg_select(i, tm, C, gate_ref)
        dmix_ref[...] = (dx * gt).astype(BF16)
        p = dx * mix_ref[...]
        zero = jnp.zeros_like(p)
        dgate_ref[0:1, :] += jnp.sum(jnp.where(isc, p, zero), axis=0, keepdims=True)
        dgate_ref[1:2, :] += jnp.sum(jnp.where(isc, zero, p), axis=0, keepdims=True)

    row = pl.BlockSpec((tm, D), lambda i: (i, 0))
    two = pl.BlockSpec((2, D), lambda i: (0, 0))
    return pl.pallas_call(
        body, name=name, grid=(T // tm,),
        in_specs=[row, row, two], out_specs=[row, two],
        out_shape=[jax.ShapeDtypeStruct((T, D), BF16), jax.ShapeDtypeStruct((2, D), F32)],
        compiler_params=_cparams(("arbitrary",)),
    )(dxo, mix, gate)


def _conv_masks(T, C):
    row = lax.broadcasted_iota(jnp.int32, (T, 1), 0)
    first = (row == 0) | (row == C)
    last = (row == C - 1) | (row == T - 1)
    return first, last


def _shift_rows(x, first, last):
    T = x.shape[0]
    prev = jnp.where(first, 0.0, pltpu.roll(x, 1, 0))
    nxt = jnp.where(last, 0.0, pltpu.roll(x, T - 1, 0))
    return prev, nxt


def _convact(upg, upv, cw, cb, C, name):
    T, F = upg.shape
    tc = LANES

    def body(g_ref, v_ref, w_ref, b_ref, o_ref):
        gp = g_ref[...]
        first, last = _conv_masks(T, C)
        prev, nxt = _shift_rows(gp, first, last)
        gc = prev * w_ref[0:1, :] + gp * w_ref[1:2, :] + nxt * w_ref[2:3, :] + b_ref[...]
        o_ref[...] = (gc * _sigmoid(gc) * v_ref[...]).astype(BF16)

    col = pl.BlockSpec((T, tc), lambda j: (0, j))
    return pl.pallas_call(
        body, name=name, grid=(F // tc,),
        in_specs=[col, col, pl.BlockSpec((3, tc), lambda j: (0, j)), pl.BlockSpec((1, tc), lambda j: (0, j))],
        out_specs=col, out_shape=jax.ShapeDtypeStruct((T, F), BF16),
        compiler_params=_cparams(("parallel",)),
    )(upg, upv, cw, cb)


def _convact_bwd(upg, upv, cw, cb, dact, C, name):
    T, F = upg.shape
    tc = LANES

    def body(g_ref, v_ref, w_ref, b_ref, da_ref, dg_ref, dv_ref, dw_ref, db_ref):
        gp = g_ref[...]
        val = v_ref[...]
        da = da_ref[...]
        first, last = _conv_masks(T, C)
        prev, nxt = _shift_rows(gp, first, last)
        w0, w1, w2 = w_ref[0:1, :], w_ref[1:2, :], w_ref[2:3, :]
        gc = prev * w0 + gp * w1 + nxt * w2 + b_ref[...]
        s = _sigmoid(gc)
        dv_ref[...] = (da * gc * s).astype(BF16)
        dgc = da * val * (s + gc * s * (1.0 - s))
        db_ref[...] = jnp.sum(dgc, axis=0, keepdims=True)
        dw_ref[0:1, :] = jnp.sum(dgc * prev, axis=0, keepdims=True)
        dw_ref[1:2, :] = jnp.sum(dgc * gp, axis=0, keepdims=True)
        dw_ref[2:3, :] = jnp.sum(dgc * nxt, axis=0, keepdims=True)
        dprev, dnxt = _shift_rows(dgc, first, last)
        dg_ref[...] = (dnxt * w0 + dgc * w1 + dprev * w2).astype(BF16)

    col = pl.BlockSpec((T, tc), lambda j: (0, j))
    w3 = pl.BlockSpec((3, tc), lambda j: (0, j))
    w1 = pl.BlockSpec((1, tc), lambda j: (0, j))
    return pl.pallas_call(
        body, name=name, grid=(F // tc,),
        in_specs=[col, col, w3, w1, col], out_specs=[col, col, w3, w1],
        out_shape=[jax.ShapeDtypeStruct((T, F), BF16), jax.ShapeDtypeStruct((T, F), BF16),
                   jax.ShapeDtypeStruct((3, F), F32), jax.ShapeDtypeStruct((1, F), F32)],
        compiler_params=_cparams(("parallel",)),
    )(upg, upv, cw, cb, dact)


def _rot_half(x):
    W = x.shape[-1]
    lane = lax.broadcasted_iota(jnp.int32, x.shape, x.ndim - 1)
    lo = (lane % 32) < 16
    return jnp.where(lo, -pltpu.roll(x, W - 16, x.ndim - 1), pltpu.roll(x, 16, x.ndim - 1))


def _swap_halves(x):
    return pltpu.roll(x, 64, x.ndim - 1)


def _rope_tables(dm):
    t = jnp.arange(dm.L)
    n_freq = HEAD_DIM // 4
    inv_freq = ROPE_BASE ** (-jnp.arange(n_freq, dtype=F32) / n_freq)
    ang_r = (t // GRID_W).astype(F32)[:, None] * inv_freq
    ang_c = (t % GRID_W).astype(F32)[:, None] * inv_freq
    ang = jnp.concatenate([ang_r, ang_r, ang_c, ang_c], axis=-1)
    cos = jnp.concatenate([jnp.ones((dm.C, HEAD_DIM), F32), jnp.cos(ang)], axis=0)
    sin = jnp.concatenate([jnp.zeros((dm.C, HEAD_DIM), F32), jnp.sin(ang)], axis=0)
    return jnp.tile(cos, (1, 2)), jnp.tile(sin, (1, 2))


def _attn_prep(z, cos, sin, dm, name):
    T = dm.T
    AW = dm.ATTN_W
    tm = _tile(T, 256, 16)
    rep = AW // LANES

    def body(z_ref, cos_ref, sin_ref, q_ref, k_ref, v_ref):
        cs, sn = cos_ref[...], sin_ref[...]
        q = z_ref[:, 0:AW]
        csq, snq = jnp.tile(cs, (1, rep)), jnp.tile(sn, (1, rep))
        q_ref[...] = (q * csq + _rot_half(q) * snq).astype(BF16)
        k = z_ref[:, dm.OFF_K:dm.OFF_V]
        k = k * cs + _rot_half(k) * sn
        v = z_ref[:, dm.OFF_V:dm.OFF_S]
        lo = lax.broadcasted_iota(jnp.int32, k.shape, 1) < HEAD_DIM
        ks, vs = _swap_halves(k), _swap_halves(v)
        k_ref[0] = jnp.where(lo, k, ks).astype(BF16)
        k_ref[1] = jnp.where(lo, ks, k).astype(BF16)
        v_ref[0] = jnp.where(lo, v, vs).astype(BF16)
        v_ref[1] = jnp.where(lo, vs, v).astype(BF16)

    tab = pl.BlockSpec((tm, LANES), lambda i: (i, 0))
    kv = pl.BlockSpec((2, tm, LANES), lambda i: (0, i, 0))
    return pl.pallas_call(
        body, name=name, grid=(T // tm,),
        in_specs=[pl.BlockSpec((tm, dm.OFF_S), lambda i: (i, 0)), tab, tab],
        out_specs=[pl.BlockSpec((tm, AW), lambda i: (i, 0)), kv, kv],
        out_shape=[jax.ShapeDtypeStruct((T, AW), BF16), jax.ShapeDtypeStruct((2, T, LANES), BF16),
                   jax.ShapeDtypeStruct((2, T, LANES), BF16)],
        compiler_params=_cparams(("parallel",)),
    )(z, cos, sin)


def _attn_prep_bwd(dq, dk2, dv2, cos, sin, dm, name):
    T = dm.T
    AW = dm.ATTN_W
    tm = _tile(T, 256, 16)
    rep = AW // LANES

    def body(dq_ref, dk_ref, dv_ref, cos_ref, sin_ref, o_ref):
        cs, sn = cos_ref[...], sin_ref[...]
        csq, snq = jnp.tile(cs, (1, rep)), jnp.tile(sn, (1, rep))
        dq = dq_ref[...]
        o_ref[:, 0:AW] = dq * csq - _rot_half(dq * snq)
        lo = lax.broadcasted_iota(jnp.int32, (tm, LANES), 1) < HEAD_DIM
        dk = jnp.where(lo, dk_ref[0], dk_ref[1])
        o_ref[:, dm.OFF_K:dm.OFF_V] = dk * cs - _rot_half(dk * sn)
        o_ref[:, dm.OFF_V:dm.OFF_S] = jnp.where(lo, dv_ref[0], dv_ref[1])

    tab = pl.BlockSpec((tm, LANES), lambda i: (i, 0))
    kv = pl.BlockSpec((2, tm, LANES), lambda i: (0, i, 0))
    return pl.pallas_call(
        body, name=name, grid=(T // tm,),
        in_specs=[pl.BlockSpec((tm, AW), lambda i: (i, 0)), kv, kv, tab, tab],
        out_specs=pl.BlockSpec((tm, dm.OFF_S), lambda i: (i, 0)),
        out_shape=jax.ShapeDtypeStruct((T, dm.OFF_S), F32),
        compiler_params=_cparams(("parallel",)),
    )(dq, dk2, dv2, cos, sin)


def _attn_specs(dm):
    B = ATTN_BLOCK
    nblk = dm.T // B
    q_spec = pl.BlockSpec((B, dm.ATTN_W), lambda i: (i, 0))
    prev = pl.BlockSpec((2, B, LANES), lambda i: (0, jnp.maximum(i - 1, 0), 0))
    own = pl.BlockSpec((2, B, LANES), lambda i: (0, i, 0))
    nxt = pl.BlockSpec((2, B, LANES), lambda i: (0, jnp.minimum(i + 1, nblk - 1), 0))
    ctx = pl.BlockSpec((2, dm.C, LANES), lambda i: (0, 0, 0))
    sink = pl.BlockSpec((2, GQA_RATIO * B, 1), lambda i: (0, 0, 0))
    return nblk, q_spec, prev, own, nxt, ctx, sink


def _attn_scores(i, q_ref, kp_ref, ko_ref, kn_ref, kc_ref, sink_ref, h, dm):
    B = ATTN_BLOCK
    nctx = dm.C // B
    nb = dm.L // B
    npair = GQA_RATIO // 2
    lo = lax.broadcasted_iota(jnp.int32, (B, LANES), 1) < HEAD_DIM
    zero = jnp.zeros((B, LANES), BF16)
    parts = []
    for p in range(npair):
        q2 = q_ref[:, (h * npair + p) * LANES:(h * npair + p + 1) * LANES]
        parts.append(jnp.where(lo, q2, zero))
        parts.append(jnp.where(lo, zero, q2))
    q8 = jnp.concatenate(parts, axis=0)
    kcat = jnp.concatenate([kp_ref[h], ko_ref[h], kn_ref[h], kc_ref[h]], axis=0)
    s = _dot(q8, kcat, _NT) * (HEAD_DIM ** -0.5)
    R, NK = s.shape
    iq = lax.broadcasted_iota(jnp.int32, (R, NK), 0) % B
    col = lax.broadcasted_iota(jnp.int32, (R, NK), 1)
    jk = col % B
    qb = i - nctx
    lat = qb >= 0
    m_prev = (col < B) & lat & (qb >= 1) & (jk >= iq)
    m_own = (col >= B) & (col < 2 * B) & lat
    m_next = (col >= 2 * B) & (col < 3 * B) & lat & (qb <= nb - 2) & (jk <= iq)
    mask = m_prev | m_own | m_next | (col >= 3 * B)
    s = jnp.where(mask, s, -jnp.inf)
    sk = sink_ref[h]
    m = jnp.maximum(jnp.max(s, axis=-1, keepdims=True), sk)
    e = jnp.exp(s - m)
    es = jnp.exp(sk - m)
    inv = 1.0 / (jnp.sum(e, axis=-1, keepdims=True) + es)
    return q8, kcat, e * inv, es * inv, lo


def _unstack_heads(x8, lo):
    B = ATTN_BLOCK
    out = []
    for p in range(GQA_RATIO // 2):
        a = x8[(2 * p) * B:(2 * p + 1) * B]
        b = x8[(2 * p + 1) * B:(2 * p + 2) * B]
        out.append(jnp.where(lo, a, b))
    return out


def _attention(q, k2, v2, sinkcol, dm, name):
    B = ATTN_BLOCK
    nblk, q_spec, prev, own, nxt, ctx, sink = _attn_specs(dm)
    npair = GQA_RATIO // 2

    def body(q_ref, kp_ref, ko_ref, kn_ref, kc_ref, vp_ref, vo_ref, vn_ref, vc_ref, sink_ref, o_ref):
        i = pl.program_id(0)
        for h in range(2):
            _, _, p, _, lo = _attn_scores(i, q_ref, kp_ref, ko_ref, kn_ref, kc_ref, sink_ref, h, dm)
            vcat = jnp.concatenate([vp_ref[h], vo_ref[h], vn_ref[h], vc_ref[h]], axis=0)
            o8 = _dot(p, vcat, _NN)
            for pi, o2 in enumerate(_unstack_heads(o8, lo)):
                c0 = (h * npair + pi) * LANES
                o_ref[:, c0:c0 + LANES] = o2.astype(BF16)

    return pl.pallas_call(
        body, name=name, grid=(nblk,),
        in_specs=[q_spec, prev, own, nxt, ctx, prev, own, nxt, ctx, sink],
        out_specs=q_spec, out_shape=jax.ShapeDtypeStruct((dm.T, dm.ATTN_W), BF16),
        compiler_params=_cparams(("parallel",)),
    )(q, k2, k2, k2, k2, v2, v2, v2, v2, sinkcol)


def _attention_bwd(q, k2, v2, sinkcol, do, dm, name):
    B = ATTN_BLOCK
    nblk, q_spec, prev, own, nxt, ctx, sink = _attn_specs(dm)
    npair = GQA_RATIO // 2

    def body(q_ref, kp_ref, ko_ref, kn_ref, kc_ref, vp_ref, vo_ref, vn_ref, vc_ref, sink_ref, do_ref,
             dq_ref, pk_ref, pv_ref, dkc_ref, dvc_ref, dsk_ref):
        i = pl.program_id(0)

        @pl.when(i == 0)
        def _():
            dkc_ref[...] = jnp.zeros_like(dkc_ref)
            dvc_ref[...] = jnp.zeros_like(dvc_ref)
            dsk_ref[...] = jnp.zeros_like(dsk_ref)

        for h in range(2):
            q8, kcat, p, ps, lo = _attn_scores(i, q_ref, kp_ref, ko_ref, kn_ref, kc_ref, sink_ref, h, dm)
            vcat = jnp.concatenate([vp_ref[h], vo_ref[h], vn_ref[h], vc_ref[h]], axis=0)
            zero = jnp.zeros((B, LANES), F32)
            parts = []
            for pi in range(npair):
                d2 = do_ref[:, (h * npair + pi) * LANES:(h * npair + pi + 1) * LANES]
                parts.append(jnp.where(lo, d2, zero))
                parts.append(jnp.where(lo, zero, d2))
            do8 = jnp.concatenate(parts, axis=0)
            dp = _dot(do8, vcat, _NT)
            delta = jnp.sum(p * dp, axis=-1, keepdims=True)
            ds = p * (dp - delta) * (HEAD_DIM ** -0.5)
            dsr = -ps * delta
            rows = [jnp.broadcast_to(jnp.sum(dsr[r * B:(r + 1) * B], axis=0, keepdims=True), (1, LANES))
                    for r in range(GQA_RATIO)]
            dsk_ref[h] += jnp.concatenate(rows, axis=0)
            dq8 = _dot(ds, kcat, _NN)
            for pi, d2 in enumerate(_unstack_heads(dq8, lo)):
                c0 = (h * npair + pi) * LANES
                dq_ref[:, c0:c0 + LANES] = d2
            dk = _dot(ds, q8, _TN)
            dv = _dot(p, do8, _TN)
            dk = dk + _swap_halves(dk)
            dv = dv + _swap_halves(dv)
            for w in range(3):
                pk_ref[0, h, w] = dk[w * B:(w + 1) * B]
                pv_ref[0, h, w] = dv[w * B:(w + 1) * B]
            dkc_ref[h] += dk[3 * B:]
            dvc_ref[h] += dv[3 * B:]

    part = pl.BlockSpec((1, 2, 3, B, LANES), lambda i: (i, 0, 0, 0, 0))
    acc_c = pl.BlockSpec((2, dm.C, LANES), lambda i: (0, 0, 0))
    acc_s = pl.BlockSpec((2, GQA_RATIO, LANES), lambda i: (0, 0, 0))
    return pl.pallas_call(
        body, name=name, grid=(nblk,),
        in_specs=[q_spec, prev, own, nxt, ctx, prev, own, nxt, ctx, sink, q_spec],
        out_specs=[q_spec, part, part, acc_c, acc_c, acc_s],
        out_shape=[jax.ShapeDtypeStruct((dm.T, dm.ATTN_W), F32),
                   jax.ShapeDtypeStruct((nblk, 2, 3, B, LANES), F32),
                   jax.ShapeDtypeStruct((nblk, 2, 3, B, LANES), F32),
                   jax.ShapeDtypeStruct((2, dm.C, LANES), F32), jax.ShapeDtypeStruct((2, dm.C, LANES), F32),
                   jax.ShapeDtypeStruct((2, GQA_RATIO, LANES), F32)],
        compiler_params=_cparams(("arbitrary",)),
    )(q, k2, k2, k2, k2, v2, v2, v2, v2, sinkcol, do)


def _kv_reduce(part, ctxacc, dm, name):
    B = ATTN_BLOCK
    nblk = dm.T // B
    nctx = dm.C // B

    def body(own_ref, nxt_ref, prv_ref, c_ref, o_ref):
        j = pl.program_id(0)
        acc = own_ref[0, :, 0]
        acc = acc + jnp.where(j < nblk - 1, nxt_ref[0, :, 0], 0.0)
        acc = acc + jnp.where(j > 0, prv_ref[0, :, 0], 0.0)
        acc = acc + jnp.where(j < nctx, c_ref[...], 0.0)
        o_ref[...] = acc

    def spec(w, f):
        return pl.BlockSpec((1, 2, 1, B, LANES), lambda j: (f(j), 0, w, 0, 0))

    return pl.pallas_call(
        body, name=name, grid=(nblk,),
        in_specs=[spec(1, lambda j: j), spec(0, lambda j: jnp.minimum(j + 1, nblk - 1)),
                  spec(2, lambda j: jnp.maximum(j - 1, 0)),
                  pl.BlockSpec((2, B, LANES), lambda j: (0, jnp.minimum(j, nctx - 1), 0))],
        out_specs=pl.BlockSpec((2, B, LANES), lambda j: (0, j, 0)),
        out_shape=jax.ShapeDtypeStruct((2, dm.T, LANES), F32),
        compiler_params=_cparams(("parallel",)),
    )(part, part, part, ctxacc)


def _gmlp_core(z_ref, lg_ref, lb_ref, ws_ref, bs_ref, dm):
    GW = dm.GMLP_W
    gu = z_ref[:, dm.OFF_GU:dm.OFF_GV]
    gv = z_ref[:, dm.OFF_GV:dm.N_IN]
    u = _gelu(gu)
    ge = _gelu(gv)
    mu = jnp.mean(ge, axis=-1, keepdims=True)
    xc = ge - mu
    r = lax.rsqrt(jnp.mean(xc * xc, axis=-1, keepdims=True) + NORM_EPS)
    xh = xc * r
    v = xh * lg_ref[...] + lb_ref[...]
    mixed = []
    for g in range(dm.GG):
        vg = v[:, g * GMLP_GROUP_W:(g + 1) * GMLP_GROUP_W]
        mixed.append(_dot(ws_ref[g], vg, _NN) + bs_ref[g])
    return gu, gv, u, xh, r, v, mixed


def _gmlp(z, ln_g, ln_b, w_s, b_s, dm, name):
    T, GW, CH = dm.T, dm.GMLP_W, GMLP_CHUNK

    def body(z_ref, lg_ref, lb_ref, ws_ref, bs_ref, o_ref):
        _, _, u, _, _, _, mixed = _gmlp_core(z_ref, lg_ref, lb_ref, ws_ref, bs_ref, dm)
        for g in range(dm.GG):
            sl = slice(g * GMLP_GROUP_W, (g + 1) * GMLP_GROUP_W)
            o_ref[:, sl] = (u[:, sl] * mixed[g]).astype(BF16)

    vec = pl.BlockSpec((1, GW), lambda i: (0, 0))
    return pl.pallas_call(
        body, name=name, grid=(T // CH,),
        in_specs=[pl.BlockSpec((CH, dm.N_IN), lambda i: (i, 0)), vec, vec,
                  pl.BlockSpec((dm.GG, CH, CH), lambda i: (0, 0, 0)),
                  pl.BlockSpec((dm.GG, CH, 1), lambda i: (0, 0, 0))],
        out_specs=pl.BlockSpec((CH, GW), lambda i: (i, 0)),
        out_shape=jax.ShapeDtypeStruct((T, GW), BF16),
        compiler_params=_cparams(("parallel",)),
    )(z, ln_g, ln_b, w_s, b_s)


def _gmlp_bwd(z, ln_g, ln_b, w_s, b_s, do, dm, name):
    T, GW, CH = dm.T, dm.GMLP_W, GMLP_CHUNK

    def body(z_ref, lg_ref, lb_ref, ws_ref, bs_ref, do_ref, dz_ref, dlg_ref, dlb_ref, dws_ref, dbs_ref):
        i = pl.program_id(0)

        @pl.when(i == 0)
        def _():
            dlg_ref[...] = jnp.zeros_like(dlg_ref)
            dlb_ref[...] = jnp.zeros_like(dlb_ref)
            dws_ref[...] = jnp.zeros_like(dws_ref)
            dbs_ref[...] = jnp.zeros_like(dbs_ref)

        gu, gv, u, xh, r, v, mixed = _gmlp_core(z_ref, lg_ref, lb_ref, ws_ref, bs_ref, dm)
        do = do_ref[...]
        dv_parts = []
        for g in range(dm.GG):
            sl = slice(g * GMLP_GROUP_W, (g + 1) * GMLP_GROUP_W)
            dog = do[:, sl]
            dz_ref[:, sl] = dog * mixed[g] * _gelu_grad(gu[:, sl])
            dmx = dog * u[:, sl]
            dbs_ref[g] += jnp.sum(dmx, axis=-1, keepdims=True)
            dws_ref[g] += _dot(dmx, v[:, sl], _NT)
            dv_parts.append(_dot(ws_ref[g], dmx, _TN))
        dv = jnp.concatenate(dv_parts, axis=-1) if dm.GG > 1 else dv_parts[0]
        dlg_ref[...] += jnp.sum(dv * xh, axis=0, keepdims=True)
        dlb_ref[...] += jnp.sum(dv, axis=0, keepdims=True)
        dxh = dv * lg_ref[...]
        dge = r * (dxh - jnp.mean(dxh, axis=-1, keepdims=True) - xh * jnp.mean(dxh * xh, axis=-1, keepdims=True))
        dz_ref[:, GW:2 * GW] = dge * _gelu_grad(gv)

    vec = pl.BlockSpec((1, GW), lambda i: (0, 0))
    wsp = pl.BlockSpec((dm.GG, CH, CH), lambda i: (0, 0, 0))
    bsp = pl.BlockSpec((dm.GG, CH, 1), lambda i: (0, 0, 0))
    return pl.pallas_call(
        body, name=name, grid=(T // CH,),
        in_specs=[pl.BlockSpec((CH, dm.N_IN), lambda i: (i, 0)), vec, vec, wsp, bsp,
                  pl.BlockSpec((CH, GW), lambda i: (i, (dm.ATTN_W + dm.SSM_W) // GW))],
        out_specs=[pl.BlockSpec((CH, 2 * GW), lambda i: (i, 0)), vec, vec, wsp, bsp],
        out_shape=[jax.ShapeDtypeStruct((T, 2 * GW), F32), jax.ShapeDtypeStruct((1, GW), F32),
                   jax.ShapeDtypeStruct((1, GW), F32), jax.ShapeDtypeStruct((dm.GG, CH, CH), F32),
                   jax.ShapeDtypeStruct((dm.GG, CH, 1), F32)],
        compiler_params=_cparams(("arbitrary",)),
    )(z, ln_g, ln_b, w_s, b_s, do)


def _disc_fn(lam_re, lam_im, ldt, b_re, b_im):
    dt = jnp.exp(ldt)
    mag = jnp.exp(lam_re * dt)
    a_re = mag * jnp.cos(lam_im * dt)
    a_im = mag * jnp.sin(lam_im * dt)
    den = lam_re * lam_re + lam_im * lam_im
    n_re = a_re - 1.0
    f_re = (n_re * lam_re + a_im * lam_im) / den
    f_im = (a_im * lam_re - n_re * lam_im) / den
    return a_re, a_im, f_re * b_re - f_im * b_im, f_re * b_im + f_im * b_re


def _s5_disc(lam_re, lam_im, ldt, bT_re, bT_im, name):
    nd, H, GP = bT_re.shape

    def body(lr_ref, li_ref, dt_ref, br_ref, bi_ref, ar_ref, ai_ref, bbr_ref, bbi_ref):
        for d in range(nd):
            a_re, a_im, bb_re, bb_im = _disc_fn(lr_ref[d], li_ref[d], dt_ref[d], br_ref[d], bi_ref[d])
            ar_ref[d] = a_re
            ai_ref[d] = a_im
            bbr_ref[d] = bb_re
            bbi_ref[d] = bb_im

    vs = jax.ShapeDtypeStruct((nd, 1, GP), F32)
    ms = jax.ShapeDtypeStruct((nd, H, GP), F32)
    return pl.pallas_call(body, name=name, out_shape=[vs, vs, ms, ms], compiler_params=_cparams())(
        lam_re, lam_im, ldt, bT_re, bT_im)


def _s5_disc_bwd(lam_re, lam_im, ldt, bT_re, bT_im, da_re, da_im, dbb_re, dbb_im, name):
    nd, H, GP = bT_re.shape

    def body(lr_ref, li_ref, dt_ref, br_ref, bi_ref, dar_ref, dai_ref, dbr_ref, dbi_ref,
             olr_ref, oli_ref, odt_ref, obr_ref, obi_ref):
        for d in range(nd):
            _, vjp = jax.vjp(_disc_fn, lr_ref[d], li_ref[d], dt_ref[d], br_ref[d], bi_ref[d])
            g = vjp((dar_ref[d], dai_ref[d], dbr_ref[d], dbi_ref[d]))
            olr_ref[d], oli_ref[d], odt_ref[d], obr_ref[d], obi_ref[d] = g

    vs = jax.ShapeDtypeStruct((nd, 1, GP), F32)
    ms = jax.ShapeDtypeStruct((nd, H, GP), F32)
    return pl.pallas_call(body, name=name, out_shape=[vs, vs, vs, ms, ms], compiler_params=_cparams())(
        lam_re, lam_im, ldt, bT_re, bT_im, da_re, da_im, dbb_re, dbb_im)


def _scan_pass(re_ref, im_ref, nsteps, ar, ai, ir, ii, reverse, store):
    def step(n, carry):
        sr, si = carry
        t = (nsteps - 1 - n) if reverse else n
        off = pl.multiple_of(t * 8, 8)
        nr = ar * sr - ai * si + re_ref[pl.ds(off, 8), :]
        ni = ar * si + ai * sr + im_ref[pl.ds(off, 8), :]
        if store:
            re_ref[pl.ds(off, 8), :] = nr
            im_ref[pl.ds(off, 8), :] = ni
        return nr, ni

    return lax.fori_loop(0, nsteps, step, (ir, ii), unroll=4)


def _cpow(ar, ai, n):
    rr = ri = None
    br, bi = ar, ai
    while n:
        if n & 1:
            rr, ri = (br, bi) if rr is None else (rr * br - ri * bi, rr * bi + ri * br)
        n >>= 1
        if n:
            br, bi = br * br - bi * bi, 2.0 * br * bi
    return rr, ri


def _row_of(x, k):
    rows = lax.broadcasted_iota(jnp.int32, x.shape, 0)
    return jnp.sum(jnp.where(rows == k, x, 0.0), axis=0, keepdims=True)


def _full_scan(re_ref, im_ref, nsteps, ar1, ai1, h0r, h0i, reverse):
    P = ar1.shape[1]
    ar, ai = jnp.broadcast_to(ar1, (8, P)), jnp.broadcast_to(ai1, (8, P))
    z = jnp.zeros((8, P), F32)
    er, ei = _scan_pass(re_ref, im_ref, nsteps, ar, ai, z, z, reverse, False)
    pr, pi = _cpow(ar1, ai1, nsteps)
    rows = lax.broadcasted_iota(jnp.int32, (8, P), 0)
    initr, initi = z, z
    cr, ci = h0r, h0i
    for k in (range(SCAN_SEGMENTS - 1, -1, -1) if reverse else range(SCAN_SEGMENTS)):
        initr = jnp.where(rows == k, cr, initr)
        initi = jnp.where(rows == k, ci, initi)
        ekr, eki = _row_of(er, k), _row_of(ei, k)
        cr, ci = ekr + pr * cr - pi * ci, eki + pr * ci + pi * cr
    _scan_pass(re_ref, im_ref, nsteps, ar, ai, initr, initi, reverse, True)
    return initr, initi, cr, ci


def _rows_loop(n, chunk, fn):
    def step(c, carry):
        fn(pl.multiple_of(c * chunk, chunk))
        return carry
    lax.fori_loop(0, n // chunk, step, 0)


def _s5_specs(dm):
    PB = STATE_BLOCK
    nsb = dm.GP // PB
    per = (LANES * SSM_STATE // SSM_GROUP) // PB
    ul = pl.BlockSpec((dm.L, LANES), lambda j: (0, j // per))
    uc = pl.BlockSpec((dm.C, LANES), lambda j: (0, j // per))
    av = pl.BlockSpec((2, 1, PB), lambda j: (0, 0, j))
    bd = pl.BlockSpec((2, LANES, PB), lambda j: (0, j // per, j))
    cd = pl.BlockSpec((2, PB, LANES), lambda j: (0, j, j // per))
    dv = pl.BlockSpec((1, LANES), lambda j: (0, j // per))
    return PB, nsb, per, ul, uc, av, bd, cd, dv


def _s5_scan(ul, uc, a_re, a_im, bd_re, bd_im, cd_re, cd_im, dskip, dm, name):
    PB, nsb, per, ul_s, uc_s, av, bd, cd, dv = _s5_specs(dm)
    L, C = dm.L, dm.C
    RC = _tile(L, 512, 8)
    RCC = _tile(C, 512, 8)

    def body(ul_ref, uc_ref, ar_ref, ai_ref, br_ref, bi_ref, cr_ref, ci_ref, d_ref, yl_ref, yc_ref,
             lre, lim, cre, cim):
        j = pl.program_id(0)

        @pl.when(j % per == 0)
        def _():
            yl_ref[...] = ul_ref[...] * d_ref[...]
            yc_ref[...] = uc_ref[...] * d_ref[...]

        for d in range(2):
            rev = d == 1
            ar1, ai1 = ar_ref[d], ai_ref[d]

            def proj(u_ref, re, im, chunk, n):
                def f(r0):
                    u = u_ref[pl.ds(r0, chunk), :]
                    re[pl.ds(r0, chunk), :] = _dot(u, br_ref[d], _NN)
                    im[pl.ds(r0, chunk), :] = _dot(u, bi_ref[d], _NN)
                _rows_loop(n, chunk, f)

            def readout(y_ref, re, im, chunk, n):
                def f(r0):
                    y_ref[pl.ds(r0, chunk), :] += (_dot(re[pl.ds(r0, chunk), :], cr_ref[d], _NN)
                                                   - _dot(im[pl.ds(r0, chunk), :], ci_ref[d], _NN))
                _rows_loop(n, chunk, f)

            z1 = jnp.zeros((1, PB), F32)
            proj(uc_ref, cre, cim, RCC, C)
            _, _, fr, fi = _full_scan(cre, cim, C // 8, ar1, ai1, z1, z1, rev)
            readout(yc_ref, cre, cim, RCC, C)
            proj(ul_ref, lre, lim, RC, L)
            _full_scan(lre, lim, L // 8, ar1, ai1, fr, fi, rev)
            readout(yl_ref, lre, lim, RC, L)

    return pl.pallas_call(
        body, name=name, grid=(nsb,),
        in_specs=[ul_s, uc_s, av, av, bd, bd, cd, cd, dv],
        out_specs=[ul_s, uc_s],
        out_shape=[jax.ShapeDtypeStruct((L, dm.SSM_W), F32), jax.ShapeDtypeStruct((C, dm.SSM_W), F32)],
        scratch_shapes=[pltpu.VMEM((L, PB), F32), pltpu.VMEM((L, PB), F32),
                        pltpu.VMEM((C, PB), F32), pltpu.VMEM((C, PB), F32)],
        compiler_params=_cparams(("arbitrary",)),
    )(ul, uc, a_re, a_im, bd_re, bd_im, cd_re, cd_im, dskip)


def _s5_scan_bwd(ul, uc, dyl, dyc, a_re, a_im, bd_re, bd_im, cd_re, cd_im, bdT_re, bdT_im, cdT_re, cdT_im,
                 dskip, dm, name):
    PB, nsb, per, ul_s, uc_s, av, bd, cd, dv = _s5_specs(dm)
    L, C = dm.L, dm.C
    RC = _tile(L, 512, 8)
    RCC = _tile(C, 512, 8)
    bdT = pl.BlockSpec((2, PB, LANES), lambda j: (0, j, j // per))
    cdT = pl.BlockSpec((2, LANES, PB), lambda j: (0, j // per, j))
    dbd = pl.BlockSpec((2, 1, LANES, PB), lambda j: (0, j, 0, 0))
    dcd = pl.BlockSpec((2, 1, PB, LANES), lambda j: (0, j, 0, 0))

    def body(ul_ref, uc_ref, dyl_ref, dyc_ref, ar_ref, ai_ref, br_ref, bi_ref, cr_ref, ci_ref,
             brT_ref, biT_ref, crT_ref, ciT_ref, d_ref,
             dul_ref, duc_ref, dar_ref, dai_ref, dbr_ref, dbi_ref, dcr_ref, dci_ref, dd_ref,
             lre, lim, cre, cim, gre, gim, hre, him):
        j = pl.program_id(0)

        @pl.when(j % per == 0)
        def _():
            dul_ref[...] = dyl_ref[...] * d_ref[...]
            duc_ref[...] = dyc_ref[...] * d_ref[...]
            dd_ref[...] = (jnp.sum(dyl_ref[...] * ul_ref[...], axis=0, keepdims=True)
                           + jnp.sum(dyc_ref[...] * uc_ref[...], axis=0, keepdims=True))

        for d in range(2):
            rev = d == 1
            ar1, ai1 = ar_ref[d], ai_ref[d]
            z1 = jnp.zeros((1, PB), F32)

            def proj(u_ref, w_re, w_im, sign, re, im, chunk, n):
                def f(r0):
                    u = u_ref[pl.ds(r0, chunk), :]
                    re[pl.ds(r0, chunk), :] = _dot(u, w_re, _NN)
                    im[pl.ds(r0, chunk), :] = sign * _dot(u, w_im, _NN)
                _rows_loop(n, chunk, f)

            proj(uc_ref, br_ref[d], bi_ref[d], 1.0, cre, cim, RCC, C)
            icr, ici, fr, fi = _full_scan(cre, cim, C // 8, ar1, ai1, z1, z1, rev)
            proj(ul_ref, br_ref[d], bi_ref[d], 1.0, lre, lim, RC, L)
            ilr, ili, _, _ = _full_scan(lre, lim, L // 8, ar1, ai1, fr, fi, rev)
            proj(dyl_ref, crT_ref[d], ciT_ref[d], -1.0, gre, gim, RC, L)
            _, _, gfr, gfi = _full_scan(gre, gim, L // 8, ar1, -ai1, z1, z1, not rev)
            proj(dyc_ref, crT_ref[d], ciT_ref[d], -1.0, hre, him, RCC, C)
            _full_scan(hre, him, C // 8, ar1, -ai1, gfr, gfi, not rev)

            dar = jnp.zeros((8, PB), F32)
            dai = jnp.zeros((8, PB), F32)
            dbr = jnp.zeros((LANES, PB), F32)
            dbi = jnp.zeros((LANES, PB), F32)
            dcr = jnp.zeros((PB, LANES), F32)
            dci = jnp.zeros((PB, LANES), F32)
            for (u_ref, dy_ref, du_ref, sre, sim, are, aim, inr, ini, n, chunk) in (
                    (ul_ref, dyl_ref, dul_ref, lre, lim, gre, gim, ilr, ili, L, RC),
                    (uc_ref, dyc_ref, duc_ref, cre, cim, hre, him, icr, ici, C, RCC)):
                nst = n // 8

                def da_step(t, carry, sre=sre, sim=sim, are=are, aim=aim, nst=nst):
                    xr, xi = carry
                    tp = (t + 1) if rev else (t - 1)
                    o = pl.multiple_of(t * 8, 8)
                    op = pl.multiple_of(tp * 8, 8)
                    g_r, g_i = are[pl.ds(o, 8), :], aim[pl.ds(o, 8), :]
                    p_r, p_i = sre[pl.ds(op, 8), :], sim[pl.ds(op, 8), :]
                    return xr + g_r * p_r + g_i * p_i, xi + g_i * p_r - g_r * p_i

                lo_t, hi_t = (0, nst - 1) if rev else (1, nst)
                dar, dai = lax.fori_loop(lo_t, hi_t, da_step, (dar, dai))
                e0 = (nst - 1) * 8 if rev else 0
                g_r, g_i = are[pl.ds(e0, 8), :], aim[pl.ds(e0, 8), :]
                dar = dar + g_r * inr + g_i * ini
                dai = dai + g_i * inr - g_r * ini

                def mats(c, carry, u_ref=u_ref, dy_ref=dy_ref, du_ref=du_ref, sre=sre, sim=sim, are=are, aim=aim,
                         chunk=chunk):
                    xbr, xbi, xcr, xci = carry
                    r0 = pl.multiple_of(c * chunk, chunk)
                    u = u_ref[pl.ds(r0, chunk), :]
                    dy = dy_ref[pl.ds(r0, chunk), :]
                    g_r, g_i = are[pl.ds(r0, chunk), :], aim[pl.ds(r0, chunk), :]
                    du_ref[pl.ds(r0, chunk), :] += _dot(g_r, brT_ref[d], _NN) + _dot(g_i, biT_ref[d], _NN)
                    xbr = xbr + _dot(u, g_r, _TN)
                    xbi = xbi + _dot(u, g_i, _TN)
                    xcr = xcr + _dot(sre[pl.ds(r0, chunk), :], dy, _TN)
                    xci = xci - _dot(sim[pl.ds(r0, chunk), :], dy, _TN)
                    return xbr, xbi, xcr, xci

                dbr, dbi, dcr, dci = lax.fori_loop(0, n // chunk, mats, (dbr, dbi, dcr, dci))
            dar_ref[d] = jnp.sum(dar, axis=0, keepdims=True)
            dai_ref[d] = jnp.sum(dai, axis=0, keepdims=True)
            dbr_ref[d, 0] = dbr
            dbi_ref[d, 0] = dbi
            dcr_ref[d, 0] = dcr
            dci_ref[d, 0] = dci

    W, GP = dm.SSM_W, dm.GP
    return pl.pallas_call(
        body, name=name, grid=(nsb,),
        in_specs=[ul_s, uc_s, ul_s, uc_s, av, av, bd, bd, cd, cd, bdT, bdT, cdT, cdT, dv],
        out_specs=[ul_s, uc_s, av, av, dbd, dbd, dcd, dcd, dv],
        out_shape=[jax.ShapeDtypeStruct((L, W), F32), jax.ShapeDtypeStruct((C, W), F32),
                   jax.ShapeDtypeStruct((2, 1, GP), F32), jax.ShapeDtypeStruct((2, 1, GP), F32),
                   jax.ShapeDtypeStruct((2, nsb, LANES, PB), F32), jax.ShapeDtypeStruct((2, nsb, LANES, PB), F32),
                   jax.ShapeDtypeStruct((2, nsb, PB, LANES), F32), jax.ShapeDtypeStruct((2, nsb, PB, LANES), F32),
                   jax.ShapeDtypeStruct((1, W), F32)],
        scratch_shapes=[pltpu.VMEM((L, PB), F32), pltpu.VMEM((L, PB), F32),
                        pltpu.VMEM((C, PB), F32), pltpu.VMEM((C, PB), F32),
                        pltpu.VMEM((L, PB), F32), pltpu.VMEM((L, PB), F32),
                        pltpu.VMEM((C, PB), F32), pltpu.VMEM((C, PB), F32)],
        compiler_params=_cparams(("arbitrary",)),
    )(ul, uc, dyl, dyc, a_re, a_im, bd_re, bd_im, cd_re, cd_im, bdT_re, bdT_im, cdT_re, cdT_im, dskip)


def _glu(y, w, b, name):
    T, W = y.shape
    tm = _tile(T, 544, 16)

    def body(y_ref, w_ref, b_ref, o_ref):
        g = _gelu(y_ref[...])
        o_ref[...] = (g * _sigmoid(_dot(g, w_ref[...], _NN) + b_ref[...])).astype(BF16)

    return pl.pallas_call(
        body, name=name, grid=(T // tm,),
        in_specs=[pl.BlockSpec((tm, W), lambda i: (i, 0)), pl.BlockSpec((W, W), lambda i: (0, 0)),
                  pl.BlockSpec((1, W), lambda i: (0, 0))],
        out_specs=pl.BlockSpec((tm, W), lambda i: (i, 0)),
        out_shape=jax.ShapeDtypeStruct((T, W), BF16),
        compiler_params=_cparams(("parallel",)),
    )(y, w, b)


def _glu_bwd(y, w, b, do, do_col, name):
    T, W = y.shape
    tm = _tile(T, 544, 16)

    def body(y_ref, w_ref, b_ref, do_ref, dy_ref, dw_ref, db_ref):
        i = pl.program_id(0)

        @pl.when(i == 0)
        def _():
            dw_ref[...] = jnp.zeros_like(dw_ref)
            db_ref[...] = jnp.zeros_like(db_ref)

        y = y_ref[...]
        do = do_ref[...]
        g = _gelu(y)
        s = _sigmoid(_dot(g, w_ref[...], _NN) + b_ref[...])
        dpre = do * g * s * (1.0 - s)
        db_ref[...] += jnp.sum(dpre, axis=0, keepdims=True)
        dw_ref[...] += _dot(g, dpre, _TN)
        dg = do * s + _dot(dpre, w_ref[...], _NT)
        dy_ref[...] = dg * _gelu_grad(y)

    row = pl.BlockSpec((tm, W), lambda i: (i, 0))
    wsp = pl.BlockSpec((W, W), lambda i: (0, 0))
    vec = pl.BlockSpec((1, W), lambda i: (0, 0))
    return pl.pallas_call(
        body, name=name, grid=(T // tm,),
        in_specs=[row, wsp, vec, pl.BlockSpec((tm, W), lambda i: (i, do_col))], out_specs=[row, wsp, vec],
        out_shape=[jax.ShapeDtypeStruct((T, W), F32), jax.ShapeDtypeStruct((W, W), F32),
                   jax.ShapeDtypeStruct((1, W), F32)],
        compiler_params=_cparams(("arbitrary",)),
    )(y, w, b, do)


def _loss_head(X, g, target, C, name):
    T, D = X.shape
    tm = _tile(math.gcd(C, T - C), 512, 16)
    nc = C // tm

    def body(x_ref, g_ref, t_ref, loss_ref, dx_ref, dg_ref):
        i = pl.program_id(0)

        @pl.when(i == 0)
        def _():
            loss_ref[...] = jnp.zeros_like(loss_ref)
            dg_ref[...] = jnp.zeros_like(dg_ref)

        @pl.when(i < nc)
        def _():
            dx_ref[...] = jnp.zeros_like(dx_ref)

        @pl.when(i >= nc)
        def _():
            x = x_ref[...]
            gv = g_ref[...]
            r = lax.rsqrt(jnp.mean(x * x, axis=-1, keepdims=True) + NORM_EPS)
            xh = x * r
            err = xh * gv - t_ref[...]
            loss_ref[...] += 0.5 * jnp.sum(jnp.mean(err * err, axis=-1, keepdims=True), axis=0, keepdims=True)
            dy = err * (1.0 / D)
            dg_ref[...] += jnp.sum(dy * xh, axis=0, keepdims=True)
            dxh = dy * gv
            dx_ref[...] = r * (dxh - xh * jnp.mean(dxh * xh, axis=-1, keepdims=True))

    row = pl.BlockSpec((tm, D), lambda i: (i, 0))
    return pl.pallas_call(
        body, name=name, grid=(T // tm,),
        in_specs=[row, pl.BlockSpec((1, D), lambda i: (0, 0)),
                  pl.BlockSpec((tm, D), lambda i: (jnp.maximum(i - nc, 0), 0))],
        out_specs=[pl.BlockSpec((1, 1), lambda i: (0, 0)), row, pl.BlockSpec((1, D), lambda i: (0, 0))],
        out_shape=[jax.ShapeDtypeStruct((1, 1), F32), jax.ShapeDtypeStruct((T, D), F32),
                   jax.ShapeDtypeStruct((1, D), F32)],
        compiler_params=_cparams(("arbitrary",)),
    )(X, g, target)


def _ada_fwd(cond, w, b, name):
    nl, D, Ns = w.shape
    tn = _tile(Ns, 1024, LANES)

    def body(c_ref, w_ref, b_ref, o_ref):
        c = c_ref[...]
        o_ref[0] = _dot(c * _sigmoid(c), w_ref[0], _NN) + b_ref[0]

    return pl.pallas_call(
        body, name=name, grid=(nl, Ns // tn),
        in_specs=[pl.BlockSpec((16, D), lambda l, j: (0, 0)), pl.BlockSpec((1, D, tn), lambda l, j: (l, 0, j)),
                  pl.BlockSpec((1, 1, tn), lambda l, j: (l, 0, j))],
        out_specs=pl.BlockSpec((1, 16, tn), lambda l, j: (l, 0, j)),
        out_shape=jax.ShapeDtypeStruct((nl, 16, Ns), F32),
        compiler_params=_cparams(("parallel", "parallel")),
    )(cond, w, b)


def _ada_bwd(cond, w, dmod, name):
    nl, D, Ns = w.shape
    tn = _tile(Ns, 1024, LANES)

    def body(c_ref, w_ref, dm_ref, dw_ref, da_ref):
        j = pl.program_id(1)

        @pl.when(j == 0)
        def _():
            da_ref[...] = jnp.zeros_like(da_ref)

        c = c_ref[...]
        dw_ref[0] = _dot(c * _sigmoid(c), dm_ref[0], _TN)
        da_ref[0] += _dot(dm_ref[0], w_ref[0], _NT)

    return pl.pallas_call(
        body, name=name, grid=(nl, Ns // tn),
        in_specs=[pl.BlockSpec((16, D), lambda l, j: (0, 0)), pl.BlockSpec((1, D, tn), lambda l, j: (l, 0, j)),
                  pl.BlockSpec((1, 16, tn), lambda l, j: (l, 0, j))],
        out_specs=[pl.BlockSpec((1, D, tn), lambda l, j: (l, 0, j)), pl.BlockSpec((1, 16, D), lambda l, j: (l, 0, 0))],
        out_shape=[jax.ShapeDtypeStruct((nl, D, Ns), F32), jax.ShapeDtypeStruct((nl, 16, D), F32)],
        compiler_params=_cparams(("parallel", "arbitrary")),
    )(cond, w, dmod)


def _cctx_grad(c_ctx, dact4, name):
    nch, nl, _, D = dact4.shape

    def body(c_ref, da_ref, o_ref):
        acc = jnp.zeros((1, D), F32)
        for ch in range(nch):
            for l in range(nl):
                acc = acc + da_ref[ch, l, 8:9, :]
        c = c_ref[...]
        s = _sigmoid(c)
        o_ref[...] = acc * (s + c * s * (1.0 - s))

    return pl.pallas_call(body, name=name, out_shape=jax.ShapeDtypeStruct((1, D), F32),
                          compiler_params=_cparams())(c_ctx, dact4)


def _adamw(w, g, m, v, name):
    R, W = w.shape
    tr = _tile(R, max(8, (1 << 19) // W // 8 * 8), 8)
    c1 = 1.0 - ADAM_B1 ** ADAM_STEP
    c2 = 1.0 - ADAM_B2 ** ADAM_STEP

    def body(w_ref, g_ref, m_ref, v_ref, d_ref, nm_ref, nv_ref):
        g = g_ref[...]
        m = ADAM_B1 * m_ref[...] + (1.0 - ADAM_B1) * g
        v = ADAM_B2 * v_ref[...] + (1.0 - ADAM_B2) * (g * g)
        nm_ref[...] = m
        nv_ref[...] = v
        d_ref[...] = -ADAM_LR * ((m / c1) / (jnp.sqrt(v / c2) + ADAM_EPS) + ADAM_WD * w_ref[...])

    blk = pl.BlockSpec((tr, W), lambda i: (i, 0))
    s = jax.ShapeDtypeStruct((R, W), F32)
    return pl.pallas_call(
        body, name=name, grid=(R // tr,), in_specs=[blk] * 4, out_specs=[blk] * 3, out_shape=[s, s, s],
        compiler_params=_cparams(("parallel",)),
    )(w, g, m, v)


def _adamw_nd(w, g, m, v, name):
    shp = w.shape
    two = (math.prod(shp[:-1]), shp[-1])
    d, nm, nv = _adamw(w.reshape(two), g.reshape(two), m.reshape(two), v.reshape(two), name)
    return d.reshape(shp), nm.reshape(shp), nv.reshape(shp)


def _interleave(a, n):
    return a.reshape(SCAN_SEGMENTS, n // SCAN_SEGMENTS, -1).transpose(1, 0, 2).reshape(n, -1)


def _deinterleave(a, n):
    return a.reshape(n // SCAN_SEGMENTS, SCAN_SEGMENTS, -1).transpose(1, 0, 2).reshape(n, -1)


def _blockdiag_in(bbT, dm):
    eye = jnp.eye(dm.SG, dtype=F32)
    b4 = bbT.reshape(2, SSM_GROUP, dm.SG, SSM_STATE)
    return jnp.einsum("dhgp,gk->dghkp", b4, eye).reshape(2, dm.SSM_W, dm.GP)


def _blockdiag_in_T(dbd, dm):
    gpb = STATE_BLOCK // SSM_STATE
    per = (LANES // SSM_GROUP) // gpb
    d8 = dbd.reshape(2, dm.SSM_W // LANES, per, per, gpb, SSM_GROUP, gpb, SSM_STATE)
    x = jnp.einsum("duaabhbp->duabhp", d8).reshape(2, dm.SG, SSM_GROUP, SSM_STATE)
    return x.transpose(0, 2, 1, 3).reshape(2, SSM_GROUP, dm.GP)


def _blockdiag_out(c, dm):
    eye = jnp.eye(dm.SG, dtype=F32)
    return jnp.einsum("dghp,gk->dgpkh", c, eye).reshape(2, dm.GP, dm.SSM_W)


def _blockdiag_out_T(dcd, dm):
    gpb = STATE_BLOCK // SSM_STATE
    per = (LANES // SSM_GROUP) // gpb
    d8 = dcd.reshape(2, dm.SSM_W // LANES, per, gpb, SSM_STATE, per, gpb, SSM_GROUP)
    return jnp.einsum("duabpabh->duabhp", d8).reshape(2, dm.SG, SSM_GROUP, SSM_STATE)


_BIG = ("w_in", "w_out", "ssm_w_glu", "ffn_w_up", "ffn_w_down")
_SHARD_AXIS = {"w_in": 2, "w_out": 1, "ssm_w_glu": 1, "ffn_w_up": 2, "ffn_w_down": 1}
_SMALL = ("g_mix", "g_ffn", "attn_sink", "ssm_lambda_re", "ssm_lambda_im", "ssm_log_dt", "ssm_b_re", "ssm_b_im",
          "ssm_c_re", "ssm_c_im", "ssm_d", "ssm_b_glu", "gmlp_ln_g", "gmlp_ln_b", "gmlp_w_s", "gmlp_b_s",
          "ffn_conv_b", "g_final", "ffn_conv_w")
_WEIGHTS = ("c_ctx", "w_ada", "b_ada", "g_mix", "g_ffn", "w_in", "w_out", "attn_sink", "ssm_lambda_re",
            "ssm_lambda_im", "ssm_log_dt", "ssm_b_re", "ssm_b_im", "ssm_c_re", "ssm_c_im", "ssm_d", "ssm_w_glu",
            "ssm_b_glu", "gmlp_ln_g", "gmlp_ln_b", "gmlp_w_s", "gmlp_b_s", "ffn_w_up", "ffn_conv_w", "ffn_conv_b",
            "ffn_w_down", "g_final")


def _step(P, M, V, x, c, ctx, loss_target):
    dm = _Dims()
    D, T, C, L = dm.D, dm.T, dm.C, dm.L
    mx, my, mc = lax.axis_index("x"), lax.axis_index("y"), lax.axis_index("c")
    chip = 2 * mx + my
    dev = 2 * chip + mc

    conv_sh = P["ffn_conv_w"].shape
    small_f = _pack([c, P["ffn_conv_w"]], F32, 8)
    small_all = _allgather8(small_f, "ag_small_fwd")
    c_all, conv_all = [], []
    for d in range(N_DEV):
        cd_, cw_ = _unpack(small_all[d], [(1, D), conv_sh])
        c_all.append(cd_)
        conv_all.append(cw_)
    conv_w = jnp.concatenate([conv_all[2 * j] for j in range(N_CHIPS)], axis=2)
    cond = jnp.concatenate(c_all + [P["c_ctx"].reshape(1, D), jnp.zeros((16 - N_DEV - 1, D), F32)], axis=0)

    n_sh = P["w_ada"].shape[2]
    b_sh = lax.dynamic_slice_in_dim(P["b_ada"], chip * n_sh, n_sh, axis=1)[:, None, :]
    mods_sh = _ada_fwd(cond, P["w_ada"], b_sh, "ada_fwd")
    mods4 = _comm(mods_sh, group="chips", by_peer=False, name="ag_mods")
    mods = jnp.concatenate([mods4[j] for j in range(N_CHIPS)], axis=-1)
    mod_lat = lax.dynamic_index_in_dim(mods, dev, axis=1, keepdims=False)
    mod_ctx = mods[:, N_DEV]
    modp = jnp.stack([mod_ctx, mod_lat], axis=1).reshape(DEPTH, 2, N_MOD, D)

    LH = DEPTH // 2
    half_shapes = [(LH,) + P[n].shape[1:] for n in _BIG]
    w_half = _pack([lax.dynamic_slice_in_dim(P[n], mc * LH, LH, axis=0) for n in _BIG], BF16, 16)
    RH = w_half.shape[0]
    w4 = _comm(w_half, group="chips", by_peer=False, name="ag_w_chips")
    w24 = _comm(w4, group="sib", by_peer=False, name="ag_w_sib")
    pieces = [[_unpack(w24[hh, j], half_shapes) for j in range(N_CHIPS)] for hh in range(2)]

    def full_weight(n, l, chips=range(N_CHIPS)):
        t = _BIG.index(n)
        return jnp.concatenate([pieces[l // LH][j][t][l % LH] for j in chips], axis=_SHARD_AXIS[n] - 1)

    cos, sin = _rope_tables(dm)
    Wl = [dict(w_in=full_weight("w_in", l), w_out=full_weight("w_out", l), w_glu=full_weight("ssm_w_glu", l),
               w_g=full_weight("ffn_w_up", l, (0, 1)), w_v=full_weight("ffn_w_up", l, (2, 3)),
               w_dn=full_weight("ffn_w_down", l)) for l in range(DEPTH)]

    X = jnp.concatenate([ctx.reshape(C, D), x.reshape(L, D)], axis=0)
    saved = []
    for l in range(DEPTH):
        mp = modp[l]
        sh_a, sc_a, gt_a, sh_f, sc_f, gt_f = [mp[:, k] for k in range(N_MOD)]
        w_in, w_out, w_glu = Wl[l]["w_in"], Wl[l]["w_out"], Wl[l]["w_glu"]
        w_g, w_v, w_dn = Wl[l]["w_g"], Wl[l]["w_v"], Wl[l]["w_dn"]
        g_mix, g_ffn = P["g_mix"][l][None], P["g_ffn"][l][None]

        h = _normmod(X, g_mix, sh_a, sc_a, C, "normmod_a")
        z = _matmul(h, w_in, mode="nn", name="mm_in")
        q, k2, v2 = _attn_prep(z, cos, sin, dm, "attn_prep")
        sinkcol = jnp.repeat(P["attn_sink"][l].reshape(2, GQA_RATIO), ATTN_BLOCK, axis=1)[..., None]
        o_attn = _attention(q, k2, v2, sinkcol, dm, "attention")
        u = z[:, dm.OFF_S:dm.OFF_GU]
        ul, uc = _interleave(u[C:], L), _interleave(u[:C], C)
        lam_re = P["ssm_lambda_re"][l].reshape(2, 1, dm.GP)
        lam_im = P["ssm_lambda_im"][l].reshape(2, 1, dm.GP)
        ldt = jnp.repeat(P["ssm_log_dt"][l], SSM_STATE, axis=-1).reshape(2, 1, dm.GP)
        bT_re = P["ssm_b_re"][l].reshape(2, dm.GP, SSM_GROUP).transpose(0, 2, 1)
        bT_im = P["ssm_b_im"][l].reshape(2, dm.GP, SSM_GROUP).transpose(0, 2, 1)
        a_re, a_im, bbT_re, bbT_im = _s5_disc(lam_re, lam_im, ldt, bT_re, bT_im, "s5_disc")
        bd_re, bd_im = _blockdiag_in(bbT_re, dm).astype(BF16), _blockdiag_in(bbT_im, dm).astype(BF16)
        cd_re = _blockdiag_out(P["ssm_c_re"][l], dm).astype(BF16)
        cd_im = _blockdiag_out(P["ssm_c_im"][l], dm).astype(BF16)
        dskip = P["ssm_d"][l][None]
        yl, yc = _s5_scan(ul, uc, a_re, a_im, bd_re, bd_im, cd_re, cd_im, dskip, dm, "s5_scan")
        y = jnp.concatenate([_deinterleave(yc, C), _deinterleave(yl, L)], axis=0)
        b_glu = P["ssm_b_glu"][l][None]
        o_ssm = _glu(y, w_glu, b_glu, "glu")
        ln_g, ln_b = P["gmlp_ln_g"][l][None], P["gmlp_ln_b"][l][None]
        w_s = P["gmlp_w_s"][l].astype(BF16)
        b_s = P["gmlp_b_s"][l][..., None]
        o_gmlp = _gmlp(z, ln_g, ln_b, w_s, b_s, dm, "gmlp")
        o_cat = jnp.concatenate([o_attn, o_ssm, o_gmlp], axis=-1)
        mix, X1 = _matmul_residual(o_cat, w_out, X, gt_a, C, "mm_out")
        h2 = _normmod(X1, g_ffn, sh_f, sc_f, C, "normmod_f")
        upg = _matmul(h2, w_g, mode="nn", name="mm_up")
        upv = _matmul(h2, w_v, mode="nn", name="mm_up")
        cw, cb = conv_w[l], P["ffn_conv_b"][l][None]
        act = _convact(upg, upv, cw, cb, C, "convact")
        ffn, X2 = _matmul_residual(act, w_dn, X1, gt_f, C, "mm_down")
        saved.append(dict(X=X, h=h, z=z, q=q, k2=k2, v2=v2, sinkcol=sinkcol, ul=ul, uc=uc, a_re=a_re, a_im=a_im,
                          bd_re=bd_re, bd_im=bd_im, cd_re=cd_re, cd_im=cd_im, y=y, o_cat=o_cat, mix=mix, X1=X1, h2=h2,
                          upg=upg, upv=upv, act=act, ffn=ffn, lam_re=lam_re, lam_im=lam_im, ldt=ldt, bT_re=bT_re,
                          bT_im=bT_im))
        X = X2

    loss_l, dX, dg_final = _loss_head(X, P["g_final"][None], loss_target.reshape(L, D), C, "loss_head")
    loss = lax.psum(loss_l[0, 0], ("x", "y", "c"))

    G = {n: [None] * DEPTH for n in _WEIGHTS if n not in ("c_ctx", "w_ada", "b_ada", "g_final")}
    dmod = [None] * DEPTH
    for l in reversed(range(DEPTH)):
        S = saved[l]
        mp = modp[l]
        sh_a, sc_a, gt_a, sh_f, sc_f, gt_f = [mp[:, k] for k in range(N_MOD)]
        w_in, w_out, w_glu = Wl[l]["w_in"], Wl[l]["w_out"], Wl[l]["w_glu"]
        w_g, w_v, w_dn = Wl[l]["w_g"], Wl[l]["w_v"], Wl[l]["w_dn"]
        g_mix, g_ffn = P["g_mix"][l][None], P["g_ffn"][l][None]
        cw, cb = conv_w[l], P["ffn_conv_b"][l][None]

        dffn, dgt_f = _gate_bwd(dX, S["ffn"], gt_f, C, "gate_bwd")
        dact = _matmul(dffn, w_dn, mode="nt", name="mm_down_dx")
        G["ffn_w_down"][l] = _matmul(S["act"], dffn, mode="tn", name="mm_down_dw")
        dupg, dupv, dcw, dcb = _convact_bwd(S["upg"], S["upv"], cw, cb, dact, C, "convact_bwd")
        G["ffn_conv_w"][l], G["ffn_conv_b"][l] = dcw, dcb[0]
        dh2 = _matmul(dupg, w_g, mode="nt", name="mm_up_dx")
        dh2 = _matmul(dupv, w_v, mode="nt", name="mm_up_dx2", add=dh2)
        G["ffn_w_up"][l] = (_matmul(S["h2"], dupg, mode="tn", name="mm_up_dw"),
                            _matmul(S["h2"], dupv, mode="tn", name="mm_up_dw"))
        dX1, dg_f, dsh_f, dsc_f = _normmod_bwd(S["X1"], g_ffn, sc_f, dh2, dX, C, "normmod_bwd")
        G["g_ffn"][l] = dg_f[0]
        dmix, dgt_a = _gate_bwd(dX1, S["mix"], gt_a, C, "gate_bwd")
        do = _matmul(dmix, w_out, mode="nt", name="mm_out_dx")
        G["w_out"][l] = _matmul(S["o_cat"], dmix, mode="tn", name="mm_out_dw")
        do_attn = do_ssm = do_gmlp = do
        ln_g, ln_b = P["gmlp_ln_g"][l][None], P["gmlp_ln_b"][l][None]
        w_s = P["gmlp_w_s"][l].astype(BF16)
        b_s = P["gmlp_b_s"][l][..., None]
        dz_g, dlg, dlb, dws, dbs = _gmlp_bwd(S["z"], ln_g, ln_b, w_s, b_s, do_gmlp, dm, "gmlp_bwd")
        G["gmlp_ln_g"][l], G["gmlp_ln_b"][l], G["gmlp_w_s"][l], G["gmlp_b_s"][l] = dlg[0], dlb[0], dws, dbs[..., 0]
        b_glu = P["ssm_b_glu"][l][None]
        dy, dwglu, dbglu = _glu_bwd(S["y"], w_glu, b_glu, do_ssm, dm.ATTN_W // dm.SSM_W, "glu_bwd")
        G["ssm_w_glu"][l], G["ssm_b_glu"][l] = dwglu, dbglu[0]
        dyl, dyc = _interleave(dy[C:], L), _interleave(dy[:C], C)
        tr = lambda a: jnp.swapaxes(a, 1, 2)
        dskip = P["ssm_d"][l][None]
        (dul, duc, da_re, da_im, dbd_re, dbd_im, dcd_re, dcd_im, dd) = _s5_scan_bwd(
            S["ul"], S["uc"], dyl, dyc, S["a_re"], S["a_im"], S["bd_re"], S["bd_im"], S["cd_re"], S["cd_im"],
            tr(S["bd_re"]), tr(S["bd_im"]), tr(S["cd_re"]), tr(S["cd_im"]), dskip, dm, "s5_scan_bwd")
        dz_s = jnp.concatenate([_deinterleave(duc, C), _deinterleave(dul, L)], axis=0)
        dlr, dli, dldt, dbTr, dbTi = _s5_disc_bwd(S["lam_re"], S["lam_im"], S["ldt"], S["bT_re"], S["bT_im"],
                                                  da_re, da_im, _blockdiag_in_T(dbd_re, dm),
                                                  _blockdiag_in_T(dbd_im, dm), "s5_disc_bwd")
        G["ssm_lambda_re"][l] = dlr.reshape(2, dm.SG, SSM_STATE)
        G["ssm_lambda_im"][l] = dli.reshape(2, dm.SG, SSM_STATE)
        G["ssm_log_dt"][l] = jnp.sum(dldt.reshape(2, dm.SG, SSM_STATE), axis=-1)
        G["ssm_b_re"][l] = dbTr.transpose(0, 2, 1).reshape(2, dm.SG, SSM_STATE, SSM_GROUP)
        G["ssm_b_im"][l] = dbTi.transpose(0, 2, 1).reshape(2, dm.SG, SSM_STATE, SSM_GROUP)
        G["ssm_c_re"][l] = _blockdiag_out_T(dcd_re, dm)
        G["ssm_c_im"][l] = _blockdiag_out_T(dcd_im, dm)
        G["ssm_d"][l] = dd[0]
        dq, pk, pv, dkc, dvc, dsk = _attention_bwd(S["q"], S["k2"], S["v2"], S["sinkcol"], do_attn, dm, "attention_bwd")
        G["attn_sink"][l] = dsk[:, :, 0].reshape(dm.NH)
        dk2 = _kv_reduce(pk, dkc, dm, "kv_reduce")
        dv2 = _kv_reduce(pv, dvc, dm, "kv_reduce")
        dz_a = _attn_prep_bwd(dq, dk2, dv2, cos, sin, dm, "attn_prep_bwd")
        dz = jnp.concatenate([dz_a, dz_s, dz_g], axis=-1).astype(BF16)
        dh = _matmul(dz, w_in, mode="nt", name="mm_in_dx")
        G["w_in"][l] = _matmul(S["h"], dz, mode="tn", name="mm_in_dw")
        dX, dg_a, dsh_a, dsc_a = _normmod_bwd(S["X"], g_mix, sc_a, dh, dX1, C, "normmod_bwd")
        G["g_mix"][l] = dg_a[0]
        dmod[l] = jnp.stack([dsh_a, dsc_a, dgt_a, dsh_f, dsc_f, dgt_f], axis=1)

    grad_x = dX[C:].reshape(1, L, D)
    Gbig = {n: G.pop(n) for n in _BIG}
    G = {n: jnp.stack(v) for n, v in G.items()}
    G["g_final"] = dg_final[0]
    dmod = jnp.stack(dmod)

    small_shapes = [G[n].shape for n in _SMALL]
    n_small = sum(math.prod(s) for s in small_shapes)
    r_small = -(-n_small // (8 * PACK_W)) * 8
    gs_pack = _pack([G[n] for n in _SMALL], F32, 8)
    dm_pack = _pack([dmod], F32, 8)
    r_dm = dm_pack.shape[0]
    allp = _allgather8(jnp.concatenate([gs_pack, dm_pack], axis=0), "ag_small_bwd")
    gs_sum = _sum_slots(allp[:, :r_small], F32, "sum_small")
    for n, a in zip(_SMALL, _unpack(gs_sum, small_shapes)):
        G[n] = a
    dmod_all = allp[:, r_small:r_small + r_dm].reshape(N_DEV, -1)[:, :DEPTH * 2 * N_MOD * D]
    dmod_all = dmod_all.reshape(N_DEV, DEPTH, 2, N_MOD * D)
    dmod_ctx = _sum_slots(dmod_all[:, :, 0].reshape(N_DEV, DEPTH, N_MOD * D), F32, "sum_dmod_ctx")
    dmod16 = jnp.concatenate([jnp.swapaxes(dmod_all[:, :, 1], 0, 1), dmod_ctx[:, None],
                              jnp.zeros((DEPTH, 16 - N_DEV - 1, N_MOD * D), F32)], axis=1)
    G["b_ada"] = _sum_slots(jnp.swapaxes(dmod16, 0, 1)[:N_DEV + 1], F32, "sum_b_ada")
    dmod16_sh = lax.dynamic_slice_in_dim(dmod16, chip * n_sh, n_sh, axis=2)
    G["w_ada"], dact = _ada_bwd(cond, P["w_ada"], dmod16_sh, "ada_bwd")
    dact4 = _comm(dact, group="chips", by_peer=False, name="ag_dact")
    G["c_ctx"] = _cctx_grad(P["c_ctx"].reshape(1, D), dact4, "cctx_grad")[0]
    conv_cols = conv_sh[2]
    G["ffn_conv_w"] = lax.dynamic_slice_in_dim(G["ffn_conv_w"], chip * conv_cols, conv_cols, axis=2)

    def shard_of(n, l, j):
        a = Gbig[n][l]
        if n == "ffn_w_up":
            a, j = a[j // 2], j % 2
            k = a.shape[1] // 2
            return lax.slice_in_dim(a, j * k, (j + 1) * k, axis=1)
        ax = _SHARD_AXIS[n] - 1
        k = a.shape[ax] // N_CHIPS
        return lax.slice_in_dim(a, j * k, (j + 1) * k, axis=ax)

    gp = jnp.stack([jnp.stack([_pack([shard_of(n, hh * LH + li, j) for n in _BIG for li in range(LH)], BF16, 16)
                               for j in range(N_CHIPS)]) for hh in range(2)])
    pair = _comm(gp.reshape(2, N_CHIPS * RH, PACK_W), group="sib", by_peer=True, name="rs_pair")
    psum = _sum_slots(pair, BF16, "sum_pair").reshape(N_CHIPS, RH, PACK_W)
    quad = _comm(psum, group="chips", by_peer=True, name="rs_chips")
    red = _sum_slots(quad, F32, "sum_chips")
    both = _comm(red, group="sib", by_peer=False, name="rs_share")
    halves = [_unpack(both[hh], half_shapes) for hh in range(2)]
    for t, n in enumerate(_BIG):
        G[n] = jnp.concatenate([halves[0][t], halves[1][t]], axis=0)

    delta, new_m, new_v = {}, {}, {}
    for n in _BIG + ("w_ada",):
        delta[n], new_m[n], new_v[n] = _adamw_nd(P[n], G[n], M[n], V[n], "adamw_" + n)
    rest = [n for n in _WEIGHTS if n not in _BIG + ("w_ada",)]
    shapes = [P[n].shape for n in rest]
    packed = [_pack([d[n] for n in rest], F32, 8) for d in (P, G, M, V)]
    outs = _adamw(*packed, "adamw_small")
    for dst, buf in zip((delta, new_m, new_v), outs):
        for n, a in zip(rest, _unpack(buf, shapes)):
            dst[n] = a
    return loss, grad_x, G, delta, new_m, new_v


def kernel(x, c, ctx, c_ctx, w_ada, b_ada, g_mix, g_ffn, w_in, w_out, attn_sink, ssm_lambda_re, ssm_lambda_im, ssm_log_dt, ssm_b_re, ssm_b_im, ssm_c_re, ssm_c_im, ssm_d, ssm_w_glu, ssm_b_glu, gmlp_ln_g, gmlp_ln_b, gmlp_w_s, gmlp_b_s, ffn_w_up, ffn_conv_w, ffn_conv_b, ffn_w_down, g_final, loss_target, m_c_ctx, m_w_ada, m_b_ada, m_g_mix, m_g_ffn, m_w_in, m_w_out, m_attn_sink, m_ssm_lambda_re, m_ssm_lambda_im, m_ssm_log_dt, m_ssm_b_re, m_ssm_b_im, m_ssm_c_re, m_ssm_c_im, m_ssm_d, m_ssm_w_glu, m_ssm_b_glu, m_gmlp_ln_g, m_gmlp_ln_b, m_gmlp_w_s, m_gmlp_b_s, m_ffn_w_up, m_ffn_conv_w, m_ffn_conv_b, m_ffn_w_down, m_g_final, v_c_ctx, v_w_ada, v_b_ada, v_g_mix, v_g_ffn, v_w_in, v_w_out, v_attn_sink, v_ssm_lambda_re, v_ssm_lambda_im, v_ssm_log_dt, v_ssm_b_re, v_ssm_b_im, v_ssm_c_re, v_ssm_c_im, v_ssm_d, v_ssm_w_glu, v_ssm_b_glu, v_gmlp_ln_g, v_gmlp_ln_b, v_gmlp_w_s, v_gmlp_b_s, v_ffn_w_up, v_ffn_conv_w, v_ffn_conv_b, v_ffn_w_down, v_g_final):
    env = locals()
    P = {n: env[n] for n in _WEIGHTS}
    M = {n: env["m_" + n] for n in _WEIGHTS}
    V = {n: env["v_" + n] for n in _WEIGHTS}
    loss, grad_x, G, delta, new_m, new_v = _step(P, M, V, x, c, ctx, loss_target)
    return (loss, grad_x, *[G[n] for n in _WEIGHTS], *[delta[n] for n in _WEIGHTS],
            *[new_m[n] for n in _WEIGHTS], *[new_v[n] for n in _WEIGHTS])
```

```python
import functools
import math

import jax
import jax.numpy as jnp
from jax import lax
from jax.experimental import pallas as pl
from jax.experimental.pallas import tpu as pltpu

F32 = jnp.float32
BF16 = jnp.bfloat16

D_MODEL = 2048
SEQ = 4096
DEPTH = 4
CTX_LEN = 256
GRID_W = 64
HEAD_DIM = 64
GQA_RATIO = 8
WINDOW = 128
ATTN_BLOCK = 128
ROPE_BASE = 10000.0
SSM_GROUP = 16
SSM_STATE = 64
GMLP_CHUNK = 128
GMLP_GROUP_W = 128
N_MOD = 6
NORM_EPS = 1e-6
ADAM_LR = 0.001
ADAM_B1 = 0.9
ADAM_B2 = 0.999
ADAM_EPS = 1e-08
ADAM_WD = 0.01
ADAM_STEP = 10

N_CHIPS = 4
N_DEV = 8
SCAN_SEGMENTS = 8
STATE_BLOCK = 256
LANES = 128
PACK_W = 1024
COMM_CHUNKS = (16, 8, 4, 2)
STAGE_BYTES = 4 << 20
VMEM_LIMIT_MB = 56
_GELU_K = math.sqrt(2.0 / math.pi)
_GELU_C = 0.044715


class _Dims:
    def __init__(self):
        self.D = D_MODEL
        self.L = SEQ
        self.C = CTX_LEN
        self.T = SEQ + CTX_LEN
        self.ATTN_W = D_MODEL // 2
        self.SSM_W = D_MODEL // 4
        self.GMLP_W = D_MODEL - self.ATTN_W - self.SSM_W
        self.NH = self.ATTN_W // HEAD_DIM
        self.NKV = self.NH // GQA_RATIO
        self.KV_W = self.NKV * HEAD_DIM
        self.SG = self.SSM_W // SSM_GROUP
        self.GP = self.SG * SSM_STATE
        self.GG = self.GMLP_W // GMLP_GROUP_W
        self.DFF = ((8 * D_MODEL // 3 + 255) // 256) * 256
        self.OFF_K = self.ATTN_W
        self.OFF_V = self.OFF_K + self.KV_W
        self.OFF_S = self.OFF_V + self.KV_W
        self.OFF_GU = self.OFF_S + self.SSM_W
        self.OFF_GV = self.OFF_GU + self.GMLP_W
        self.N_IN = self.OFF_GV + self.GMLP_W
        assert self.NKV == 2 and self.KV_W == LANES
        assert self.C % (SCAN_SEGMENTS * 8) == 0 and self.L % (SCAN_SEGMENTS * 8) == 0
        assert self.C % ATTN_BLOCK == 0 and self.L % ATTN_BLOCK == 0


def _tile(n, cap, mult):
    best = None
    for t in range(mult, min(n, cap) + 1, mult):
        if n % t == 0:
            best = t
    return best if best is not None else n


def _cparams(sem=None):
    kw = dict(vmem_limit_bytes=VMEM_LIMIT_MB << 20)
    if sem is not None:
        kw["dimension_semantics"] = sem
    return pltpu.CompilerParams(**kw)


def _gelu(x):
    return 0.5 * x * (1.0 + jnp.tanh(_GELU_K * (x + _GELU_C * x * x * x)))


def _gelu_grad(x):
    t = jnp.tanh(_GELU_K * (x + _GELU_C * x * x * x))
    return 0.5 * (1.0 + t) + 0.5 * x * (1.0 - t * t) * _GELU_K * (1.0 + 3.0 * _GELU_C * x * x)


def _sigmoid(x):
    return 1.0 / (1.0 + jnp.exp(-x))


def _dot(a, b, dims):
    return lax.dot_general(a.astype(BF16), b.astype(BF16), (dims, ((), ())), preferred_element_type=F32)


_NN = ((1,), (0,))
_NT = ((1,), (1,))
_TN = ((0,), (0,))


def _sib_staged(x, by_peer, name):
    rows, W = x.shape[-2:]
    cr = _tile(rows, max(16, STAGE_BYTES // (W * x.dtype.itemsize) // 16 * 16), 16)
    nch = rows // cr

    def body(x_ref, o_ref, sbuf, rbuf, send_sems, recv_sems, ld_sem, st_sem, own_sem, credit_sem):
        mx, my, mc = lax.axis_index("x"), lax.axis_index("y"), lax.axis_index("c")
        sib = (mx, my, 1 - mc)
        mine = x_ref.at[mc] if by_peer else x_ref
        theirs = x_ref.at[1 - mc] if by_peer else x_ref
        own = pltpu.make_async_copy(mine, o_ref.at[mc], own_sem)
        own.start()

        def step(i, carry):
            s = i % 2
            r0 = pl.multiple_of(i * cr, 16)
            ld = pltpu.make_async_copy(theirs.at[pl.ds(r0, cr), :], sbuf.at[s], ld_sem)
            ld.start()
            ld.wait()

            @pl.when(i >= 2)
            def _():
                pl.semaphore_wait(credit_sem, 1)

            cp = pltpu.make_async_remote_copy(src_ref=sbuf.at[s], dst_ref=rbuf.at[s], send_sem=send_sems.at[s],
                                              recv_sem=recv_sems.at[s], device_id=sib,
                                              device_id_type=pl.DeviceIdType.MESH)
            cp.start()
            cp.wait_recv()
            st = pltpu.make_async_copy(rbuf.at[s], o_ref.at[1 - mc, pl.ds(r0, cr), :], st_sem)
            st.start()
            st.wait()

            @pl.when(i < nch - 2)
            def _():
                pl.semaphore_signal(credit_sem, inc=1, device_id=sib, device_id_type=pl.DeviceIdType.MESH)

            cp.wait_send()
            return carry

        lax.fori_loop(0, nch, step, 0)
        own.wait()

    return pl.pallas_call(
        body, name=name,
        out_shape=jax.ShapeDtypeStruct((2, rows, W), x.dtype),
        in_specs=[pl.BlockSpec(memory_space=pl.ANY)],
        out_specs=pl.BlockSpec(memory_space=pl.ANY),
        scratch_shapes=[pltpu.VMEM((2, cr, W), x.dtype), pltpu.VMEM((2, cr, W), x.dtype),
                        pltpu.SemaphoreType.DMA((2,)), pltpu.SemaphoreType.DMA((2,)),
                        pltpu.SemaphoreType.DMA, pltpu.SemaphoreType.DMA, pltpu.SemaphoreType.DMA,
                        pltpu.SemaphoreType.REGULAR],
        compiler_params=_cparams(),
    )(x)


def _comm(x, *, group, by_peer, name):
    n = {"sib": 2, "chips": N_CHIPS, "all": N_DEV}[group]
    blk = x.shape[1:] if by_peer else x.shape
    npeer = n - 1
    W = blk[-1]
    rows = math.prod(blk[:-1])
    x2 = x.reshape((n, rows, W) if by_peer else (rows, W))
    if group == "sib":
        return _sib_staged(x2, by_peer, name).reshape((n,) + tuple(blk))
    nchunk = next((k for k in COMM_CHUNKS if rows % (k * 16) == 0), 1)
    cr = rows // nchunk

    def body(x_ref, o_ref, send_sems, recv_sems, local_sem):
        mx, my, mc = lax.axis_index("x"), lax.axis_index("y"), lax.axis_index("c")
        if group == "chips":
            me = 2 * mx + my
            peers = [((mx ^ fx, my ^ fy, mc), 2 * (mx ^ fx) + (my ^ fy)) for fx, fy in ((1, 0), (0, 1), (1, 1))]
        else:
            me = 4 * mx + 2 * my + mc
            peers = [((mx ^ fx, my ^ fy, mc ^ fc), 4 * (mx ^ fx) + 2 * (my ^ fy) + (mc ^ fc))
                     for fx in (0, 1) for fy in (0, 1) for fc in (0, 1) if fx + fy + fc]

        def src_of(pid):
            return x_ref.at[pid] if by_peer else x_ref

        def chunk(ref, ch):
            return ref.at[pl.ds(ch * cr, cr), :]

        def remote(k, dev, pid, ch=None):
            s, d = src_of(pid), o_ref.at[me]
            if ch is not None:
                s, d = chunk(s, ch), chunk(d, ch)
            return pltpu.make_async_remote_copy(src_ref=s, dst_ref=d, send_sem=send_sems.at[k],
                                                recv_sem=recv_sems.at[k], device_id=dev,
                                                device_id_type=pl.DeviceIdType.MESH)

        for ch in range(nchunk):
            pltpu.make_async_copy(chunk(src_of(me), ch), chunk(o_ref.at[me], ch), local_sem).start()
            for k, (dev, pid) in enumerate(peers):
                remote(k, dev, pid, ch).start()
        for k, (dev, pid) in enumerate(peers):
            remote(k, dev, pid).wait_recv()
        for k, (dev, pid) in enumerate(peers):
            remote(k, dev, pid).wait_send()
        pltpu.make_async_copy(src_of(me), o_ref.at[me], local_sem).wait()

    out = pl.pallas_call(
        body, name=name,
        out_shape=jax.ShapeDtypeStruct((n, rows, W), x.dtype),
        in_specs=[pl.BlockSpec(memory_space=pl.ANY)],
        out_specs=pl.BlockSpec(memory_space=pl.ANY),
        scratch_shapes=[pltpu.SemaphoreType.DMA((npeer,)), pltpu.SemaphoreType.DMA((npeer,)),
                        pltpu.SemaphoreType.DMA],
    )(x2)
    return out.reshape((n,) + tuple(blk))


def _allgather8(x, name):
    return _comm(x, group="all", by_peer=False, name=name)


def _sum_slots(x, out_dtype, name):
    n, R, W = x.shape
    tr = _tile(R, 512, 16)

    def body(x_ref, o_ref):
        acc = x_ref[0].astype(F32)
        for s in range(1, n):
            acc = acc + x_ref[s].astype(F32)
        o_ref[...] = acc.astype(o_ref.dtype)

    return pl.pallas_call(
        body, name=name, grid=(R // tr,),
        in_specs=[pl.BlockSpec((n, tr, W), lambda i: (0, i, 0))],
        out_specs=pl.BlockSpec((tr, W), lambda i: (i, 0)),
        out_shape=jax.ShapeDtypeStruct((R, W), out_dtype),
        compiler_params=_cparams(("parallel",)),
    )(x)


def _pack(arrs, dtype, row_mult):
    flat = jnp.concatenate([a.reshape(-1).astype(dtype) for a in arrs])
    n = flat.shape[0]
    quant = row_mult * PACK_W
    total = -(-n // quant) * quant
    if total != n:
        flat = jnp.concatenate([flat, jnp.zeros((total - n,), dtype)])
    return flat.reshape(total // PACK_W, PACK_W)


def _unpack(buf, shapes):
    flat = buf.reshape(-1)
    out, off = [], 0
    for s in shapes:
        n = math.prod(s)
        out.append(flat[off:off + n].reshape(s))
        off += n
    return out


def _matmul(a, b, *, mode, name, out_dtype=F32, add=None, tm_cap=1088, tn_cap=1536, tk_cap=2048):
    if mode == "nn":
        (M, K), (_, N) = a.shape, b.shape
    elif mode == "nt":
        (M, K), (N, _) = a.shape, b.shape
    else:
        (K, M), (_, N) = a.shape, b.shape
    tm = _tile(M, tm_cap, 16 if mode != "tn" else LANES)
    tn = _tile(N, tn_cap, LANES)
    tk = _tile(K, tk_cap, LANES if mode != "tn" else 16)
    nk = K // tk
    dims = {"nn": _NN, "nt": _NT, "tn": _TN}[mode]

    def body(*refs):
        if add is None:
            a_ref, b_ref, o_ref, acc_ref = refs
            add_ref = None
        else:
            a_ref, b_ref, add_ref, o_ref, acc_ref = refs
        k = pl.program_id(2)

        @pl.when(k == 0)
        def _():
            acc_ref[...] = jnp.zeros_like(acc_ref)

        acc_ref[...] += _dot(a_ref[...], b_ref[...], dims)

        @pl.when(k == nk - 1)
        def _():
            r = acc_ref[...]
            if add_ref is not None:
                r = r + add_ref[...].astype(F32)
            o_ref[...] = r.astype(o_ref.dtype)

    if mode == "nn":
        a_spec = pl.BlockSpec((tm, tk), lambda i, j, k: (i, k))
        b_spec = pl.BlockSpec((tk, tn), lambda i, j, k: (k, j))
    elif mode == "nt":
        a_spec = pl.BlockSpec((tm, tk), lambda i, j, k: (i, k))
        b_spec = pl.BlockSpec((tn, tk), lambda i, j, k: (j, k))
    else:
        a_spec = pl.BlockSpec((tk, tm), lambda i, j, k: (k, i))
        b_spec = pl.BlockSpec((tk, tn), lambda i, j, k: (k, j))
    o_spec = pl.BlockSpec((tm, tn), lambda i, j, k: (i, j))
    in_specs = [a_spec, b_spec] + ([o_spec] if add is not None else [])
    args = (a, b) + ((add,) if add is not None else ())
    return pl.pallas_call(
        body, name=name, grid=(M // tm, N // tn, nk),
        in_specs=in_specs, out_specs=o_spec,
        out_shape=jax.ShapeDtypeStruct((M, N), out_dtype),
        scratch_shapes=[pltpu.VMEM((tm, tn), F32)],
        compiler_params=_cparams(("parallel", "parallel", "arbitrary")),
    )(*args)


def _matmul_residual(a, w, res, gate, C, name):
    M, K = a.shape
    N = w.shape[1]
    tm = _tile(M, 1088, 16)
    tn = _tile(N, 1024, LANES)
    tk = _tile(K, 2048, LANES)
    nk = K // tk

    def body(a_ref, w_ref, res_ref, gate_ref, mix_ref, o_ref, acc_ref):
        i, k = pl.program_id(0), pl.program_id(2)

        @pl.when(k == 0)
        def _():
            acc_ref[...] = jnp.zeros_like(acc_ref)

        acc_ref[...] += _dot(a_ref[...], w_ref[...], _NN)

        @pl.when(k == nk - 1)
        def _():
            r = acc_ref[...]
            row = i * tm + lax.broadcasted_iota(jnp.int32, (tm, 1), 0)
            g = jnp.where(row < C, gate_ref[0:1, :], gate_ref[1:2, :])
            mix_ref[...] = r
            o_ref[...] = res_ref[...] + g * r

    o_spec = pl.BlockSpec((tm, tn), lambda i, j, k: (i, j))
    return pl.pallas_call(
        body, name=name, grid=(M // tm, N // tn, nk),
        in_specs=[pl.BlockSpec((tm, tk), lambda i, j, k: (i, k)),
                  pl.BlockSpec((tk, tn), lambda i, j, k: (k, j)),
                  o_spec,
                  pl.BlockSpec((2, tn), lambda i, j, k: (0, j))],
        out_specs=[o_spec, o_spec],
        out_shape=[jax.ShapeDtypeStruct((M, N), F32), jax.ShapeDtypeStruct((M, N), F32)],
        scratch_shapes=[pltpu.VMEM((tm, tn), F32)],
        compiler_params=_cparams(("parallel", "parallel", "arbitrary")),
    )(a, w, res, gate)


def _seg_select(i, tm, C, ref):
    row = i * tm + lax.broadcasted_iota(jnp.int32, (tm, 1), 0)
    return row < C, jnp.where(row < C, ref[0:1, :], ref[1:2, :])


def _normmod(X, g, shift, scale, C, name):
    T, D = X.shape
    tm = _tile(T, 512, 16)

    def body(x_ref, g_ref, sh_ref, sc_ref, h_ref):
        i = pl.program_id(0)
        x = x_ref[...]
        r = lax.rsqrt(jnp.mean(x * x, axis=-1, keepdims=True) + NORM_EPS)
        _, sh = _seg_select(i, tm, C, sh_ref)
        _, sc = _seg_select(i, tm, C, sc_ref)
        h_ref[...] = ((x * r) * g_ref[...] * (1.0 + sc) + sh).astype(BF16)

    row = pl.BlockSpec((tm, D), lambda i: (i, 0))
    vec = pl.BlockSpec((1, D), lambda i: (0, 0))
    two = pl.BlockSpec((2, D), lambda i: (0, 0))
    return pl.pallas_call(
        body, name=name, grid=(T // tm,),
        in_specs=[row, vec, two, two], out_specs=row,
        out_shape=jax.ShapeDtypeStruct((T, D), BF16),
        compiler_params=_cparams(("parallel",)),
    )(X, g, shift, scale)


def _normmod_bwd(X, g, scale, dh, dres, C, name):
    T, D = X.shape
    tm = _tile(T, 512, 16)

    def body(x_ref, g_ref, sc_ref, dh_ref, dres_ref, dx_ref, dg_ref, dsh_ref, dsc_ref):
        i = pl.program_id(0)

        @pl.when(i == 0)
        def _():
            dg_ref[...] = jnp.zeros_like(dg_ref)
            dsh_ref[...] = jnp.zeros_like(dsh_ref)
            dsc_ref[...] = jnp.zeros_like(dsc_ref)

        x = x_ref[...]
        dh = dh_ref[...]
        gv = g_ref[...]
        r = lax.rsqrt(jnp.mean(x * x, axis=-1, keepdims=True) + NORM_EPS)
        xh = x * r
        isc, sc = _seg_select(i, tm, C, sc_ref)
        n = xh * gv
        dhn = dh * n
        zero = jnp.zeros_like(dh)
        dsh_ref[0:1, :] += jnp.sum(jnp.where(isc, dh, zero), axis=0, keepdims=True)
        dsh_ref[1:2, :] += jnp.sum(jnp.where(isc, zero, dh), axis=0, keepdims=True)
        dsc_ref[0:1, :] += jnp.sum(jnp.where(isc, dhn, zero), axis=0, keepdims=True)
        dsc_ref[1:2, :] += jnp.sum(jnp.where(isc, zero, dhn), axis=0, keepdims=True)
        dn = dh * (1.0 + sc)
        dg_ref[...] += jnp.sum(dn * xh, axis=0, keepdims=True)
        dxh = dn * gv
        dx_ref[...] = dres_ref[...] + r * (dxh - xh * jnp.mean(dxh * xh, axis=-1, keepdims=True))

    row = pl.BlockSpec((tm, D), lambda i: (i, 0))
    vec = pl.BlockSpec((1, D), lambda i: (0, 0))
    two = pl.BlockSpec((2, D), lambda i: (0, 0))
    return pl.pallas_call(
        body, name=name, grid=(T // tm,),
        in_specs=[row, vec, two, row, row], out_specs=[row, vec, two, two],
        out_shape=[jax.ShapeDtypeStruct((T, D), F32), jax.ShapeDtypeStruct((1, D), F32),
                   jax.ShapeDtypeStruct((2, D), F32), jax.ShapeDtypeStruct((2, D), F32)],
        compiler_params=_cparams(("arbitrary",)),
    )(X, g, scale, dh, dres)


def _gate_bwd(dxo, mix, gate, C, name):
    T, D = dxo.shape
    tm = _tile(T, 512, 16)

    def body(dx_ref, mix_ref, gate_ref, dmix_ref, dgate_ref):
        i = pl.program_id(0)

        @pl.when(i == 0)
        def _():
            dgate_ref[...] = jnp.zeros_like(dgate_ref)

        dx = dx_ref[...]
        isc, gt = _seg_select(i, tm, C, gate_ref)
        dmix_ref[...] = (dx * gt).astype(BF16)
        p = dx * mix_ref[...]
        zero = jnp.zeros_like(p)
        dgate_ref[0:1, :] += jnp.sum(jnp.where(isc, p, zero), axis=0, keepdims=True)
        dgate_ref[1:2, :] += jnp.sum(jnp.where(isc, zero, p), axis=0, keepdims=True)

    row = pl.BlockSpec((tm, D), lambda i: (i, 0))
    two = pl.BlockSpec((2, D), lambda i: (0, 0))
    return pl.pallas_call(
        body, name=name, grid=(T // tm,),
        in_specs=[row, row, two], out_specs=[row, two],
        out_shape=[jax.ShapeDtypeStruct((T, D), BF16), jax.ShapeDtypeStruct((2, D), F32)],
        compiler_params=_cparams(("arbitrary",)),
    )(dxo, mix, gate)


def _conv_masks(T, C):
    row = lax.broadcasted_iota(jnp.int32, (T, 1), 0)
    first = (row == 0) | (row == C)
    last = (row == C - 1) | (row == T - 1)
    return first, last


def _shift_rows(x, first, last):
    T = x.shape[0]
    prev = jnp.where(first, 0.0, pltpu.roll(x, 1, 0))
    nxt = jnp.where(last, 0.0, pltpu.roll(x, T - 1, 0))
    return prev, nxt


def _convact(upg, upv, cw, cb, C, name):
    T, F = upg.shape
    tc = LANES

    def body(g_ref, v_ref, w_ref, b_ref, o_ref):
        gp = g_ref[...]
        first, last = _conv_masks(T, C)
        prev, nxt = _shift_rows(gp, first, last)
        gc = prev * w_ref[0:1, :] + gp * w_ref[1:2, :] + nxt * w_ref[2:3, :] + b_ref[...]
        o_ref[...] = (gc * _sigmoid(gc) * v_ref[...]).astype(BF16)

    col = pl.BlockSpec((T, tc), lambda j: (0, j))
    return pl.pallas_call(
        body, name=name, grid=(F // tc,),
        in_specs=[col, col, pl.BlockSpec((3, tc), lambda j: (0, j)), pl.BlockSpec((1, tc), lambda j: (0, j))],
        out_specs=col, out_shape=jax.ShapeDtypeStruct((T, F), BF16),
        compiler_params=_cparams(("parallel",)),
    )(upg, upv, cw, cb)


def _convact_bwd(upg, upv, cw, cb, dact, C, name):
    T, F = upg.shape
    tc = LANES

    def body(g_ref, v_ref, w_ref, b_ref, da_ref, dg_ref, dv_ref, dw_ref, db_ref):
        gp = g_ref[...]
        val = v_ref[...]
        da = da_ref[...]
        first, last = _conv_masks(T, C)
        prev, nxt = _shift_rows(gp, first, last)
        w0, w1, w2 = w_ref[0:1, :], w_ref[1:2, :], w_ref[2:3, :]
        gc = prev * w0 + gp * w1 + nxt * w2 + b_ref[...]
        s = _sigmoid(gc)
        dv_ref[...] = (da * gc * s).astype(BF16)
        dgc = da * val * (s + gc * s * (1.0 - s))
        db_ref[...] = jnp.sum(dgc, axis=0, keepdims=True)
        dw_ref[0:1, :] = jnp.sum(dgc * prev, axis=0, keepdims=True)
        dw_ref[1:2, :] = jnp.sum(dgc * gp, axis=0, keepdims=True)
        dw_ref[2:3, :] = jnp.sum(dgc * nxt, axis=0, keepdims=True)
        dprev, dnxt = _shift_rows(dgc, first, last)
        dg_ref[...] = (dnxt * w0 + dgc * w1 + dprev * w2).astype(BF16)

    col = pl.BlockSpec((T, tc), lambda j: (0, j))
    w3 = pl.BlockSpec((3, tc), lambda j: (0, j))
    w1 = pl.BlockSpec((1, tc), lambda j: (0, j))
    return pl.pallas_call(
        body, name=name, grid=(F // tc,),
        in_specs=[col, col, w3, w1, col], out_specs=[col, col, w3, w1],
        out_shape=[jax.ShapeDtypeStruct((T, F), BF16), jax.ShapeDtypeStruct((T, F), BF16),
                   jax.ShapeDtypeStruct((3, F), F32), jax.ShapeDtypeStruct((1, F), F32)],
        compiler_params=_cparams(("parallel",)),
    )(upg, upv, cw, cb, dact)


def _rot_half(x):
    W = x.shape[-1]
    lane = lax.broadcasted_iota(jnp.int32, x.shape, x.ndim - 1)
    lo = (lane % 32) < 16
    return jnp.where(lo, -pltpu.roll(x, W - 16, x.ndim - 1), pltpu.roll(x, 16, x.ndim - 1))


def _swap_halves(x):
    return pltpu.roll(x, 64, x.ndim - 1)


def _rope_tables(dm):
    t = jnp.arange(dm.L)
    n_freq = HEAD_DIM // 4
    inv_freq = ROPE_BASE ** (-jnp.arange(n_freq, dtype=F32) / n_freq)
    ang_r = (t // GRID_W).astype(F32)[:, None] * inv_freq
    ang_c = (t % GRID_W).astype(F32)[:, None] * inv_freq
    ang = jnp.concatenate([ang_r, ang_r, ang_c, ang_c], axis=-1)
    cos = jnp.concatenate([jnp.ones((dm.C, HEAD_DIM), F32), jnp.cos(ang)], axis=0)
    sin = jnp.concatenate([jnp.zeros((dm.C, HEAD_DIM), F32), jnp.sin(ang)], axis=0)
    return jnp.tile(cos, (1, 2)), jnp.tile(sin, (1, 2))


def _attn_prep(z, cos, sin, dm, name):
    T = dm.T
    AW = dm.ATTN_W
    tm = _tile(T, 256, 16)
    rep = AW // LANES

    def body(z_ref, cos_ref, sin_ref, q_ref, k_ref, v_ref):
        cs, sn = cos_ref[...], sin_ref[...]
        q = z_ref[:, 0:AW]
        csq, snq = jnp.tile(cs, (1, rep)), jnp.tile(sn, (1, rep))
        q_ref[...] = (q * csq + _rot_half(q) * snq).astype(BF16)
        k = z_ref[:, dm.OFF_K:dm.OFF_V]
        k = k * cs + _rot_half(k) * sn
        v = z_ref[:, dm.OFF_V:dm.OFF_S]
        lo = lax.broadcasted_iota(jnp.int32, k.shape, 1) < HEAD_DIM
        ks, vs = _swap_halves(k), _swap_halves(v)
        k_ref[0] = jnp.where(lo, k, ks).astype(BF16)
        k_ref[1] = jnp.where(lo, ks, k).astype(BF16)
        v_ref[0] = jnp.where(lo, v, vs).astype(BF16)
        v_ref[1] = jnp.where(lo, vs, v).astype(BF16)

    tab = pl.BlockSpec((tm, LANES), lambda i: (i, 0))
    kv = pl.BlockSpec((2, tm, LANES), lambda i: (0, i, 0))
    return pl.pallas_call(
        body, name=name, grid=(T // tm,),
        in_specs=[pl.BlockSpec((tm, dm.OFF_S), lambda i: (i, 0)), tab, tab],
        out_specs=[pl.BlockSpec((tm, AW), lambda i: (i, 0)), kv, kv],
        out_shape=[jax.ShapeDtypeStruct((T, AW), BF16), jax.ShapeDtypeStruct((2, T, LANES), BF16),
                   jax.ShapeDtypeStruct((2, T, LANES), BF16)],
        compiler_params=_cparams(("parallel",)),
    )(z, cos, sin)


def _attn_prep_bwd(dq, dk2, dv2, cos, sin, dm, name):
    T = dm.T
    AW = dm.ATTN_W
    tm = _tile(T, 256, 16)
    rep = AW // LANES

    def body(dq_ref, dk_ref, dv_ref, cos_ref, sin_ref, o_ref):
        cs, sn = cos_ref[...], sin_ref[...]
        csq, snq = jnp.tile(cs, (1, rep)), jnp.tile(sn, (1, rep))
        dq = dq_ref[...]
        o_ref[:, 0:AW] = dq * csq - _rot_half(dq * snq)
        lo = lax.broadcasted_iota(jnp.int32, (tm, LANES), 1) < HEAD_DIM
        dk = jnp.where(lo, dk_ref[0], dk_ref[1])
        o_ref[:, dm.OFF_K:dm.OFF_V] = dk * cs - _rot_half(dk * sn)
        o_ref[:, dm.OFF_V:dm.OFF_S] = jnp.where(lo, dv_ref[0], dv_ref[1])

    tab = pl.BlockSpec((tm, LANES), lambda i: (i, 0))
    kv = pl.BlockSpec((2, tm, LANES), lambda i: (0, i, 0))
    return pl.pallas_call(
        body, name=name, grid=(T // tm,),
        in_specs=[pl.BlockSpec((tm, AW), lambda i: (i, 0)), kv, kv, tab, tab],
        out_specs=pl.BlockSpec((tm, dm.OFF_S), lambda i: (i, 0)),
        out_shape=jax.ShapeDtypeStruct((T, dm.OFF_S), F32),
        compiler_params=_cparams(("parallel",)),
    )(dq, dk2, dv2, cos, sin)


def _attn_specs(dm):
    B = ATTN_BLOCK
    nblk = dm.T // B
    q_spec = pl.BlockSpec((B, dm.ATTN_W), lambda i: (i, 0))
    prev = pl.BlockSpec((2, B, LANES), lambda i: (0, jnp.maximum(i - 1, 0), 0))
    own = pl.BlockSpec((2, B, LANES), lambda i: (0, i, 0))
    nxt = pl.BlockSpec((2, B, LANES), lambda i: (0, jnp.minimum(i + 1, nblk - 1), 0))
    ctx = pl.BlockSpec((2, dm.C, LANES), lambda i: (0, 0, 0))
    sink = pl.BlockSpec((2, GQA_RATIO * B, 1), lambda i: (0, 0, 0))
    return nblk, q_spec, prev, own, nxt, ctx, sink


def _attn_scores(i, q_ref, kp_ref, ko_ref, kn_ref, kc_ref, sink_ref, h, dm):
    B = ATTN_BLOCK
    nctx = dm.C // B
    nb = dm.L // B
    npair = GQA_RATIO // 2
    lo = lax.broadcasted_iota(jnp.int32, (B, LANES), 1) < HEAD_DIM
    zero = jnp.zeros((B, LANES), BF16)
    parts = []
    for p in range(npair):
        q2 = q_ref[:, (h * npair + p) * LANES:(h * npair + p + 1) * LANES]
        parts.append(jnp.where(lo, q2, zero))
        parts.append(jnp.where(lo, zero, q2))
    q8 = jnp.concatenate(parts, axis=0)
    kcat = jnp.concatenate([kp_ref[h], ko_ref[h], kn_ref[h], kc_ref[h]], axis=0)
    s = _dot(q8, kcat, _NT) * (HEAD_DIM ** -0.5)
    R, NK = s.shape
    iq = lax.broadcasted_iota(jnp.int32, (R, NK), 0) % B
    col = lax.broadcasted_iota(jnp.int32, (R, NK), 1)
    jk = col % B
    qb = i - nctx
    lat = qb >= 0
    m_prev = (col < B) & lat & (qb >= 1) & (jk >= iq)
    m_own = (col >= B) & (col < 2 * B) & lat
    m_next = (col >= 2 * B) & (col < 3 * B) & lat & (qb <= nb - 2) & (jk <= iq)
    mask = m_prev | m_own | m_next | (col >= 3 * B)
    s = jnp.where(mask, s, -jnp.inf)
    sk = sink_ref[h]
    m = jnp.maximum(jnp.max(s, axis=-1, keepdims=True), sk)
    e = jnp.exp(s - m)
    es = jnp.exp(sk - m)
    inv = 1.0 / (jnp.sum(e, axis=-1, keepdims=True) + es)
    return q8, kcat, e * inv, es * inv, lo


def _unstack_heads(x8, lo):
    B = ATTN_BLOCK
    out = []
    for p in range(GQA_RATIO // 2):
        a = x8[(2 * p) * B:(2 * p + 1) * B]
        b = x8[(2 * p + 1) * B:(2 * p + 2) * B]
        out.append(jnp.where(lo, a, b))
    return out


def _attention(q, k2, v2, sinkcol, dm, name):
    B = ATTN_BLOCK
    nblk, q_spec, prev, own, nxt, ctx, sink = _attn_specs(dm)
    npair = GQA_RATIO // 2

    def body(q_ref, kp_ref, ko_ref, kn_ref, kc_ref, vp_ref, vo_ref, vn_ref, vc_ref, sink_ref, o_ref):
        i = pl.program_id(0)
        for h in range(2):
            _, _, p, _, lo = _attn_scores(i, q_ref, kp_ref, ko_ref, kn_ref, kc_ref, sink_ref, h, dm)
            vcat = jnp.concatenate([vp_ref[h], vo_ref[h], vn_ref[h], vc_ref[h]], axis=0)
            o8 = _dot(p, vcat, _NN)
            for pi, o2 in enumerate(_unstack_heads(o8, lo)):
                c0 = (h * npair + pi) * LANES
                o_ref[:, c0:c0 + LANES] = o2.astype(BF16)

    return pl.pallas_call(
        body, name=name, grid=(nblk,),
        in_specs=[q_spec, prev, own, nxt, ctx, prev, own, nxt, ctx, sink],
        out_specs=q_spec, out_shape=jax.ShapeDtypeStruct((dm.T, dm.ATTN_W), BF16),
        compiler_params=_cparams(("parallel",)),
    )(q, k2, k2, k2, k2, v2, v2, v2, v2, sinkcol)


def _attention_bwd(q, k2, v2, sinkcol, do, dm, name):
    B = ATTN_BLOCK
    nblk, q_spec, prev, own, nxt, ctx, sink = _attn_specs(dm)
    npair = GQA_RATIO // 2

    def body(q_ref, kp_ref, ko_ref, kn_ref, kc_ref, vp_ref, vo_ref, vn_ref, vc_ref, sink_ref, do_ref,
             dq_ref, pk_ref, pv_ref, dkc_ref, dvc_ref, dsk_ref):
        i = pl.program_id(0)

        @pl.when(i == 0)
        def _():
            dkc_ref[...] = jnp.zeros_like(dkc_ref)
            dvc_ref[...] = jnp.zeros_like(dvc_ref)
            dsk_ref[...] = jnp.zeros_like(dsk_ref)

        for h in range(2):
            q8, kcat, p, ps, lo = _attn_scores(i, q_ref, kp_ref, ko_ref, kn_ref, kc_ref, sink_ref, h, dm)
            vcat = jnp.concatenate([vp_ref[h], vo_ref[h], vn_ref[h], vc_ref[h]], axis=0)
            zero = jnp.zeros((B, LANES), F32)
            parts = []
            for pi in range(npair):
                d2 = do_ref[:, (h * npair + pi) * LANES:(h * npair + pi + 1) * LANES]
                parts.append(jnp.where(lo, d2, zero))
                parts.append(jnp.where(lo, zero, d2))
            do8 = jnp.concatenate(parts, axis=0)
            dp = _dot(do8, vcat, _NT)
            delta = jnp.sum(p * dp, axis=-1, keepdims=True)
            ds = p * (dp - delta) * (HEAD_DIM ** -0.5)
            dsr = -ps * delta
            rows = [jnp.broadcast_to(jnp.sum(dsr[r * B:(r + 1) * B], axis=0, keepdims=True), (1, LANES))
                    for r in range(GQA_RATIO)]
            dsk_ref[h] += jnp.concatenate(rows, axis=0)
            dq8 = _dot(ds, kcat, _NN)
            for pi, d2 in enumerate(_unstack_heads(dq8, lo)):
                c0 = (h * npair + pi) * LANES
                dq_ref[:, c0:c0 + LANES] = d2
            dk = _dot(ds, q8, _TN)
            dv = _dot(p, do8, _TN)
            dk = dk + _swap_halves(dk)
            dv = dv + _swap_halves(dv)
            for w in range(3):
                pk_ref[0, h, w] = dk[w * B:(w + 1) * B]
                pv_ref[0, h, w] = dv[w * B:(w + 1) * B]
            dkc_ref[h] += dk[3 * B:]
            dvc_ref[h] += dv[3 * B:]

    part = pl.BlockSpec((1, 2, 3, B, LANES), lambda i: (i, 0, 0, 0, 0))
    acc_c = pl.BlockSpec((2, dm.C, LANES), lambda i: (0, 0, 0))
    acc_s = pl.BlockSpec((2, GQA_RATIO, LANES), lambda i: (0, 0, 0))
    return pl.pallas_call(
        body, name=name, grid=(nblk,),
        in_specs=[q_spec, prev, own, nxt, ctx, prev, own, nxt, ctx, sink, q_spec],
        out_specs=[q_spec, part, part, acc_c, acc_c, acc_s],
        out_shape=[jax.ShapeDtypeStruct((dm.T, dm.ATTN_W), F32),
                   jax.ShapeDtypeStruct((nblk, 2, 3, B, LANES), F32),
                   jax.ShapeDtypeStruct((nblk, 2, 3, B, LANES), F32),
                   jax.ShapeDtypeStruct((2, dm.C, LANES), F32), jax.ShapeDtypeStruct((2, dm.C, LANES), F32),
                   jax.ShapeDtypeStruct((2, GQA_RATIO, LANES), F32)],
        compiler_params=_cparams(("arbitrary",)),
    )(q, k2, k2, k2, k2, v2, v2, v2, v2, sinkcol, do)


def _kv_reduce(part, ctxacc, dm, name):
    B = ATTN_BLOCK
    nblk = dm.T // B
    nctx = dm.C // B

    def body(own_ref, nxt_ref, prv_ref, c_ref, o_ref):
        j = pl.program_id(0)
        acc = own_ref[0, :, 0]
        acc = acc + jnp.where(j < nblk - 1, nxt_ref[0, :, 0], 0.0)
        acc = acc + jnp.where(j > 0, prv_ref[0, :, 0], 0.0)
        acc = acc + jnp.where(j < nctx, c_ref[...], 0.0)
        o_ref[...] = acc

    def spec(w, f):
        return pl.BlockSpec((1, 2, 1, B, LANES), lambda j: (f(j), 0, w, 0, 0))

    return pl.pallas_call(
        body, name=name, grid=(nblk,),
        in_specs=[spec(1, lambda j: j), spec(0, lambda j: jnp.minimum(j + 1, nblk - 1)),
                  spec(2, lambda j: jnp.maximum(j - 1, 0)),
                  pl.BlockSpec((2, B, LANES), lambda j: (0, jnp.minimum(j, nctx - 1), 0))],
        out_specs=pl.BlockSpec((2, B, LANES), lambda j: (0, j, 0)),
        out_shape=jax.ShapeDtypeStruct((2, dm.T, LANES), F32),
        compiler_params=_cparams(("parallel",)),
    )(part, part, part, ctxacc)


def _gmlp_core(z_ref, lg_ref, lb_ref, ws_ref, bs_ref, dm):
    GW = dm.GMLP_W
    gu = z_ref[:, dm.OFF_GU:dm.OFF_GV]
    gv = z_ref[:, dm.OFF_GV:dm.N_IN]
    u = _gelu(gu)
    ge = _gelu(gv)
    mu = jnp.mean(ge, axis=-1, keepdims=True)
    xc = ge - mu
    r = lax.rsqrt(jnp.mean(xc * xc, axis=-1, keepdims=True) + NORM_EPS)
    xh = xc * r
    v = xh * lg_ref[...] + lb_ref[...]
    mixed = []
    for g in range(dm.GG):
        vg = v[:, g * GMLP_GROUP_W:(g + 1) * GMLP_GROUP_W]
        mixed.append(_dot(ws_ref[g], vg, _NN) + bs_ref[g])
    return gu, gv, u, xh, r, v, mixed


def _gmlp(z, ln_g, ln_b, w_s, b_s, dm, name):
    T, GW, CH = dm.T, dm.GMLP_W, GMLP_CHUNK

    def body(z_ref, lg_ref, lb_ref, ws_ref, bs_ref, o_ref):
        _, _, u, _, _, _, mixed = _gmlp_core(z_ref, lg_ref, lb_ref, ws_ref, bs_ref, dm)
        for g in range(dm.GG):
            sl = slice(g * GMLP_GROUP_W, (g + 1) * GMLP_GROUP_W)
            o_ref[:, sl] = (u[:, sl] * mixed[g]).astype(BF16)

    vec = pl.BlockSpec((1, GW), lambda i: (0, 0))
    return pl.pallas_call(
        body, name=name, grid=(T // CH,),
        in_specs=[pl.BlockSpec((CH, dm.N_IN), lambda i: (i, 0)), vec, vec,
                  pl.BlockSpec((dm.GG, CH, CH), lambda i: (0, 0, 0)),
                  pl.BlockSpec((dm.GG, CH, 1), lambda i: (0, 0, 0))],
        out_specs=pl.BlockSpec((CH, GW), lambda i: (i, 0)),
        out_shape=jax.ShapeDtypeStruct((T, GW), BF16),
        compiler_params=_cparams(("parallel",)),
    )(z, ln_g, ln_b, w_s, b_s)


def _gmlp_bwd(z, ln_g, ln_b, w_s, b_s, do, dm, name):
    T, GW, CH = dm.T, dm.GMLP_W, GMLP_CHUNK

    def body(z_ref, lg_ref, lb_ref, ws_ref, bs_ref, do_ref, dz_ref, dlg_ref, dlb_ref, dws_ref, dbs_ref):
        i = pl.program_id(0)

        @pl.when(i == 0)
        def _():
            dlg_ref[...] = jnp.zeros_like(dlg_ref)
            dlb_ref[...] = jnp.zeros_like(dlb_ref)
            dws_ref[...] = jnp.zeros_like(dws_ref)
            dbs_ref[...] = jnp.zeros_like(dbs_ref)

        gu, gv, u, xh, r, v, mixed = _gmlp_core(z_ref, lg_ref, lb_ref, ws_ref, bs_ref, dm)
        do = do_ref[...]
        dv_parts = []
        for g in range(dm.GG):
            sl = slice(g * GMLP_GROUP_W, (g + 1) * GMLP_GROUP_W)
            dog = do[:, sl]
            dz_ref[:, sl] = dog * mixed[g] * _gelu_grad(gu[:, sl])
            dmx = dog * u[:, sl]
            dbs_ref[g] += jnp.sum(dmx, axis=-1, keepdims=True)
            dws_ref[g] += _dot(dmx, v[:, sl], _NT)
            dv_parts.append(_dot(ws_ref[g], dmx, _TN))
        dv = jnp.concatenate(dv_parts, axis=-1) if dm.GG > 1 else dv_parts[0]
        dlg_ref[...] += jnp.sum(dv * xh, axis=0, keepdims=True)
        dlb_ref[...] += jnp.sum(dv, axis=0, keepdims=True)
        dxh = dv * lg_ref[...]
        dge = r * (dxh - jnp.mean(dxh, axis=-1, keepdims=True) - xh * jnp.mean(dxh * xh, axis=-1, keepdims=True))
        dz_ref[:, GW:2 * GW] = dge * _gelu_grad(gv)

    vec = pl.BlockSpec((1, GW), lambda i: (0, 0))
    wsp = pl.BlockSpec((dm.GG, CH, CH), lambda i: (0, 0, 0))
    bsp = pl.BlockSpec((dm.GG, CH, 1), lambda i: (0, 0, 0))
    return pl.pallas_call(
        body, name=name, grid=(T // CH,),
        in_specs=[pl.BlockSpec((CH, dm.N_IN), lambda i: (i, 0)), vec, vec, wsp, bsp,
                  pl.BlockSpec((CH, GW), lambda i: (i, (dm.ATTN_W + dm.SSM_W) // GW))],
        out_specs=[pl.BlockSpec((CH, 2 * GW), lambda i: (i, 0)), vec, vec, wsp, bsp],
        out_shape=[jax.ShapeDtypeStruct((T, 2 * GW), F32), jax.ShapeDtypeStruct((1, GW), F32),
                   jax.ShapeDtypeStruct((1, GW), F32), jax.ShapeDtypeStruct((dm.GG, CH, CH), F32),
                   jax.ShapeDtypeStruct((dm.GG, CH, 1), F32)],
        compiler_params=_cparams(("arbitrary",)),
    )(z, ln_g, ln_b, w_s, b_s, do)


def _disc_fn(lam_re, lam_im, ldt, b_re, b_im):
    dt = jnp.exp(ldt)
    mag = jnp.exp(lam_re * dt)
    a_re = mag * jnp.cos(lam_im * dt)
    a_im = mag * jnp.sin(lam_im * dt)
    den = lam_re * lam_re + lam_im * lam_im
    n_re = a_re - 1.0
    f_re = (n_re * lam_re + a_im * lam_im) / den
    f_im = (a_im * lam_re - n_re * lam_im) / den
    return a_re, a_im, f_re * b_re - f_im * b_im, f_re * b_im + f_im * b_re


def _s5_disc(lam_re, lam_im, ldt, bT_re, bT_im, name):
    nd, H, GP = bT_re.shape

    def body(lr_ref, li_ref, dt_ref, br_ref, bi_ref, ar_ref, ai_ref, bbr_ref, bbi_ref):
        for d in range(nd):
            a_re, a_im, bb_re, bb_im = _disc_fn(lr_ref[d], li_ref[d], dt_ref[d], br_ref[d], bi_ref[d])
            ar_ref[d] = a_re
            ai_ref[d] = a_im
            bbr_ref[d] = bb_re
            bbi_ref[d] = bb_im

    vs = jax.ShapeDtypeStruct((nd, 1, GP), F32)
    ms = jax.ShapeDtypeStruct((nd, H, GP), F32)
    return pl.pallas_call(body, name=name, out_shape=[vs, vs, ms, ms], compiler_params=_cparams())(
        lam_re, lam_im, ldt, bT_re, bT_im)


def _s5_disc_bwd(lam_re, lam_im, ldt, bT_re, bT_im, da_re, da_im, dbb_re, dbb_im, name):
    nd, H, GP = bT_re.shape

    def body(lr_ref, li_ref, dt_ref, br_ref, bi_ref, dar_ref, dai_ref, dbr_ref, dbi_ref,
             olr_ref, oli_ref, odt_ref, obr_ref, obi_ref):
        for d in range(nd):
            _, vjp = jax.vjp(_disc_fn, lr_ref[d], li_ref[d], dt_ref[d], br_ref[d], bi_ref[d])
            g = vjp((dar_ref[d], dai_ref[d], dbr_ref[d], dbi_ref[d]))
            olr_ref[d], oli_ref[d], odt_ref[d], obr_ref[d], obi_ref[d] = g

    vs = jax.ShapeDtypeStruct((nd, 1, GP), F32)
    ms = jax.ShapeDtypeStruct((nd, H, GP), F32)
    return pl.pallas_call(body, name=name, out_shape=[vs, vs, vs, ms, ms], compiler_params=_cparams())(
        lam_re, lam_im, ldt, bT_re, bT_im, da_re, da_im, dbb_re, dbb_im)


def _scan_pass(re_ref, im_ref, nsteps, ar, ai, ir, ii, reverse, store):
    def step(n, carry):
        sr, si = carry
        t = (nsteps - 1 - n) if reverse else n
        off = pl.multiple_of(t * 8, 8)
        nr = ar * sr - ai * si + re_ref[pl.ds(off, 8), :]
        ni = ar * si + ai * sr + im_ref[pl.ds(off, 8), :]
        if store:
            re_ref[pl.ds(off, 8), :] = nr
            im_ref[pl.ds(off, 8), :] = ni
        return nr, ni

    return lax.fori_loop(0, nsteps, step, (ir, ii), unroll=4)


def _cpow(ar, ai, n):
    rr = ri = None
    br, bi = ar, ai
    while n:
        if n & 1:
            rr, ri = (br, bi) if rr is None else (rr * br - ri * bi, rr * bi + ri * br)
        n >>= 1
        if n:
            br, bi = br * br - bi * bi, 2.0 * br * bi
    return rr, ri


def _row_of(x, k):
    rows = lax.broadcasted_iota(jnp.int32, x.shape, 0)
    return jnp.sum(jnp.where(rows == k, x, 0.0), axis=0, keepdims=True)


def _full_scan(re_ref, im_ref, nsteps, ar1, ai1, h0r, h0i, reverse):
    P = ar1.shape[1]
    ar, ai = jnp.broadcast_to(ar1, (8, P)), jnp.broadcast_to(ai1, (8, P))
    z = jnp.zeros((8, P), F32)
    er, ei = _scan_pass(re_ref, im_ref, nsteps, ar, ai, z, z, reverse, False)
    pr, pi = _cpow(ar1, ai1, nsteps)
    rows = lax.broadcasted_iota(jnp.int32, (8, P), 0)
    initr, initi = z, z
    cr, ci = h0r, h0i
    for k in (range(SCAN_SEGMENTS - 1, -1, -1) if reverse else range(SCAN_SEGMENTS)):
        initr = jnp.where(rows == k, cr, initr)
        initi = jnp.where(rows == k, ci, initi)
        ekr, eki = _row_of(er, k), _row_of(ei, k)
        cr, ci = ekr + pr * cr - pi * ci, eki + pr * ci + pi * cr
    _scan_pass(re_ref, im_ref, nsteps, ar, ai, initr, initi, reverse, True)
    return initr, initi, cr, ci


def _rows_loop(n, chunk, fn):
    def step(c, carry):
        fn(pl.multiple_of(c * chunk, chunk))
        return carry
    lax.fori_loop(0, n // chunk, step, 0)


def _s5_specs(dm):
    PB = STATE_BLOCK
    nsb = dm.GP // PB
    per = (LANES * SSM_STATE // SSM_GROUP) // PB
    ul = pl.BlockSpec((dm.L, LANES), lambda j: (0, j // per))
    uc = pl.BlockSpec((dm.C, LANES), lambda j: (0, j // per))
    av = pl.BlockSpec((2, 1, PB), lambda j: (0, 0, j))
    bd = pl.BlockSpec((2, LANES, PB), lambda j: (0, j // per, j))
    cd = pl.BlockSpec((2, PB, LANES), lambda j: (0, j, j // per))
    dv = pl.BlockSpec((1, LANES), lambda j: (0, j // per))
    return PB, nsb, per, ul, uc, av, bd, cd, dv


def _s5_scan(ul, uc, a_re, a_im, bd_re, bd_im, cd_re, cd_im, dskip, dm, name):
    PB, nsb, per, ul_s, uc_s, av, bd, cd, dv = _s5_specs(dm)
    L, C = dm.L, dm.C
    RC = _tile(L, 512, 8)
    RCC = _tile(C, 512, 8)

    def body(ul_ref, uc_ref, ar_ref, ai_ref, br_ref, bi_ref, cr_ref, ci_ref, d_ref, yl_ref, yc_ref,
             lre, lim, cre, cim):
        j = pl.program_id(0)

        @pl.when(j % per == 0)
        def _():
            yl_ref[...] = ul_ref[...] * d_ref[...]
            yc_ref[...] = uc_ref[...] * d_ref[...]

        for d in range(2):
            rev = d == 1
            ar1, ai1 = ar_ref[d], ai_ref[d]

            def proj(u_ref, re, im, chunk, n):
                def f(r0):
                    u = u_ref[pl.ds(r0, chunk), :]
                    re[pl.ds(r0, chunk), :] = _dot(u, br_ref[d], _NN)
                    im[pl.ds(r0, chunk), :] = _dot(u, bi_ref[d], _NN)
                _rows_loop(n, chunk, f)

            def readout(y_ref, re, im, chunk, n):
                def f(r0):
                    y_ref[pl.ds(r0, chunk), :] += (_dot(re[pl.ds(r0, chunk), :], cr_ref[d], _NN)
                                                   - _dot(im[pl.ds(r0, chunk), :], ci_ref[d], _NN))
                _rows_loop(n, chunk, f)

            z1 = jnp.zeros((1, PB), F32)
            proj(uc_ref, cre, cim, RCC, C)
            _, _, fr, fi = _full_scan(cre, cim, C // 8, ar1, ai1, z1, z1, rev)
            readout(yc_ref, cre, cim, RCC, C)
            proj(ul_ref, lre, lim, RC, L)
            _full_scan(lre, lim, L // 8, ar1, ai1, fr, fi, rev)
            readout(yl_ref, lre, lim, RC, L)

    return pl.pallas_call(
        body, name=name, grid=(nsb,),
        in_specs=[ul_s, uc_s, av, av, bd, bd, cd, cd, dv],
        out_specs=[ul_s, uc_s],
        out_shape=[jax.ShapeDtypeStruct((L, dm.SSM_W), F32), jax.ShapeDtypeStruct((C, dm.SSM_W), F32)],
        scratch_shapes=[pltpu.VMEM((L, PB), F32), pltpu.VMEM((L, PB), F32),
                        pltpu.VMEM((C, PB), F32), pltpu.VMEM((C, PB), F32)],
        compiler_params=_cparams(("arbitrary",)),
    )(ul, uc, a_re, a_im, bd_re, bd_im, cd_re, cd_im, dskip)


def _s5_scan_bwd(ul, uc, dyl, dyc, a_re, a_im, bd_re, bd_im, cd_re, cd_im, bdT_re, bdT_im, cdT_re, cdT_im,
                 dskip, dm, name):
    PB, nsb, per, ul_s, uc_s, av, bd, cd, dv = _s5_specs(dm)
    L, C = dm.L, dm.C
    RC = _tile(L, 512, 8)
    RCC = _tile(C, 512, 8)
    bdT = pl.BlockSpec((2, PB, LANES), lambda j: (0, j, j // per))
    cdT = pl.BlockSpec((2, LANES, PB), lambda j: (0, j // per, j))
    dbd = pl.BlockSpec((2, 1, LANES, PB), lambda j: (0, j, 0, 0))
    dcd = pl.BlockSpec((2, 1, PB, LANES), lambda j: (0, j, 0, 0))

    def body(ul_ref, uc_ref, dyl_ref, dyc_ref, ar_ref, ai_ref, br_ref, bi_ref, cr_ref, ci_ref,
             brT_ref, biT_ref, crT_ref, ciT_ref, d_ref,
             dul_ref, duc_ref, dar_ref, dai_ref, dbr_ref, dbi_ref, dcr_ref, dci_ref, dd_ref,
             lre, lim, cre, cim, gre, gim, hre, him):
        j = pl.program_id(0)

        @pl.when(j % per == 0)
        def _():
            dul_ref[...] = dyl_ref[...] * d_ref[...]
            duc_ref[...] = dyc_ref[...] * d_ref[...]
            dd_ref[...] = (jnp.sum(dyl_ref[...] * ul_ref[...], axis=0, keepdims=True)
                           + jnp.sum(dyc_ref[...] * uc_ref[...], axis=0, keepdims=True))

        for d in range(2):
            rev = d == 1
            ar1, ai1 = ar_ref[d], ai_ref[d]
            z1 = jnp.zeros((1, PB), F32)

            def proj(u_ref, w_re, w_im, sign, re, im, chunk, n):
                def f(r0):
                    u = u_ref[pl.ds(r0, chunk), :]
                    re[pl.ds(r0, chunk), :] = _dot(u, w_re, _NN)
                    im[pl.ds(r0, chunk), :] = sign * _dot(u, w_im, _NN)
                _rows_loop(n, chunk, f)

            proj(uc_ref, br_ref[d], bi_ref[d], 1.0, cre, cim, RCC, C)
            icr, ici, fr, fi = _full_scan(cre, cim, C // 8, ar1, ai1, z1, z1, rev)
            proj(ul_ref, br_ref[d], bi_ref[d], 1.0, lre, lim, RC, L)
            ilr, ili, _, _ = _full_scan(lre, lim, L // 8, ar1, ai1, fr, fi, rev)
            proj(dyl_ref, crT_ref[d], ciT_ref[d], -1.0, gre, gim, RC, L)
            _, _, gfr, gfi = _full_scan(gre, gim, L // 8, ar1, -ai1, z1, z1, not rev)
            proj(dyc_ref, crT_ref[d], ciT_ref[d], -1.0, hre, him, RCC, C)
            _full_scan(hre, him, C // 8, ar1, -ai1, gfr, gfi, not rev)

            dar = jnp.zeros((8, PB), F32)
            dai = jnp.zeros((8, PB), F32)
            dbr = jnp.zeros((LANES, PB), F32)
            dbi = jnp.zeros((LANES, PB), F32)
            dcr = jnp.zeros((PB, LANES), F32)
            dci = jnp.zeros((PB, LANES), F32)
            for (u_ref, dy_ref, du_ref, sre, sim, are, aim, inr, ini, n, chunk) in (
                    (ul_ref, dyl_ref, dul_ref, lre, lim, gre, gim, ilr, ili, L, RC),
                    (uc_ref, dyc_ref, duc_ref, cre, cim, hre, him, icr, ici, C, RCC)):
                nst = n // 8

                def da_step(t, carry, sre=sre, sim=sim, are=are, aim=aim, nst=nst):
                    xr, xi = carry
                    tp = (t + 1) if rev else (t - 1)
                    o = pl.multiple_of(t * 8, 8)
                    op = pl.multiple_of(tp * 8, 8)
                    g_r, g_i = are[pl.ds(o, 8), :], aim[pl.ds(o, 8), :]
                    p_r, p_i = sre[pl.ds(op, 8), :], sim[pl.ds(op, 8), :]
                    return xr + g_r * p_r + g_i * p_i, xi + g_i * p_r - g_r * p_i

                lo_t, hi_t = (0, nst - 1) if rev else (1, nst)
                dar, dai = lax.fori_loop(lo_t, hi_t, da_step, (dar, dai))
                e0 = (nst - 1) * 8 if rev else 0
                g_r, g_i = are[pl.ds(e0, 8), :], aim[pl.ds(e0, 8), :]
                dar = dar + g_r * inr + g_i * ini
                dai = dai + g_i * inr - g_r * ini

                def mats(c, carry, u_ref=u_ref, dy_ref=dy_ref, du_ref=du_ref, sre=sre, sim=sim, are=are, aim=aim,
                         chunk=chunk):
                    xbr, xbi, xcr, xci = carry
                    r0 = pl.multiple_of(c * chunk, chunk)
                    u = u_ref[pl.ds(r0, chunk), :]
                    dy = dy_ref[pl.ds(r0, chunk), :]
                    g_r, g_i = are[pl.ds(r0, chunk), :], aim[pl.ds(r0, chunk), :]
                    du_ref[pl.ds(r0, chunk), :] += _dot(g_r, brT_ref[d], _NN) + _dot(g_i, biT_ref[d], _NN)
                    xbr = xbr + _dot(u, g_r, _TN)
                    xbi = xbi + _dot(u, g_i, _TN)
                    xcr = xcr + _dot(sre[pl.ds(r0, chunk), :], dy, _TN)
                    xci = xci - _dot(sim[pl.ds(r0, chunk), :], dy, _TN)
                    return xbr, xbi, xcr, xci

                dbr, dbi, dcr, dci = lax.fori_loop(0, n // chunk, mats, (dbr, dbi, dcr, dci))
            dar_ref[d] = jnp.sum(dar, axis=0, keepdims=True)
            dai_ref[d] = jnp.sum(dai, axis=0, keepdims=True)
            dbr_ref[d, 0] = dbr
            dbi_ref[d, 0] = dbi
            dcr_ref[d, 0] = dcr
            dci_ref[d, 0] = dci

    W, GP = dm.SSM_W, dm.GP
    return pl.pallas_call(
        body, name=name, grid=(nsb,),
        in_specs=[ul_s, uc_s, ul_s, uc_s, av, av, bd, bd, cd, cd, bdT, bdT, cdT, cdT, dv],
        out_specs=[ul_s, uc_s, av, av, dbd, dbd, dcd, dcd, dv],
        out_shape=[jax.ShapeDtypeStruct((L, W), F32), jax.ShapeDtypeStruct((C, W), F32),
                   jax.ShapeDtypeStruct((2, 1, GP), F32), jax.ShapeDtypeStruct((2, 1, GP), F32),
                   jax.ShapeDtypeStruct((2, nsb, LANES, PB), F32), jax.ShapeDtypeStruct((2, nsb, LANES, PB), F32),
                   jax.ShapeDtypeStruct((2, nsb, PB, LANES), F32), jax.ShapeDtypeStruct((2, nsb, PB, LANES), F32),
                   jax.ShapeDtypeStruct((1, W), F32)],
        scratch_shapes=[pltpu.VMEM((L, PB), F32), pltpu.VMEM((L, PB), F32),
                        pltpu.VMEM((C, PB), F32), pltpu.VMEM((C, PB), F32),
                        pltpu.VMEM((L, PB), F32), pltpu.VMEM((L, PB), F32),
                        pltpu.VMEM((C, PB), F32), pltpu.VMEM((C, PB), F32)],
        compiler_params=_cparams(("arbitrary",)),
    )(ul, uc, dyl, dyc, a_re, a_im, bd_re, bd_im, cd_re, cd_im, bdT_re, bdT_im, cdT_re, cdT_im, dskip)


def _glu(y, w, b, name):
    T, W = y.shape
    tm = _tile(T, 544, 16)

    def body(y_ref, w_ref, b_ref, o_ref):
        g = _gelu(y_ref[...])
        o_ref[...] = (g * _sigmoid(_dot(g, w_ref[...], _NN) + b_ref[...])).astype(BF16)

    return pl.pallas_call(
        body, name=name, grid=(T // tm,),
        in_specs=[pl.BlockSpec((tm, W), lambda i: (i, 0)), pl.BlockSpec((W, W), lambda i: (0, 0)),
                  pl.BlockSpec((1, W), lambda i: (0, 0))],
        out_specs=pl.BlockSpec((tm, W), lambda i: (i, 0)),
        out_shape=jax.ShapeDtypeStruct((T, W), BF16),
        compiler_params=_cparams(("parallel",)),
    )(y, w, b)


def _glu_bwd(y, w, b, do, do_col, name):
    T, W = y.shape
    tm = _tile(T, 544, 16)

    def body(y_ref, w_ref, b_ref, do_ref, dy_ref, dw_ref, db_ref):
        i = pl.program_id(0)

        @pl.when(i == 0)
        def _():
            dw_ref[...] = jnp.zeros_like(dw_ref)
            db_ref[...] = jnp.zeros_like(db_ref)

        y = y_ref[...]
        do = do_ref[...]
        g = _gelu(y)
        s = _sigmoid(_dot(g, w_ref[...], _NN) + b_ref[...])
        dpre = do * g * s * (1.0 - s)
        db_ref[...] += jnp.sum(dpre, axis=0, keepdims=True)
        dw_ref[...] += _dot(g, dpre, _TN)
        dg = do * s + _dot(dpre, w_ref[...], _NT)
        dy_ref[...] = dg * _gelu_grad(y)

    row = pl.BlockSpec((tm, W), lambda i: (i, 0))
    wsp = pl.BlockSpec((W, W), lambda i: (0, 0))
    vec = pl.BlockSpec((1, W), lambda i: (0, 0))
    return pl.pallas_call(
        body, name=name, grid=(T // tm,),
        in_specs=[row, wsp, vec, pl.BlockSpec((tm, W), lambda i: (i, do_col))], out_specs=[row, wsp, vec],
        out_shape=[jax.ShapeDtypeStruct((T, W), F32), jax.ShapeDtypeStruct((W, W), F32),
                   jax.ShapeDtypeStruct((1, W), F32)],
        compiler_params=_cparams(("arbitrary",)),
    )(y, w, b, do)


def _loss_head(X, g, target, C, name):
    T, D = X.shape
    tm = _tile(math.gcd(C, T - C), 512, 16)
    nc = C // tm

    def body(x_ref, g_ref, t_ref, loss_ref, dx_ref, dg_ref):
        i = pl.program_id(0)

        @pl.when(i == 0)
        def _():
            loss_ref[...] = jnp.zeros_like(loss_ref)
            dg_ref[...] = jnp.zeros_like(dg_ref)

        @pl.when(i < nc)
        def _():
            dx_ref[...] = jnp.zeros_like(dx_ref)

        @pl.when(i >= nc)
        def _():
            x = x_ref[...]
            gv = g_ref[...]
            r = lax.rsqrt(jnp.mean(x * x, axis=-1, keepdims=True) + NORM_EPS)
            xh = x * r
            err = xh * gv - t_ref[...]
            loss_ref[...] += 0.5 * jnp.sum(jnp.mean(err * err, axis=-1, keepdims=True), axis=0, keepdims=True)
            dy = err * (1.0 / D)
            dg_ref[...] += jnp.sum(dy * xh, axis=0, keepdims=True)
            dxh = dy * gv
            dx_ref[...] = r * (dxh - xh * jnp.mean(dxh * xh, axis=-1, keepdims=True))

    row = pl.BlockSpec((tm, D), lambda i: (i, 0))
    return pl.pallas_call(
        body, name=name, grid=(T // tm,),
        in_specs=[row, pl.BlockSpec((1, D), lambda i: (0, 0)),
                  pl.BlockSpec((tm, D), lambda i: (jnp.maximum(i - nc, 0), 0))],
        out_specs=[pl.BlockSpec((1, 1), lambda i: (0, 0)), row, pl.BlockSpec((1, D), lambda i: (0, 0))],
        out_shape=[jax.ShapeDtypeStruct((1, 1), F32), jax.ShapeDtypeStruct((T, D), F32),
                   jax.ShapeDtypeStruct((1, D), F32)],
        compiler_params=_cparams(("arbitrary",)),
    )(X, g, target)


def _ada_fwd(cond, w, b, name):
    nl, D, Ns = w.shape
    tn = _tile(Ns, 1024, LANES)

    def body(c_ref, w_ref, b_ref, o_ref):
        c = c_ref[...]
        o_ref[0] = _dot(c * _sigmoid(c), w_ref[0], _NN) + b_ref[0]

    return pl.pallas_call(
        body, name=name, grid=(nl, Ns // tn),
        in_specs=[pl.BlockSpec((16, D), lambda l, j: (0, 0)), pl.BlockSpec((1, D, tn), lambda l, j: (l, 0, j)),
                  pl.BlockSpec((1, 1, tn), lambda l, j: (l, 0, j))],
        out_specs=pl.BlockSpec((1, 16, tn), lambda l, j: (l, 0, j)),
        out_shape=jax.ShapeDtypeStruct((nl, 16, Ns), F32),
        compiler_params=_cparams(("parallel", "parallel")),
    )(cond, w, b)


def _ada_bwd(cond, w, dmod, name):
    nl, D, Ns = w.shape
    tn = _tile(Ns, 1024, LANES)

    def body(c_ref, w_ref, dm_ref, dw_ref, da_ref):
        j = pl.program_id(1)

        @pl.when(j == 0)
        def _():
            da_ref[...] = jnp.zeros_like(da_ref)

        c = c_ref[...]
        dw_ref[0] = _dot(c * _sigmoid(c), dm_ref[0], _TN)
        da_ref[0] += _dot(dm_ref[0], w_ref[0], _NT)

    return pl.pallas_call(
        body, name=name, grid=(nl, Ns // tn),
        in_specs=[pl.BlockSpec((16, D), lambda l, j: (0, 0)), pl.BlockSpec((1, D, tn), lambda l, j: (l, 0, j)),
                  pl.BlockSpec((1, 16, tn), lambda l, j: (l, 0, j))],
        out_specs=[pl.BlockSpec((1, D, tn), lambda l, j: (l, 0, j)), pl.BlockSpec((1, 16, D), lambda l, j: (l, 0, 0))],
        out_shape=[jax.ShapeDtypeStruct((nl, D, Ns), F32), jax.ShapeDtypeStruct((nl, 16, D), F32)],
        compiler_params=_cparams(("parallel", "arbitrary")),
    )(cond, w, dmod)


def _cctx_grad(c_ctx, dact4, name):
    nch, nl, _, D = dact4.shape

    def body(c_ref, da_ref, o_ref):
        acc = jnp.zeros((1, D), F32)
        for ch in range(nch):
            for l in range(nl):
                acc = acc + da_ref[ch, l, 8:9, :]
        c = c_ref[...]
        s = _sigmoid(c)
        o_ref[...] = acc * (s + c * s * (1.0 - s))

    return pl.pallas_call(body, name=name, out_shape=jax.ShapeDtypeStruct((1, D), F32),
                          compiler_params=_cparams())(c_ctx, dact4)


def _adamw(w, g, m, v, name):
    R, W = w.shape
    tr = _tile(R, max(8, (1 << 19) // W // 8 * 8), 8)
    c1 = 1.0 - ADAM_B1 ** ADAM_STEP
    c2 = 1.0 - ADAM_B2 ** ADAM_STEP

    def body(w_ref, g_ref, m_ref, v_ref, d_ref, nm_ref, nv_ref):
        g = g_ref[...]
        m = ADAM_B1 * m_ref[...] + (1.0 - ADAM_B1) * g
        v = ADAM_B2 * v_ref[...] + (1.0 - ADAM_B2) * (g * g)
        nm_ref[...] = m
        nv_ref[...] = v
        d_ref[...] = -ADAM_LR * ((m / c1) / (jnp.sqrt(v / c2) + ADAM_EPS) + ADAM_WD * w_ref[...])

    blk = pl.BlockSpec((tr, W), lambda i: (i, 0))
    s = jax.ShapeDtypeStruct((R, W), F32)
    return pl.pallas_call(
        body, name=name, grid=(R // tr,), in_specs=[blk] * 4, out_specs=[blk] * 3, out_shape=[s, s, s],
        compiler_params=_cparams(("parallel",)),
    )(w, g, m, v)


def _adamw_nd(w, g, m, v, name):
    shp = w.shape
    two = (math.prod(shp[:-1]), shp[-1])
    d, nm, nv = _adamw(w.reshape(two), g.reshape(two), m.reshape(two), v.reshape(two), name)
    return d.reshape(shp), nm.reshape(shp), nv.reshape(shp)


def _interleave(a, n):
    return a.reshape(SCAN_SEGMENTS, n // SCAN_SEGMENTS, -1).transpose(1, 0, 2).reshape(n, -1)


def _deinterleave(a, n):
    return a.reshape(n // SCAN_SEGMENTS, SCAN_SEGMENTS, -1).transpose(1, 0, 2).reshape(n, -1)


def _blockdiag_in(bbT, dm):
    eye = jnp.eye(dm.SG, dtype=F32)
    b4 = bbT.reshape(2, SSM_GROUP, dm.SG, SSM_STATE)
    return jnp.einsum("dhgp,gk->dghkp", b4, eye).reshape(2, dm.SSM_W, dm.GP)


def _blockdiag_in_T(dbd, dm):
    gpb = STATE_BLOCK // SSM_STATE
    per = (LANES // SSM_GROUP) // gpb
    d8 = dbd.reshape(2, dm.SSM_W // LANES, per, per, gpb, SSM_GROUP, gpb, SSM_STATE)
    x = jnp.einsum("duaabhbp->duabhp", d8).reshape(2, dm.SG, SSM_GROUP, SSM_STATE)
    return x.transpose(0, 2, 1, 3).reshape(2, SSM_GROUP, dm.GP)


def _blockdiag_out(c, dm):
    eye = jnp.eye(dm.SG, dtype=F32)
    return jnp.einsum("dghp,gk->dgpkh", c, eye).reshape(2, dm.GP, dm.SSM_W)


def _blockdiag_out_T(dcd, dm):
    gpb = STATE_BLOCK // SSM_STATE
    per = (LANES // SSM_GROUP) // gpb
    d8 = dcd.reshape(2, dm.SSM_W // LANES, per, gpb, SSM_STATE, per, gpb, SSM_GROUP)
    return jnp.einsum("duabpabh->duabhp", d8).reshape(2, dm.SG, SSM_GROUP, SSM_STATE)


_BIG = ("w_in", "w_out", "ssm_w_glu", "ffn_w_up", "ffn_w_down")
_SHARD_AXIS = {"w_in": 2, "w_out": 1, "ssm_w_glu": 1, "ffn_w_up": 2, "ffn_w_down": 1}
_SMALL = ("g_mix", "g_ffn", "attn_sink", "ssm_lambda_re", "ssm_lambda_im", "ssm_log_dt", "ssm_b_re", "ssm_b_im",
          "ssm_c_re", "ssm_c_im", "ssm_d", "ssm_b_glu", "gmlp_ln_g", "gmlp_ln_b", "gmlp_w_s", "gmlp_b_s",
          "ffn_conv_b", "g_final", "ffn_conv_w")
_WEIGHTS = ("c_ctx", "w_ada", "b_ada", "g_mix", "g_ffn", "w_in", "w_out", "attn_sink", "ssm_lambda_re",
            "ssm_lambda_im", "ssm_log_dt", "ssm_b_re", "ssm_b_im", "ssm_c_re", "ssm_c_im", "ssm_d", "ssm_w_glu",
            "ssm_b_glu", "gmlp_ln_g", "gmlp_ln_b", "gmlp_w_s", "gmlp_b_s", "ffn_w_up", "ffn_conv_w", "ffn_conv_b",
            "ffn_w_down", "g_final")


def _step(P, M, V, x, c, ctx, loss_target):
    dm = _Dims()
    D, T, C, L = dm.D, dm.T, dm.C, dm.L
    mx, my, mc = lax.axis_index("x"), lax.axis_index("y"), lax.axis_index("c")
    chip = 2 * mx + my
    dev = 2 * chip + mc

    conv_sh = P["ffn_conv_w"].shape
    small_f = _pack([c, P["ffn_conv_w"]], F32, 8)
    small_all = _allgather8(small_f, "ag_small_fwd")
    c_all, conv_all = [], []
    for d in range(N_DEV):
        cd_, cw_ = _unpack(small_all[d], [(1, D), conv_sh])
        c_all.append(cd_)
        conv_all.append(cw_)
    conv_w = jnp.concatenate([conv_all[2 * j] for j in range(N_CHIPS)], axis=2)
    cond = jnp.concatenate(c_all + [P["c_ctx"].reshape(1, D), jnp.zeros((16 - N_DEV - 1, D), F32)], axis=0)

    n_sh = P["w_ada"].shape[2]
    b_sh = lax.dynamic_slice_in_dim(P["b_ada"], chip * n_sh, n_sh, axis=1)[:, None, :]
    mods_sh = _ada_fwd(cond, P["w_ada"], b_sh, "ada_fwd")
    mods4 = _comm(mods_sh, group="chips", by_peer=False, name="ag_mods")
    mods = jnp.concatenate([mods4[j] for j in range(N_CHIPS)], axis=-1)
    mod_lat = lax.dynamic_index_in_dim(mods, dev, axis=1, keepdims=False)
    mod_ctx = mods[:, N_DEV]
    modp = jnp.stack([mod_ctx, mod_lat], axis=1).reshape(DEPTH, 2, N_MOD, D)

    big_shapes = [P[n].shape for n in _BIG]
    w4 = _comm(_pack([P[n] for n in _BIG], BF16, 16), group="chips", by_peer=False, name="ag_w_chips")
    pieces = [_unpack(w4[j], big_shapes) for j in range(N_CHIPS)]

    def full_weight(n, l, chips=range(N_CHIPS)):
        t = _BIG.index(n)
        return jnp.concatenate([pieces[j][t][l] for j in chips], axis=_SHARD_AXIS[n] - 1)

    cos, sin = _rope_tables(dm)
    Wl = [dict(w_in=full_weight("w_in", l), w_out=full_weight("w_out", l), w_glu=full_weight("ssm_w_glu", l),
               w_g=full_weight("ffn_w_up", l, (0, 1)), w_v=full_weight("ffn_w_up", l, (2, 3)),
               w_dn=full_weight("ffn_w_down", l)) for l in range(DEPTH)]

    X = jnp.concatenate([ctx.reshape(C, D), x.reshape(L, D)], axis=0)
    saved = []
    for l in range(DEPTH):
        mp = modp[l]
        sh_a, sc_a, gt_a, sh_f, sc_f, gt_f = [mp[:, k] for k in range(N_MOD)]
        w_in, w_out, w_glu = Wl[l]["w_in"], Wl[l]["w_out"], Wl[l]["w_glu"]
        w_g, w_v, w_dn = Wl[l]["w_g"], Wl[l]["w_v"], Wl[l]["w_dn"]
        g_mix, g_ffn = P["g_mix"][l][None], P["g_ffn"][l][None]

        h = _normmod(X, g_mix, sh_a, sc_a, C, "normmod_a")
        z = _matmul(h, w_in, mode="nn", name="mm_in")
        q, k2, v2 = _attn_prep(z, cos, sin, dm, "attn_prep")
        sinkcol = jnp.repeat(P["attn_sink"][l].reshape(2, GQA_RATIO), ATTN_BLOCK, axis=1)[..., None]
        o_attn = _attention(q, k2, v2, sinkcol, dm, "attention")
        u = z[:, dm.OFF_S:dm.OFF_GU]
        ul, uc = _interleave(u[C:], L), _interleave(u[:C], C)
        lam_re = P["ssm_lambda_re"][l].reshape(2, 1, dm.GP)
        lam_im = P["ssm_lambda_im"][l].reshape(2, 1, dm.GP)
        ldt = jnp.repeat(P["ssm_log_dt"][l], SSM_STATE, axis=-1).reshape(2, 1, dm.GP)
        bT_re = P["ssm_b_re"][l].reshape(2, dm.GP, SSM_GROUP).transpose(0, 2, 1)
        bT_im = P["ssm_b_im"][l].reshape(2, dm.GP, SSM_GROUP).transpose(0, 2, 1)
        a_re, a_im, bbT_re, bbT_im = _s5_disc(lam_re, lam_im, ldt, bT_re, bT_im, "s5_disc")
        bd_re, bd_im = _blockdiag_in(bbT_re, dm).astype(BF16), _blockdiag_in(bbT_im, dm).astype(BF16)
        cd_re = _blockdiag_out(P["ssm_c_re"][l], dm).astype(BF16)
        cd_im = _blockdiag_out(P["ssm_c_im"][l], dm).astype(BF16)
        dskip = P["ssm_d"][l][None]
        yl, yc = _s5_scan(ul, uc, a_re, a_im, bd_re, bd_im, cd_re, cd_im, dskip, dm, "s5_scan")
        y = jnp.concatenate([_deinterleave(yc, C), _deinterleave(yl, L)], axis=0)
        b_glu = P["ssm_b_glu"][l][None]
        o_ssm = _glu(y, w_glu, b_glu, "glu")
        ln_g, ln_b = P["gmlp_ln_g"][l][None], P["gmlp_ln_b"][l][None]
        w_s = P["gmlp_w_s"][l].astype(BF16)
        b_s = P["gmlp_b_s"][l][..., None]
        o_gmlp = _gmlp(z, ln_g, ln_b, w_s, b_s, dm, "gmlp")
        o_cat = jnp.concatenate([o_attn, o_ssm, o_gmlp], axis=-1)
        mix, X1 = _matmul_residual(o_cat, w_out, X, gt_a, C, "mm_out")
        h2 = _normmod(X1, g_ffn, sh_f, sc_f, C, "normmod_f")
        upg = _matmul(h2, w_g, mode="nn", name="mm_up")
        upv = _matmul(h2, w_v, mode="nn", name="mm_up")
        cw, cb = conv_w[l], P["ffn_conv_b"][l][None]
        act = _convact(upg, upv, cw, cb, C, "convact")
        ffn, X2 = _matmul_residual(act, w_dn, X1, gt_f, C, "mm_down")
        saved.append(dict(X=X, h=h, z=z, q=q, k2=k2, v2=v2, sinkcol=sinkcol, ul=ul, uc=uc, a_re=a_re, a_im=a_im,
                          bd_re=bd_re, bd_im=bd_im, cd_re=cd_re, cd_im=cd_im, y=y, o_cat=o_cat, mix=mix, X1=X1, h2=h2,
                          upg=upg, upv=upv, act=act, ffn=ffn, lam_re=lam_re, lam_im=lam_im, ldt=ldt, bT_re=bT_re,
                          bT_im=bT_im))
        X = X2

    loss_l, dX, dg_final = _loss_head(X, P["g_final"][None], loss_target.reshape(L, D), C, "loss_head")
    loss = lax.psum(loss_l[0, 0], ("x", "y", "c"))

    G = {n: [None] * DEPTH for n in _WEIGHTS if n not in ("c_ctx", "w_ada", "b_ada", "g_final")}
    dmod = [None] * DEPTH
    for l in reversed(range(DEPTH)):
        S = saved[l]
        mp = modp[l]
        sh_a, sc_a, gt_a, sh_f, sc_f, gt_f = [mp[:, k] for k in range(N_MOD)]
        w_in, w_out, w_glu = Wl[l]["w_in"], Wl[l]["w_out"], Wl[l]["w_glu"]
        w_g, w_v, w_dn = Wl[l]["w_g"], Wl[l]["w_v"], Wl[l]["w_dn"]
        g_mix, g_ffn = P["g_mix"][l][None], P["g_ffn"][l][None]
        cw, cb = conv_w[l], P["ffn_conv_b"][l][None]

        dffn, dgt_f = _gate_bwd(dX, S["ffn"], gt_f, C, "gate_bwd")
        dact = _matmul(dffn, w_dn, mode="nt", name="mm_down_dx")
        G["ffn_w_down"][l] = _matmul(S["act"], dffn, mode="tn", name="mm_down_dw")
        dupg, dupv, dcw, dcb = _convact_bwd(S["upg"], S["upv"], cw, cb, dact, C, "convact_bwd")
        G["ffn_conv_w"][l], G["ffn_conv_b"][l] = dcw, dcb[0]
        dh2 = _matmul(dupg, w_g, mode="nt", name="mm_up_dx")
        dh2 = _matmul(dupv, w_v, mode="nt", name="mm_up_dx2", add=dh2)
        G["ffn_w_up"][l] = (_matmul(S["h2"], dupg, mode="tn", name="mm_up_dw"),
                            _matmul(S["h2"], dupv, mode="tn", name="mm_up_dw"))
        dX1, dg_f, dsh_f, dsc_f = _normmod_bwd(S["X1"], g_ffn, sc_f, dh2, dX, C, "normmod_bwd")
        G["g_ffn"][l] = dg_f[0]
        dmix, dgt_a = _gate_bwd(dX1, S["mix"], gt_a, C, "gate_bwd")
        do = _matmul(dmix, w_out, mode="nt", name="mm_out_dx")
        G["w_out"][l] = _matmul(S["o_cat"], dmix, mode="tn", name="mm_out_dw")
        do_attn = do_ssm = do_gmlp = do
        ln_g, ln_b = P["gmlp_ln_g"][l][None], P["gmlp_ln_b"][l][None]
        w_s = P["gmlp_w_s"][l].astype(BF16)
        b_s = P["gmlp_b_s"][l][..., None]
        dz_g, dlg, dlb, dws, dbs = _gmlp_bwd(S["z"], ln_g, ln_b, w_s, b_s, do_gmlp, dm, "gmlp_bwd")
        G["gmlp_ln_g"][l], G["gmlp_ln_b"][l], G["gmlp_w_s"][l], G["gmlp_b_s"][l] = dlg[0], dlb[0], dws, dbs[..., 0]
        b_glu = P["ssm_b_glu"][l][None]
        dy, dwglu, dbglu = _glu_bwd(S["y"], w_glu, b_glu, do_ssm, dm.ATTN_W // dm.SSM_W, "glu_bwd")
        G["ssm_w_glu"][l], G["ssm_b_glu"][l] = dwglu, dbglu[0]
        dyl, dyc = _interleave(dy[C:], L), _interleave(dy[:C], C)
        tr = lambda a: jnp.swapaxes(a, 1, 2)
        dskip = P["ssm_d"][l][None]
        (dul, duc, da_re, da_im, dbd_re, dbd_im, dcd_re, dcd_im, dd) = _s5_scan_bwd(
            S["ul"], S["uc"], dyl, dyc, S["a_re"], S["a_im"], S["bd_re"], S["bd_im"], S["cd_re"], S["cd_im"],
            tr(S["bd_re"]), tr(S["bd_im"]), tr(S["cd_re"]), tr(S["cd_im"]), dskip, dm, "s5_scan_bwd")
        dz_s = jnp.concatenate([_deinterleave(duc, C), _deinterleave(dul, L)], axis=0)
        dlr, dli, dldt, dbTr, dbTi = _s5_disc_bwd(S["lam_re"], S["lam_im"], S["ldt"], S["bT_re"], S["bT_im"],
                                                  da_re, da_im, _blockdiag_in_T(dbd_re, dm),
                                                  _blockdiag_in_T(dbd_im, dm), "s5_disc_bwd")
        G["ssm_lambda_re"][l] = dlr.reshape(2, dm.SG, SSM_STATE)
        G["ssm_lambda_im"][l] = dli.reshape(2, dm.SG, SSM_STATE)
        G["ssm_log_dt"][l] = jnp.sum(dldt.reshape(2, dm.SG, SSM_STATE), axis=-1)
        G["ssm_b_re"][l] = dbTr.transpose(0, 2, 1).reshape(2, dm.SG, SSM_STATE, SSM_GROUP)
        G["ssm_b_im"][l] = dbTi.transpose(0, 2, 1).reshape(2, dm.SG, SSM_STATE, SSM_GROUP)
        G["ssm_c_re"][l] = _blockdiag_out_T(dcd_re, dm)
        G["ssm_c_im"][l] = _blockdiag_out_T(dcd_im, dm)
        G["ssm_d"][l] = dd[0]
        dq, pk, pv, dkc, dvc, dsk = _attention_bwd(S["q"], S["k2"], S["v2"], S["sinkcol"], do_attn, dm, "attention_bwd")
        G["attn_sink"][l] = dsk[:, :, 0].reshape(dm.NH)
        dk2 = _kv_reduce(pk, dkc, dm, "kv_reduce")
        dv2 = _kv_reduce(pv, dvc, dm, "kv_reduce")
        dz_a = _attn_prep_bwd(dq, dk2, dv2, cos, sin, dm, "attn_prep_bwd")
        dz = jnp.concatenate([dz_a, dz_s, dz_g], axis=-1).astype(BF16)
        dh = _matmul(dz, w_in, mode="nt", name="mm_in_dx")
        G["w_in"][l] = _matmul(S["h"], dz, mode="tn", name="mm_in_dw")
        dX, dg_a, dsh_a, dsc_a = _normmod_bwd(S["X"], g_mix, sc_a, dh, dX1, C, "normmod_bwd")
        G["g_mix"][l] = dg_a[0]
        dmod[l] = jnp.stack([dsh_a, dsc_a, dgt_a, dsh_f, dsc_f, dgt_f], axis=1)

    grad_x = dX[C:].reshape(1, L, D)
    Gbig = {n: G.pop(n) for n in _BIG}
    G = {n: jnp.stack(v) for n, v in G.items()}
    G["g_final"] = dg_final[0]
    dmod = jnp.stack(dmod)

    small_shapes = [G[n].shape for n in _SMALL]
    n_small = sum(math.prod(s) for s in small_shapes)
    r_small = -(-n_small // (8 * PACK_W)) * 8
    gs_pack = _pack([G[n] for n in _SMALL], F32, 8)
    dm_pack = _pack([dmod], F32, 8)
    r_dm = dm_pack.shape[0]
    allp = _allgather8(jnp.concatenate([gs_pack, dm_pack], axis=0), "ag_small_bwd")
    gs_sum = _sum_slots(allp[:, :r_small], F32, "sum_small")
    for n, a in zip(_SMALL, _unpack(gs_sum, small_shapes)):
        G[n] = a
    dmod_all = allp[:, r_small:r_small + r_dm].reshape(N_DEV, -1)[:, :DEPTH * 2 * N_MOD * D]
    dmod_all = dmod_all.reshape(N_DEV, DEPTH, 2, N_MOD * D)
    dmod_ctx = _sum_slots(dmod_all[:, :, 0].reshape(N_DEV, DEPTH, N_MOD * D), F32, "sum_dmod_ctx")
    dmod16 = jnp.concatenate([jnp.swapaxes(dmod_all[:, :, 1], 0, 1), dmod_ctx[:, None],
                              jnp.zeros((DEPTH, 16 - N_DEV - 1, N_MOD * D), F32)], axis=1)
    G["b_ada"] = _sum_slots(jnp.swapaxes(dmod16, 0, 1)[:N_DEV + 1], F32, "sum_b_ada")
    dmod16_sh = lax.dynamic_slice_in_dim(dmod16, chip * n_sh, n_sh, axis=2)
    G["w_ada"], dact = _ada_bwd(cond, P["w_ada"], dmod16_sh, "ada_bwd")
    dact4 = _comm(dact, group="chips", by_peer=False, name="ag_dact")
    G["c_ctx"] = _cctx_grad(P["c_ctx"].reshape(1, D), dact4, "cctx_grad")[0]
    conv_cols = conv_sh[2]
    G["ffn_conv_w"] = lax.dynamic_slice_in_dim(G["ffn_conv_w"], chip * conv_cols, conv_cols, axis=2)

    def shard_of(n, l, j):
        a = Gbig[n][l]
        if n == "ffn_w_up":
            a, j = a[j // 2], j % 2
            k = a.shape[1] // 2
            return lax.slice_in_dim(a, j * k, (j + 1) * k, axis=1)
        ax = _SHARD_AXIS[n] - 1
        k = a.shape[ax] // N_CHIPS
        return lax.slice_in_dim(a, j * k, (j + 1) * k, axis=ax)

    LH = DEPTH // 2
    half_shapes = [(LH,) + P[n].shape[1:] for n in _BIG]
    gp = jnp.stack([_pack([shard_of(n, hh * LH + li, j) for n in _BIG for li in range(LH)], BF16, 16)
                    for j in range(N_CHIPS) for hh in range(2)])
    parts = _comm(gp, group="all", by_peer=True, name="rs_all")
    red = _sum_slots(parts, BF16, "sum_all")
    both = _comm(red, group="sib", by_peer=False, name="rs_share")
    halves = [_unpack(both[hh], half_shapes) for hh in range(2)]
    for t, n in enumerate(_BIG):
        G[n] = jnp.concatenate([halves[0][t], halves[1][t]], axis=0).astype(F32)

    delta, new_m, new_v = {}, {}, {}
    for n in _BIG + ("w_ada",):
        delta[n], new_m[n], new_v[n] = _adamw_nd(P[n], G[n], M[n], V[n], "adamw_" + n)
    rest = [n for n in _WEIGHTS if n not in _BIG + ("w_ada",)]
    shapes = [P[n].shape for n in rest]
    packed = [_pack([d[n] for n in rest], F32, 8) for d in (P, G, M, V)]
    outs = _adamw(*packed, "adamw_small")
    for dst, buf in zip((delta, new_m, new_v), outs):
        for n, a in zip(rest, _unpack(buf, shapes)):
            dst[n] = a
    return loss, grad_x, G, delta, new_m, new_v


def kernel(x, c, ctx, c_ctx, w_ada, b_ada, g_mix, g_ffn, w_in, w_out, attn_sink, ssm_lambda_re, ssm_lambda_im, ssm_log_dt, ssm_b_re, ssm_b_im, ssm_c_re, ssm_c_im, ssm_d, ssm_w_glu, ssm_b_glu, gmlp_ln_g, gmlp_ln_b, gmlp_w_s, gmlp_b_s, ffn_w_up, ffn_conv_w, ffn_conv_b, ffn_w_down, g_final, loss_target, m_c_ctx, m_w_ada, m_b_ada, m_g_mix, m_g_ffn, m_w_in, m_w_out, m_attn_sink, m_ssm_lambda_re, m_ssm_lambda_im, m_ssm_log_dt, m_ssm_b_re, m_ssm_b_im, m_ssm_c_re, m_ssm_c_im, m_ssm_d, m_ssm_w_glu, m_ssm_b_glu, m_gmlp_ln_g, m_gmlp_ln_b, m_gmlp_w_s, m_gmlp_b_s, m_ffn_w_up, m_ffn_conv_w, m_ffn_conv_b, m_ffn_w_down, m_g_final, v_c_ctx, v_w_ada, v_b_ada, v_g_mix, v_g_ffn, v_w_in, v_w_out, v_attn_sink, v_ssm_lambda_re, v_ssm_lambda_im, v_ssm_log_dt, v_ssm_b_re, v_ssm_b_im, v_ssm_c_re, v_ssm_c_im, v_ssm_d, v_ssm_w_glu, v_ssm_b_glu, v_gmlp_ln_g, v_gmlp_ln_b, v_gmlp_w_s, v_gmlp_b_s, v_ffn_w_up, v_ffn_conv_w, v_ffn_conv_b, v_ffn_w_down, v_g_final):
    env = locals()
    P = {n: env[n] for n in _WEIGHTS}
    M = {n: env["m_" + n] for n in _WEIGHTS}
    V = {n: env["v_" + n] for n in _WEIGHTS}
    loss, grad_x, G, delta, new_m, new_v = _step(P, M, V, x, c, ctx, loss_target)
    return (loss, grad_x, *[G[n] for n in _WEIGHTS], *[delta[n] for n in _WEIGHTS],
            *[new_m[n] for n in _WEIGHTS], *[new_v[n] for n in _WEIGHTS])
```

```python
import functools
import math

import jax
import jax.numpy as jnp
from jax import lax
from jax.experimental import pallas as pl
from jax.experimental.pallas import tpu as pltpu

F32 = jnp.float32
BF16 = jnp.bfloat16

D_MODEL = 2048
SEQ = 4096
DEPTH = 4
CTX_LEN = 256
GRID_W = 64
HEAD_DIM = 64
GQA_RATIO = 8
WINDOW = 128
ATTN_BLOCK = 128
ROPE_BASE = 10000.0
SSM_GROUP = 16
SSM_STATE = 64
GMLP_CHUNK = 128
GMLP_GROUP_W = 128
N_MOD = 6
NORM_EPS = 1e-6
ADAM_LR = 0.001
ADAM_B1 = 0.9
ADAM_B2 = 0.999
ADAM_EPS = 1e-08
ADAM_WD = 0.01
ADAM_STEP = 10

N_CHIPS = 4
N_DEV = 8
SCAN_SEGMENTS = 8
STATE_BLOCK = 256
LANES = 128
PACK_W = 1024
COMM_CHUNKS = (16, 8, 4, 2)
STAGE_BYTES = 4 << 20
SUM_BLOCK_BYTES = 8 << 20
VMEM_LIMIT_MB = 56
_GELU_K = math.sqrt(2.0 / math.pi)
_GELU_C = 0.044715


class _Dims:
    def __init__(self):
        self.D = D_MODEL
        self.L = SEQ
        self.C = CTX_LEN
        self.T = SEQ + CTX_LEN
        self.ATTN_W = D_MODEL // 2
        self.SSM_W = D_MODEL // 4
        self.GMLP_W = D_MODEL - self.ATTN_W - self.SSM_W
        self.NH = self.ATTN_W // HEAD_DIM
        self.NKV = self.NH // GQA_RATIO
        self.KV_W = self.NKV * HEAD_DIM
        self.SG = self.SSM_W // SSM_GROUP
        self.GP = self.SG * SSM_STATE
        self.GG = self.GMLP_W // GMLP_GROUP_W
        self.DFF = ((8 * D_MODEL // 3 + 255) // 256) * 256
        self.OFF_K = self.ATTN_W
        self.OFF_V = self.OFF_K + self.KV_W
        self.OFF_S = self.OFF_V + self.KV_W
        self.OFF_GU = self.OFF_S + self.SSM_W
        self.OFF_GV = self.OFF_GU + self.GMLP_W
        self.N_IN = self.OFF_GV + self.GMLP_W
        assert self.NKV == 2 and self.KV_W == LANES
        assert self.C % (SCAN_SEGMENTS * 8) == 0 and self.L % (SCAN_SEGMENTS * 8) == 0
        assert self.C % ATTN_BLOCK == 0 and self.L % ATTN_BLOCK == 0


def _tile(n, cap, mult):
    best = None
    for t in range(mult, min(n, cap) + 1, mult):
        if n % t == 0:
            best = t
    return best if best is not None else n


def _cparams(sem=None):
    kw = dict(vmem_limit_bytes=VMEM_LIMIT_MB << 20)
    if sem is not None:
        kw["dimension_semantics"] = sem
    return pltpu.CompilerParams(**kw)


def _gelu(x):
    return 0.5 * x * (1.0 + jnp.tanh(_GELU_K * (x + _GELU_C * x * x * x)))


def _gelu_grad(x):
    t = jnp.tanh(_GELU_K * (x + _GELU_C * x * x * x))
    return 0.5 * (1.0 + t) + 0.5 * x * (1.0 - t * t) * _GELU_K * (1.0 + 3.0 * _GELU_C * x * x)


def _sigmoid(x):
    return 1.0 / (1.0 + jnp.exp(-x))


def _dot(a, b, dims):
    return lax.dot_general(a.astype(BF16), b.astype(BF16), (dims, ((), ())), preferred_element_type=F32)


_NN = ((1,), (0,))
_NT = ((1,), (1,))
_TN = ((0,), (0,))


def _sib_staged(x, by_peer, name):
    rows, W = x.shape[-2:]
    cr = _tile(rows, max(16, STAGE_BYTES // (W * x.dtype.itemsize) // 16 * 16), 16)
    nch = rows // cr

    def body(x_ref, o_ref, sbuf, rbuf, send_sems, recv_sems, ld_sem, st_sem, own_sem, credit_sem):
        mx, my, mc = lax.axis_index("x"), lax.axis_index("y"), lax.axis_index("c")
        sib = (mx, my, 1 - mc)
        mine = x_ref.at[mc] if by_peer else x_ref
        theirs = x_ref.at[1 - mc] if by_peer else x_ref
        own = pltpu.make_async_copy(mine, o_ref.at[mc], own_sem)
        own.start()

        def step(i, carry):
            s = i % 2
            r0 = pl.multiple_of(i * cr, 16)
            ld = pltpu.make_async_copy(theirs.at[pl.ds(r0, cr), :], sbuf.at[s], ld_sem)
            ld.start()
            ld.wait()

            @pl.when(i >= 2)
            def _():
                pl.semaphore_wait(credit_sem, 1)

            cp = pltpu.make_async_remote_copy(src_ref=sbuf.at[s], dst_ref=rbuf.at[s], send_sem=send_sems.at[s],
                                              recv_sem=recv_sems.at[s], device_id=sib,
                                              device_id_type=pl.DeviceIdType.MESH)
            cp.start()
            cp.wait_recv()
            st = pltpu.make_async_copy(rbuf.at[s], o_ref.at[1 - mc, pl.ds(r0, cr), :], st_sem)
            st.start()
            st.wait()

            @pl.when(i < nch - 2)
            def _():
                pl.semaphore_signal(credit_sem, inc=1, device_id=sib, device_id_type=pl.DeviceIdType.MESH)

            cp.wait_send()
            return carry

        lax.fori_loop(0, nch, step, 0)
        own.wait()

    return pl.pallas_call(
        body, name=name,
        out_shape=jax.ShapeDtypeStruct((2, rows, W), x.dtype),
        in_specs=[pl.BlockSpec(memory_space=pl.ANY)],
        out_specs=pl.BlockSpec(memory_space=pl.ANY),
        scratch_shapes=[pltpu.VMEM((2, cr, W), x.dtype), pltpu.VMEM((2, cr, W), x.dtype),
                        pltpu.SemaphoreType.DMA((2,)), pltpu.SemaphoreType.DMA((2,)),
                        pltpu.SemaphoreType.DMA, pltpu.SemaphoreType.DMA, pltpu.SemaphoreType.DMA,
                        pltpu.SemaphoreType.REGULAR],
        compiler_params=_cparams(),
    )(x)


def _comm(x, *, group, by_peer, name):
    n = {"sib": 2, "chips": N_CHIPS, "all": N_DEV}[group]
    blk = x.shape[1:] if by_peer else x.shape
    npeer = n - 1
    W = blk[-1]
    rows = math.prod(blk[:-1])
    x2 = x.reshape((n, rows, W) if by_peer else (rows, W))
    if group == "sib":
        return _sib_staged(x2, by_peer, name).reshape((n,) + tuple(blk))
    nchunk = next((k for k in COMM_CHUNKS if rows % (k * 16) == 0), 1)
    cr = rows // nchunk

    def body(x_ref, o_ref, send_sems, recv_sems, local_sem):
        mx, my, mc = lax.axis_index("x"), lax.axis_index("y"), lax.axis_index("c")
        if group == "chips":
            me = 2 * mx + my
            peers = [((mx ^ fx, my ^ fy, mc), 2 * (mx ^ fx) + (my ^ fy)) for fx, fy in ((1, 0), (0, 1), (1, 1))]
        else:
            me = 4 * mx + 2 * my + mc
            peers = [((mx ^ fx, my ^ fy, mc ^ fc), 4 * (mx ^ fx) + 2 * (my ^ fy) + (mc ^ fc))
                     for fx in (0, 1) for fy in (0, 1) for fc in (0, 1) if fx + fy + fc]

        def src_of(pid):
            return x_ref.at[pid] if by_peer else x_ref

        def chunk(ref, ch):
            return ref.at[pl.ds(ch * cr, cr), :]

        def remote(k, dev, pid, ch=None):
            s, d = src_of(pid), o_ref.at[me]
            if ch is not None:
                s, d = chunk(s, ch), chunk(d, ch)
            return pltpu.make_async_remote_copy(src_ref=s, dst_ref=d, send_sem=send_sems.at[k],
                                                recv_sem=recv_sems.at[k], device_id=dev,
                                                device_id_type=pl.DeviceIdType.MESH)

        for ch in range(nchunk):
            pltpu.make_async_copy(chunk(src_of(me), ch), chunk(o_ref.at[me], ch), local_sem).start()
            for k, (dev, pid) in enumerate(peers):
                remote(k, dev, pid, ch).start()
        for k, (dev, pid) in enumerate(peers):
            remote(k, dev, pid).wait_recv()
        for k, (dev, pid) in enumerate(peers):
            remote(k, dev, pid).wait_send()
        pltpu.make_async_copy(src_of(me), o_ref.at[me], local_sem).wait()

    out = pl.pallas_call(
        body, name=name,
        out_shape=jax.ShapeDtypeStruct((n, rows, W), x.dtype),
        in_specs=[pl.BlockSpec(memory_space=pl.ANY)],
        out_specs=pl.BlockSpec(memory_space=pl.ANY),
        scratch_shapes=[pltpu.SemaphoreType.DMA((npeer,)), pltpu.SemaphoreType.DMA((npeer,)),
                        pltpu.SemaphoreType.DMA],
    )(x2)
    return out.reshape((n,) + tuple(blk))


def _comm_multi(xs, *, group, by_peer, name):
    nt = len(xs)
    n = {"chips": N_CHIPS, "all": N_DEV}[group]
    npeer = n - 1

    def body(*refs):
        x_refs, o_refs = refs[:nt], refs[nt:2 * nt]
        send_sems, recv_sems, local_sems = refs[2 * nt:]
        mx, my, mc = lax.axis_index("x"), lax.axis_index("y"), lax.axis_index("c")
        if group == "chips":
            me = 2 * mx + my
            masks = [(fx, fy, 0) for fx, fy in ((1, 0), (0, 1), (1, 1))]
        else:
            me = 4 * mx + 2 * my + mc
            masks = [(fx, fy, fc) for fx in (0, 1) for fy in (0, 1) for fc in (0, 1) if fx + fy + fc]

        def src(t, px, py, pc):
            return x_refs[t].at[pc, 2 * px + py] if by_peer else x_refs[t]

        copies = []
        for t in range(nt):
            pltpu.make_async_copy(src(t, mx, my, mc), o_refs[t].at[me], local_sems.at[t]).start()
            for k, (fx, fy, fc) in enumerate(masks):
                px, py, pc = mx ^ fx, my ^ fy, mc ^ fc
                cp = pltpu.make_async_remote_copy(
                    src_ref=src(t, px, py, pc), dst_ref=o_refs[t].at[me],
                    send_sem=send_sems.at[t * npeer + k], recv_sem=recv_sems.at[t * npeer + k],
                    device_id=(px, py, pc), device_id_type=pl.DeviceIdType.MESH)
                cp.start()
                copies.append(cp)
        for cp in copies:
            cp.wait_recv()
        for cp in copies:
            cp.wait_send()
        for t in range(nt):
            pltpu.make_async_copy(src(t, mx, my, mc), o_refs[t].at[me], local_sems.at[t]).wait()

    any_spec = pl.BlockSpec(memory_space=pl.ANY)
    return pl.pallas_call(
        body, name=name,
        out_shape=[jax.ShapeDtypeStruct((n,) + tuple(x.shape[-2:]), x.dtype) for x in xs],
        in_specs=[any_spec] * nt, out_specs=[any_spec] * nt,
        scratch_shapes=[pltpu.SemaphoreType.DMA((nt * npeer,)), pltpu.SemaphoreType.DMA((nt * npeer,)),
                        pltpu.SemaphoreType.DMA((nt,))],
    )(*xs)


def _allgather8(x, name):
    return _comm(x, group="all", by_peer=False, name=name)


def _sum_slots(x, out_dtype, name):
    n, R, W = x.shape
    tr = _tile(R, max(16, min(512, SUM_BLOCK_BYTES // (n * W * x.dtype.itemsize) // 16 * 16)), 16)

    def body(x_ref, o_ref):
        acc = x_ref[0].astype(F32)
        for s in range(1, n):
            acc = acc + x_ref[s].astype(F32)
        o_ref[...] = acc.astype(o_ref.dtype)

    return pl.pallas_call(
        body, name=name, grid=(R // tr,),
        in_specs=[pl.BlockSpec((n, tr, W), lambda i: (0, i, 0))],
        out_specs=pl.BlockSpec((tr, W), lambda i: (i, 0)),
        out_shape=jax.ShapeDtypeStruct((R, W), out_dtype),
        compiler_params=_cparams(("parallel",)),
    )(x)


def _pack(arrs, dtype, row_mult):
    flat = jnp.concatenate([a.reshape(-1).astype(dtype) for a in arrs])
    n = flat.shape[0]
    quant = row_mult * PACK_W
    total = -(-n // quant) * quant
    if total != n:
        flat = jnp.concatenate([flat, jnp.zeros((total - n,), dtype)])
    return flat.reshape(total // PACK_W, PACK_W)


def _unpack(buf, shapes):
    flat = buf.reshape(-1)
    out, off = [], 0
    for s in shapes:
        n = math.prod(s)
        out.append(flat[off:off + n].reshape(s))
        off += n
    return out


class _Mat:
    def __init__(self, arr, axis=None, layer=None, part0=0, nparts=None):
        self.arr, self.axis, self.layer, self.part0 = arr, axis, layer, part0
        self.r, self.c = arr.shape[-2:]
        self.nparts = 1 if axis is None else (arr.shape[0] if nparts is None else nparts)

    @property
    def shape(self):
        return (self.r * (self.nparts if self.axis == 0 else 1), self.c * (self.nparts if self.axis == 1 else 1))

    def unit(self, ax):
        return self.r if ax == 0 else self.c

    def spec(self, tr, tc, fn):
        def imap(*g):
            rb, cb = fn(*g)
            lead = []
            if self.axis == 0:
                per = self.r // tr
                lead.append(self.part0 + rb // per)
                rb = rb % per
            elif self.axis == 1:
                per = self.c // tc
                lead.append(self.part0 + cb // per)
                cb = cb % per
            if self.layer is not None:
                lead.append(self.layer)
            return (*lead, rb, cb)

        return pl.BlockSpec((None,) * (self.arr.ndim - 2) + (tr, tc), imap)


def _as_mat(x):
    return x if isinstance(x, _Mat) else _Mat(x)


def _matmul(a, b, *, mode, name, out_dtype=F32, add=None, out_split=None, tm_cap=1088, tn_cap=1536, tk_cap=2048):
    a, b = _as_mat(a), _as_mat(b)
    am, ak = (0, 1) if mode != "tn" else (1, 0)
    bk, bn = (0, 1) if mode != "nt" else (1, 0)
    M, K, N = a.shape[am], a.shape[ak], b.shape[bn]
    assert b.shape[bk] == K
    if out_split is None:
        o = _Mat(jax.ShapeDtypeStruct((M, N), out_dtype))
    else:
        ax, parts = out_split
        o = _Mat(jax.ShapeDtypeStruct((parts, M // parts if ax == 0 else M, N // parts if ax == 1 else N), out_dtype),
                 axis=ax)
    tm = _tile(math.gcd(a.unit(am), o.unit(0)), tm_cap, 16 if mode != "tn" else LANES)
    tn = _tile(math.gcd(b.unit(bn), o.unit(1)), tn_cap, LANES)
    tk = _tile(math.gcd(a.unit(ak), b.unit(bk)), tk_cap, LANES if mode != "tn" else 16)
    nk = K // tk
    dims = {"nn": _NN, "nt": _NT, "tn": _TN}[mode]

    def body(*refs):
        if add is None:
            a_ref, b_ref, o_ref, acc_ref = refs
            add_ref = None
        else:
            a_ref, b_ref, add_ref, o_ref, acc_ref = refs
        k = pl.program_id(2)

        @pl.when(k == 0)
        def _():
            acc_ref[...] = jnp.zeros_like(acc_ref)

        acc_ref[...] += _dot(a_ref[...], b_ref[...], dims)

        @pl.when(k == nk - 1)
        def _():
            r = acc_ref[...]
            if add_ref is not None:
                r = r + add_ref[...].astype(F32)
            o_ref[...] = r.astype(o_ref.dtype)

    if mode == "tn":
        a_spec = a.spec(tk, tm, lambda i, j, k: (k, i))
    else:
        a_spec = a.spec(tm, tk, lambda i, j, k: (i, k))
    if mode == "nt":
        b_spec = b.spec(tn, tk, lambda i, j, k: (j, k))
    else:
        b_spec = b.spec(tk, tn, lambda i, j, k: (k, j))
    o_spec = o.spec(tm, tn, lambda i, j, k: (i, j))
    in_specs = [a_spec, b_spec] + ([pl.BlockSpec((tm, tn), lambda i, j, k: (i, j))] if add is not None else [])
    args = (a.arr, b.arr) + ((add,) if add is not None else ())
    return pl.pallas_call(
        body, name=name, grid=(M // tm, N // tn, nk),
        in_specs=in_specs, out_specs=o_spec,
        out_shape=o.arr,
        scratch_shapes=[pltpu.VMEM((tm, tn), F32)],
        compiler_params=_cparams(("parallel", "parallel", "arbitrary")),
    )(*args)


def _matmul_residual(a, w, res, gate, C, name):
    w = _as_mat(w)
    M, K = a.shape
    N = w.shape[1]
    tm = _tile(M, 1088, 16)
    tn = _tile(w.unit(1), 1024, LANES)
    tk = _tile(w.unit(0), 2048, LANES)
    nk = K // tk

    def body(a_ref, w_ref, res_ref, gate_ref, mix_ref, o_ref, acc_ref):
        i, k = pl.program_id(0), pl.program_id(2)

        @pl.when(k == 0)
        def _():
            acc_ref[...] = jnp.zeros_like(acc_ref)

        acc_ref[...] += _dot(a_ref[...], w_ref[...], _NN)

        @pl.when(k == nk - 1)
        def _():
            r = acc_ref[...]
            row = i * tm + lax.broadcasted_iota(jnp.int32, (tm, 1), 0)
            g = jnp.where(row < C, gate_ref[0:1, :], gate_ref[1:2, :])
            mix_ref[...] = r
            o_ref[...] = res_ref[...] + g * r

    o_spec = pl.BlockSpec((tm, tn), lambda i, j, k: (i, j))
    return pl.pallas_call(
        body, name=name, grid=(M // tm, N // tn, nk),
        in_specs=[pl.BlockSpec((tm, tk), lambda i, j, k: (i, k)),
                  w.spec(tk, tn, lambda i, j, k: (k, j)),
                  o_spec,
                  pl.BlockSpec((2, tn), lambda i, j, k: (0, j))],
        out_specs=[o_spec, o_spec],
        out_shape=[jax.ShapeDtypeStruct((M, N), F32), jax.ShapeDtypeStruct((M, N), F32)],
        scratch_shapes=[pltpu.VMEM((tm, tn), F32)],
        compiler_params=_cparams(("parallel", "parallel", "arbitrary")),
    )(a, w.arr, res, gate)


def _seg_select(i, tm, C, ref):
    row = i * tm + lax.broadcasted_iota(jnp.int32, (tm, 1), 0)
    return row < C, jnp.where(row < C, ref[0:1, :], ref[1:2, :])


def _normmod(X, g, shift, scale, C, name):
    T, D = X.shape
    tm = _tile(T, 512, 16)

    def body(x_ref, g_ref, sh_ref, sc_ref, h_ref):
        i = pl.program_id(0)
        x = x_ref[...]
        r = lax.rsqrt(jnp.mean(x * x, axis=-1, keepdims=True) + NORM_EPS)
        _, sh = _seg_select(i, tm, C, sh_ref)
        _, sc = _seg_select(i, tm, C, sc_ref)
        h_ref[...] = ((x * r) * g_ref[...] * (1.0 + sc) + sh).astype(BF16)

    row = pl.BlockSpec((tm, D), lambda i: (i, 0))
    vec = pl.BlockSpec((1, D), lambda i: (0, 0))
    two = pl.BlockSpec((2, D), lambda i: (0, 0))
    return pl.pallas_call(
        body, name=name, grid=(T // tm,),
        in_specs=[row, vec, two, two], out_specs=row,
        out_shape=jax.ShapeDtypeStruct((T, D), BF16),
        compiler_params=_cparams(("parallel",)),
    )(X, g, shift, scale)


def _normmod_bwd(X, g, scale, dh, dres, C, name):
    T, D = X.shape
    tm = _tile(T, 512, 16)

    def body(x_ref, g_ref, sc_ref, dh_ref, dres_ref, dx_ref, dg_ref, dsh_ref, dsc_ref):
        i = pl.program_id(0)

        @pl.when(i == 0)
        def _():
            dg_ref[...] = jnp.zeros_like(dg_ref)
            dsh_ref[...] = jnp.zeros_like(dsh_ref)
            dsc_ref[...] = jnp.zeros_like(dsc_ref)

        x = x_ref[...]
        dh = dh_ref[...]
        gv = g_ref[...]
        r = lax.rsqrt(jnp.mean(x * x, axis=-1, keepdims=True) + NORM_EPS)
        xh = x * r
        isc, sc = _seg_select(i, tm, C, sc_ref)
        n = xh * gv
        dhn = dh * n
        zero = jnp.zeros_like(dh)
        dsh_ref[0:1, :] += jnp.sum(jnp.where(isc, dh, zero), axis=0, keepdims=True)
        dsh_ref[1:2, :] += jnp.sum(jnp.where(isc, zero, dh), axis=0, keepdims=True)
        dsc_ref[0:1, :] += jnp.sum(jnp.where(isc, dhn, zero), axis=0, keepdims=True)
        dsc_ref[1:2, :] += jnp.sum(jnp.where(isc, zero, dhn), axis=0, keepdims=True)
        dn = dh * (1.0 + sc)
        dg_ref[...] += jnp.sum(dn * xh, axis=0, keepdims=True)
        dxh = dn * gv
        dx_ref[...] = dres_ref[...] + r * (dxh - xh * jnp.mean(dxh * xh, axis=-1, keepdims=True))

    row = pl.BlockSpec((tm, D), lambda i: (i, 0))
    vec = pl.BlockSpec((1, D), lambda i: (0, 0))
    two = pl.BlockSpec((2, D), lambda i: (0, 0))
    return pl.pallas_call(
        body, name=name, grid=(T // tm,),
        in_specs=[row, vec, two, row, row], out_specs=[row, vec, two, two],
        out_shape=[jax.ShapeDtypeStruct((T, D), F32), jax.ShapeDtypeStruct((1, D), F32),
                   jax.ShapeDtypeStruct((2, D), F32), jax.ShapeDtypeStruct((2, D), F32)],
        compiler_params=_cparams(("arbitrary",)),
    )(X, g, scale, dh, dres)


def _gate_bwd(dxo, mix, gate, C, name):
    T, D = dxo.shape
    tm = _tile(T, 512, 16)

    def body(dx_ref, mix_ref, gate_ref, dmix_ref, dgate_ref):
        i = pl.program_id(0)

        @pl.when(i == 0)
        def _():
            dgate_ref[...] = jnp.zeros_like(dgate_ref)

        dx = dx_ref[...]
        isc, gt = _seg_select(i, tm, C, gate_ref)
        dmix_ref[...] = (dx * gt).astype(BF16)
        p = dx * mix_ref[...]
        zero = jnp.zeros_like(p)
        dgate_ref[0:1, :] += jnp.sum(jnp.where(isc, p, zero), axis=0, keepdims=True)
        dgate_ref[1:2, :] += jnp.sum(jnp.where(isc, zero, p), axis=0, keepdims=True)

    row = pl.BlockSpec((tm, D), lambda i: (i, 0))
    two = pl.BlockSpec((2, D), lambda i: (0, 0))
    return pl.pallas_call(
        body, name=name, grid=(T // tm,),
        in_specs=[row, row, two], out_specs=[row, two],
        out_shape=[jax.ShapeDtypeStruct((T, D), BF16), jax.ShapeDtypeStruct((2, D), F32)],
        compiler_params=_cparams(("arbitrary",)),
    )(dxo, mix, gate)


def _conv_masks(T, C):
    row = lax.broadcasted_iota(jnp.int32, (T, 1), 0)
    first = (row == 0) | (row == C)
    last = (row == C - 1) | (row == T - 1)
    return first, last


def _shift_rows(x, first, last):
    T = x.shape[0]
    prev = jnp.where(first, 0.0, pltpu.roll(x, 1, 0))
    nxt = jnp.where(last, 0.0, pltpu.roll(x, T - 1, 0))
    return prev, nxt


def _convact(up, cw, cb, C, name):
    _, T, F = up.shape
    tc = LANES

    def body(u_ref, w_ref, b_ref, o_ref):
        gp = u_ref[0]
        first, last = _conv_masks(T, C)
        prev, nxt = _shift_rows(gp, first, last)
        gc = prev * w_ref[0:1, :] + gp * w_ref[1:2, :] + nxt * w_ref[2:3, :] + b_ref[...]
        o_ref[...] = (gc * _sigmoid(gc) * u_ref[1]).astype(BF16)

    col = pl.BlockSpec((T, tc), lambda j: (0, j))
    return pl.pallas_call(
        body, name=name, grid=(F // tc,),
        in_specs=[pl.BlockSpec((2, T, tc), lambda j: (0, 0, j)), pl.BlockSpec((3, tc), lambda j: (0, j)),
                  pl.BlockSpec((1, tc), lambda j: (0, j))],
        out_specs=col, out_shape=jax.ShapeDtypeStruct((T, F), BF16),
        compiler_params=_cparams(("parallel",)),
    )(up, cw, cb)


def _convact_bwd(up, cw, cb, dact, C, name):
    _, T, F = up.shape
    tc = LANES

    def body(u_ref, w_ref, b_ref, da_ref, du_ref, dw_ref, db_ref):
        gp = u_ref[0]
        val = u_ref[1]
        da = da_ref[...]
        first, last = _conv_masks(T, C)
        prev, nxt = _shift_rows(gp, first, last)
        w0, w1, w2 = w_ref[0:1, :], w_ref[1:2, :], w_ref[2:3, :]
        gc = prev * w0 + gp * w1 + nxt * w2 + b_ref[...]
        s = _sigmoid(gc)
        du_ref[1] = (da * gc * s).astype(BF16)
        dgc = da * val * (s + gc * s * (1.0 - s))
        db_ref[...] = jnp.sum(dgc, axis=0, keepdims=True)
        dw_ref[0:1, :] = jnp.sum(dgc * prev, axis=0, keepdims=True)
        dw_ref[1:2, :] = jnp.sum(dgc * gp, axis=0, keepdims=True)
        dw_ref[2:3, :] = jnp.sum(dgc * nxt, axis=0, keepdims=True)
        dprev, dnxt = _shift_rows(dgc, first, last)
        du_ref[0] = (dnxt * w0 + dgc * w1 + dprev * w2).astype(BF16)

    col = pl.BlockSpec((T, tc), lambda j: (0, j))
    two = pl.BlockSpec((2, T, tc), lambda j: (0, 0, j))
    w3 = pl.BlockSpec((3, tc), lambda j: (0, j))
    w1 = pl.BlockSpec((1, tc), lambda j: (0, j))
    return pl.pallas_call(
        body, name=name, grid=(F // tc,),
        in_specs=[two, w3, w1, col], out_specs=[two, w3, w1],
        out_shape=[jax.ShapeDtypeStruct((2, T, F), BF16),
                   jax.ShapeDtypeStruct((3, F), F32), jax.ShapeDtypeStruct((1, F), F32)],
        compiler_params=_cparams(("parallel",)),
    )(up, cw, cb, dact)


def _rot_half(x):
    W = x.shape[-1]
    lane = lax.broadcasted_iota(jnp.int32, x.shape, x.ndim - 1)
    lo = (lane % 32) < 16
    return jnp.where(lo, -pltpu.roll(x, W - 16, x.ndim - 1), pltpu.roll(x, 16, x.ndim - 1))


def _swap_halves(x):
    return pltpu.roll(x, 64, x.ndim - 1)


def _rope_tables(dm):
    t = jnp.arange(dm.L)
    n_freq = HEAD_DIM // 4
    inv_freq = ROPE_BASE ** (-jnp.arange(n_freq, dtype=F32) / n_freq)
    ang_r = (t // GRID_W).astype(F32)[:, None] * inv_freq
    ang_c = (t % GRID_W).astype(F32)[:, None] * inv_freq
    ang = jnp.concatenate([ang_r, ang_r, ang_c, ang_c], axis=-1)
    cos = jnp.concatenate([jnp.ones((dm.C, HEAD_DIM), F32), jnp.cos(ang)], axis=0)
    sin = jnp.concatenate([jnp.zeros((dm.C, HEAD_DIM), F32), jnp.sin(ang)], axis=0)
    return jnp.tile(cos, (1, 2)), jnp.tile(sin, (1, 2))


def _attn_prep(z, cos, sin, dm, name):
    T = dm.T
    AW = dm.ATTN_W
    tm = _tile(T, 256, 16)
    rep = AW // LANES

    def body(z_ref, cos_ref, sin_ref, q_ref, k_ref, v_ref):
        cs, sn = cos_ref[...], sin_ref[...]
        q = z_ref[:, 0:AW]
        csq, snq = jnp.tile(cs, (1, rep)), jnp.tile(sn, (1, rep))
        q_ref[...] = (q * csq + _rot_half(q) * snq).astype(BF16)
        k = z_ref[:, dm.OFF_K:dm.OFF_V]
        k = k * cs + _rot_half(k) * sn
        v = z_ref[:, dm.OFF_V:dm.OFF_S]
        lo = lax.broadcasted_iota(jnp.int32, k.shape, 1) < HEAD_DIM
        ks, vs = _swap_halves(k), _swap_halves(v)
        k_ref[0] = jnp.where(lo, k, ks).astype(BF16)
        k_ref[1] = jnp.where(lo, ks, k).astype(BF16)
        v_ref[0] = jnp.where(lo, v, vs).astype(BF16)
        v_ref[1] = jnp.where(lo, vs, v).astype(BF16)

    tab = pl.BlockSpec((tm, LANES), lambda i: (i, 0))
    kv = pl.BlockSpec((2, tm, LANES), lambda i: (0, i, 0))
    return pl.pallas_call(
        body, name=name, grid=(T // tm,),
        in_specs=[pl.BlockSpec((tm, dm.OFF_S), lambda i: (i, 0)), tab, tab],
        out_specs=[pl.BlockSpec((tm, AW), lambda i: (i, 0)), kv, kv],
        out_shape=[jax.ShapeDtypeStruct((T, AW), BF16), jax.ShapeDtypeStruct((2, T, LANES), BF16),
                   jax.ShapeDtypeStruct((2, T, LANES), BF16)],
        compiler_params=_cparams(("parallel",)),
    )(z, cos, sin)


def _attn_prep_bwd(dq, dk2, dv2, cos, sin, dm, name):
    T = dm.T
    AW = dm.ATTN_W
    tm = _tile(T, 256, 16)
    rep = AW // LANES

    def body(dq_ref, dk_ref, dv_ref, cos_ref, sin_ref, o_ref):
        cs, sn = cos_ref[...], sin_ref[...]
        csq, snq = jnp.tile(cs, (1, rep)), jnp.tile(sn, (1, rep))
        dq = dq_ref[...]
        o_ref[:, 0:AW] = dq * csq - _rot_half(dq * snq)
        lo = lax.broadcasted_iota(jnp.int32, (tm, LANES), 1) < HEAD_DIM
        dk = jnp.where(lo, dk_ref[0], dk_ref[1])
        o_ref[:, dm.OFF_K:dm.OFF_V] = dk * cs - _rot_half(dk * sn)
        o_ref[:, dm.OFF_V:dm.OFF_S] = jnp.where(lo, dv_ref[0], dv_ref[1])

    tab = pl.BlockSpec((tm, LANES), lambda i: (i, 0))
    kv = pl.BlockSpec((2, tm, LANES), lambda i: (0, i, 0))
    return pl.pallas_call(
        body, name=name, grid=(T // tm,),
        in_specs=[pl.BlockSpec((tm, AW), lambda i: (i, 0)), kv, kv, tab, tab],
        out_specs=pl.BlockSpec((tm, dm.OFF_S), lambda i: (i, 0)),
        out_shape=jax.ShapeDtypeStruct((T, dm.OFF_S), F32),
        compiler_params=_cparams(("parallel",)),
    )(dq, dk2, dv2, cos, sin)


def _attn_specs(dm):
    B = ATTN_BLOCK
    nblk = dm.T // B
    q_spec = pl.BlockSpec((B, dm.ATTN_W), lambda i: (i, 0))
    prev = pl.BlockSpec((2, B, LANES), lambda i: (0, jnp.maximum(i - 1, 0), 0))
    own = pl.BlockSpec((2, B, LANES), lambda i: (0, i, 0))
    nxt = pl.BlockSpec((2, B, LANES), lambda i: (0, jnp.minimum(i + 1, nblk - 1), 0))
    ctx = pl.BlockSpec((2, dm.C, LANES), lambda i: (0, 0, 0))
    sink = pl.BlockSpec((2, GQA_RATIO * B, 1), lambda i: (0, 0, 0))
    return nblk, q_spec, prev, own, nxt, ctx, sink


def _attn_scores(i, q_ref, kp_ref, ko_ref, kn_ref, kc_ref, sink_ref, h, dm):
    B = ATTN_BLOCK
    nctx = dm.C // B
    nb = dm.L // B
    npair = GQA_RATIO // 2
    lo = lax.broadcasted_iota(jnp.int32, (B, LANES), 1) < HEAD_DIM
    zero = jnp.zeros((B, LANES), BF16)
    parts = []
    for p in range(npair):
        q2 = q_ref[:, (h * npair + p) * LANES:(h * npair + p + 1) * LANES]
        parts.append(jnp.where(lo, q2, zero))
        parts.append(jnp.where(lo, zero, q2))
    q8 = jnp.concatenate(parts, axis=0)
    kcat = jnp.concatenate([kp_ref[h], ko_ref[h], kn_ref[h], kc_ref[h]], axis=0)
    s = _dot(q8, kcat, _NT) * (HEAD_DIM ** -0.5)
    R, NK = s.shape
    iq = lax.broadcasted_iota(jnp.int32, (R, NK), 0) % B
    col = lax.broadcasted_iota(jnp.int32, (R, NK), 1)
    jk = col % B
    qb = i - nctx
    lat = qb >= 0
    m_prev = (col < B) & lat & (qb >= 1) & (jk >= iq)
    m_own = (col >= B) & (col < 2 * B) & lat
    m_next = (col >= 2 * B) & (col < 3 * B) & lat & (qb <= nb - 2) & (jk <= iq)
    mask = m_prev | m_own | m_next | (col >= 3 * B)
    s = jnp.where(mask, s, -jnp.inf)
    sk = sink_ref[h]
    m = jnp.maximum(jnp.max(s, axis=-1, keepdims=True), sk)
    e = jnp.exp(s - m)
    es = jnp.exp(sk - m)
    inv = 1.0 / (jnp.sum(e, axis=-1, keepdims=True) + es)
    return q8, kcat, e * inv, es * inv, lo


def _unstack_heads(x8, lo):
    B = ATTN_BLOCK
    out = []
    for p in range(GQA_RATIO // 2):
        a = x8[(2 * p) * B:(2 * p + 1) * B]
        b = x8[(2 * p + 1) * B:(2 * p + 2) * B]
        out.append(jnp.where(lo, a, b))
    return out


def _attention(q, k2, v2, sinkcol, dm, name):
    B = ATTN_BLOCK
    nblk, q_spec, prev, own, nxt, ctx, sink = _attn_specs(dm)
    npair = GQA_RATIO // 2

    def body(q_ref, kp_ref, ko_ref, kn_ref, kc_ref, vp_ref, vo_ref, vn_ref, vc_ref, sink_ref, o_ref):
        i = pl.program_id(0)
        for h in range(2):
            _, _, p, _, lo = _attn_scores(i, q_ref, kp_ref, ko_ref, kn_ref, kc_ref, sink_ref, h, dm)
            vcat = jnp.concatenate([vp_ref[h], vo_ref[h], vn_ref[h], vc_ref[h]], axis=0)
            o8 = _dot(p, vcat, _NN)
            for pi, o2 in enumerate(_unstack_heads(o8, lo)):
                c0 = (h * npair + pi) * LANES
                o_ref[:, c0:c0 + LANES] = o2.astype(BF16)

    return pl.pallas_call(
        body, name=name, grid=(nblk,),
        in_specs=[q_spec, prev, own, nxt, ctx, prev, own, nxt, ctx, sink],
        out_specs=q_spec, out_shape=jax.ShapeDtypeStruct((dm.T, dm.ATTN_W), BF16),
        compiler_params=_cparams(("parallel",)),
    )(q, k2, k2, k2, k2, v2, v2, v2, v2, sinkcol)


def _attention_bwd(q, k2, v2, sinkcol, do, dm, name):
    B = ATTN_BLOCK
    nblk, q_spec, prev, own, nxt, ctx, sink = _attn_specs(dm)
    npair = GQA_RATIO // 2

    def body(q_ref, kp_ref, ko_ref, kn_ref, kc_ref, vp_ref, vo_ref, vn_ref, vc_ref, sink_ref, do_ref,
             dq_ref, pk_ref, pv_ref, dkc_ref, dvc_ref, dsk_ref):
        i = pl.program_id(0)

        @pl.when(i == 0)
        def _():
            dkc_ref[...] = jnp.zeros_like(dkc_ref)
            dvc_ref[...] = jnp.zeros_like(dvc_ref)
            dsk_ref[...] = jnp.zeros_like(dsk_ref)

        for h in range(2):
            q8, kcat, p, ps, lo = _attn_scores(i, q_ref, kp_ref, ko_ref, kn_ref, kc_ref, sink_ref, h, dm)
            vcat = jnp.concatenate([vp_ref[h], vo_ref[h], vn_ref[h], vc_ref[h]], axis=0)
            zero = jnp.zeros((B, LANES), F32)
            parts = []
            for pi in range(npair):
                d2 = do_ref[:, (h * npair + pi) * LANES:(h * npair + pi + 1) * LANES]
                parts.append(jnp.where(lo, d2, zero))
                parts.append(jnp.where(lo, zero, d2))
            do8 = jnp.concatenate(parts, axis=0)
            dp = _dot(do8, vcat, _NT)
            delta = jnp.sum(p * dp, axis=-1, keepdims=True)
            ds = p * (dp - delta) * (HEAD_DIM ** -0.5)
            dsr = -ps * delta
            rows = [jnp.broadcast_to(jnp.sum(dsr[r * B:(r + 1) * B], axis=0, keepdims=True), (1, LANES))
                    for r in range(GQA_RATIO)]
            dsk_ref[h] += jnp.concatenate(rows, axis=0)
            dq8 = _dot(ds, kcat, _NN)
            for pi, d2 in enumerate(_unstack_heads(dq8, lo)):
                c0 = (h * npair + pi) * LANES
                dq_ref[:, c0:c0 + LANES] = d2
            dk = _dot(ds, q8, _TN)
            dv = _dot(p, do8, _TN)
            dk = dk + _swap_halves(dk)
            dv = dv + _swap_halves(dv)
            for w in range(3):
                pk_ref[0, h, w] = dk[w * B:(w + 1) * B]
                pv_ref[0, h, w] = dv[w * B:(w + 1) * B]
            dkc_ref[h] += dk[3 * B:]
            dvc_ref[h] += dv[3 * B:]

    part = pl.BlockSpec((1, 2, 3, B, LANES), lambda i: (i, 0, 0, 0, 0))
    acc_c = pl.BlockSpec((2, dm.C, LANES), lambda i: (0, 0, 0))
    acc_s = pl.BlockSpec((2, GQA_RATIO, LANES), lambda i: (0, 0, 0))
    return pl.pallas_call(
        body, name=name, grid=(nblk,),
        in_specs=[q_spec, prev, own, nxt, ctx, prev, own, nxt, ctx, sink, q_spec],
        out_specs=[q_spec, part, part, acc_c, acc_c, acc_s],
        out_shape=[jax.ShapeDtypeStruct((dm.T, dm.ATTN_W), F32),
                   jax.ShapeDtypeStruct((nblk, 2, 3, B, LANES), F32),
                   jax.ShapeDtypeStruct((nblk, 2, 3, B, LANES), F32),
                   jax.ShapeDtypeStruct((2, dm.C, LANES), F32), jax.ShapeDtypeStruct((2, dm.C, LANES), F32),
                   jax.ShapeDtypeStruct((2, GQA_RATIO, LANES), F32)],
        compiler_params=_cparams(("arbitrary",)),
    )(q, k2, k2, k2, k2, v2, v2, v2, v2, sinkcol, do)


def _kv_reduce(part, ctxacc, dm, name):
    B = ATTN_BLOCK
    nblk = dm.T // B
    nctx = dm.C // B

    def body(own_ref, nxt_ref, prv_ref, c_ref, o_ref):
        j = pl.program_id(0)
        acc = own_ref[0, :, 0]
        acc = acc + jnp.where(j < nblk - 1, nxt_ref[0, :, 0], 0.0)
        acc = acc + jnp.where(j > 0, prv_ref[0, :, 0], 0.0)
        acc = acc + jnp.where(j < nctx, c_ref[...], 0.0)
        o_ref[...] = acc

    def spec(w, f):
        return pl.BlockSpec((1, 2, 1, B, LANES), lambda j: (f(j), 0, w, 0, 0))

    return pl.pallas_call(
        body, name=name, grid=(nblk,),
        in_specs=[spec(1, lambda j: j), spec(0, lambda j: jnp.minimum(j + 1, nblk - 1)),
                  spec(2, lambda j: jnp.maximum(j - 1, 0)),
                  pl.BlockSpec((2, B, LANES), lambda j: (0, jnp.minimum(j, nctx - 1), 0))],
        out_specs=pl.BlockSpec((2, B, LANES), lambda j: (0, j, 0)),
        out_shape=jax.ShapeDtypeStruct((2, dm.T, LANES), F32),
        compiler_params=_cparams(("parallel",)),
    )(part, part, part, ctxacc)


def _gmlp_core(z_ref, lg_ref, lb_ref, ws_ref, bs_ref, dm):
    GW = dm.GMLP_W
    gu = z_ref[:, dm.OFF_GU:dm.OFF_GV]
    gv = z_ref[:, dm.OFF_GV:dm.N_IN]
    u = _gelu(gu)
    ge = _gelu(gv)
    mu = jnp.mean(ge, axis=-1, keepdims=True)
    xc = ge - mu
    r = lax.rsqrt(jnp.mean(xc * xc, axis=-1, keepdims=True) + NORM_EPS)
    xh = xc * r
    v = xh * lg_ref[...] + lb_ref[...]
    mixed = []
    for g in range(dm.GG):
        vg = v[:, g * GMLP_GROUP_W:(g + 1) * GMLP_GROUP_W]
        mixed.append(_dot(ws_ref[g], vg, _NN) + bs_ref[g])
    return gu, gv, u, xh, r, v, mixed


def _gmlp(z, ln_g, ln_b, w_s, b_s, dm, name):
    T, GW, CH = dm.T, dm.GMLP_W, GMLP_CHUNK

    def body(z_ref, lg_ref, lb_ref, ws_ref, bs_ref, o_ref):
        _, _, u, _, _, _, mixed = _gmlp_core(z_ref, lg_ref, lb_ref, ws_ref, bs_ref, dm)
        for g in range(dm.GG):
            sl = slice(g * GMLP_GROUP_W, (g + 1) * GMLP_GROUP_W)
            o_ref[:, sl] = (u[:, sl] * mixed[g]).astype(BF16)

    vec = pl.BlockSpec((1, GW), lambda i: (0, 0))
    return pl.pallas_call(
        body, name=name, grid=(T // CH,),
        in_specs=[pl.BlockSpec((CH, dm.N_IN), lambda i: (i, 0)), vec, vec,
                  pl.BlockSpec((dm.GG, CH, CH), lambda i: (0, 0, 0)),
                  pl.BlockSpec((dm.GG, CH, 1), lambda i: (0, 0, 0))],
        out_specs=pl.BlockSpec((CH, GW), lambda i: (i, 0)),
        out_shape=jax.ShapeDtypeStruct((T, GW), BF16),
        compiler_params=_cparams(("parallel",)),
    )(z, ln_g, ln_b, w_s, b_s)


def _gmlp_bwd(z, ln_g, ln_b, w_s, b_s, do, dm, name):
    T, GW, CH = dm.T, dm.GMLP_W, GMLP_CHUNK

    def body(z_ref, lg_ref, lb_ref, ws_ref, bs_ref, do_ref, dz_ref, dlg_ref, dlb_ref, dws_ref, dbs_ref):
        i = pl.program_id(0)

        @pl.when(i == 0)
        def _():
            dlg_ref[...] = jnp.zeros_like(dlg_ref)
            dlb_ref[...] = jnp.zeros_like(dlb_ref)
            dws_ref[...] = jnp.zeros_like(dws_ref)
            dbs_ref[...] = jnp.zeros_like(dbs_ref)

        gu, gv, u, xh, r, v, mixed = _gmlp_core(z_ref, lg_ref, lb_ref, ws_ref, bs_ref, dm)
        do = do_ref[...]
        dv_parts = []
        for g in range(dm.GG):
            sl = slice(g * GMLP_GROUP_W, (g + 1) * GMLP_GROUP_W)
            dog = do[:, sl]
            dz_ref[:, sl] = dog * mixed[g] * _gelu_grad(gu[:, sl])
            dmx = dog * u[:, sl]
            dbs_ref[g] += jnp.sum(dmx, axis=-1, keepdims=True)
            dws_ref[g] += _dot(dmx, v[:, sl], _NT)
            dv_parts.append(_dot(ws_ref[g], dmx, _TN))
        dv = jnp.concatenate(dv_parts, axis=-1) if dm.GG > 1 else dv_parts[0]
        dlg_ref[...] += jnp.sum(dv * xh, axis=0, keepdims=True)
        dlb_ref[...] += jnp.sum(dv, axis=0, keepdims=True)
        dxh = dv * lg_ref[...]
        dge = r * (dxh - jnp.mean(dxh, axis=-1, keepdims=True) - xh * jnp.mean(dxh * xh, axis=-1, keepdims=True))
        dz_ref[:, GW:2 * GW] = dge * _gelu_grad(gv)

    vec = pl.BlockSpec((1, GW), lambda i: (0, 0))
    wsp = pl.BlockSpec((dm.GG, CH, CH), lambda i: (0, 0, 0))
    bsp = pl.BlockSpec((dm.GG, CH, 1), lambda i: (0, 0, 0))
    return pl.pallas_call(
        body, name=name, grid=(T // CH,),
        in_specs=[pl.BlockSpec((CH, dm.N_IN), lambda i: (i, 0)), vec, vec, wsp, bsp,
                  pl.BlockSpec((CH, GW), lambda i: (i, (dm.ATTN_W + dm.SSM_W) // GW))],
        out_specs=[pl.BlockSpec((CH, 2 * GW), lambda i: (i, 0)), vec, vec, wsp, bsp],
        out_shape=[jax.ShapeDtypeStruct((T, 2 * GW), F32), jax.ShapeDtypeStruct((1, GW), F32),
                   jax.ShapeDtypeStruct((1, GW), F32), jax.ShapeDtypeStruct((dm.GG, CH, CH), F32),
                   jax.ShapeDtypeStruct((dm.GG, CH, 1), F32)],
        compiler_params=_cparams(("arbitrary",)),
    )(z, ln_g, ln_b, w_s, b_s, do)


def _disc_fn(lam_re, lam_im, ldt, b_re, b_im):
    dt = jnp.exp(ldt)
    mag = jnp.exp(lam_re * dt)
    a_re = mag * jnp.cos(lam_im * dt)
    a_im = mag * jnp.sin(lam_im * dt)
    den = lam_re * lam_re + lam_im * lam_im
    n_re = a_re - 1.0
    f_re = (n_re * lam_re + a_im * lam_im) / den
    f_im = (a_im * lam_re - n_re * lam_im) / den
    return a_re, a_im, f_re * b_re - f_im * b_im, f_re * b_im + f_im * b_re


def _s5_disc(lam_re, lam_im, ldt, bT_re, bT_im, name):
    nd, H, GP = bT_re.shape

    def body(lr_ref, li_ref, dt_ref, br_ref, bi_ref, ar_ref, ai_ref, bbr_ref, bbi_ref):
        for d in range(nd):
            a_re, a_im, bb_re, bb_im = _disc_fn(lr_ref[d], li_ref[d], dt_ref[d], br_ref[d], bi_ref[d])
            ar_ref[d] = a_re
            ai_ref[d] = a_im
            bbr_ref[d] = bb_re
            bbi_ref[d] = bb_im

    vs = jax.ShapeDtypeStruct((nd, 1, GP), F32)
    ms = jax.ShapeDtypeStruct((nd, H, GP), F32)
    return pl.pallas_call(body, name=name, out_shape=[vs, vs, ms, ms], compiler_params=_cparams())(
        lam_re, lam_im, ldt, bT_re, bT_im)


def _s5_disc_bwd(lam_re, lam_im, ldt, bT_re, bT_im, da_re, da_im, dbb_re, dbb_im, name):
    nd, H, GP = bT_re.shape

    def body(lr_ref, li_ref, dt_ref, br_ref, bi_ref, dar_ref, dai_ref, dbr_ref, dbi_ref,
             olr_ref, oli_ref, odt_ref, obr_ref, obi_ref):
        for d in range(nd):
            _, vjp = jax.vjp(_disc_fn, lr_ref[d], li_ref[d], dt_ref[d], br_ref[d], bi_ref[d])
            g = vjp((dar_ref[d], dai_ref[d], dbr_ref[d], dbi_ref[d]))
            olr_ref[d], oli_ref[d], odt_ref[d], obr_ref[d], obi_ref[d] = g

    vs = jax.ShapeDtypeStruct((nd, 1, GP), F32)
    ms = jax.ShapeDtypeStruct((nd, H, GP), F32)
    return pl.pallas_call(body, name=name, out_shape=[vs, vs, vs, ms, ms], compiler_params=_cparams())(
        lam_re, lam_im, ldt, bT_re, bT_im, da_re, da_im, dbb_re, dbb_im)


def _scan_pass(re_ref, im_ref, nsteps, ar, ai, ir, ii, reverse, store):
    def step(n, carry):
        sr, si = carry
        t = (nsteps - 1 - n) if reverse else n
        off = pl.multiple_of(t * 8, 8)
        nr = ar * sr - ai * si + re_ref[pl.ds(off, 8), :]
        ni = ar * si + ai * sr + im_ref[pl.ds(off, 8), :]
        if store:
            re_ref[pl.ds(off, 8), :] = nr
            im_ref[pl.ds(off, 8), :] = ni
        return nr, ni

    return lax.fori_loop(0, nsteps, step, (ir, ii), unroll=4)


def _cpow(ar, ai, n):
    rr = ri = None
    br, bi = ar, ai
    while n:
        if n & 1:
            rr, ri = (br, bi) if rr is None else (rr * br - ri * bi, rr * bi + ri * br)
        n >>= 1
        if n:
            br, bi = br * br - bi * bi, 2.0 * br * bi
    return rr, ri


def _row_of(x, k):
    rows = lax.broadcasted_iota(jnp.int32, x.shape, 0)
    return jnp.sum(jnp.where(rows == k, x, 0.0), axis=0, keepdims=True)


def _full_scan(re_ref, im_ref, nsteps, ar1, ai1, h0r, h0i, reverse):
    P = ar1.shape[1]
    ar, ai = jnp.broadcast_to(ar1, (8, P)), jnp.broadcast_to(ai1, (8, P))
    z = jnp.zeros((8, P), F32)
    er, ei = _scan_pass(re_ref, im_ref, nsteps, ar, ai, z, z, reverse, False)
    pr, pi = _cpow(ar1, ai1, nsteps)
    rows = lax.broadcasted_iota(jnp.int32, (8, P), 0)
    initr, initi = z, z
    cr, ci = h0r, h0i
    for k in (range(SCAN_SEGMENTS - 1, -1, -1) if reverse else range(SCAN_SEGMENTS)):
        initr = jnp.where(rows == k, cr, initr)
        initi = jnp.where(rows == k, ci, initi)
        ekr, eki = _row_of(er, k), _row_of(ei, k)
        cr, ci = ekr + pr * cr - pi * ci, eki + pr * ci + pi * cr
    _scan_pass(re_ref, im_ref, nsteps, ar, ai, initr, initi, reverse, True)
    return initr, initi, cr, ci


def _rows_loop(n, chunk, fn):
    def step(c, carry):
        fn(pl.multiple_of(c * chunk, chunk))
        return carry
    lax.fori_loop(0, n // chunk, step, 0)


def _s5_specs(dm):
    PB = STATE_BLOCK
    nsb = dm.GP // PB
    per = (LANES * SSM_STATE // SSM_GROUP) // PB
    ul = pl.BlockSpec((dm.L, LANES), lambda j: (0, j // per))
    uc = pl.BlockSpec((dm.C, LANES), lambda j: (0, j // per))
    av = pl.BlockSpec((2, 1, PB), lambda j: (0, 0, j))
    bd = pl.BlockSpec((2, LANES, PB), lambda j: (0, j // per, j))
    cd = pl.BlockSpec((2, PB, LANES), lambda j: (0, j, j // per))
    dv = pl.BlockSpec((1, LANES), lambda j: (0, j // per))
    return PB, nsb, per, ul, uc, av, bd, cd, dv


def _s5_scan(ul, uc, a_re, a_im, bd_re, bd_im, cd_re, cd_im, dskip, dm, name):
    PB, nsb, per, ul_s, uc_s, av, bd, cd, dv = _s5_specs(dm)
    L, C = dm.L, dm.C
    RC = _tile(L, 512, 8)
    RCC = _tile(C, 512, 8)

    def body(ul_ref, uc_ref, ar_ref, ai_ref, br_ref, bi_ref, cr_ref, ci_ref, d_ref, yl_ref, yc_ref,
             lre, lim, cre, cim):
        j = pl.program_id(0)

        @pl.when(j % per == 0)
        def _():
            yl_ref[...] = ul_ref[...] * d_ref[...]
            yc_ref[...] = uc_ref[...] * d_ref[...]

        for d in range(2):
            rev = d == 1
            ar1, ai1 = ar_ref[d], ai_ref[d]

            def proj(u_ref, re, im, chunk, n):
                def f(r0):
                    u = u_ref[pl.ds(r0, chunk), :]
                    re[pl.ds(r0, chunk), :] = _dot(u, br_ref[d], _NN)
                    im[pl.ds(r0, chunk), :] = _dot(u, bi_ref[d], _NN)
                _rows_loop(n, chunk, f)

            def readout(y_ref, re, im, chunk, n):
                def f(r0):
                    y_ref[pl.ds(r0, chunk), :] += (_dot(re[pl.ds(r0, chunk), :], cr_ref[d], _NN)
                                                   - _dot(im[pl.ds(r0, chunk), :], ci_ref[d], _NN))
                _rows_loop(n, chunk, f)

            z1 = jnp.zeros((1, PB), F32)
            proj(uc_ref, cre, cim, RCC, C)
            _, _, fr, fi = _full_scan(cre, cim, C // 8, ar1, ai1, z1, z1, rev)
            readout(yc_ref, cre, cim, RCC, C)
            proj(ul_ref, lre, lim, RC, L)
            _full_scan(lre, lim, L // 8, ar1, ai1, fr, fi, rev)
            readout(yl_ref, lre, lim, RC, L)

    return pl.pallas_call(
        body, name=name, grid=(nsb,),
        in_specs=[ul_s, uc_s, av, av, bd, bd, cd, cd, dv],
        out_specs=[ul_s, uc_s],
        out_shape=[jax.ShapeDtypeStruct((L, dm.SSM_W), F32), jax.ShapeDtypeStruct((C, dm.SSM_W), F32)],
        scratch_shapes=[pltpu.VMEM((L, PB), F32), pltpu.VMEM((L, PB), F32),
                        pltpu.VMEM((C, PB), F32), pltpu.VMEM((C, PB), F32)],
        compiler_params=_cparams(("arbitrary",)),
    )(ul, uc, a_re, a_im, bd_re, bd_im, cd_re, cd_im, dskip)


def _s5_scan_bwd(ul, uc, dyl, dyc, a_re, a_im, bd_re, bd_im, cd_re, cd_im, bdT_re, bdT_im, cdT_re, cdT_im,
                 dskip, dm, name):
    PB, nsb, per, ul_s, uc_s, av, bd, cd, dv = _s5_specs(dm)
    L, C = dm.L, dm.C
    RC = _tile(L, 512, 8)
    RCC = _tile(C, 512, 8)
    bdT = pl.BlockSpec((2, PB, LANES), lambda j: (0, j, j // per))
    cdT = pl.BlockSpec((2, LANES, PB), lambda j: (0, j // per, j))
    dbd = pl.BlockSpec((2, 1, LANES, PB), lambda j: (0, j, 0, 0))
    dcd = pl.BlockSpec((2, 1, PB, LANES), lambda j: (0, j, 0, 0))

    def body(ul_ref, uc_ref, dyl_ref, dyc_ref, ar_ref, ai_ref, br_ref, bi_ref, cr_ref, ci_ref,
             brT_ref, biT_ref, crT_ref, ciT_ref, d_ref,
             dul_ref, duc_ref, dar_ref, dai_ref, dbr_ref, dbi_ref, dcr_ref, dci_ref, dd_ref,
             lre, lim, cre, cim, gre, gim, hre, him):
        j = pl.program_id(0)

        @pl.when(j % per == 0)
        def _():
            dul_ref[...] = dyl_ref[...] * d_ref[...]
            duc_ref[...] = dyc_ref[...] * d_ref[...]
            dd_ref[...] = (jnp.sum(dyl_ref[...] * ul_ref[...], axis=0, keepdims=True)
                           + jnp.sum(dyc_ref[...] * uc_ref[...], axis=0, keepdims=True))

        for d in range(2):
            rev = d == 1
            ar1, ai1 = ar_ref[d], ai_ref[d]
            z1 = jnp.zeros((1, PB), F32)

            def proj(u_ref, w_re, w_im, sign, re, im, chunk, n):
                def f(r0):
                    u = u_ref[pl.ds(r0, chunk), :]
                    re[pl.ds(r0, chunk), :] = _dot(u, w_re, _NN)
                    im[pl.ds(r0, chunk), :] = sign * _dot(u, w_im, _NN)
                _rows_loop(n, chunk, f)

            proj(uc_ref, br_ref[d], bi_ref[d], 1.0, cre, cim, RCC, C)
            icr, ici, fr, fi = _full_scan(cre, cim, C // 8, ar1, ai1, z1, z1, rev)
            proj(ul_ref, br_ref[d], bi_ref[d], 1.0, lre, lim, RC, L)
            ilr, ili, _, _ = _full_scan(lre, lim, L // 8, ar1, ai1, fr, fi, rev)
            proj(dyl_ref, crT_ref[d], ciT_ref[d], -1.0, gre, gim, RC, L)
            _, _, gfr, gfi = _full_scan(gre, gim, L // 8, ar1, -ai1, z1, z1, not rev)
            proj(dyc_ref, crT_ref[d], ciT_ref[d], -1.0, hre, him, RCC, C)
            _full_scan(hre, him, C // 8, ar1, -ai1, gfr, gfi, not rev)

            dar = jnp.zeros((8, PB), F32)
            dai = jnp.zeros((8, PB), F32)
            dbr = jnp.zeros((LANES, PB), F32)
            dbi = jnp.zeros((LANES, PB), F32)
            dcr = jnp.zeros((PB, LANES), F32)
            dci = jnp.zeros((PB, LANES), F32)
            for (u_ref, dy_ref, du_ref, sre, sim, are, aim, inr, ini, n, chunk) in (
                    (ul_ref, dyl_ref, dul_ref, lre, lim, gre, gim, ilr, ili, L, RC),
                    (uc_ref, dyc_ref, duc_ref, cre, cim, hre, him, icr, ici, C, RCC)):
                nst = n // 8

                def da_step(t, carry, sre=sre, sim=sim, are=are, aim=aim, nst=nst):
                    xr, xi = carry
                    tp = (t + 1) if rev else (t - 1)
                    o = pl.multiple_of(t * 8, 8)
                    op = pl.multiple_of(tp * 8, 8)
                    g_r, g_i = are[pl.ds(o, 8), :], aim[pl.ds(o, 8), :]
                    p_r, p_i = sre[pl.ds(op, 8), :], sim[pl.ds(op, 8), :]
                    return xr + g_r * p_r + g_i * p_i, xi + g_i * p_r - g_r * p_i

                lo_t, hi_t = (0, nst - 1) if rev else (1, nst)
                dar, dai = lax.fori_loop(lo_t, hi_t, da_step, (dar, dai))
                e0 = (nst - 1) * 8 if rev else 0
                g_r, g_i = are[pl.ds(e0, 8), :], aim[pl.ds(e0, 8), :]
                dar = dar + g_r * inr + g_i * ini
                dai = dai + g_i * inr - g_r * ini

                def mats(c, carry, u_ref=u_ref, dy_ref=dy_ref, du_ref=du_ref, sre=sre, sim=sim, are=are, aim=aim,
                         chunk=chunk):
                    xbr, xbi, xcr, xci = carry
                    r0 = pl.multiple_of(c * chunk, chunk)
                    u = u_ref[pl.ds(r0, chunk), :]
                    dy = dy_ref[pl.ds(r0, chunk), :]
                    g_r, g_i = are[pl.ds(r0, chunk), :], aim[pl.ds(r0, chunk), :]
                    du_ref[pl.ds(r0, chunk), :] += _dot(g_r, brT_ref[d], _NN) + _dot(g_i, biT_ref[d], _NN)
                    xbr = xbr + _dot(u, g_r, _TN)
                    xbi = xbi + _dot(u, g_i, _TN)
                    xcr = xcr + _dot(sre[pl.ds(r0, chunk), :], dy, _TN)
                    xci = xci - _dot(sim[pl.ds(r0, chunk), :], dy, _TN)
                    return xbr, xbi, xcr, xci

                dbr, dbi, dcr, dci = lax.fori_loop(0, n // chunk, mats, (dbr, dbi, dcr, dci))
            dar_ref[d] = jnp.sum(dar, axis=0, keepdims=True)
            dai_ref[d] = jnp.sum(dai, axis=0, keepdims=True)
            dbr_ref[d, 0] = dbr
            dbi_ref[d, 0] = dbi
            dcr_ref[d, 0] = dcr
            dci_ref[d, 0] = dci

    W, GP = dm.SSM_W, dm.GP
    return pl.pallas_call(
        body, name=name, grid=(nsb,),
        in_specs=[ul_s, uc_s, ul_s, uc_s, av, av, bd, bd, cd, cd, bdT, bdT, cdT, cdT, dv],
        out_specs=[ul_s, uc_s, av, av, dbd, dbd, dcd, dcd, dv],
        out_shape=[jax.ShapeDtypeStruct((L, W), F32), jax.ShapeDtypeStruct((C, W), F32),
                   jax.ShapeDtypeStruct((2, 1, GP), F32), jax.ShapeDtypeStruct((2, 1, GP), F32),
                   jax.ShapeDtypeStruct((2, nsb, LANES, PB), F32), jax.ShapeDtypeStruct((2, nsb, LANES, PB), F32),
                   jax.ShapeDtypeStruct((2, nsb, PB, LANES), F32), jax.ShapeDtypeStruct((2, nsb, PB, LANES), F32),
                   jax.ShapeDtypeStruct((1, W), F32)],
        scratch_shapes=[pltpu.VMEM((L, PB), F32), pltpu.VMEM((L, PB), F32),
                        pltpu.VMEM((C, PB), F32), pltpu.VMEM((C, PB), F32),
                        pltpu.VMEM((L, PB), F32), pltpu.VMEM((L, PB), F32),
                        pltpu.VMEM((C, PB), F32), pltpu.VMEM((C, PB), F32)],
        compiler_params=_cparams(("arbitrary",)),
    )(ul, uc, dyl, dyc, a_re, a_im, bd_re, bd_im, cd_re, cd_im, bdT_re, bdT_im, cdT_re, cdT_im, dskip)


def _glu(y, w, b, name):
    T, W = y.shape
    tm = _tile(T, 544, 16)

    def body(y_ref, w_ref, b_ref, o_ref):
        g = _gelu(y_ref[...])
        o_ref[...] = (g * _sigmoid(_dot(g, w_ref[...], _NN) + b_ref[...])).astype(BF16)

    return pl.pallas_call(
        body, name=name, grid=(T // tm,),
        in_specs=[pl.BlockSpec((tm, W), lambda i: (i, 0)), pl.BlockSpec((W, W), lambda i: (0, 0)),
                  pl.BlockSpec((1, W), lambda i: (0, 0))],
        out_specs=pl.BlockSpec((tm, W), lambda i: (i, 0)),
        out_shape=jax.ShapeDtypeStruct((T, W), BF16),
        compiler_params=_cparams(("parallel",)),
    )(y, w, b)


def _glu_bwd(y, w, b, do, do_col, name):
    T, W = y.shape
    tm = _tile(T, 544, 16)

    def body(y_ref, w_ref, b_ref, do_ref, dy_ref, dw_ref, db_ref):
        i = pl.program_id(0)

        @pl.when(i == 0)
        def _():
            dw_ref[...] = jnp.zeros_like(dw_ref)
            db_ref[...] = jnp.zeros_like(db_ref)

        y = y_ref[...]
        do = do_ref[...]
        g = _gelu(y)
        s = _sigmoid(_dot(g, w_ref[...], _NN) + b_ref[...])
        dpre = do * g * s * (1.0 - s)
        db_ref[...] += jnp.sum(dpre, axis=0, keepdims=True)
        dw_ref[...] += _dot(g, dpre, _TN)
        dg = do * s + _dot(dpre, w_ref[...], _NT)
        dy_ref[...] = dg * _gelu_grad(y)

    row = pl.BlockSpec((tm, W), lambda i: (i, 0))
    wsp = pl.BlockSpec((W, W), lambda i: (0, 0))
    vec = pl.BlockSpec((1, W), lambda i: (0, 0))
    return pl.pallas_call(
        body, name=name, grid=(T // tm,),
        in_specs=[row, wsp, vec, pl.BlockSpec((tm, W), lambda i: (i, do_col))], out_specs=[row, wsp, vec],
        out_shape=[jax.ShapeDtypeStruct((T, W), F32), jax.ShapeDtypeStruct((W, W), F32),
                   jax.ShapeDtypeStruct((1, W), F32)],
        compiler_params=_cparams(("arbitrary",)),
    )(y, w, b, do)


def _loss_head(X, g, target, C, name):
    T, D = X.shape
    tm = _tile(math.gcd(C, T - C), 512, 16)
    nc = C // tm

    def body(x_ref, g_ref, t_ref, loss_ref, dx_ref, dg_ref):
        i = pl.program_id(0)

        @pl.when(i == 0)
        def _():
            loss_ref[...] = jnp.zeros_like(loss_ref)
            dg_ref[...] = jnp.zeros_like(dg_ref)

        @pl.when(i < nc)
        def _():
            dx_ref[...] = jnp.zeros_like(dx_ref)

        @pl.when(i >= nc)
        def _():
            x = x_ref[...]
            gv = g_ref[...]
            r = lax.rsqrt(jnp.mean(x * x, axis=-1, keepdims=True) + NORM_EPS)
            xh = x * r
            err = xh * gv - t_ref[...]
            loss_ref[...] += 0.5 * jnp.sum(jnp.mean(err * err, axis=-1, keepdims=True), axis=0, keepdims=True)
            dy = err * (1.0 / D)
            dg_ref[...] += jnp.sum(dy * xh, axis=0, keepdims=True)
            dxh = dy * gv
            dx_ref[...] = r * (dxh - xh * jnp.mean(dxh * xh, axis=-1, keepdims=True))

    row = pl.BlockSpec((tm, D), lambda i: (i, 0))
    return pl.pallas_call(
        body, name=name, grid=(T // tm,),
        in_specs=[row, pl.BlockSpec((1, D), lambda i: (0, 0)),
                  pl.BlockSpec((tm, D), lambda i: (jnp.maximum(i - nc, 0), 0))],
        out_specs=[pl.BlockSpec((1, 1), lambda i: (0, 0)), row, pl.BlockSpec((1, D), lambda i: (0, 0))],
        out_shape=[jax.ShapeDtypeStruct((1, 1), F32), jax.ShapeDtypeStruct((T, D), F32),
                   jax.ShapeDtypeStruct((1, D), F32)],
        compiler_params=_cparams(("arbitrary",)),
    )(X, g, target)


def _ada_fwd(cond, w, b, name):
    nl, D, Ns = w.shape
    tn = _tile(Ns, 1024, LANES)

    def body(c_ref, w_ref, b_ref, o_ref):
        c = c_ref[...]
        o_ref[0] = _dot(c * _sigmoid(c), w_ref[0], _NN) + b_ref[0]

    return pl.pallas_call(
        body, name=name, grid=(nl, Ns // tn),
        in_specs=[pl.BlockSpec((16, D), lambda l, j: (0, 0)), pl.BlockSpec((1, D, tn), lambda l, j: (l, 0, j)),
                  pl.BlockSpec((1, 1, tn), lambda l, j: (l, 0, j))],
        out_specs=pl.BlockSpec((1, 16, tn), lambda l, j: (l, 0, j)),
        out_shape=jax.ShapeDtypeStruct((nl, 16, Ns), F32),
        compiler_params=_cparams(("parallel", "parallel")),
    )(cond, w, b)


def _ada_bwd(cond, w, dmod, name):
    nl, D, Ns = w.shape
    tn = _tile(Ns, 1024, LANES)

    def body(c_ref, w_ref, dm_ref, dw_ref, da_ref):
        j = pl.program_id(1)

        @pl.when(j == 0)
        def _():
            da_ref[...] = jnp.zeros_like(da_ref)

        c = c_ref[...]
        dw_ref[0] = _dot(c * _sigmoid(c), dm_ref[0], _TN)
        da_ref[0] += _dot(dm_ref[0], w_ref[0], _NT)

    return pl.pallas_call(
        body, name=name, grid=(nl, Ns // tn),
        in_specs=[pl.BlockSpec((16, D), lambda l, j: (0, 0)), pl.BlockSpec((1, D, tn), lambda l, j: (l, 0, j)),
                  pl.BlockSpec((1, 16, tn), lambda l, j: (l, 0, j))],
        out_specs=[pl.BlockSpec((1, D, tn), lambda l, j: (l, 0, j)), pl.BlockSpec((1, 16, D), lambda l, j: (l, 0, 0))],
        out_shape=[jax.ShapeDtypeStruct((nl, D, Ns), F32), jax.ShapeDtypeStruct((nl, 16, D), F32)],
        compiler_params=_cparams(("parallel", "arbitrary")),
    )(cond, w, dmod)


def _cctx_grad(c_ctx, dact4, name):
    nch, nl, _, D = dact4.shape

    def body(c_ref, da_ref, o_ref):
        acc = jnp.zeros((1, D), F32)
        for ch in range(nch):
            for l in range(nl):
                acc = acc + da_ref[ch, l, 8:9, :]
        c = c_ref[...]
        s = _sigmoid(c)
        o_ref[...] = acc * (s + c * s * (1.0 - s))

    return pl.pallas_call(body, name=name, out_shape=jax.ShapeDtypeStruct((1, D), F32),
                          compiler_params=_cparams())(c_ctx, dact4)


def _adamw(w, g, m, v, name):
    R, W = w.shape
    tr = _tile(R, max(8, (1 << 19) // W // 8 * 8), 8)
    c1 = 1.0 - ADAM_B1 ** ADAM_STEP
    c2 = 1.0 - ADAM_B2 ** ADAM_STEP

    def body(w_ref, g_ref, m_ref, v_ref, d_ref, nm_ref, nv_ref):
        g = g_ref[...]
        m = ADAM_B1 * m_ref[...] + (1.0 - ADAM_B1) * g
        v = ADAM_B2 * v_ref[...] + (1.0 - ADAM_B2) * (g * g)
        nm_ref[...] = m
        nv_ref[...] = v
        d_ref[...] = -ADAM_LR * ((m / c1) / (jnp.sqrt(v / c2) + ADAM_EPS) + ADAM_WD * w_ref[...])

    blk = pl.BlockSpec((tr, W), lambda i: (i, 0))
    s = jax.ShapeDtypeStruct((R, W), F32)
    return pl.pallas_call(
        body, name=name, grid=(R // tr,), in_specs=[blk] * 4, out_specs=[blk] * 3, out_shape=[s, s, s],
        compiler_params=_cparams(("parallel",)),
    )(w, g, m, v)


def _adamw_nd(w, g, m, v, name):
    shp = w.shape
    two = (math.prod(shp[:-1]), shp[-1])
    d, nm, nv = _adamw(w.reshape(two), g.reshape(two), m.reshape(two), v.reshape(two), name)
    return d.reshape(shp), nm.reshape(shp), nv.reshape(shp)


def _interleave(a, n):
    return a.reshape(SCAN_SEGMENTS, n // SCAN_SEGMENTS, -1).transpose(1, 0, 2).reshape(n, -1)


def _deinterleave(a, n):
    return a.reshape(n // SCAN_SEGMENTS, SCAN_SEGMENTS, -1).transpose(1, 0, 2).reshape(n, -1)


def _blockdiag_in(bbT, dm):
    eye = jnp.eye(dm.SG, dtype=F32)
    b4 = bbT.reshape(2, SSM_GROUP, dm.SG, SSM_STATE)
    return jnp.einsum("dhgp,gk->dghkp", b4, eye).reshape(2, dm.SSM_W, dm.GP)


def _blockdiag_in_T(dbd, dm):
    gpb = STATE_BLOCK // SSM_STATE
    per = (LANES // SSM_GROUP) // gpb
    d8 = dbd.reshape(2, dm.SSM_W // LANES, per, per, gpb, SSM_GROUP, gpb, SSM_STATE)
    x = jnp.einsum("duaabhbp->duabhp", d8).reshape(2, dm.SG, SSM_GROUP, SSM_STATE)
    return x.transpose(0, 2, 1, 3).reshape(2, SSM_GROUP, dm.GP)


def _blockdiag_out(c, dm):
    eye = jnp.eye(dm.SG, dtype=F32)
    return jnp.einsum("dghp,gk->dgpkh", c, eye).reshape(2, dm.GP, dm.SSM_W)


def _blockdiag_out_T(dcd, dm):
    gpb = STATE_BLOCK // SSM_STATE
    per = (LANES // SSM_GROUP) // gpb
    d8 = dcd.reshape(2, dm.SSM_W // LANES, per, gpb, SSM_STATE, per, gpb, SSM_GROUP)
    return jnp.einsum("duabpabh->duabhp", d8).reshape(2, dm.SG, SSM_GROUP, SSM_STATE)


_BIG = ("w_in", "w_out", "ssm_w_glu", "ffn_w_up", "ffn_w_down")
_SHARD_AXIS = {"w_in": 2, "w_out": 1, "ssm_w_glu": 1, "ffn_w_up": 2, "ffn_w_down": 1}
_SMALL = ("g_mix", "g_ffn", "attn_sink", "ssm_lambda_re", "ssm_lambda_im", "ssm_log_dt", "ssm_b_re", "ssm_b_im",
          "ssm_c_re", "ssm_c_im", "ssm_d", "ssm_b_glu", "gmlp_ln_g", "gmlp_ln_b", "gmlp_w_s", "gmlp_b_s",
          "ffn_conv_b", "g_final", "ffn_conv_w")
_WEIGHTS = ("c_ctx", "w_ada", "b_ada", "g_mix", "g_ffn", "w_in", "w_out", "attn_sink", "ssm_lambda_re",
            "ssm_lambda_im", "ssm_log_dt", "ssm_b_re", "ssm_b_im", "ssm_c_re", "ssm_c_im", "ssm_d", "ssm_w_glu",
            "ssm_b_glu", "gmlp_ln_g", "gmlp_ln_b", "gmlp_w_s", "gmlp_b_s", "ffn_w_up", "ffn_conv_w", "ffn_conv_b",
            "ffn_w_down", "g_final")


def _step(P, M, V, x, c, ctx, loss_target):
    dm = _Dims()
    D, T, C, L = dm.D, dm.T, dm.C, dm.L
    mx, my, mc = lax.axis_index("x"), lax.axis_index("y"), lax.axis_index("c")
    chip = 2 * mx + my
    dev = 2 * chip + mc

    conv_sh = P["ffn_conv_w"].shape
    small_f = _pack([c, P["ffn_conv_w"]], F32, 8)
    small_all = _allgather8(small_f, "ag_small_fwd")
    c_all, conv_all = [], []
    for d in range(N_DEV):
        cd_, cw_ = _unpack(small_all[d], [(1, D), conv_sh])
        c_all.append(cd_)
        conv_all.append(cw_)
    conv_w = jnp.concatenate([conv_all[2 * j] for j in range(N_CHIPS)], axis=2)
    cond = jnp.concatenate(c_all + [P["c_ctx"].reshape(1, D), jnp.zeros((16 - N_DEV - 1, D), F32)], axis=0)

    n_sh = P["w_ada"].shape[2]
    b_sh = lax.dynamic_slice_in_dim(P["b_ada"], chip * n_sh, n_sh, axis=1)[:, None, :]
    mods_sh = _ada_fwd(cond, P["w_ada"], b_sh, "ada_fwd")
    mods4 = _comm(mods_sh, group="chips", by_peer=False, name="ag_mods")
    mods = jnp.concatenate([mods4[j] for j in range(N_CHIPS)], axis=-1)
    mod_lat = lax.dynamic_index_in_dim(mods, dev, axis=1, keepdims=False)
    mod_ctx = mods[:, N_DEV]
    modp = jnp.stack([mod_ctx, mod_lat], axis=1).reshape(DEPTH, 2, N_MOD, D)

    w4 = _comm_multi([P[n].astype(BF16).reshape(-1, P[n].shape[-1]) for n in _BIG], group="chips", by_peer=False,
                     name="ag_w_chips")
    Wg = {n: a.reshape((N_CHIPS,) + P[n].shape) for n, a in zip(_BIG, w4)}

    cos, sin = _rope_tables(dm)
    Wl = [dict(w_in=jnp.concatenate([Wg["w_in"][j, l] for j in range(N_CHIPS)], axis=1),
               w_glu=Wg["ssm_w_glu"][:, l].reshape(dm.SSM_W, dm.SSM_W),
               w_out=_Mat(Wg["w_out"], axis=0, layer=l), w_dn=_Mat(Wg["ffn_w_down"], axis=0, layer=l),
               w_up=_Mat(Wg["ffn_w_up"], axis=1, layer=l)) for l in range(DEPTH)]

    X = jnp.concatenate([ctx.reshape(C, D), x.reshape(L, D)], axis=0)
    saved = []
    for l in range(DEPTH):
        mp = modp[l]
        sh_a, sc_a, gt_a, sh_f, sc_f, gt_f = [mp[:, k] for k in range(N_MOD)]
        w_in, w_out, w_glu, w_up, w_dn = [Wl[l][k] for k in ("w_in", "w_out", "w_glu", "w_up", "w_dn")]
        g_mix, g_ffn = P["g_mix"][l][None], P["g_ffn"][l][None]

        h = _normmod(X, g_mix, sh_a, sc_a, C, "normmod_a")
        z = _matmul(h, w_in, mode="nn", name="mm_in")
        q, k2, v2 = _attn_prep(z, cos, sin, dm, "attn_prep")
        sinkcol = jnp.repeat(P["attn_sink"][l].reshape(2, GQA_RATIO), ATTN_BLOCK, axis=1)[..., None]
        o_attn = _attention(q, k2, v2, sinkcol, dm, "attention")
        u = z[:, dm.OFF_S:dm.OFF_GU]
        ul, uc = _interleave(u[C:], L), _interleave(u[:C], C)
        lam_re = P["ssm_lambda_re"][l].reshape(2, 1, dm.GP)
        lam_im = P["ssm_lambda_im"][l].reshape(2, 1, dm.GP)
        ldt = jnp.repeat(P["ssm_log_dt"][l], SSM_STATE, axis=-1).reshape(2, 1, dm.GP)
        bT_re = P["ssm_b_re"][l].reshape(2, dm.GP, SSM_GROUP).transpose(0, 2, 1)
        bT_im = P["ssm_b_im"][l].reshape(2, dm.GP, SSM_GROUP).transpose(0, 2, 1)
        a_re, a_im, bbT_re, bbT_im = _s5_disc(lam_re, lam_im, ldt, bT_re, bT_im, "s5_disc")
        bd_re, bd_im = _blockdiag_in(bbT_re, dm).astype(BF16), _blockdiag_in(bbT_im, dm).astype(BF16)
        cd_re = _blockdiag_out(P["ssm_c_re"][l], dm).astype(BF16)
        cd_im = _blockdiag_out(P["ssm_c_im"][l], dm).astype(BF16)
        dskip = P["ssm_d"][l][None]
        yl, yc = _s5_scan(ul, uc, a_re, a_im, bd_re, bd_im, cd_re, cd_im, dskip, dm, "s5_scan")
        y = jnp.concatenate([_deinterleave(yc, C), _deinterleave(yl, L)], axis=0)
        b_glu = P["ssm_b_glu"][l][None]
        o_ssm = _glu(y, w_glu, b_glu, "glu")
        ln_g, ln_b = P["gmlp_ln_g"][l][None], P["gmlp_ln_b"][l][None]
        w_s = P["gmlp_w_s"][l].astype(BF16)
        b_s = P["gmlp_b_s"][l][..., None]
        o_gmlp = _gmlp(z, ln_g, ln_b, w_s, b_s, dm, "gmlp")
        o_cat = jnp.concatenate([o_attn, o_ssm, o_gmlp], axis=-1)
        mix, X1 = _matmul_residual(o_cat, w_out, X, gt_a, C, "mm_out")
        h2 = _normmod(X1, g_ffn, sh_f, sc_f, C, "normmod_f")
        up = _matmul(h2, w_up, mode="nn", name="mm_up", out_split=(1, 2))
        cw, cb = conv_w[l], P["ffn_conv_b"][l][None]
        act = _convact(up, cw, cb, C, "convact")
        ffn, X2 = _matmul_residual(act, w_dn, X1, gt_f, C, "mm_down")
        saved.append(dict(X=X, h=h, z=z, q=q, k2=k2, v2=v2, sinkcol=sinkcol, ul=ul, uc=uc, a_re=a_re, a_im=a_im,
                          bd_re=bd_re, bd_im=bd_im, cd_re=cd_re, cd_im=cd_im, y=y, o_cat=o_cat, mix=mix, X1=X1, h2=h2,
                          up=up, act=act, ffn=ffn, lam_re=lam_re, lam_im=lam_im, ldt=ldt, bT_re=bT_re,
                          bT_im=bT_im))
        X = X2

    loss_l, dX, dg_final = _loss_head(X, P["g_final"][None], loss_target.reshape(L, D), C, "loss_head")
    loss = lax.psum(loss_l[0, 0], ("x", "y", "c"))

    G = {n: [None] * DEPTH for n in _WEIGHTS if n not in ("c_ctx", "w_ada", "b_ada", "g_final")}
    dmod = [None] * DEPTH
    for l in reversed(range(DEPTH)):
        S = saved[l]
        mp = modp[l]
        sh_a, sc_a, gt_a, sh_f, sc_f, gt_f = [mp[:, k] for k in range(N_MOD)]
        w_in, w_out, w_glu, w_up, w_dn = [Wl[l][k] for k in ("w_in", "w_out", "w_glu", "w_up", "w_dn")]
        g_mix, g_ffn = P["g_mix"][l][None], P["g_ffn"][l][None]
        cw, cb = conv_w[l], P["ffn_conv_b"][l][None]

        dffn, dgt_f = _gate_bwd(dX, S["ffn"], gt_f, C, "gate_bwd")
        dact = _matmul(dffn, w_dn, mode="nt", name="mm_down_dx")
        G["ffn_w_down"][l] = _matmul(S["act"], dffn, mode="tn", name="mm_down_dw", out_dtype=BF16,
                                     out_split=(0, N_CHIPS), tm_cap=1408)
        dup, dcw, dcb = _convact_bwd(S["up"], cw, cb, dact, C, "convact_bwd")
        G["ffn_conv_w"][l], G["ffn_conv_b"][l] = dcw, dcb[0]
        dh2 = _matmul(_Mat(dup, axis=1), w_up, mode="nt", name="mm_up_dx")
        G["ffn_w_up"][l] = _matmul(S["h2"], _Mat(dup, axis=1), mode="tn", name="mm_up_dw", out_dtype=BF16,
                                   out_split=(1, N_CHIPS))
        dX1, dg_f, dsh_f, dsc_f = _normmod_bwd(S["X1"], g_ffn, sc_f, dh2, dX, C, "normmod_bwd")
        G["g_ffn"][l] = dg_f[0]
        dmix, dgt_a = _gate_bwd(dX1, S["mix"], gt_a, C, "gate_bwd")
        do = _matmul(dmix, w_out, mode="nt", name="mm_out_dx")
        G["w_out"][l] = _matmul(S["o_cat"], dmix, mode="tn", name="mm_out_dw", out_dtype=BF16,
                                out_split=(0, N_CHIPS))
        do_attn = do_ssm = do_gmlp = do
        ln_g, ln_b = P["gmlp_ln_g"][l][None], P["gmlp_ln_b"][l][None]
        w_s = P["gmlp_w_s"][l].astype(BF16)
        b_s = P["gmlp_b_s"][l][..., None]
        dz_g, dlg, dlb, dws, dbs = _gmlp_bwd(S["z"], ln_g, ln_b, w_s, b_s, do_gmlp, dm, "gmlp_bwd")
        G["gmlp_ln_g"][l], G["gmlp_ln_b"][l], G["gmlp_w_s"][l], G["gmlp_b_s"][l] = dlg[0], dlb[0], dws, dbs[..., 0]
        b_glu = P["ssm_b_glu"][l][None]
        dy, dwglu, dbglu = _glu_bwd(S["y"], w_glu, b_glu, do_ssm, dm.ATTN_W // dm.SSM_W, "glu_bwd")
        G["ssm_w_glu"][l], G["ssm_b_glu"][l] = dwglu.astype(BF16).reshape(N_CHIPS, -1, dm.SSM_W), dbglu[0]
        dyl, dyc = _interleave(dy[C:], L), _interleave(dy[:C], C)
        tr = lambda a: jnp.swapaxes(a, 1, 2)
        dskip = P["ssm_d"][l][None]
        (dul, duc, da_re, da_im, dbd_re, dbd_im, dcd_re, dcd_im, dd) = _s5_scan_bwd(
            S["ul"], S["uc"], dyl, dyc, S["a_re"], S["a_im"], S["bd_re"], S["bd_im"], S["cd_re"], S["cd_im"],
            tr(S["bd_re"]), tr(S["bd_im"]), tr(S["cd_re"]), tr(S["cd_im"]), dskip, dm, "s5_scan_bwd")
        dz_s = jnp.concatenate([_deinterleave(duc, C), _deinterleave(dul, L)], axis=0)
        dlr, dli, dldt, dbTr, dbTi = _s5_disc_bwd(S["lam_re"], S["lam_im"], S["ldt"], S["bT_re"], S["bT_im"],
                                                  da_re, da_im, _blockdiag_in_T(dbd_re, dm),
                                                  _blockdiag_in_T(dbd_im, dm), "s5_disc_bwd")
        G["ssm_lambda_re"][l] = dlr.reshape(2, dm.SG, SSM_STATE)
        G["ssm_lambda_im"][l] = dli.reshape(2, dm.SG, SSM_STATE)
        G["ssm_log_dt"][l] = jnp.sum(dldt.reshape(2, dm.SG, SSM_STATE), axis=-1)
        G["ssm_b_re"][l] = dbTr.transpose(0, 2, 1).reshape(2, dm.SG, SSM_STATE, SSM_GROUP)
        G["ssm_b_im"][l] = dbTi.transpose(0, 2, 1).reshape(2, dm.SG, SSM_STATE, SSM_GROUP)
        G["ssm_c_re"][l] = _blockdiag_out_T(dcd_re, dm)
        G["ssm_c_im"][l] = _blockdiag_out_T(dcd_im, dm)
        G["ssm_d"][l] = dd[0]
        dq, pk, pv, dkc, dvc, dsk = _attention_bwd(S["q"], S["k2"], S["v2"], S["sinkcol"], do_attn, dm, "attention_bwd")
        G["attn_sink"][l] = dsk[:, :, 0].reshape(dm.NH)
        dk2 = _kv_reduce(pk, dkc, dm, "kv_reduce")
        dv2 = _kv_reduce(pv, dvc, dm, "kv_reduce")
        dz_a = _attn_prep_bwd(dq, dk2, dv2, cos, sin, dm, "attn_prep_bwd")
        dz = jnp.concatenate([dz_a, dz_s, dz_g], axis=-1).astype(BF16)
        dh = _matmul(dz, w_in, mode="nt", name="mm_in_dx")
        dw_in = _matmul(S["h"], dz, mode="tn", name="mm_in_dw", out_dtype=BF16)
        n_in = dm.N_IN // N_CHIPS
        G["w_in"][l] = jnp.stack([dw_in[:, j * n_in:(j + 1) * n_in] for j in range(N_CHIPS)])
        dX, dg_a, dsh_a, dsc_a = _normmod_bwd(S["X"], g_mix, sc_a, dh, dX1, C, "normmod_bwd")
        G["g_mix"][l] = dg_a[0]
        dmod[l] = jnp.stack([dsh_a, dsc_a, dgt_a, dsh_f, dsc_f, dgt_f], axis=1)

    grad_x = dX[C:].reshape(1, L, D)
    Gbig = {n: G.pop(n) for n in _BIG}
    G = {n: jnp.stack(v) for n, v in G.items()}
    G["g_final"] = dg_final[0]
    dmod = jnp.stack(dmod)

    small_shapes = [G[n].shape for n in _SMALL]
    n_small = sum(math.prod(s) for s in small_shapes)
    r_small = -(-n_small // (8 * PACK_W)) * 8
    gs_pack = _pack([G[n] for n in _SMALL], F32, 8)
    dm_pack = _pack([dmod], F32, 8)
    r_dm = dm_pack.shape[0]
    allp = _allgather8(jnp.concatenate([gs_pack, dm_pack], axis=0), "ag_small_bwd")
    gs_sum = _sum_slots(allp[:, :r_small], F32, "sum_small")
    for n, a in zip(_SMALL, _unpack(gs_sum, small_shapes)):
        G[n] = a
    dmod_all = allp[:, r_small:r_small + r_dm].reshape(N_DEV, -1)[:, :DEPTH * 2 * N_MOD * D]
    dmod_all = dmod_all.reshape(N_DEV, DEPTH, 2, N_MOD * D)
    dmod_ctx = _sum_slots(dmod_all[:, :, 0].reshape(N_DEV, DEPTH, N_MOD * D), F32, "sum_dmod_ctx")
    dmod16 = jnp.concatenate([jnp.swapaxes(dmod_all[:, :, 1], 0, 1), dmod_ctx[:, None],
                              jnp.zeros((DEPTH, 16 - N_DEV - 1, N_MOD * D), F32)], axis=1)
    G["b_ada"] = _sum_slots(jnp.swapaxes(dmod16, 0, 1)[:N_DEV + 1], F32, "sum_b_ada")
    dmod16_sh = lax.dynamic_slice_in_dim(dmod16, chip * n_sh, n_sh, axis=2)
    G["w_ada"], dact = _ada_bwd(cond, P["w_ada"], dmod16_sh, "ada_bwd")
    dact4 = _comm(dact, group="chips", by_peer=False, name="ag_dact")
    G["c_ctx"] = _cctx_grad(P["c_ctx"].reshape(1, D), dact4, "cctx_grad")[0]
    conv_cols = conv_sh[2]
    G["ffn_conv_w"] = lax.dynamic_slice_in_dim(G["ffn_conv_w"], chip * conv_cols, conv_cols, axis=2)

    LH = DEPTH // 2
    xs = [jnp.stack([jnp.stack([jnp.concatenate([Gbig[n][hh * LH + li][j] for li in range(LH)], axis=0)
                                for j in range(N_CHIPS)]) for hh in range(2)]) for n in _BIG]
    parts = _comm_multi(xs, group="all", by_peer=True, name="rs_all")
    for n, part in zip(_BIG, parts):
        red = _sum_slots(part, BF16, "sum_all")
        both = _comm(red, group="sib", by_peer=False, name="rs_share")
        G[n] = both.reshape(P[n].shape).astype(F32)

    delta, new_m, new_v = {}, {}, {}
    for n in _BIG + ("w_ada",):
        delta[n], new_m[n], new_v[n] = _adamw_nd(P[n], G[n], M[n], V[n], "adamw_" + n)
    rest = [n for n in _WEIGHTS if n not in _BIG + ("w_ada",)]
    shapes = [P[n].shape for n in rest]
    packed = [_pack([d[n] for n in rest], F32, 8) for d in (P, G, M, V)]
    outs = _adamw(*packed, "adamw_small")
    for dst, buf in zip((delta, new_m, new_v), outs):
        for n, a in zip(rest, _unpack(buf, shapes)):
            dst[n] = a
    return loss, grad_x, G, delta, new_m, new_v


def kernel(x, c, ctx, c_ctx, w_ada, b_ada, g_mix, g_ffn, w_in, w_out, attn_sink, ssm_lambda_re, ssm_lambda_im, ssm_log_dt, ssm_b_re, ssm_b_im, ssm_c_re, ssm_c_im, ssm_d, ssm_w_glu, ssm_b_glu, gmlp_ln_g, gmlp_ln_b, gmlp_w_s, gmlp_b_s, ffn_w_up, ffn_conv_w, ffn_conv_b, ffn_w_down, g_final, loss_target, m_c_ctx, m_w_ada, m_b_ada, m_g_mix, m_g_ffn, m_w_in, m_w_out, m_attn_sink, m_ssm_lambda_re, m_ssm_lambda_im, m_ssm_log_dt, m_ssm_b_re, m_ssm_b_im, m_ssm_c_re, m_ssm_c_im, m_ssm_d, m_ssm_w_glu, m_ssm_b_glu, m_gmlp_ln_g, m_gmlp_ln_b, m_gmlp_w_s, m_gmlp_b_s, m_ffn_w_up, m_ffn_conv_w, m_ffn_conv_b, m_ffn_w_down, m_g_final, v_c_ctx, v_w_ada, v_b_ada, v_g_mix, v_g_ffn, v_w_in, v_w_out, v_attn_sink, v_ssm_lambda_re, v_ssm_lambda_im, v_ssm_log_dt, v_ssm_b_re, v_ssm_b_im, v_ssm_c_re, v_ssm_c_im, v_ssm_d, v_ssm_w_glu, v_ssm_b_glu, v_gmlp_ln_g, v_gmlp_ln_b, v_gmlp_w_s, v_gmlp_b_s, v_ffn_w_up, v_ffn_conv_w, v_ffn_conv_b, v_ffn_w_down, v_g_final):
    env = locals()
    P = {n: env[n] for n in _WEIGHTS}
    M = {n: env["m_" + n] for n in _WEIGHTS}
    V = {n: env["v_" + n] for n in _WEIGHTS}
    loss, grad_x, G, delta, new_m, new_v = _step(P, M, V, x, c, ctx, loss_target)
    return (loss, grad_x, *[G[n] for n in _WEIGHTS], *[delta[n] for n in _WEIGHTS],
            *[new_m[n] for n in _WEIGHTS], *[new_v[n] for n in _WEIGHTS])
```

```python
import functools
import math

import jax
import jax.numpy as jnp
from jax import lax
from jax.experimental import pallas as pl
from jax.experimental.pallas import tpu as pltpu

F32 = jnp.float32
BF16 = jnp.bfloat16

D_MODEL = 2048
SEQ = 4096
DEPTH = 4
CTX_LEN = 256
GRID_W = 64
HEAD_DIM = 64
GQA_RATIO = 8
WINDOW = 128
ATTN_BLOCK = 128
ROPE_BASE = 10000.0
SSM_GROUP = 16
SSM_STATE = 64
GMLP_CHUNK = 128
GMLP_GROUP_W = 128
N_MOD = 6
NORM_EPS = 1e-6
ADAM_LR = 0.001
ADAM_B1 = 0.9
ADAM_B2 = 0.999
ADAM_EPS = 1e-08
ADAM_WD = 0.01
ADAM_STEP = 10

N_CHIPS = 4
N_DEV = 8
SCAN_SEGMENTS = 8
STATE_BLOCK = 256
STATE_BLOCK_FWD = 512
LANES = 128
PACK_W = 1024
COMM_CHUNKS = (16, 8, 4, 2)
STAGE_BYTES = 4 << 20
SUM_BLOCK_BYTES = 8 << 20
VMEM_LIMIT_MB = 56
_GELU_K = math.sqrt(2.0 / math.pi)
_GELU_C = 0.044715


class _Dims:
    def __init__(self):
        self.D = D_MODEL
        self.L = SEQ
        self.C = CTX_LEN
        self.T = SEQ + CTX_LEN
        self.ATTN_W = D_MODEL // 2
        self.SSM_W = D_MODEL // 4
        self.GMLP_W = D_MODEL - self.ATTN_W - self.SSM_W
        self.NH = self.ATTN_W // HEAD_DIM
        self.NKV = self.NH // GQA_RATIO
        self.KV_W = self.NKV * HEAD_DIM
        self.SG = self.SSM_W // SSM_GROUP
        self.GP = self.SG * SSM_STATE
        self.GG = self.GMLP_W // GMLP_GROUP_W
        self.DFF = ((8 * D_MODEL // 3 + 255) // 256) * 256
        self.OFF_K = self.ATTN_W
        self.OFF_V = self.OFF_K + self.KV_W
        self.OFF_S = self.OFF_V + self.KV_W
        self.OFF_GU = self.OFF_S + self.SSM_W
        self.OFF_GV = self.OFF_GU + self.GMLP_W
        self.N_IN = self.OFF_GV + self.GMLP_W
        assert self.NKV == 2 and self.KV_W == LANES
        assert self.C % (SCAN_SEGMENTS * 8) == 0 and self.L % (SCAN_SEGMENTS * 8) == 0
        assert self.C % ATTN_BLOCK == 0 and self.L % ATTN_BLOCK == 0


def _tile(n, cap, mult):
    best = None
    for t in range(mult, min(n, cap) + 1, mult):
        if n % t == 0:
            best = t
    return best if best is not None else n


def _cparams(sem=None):
    kw = dict(vmem_limit_bytes=VMEM_LIMIT_MB << 20)
    if sem is not None:
        kw["dimension_semantics"] = sem
    return pltpu.CompilerParams(**kw)


def _gelu(x):
    return 0.5 * x * (1.0 + jnp.tanh(_GELU_K * (x + _GELU_C * x * x * x)))


def _gelu_grad(x):
    t = jnp.tanh(_GELU_K * (x + _GELU_C * x * x * x))
    return 0.5 * (1.0 + t) + 0.5 * x * (1.0 - t * t) * _GELU_K * (1.0 + 3.0 * _GELU_C * x * x)


def _sigmoid(x):
    return 1.0 / (1.0 + jnp.exp(-x))


def _dot(a, b, dims):
    return lax.dot_general(a.astype(BF16), b.astype(BF16), (dims, ((), ())), preferred_element_type=F32)


_NN = ((1,), (0,))
_NT = ((1,), (1,))
_TN = ((0,), (0,))


def _sib_staged(x, by_peer, name):
    rows, W = x.shape[-2:]
    cr = _tile(rows, max(16, STAGE_BYTES // (W * x.dtype.itemsize) // 16 * 16), 16)
    nch = rows // cr

    def body(x_ref, o_ref, sbuf, rbuf, send_sems, recv_sems, ld_sem, st_sem, own_sem, credit_sem):
        mx, my, mc = lax.axis_index("x"), lax.axis_index("y"), lax.axis_index("c")
        sib = (mx, my, 1 - mc)
        mine = x_ref.at[mc] if by_peer else x_ref
        theirs = x_ref.at[1 - mc] if by_peer else x_ref
        own = pltpu.make_async_copy(mine, o_ref.at[mc], own_sem)
        own.start()

        def step(i, carry):
            s = i % 2
            r0 = pl.multiple_of(i * cr, 16)
            ld = pltpu.make_async_copy(theirs.at[pl.ds(r0, cr), :], sbuf.at[s], ld_sem)
            ld.start()
            ld.wait()

            @pl.when(i >= 2)
            def _():
                pl.semaphore_wait(credit_sem, 1)

            cp = pltpu.make_async_remote_copy(src_ref=sbuf.at[s], dst_ref=rbuf.at[s], send_sem=send_sems.at[s],
                                              recv_sem=recv_sems.at[s], device_id=sib,
                                              device_id_type=pl.DeviceIdType.MESH)
            cp.start()
            cp.wait_recv()
            st = pltpu.make_async_copy(rbuf.at[s], o_ref.at[1 - mc, pl.ds(r0, cr), :], st_sem)
            st.start()
            st.wait()

            @pl.when(i < nch - 2)
            def _():
                pl.semaphore_signal(credit_sem, inc=1, device_id=sib, device_id_type=pl.DeviceIdType.MESH)

            cp.wait_send()
            return carry

        lax.fori_loop(0, nch, step, 0)
        own.wait()

    return pl.pallas_call(
        body, name=name,
        out_shape=jax.ShapeDtypeStruct((2, rows, W), x.dtype),
        in_specs=[pl.BlockSpec(memory_space=pl.ANY)],
        out_specs=pl.BlockSpec(memory_space=pl.ANY),
        scratch_shapes=[pltpu.VMEM((2, cr, W), x.dtype), pltpu.VMEM((2, cr, W), x.dtype),
                        pltpu.SemaphoreType.DMA((2,)), pltpu.SemaphoreType.DMA((2,)),
                        pltpu.SemaphoreType.DMA, pltpu.SemaphoreType.DMA, pltpu.SemaphoreType.DMA,
                        pltpu.SemaphoreType.REGULAR],
        compiler_params=_cparams(),
    )(x)


def _comm(x, *, group, by_peer, name):
    n = {"sib": 2, "chips": N_CHIPS, "all": N_DEV}[group]
    blk = x.shape[1:] if by_peer else x.shape
    npeer = n - 1
    W = blk[-1]
    rows = math.prod(blk[:-1])
    x2 = x.reshape((n, rows, W) if by_peer else (rows, W))
    if group == "sib":
        return _sib_staged(x2, by_peer, name).reshape((n,) + tuple(blk))
    nchunk = next((k for k in COMM_CHUNKS if rows % (k * 16) == 0), 1)
    cr = rows // nchunk

    def body(x_ref, o_ref, send_sems, recv_sems, local_sem):
        mx, my, mc = lax.axis_index("x"), lax.axis_index("y"), lax.axis_index("c")
        if group == "chips":
            me = 2 * mx + my
            peers = [((mx ^ fx, my ^ fy, mc), 2 * (mx ^ fx) + (my ^ fy)) for fx, fy in ((1, 0), (0, 1), (1, 1))]
        else:
            me = 4 * mx + 2 * my + mc
            peers = [((mx ^ fx, my ^ fy, mc ^ fc), 4 * (mx ^ fx) + 2 * (my ^ fy) + (mc ^ fc))
                     for fx in (0, 1) for fy in (0, 1) for fc in (0, 1) if fx + fy + fc]

        def src_of(pid):
            return x_ref.at[pid] if by_peer else x_ref

        def chunk(ref, ch):
            return ref.at[pl.ds(ch * cr, cr), :]

        def remote(k, dev, pid, ch=None):
            s, d = src_of(pid), o_ref.at[me]
            if ch is not None:
                s, d = chunk(s, ch), chunk(d, ch)
            return pltpu.make_async_remote_copy(src_ref=s, dst_ref=d, send_sem=send_sems.at[k],
                                                recv_sem=recv_sems.at[k], device_id=dev,
                                                device_id_type=pl.DeviceIdType.MESH)

        for ch in range(nchunk):
            pltpu.make_async_copy(chunk(src_of(me), ch), chunk(o_ref.at[me], ch), local_sem).start()
            for k, (dev, pid) in enumerate(peers):
                remote(k, dev, pid, ch).start()
        for k, (dev, pid) in enumerate(peers):
            remote(k, dev, pid).wait_recv()
        for k, (dev, pid) in enumerate(peers):
            remote(k, dev, pid).wait_send()
        pltpu.make_async_copy(src_of(me), o_ref.at[me], local_sem).wait()

    out = pl.pallas_call(
        body, name=name,
        out_shape=jax.ShapeDtypeStruct((n, rows, W), x.dtype),
        in_specs=[pl.BlockSpec(memory_space=pl.ANY)],
        out_specs=pl.BlockSpec(memory_space=pl.ANY),
        scratch_shapes=[pltpu.SemaphoreType.DMA((npeer,)), pltpu.SemaphoreType.DMA((npeer,)),
                        pltpu.SemaphoreType.DMA],
    )(x2)
    return out.reshape((n,) + tuple(blk))


def _comm_multi(xs, *, group, by_peer, name):
    nt = len(xs)
    n = {"chips": N_CHIPS, "all": N_DEV}[group]
    npeer = n - 1

    def body(*refs):
        x_refs, o_refs = refs[:nt], refs[nt:2 * nt]
        send_sems, recv_sems, local_sems = refs[2 * nt:]
        mx, my, mc = lax.axis_index("x"), lax.axis_index("y"), lax.axis_index("c")
        if group == "chips":
            me = 2 * mx + my
            masks = [(fx, fy, 0) for fx, fy in ((1, 0), (0, 1), (1, 1))]
        else:
            me = 4 * mx + 2 * my + mc
            masks = [(fx, fy, fc) for fx in (0, 1) for fy in (0, 1) for fc in (0, 1) if fx + fy + fc]

        def src(t, px, py, pc):
            return x_refs[t].at[pc, 2 * px + py] if by_peer else x_refs[t]

        copies = []
        for t in range(nt):
            pltpu.make_async_copy(src(t, mx, my, mc), o_refs[t].at[me], local_sems.at[t]).start()
            for k, (fx, fy, fc) in enumerate(masks):
                px, py, pc = mx ^ fx, my ^ fy, mc ^ fc
                cp = pltpu.make_async_remote_copy(
                    src_ref=src(t, px, py, pc), dst_ref=o_refs[t].at[me],
                    send_sem=send_sems.at[t * npeer + k], recv_sem=recv_sems.at[t * npeer + k],
                    device_id=(px, py, pc), device_id_type=pl.DeviceIdType.MESH)
                cp.start()
                copies.append(cp)
        for cp in copies:
            cp.wait_recv()
        for cp in copies:
            cp.wait_send()
        for t in range(nt):
            pltpu.make_async_copy(src(t, mx, my, mc), o_refs[t].at[me], local_sems.at[t]).wait()

    any_spec = pl.BlockSpec(memory_space=pl.ANY)
    return pl.pallas_call(
        body, name=name,
        out_shape=[jax.ShapeDtypeStruct((n,) + tuple(x.shape[-2:]), x.dtype) for x in xs],
        in_specs=[any_spec] * nt, out_specs=[any_spec] * nt,
        scratch_shapes=[pltpu.SemaphoreType.DMA((nt * npeer,)), pltpu.SemaphoreType.DMA((nt * npeer,)),
                        pltpu.SemaphoreType.DMA((nt,))],
    )(*xs)


def _allgather8(x, name):
    return _comm(x, group="all", by_peer=False, name=name)


def _sum_slots(x, out_dtype, name):
    n, R, W = x.shape
    tr = _tile(R, max(16, min(512, SUM_BLOCK_BYTES // (n * W * x.dtype.itemsize) // 16 * 16)), 16)

    def body(x_ref, o_ref):
        acc = x_ref[0].astype(F32)
        for s in range(1, n):
            acc = acc + x_ref[s].astype(F32)
        o_ref[...] = acc.astype(o_ref.dtype)

    return pl.pallas_call(
        body, name=name, grid=(R // tr,),
        in_specs=[pl.BlockSpec((n, tr, W), lambda i: (0, i, 0))],
        out_specs=pl.BlockSpec((tr, W), lambda i: (i, 0)),
        out_shape=jax.ShapeDtypeStruct((R, W), out_dtype),
        compiler_params=_cparams(("parallel",)),
    )(x)


def _pack(arrs, dtype, row_mult):
    flat = jnp.concatenate([a.reshape(-1).astype(dtype) for a in arrs])
    n = flat.shape[0]
    quant = row_mult * PACK_W
    total = -(-n // quant) * quant
    if total != n:
        flat = jnp.concatenate([flat, jnp.zeros((total - n,), dtype)])
    return flat.reshape(total // PACK_W, PACK_W)


def _unpack(buf, shapes):
    flat = buf.reshape(-1)
    out, off = [], 0
    for s in shapes:
        n = math.prod(s)
        out.append(flat[off:off + n].reshape(s))
        off += n
    return out


class _Mat:
    def __init__(self, arr, axis=None, layer=None, part0=0, nparts=None):
        self.arr, self.axis, self.layer, self.part0 = arr, axis, layer, part0
        self.r, self.c = arr.shape[-2:]
        self.nparts = 1 if axis is None else (arr.shape[0] if nparts is None else nparts)

    @property
    def shape(self):
        return (self.r * (self.nparts if self.axis == 0 else 1), self.c * (self.nparts if self.axis == 1 else 1))

    def unit(self, ax):
        return self.r if ax == 0 else self.c

    def spec(self, tr, tc, fn):
        def imap(*g):
            rb, cb = fn(*g)
            lead = []
            if self.axis == 0:
                per = self.r // tr
                lead.append(self.part0 + rb // per)
                rb = rb % per
            elif self.axis == 1:
                per = self.c // tc
                lead.append(self.part0 + cb // per)
                cb = cb % per
            if self.layer is not None:
                lead.append(self.layer)
            return (*lead, rb, cb)

        return pl.BlockSpec((None,) * (self.arr.ndim - 2) + (tr, tc), imap)


def _as_mat(x):
    return x if isinstance(x, _Mat) else _Mat(x)


def _matmul(a, b, *, mode, name, out_dtype=F32, add=None, out_split=None, tm_cap=1088, tn_cap=1536, tk_cap=2048):
    a, b = _as_mat(a), _as_mat(b)
    am, ak = (0, 1) if mode != "tn" else (1, 0)
    bk, bn = (0, 1) if mode != "nt" else (1, 0)
    M, K, N = a.shape[am], a.shape[ak], b.shape[bn]
    assert b.shape[bk] == K
    if out_split is None:
        o = _Mat(jax.ShapeDtypeStruct((M, N), out_dtype))
    else:
        ax, parts = out_split
        o = _Mat(jax.ShapeDtypeStruct((parts, M // parts if ax == 0 else M, N // parts if ax == 1 else N), out_dtype),
                 axis=ax)
    tm = _tile(math.gcd(a.unit(am), o.unit(0)), tm_cap, 16 if mode != "tn" else LANES)
    tn = _tile(math.gcd(b.unit(bn), o.unit(1)), tn_cap, LANES)
    tk = _tile(math.gcd(a.unit(ak), b.unit(bk)), tk_cap, LANES if mode != "tn" else 16)
    nk = K // tk
    dims = {"nn": _NN, "nt": _NT, "tn": _TN}[mode]

    def body(*refs):
        if add is None:
            a_ref, b_ref, o_ref, acc_ref = refs
            add_ref = None
        else:
            a_ref, b_ref, add_ref, o_ref, acc_ref = refs
        k = pl.program_id(2)

        @pl.when(k == 0)
        def _():
            acc_ref[...] = jnp.zeros_like(acc_ref)

        acc_ref[...] += _dot(a_ref[...], b_ref[...], dims)

        @pl.when(k == nk - 1)
        def _():
            r = acc_ref[...]
            if add_ref is not None:
                r = r + add_ref[...].astype(F32)
            o_ref[...] = r.astype(o_ref.dtype)

    if mode == "tn":
        a_spec = a.spec(tk, tm, lambda i, j, k: (k, i))
    else:
        a_spec = a.spec(tm, tk, lambda i, j, k: (i, k))
    if mode == "nt":
        b_spec = b.spec(tn, tk, lambda i, j, k: (j, k))
    else:
        b_spec = b.spec(tk, tn, lambda i, j, k: (k, j))
    o_spec = o.spec(tm, tn, lambda i, j, k: (i, j))
    in_specs = [a_spec, b_spec] + ([pl.BlockSpec((tm, tn), lambda i, j, k: (i, j))] if add is not None else [])
    args = (a.arr, b.arr) + ((add,) if add is not None else ())
    return pl.pallas_call(
        body, name=name, grid=(M // tm, N // tn, nk),
        in_specs=in_specs, out_specs=o_spec,
        out_shape=o.arr,
        scratch_shapes=[pltpu.VMEM((tm, tn), F32)],
        compiler_params=_cparams(("parallel", "parallel", "arbitrary")),
    )(*args)


def _matmul_residual(a, w, res, gate, C, name):
    w = _as_mat(w)
    M, K = a.shape
    N = w.shape[1]
    tm = _tile(M, 1088, 16)
    tn = _tile(w.unit(1), 1024, LANES)
    tk = _tile(w.unit(0), 2048, LANES)
    nk = K // tk

    def body(a_ref, w_ref, res_ref, gate_ref, mix_ref, o_ref, acc_ref):
        i, k = pl.program_id(0), pl.program_id(2)

        @pl.when(k == 0)
        def _():
            acc_ref[...] = jnp.zeros_like(acc_ref)

        acc_ref[...] += _dot(a_ref[...], w_ref[...], _NN)

        @pl.when(k == nk - 1)
        def _():
            r = acc_ref[...]
            row = i * tm + lax.broadcasted_iota(jnp.int32, (tm, 1), 0)
            g = jnp.where(row < C, gate_ref[0:1, :], gate_ref[1:2, :])
            mix_ref[...] = r
            o_ref[...] = res_ref[...] + g * r

    o_spec = pl.BlockSpec((tm, tn), lambda i, j, k: (i, j))
    return pl.pallas_call(
        body, name=name, grid=(M // tm, N // tn, nk),
        in_specs=[pl.BlockSpec((tm, tk), lambda i, j, k: (i, k)),
                  w.spec(tk, tn, lambda i, j, k: (k, j)),
                  o_spec,
                  pl.BlockSpec((2, tn), lambda i, j, k: (0, j))],
        out_specs=[o_spec, o_spec],
        out_shape=[jax.ShapeDtypeStruct((M, N), F32), jax.ShapeDtypeStruct((M, N), F32)],
        scratch_shapes=[pltpu.VMEM((tm, tn), F32)],
        compiler_params=_cparams(("parallel", "parallel", "arbitrary")),
    )(a, w.arr, res, gate)


def _seg_select(i, tm, C, ref):
    row = i * tm + lax.broadcasted_iota(jnp.int32, (tm, 1), 0)
    return row < C, jnp.where(row < C, ref[0:1, :], ref[1:2, :])


def _normmod(X, g, shift, scale, C, name):
    T, D = X.shape
    tm = _tile(T, 512, 16)

    def body(x_ref, g_ref, sh_ref, sc_ref, h_ref):
        i = pl.program_id(0)
        x = x_ref[...]
        r = lax.rsqrt(jnp.mean(x * x, axis=-1, keepdims=True) + NORM_EPS)
        _, sh = _seg_select(i, tm, C, sh_ref)
        _, sc = _seg_select(i, tm, C, sc_ref)
        h_ref[...] = ((x * r) * g_ref[...] * (1.0 + sc) + sh).astype(BF16)

    row = pl.BlockSpec((tm, D), lambda i: (i, 0))
    vec = pl.BlockSpec((1, D), lambda i: (0, 0))
    two = pl.BlockSpec((2, D), lambda i: (0, 0))
    return pl.pallas_call(
        body, name=name, grid=(T // tm,),
        in_specs=[row, vec, two, two], out_specs=row,
        out_shape=jax.ShapeDtypeStruct((T, D), BF16),
        compiler_params=_cparams(("parallel",)),
    )(X, g, shift, scale)


def _normmod_bwd(X, g, scale, dh, dres, C, name):
    T, D = X.shape
    tm = _tile(T, 512, 16)

    def body(x_ref, g_ref, sc_ref, dh_ref, dres_ref, dx_ref, dg_ref, dsh_ref, dsc_ref):
        i = pl.program_id(0)

        @pl.when(i == 0)
        def _():
            dg_ref[...] = jnp.zeros_like(dg_ref)
            dsh_ref[...] = jnp.zeros_like(dsh_ref)
            dsc_ref[...] = jnp.zeros_like(dsc_ref)

        x = x_ref[...]
        dh = dh_ref[...]
        gv = g_ref[...]
        r = lax.rsqrt(jnp.mean(x * x, axis=-1, keepdims=True) + NORM_EPS)
        xh = x * r
        isc, sc = _seg_select(i, tm, C, sc_ref)
        n = xh * gv
        dhn = dh * n
        zero = jnp.zeros_like(dh)
        dsh_ref[0:1, :] += jnp.sum(jnp.where(isc, dh, zero), axis=0, keepdims=True)
        dsh_ref[1:2, :] += jnp.sum(jnp.where(isc, zero, dh), axis=0, keepdims=True)
        dsc_ref[0:1, :] += jnp.sum(jnp.where(isc, dhn, zero), axis=0, keepdims=True)
        dsc_ref[1:2, :] += jnp.sum(jnp.where(isc, zero, dhn), axis=0, keepdims=True)
        dn = dh * (1.0 + sc)
        dg_ref[...] += jnp.sum(dn * xh, axis=0, keepdims=True)
        dxh = dn * gv
        dx_ref[...] = dres_ref[...] + r * (dxh - xh * jnp.mean(dxh * xh, axis=-1, keepdims=True))

    row = pl.BlockSpec((tm, D), lambda i: (i, 0))
    vec = pl.BlockSpec((1, D), lambda i: (0, 0))
    two = pl.BlockSpec((2, D), lambda i: (0, 0))
    return pl.pallas_call(
        body, name=name, grid=(T // tm,),
        in_specs=[row, vec, two, row, row], out_specs=[row, vec, two, two],
        out_shape=[jax.ShapeDtypeStruct((T, D), F32), jax.ShapeDtypeStruct((1, D), F32),
                   jax.ShapeDtypeStruct((2, D), F32), jax.ShapeDtypeStruct((2, D), F32)],
        compiler_params=_cparams(("arbitrary",)),
    )(X, g, scale, dh, dres)


def _gate_bwd(dxo, mix, gate, C, name):
    T, D = dxo.shape
    tm = _tile(T, 512, 16)

    def body(dx_ref, mix_ref, gate_ref, dmix_ref, dgate_ref):
        i = pl.program_id(0)

        @pl.when(i == 0)
        def _():
            dgate_ref[...] = jnp.zeros_like(dgate_ref)

        dx = dx_ref[...]
        isc, gt = _seg_select(i, tm, C, gate_ref)
        dmix_ref[...] = (dx * gt).astype(BF16)
        p = dx * mix_ref[...]
        zero = jnp.zeros_like(p)
        dgate_ref[0:1, :] += jnp.sum(jnp.where(isc, p, zero), axis=0, keepdims=True)
        dgate_ref[1:2, :] += jnp.sum(jnp.where(isc, zero, p), axis=0, keepdims=True)

    row = pl.BlockSpec((tm, D), lambda i: (i, 0))
    two = pl.BlockSpec((2, D), lambda i: (0, 0))
    return pl.pallas_call(
        body, name=name, grid=(T // tm,),
        in_specs=[row, row, two], out_specs=[row, two],
        out_shape=[jax.ShapeDtypeStruct((T, D), BF16), jax.ShapeDtypeStruct((2, D), F32)],
        compiler_params=_cparams(("arbitrary",)),
    )(dxo, mix, gate)


def _conv_masks(T, C):
    row = lax.broadcasted_iota(jnp.int32, (T, 1), 0)
    first = (row == 0) | (row == C)
    last = (row == C - 1) | (row == T - 1)
    return first, last


def _shift_rows(x, first, last):
    T = x.shape[0]
    prev = jnp.where(first, 0.0, pltpu.roll(x, 1, 0))
    nxt = jnp.where(last, 0.0, pltpu.roll(x, T - 1, 0))
    return prev, nxt


def _convact(up, cw, cb, C, name):
    _, T, F = up.shape
    tc = LANES

    def body(u_ref, w_ref, b_ref, o_ref):
        gp = u_ref[0].astype(F32)
        first, last = _conv_masks(T, C)
        prev, nxt = _shift_rows(gp, first, last)
        gc = prev * w_ref[0:1, :] + gp * w_ref[1:2, :] + nxt * w_ref[2:3, :] + b_ref[...]
        o_ref[...] = (gc * _sigmoid(gc) * u_ref[1].astype(F32)).astype(BF16)

    col = pl.BlockSpec((T, tc), lambda j: (0, j))
    return pl.pallas_call(
        body, name=name, grid=(F // tc,),
        in_specs=[pl.BlockSpec((2, T, tc), lambda j: (0, 0, j)), pl.BlockSpec((3, tc), lambda j: (0, j)),
                  pl.BlockSpec((1, tc), lambda j: (0, j))],
        out_specs=col, out_shape=jax.ShapeDtypeStruct((T, F), BF16),
        compiler_params=_cparams(("parallel",)),
    )(up, cw, cb)


def _convact_bwd(up, cw, cb, dact, C, name):
    _, T, F = up.shape
    tc = LANES

    def body(u_ref, w_ref, b_ref, da_ref, du_ref, dw_ref, db_ref):
        gp = u_ref[0].astype(F32)
        val = u_ref[1].astype(F32)
        da = da_ref[...].astype(F32)
        first, last = _conv_masks(T, C)
        prev, nxt = _shift_rows(gp, first, last)
        w0, w1, w2 = w_ref[0:1, :], w_ref[1:2, :], w_ref[2:3, :]
        gc = prev * w0 + gp * w1 + nxt * w2 + b_ref[...]
        s = _sigmoid(gc)
        du_ref[1] = (da * gc * s).astype(BF16)
        dgc = da * val * (s + gc * s * (1.0 - s))
        db_ref[...] = jnp.sum(dgc, axis=0, keepdims=True)
        dw_ref[0:1, :] = jnp.sum(dgc * prev, axis=0, keepdims=True)
        dw_ref[1:2, :] = jnp.sum(dgc * gp, axis=0, keepdims=True)
        dw_ref[2:3, :] = jnp.sum(dgc * nxt, axis=0, keepdims=True)
        dprev, dnxt = _shift_rows(dgc, first, last)
        du_ref[0] = (dnxt * w0 + dgc * w1 + dprev * w2).astype(BF16)

    col = pl.BlockSpec((T, tc), lambda j: (0, j))
    two = pl.BlockSpec((2, T, tc), lambda j: (0, 0, j))
    w3 = pl.BlockSpec((3, tc), lambda j: (0, j))
    w1 = pl.BlockSpec((1, tc), lambda j: (0, j))
    return pl.pallas_call(
        body, name=name, grid=(F // tc,),
        in_specs=[two, w3, w1, col], out_specs=[two, w3, w1],
        out_shape=[jax.ShapeDtypeStruct((2, T, F), BF16),
                   jax.ShapeDtypeStruct((3, F), F32), jax.ShapeDtypeStruct((1, F), F32)],
        compiler_params=_cparams(("parallel",)),
    )(up, cw, cb, dact)


def _rot_half(x):
    W = x.shape[-1]
    lane = lax.broadcasted_iota(jnp.int32, x.shape, x.ndim - 1)
    lo = (lane % 32) < 16
    return jnp.where(lo, -pltpu.roll(x, W - 16, x.ndim - 1), pltpu.roll(x, 16, x.ndim - 1))


def _swap_halves(x):
    return pltpu.roll(x, 64, x.ndim - 1)


def _rope_tables(dm):
    t = jnp.arange(dm.L)
    n_freq = HEAD_DIM // 4
    inv_freq = ROPE_BASE ** (-jnp.arange(n_freq, dtype=F32) / n_freq)
    ang_r = (t // GRID_W).astype(F32)[:, None] * inv_freq
    ang_c = (t % GRID_W).astype(F32)[:, None] * inv_freq
    ang = jnp.concatenate([ang_r, ang_r, ang_c, ang_c], axis=-1)
    cos = jnp.concatenate([jnp.ones((dm.C, HEAD_DIM), F32), jnp.cos(ang)], axis=0)
    sin = jnp.concatenate([jnp.zeros((dm.C, HEAD_DIM), F32), jnp.sin(ang)], axis=0)
    return jnp.tile(cos, (1, 2)), jnp.tile(sin, (1, 2))


def _attn_prep(z, cos, sin, dm, name):
    T = dm.T
    AW = dm.ATTN_W
    tm = _tile(T, 256, 16)
    rep = AW // LANES

    def body(z_ref, cos_ref, sin_ref, q_ref, k_ref, v_ref):
        cs, sn = cos_ref[...], sin_ref[...]
        q = z_ref[:, 0:AW]
        csq, snq = jnp.tile(cs, (1, rep)), jnp.tile(sn, (1, rep))
        q_ref[...] = (q * csq + _rot_half(q) * snq).astype(BF16)
        k = z_ref[:, dm.OFF_K:dm.OFF_V]
        k = k * cs + _rot_half(k) * sn
        v = z_ref[:, dm.OFF_V:dm.OFF_S]
        lo = lax.broadcasted_iota(jnp.int32, k.shape, 1) < HEAD_DIM
        ks, vs = _swap_halves(k), _swap_halves(v)
        k_ref[0] = jnp.where(lo, k, ks).astype(BF16)
        k_ref[1] = jnp.where(lo, ks, k).astype(BF16)
        v_ref[0] = jnp.where(lo, v, vs).astype(BF16)
        v_ref[1] = jnp.where(lo, vs, v).astype(BF16)

    tab = pl.BlockSpec((tm, LANES), lambda i: (i, 0))
    kv = pl.BlockSpec((2, tm, LANES), lambda i: (0, i, 0))
    return pl.pallas_call(
        body, name=name, grid=(T // tm,),
        in_specs=[pl.BlockSpec((tm, dm.OFF_S), lambda i: (i, 0)), tab, tab],
        out_specs=[pl.BlockSpec((tm, AW), lambda i: (i, 0)), kv, kv],
        out_shape=[jax.ShapeDtypeStruct((T, AW), BF16), jax.ShapeDtypeStruct((2, T, LANES), BF16),
                   jax.ShapeDtypeStruct((2, T, LANES), BF16)],
        compiler_params=_cparams(("parallel",)),
    )(z, cos, sin)


def _attn_prep_bwd(dq, dk2, dv2, cos, sin, dm, name):
    T = dm.T
    AW = dm.ATTN_W
    tm = _tile(T, 256, 16)
    rep = AW // LANES

    def body(dq_ref, dk_ref, dv_ref, cos_ref, sin_ref, o_ref):
        cs, sn = cos_ref[...], sin_ref[...]
        csq, snq = jnp.tile(cs, (1, rep)), jnp.tile(sn, (1, rep))
        dq = dq_ref[...]
        o_ref[:, 0:AW] = dq * csq - _rot_half(dq * snq)
        lo = lax.broadcasted_iota(jnp.int32, (tm, LANES), 1) < HEAD_DIM
        dk = jnp.where(lo, dk_ref[0], dk_ref[1])
        o_ref[:, dm.OFF_K:dm.OFF_V] = dk * cs - _rot_half(dk * sn)
        o_ref[:, dm.OFF_V:dm.OFF_S] = jnp.where(lo, dv_ref[0], dv_ref[1])

    tab = pl.BlockSpec((tm, LANES), lambda i: (i, 0))
    kv = pl.BlockSpec((2, tm, LANES), lambda i: (0, i, 0))
    return pl.pallas_call(
        body, name=name, grid=(T // tm,),
        in_specs=[pl.BlockSpec((tm, AW), lambda i: (i, 0)), kv, kv, tab, tab],
        out_specs=pl.BlockSpec((tm, dm.OFF_S), lambda i: (i, 0)),
        out_shape=jax.ShapeDtypeStruct((T, dm.OFF_S), F32),
        compiler_params=_cparams(("parallel",)),
    )(dq, dk2, dv2, cos, sin)


def _attn_specs(dm):
    B = ATTN_BLOCK
    nblk = dm.T // B
    q_spec = pl.BlockSpec((B, dm.ATTN_W), lambda i: (i, 0))
    prev = pl.BlockSpec((2, B, LANES), lambda i: (0, jnp.maximum(i - 1, 0), 0))
    own = pl.BlockSpec((2, B, LANES), lambda i: (0, i, 0))
    nxt = pl.BlockSpec((2, B, LANES), lambda i: (0, jnp.minimum(i + 1, nblk - 1), 0))
    ctx = pl.BlockSpec((2, dm.C, LANES), lambda i: (0, 0, 0))
    sink = pl.BlockSpec((2, GQA_RATIO * B, 1), lambda i: (0, 0, 0))
    return nblk, q_spec, prev, own, nxt, ctx, sink


def _attn_scores(i, q_ref, kp_ref, ko_ref, kn_ref, kc_ref, sink_ref, h, dm):
    B = ATTN_BLOCK
    nctx = dm.C // B
    nb = dm.L // B
    npair = GQA_RATIO // 2
    lo = lax.broadcasted_iota(jnp.int32, (B, LANES), 1) < HEAD_DIM
    zero = jnp.zeros((B, LANES), BF16)
    parts = []
    for p in range(npair):
        q2 = q_ref[:, (h * npair + p) * LANES:(h * npair + p + 1) * LANES]
        parts.append(jnp.where(lo, q2, zero))
        parts.append(jnp.where(lo, zero, q2))
    q8 = jnp.concatenate(parts, axis=0)
    kcat = jnp.concatenate([kp_ref[h], ko_ref[h], kn_ref[h], kc_ref[h]], axis=0)
    s = _dot(q8, kcat, _NT) * (HEAD_DIM ** -0.5)
    R, NK = s.shape
    iq = lax.broadcasted_iota(jnp.int32, (R, NK), 0) % B
    col = lax.broadcasted_iota(jnp.int32, (R, NK), 1)
    jk = col % B
    qb = i - nctx
    lat = qb >= 0
    m_prev = (col < B) & lat & (qb >= 1) & (jk >= iq)
    m_own = (col >= B) & (col < 2 * B) & lat
    m_next = (col >= 2 * B) & (col < 3 * B) & lat & (qb <= nb - 2) & (jk <= iq)
    mask = m_prev | m_own | m_next | (col >= 3 * B)
    s = jnp.where(mask, s, -jnp.inf)
    sk = sink_ref[h]
    m = jnp.maximum(jnp.max(s, axis=-1, keepdims=True), sk)
    e = jnp.exp(s - m)
    es = jnp.exp(sk - m)
    inv = 1.0 / (jnp.sum(e, axis=-1, keepdims=True) + es)
    return q8, kcat, e * inv, es * inv, lo


def _unstack_heads(x8, lo):
    B = ATTN_BLOCK
    out = []
    for p in range(GQA_RATIO // 2):
        a = x8[(2 * p) * B:(2 * p + 1) * B]
        b = x8[(2 * p + 1) * B:(2 * p + 2) * B]
        out.append(jnp.where(lo, a, b))
    return out


def _attention(q, k2, v2, sinkcol, dm, name):
    B = ATTN_BLOCK
    nblk, q_spec, prev, own, nxt, ctx, sink = _attn_specs(dm)
    npair = GQA_RATIO // 2

    def body(q_ref, kp_ref, ko_ref, kn_ref, kc_ref, vp_ref, vo_ref, vn_ref, vc_ref, sink_ref, o_ref):
        i = pl.program_id(0)
        for h in range(2):
            _, _, p, _, lo = _attn_scores(i, q_ref, kp_ref, ko_ref, kn_ref, kc_ref, sink_ref, h, dm)
            vcat = jnp.concatenate([vp_ref[h], vo_ref[h], vn_ref[h], vc_ref[h]], axis=0)
            o8 = _dot(p, vcat, _NN)
            for pi, o2 in enumerate(_unstack_heads(o8, lo)):
                c0 = (h * npair + pi) * LANES
                o_ref[:, c0:c0 + LANES] = o2.astype(BF16)

    return pl.pallas_call(
        body, name=name, grid=(nblk,),
        in_specs=[q_spec, prev, own, nxt, ctx, prev, own, nxt, ctx, sink],
        out_specs=q_spec, out_shape=jax.ShapeDtypeStruct((dm.T, dm.ATTN_W), BF16),
        compiler_params=_cparams(("parallel",)),
    )(q, k2, k2, k2, k2, v2, v2, v2, v2, sinkcol)


def _attention_bwd(q, k2, v2, sinkcol, do, dm, name):
    B = ATTN_BLOCK
    nblk, q_spec, prev, own, nxt, ctx, sink = _attn_specs(dm)
    npair = GQA_RATIO // 2

    def body(q_ref, kp_ref, ko_ref, kn_ref, kc_ref, vp_ref, vo_ref, vn_ref, vc_ref, sink_ref, do_ref,
             dq_ref, pk_ref, pv_ref, dkc_ref, dvc_ref, dsk_ref):
        i = pl.program_id(0)

        @pl.when(i == 0)
        def _():
            dkc_ref[...] = jnp.zeros_like(dkc_ref)
            dvc_ref[...] = jnp.zeros_like(dvc_ref)
            dsk_ref[...] = jnp.zeros_like(dsk_ref)

        for h in range(2):
            q8, kcat, p, ps, lo = _attn_scores(i, q_ref, kp_ref, ko_ref, kn_ref, kc_ref, sink_ref, h, dm)
            vcat = jnp.concatenate([vp_ref[h], vo_ref[h], vn_ref[h], vc_ref[h]], axis=0)
            zero = jnp.zeros((B, LANES), F32)
            parts = []
            for pi in range(npair):
                d2 = do_ref[:, (h * npair + pi) * LANES:(h * npair + pi + 1) * LANES]
                parts.append(jnp.where(lo, d2, zero))
                parts.append(jnp.where(lo, zero, d2))
            do8 = jnp.concatenate(parts, axis=0)
            dp = _dot(do8, vcat, _NT)
            delta = jnp.sum(p * dp, axis=-1, keepdims=True)
            ds = p * (dp - delta) * (HEAD_DIM ** -0.5)
            dsr = -ps * delta
            rows = [jnp.broadcast_to(jnp.sum(dsr[r * B:(r + 1) * B], axis=0, keepdims=True), (1, LANES))
                    for r in range(GQA_RATIO)]
            dsk_ref[h] += jnp.concatenate(rows, axis=0)
            dq8 = _dot(ds, kcat, _NN)
            for pi, d2 in enumerate(_unstack_heads(dq8, lo)):
                c0 = (h * npair + pi) * LANES
                dq_ref[:, c0:c0 + LANES] = d2
            dk = _dot(ds, q8, _TN)
            dv = _dot(p, do8, _TN)
            dk = dk + _swap_halves(dk)
            dv = dv + _swap_halves(dv)
            for w in range(3):
                pk_ref[0, h, w] = dk[w * B:(w + 1) * B]
                pv_ref[0, h, w] = dv[w * B:(w + 1) * B]
            dkc_ref[h] += dk[3 * B:]
            dvc_ref[h] += dv[3 * B:]

    part = pl.BlockSpec((1, 2, 3, B, LANES), lambda i: (i, 0, 0, 0, 0))
    acc_c = pl.BlockSpec((2, dm.C, LANES), lambda i: (0, 0, 0))
    acc_s = pl.BlockSpec((2, GQA_RATIO, LANES), lambda i: (0, 0, 0))
    return pl.pallas_call(
        body, name=name, grid=(nblk,),
        in_specs=[q_spec, prev, own, nxt, ctx, prev, own, nxt, ctx, sink, q_spec],
        out_specs=[q_spec, part, part, acc_c, acc_c, acc_s],
        out_shape=[jax.ShapeDtypeStruct((dm.T, dm.ATTN_W), F32),
                   jax.ShapeDtypeStruct((nblk, 2, 3, B, LANES), F32),
                   jax.ShapeDtypeStruct((nblk, 2, 3, B, LANES), F32),
                   jax.ShapeDtypeStruct((2, dm.C, LANES), F32), jax.ShapeDtypeStruct((2, dm.C, LANES), F32),
                   jax.ShapeDtypeStruct((2, GQA_RATIO, LANES), F32)],
        compiler_params=_cparams(("arbitrary",)),
    )(q, k2, k2, k2, k2, v2, v2, v2, v2, sinkcol, do)


def _kv_reduce(part, ctxacc, dm, name):
    B = ATTN_BLOCK
    nblk = dm.T // B
    nctx = dm.C // B

    def body(own_ref, nxt_ref, prv_ref, c_ref, o_ref):
        j = pl.program_id(0)
        acc = own_ref[0, :, 0]
        acc = acc + jnp.where(j < nblk - 1, nxt_ref[0, :, 0], 0.0)
        acc = acc + jnp.where(j > 0, prv_ref[0, :, 0], 0.0)
        acc = acc + jnp.where(j < nctx, c_ref[...], 0.0)
        o_ref[...] = acc

    def spec(w, f):
        return pl.BlockSpec((1, 2, 1, B, LANES), lambda j: (f(j), 0, w, 0, 0))

    return pl.pallas_call(
        body, name=name, grid=(nblk,),
        in_specs=[spec(1, lambda j: j), spec(0, lambda j: jnp.minimum(j + 1, nblk - 1)),
                  spec(2, lambda j: jnp.maximum(j - 1, 0)),
                  pl.BlockSpec((2, B, LANES), lambda j: (0, jnp.minimum(j, nctx - 1), 0))],
        out_specs=pl.BlockSpec((2, B, LANES), lambda j: (0, j, 0)),
        out_shape=jax.ShapeDtypeStruct((2, dm.T, LANES), F32),
        compiler_params=_cparams(("parallel",)),
    )(part, part, part, ctxacc)


def _gmlp_core(z_ref, lg_ref, lb_ref, ws_ref, bs_ref, dm):
    GW = dm.GMLP_W
    gu = z_ref[:, dm.OFF_GU:dm.OFF_GV]
    gv = z_ref[:, dm.OFF_GV:dm.N_IN]
    u = _gelu(gu)
    ge = _gelu(gv)
    mu = jnp.mean(ge, axis=-1, keepdims=True)
    xc = ge - mu
    r = lax.rsqrt(jnp.mean(xc * xc, axis=-1, keepdims=True) + NORM_EPS)
    xh = xc * r
    v = xh * lg_ref[...] + lb_ref[...]
    mixed = []
    for g in range(dm.GG):
        vg = v[:, g * GMLP_GROUP_W:(g + 1) * GMLP_GROUP_W]
        mixed.append(_dot(ws_ref[g], vg, _NN) + bs_ref[g])
    return gu, gv, u, xh, r, v, mixed


def _gmlp(z, ln_g, ln_b, w_s, b_s, dm, name):
    T, GW, CH = dm.T, dm.GMLP_W, GMLP_CHUNK

    def body(z_ref, lg_ref, lb_ref, ws_ref, bs_ref, o_ref):
        _, _, u, _, _, _, mixed = _gmlp_core(z_ref, lg_ref, lb_ref, ws_ref, bs_ref, dm)
        for g in range(dm.GG):
            sl = slice(g * GMLP_GROUP_W, (g + 1) * GMLP_GROUP_W)
            o_ref[:, sl] = (u[:, sl] * mixed[g]).astype(BF16)

    vec = pl.BlockSpec((1, GW), lambda i: (0, 0))
    return pl.pallas_call(
        body, name=name, grid=(T // CH,),
        in_specs=[pl.BlockSpec((CH, dm.N_IN), lambda i: (i, 0)), vec, vec,
                  pl.BlockSpec((dm.GG, CH, CH), lambda i: (0, 0, 0)),
                  pl.BlockSpec((dm.GG, CH, 1), lambda i: (0, 0, 0))],
        out_specs=pl.BlockSpec((CH, GW), lambda i: (i, 0)),
        out_shape=jax.ShapeDtypeStruct((T, GW), BF16),
        compiler_params=_cparams(("parallel",)),
    )(z, ln_g, ln_b, w_s, b_s)


def _gmlp_bwd(z, ln_g, ln_b, w_s, b_s, do, dm, name):
    T, GW, CH = dm.T, dm.GMLP_W, GMLP_CHUNK

    def body(z_ref, lg_ref, lb_ref, ws_ref, bs_ref, do_ref, dz_ref, dlg_ref, dlb_ref, dws_ref, dbs_ref):
        i = pl.program_id(0)

        @pl.when(i == 0)
        def _():
            dlg_ref[...] = jnp.zeros_like(dlg_ref)
            dlb_ref[...] = jnp.zeros_like(dlb_ref)
            dws_ref[...] = jnp.zeros_like(dws_ref)
            dbs_ref[...] = jnp.zeros_like(dbs_ref)

        gu, gv, u, xh, r, v, mixed = _gmlp_core(z_ref, lg_ref, lb_ref, ws_ref, bs_ref, dm)
        do = do_ref[...]
        dv_parts = []
        for g in range(dm.GG):
            sl = slice(g * GMLP_GROUP_W, (g + 1) * GMLP_GROUP_W)
            dog = do[:, sl]
            dz_ref[:, sl] = dog * mixed[g] * _gelu_grad(gu[:, sl])
            dmx = dog * u[:, sl]
            dbs_ref[g] += jnp.sum(dmx, axis=-1, keepdims=True)
            dws_ref[g] += _dot(dmx, v[:, sl], _NT)
            dv_parts.append(_dot(ws_ref[g], dmx, _TN))
        dv = jnp.concatenate(dv_parts, axis=-1) if dm.GG > 1 else dv_parts[0]
        dlg_ref[...] += jnp.sum(dv * xh, axis=0, keepdims=True)
        dlb_ref[...] += jnp.sum(dv, axis=0, keepdims=True)
        dxh = dv * lg_ref[...]
        dge = r * (dxh - jnp.mean(dxh, axis=-1, keepdims=True) - xh * jnp.mean(dxh * xh, axis=-1, keepdims=True))
        dz_ref[:, GW:2 * GW] = dge * _gelu_grad(gv)

    vec = pl.BlockSpec((1, GW), lambda i: (0, 0))
    wsp = pl.BlockSpec((dm.GG, CH, CH), lambda i: (0, 0, 0))
    bsp = pl.BlockSpec((dm.GG, CH, 1), lambda i: (0, 0, 0))
    return pl.pallas_call(
        body, name=name, grid=(T // CH,),
        in_specs=[pl.BlockSpec((CH, dm.N_IN), lambda i: (i, 0)), vec, vec, wsp, bsp,
                  pl.BlockSpec((CH, GW), lambda i: (i, (dm.ATTN_W + dm.SSM_W) // GW))],
        out_specs=[pl.BlockSpec((CH, 2 * GW), lambda i: (i, 0)), vec, vec, wsp, bsp],
        out_shape=[jax.ShapeDtypeStruct((T, 2 * GW), F32), jax.ShapeDtypeStruct((1, GW), F32),
                   jax.ShapeDtypeStruct((1, GW), F32), jax.ShapeDtypeStruct((dm.GG, CH, CH), F32),
                   jax.ShapeDtypeStruct((dm.GG, CH, 1), F32)],
        compiler_params=_cparams(("arbitrary",)),
    )(z, ln_g, ln_b, w_s, b_s, do)


def _disc_fn(lam_re, lam_im, ldt, b_re, b_im):
    dt = jnp.exp(ldt)
    mag = jnp.exp(lam_re * dt)
    a_re = mag * jnp.cos(lam_im * dt)
    a_im = mag * jnp.sin(lam_im * dt)
    den = lam_re * lam_re + lam_im * lam_im
    n_re = a_re - 1.0
    f_re = (n_re * lam_re + a_im * lam_im) / den
    f_im = (a_im * lam_re - n_re * lam_im) / den
    return a_re, a_im, f_re * b_re - f_im * b_im, f_re * b_im + f_im * b_re


def _s5_disc(lam_re, lam_im, ldt, bT_re, bT_im, name):
    nd, H, GP = bT_re.shape

    def body(lr_ref, li_ref, dt_ref, br_ref, bi_ref, ar_ref, ai_ref, bbr_ref, bbi_ref):
        for d in range(nd):
            a_re, a_im, bb_re, bb_im = _disc_fn(lr_ref[d], li_ref[d], dt_ref[d], br_ref[d], bi_ref[d])
            ar_ref[d] = a_re
            ai_ref[d] = a_im
            bbr_ref[d] = bb_re
            bbi_ref[d] = bb_im

    vs = jax.ShapeDtypeStruct((nd, 1, GP), F32)
    ms = jax.ShapeDtypeStruct((nd, H, GP), F32)
    return pl.pallas_call(body, name=name, out_shape=[vs, vs, ms, ms], compiler_params=_cparams())(
        lam_re, lam_im, ldt, bT_re, bT_im)


def _s5_disc_bwd(lam_re, lam_im, ldt, bT_re, bT_im, da_re, da_im, dbb_re, dbb_im, name):
    nd, H, GP = bT_re.shape

    def body(lr_ref, li_ref, dt_ref, br_ref, bi_ref, dar_ref, dai_ref, dbr_ref, dbi_ref,
             olr_ref, oli_ref, odt_ref, obr_ref, obi_ref):
        for d in range(nd):
            _, vjp = jax.vjp(_disc_fn, lr_ref[d], li_ref[d], dt_ref[d], br_ref[d], bi_ref[d])
            g = vjp((dar_ref[d], dai_ref[d], dbr_ref[d], dbi_ref[d]))
            olr_ref[d], oli_ref[d], odt_ref[d], obr_ref[d], obi_ref[d] = g

    vs = jax.ShapeDtypeStruct((nd, 1, GP), F32)
    ms = jax.ShapeDtypeStruct((nd, H, GP), F32)
    return pl.pallas_call(body, name=name, out_shape=[vs, vs, vs, ms, ms], compiler_params=_cparams())(
        lam_re, lam_im, ldt, bT_re, bT_im, da_re, da_im, dbb_re, dbb_im)


def _scan_pass(re_ref, im_ref, nsteps, ar, ai, ir, ii, reverse, store):
    def step(n, carry):
        sr, si = carry
        t = (nsteps - 1 - n) if reverse else n
        off = pl.multiple_of(t * 8, 8)
        nr = ar * sr - ai * si + re_ref[pl.ds(off, 8), :]
        ni = ar * si + ai * sr + im_ref[pl.ds(off, 8), :]
        if store:
            re_ref[pl.ds(off, 8), :] = nr
            im_ref[pl.ds(off, 8), :] = ni
        return nr, ni

    return lax.fori_loop(0, nsteps, step, (ir, ii), unroll=4)


def _cpow(ar, ai, n):
    rr = ri = None
    br, bi = ar, ai
    while n:
        if n & 1:
            rr, ri = (br, bi) if rr is None else (rr * br - ri * bi, rr * bi + ri * br)
        n >>= 1
        if n:
            br, bi = br * br - bi * bi, 2.0 * br * bi
    return rr, ri


def _row_of(x, k):
    rows = lax.broadcasted_iota(jnp.int32, x.shape, 0)
    return jnp.sum(jnp.where(rows == k, x, 0.0), axis=0, keepdims=True)


def _full_scan(re_ref, im_ref, nsteps, ar1, ai1, h0r, h0i, reverse):
    P = ar1.shape[1]
    ar, ai = jnp.broadcast_to(ar1, (8, P)), jnp.broadcast_to(ai1, (8, P))
    z = jnp.zeros((8, P), F32)
    er, ei = _scan_pass(re_ref, im_ref, nsteps, ar, ai, z, z, reverse, False)
    pr, pi = _cpow(ar1, ai1, nsteps)
    rows = lax.broadcasted_iota(jnp.int32, (8, P), 0)
    initr, initi = z, z
    cr, ci = h0r, h0i
    for k in (range(SCAN_SEGMENTS - 1, -1, -1) if reverse else range(SCAN_SEGMENTS)):
        initr = jnp.where(rows == k, cr, initr)
        initi = jnp.where(rows == k, ci, initi)
        ekr, eki = _row_of(er, k), _row_of(ei, k)
        cr, ci = ekr + pr * cr - pi * ci, eki + pr * ci + pi * cr
    _scan_pass(re_ref, im_ref, nsteps, ar, ai, initr, initi, reverse, True)
    return initr, initi, cr, ci


def _rows_loop(n, chunk, fn):
    def step(c, carry):
        fn(pl.multiple_of(c * chunk, chunk))
        return carry
    lax.fori_loop(0, n // chunk, step, 0)


def _s5_specs(dm, PB):
    nsb = dm.GP // PB
    per = (LANES * SSM_STATE // SSM_GROUP) // PB
    ul = pl.BlockSpec((dm.L, LANES), lambda j: (0, j // per))
    uc = pl.BlockSpec((dm.C, LANES), lambda j: (0, j // per))
    av = pl.BlockSpec((2, 1, PB), lambda j: (0, 0, j))
    bd = pl.BlockSpec((2, LANES, PB), lambda j: (0, j // per, j))
    cd = pl.BlockSpec((2, PB, LANES), lambda j: (0, j, j // per))
    dv = pl.BlockSpec((1, LANES), lambda j: (0, j // per))
    return PB, nsb, per, ul, uc, av, bd, cd, dv


def _s5_scan(ul, uc, a_re, a_im, bd_re, bd_im, cd_re, cd_im, dskip, dm, name):
    PB, nsb, per, ul_s, uc_s, av, bd, cd, dv = _s5_specs(dm, STATE_BLOCK_FWD)
    L, C = dm.L, dm.C
    RC = _tile(L, 512, 8)
    RCC = _tile(C, 512, 8)

    def body(ul_ref, uc_ref, ar_ref, ai_ref, br_ref, bi_ref, cr_ref, ci_ref, d_ref, yl_ref, yc_ref,
             lre, lim, cre, cim):
        j = pl.program_id(0)

        @pl.when(j % per == 0)
        def _():
            yl_ref[...] = ul_ref[...] * d_ref[...]
            yc_ref[...] = uc_ref[...] * d_ref[...]

        for d in range(2):
            rev = d == 1
            ar1, ai1 = ar_ref[d], ai_ref[d]

            def proj(u_ref, re, im, chunk, n):
                def f(r0):
                    u = u_ref[pl.ds(r0, chunk), :]
                    re[pl.ds(r0, chunk), :] = _dot(u, br_ref[d], _NN)
                    im[pl.ds(r0, chunk), :] = _dot(u, bi_ref[d], _NN)
                _rows_loop(n, chunk, f)

            def readout(y_ref, re, im, chunk, n):
                def f(r0):
                    y_ref[pl.ds(r0, chunk), :] += (_dot(re[pl.ds(r0, chunk), :], cr_ref[d], _NN)
                                                   - _dot(im[pl.ds(r0, chunk), :], ci_ref[d], _NN))
                _rows_loop(n, chunk, f)

            z1 = jnp.zeros((1, PB), F32)
            proj(uc_ref, cre, cim, RCC, C)
            _, _, fr, fi = _full_scan(cre, cim, C // 8, ar1, ai1, z1, z1, rev)
            readout(yc_ref, cre, cim, RCC, C)
            proj(ul_ref, lre, lim, RC, L)
            _full_scan(lre, lim, L // 8, ar1, ai1, fr, fi, rev)
            readout(yl_ref, lre, lim, RC, L)

    return pl.pallas_call(
        body, name=name, grid=(nsb,),
        in_specs=[ul_s, uc_s, av, av, bd, bd, cd, cd, dv],
        out_specs=[ul_s, uc_s],
        out_shape=[jax.ShapeDtypeStruct((L, dm.SSM_W), F32), jax.ShapeDtypeStruct((C, dm.SSM_W), F32)],
        scratch_shapes=[pltpu.VMEM((L, PB), F32), pltpu.VMEM((L, PB), F32),
                        pltpu.VMEM((C, PB), F32), pltpu.VMEM((C, PB), F32)],
        compiler_params=_cparams(("arbitrary",)),
    )(ul, uc, a_re, a_im, bd_re, bd_im, cd_re, cd_im, dskip)


def _s5_scan_bwd(ul, uc, dyl, dyc, a_re, a_im, bd_re, bd_im, cd_re, cd_im, bdT_re, bdT_im, cdT_re, cdT_im,
                 dskip, dm, name):
    PB, nsb, per, ul_s, uc_s, av, bd, cd, dv = _s5_specs(dm, STATE_BLOCK)
    L, C = dm.L, dm.C
    RC = _tile(L, 512, 8)
    RCC = _tile(C, 512, 8)
    bdT = pl.BlockSpec((2, PB, LANES), lambda j: (0, j, j // per))
    cdT = pl.BlockSpec((2, LANES, PB), lambda j: (0, j // per, j))
    dbd = pl.BlockSpec((2, 1, LANES, PB), lambda j: (0, j, 0, 0))
    dcd = pl.BlockSpec((2, 1, PB, LANES), lambda j: (0, j, 0, 0))

    def body(ul_ref, uc_ref, dyl_ref, dyc_ref, ar_ref, ai_ref, br_ref, bi_ref, cr_ref, ci_ref,
             brT_ref, biT_ref, crT_ref, ciT_ref, d_ref,
             dul_ref, duc_ref, dar_ref, dai_ref, dbr_ref, dbi_ref, dcr_ref, dci_ref, dd_ref,
             lre, lim, cre, cim, gre, gim, hre, him):
        j = pl.program_id(0)

        @pl.when(j % per == 0)
        def _():
            dul_ref[...] = dyl_ref[...] * d_ref[...]
            duc_ref[...] = dyc_ref[...] * d_ref[...]
            dd_ref[...] = (jnp.sum(dyl_ref[...] * ul_ref[...], axis=0, keepdims=True)
                           + jnp.sum(dyc_ref[...] * uc_ref[...], axis=0, keepdims=True))

        for d in range(2):
            rev = d == 1
            ar1, ai1 = ar_ref[d], ai_ref[d]
            z1 = jnp.zeros((1, PB), F32)

            def proj(u_ref, w_re, w_im, sign, re, im, chunk, n):
                def f(r0):
                    u = u_ref[pl.ds(r0, chunk), :]
                    re[pl.ds(r0, chunk), :] = _dot(u, w_re, _NN)
                    im[pl.ds(r0, chunk), :] = sign * _dot(u, w_im, _NN)
                _rows_loop(n, chunk, f)

            proj(uc_ref, br_ref[d], bi_ref[d], 1.0, cre, cim, RCC, C)
            icr, ici, fr, fi = _full_scan(cre, cim, C // 8, ar1, ai1, z1, z1, rev)
            proj(ul_ref, br_ref[d], bi_ref[d], 1.0, lre, lim, RC, L)
            ilr, ili, _, _ = _full_scan(lre, lim, L // 8, ar1, ai1, fr, fi, rev)
            proj(dyl_ref, crT_ref[d], ciT_ref[d], -1.0, gre, gim, RC, L)
            _, _, gfr, gfi = _full_scan(gre, gim, L // 8, ar1, -ai1, z1, z1, not rev)
            proj(dyc_ref, crT_ref[d], ciT_ref[d], -1.0, hre, him, RCC, C)
            _full_scan(hre, him, C // 8, ar1, -ai1, gfr, gfi, not rev)

            dar = jnp.zeros((8, PB), F32)
            dai = jnp.zeros((8, PB), F32)
            dbr = jnp.zeros((LANES, PB), F32)
            dbi = jnp.zeros((LANES, PB), F32)
            dcr = jnp.zeros((PB, LANES), F32)
            dci = jnp.zeros((PB, LANES), F32)
            for (u_ref, dy_ref, du_ref, sre, sim, are, aim, inr, ini, n, chunk) in (
                    (ul_ref, dyl_ref, dul_ref, lre, lim, gre, gim, ilr, ili, L, RC),
                    (uc_ref, dyc_ref, duc_ref, cre, cim, hre, him, icr, ici, C, RCC)):
                nst = n // 8

                def da_step(t, carry, sre=sre, sim=sim, are=are, aim=aim, nst=nst):
                    xr, xi = carry
                    tp = (t + 1) if rev else (t - 1)
                    o = pl.multiple_of(t * 8, 8)
                    op = pl.multiple_of(tp * 8, 8)
                    g_r, g_i = are[pl.ds(o, 8), :], aim[pl.ds(o, 8), :]
                    p_r, p_i = sre[pl.ds(op, 8), :], sim[pl.ds(op, 8), :]
                    return xr + g_r * p_r + g_i * p_i, xi + g_i * p_r - g_r * p_i

                lo_t, hi_t = (0, nst - 1) if rev else (1, nst)
                dar, dai = lax.fori_loop(lo_t, hi_t, da_step, (dar, dai))
                e0 = (nst - 1) * 8 if rev else 0
                g_r, g_i = are[pl.ds(e0, 8), :], aim[pl.ds(e0, 8), :]
                dar = dar + g_r * inr + g_i * ini
                dai = dai + g_i * inr - g_r * ini

                def mats(c, carry, u_ref=u_ref, dy_ref=dy_ref, du_ref=du_ref, sre=sre, sim=sim, are=are, aim=aim,
                         chunk=chunk):
                    xbr, xbi, xcr, xci = carry
                    r0 = pl.multiple_of(c * chunk, chunk)
                    u = u_ref[pl.ds(r0, chunk), :]
                    dy = dy_ref[pl.ds(r0, chunk), :]
                    g_r, g_i = are[pl.ds(r0, chunk), :], aim[pl.ds(r0, chunk), :]
                    du_ref[pl.ds(r0, chunk), :] += _dot(g_r, brT_ref[d], _NN) + _dot(g_i, biT_ref[d], _NN)
                    xbr = xbr + _dot(u, g_r, _TN)
                    xbi = xbi + _dot(u, g_i, _TN)
                    xcr = xcr + _dot(sre[pl.ds(r0, chunk), :], dy, _TN)
                    xci = xci - _dot(sim[pl.ds(r0, chunk), :], dy, _TN)
                    return xbr, xbi, xcr, xci

                dbr, dbi, dcr, dci = lax.fori_loop(0, n // chunk, mats, (dbr, dbi, dcr, dci))
            dar_ref[d] = jnp.sum(dar, axis=0, keepdims=True)
            dai_ref[d] = jnp.sum(dai, axis=0, keepdims=True)
            dbr_ref[d, 0] = dbr
            dbi_ref[d, 0] = dbi
            dcr_ref[d, 0] = dcr
            dci_ref[d, 0] = dci

    W, GP = dm.SSM_W, dm.GP
    return pl.pallas_call(
        body, name=name, grid=(nsb,),
        in_specs=[ul_s, uc_s, ul_s, uc_s, av, av, bd, bd, cd, cd, bdT, bdT, cdT, cdT, dv],
        out_specs=[ul_s, uc_s, av, av, dbd, dbd, dcd, dcd, dv],
        out_shape=[jax.ShapeDtypeStruct((L, W), F32), jax.ShapeDtypeStruct((C, W), F32),
                   jax.ShapeDtypeStruct((2, 1, GP), F32), jax.ShapeDtypeStruct((2, 1, GP), F32),
                   jax.ShapeDtypeStruct((2, nsb, LANES, PB), F32), jax.ShapeDtypeStruct((2, nsb, LANES, PB), F32),
                   jax.ShapeDtypeStruct((2, nsb, PB, LANES), F32), jax.ShapeDtypeStruct((2, nsb, PB, LANES), F32),
                   jax.ShapeDtypeStruct((1, W), F32)],
        scratch_shapes=[pltpu.VMEM((L, PB), F32), pltpu.VMEM((L, PB), F32),
                        pltpu.VMEM((C, PB), F32), pltpu.VMEM((C, PB), F32),
                        pltpu.VMEM((L, PB), F32), pltpu.VMEM((L, PB), F32),
                        pltpu.VMEM((C, PB), F32), pltpu.VMEM((C, PB), F32)],
        compiler_params=_cparams(("arbitrary",)),
    )(ul, uc, dyl, dyc, a_re, a_im, bd_re, bd_im, cd_re, cd_im, bdT_re, bdT_im, cdT_re, cdT_im, dskip)


def _glu(y, w, b, name):
    T, W = y.shape
    tm = _tile(T, 544, 16)

    def body(y_ref, w_ref, b_ref, o_ref):
        g = _gelu(y_ref[...])
        o_ref[...] = (g * _sigmoid(_dot(g, w_ref[...], _NN) + b_ref[...])).astype(BF16)

    return pl.pallas_call(
        body, name=name, grid=(T // tm,),
        in_specs=[pl.BlockSpec((tm, W), lambda i: (i, 0)), pl.BlockSpec((W, W), lambda i: (0, 0)),
                  pl.BlockSpec((1, W), lambda i: (0, 0))],
        out_specs=pl.BlockSpec((tm, W), lambda i: (i, 0)),
        out_shape=jax.ShapeDtypeStruct((T, W), BF16),
        compiler_params=_cparams(("parallel",)),
    )(y, w, b)


def _glu_bwd(y, w, b, do, do_col, name):
    T, W = y.shape
    tm = _tile(T, 544, 16)

    def body(y_ref, w_ref, b_ref, do_ref, dy_ref, dw_ref, db_ref):
        i = pl.program_id(0)

        @pl.when(i == 0)
        def _():
            dw_ref[...] = jnp.zeros_like(dw_ref)
            db_ref[...] = jnp.zeros_like(db_ref)

        y = y_ref[...]
        do = do_ref[...]
        g = _gelu(y)
        s = _sigmoid(_dot(g, w_ref[...], _NN) + b_ref[...])
        dpre = do * g * s * (1.0 - s)
        db_ref[...] += jnp.sum(dpre, axis=0, keepdims=True)
        dw_ref[...] += _dot(g, dpre, _TN)
        dg = do * s + _dot(dpre, w_ref[...], _NT)
        dy_ref[...] = dg * _gelu_grad(y)

    row = pl.BlockSpec((tm, W), lambda i: (i, 0))
    wsp = pl.BlockSpec((W, W), lambda i: (0, 0))
    vec = pl.BlockSpec((1, W), lambda i: (0, 0))
    return pl.pallas_call(
        body, name=name, grid=(T // tm,),
        in_specs=[row, wsp, vec, pl.BlockSpec((tm, W), lambda i: (i, do_col))], out_specs=[row, wsp, vec],
        out_shape=[jax.ShapeDtypeStruct((T, W), F32), jax.ShapeDtypeStruct((W, W), F32),
                   jax.ShapeDtypeStruct((1, W), F32)],
        compiler_params=_cparams(("arbitrary",)),
    )(y, w, b, do)


def _loss_head(X, g, target, C, name):
    T, D = X.shape
    tm = _tile(math.gcd(C, T - C), 512, 16)
    nc = C // tm

    def body(x_ref, g_ref, t_ref, loss_ref, dx_ref, dg_ref):
        i = pl.program_id(0)

        @pl.when(i == 0)
        def _():
            loss_ref[...] = jnp.zeros_like(loss_ref)
            dg_ref[...] = jnp.zeros_like(dg_ref)

        @pl.when(i < nc)
        def _():
            dx_ref[...] = jnp.zeros_like(dx_ref)

        @pl.when(i >= nc)
        def _():
            x = x_ref[...]
            gv = g_ref[...]
            r = lax.rsqrt(jnp.mean(x * x, axis=-1, keepdims=True) + NORM_EPS)
            xh = x * r
            err = xh * gv - t_ref[...]
            loss_ref[...] += 0.5 * jnp.sum(jnp.mean(err * err, axis=-1, keepdims=True), axis=0, keepdims=True)
            dy = err * (1.0 / D)
            dg_ref[...] += jnp.sum(dy * xh, axis=0, keepdims=True)
            dxh = dy * gv
            dx_ref[...] = r * (dxh - xh * jnp.mean(dxh * xh, axis=-1, keepdims=True))

    row = pl.BlockSpec((tm, D), lambda i: (i, 0))
    return pl.pallas_call(
        body, name=name, grid=(T // tm,),
        in_specs=[row, pl.BlockSpec((1, D), lambda i: (0, 0)),
                  pl.BlockSpec((tm, D), lambda i: (jnp.maximum(i - nc, 0), 0))],
        out_specs=[pl.BlockSpec((1, 1), lambda i: (0, 0)), row, pl.BlockSpec((1, D), lambda i: (0, 0))],
        out_shape=[jax.ShapeDtypeStruct((1, 1), F32), jax.ShapeDtypeStruct((T, D), F32),
                   jax.ShapeDtypeStruct((1, D), F32)],
        compiler_params=_cparams(("arbitrary",)),
    )(X, g, target)


def _ada_fwd(cond, w, b, name):
    nl, D, Ns = w.shape
    tn = _tile(Ns, 1024, LANES)

    def body(c_ref, w_ref, b_ref, o_ref):
        c = c_ref[...]
        o_ref[0] = _dot(c * _sigmoid(c), w_ref[0], _NN) + b_ref[0]

    return pl.pallas_call(
        body, name=name, grid=(nl, Ns // tn),
        in_specs=[pl.BlockSpec((16, D), lambda l, j: (0, 0)), pl.BlockSpec((1, D, tn), lambda l, j: (l, 0, j)),
                  pl.BlockSpec((1, 1, tn), lambda l, j: (l, 0, j))],
        out_specs=pl.BlockSpec((1, 16, tn), lambda l, j: (l, 0, j)),
        out_shape=jax.ShapeDtypeStruct((nl, 16, Ns), F32),
        compiler_params=_cparams(("parallel", "parallel")),
    )(cond, w, b)


def _ada_bwd(cond, w, dmod, name):
    nl, D, Ns = w.shape
    tn = _tile(Ns, 1024, LANES)

    def body(c_ref, w_ref, dm_ref, dw_ref, da_ref):
        j = pl.program_id(1)

        @pl.when(j == 0)
        def _():
            da_ref[...] = jnp.zeros_like(da_ref)

        c = c_ref[...]
        dw_ref[0] = _dot(c * _sigmoid(c), dm_ref[0], _TN)
        da_ref[0] += _dot(dm_ref[0], w_ref[0], _NT)

    return pl.pallas_call(
        body, name=name, grid=(nl, Ns // tn),
        in_specs=[pl.BlockSpec((16, D), lambda l, j: (0, 0)), pl.BlockSpec((1, D, tn), lambda l, j: (l, 0, j)),
                  pl.BlockSpec((1, 16, tn), lambda l, j: (l, 0, j))],
        out_specs=[pl.BlockSpec((1, D, tn), lambda l, j: (l, 0, j)), pl.BlockSpec((1, 16, D), lambda l, j: (l, 0, 0))],
        out_shape=[jax.ShapeDtypeStruct((nl, D, Ns), F32), jax.ShapeDtypeStruct((nl, 16, D), F32)],
        compiler_params=_cparams(("parallel", "arbitrary")),
    )(cond, w, dmod)


def _cctx_grad(c_ctx, dact4, name):
    nch, nl, _, D = dact4.shape

    def body(c_ref, da_ref, o_ref):
        acc = jnp.zeros((1, D), F32)
        for ch in range(nch):
            for l in range(nl):
                acc = acc + da_ref[ch, l, 8:9, :]
        c = c_ref[...]
        s = _sigmoid(c)
        o_ref[...] = acc * (s + c * s * (1.0 - s))

    return pl.pallas_call(body, name=name, out_shape=jax.ShapeDtypeStruct((1, D), F32),
                          compiler_params=_cparams())(c_ctx, dact4)


def _adamw(w, g, m, v, name):
    R, W = w.shape
    tr = _tile(R, max(16, (1 << 19) // W // 16 * 16), 16 if g.dtype == BF16 else 8)
    c1 = 1.0 - ADAM_B1 ** ADAM_STEP
    c2 = 1.0 - ADAM_B2 ** ADAM_STEP

    def body(w_ref, g_ref, m_ref, v_ref, d_ref, nm_ref, nv_ref, g32_ref):
        g = g_ref[...].astype(F32)
        g32_ref[...] = g
        m = ADAM_B1 * m_ref[...] + (1.0 - ADAM_B1) * g
        v = ADAM_B2 * v_ref[...] + (1.0 - ADAM_B2) * (g * g)
        nm_ref[...] = m
        nv_ref[...] = v
        d_ref[...] = -ADAM_LR * ((m / c1) / (jnp.sqrt(v / c2) + ADAM_EPS) + ADAM_WD * w_ref[...])

    blk = pl.BlockSpec((tr, W), lambda i: (i, 0))
    s = jax.ShapeDtypeStruct((R, W), F32)
    return pl.pallas_call(
        body, name=name, grid=(R // tr,), in_specs=[blk] * 4, out_specs=[blk] * 4, out_shape=[s, s, s, s],
        compiler_params=_cparams(("parallel",)),
    )(w, g, m, v)


def _adamw_nd(w, g, m, v, name):
    shp = w.shape
    two = (math.prod(shp[:-1]), shp[-1])
    outs = _adamw(w.reshape(two), g.reshape(two), m.reshape(two), v.reshape(two), name)
    return [o.reshape(shp) for o in outs]


def _interleave(a, n):
    return a.reshape(SCAN_SEGMENTS, n // SCAN_SEGMENTS, -1).transpose(1, 0, 2).reshape(n, -1)


def _deinterleave(a, n):
    return a.reshape(n // SCAN_SEGMENTS, SCAN_SEGMENTS, -1).transpose(1, 0, 2).reshape(n, -1)


def _blockdiag_in(bbT, dm):
    eye = jnp.eye(dm.SG, dtype=F32)
    b4 = bbT.reshape(2, SSM_GROUP, dm.SG, SSM_STATE)
    return jnp.einsum("dhgp,gk->dghkp", b4, eye).reshape(2, dm.SSM_W, dm.GP)


def _blockdiag_in_T(dbd, dm):
    gpb = STATE_BLOCK // SSM_STATE
    per = (LANES // SSM_GROUP) // gpb
    d8 = dbd.reshape(2, dm.SSM_W // LANES, per, per, gpb, SSM_GROUP, gpb, SSM_STATE)
    x = jnp.einsum("duaabhbp->duabhp", d8).reshape(2, dm.SG, SSM_GROUP, SSM_STATE)
    return x.transpose(0, 2, 1, 3).reshape(2, SSM_GROUP, dm.GP)


def _blockdiag_out(c, dm):
    eye = jnp.eye(dm.SG, dtype=F32)
    return jnp.einsum("dghp,gk->dgpkh", c, eye).reshape(2, dm.GP, dm.SSM_W)


def _blockdiag_out_T(dcd, dm):
    gpb = STATE_BLOCK // SSM_STATE
    per = (LANES // SSM_GROUP) // gpb
    d8 = dcd.reshape(2, dm.SSM_W // LANES, per, gpb, SSM_STATE, per, gpb, SSM_GROUP)
    return jnp.einsum("duabpabh->duabhp", d8).reshape(2, dm.SG, SSM_GROUP, SSM_STATE)


_BIG = ("w_in", "w_out", "ssm_w_glu", "ffn_w_up", "ffn_w_down")
_SHARD_AXIS = {"w_in": 2, "w_out": 1, "ssm_w_glu": 1, "ffn_w_up": 2, "ffn_w_down": 1}
_SMALL = ("g_mix", "g_ffn", "attn_sink", "ssm_lambda_re", "ssm_lambda_im", "ssm_log_dt", "ssm_b_re", "ssm_b_im",
          "ssm_c_re", "ssm_c_im", "ssm_d", "ssm_b_glu", "gmlp_ln_g", "gmlp_ln_b", "gmlp_w_s", "gmlp_b_s",
          "ffn_conv_b", "g_final", "ffn_conv_w")
_WEIGHTS = ("c_ctx", "w_ada", "b_ada", "g_mix", "g_ffn", "w_in", "w_out", "attn_sink", "ssm_lambda_re",
            "ssm_lambda_im", "ssm_log_dt", "ssm_b_re", "ssm_b_im", "ssm_c_re", "ssm_c_im", "ssm_d", "ssm_w_glu",
            "ssm_b_glu", "gmlp_ln_g", "gmlp_ln_b", "gmlp_w_s", "gmlp_b_s", "ffn_w_up", "ffn_conv_w", "ffn_conv_b",
            "ffn_w_down", "g_final")


def _step(P, M, V, x, c, ctx, loss_target):
    dm = _Dims()
    D, T, C, L = dm.D, dm.T, dm.C, dm.L
    mx, my, mc = lax.axis_index("x"), lax.axis_index("y"), lax.axis_index("c")
    chip = 2 * mx + my
    dev = 2 * chip + mc

    conv_sh = P["ffn_conv_w"].shape
    small_f = _pack([c, P["ffn_conv_w"]], F32, 8)
    small_all = _allgather8(small_f, "ag_small_fwd")
    c_all, conv_all = [], []
    for d in range(N_DEV):
        cd_, cw_ = _unpack(small_all[d], [(1, D), conv_sh])
        c_all.append(cd_)
        conv_all.append(cw_)
    conv_w = jnp.concatenate([conv_all[2 * j] for j in range(N_CHIPS)], axis=2)
    cond = jnp.concatenate(c_all + [P["c_ctx"].reshape(1, D), jnp.zeros((16 - N_DEV - 1, D), F32)], axis=0)

    n_sh = P["w_ada"].shape[2]
    b_sh = lax.dynamic_slice_in_dim(P["b_ada"], chip * n_sh, n_sh, axis=1)[:, None, :]
    mods_sh = _ada_fwd(cond, P["w_ada"], b_sh, "ada_fwd")
    mods4 = _comm(mods_sh, group="chips", by_peer=False, name="ag_mods")
    mods = jnp.concatenate([mods4[j] for j in range(N_CHIPS)], axis=-1)
    mod_lat = lax.dynamic_index_in_dim(mods, dev, axis=1, keepdims=False)
    mod_ctx = mods[:, N_DEV]
    modp = jnp.stack([mod_ctx, mod_lat], axis=1).reshape(DEPTH, 2, N_MOD, D)

    w4 = _comm_multi([P[n].astype(BF16).reshape(-1, P[n].shape[-1]) for n in _BIG], group="chips", by_peer=False,
                     name="ag_w_chips")
    Wg = {n: a.reshape((N_CHIPS,) + P[n].shape) for n, a in zip(_BIG, w4)}

    cos, sin = _rope_tables(dm)
    Wl = [dict(w_in=jnp.concatenate([Wg["w_in"][j, l] for j in range(N_CHIPS)], axis=1),
               w_glu=Wg["ssm_w_glu"][:, l].reshape(dm.SSM_W, dm.SSM_W),
               w_out=_Mat(Wg["w_out"], axis=0, layer=l), w_dn=_Mat(Wg["ffn_w_down"], axis=0, layer=l),
               w_up=_Mat(Wg["ffn_w_up"], axis=1, layer=l)) for l in range(DEPTH)]

    X = jnp.concatenate([ctx.reshape(C, D), x.reshape(L, D)], axis=0)
    saved = []
    for l in range(DEPTH):
        mp = modp[l]
        sh_a, sc_a, gt_a, sh_f, sc_f, gt_f = [mp[:, k] for k in range(N_MOD)]
        w_in, w_out, w_glu, w_up, w_dn = [Wl[l][k] for k in ("w_in", "w_out", "w_glu", "w_up", "w_dn")]
        g_mix, g_ffn = P["g_mix"][l][None], P["g_ffn"][l][None]

        h = _normmod(X, g_mix, sh_a, sc_a, C, "normmod_a")
        z = _matmul(h, w_in, mode="nn", name="mm_in")
        q, k2, v2 = _attn_prep(z, cos, sin, dm, "attn_prep")
        sinkcol = jnp.repeat(P["attn_sink"][l].reshape(2, GQA_RATIO), ATTN_BLOCK, axis=1)[..., None]
        o_attn = _attention(q, k2, v2, sinkcol, dm, "attention")
        u = z[:, dm.OFF_S:dm.OFF_GU]
        ul, uc = _interleave(u[C:], L), _interleave(u[:C], C)
        lam_re = P["ssm_lambda_re"][l].reshape(2, 1, dm.GP)
        lam_im = P["ssm_lambda_im"][l].reshape(2, 1, dm.GP)
        ldt = jnp.repeat(P["ssm_log_dt"][l], SSM_STATE, axis=-1).reshape(2, 1, dm.GP)
        bT_re = P["ssm_b_re"][l].reshape(2, dm.GP, SSM_GROUP).transpose(0, 2, 1)
        bT_im = P["ssm_b_im"][l].reshape(2, dm.GP, SSM_GROUP).transpose(0, 2, 1)
        a_re, a_im, bbT_re, bbT_im = _s5_disc(lam_re, lam_im, ldt, bT_re, bT_im, "s5_disc")
        bd_re, bd_im = _blockdiag_in(bbT_re, dm).astype(BF16), _blockdiag_in(bbT_im, dm).astype(BF16)
        cd_re = _blockdiag_out(P["ssm_c_re"][l], dm).astype(BF16)
        cd_im = _blockdiag_out(P["ssm_c_im"][l], dm).astype(BF16)
        dskip = P["ssm_d"][l][None]
        yl, yc = _s5_scan(ul, uc, a_re, a_im, bd_re, bd_im, cd_re, cd_im, dskip, dm, "s5_scan")
        y = jnp.concatenate([_deinterleave(yc, C), _deinterleave(yl, L)], axis=0)
        b_glu = P["ssm_b_glu"][l][None]
        o_ssm = _glu(y, w_glu, b_glu, "glu")
        ln_g, ln_b = P["gmlp_ln_g"][l][None], P["gmlp_ln_b"][l][None]
        w_s = P["gmlp_w_s"][l].astype(BF16)
        b_s = P["gmlp_b_s"][l][..., None]
        o_gmlp = _gmlp(z, ln_g, ln_b, w_s, b_s, dm, "gmlp")
        o_cat = jnp.concatenate([o_attn, o_ssm, o_gmlp], axis=-1)
        mix, X1 = _matmul_residual(o_cat, w_out, X, gt_a, C, "mm_out")
        h2 = _normmod(X1, g_ffn, sh_f, sc_f, C, "normmod_f")
        up = _matmul(h2, w_up, mode="nn", name="mm_up", out_dtype=BF16, out_split=(1, 2))
        cw, cb = conv_w[l], P["ffn_conv_b"][l][None]
        act = _convact(up, cw, cb, C, "convact")
        ffn, X2 = _matmul_residual(act, w_dn, X1, gt_f, C, "mm_down")
        saved.append(dict(X=X, h=h, z=z, q=q, k2=k2, v2=v2, sinkcol=sinkcol, ul=ul, uc=uc, a_re=a_re, a_im=a_im,
                          bd_re=bd_re, bd_im=bd_im, cd_re=cd_re, cd_im=cd_im, y=y, o_cat=o_cat, mix=mix, X1=X1, h2=h2,
                          up=up, act=act, ffn=ffn, lam_re=lam_re, lam_im=lam_im, ldt=ldt, bT_re=bT_re,
                          bT_im=bT_im))
        X = X2

    loss_l, dX, dg_final = _loss_head(X, P["g_final"][None], loss_target.reshape(L, D), C, "loss_head")
    loss = lax.psum(loss_l[0, 0], ("x", "y", "c"))

    G = {n: [None] * DEPTH for n in _WEIGHTS if n not in ("c_ctx", "w_ada", "b_ada", "g_final")}
    dmod = [None] * DEPTH
    for l in reversed(range(DEPTH)):
        S = saved[l]
        mp = modp[l]
        sh_a, sc_a, gt_a, sh_f, sc_f, gt_f = [mp[:, k] for k in range(N_MOD)]
        w_in, w_out, w_glu, w_up, w_dn = [Wl[l][k] for k in ("w_in", "w_out", "w_glu", "w_up", "w_dn")]
        g_mix, g_ffn = P["g_mix"][l][None], P["g_ffn"][l][None]
        cw, cb = conv_w[l], P["ffn_conv_b"][l][None]

        dffn, dgt_f = _gate_bwd(dX, S["ffn"], gt_f, C, "gate_bwd")
        dact = _matmul(dffn, w_dn, mode="nt", name="mm_down_dx", out_dtype=BF16)
        G["ffn_w_down"][l] = _matmul(S["act"], dffn, mode="tn", name="mm_down_dw", out_dtype=BF16,
                                     out_split=(0, N_CHIPS), tm_cap=1408)
        dup, dcw, dcb = _convact_bwd(S["up"], cw, cb, dact, C, "convact_bwd")
        G["ffn_conv_w"][l], G["ffn_conv_b"][l] = dcw, dcb[0]
        dh2 = _matmul(_Mat(dup, axis=1), w_up, mode="nt", name="mm_up_dx")
        G["ffn_w_up"][l] = _matmul(S["h2"], _Mat(dup, axis=1), mode="tn", name="mm_up_dw", out_dtype=BF16,
                                   out_split=(1, N_CHIPS))
        dX1, dg_f, dsh_f, dsc_f = _normmod_bwd(S["X1"], g_ffn, sc_f, dh2, dX, C, "normmod_bwd")
        G["g_ffn"][l] = dg_f[0]
        dmix, dgt_a = _gate_bwd(dX1, S["mix"], gt_a, C, "gate_bwd")
        do = _matmul(dmix, w_out, mode="nt", name="mm_out_dx")
        G["w_out"][l] = _matmul(S["o_cat"], dmix, mode="tn", name="mm_out_dw", out_dtype=BF16,
                                out_split=(0, N_CHIPS))
        do_attn = do_ssm = do_gmlp = do
        ln_g, ln_b = P["gmlp_ln_g"][l][None], P["gmlp_ln_b"][l][None]
        w_s = P["gmlp_w_s"][l].astype(BF16)
        b_s = P["gmlp_b_s"][l][..., None]
        dz_g, dlg, dlb, dws, dbs = _gmlp_bwd(S["z"], ln_g, ln_b, w_s, b_s, do_gmlp, dm, "gmlp_bwd")
        G["gmlp_ln_g"][l], G["gmlp_ln_b"][l], G["gmlp_w_s"][l], G["gmlp_b_s"][l] = dlg[0], dlb[0], dws, dbs[..., 0]
        b_glu = P["ssm_b_glu"][l][None]
        dy, dwglu, dbglu = _glu_bwd(S["y"], w_glu, b_glu, do_ssm, dm.ATTN_W // dm.SSM_W, "glu_bwd")
        G["ssm_w_glu"][l], G["ssm_b_glu"][l] = dwglu.astype(BF16).reshape(N_CHIPS, -1, dm.SSM_W), dbglu[0]
        dyl, dyc = _interleave(dy[C:], L), _interleave(dy[:C], C)
        tr = lambda a: jnp.swapaxes(a, 1, 2)
        dskip = P["ssm_d"][l][None]
        (dul, duc, da_re, da_im, dbd_re, dbd_im, dcd_re, dcd_im, dd) = _s5_scan_bwd(
            S["ul"], S["uc"], dyl, dyc, S["a_re"], S["a_im"], S["bd_re"], S["bd_im"], S["cd_re"], S["cd_im"],
            tr(S["bd_re"]), tr(S["bd_im"]), tr(S["cd_re"]), tr(S["cd_im"]), dskip, dm, "s5_scan_bwd")
        dz_s = jnp.concatenate([_deinterleave(duc, C), _deinterleave(dul, L)], axis=0)
        dlr, dli, dldt, dbTr, dbTi = _s5_disc_bwd(S["lam_re"], S["lam_im"], S["ldt"], S["bT_re"], S["bT_im"],
                                                  da_re, da_im, _blockdiag_in_T(dbd_re, dm),
                                                  _blockdiag_in_T(dbd_im, dm), "s5_disc_bwd")
        G["ssm_lambda_re"][l] = dlr.reshape(2, dm.SG, SSM_STATE)
        G["ssm_lambda_im"][l] = dli.reshape(2, dm.SG, SSM_STATE)
        G["ssm_log_dt"][l] = jnp.sum(dldt.reshape(2, dm.SG, SSM_STATE), axis=-1)
        G["ssm_b_re"][l] = dbTr.transpose(0, 2, 1).reshape(2, dm.SG, SSM_STATE, SSM_GROUP)
        G["ssm_b_im"][l] = dbTi.transpose(0, 2, 1).reshape(2, dm.SG, SSM_STATE, SSM_GROUP)
        G["ssm_c_re"][l] = _blockdiag_out_T(dcd_re, dm)
        G["ssm_c_im"][l] = _blockdiag_out_T(dcd_im, dm)
        G["ssm_d"][l] = dd[0]
        dq, pk, pv, dkc, dvc, dsk = _attention_bwd(S["q"], S["k2"], S["v2"], S["sinkcol"], do_attn, dm, "attention_bwd")
        G["attn_sink"][l] = dsk[:, :, 0].reshape(dm.NH)
        dk2 = _kv_reduce(pk, dkc, dm, "kv_reduce")
        dv2 = _kv_reduce(pv, dvc, dm, "kv_reduce")
        dz_a = _attn_prep_bwd(dq, dk2, dv2, cos, sin, dm, "attn_prep_bwd")
        dz = jnp.concatenate([dz_a, dz_s, dz_g], axis=-1).astype(BF16)
        dh = _matmul(dz, w_in, mode="nt", name="mm_in_dx")
        dw_in = _matmul(S["h"], dz, mode="tn", name="mm_in_dw", out_dtype=BF16)
        n_in = dm.N_IN // N_CHIPS
        G["w_in"][l] = jnp.stack([dw_in[:, j * n_in:(j + 1) * n_in] for j in range(N_CHIPS)])
        dX, dg_a, dsh_a, dsc_a = _normmod_bwd(S["X"], g_mix, sc_a, dh, dX1, C, "normmod_bwd")
        G["g_mix"][l] = dg_a[0]
        dmod[l] = jnp.stack([dsh_a, dsc_a, dgt_a, dsh_f, dsc_f, dgt_f], axis=1)

    grad_x = dX[C:].reshape(1, L, D)
    Gbig = {n: G.pop(n) for n in _BIG}
    G = {n: jnp.stack(v) for n, v in G.items()}
    G["g_final"] = dg_final[0]
    dmod = jnp.stack(dmod)

    small_shapes = [G[n].shape for n in _SMALL]
    gs_pack = _pack([G[n] for n in _SMALL], F32, 16 * N_DEV)
    r8 = gs_pack.shape[0] // N_DEV
    gs_parts = _comm(gs_pack.reshape(N_DEV, r8, PACK_W), group="all", by_peer=True, name="rs_small")
    gs_sum = _allgather8(_sum_slots(gs_parts, F32, "sum_small"), "ag_small_red").reshape(-1, PACK_W)
    for n, a in zip(_SMALL, _unpack(gs_sum, small_shapes)):
        G[n] = a
    allp = _allgather8(_pack([dmod], F32, 16), "ag_dmod")
    dmod_all = allp.reshape(N_DEV, -1)[:, :DEPTH * 2 * N_MOD * D]
    dmod_all = dmod_all.reshape(N_DEV, DEPTH, 2, N_MOD * D)
    dmod_ctx = _sum_slots(dmod_all[:, :, 0].reshape(N_DEV, DEPTH, N_MOD * D), F32, "sum_dmod_ctx")
    dmod16 = jnp.concatenate([jnp.swapaxes(dmod_all[:, :, 1], 0, 1), dmod_ctx[:, None],
                              jnp.zeros((DEPTH, 16 - N_DEV - 1, N_MOD * D), F32)], axis=1)
    G["b_ada"] = _sum_slots(jnp.swapaxes(dmod16, 0, 1)[:N_DEV + 1], F32, "sum_b_ada")
    dmod16_sh = lax.dynamic_slice_in_dim(dmod16, chip * n_sh, n_sh, axis=2)
    G["w_ada"], dact = _ada_bwd(cond, P["w_ada"], dmod16_sh, "ada_bwd")
    dact4 = _comm(dact, group="chips", by_peer=False, name="ag_dact")
    G["c_ctx"] = _cctx_grad(P["c_ctx"].reshape(1, D), dact4, "cctx_grad")[0]
    conv_cols = conv_sh[2]
    G["ffn_conv_w"] = lax.dynamic_slice_in_dim(G["ffn_conv_w"], chip * conv_cols, conv_cols, axis=2)

    LH = DEPTH // 2
    xs = [jnp.stack([jnp.stack([jnp.concatenate([Gbig[n][hh * LH + li][j] for li in range(LH)], axis=0)
                                for j in range(N_CHIPS)]) for hh in range(2)]) for n in _BIG]
    parts = _comm_multi(xs, group="all", by_peer=True, name="rs_all")
    for n, part in zip(_BIG, parts):
        red = _sum_slots(part, BF16, "sum_all")
        both = _comm(red, group="sib", by_peer=False, name="rs_share")
        G[n] = both.reshape(P[n].shape)

    delta, new_m, new_v = {}, {}, {}
    for n in _BIG + ("w_ada",):
        delta[n], new_m[n], new_v[n], G[n] = _adamw_nd(P[n], G[n], M[n], V[n], "adamw_" + n)
    rest = [n for n in _WEIGHTS if n not in _BIG + ("w_ada",)]
    shapes = [P[n].shape for n in rest]
    packed = [_pack([d[n] for n in rest], F32, 8) for d in (P, G, M, V)]
    outs = _adamw(*packed, "adamw_small")[:3]
    for dst, buf in zip((delta, new_m, new_v), outs):
        for n, a in zip(rest, _unpack(buf, shapes)):
            dst[n] = a
    return loss, grad_x, G, delta, new_m, new_v


def kernel(x, c, ctx, c_ctx, w_ada, b_ada, g_mix, g_ffn, w_in, w_out, attn_sink, ssm_lambda_re, ssm_lambda_im, ssm_log_dt, ssm_b_re, ssm_b_im, ssm_c_re, ssm_c_im, ssm_d, ssm_w_glu, ssm_b_glu, gmlp_ln_g, gmlp_ln_b, gmlp_w_s, gmlp_b_s, ffn_w_up, ffn_conv_w, ffn_conv_b, ffn_w_down, g_final, loss_target, m_c_ctx, m_w_ada, m_b_ada, m_g_mix, m_g_ffn, m_w_in, m_w_out, m_attn_sink, m_ssm_lambda_re, m_ssm_lambda_im, m_ssm_log_dt, m_ssm_b_re, m_ssm_b_im, m_ssm_c_re, m_ssm_c_im, m_ssm_d, m_ssm_w_glu, m_ssm_b_glu, m_gmlp_ln_g, m_gmlp_ln_b, m_gmlp_w_s, m_gmlp_b_s, m_ffn_w_up, m_ffn_conv_w, m_ffn_conv_b, m_ffn_w_down, m_g_final, v_c_ctx, v_w_ada, v_b_ada, v_g_mix, v_g_ffn, v_w_in, v_w_out, v_attn_sink, v_ssm_lambda_re, v_ssm_lambda_im, v_ssm_log_dt, v_ssm_b_re, v_ssm_b_im, v_ssm_c_re, v_ssm_c_im, v_ssm_d, v_ssm_w_glu, v_ssm_b_glu, v_gmlp_ln_g, v_gmlp_ln_b, v_gmlp_w_s, v_gmlp_b_s, v_ffn_w_up, v_ffn_conv_w, v_ffn_conv_b, v_ffn_w_down, v_g_final):
    env = locals()
    P = {n: env[n] for n in _WEIGHTS}
    M = {n: env["m_" + n] for n in _WEIGHTS}
    V = {n: env["v_" + n] for n in _WEIGHTS}
    loss, grad_x, G, delta, new_m, new_v = _step(P, M, V, x, c, ctx, loss_target)
    return (loss, grad_x, *[G[n] for n in _WEIGHTS], *[delta[n] for n in _WEIGHTS],
            *[new_m[n] for n in _WEIGHTS], *[new_v[n] for n in _WEIGHTS])
```

```python
import functools
import math

import jax
import jax.numpy as jnp
from jax import lax
from jax.experimental import pallas as pl
from jax.experimental.pallas import tpu as pltpu

F32 = jnp.float32
BF16 = jnp.bfloat16

D_MODEL = 2048
SEQ = 4096
DEPTH = 4
CTX_LEN = 256
GRID_W = 64
HEAD_DIM = 64
GQA_RATIO = 8
WINDOW = 128
ATTN_BLOCK = 128
ROPE_BASE = 10000.0
SSM_GROUP = 16
SSM_STATE = 64
GMLP_CHUNK = 128
GMLP_GROUP_W = 128
N_MOD = 6
NORM_EPS = 1e-6
ADAM_LR = 0.001
ADAM_B1 = 0.9
ADAM_B2 = 0.999
ADAM_EPS = 1e-08
ADAM_WD = 0.01
ADAM_STEP = 10

N_CHIPS = 4
N_DEV = 8
SCAN_SEGMENTS = 8
STATE_BLOCK = 256
STATE_BLOCK_FWD = 512
LANES = 128
PACK_W = 1024
COMM_CHUNKS = (16, 8, 4, 2)
STAGE_BYTES = 4 << 20
SUM_BLOCK_BYTES = 8 << 20
VMEM_LIMIT_MB = 56
_GELU_K = math.sqrt(2.0 / math.pi)
_GELU_C = 0.044715


class _Dims:
    def __init__(self):
        self.D = D_MODEL
        self.L = SEQ
        self.C = CTX_LEN
        self.T = SEQ + CTX_LEN
        self.ATTN_W = D_MODEL // 2
        self.SSM_W = D_MODEL // 4
        self.GMLP_W = D_MODEL - self.ATTN_W - self.SSM_W
        self.NH = self.ATTN_W // HEAD_DIM
        self.NKV = self.NH // GQA_RATIO
        self.KV_W = self.NKV * HEAD_DIM
        self.SG = self.SSM_W // SSM_GROUP
        self.GP = self.SG * SSM_STATE
        self.GG = self.GMLP_W // GMLP_GROUP_W
        self.DFF = ((8 * D_MODEL // 3 + 255) // 256) * 256
        self.OFF_K = self.ATTN_W
        self.OFF_V = self.OFF_K + self.KV_W
        self.OFF_S = self.OFF_V + self.KV_W
        self.OFF_GU = self.OFF_S + self.SSM_W
        self.OFF_GV = self.OFF_GU + self.GMLP_W
        self.N_IN = self.OFF_GV + self.GMLP_W
        assert self.NKV == 2 and self.KV_W == LANES
        assert self.C % (SCAN_SEGMENTS * 8) == 0 and self.L % (SCAN_SEGMENTS * 8) == 0
        assert self.C % ATTN_BLOCK == 0 and self.L % ATTN_BLOCK == 0


def _tile(n, cap, mult):
    best = None
    for t in range(mult, min(n, cap) + 1, mult):
        if n % t == 0:
            best = t
    return best if best is not None else n


def _cparams(sem=None):
    kw = dict(vmem_limit_bytes=VMEM_LIMIT_MB << 20)
    if sem is not None:
        kw["dimension_semantics"] = sem
    return pltpu.CompilerParams(**kw)


def _gelu(x):
    return 0.5 * x * (1.0 + jnp.tanh(_GELU_K * (x + _GELU_C * x * x * x)))


def _gelu_grad(x):
    t = jnp.tanh(_GELU_K * (x + _GELU_C * x * x * x))
    return 0.5 * (1.0 + t) + 0.5 * x * (1.0 - t * t) * _GELU_K * (1.0 + 3.0 * _GELU_C * x * x)


def _sigmoid(x):
    return 1.0 / (1.0 + jnp.exp(-x))


def _dot(a, b, dims):
    return lax.dot_general(a.astype(BF16), b.astype(BF16), (dims, ((), ())), preferred_element_type=F32)


_NN = ((1,), (0,))
_NT = ((1,), (1,))
_TN = ((0,), (0,))


def _sib_staged(x, by_peer, name):
    rows, W = x.shape[-2:]
    cr = _tile(rows, max(16, STAGE_BYTES // (W * x.dtype.itemsize) // 16 * 16), 16)
    nch = rows // cr

    def body(x_ref, o_ref, sbuf, rbuf, send_sems, recv_sems, ld_sem, st_sem, own_sem, credit_sem):
        mx, my, mc = lax.axis_index("x"), lax.axis_index("y"), lax.axis_index("c")
        sib = (mx, my, 1 - mc)
        mine = x_ref.at[mc] if by_peer else x_ref
        theirs = x_ref.at[1 - mc] if by_peer else x_ref
        own = pltpu.make_async_copy(mine, o_ref.at[mc], own_sem)
        own.start()

        def step(i, carry):
            s = i % 2
            r0 = pl.multiple_of(i * cr, 16)
            ld = pltpu.make_async_copy(theirs.at[pl.ds(r0, cr), :], sbuf.at[s], ld_sem)
            ld.start()
            ld.wait()

            @pl.when(i >= 2)
            def _():
                pl.semaphore_wait(credit_sem, 1)

            cp = pltpu.make_async_remote_copy(src_ref=sbuf.at[s], dst_ref=rbuf.at[s], send_sem=send_sems.at[s],
                                              recv_sem=recv_sems.at[s], device_id=sib,
                                              device_id_type=pl.DeviceIdType.MESH)
            cp.start()
            cp.wait_recv()
            st = pltpu.make_async_copy(rbuf.at[s], o_ref.at[1 - mc, pl.ds(r0, cr), :], st_sem)
            st.start()
            st.wait()

            @pl.when(i < nch - 2)
            def _():
                pl.semaphore_signal(credit_sem, inc=1, device_id=sib, device_id_type=pl.DeviceIdType.MESH)

            cp.wait_send()
            return carry

        lax.fori_loop(0, nch, step, 0)
        own.wait()

    return pl.pallas_call(
        body, name=name,
        out_shape=jax.ShapeDtypeStruct((2, rows, W), x.dtype),
        in_specs=[pl.BlockSpec(memory_space=pl.ANY)],
        out_specs=pl.BlockSpec(memory_space=pl.ANY),
        scratch_shapes=[pltpu.VMEM((2, cr, W), x.dtype), pltpu.VMEM((2, cr, W), x.dtype),
                        pltpu.SemaphoreType.DMA((2,)), pltpu.SemaphoreType.DMA((2,)),
                        pltpu.SemaphoreType.DMA, pltpu.SemaphoreType.DMA, pltpu.SemaphoreType.DMA,
                        pltpu.SemaphoreType.REGULAR],
        compiler_params=_cparams(),
    )(x)


def _comm(x, *, group, by_peer, name):
    n = {"sib": 2, "chips": N_CHIPS, "all": N_DEV}[group]
    blk = x.shape[1:] if by_peer else x.shape
    npeer = n - 1
    W = blk[-1]
    rows = math.prod(blk[:-1])
    x2 = x.reshape((n, rows, W) if by_peer else (rows, W))
    if group == "sib":
        return _sib_staged(x2, by_peer, name).reshape((n,) + tuple(blk))
    nchunk = next((k for k in COMM_CHUNKS if rows % (k * 16) == 0), 1)
    cr = rows // nchunk

    def body(x_ref, o_ref, send_sems, recv_sems, local_sem):
        mx, my, mc = lax.axis_index("x"), lax.axis_index("y"), lax.axis_index("c")
        if group == "chips":
            me = 2 * mx + my
            peers = [((mx ^ fx, my ^ fy, mc), 2 * (mx ^ fx) + (my ^ fy)) for fx, fy in ((1, 0), (0, 1), (1, 1))]
        else:
            me = 4 * mx + 2 * my + mc
            peers = [((mx ^ fx, my ^ fy, mc ^ fc), 4 * (mx ^ fx) + 2 * (my ^ fy) + (mc ^ fc))
                     for fx in (0, 1) for fy in (0, 1) for fc in (0, 1) if fx + fy + fc]

        def src_of(pid):
            return x_ref.at[pid] if by_peer else x_ref

        def chunk(ref, ch):
            return ref.at[pl.ds(ch * cr, cr), :]

        def remote(k, dev, pid, ch=None):
            s, d = src_of(pid), o_ref.at[me]
            if ch is not None:
                s, d = chunk(s, ch), chunk(d, ch)
            return pltpu.make_async_remote_copy(src_ref=s, dst_ref=d, send_sem=send_sems.at[k],
                                                recv_sem=recv_sems.at[k], device_id=dev,
                                                device_id_type=pl.DeviceIdType.MESH)

        for ch in range(nchunk):
            pltpu.make_async_copy(chunk(src_of(me), ch), chunk(o_ref.at[me], ch), local_sem).start()
            for k, (dev, pid) in enumerate(peers):
                remote(k, dev, pid, ch).start()
        for k, (dev, pid) in enumerate(peers):
            remote(k, dev, pid).wait_recv()
        for k, (dev, pid) in enumerate(peers):
            remote(k, dev, pid).wait_send()
        pltpu.make_async_copy(src_of(me), o_ref.at[me], local_sem).wait()

    out = pl.pallas_call(
        body, name=name,
        out_shape=jax.ShapeDtypeStruct((n, rows, W), x.dtype),
        in_specs=[pl.BlockSpec(memory_space=pl.ANY)],
        out_specs=pl.BlockSpec(memory_space=pl.ANY),
        scratch_shapes=[pltpu.SemaphoreType.DMA((npeer,)), pltpu.SemaphoreType.DMA((npeer,)),
                        pltpu.SemaphoreType.DMA],
    )(x2)
    return out.reshape((n,) + tuple(blk))


def _carried_allgather(xs):
    nt = len(xs)
    masks = ((1, 0), (0, 1), (1, 1))

    def copies(x_refs, o_refs, sems):
        send_sems, recv_sems, local_sems = sems
        mx, my, mc = lax.axis_index("x"), lax.axis_index("y"), lax.axis_index("c")
        me = 2 * mx + my
        local = [pltpu.make_async_copy(x_refs[t], o_refs[t].at[me], local_sems.at[t]) for t in range(nt)]
        remote = [pltpu.make_async_remote_copy(
            src_ref=x_refs[t], dst_ref=o_refs[t].at[me], send_sem=send_sems.at[t * 3 + k],
            recv_sem=recv_sems.at[t * 3 + k], device_id=(mx ^ fx, my ^ fy, mc),
            device_id_type=pl.DeviceIdType.MESH) for t in range(nt) for k, (fx, fy) in enumerate(masks)]
        return local, remote

    def start(x_refs, o_refs, sems):
        local, remote = copies(x_refs, o_refs, sems)
        for cp in local + remote:
            cp.start()

    def wait(x_refs, o_refs, sems):
        local, remote = copies(x_refs, o_refs, sems)
        for cp in remote:
            cp.wait_recv()
        for cp in remote:
            cp.wait_send()
        for cp in local:
            cp.wait()

    out_shapes = [jax.ShapeDtypeStruct((N_CHIPS,) + tuple(x.shape), x.dtype) for x in xs]
    scratch = [pltpu.SemaphoreType.DMA((nt * 3,)), pltpu.SemaphoreType.DMA((nt * 3,)), pltpu.SemaphoreType.DMA((nt,))]
    return out_shapes, scratch, start, wait


def _comm_multi(xs, *, group, by_peer, name):
    nt = len(xs)
    n = {"chips": N_CHIPS, "all": N_DEV}[group]
    npeer = n - 1

    def body(*refs):
        x_refs, o_refs = refs[:nt], refs[nt:2 * nt]
        send_sems, recv_sems, local_sems = refs[2 * nt:]
        mx, my, mc = lax.axis_index("x"), lax.axis_index("y"), lax.axis_index("c")
        if group == "chips":
            me = 2 * mx + my
            masks = [(fx, fy, 0) for fx, fy in ((1, 0), (0, 1), (1, 1))]
        else:
            me = 4 * mx + 2 * my + mc
            masks = [(fx, fy, fc) for fx in (0, 1) for fy in (0, 1) for fc in (0, 1) if fx + fy + fc]

        def src(t, px, py, pc):
            return x_refs[t].at[pc, 2 * px + py] if by_peer else x_refs[t]

        copies = []
        for t in range(nt):
            pltpu.make_async_copy(src(t, mx, my, mc), o_refs[t].at[me], local_sems.at[t]).start()
            for k, (fx, fy, fc) in enumerate(masks):
                px, py, pc = mx ^ fx, my ^ fy, mc ^ fc
                cp = pltpu.make_async_remote_copy(
                    src_ref=src(t, px, py, pc), dst_ref=o_refs[t].at[me],
                    send_sem=send_sems.at[t * npeer + k], recv_sem=recv_sems.at[t * npeer + k],
                    device_id=(px, py, pc), device_id_type=pl.DeviceIdType.MESH)
                cp.start()
                copies.append(cp)
        for cp in copies:
            cp.wait_recv()
        for cp in copies:
            cp.wait_send()
        for t in range(nt):
            pltpu.make_async_copy(src(t, mx, my, mc), o_refs[t].at[me], local_sems.at[t]).wait()

    any_spec = pl.BlockSpec(memory_space=pl.ANY)
    return pl.pallas_call(
        body, name=name,
        out_shape=[jax.ShapeDtypeStruct((n,) + tuple(x.shape[-2:]), x.dtype) for x in xs],
        in_specs=[any_spec] * nt, out_specs=[any_spec] * nt,
        scratch_shapes=[pltpu.SemaphoreType.DMA((nt * npeer,)), pltpu.SemaphoreType.DMA((nt * npeer,)),
                        pltpu.SemaphoreType.DMA((nt,))],
    )(*xs)


def _allgather8(x, name):
    return _comm(x, group="all", by_peer=False, name=name)


def _sum_slots(x, out_dtype, name):
    n, R, W = x.shape
    tr = _tile(R, max(16, min(512, SUM_BLOCK_BYTES // (n * W * x.dtype.itemsize) // 16 * 16)), 16)

    def body(x_ref, o_ref):
        acc = x_ref[0].astype(F32)
        for s in range(1, n):
            acc = acc + x_ref[s].astype(F32)
        o_ref[...] = acc.astype(o_ref.dtype)

    return pl.pallas_call(
        body, name=name, grid=(R // tr,),
        in_specs=[pl.BlockSpec((n, tr, W), lambda i: (0, i, 0))],
        out_specs=pl.BlockSpec((tr, W), lambda i: (i, 0)),
        out_shape=jax.ShapeDtypeStruct((R, W), out_dtype),
        compiler_params=_cparams(("parallel",)),
    )(x)


def _pack(arrs, dtype, row_mult):
    flat = jnp.concatenate([a.reshape(-1).astype(dtype) for a in arrs])
    n = flat.shape[0]
    quant = row_mult * PACK_W
    total = -(-n // quant) * quant
    if total != n:
        flat = jnp.concatenate([flat, jnp.zeros((total - n,), dtype)])
    return flat.reshape(total // PACK_W, PACK_W)


def _unpack(buf, shapes):
    flat = buf.reshape(-1)
    out, off = [], 0
    for s in shapes:
        n = math.prod(s)
        out.append(flat[off:off + n].reshape(s))
        off += n
    return out


class _Mat:
    def __init__(self, arr, axis=None, layer=None, part0=0, nparts=None):
        self.arr, self.axis, self.layer, self.part0 = arr, axis, layer, part0
        self.r, self.c = arr.shape[-2:]
        self.nparts = 1 if axis is None else (arr.shape[0] if nparts is None else nparts)

    @property
    def shape(self):
        return (self.r * (self.nparts if self.axis == 0 else 1), self.c * (self.nparts if self.axis == 1 else 1))

    def unit(self, ax):
        return self.r if ax == 0 else self.c

    def spec(self, tr, tc, fn):
        def imap(*g):
            rb, cb = fn(*g)
            lead = []
            if self.axis == 0:
                per = self.r // tr
                lead.append(self.part0 + rb // per)
                rb = rb % per
            elif self.axis == 1:
                per = self.c // tc
                lead.append(self.part0 + cb // per)
                cb = cb % per
            if self.layer is not None:
                lead.append(self.layer)
            return (*lead, rb, cb)

        return pl.BlockSpec((None,) * (self.arr.ndim - 2) + (tr, tc), imap)


def _as_mat(x):
    return x if isinstance(x, _Mat) else _Mat(x)


def _matmul(a, b, *, mode, name, out_dtype=F32, add=None, out_split=None, carry=None, tm_cap=1088, tn_cap=1536,
            tk_cap=2048):
    a, b = _as_mat(a), _as_mat(b)
    am, ak = (0, 1) if mode != "tn" else (1, 0)
    bk, bn = (0, 1) if mode != "nt" else (1, 0)
    M, K, N = a.shape[am], a.shape[ak], b.shape[bn]
    assert b.shape[bk] == K
    if out_split is None:
        o = _Mat(jax.ShapeDtypeStruct((M, N), out_dtype))
    else:
        ax, parts = out_split
        o = _Mat(jax.ShapeDtypeStruct((parts, M // parts if ax == 0 else M, N // parts if ax == 1 else N), out_dtype),
                 axis=ax)
    tm = _tile(math.gcd(a.unit(am), o.unit(0)), tm_cap, 16 if mode != "tn" else LANES)
    tn = _tile(math.gcd(b.unit(bn), o.unit(1)), tn_cap, LANES)
    tk = _tile(math.gcd(a.unit(ak), b.unit(bk)), tk_cap, LANES if mode != "tn" else 16)
    nk = K // tk
    dims = {"nn": _NN, "nt": _NT, "tn": _TN}[mode]

    carry = list(carry or [])
    nc = len(carry)
    c_out, c_scratch, c_start, c_wait = _carried_allgather(carry)
    grid = (M // tm, N // tn, nk)

    def body(*refs):
        a_ref, b_ref = refs[:2]
        add_ref = refs[2] if add is not None else None
        n_in = 2 + (add is not None)
        cx_refs, o_ref, co_refs = refs[n_in:n_in + nc], refs[n_in + nc], refs[n_in + nc + 1:n_in + 2 * nc + 1]
        acc_ref, sems = refs[n_in + 2 * nc + 1], refs[n_in + 2 * nc + 2:]
        i, j, k = pl.program_id(0), pl.program_id(1), pl.program_id(2)
        if nc:
            @pl.when((i == 0) & (j == 0) & (k == 0))
            def _():
                c_start(cx_refs, co_refs, sems)

        @pl.when(k == 0)
        def _():
            acc_ref[...] = jnp.zeros_like(acc_ref)

        acc_ref[...] += _dot(a_ref[...], b_ref[...], dims)

        @pl.when(k == nk - 1)
        def _():
            r = acc_ref[...]
            if add_ref is not None:
                r = r + add_ref[...].astype(F32)
            o_ref[...] = r.astype(o_ref.dtype)

        if nc:
            @pl.when((i == grid[0] - 1) & (j == grid[1] - 1) & (k == nk - 1))
            def _():
                c_wait(cx_refs, co_refs, sems)

    if mode == "tn":
        a_spec = a.spec(tk, tm, lambda i, j, k: (k, i))
    else:
        a_spec = a.spec(tm, tk, lambda i, j, k: (i, k))
    if mode == "nt":
        b_spec = b.spec(tn, tk, lambda i, j, k: (j, k))
    else:
        b_spec = b.spec(tk, tn, lambda i, j, k: (k, j))
    o_spec = o.spec(tm, tn, lambda i, j, k: (i, j))
    any_spec = pl.BlockSpec(memory_space=pl.ANY)
    in_specs = [a_spec, b_spec] + ([pl.BlockSpec((tm, tn), lambda i, j, k: (i, j))] if add is not None else [])
    args = (a.arr, b.arr) + ((add,) if add is not None else ())
    outs = pl.pallas_call(
        body, name=name, grid=grid,
        in_specs=in_specs + [any_spec] * nc, out_specs=[o_spec] + [any_spec] * nc,
        out_shape=[o.arr] + c_out,
        scratch_shapes=[pltpu.VMEM((tm, tn), F32)] + (c_scratch if nc else []),
        compiler_params=_cparams(("arbitrary",) * 3 if nc else ("parallel", "parallel", "arbitrary")),
    )(*args, *carry)
    return (outs[0], outs[1:]) if nc else outs[0]


def _matmul_residual(a, w, res, gate, C, name, carry=None):
    w = _as_mat(w)
    M, K = a.shape
    N = w.shape[1]
    tm = _tile(M, 1088, 16)
    tn = _tile(w.unit(1), 1024, LANES)
    tk = _tile(w.unit(0), 2048, LANES)
    nk = K // tk
    carry = list(carry or [])
    nc = len(carry)
    c_out, c_scratch, c_start, c_wait = _carried_allgather(carry)
    grid = (M // tm, N // tn, nk)

    def body(*refs):
        a_ref, w_ref, res_ref, gate_ref = refs[:4]
        cx_refs, (mix_ref, o_ref), co_refs = refs[4:4 + nc], refs[4 + nc:6 + nc], refs[6 + nc:6 + 2 * nc]
        acc_ref, sems = refs[6 + 2 * nc], refs[7 + 2 * nc:]
        i, j, k = pl.program_id(0), pl.program_id(1), pl.program_id(2)
        if nc:
            @pl.when((i == 0) & (j == 0) & (k == 0))
            def _():
                c_start(cx_refs, co_refs, sems)

        @pl.when(k == 0)
        def _():
            acc_ref[...] = jnp.zeros_like(acc_ref)

        acc_ref[...] += _dot(a_ref[...], w_ref[...], _NN)

        @pl.when(k == nk - 1)
        def _():
            r = acc_ref[...]
            row = i * tm + lax.broadcasted_iota(jnp.int32, (tm, 1), 0)
            g = jnp.where(row < C, gate_ref[0:1, :], gate_ref[1:2, :])
            mix_ref[...] = r
            o_ref[...] = res_ref[...] + g * r

        if nc:
            @pl.when((i == grid[0] - 1) & (j == grid[1] - 1) & (k == nk - 1))
            def _():
                c_wait(cx_refs, co_refs, sems)

    o_spec = pl.BlockSpec((tm, tn), lambda i, j, k: (i, j))
    any_spec = pl.BlockSpec(memory_space=pl.ANY)
    outs = pl.pallas_call(
        body, name=name, grid=grid,
        in_specs=[pl.BlockSpec((tm, tk), lambda i, j, k: (i, k)),
                  w.spec(tk, tn, lambda i, j, k: (k, j)),
                  o_spec,
                  pl.BlockSpec((2, tn), lambda i, j, k: (0, j))] + [any_spec] * nc,
        out_specs=[o_spec, o_spec] + [any_spec] * nc,
        out_shape=[jax.ShapeDtypeStruct((M, N), F32), jax.ShapeDtypeStruct((M, N), F32)] + c_out,
        scratch_shapes=[pltpu.VMEM((tm, tn), F32)] + (c_scratch if nc else []),
        compiler_params=_cparams(("arbitrary",) * 3 if nc else ("parallel", "parallel", "arbitrary")),
    )(a, w.arr, res, gate, *carry)
    return (outs[0], outs[1], outs[2:]) if nc else (outs[0], outs[1])


def _seg_select(i, tm, C, ref):
    row = i * tm + lax.broadcasted_iota(jnp.int32, (tm, 1), 0)
    return row < C, jnp.where(row < C, ref[0:1, :], ref[1:2, :])


def _normmod(X, g, shift, scale, C, name):
    T, D = X.shape
    tm = _tile(T, 512, 16)

    def body(x_ref, g_ref, sh_ref, sc_ref, h_ref):
        i = pl.program_id(0)
        x = x_ref[...]
        r = lax.rsqrt(jnp.mean(x * x, axis=-1, keepdims=True) + NORM_EPS)
        _, sh = _seg_select(i, tm, C, sh_ref)
        _, sc = _seg_select(i, tm, C, sc_ref)
        h_ref[...] = ((x * r) * g_ref[...] * (1.0 + sc) + sh).astype(BF16)

    row = pl.BlockSpec((tm, D), lambda i: (i, 0))
    vec = pl.BlockSpec((1, D), lambda i: (0, 0))
    two = pl.BlockSpec((2, D), lambda i: (0, 0))
    return pl.pallas_call(
        body, name=name, grid=(T // tm,),
        in_specs=[row, vec, two, two], out_specs=row,
        out_shape=jax.ShapeDtypeStruct((T, D), BF16),
        compiler_params=_cparams(("parallel",)),
    )(X, g, shift, scale)


def _normmod_bwd(X, g, scale, dh, dres, C, name):
    T, D = X.shape
    tm = _tile(T, 512, 16)

    def body(x_ref, g_ref, sc_ref, dh_ref, dres_ref, dx_ref, dg_ref, dsh_ref, dsc_ref):
        i = pl.program_id(0)

        @pl.when(i == 0)
        def _():
            dg_ref[...] = jnp.zeros_like(dg_ref)
            dsh_ref[...] = jnp.zeros_like(dsh_ref)
            dsc_ref[...] = jnp.zeros_like(dsc_ref)

        x = x_ref[...]
        dh = dh_ref[...]
        gv = g_ref[...]
        r = lax.rsqrt(jnp.mean(x * x, axis=-1, keepdims=True) + NORM_EPS)
        xh = x * r
        isc, sc = _seg_select(i, tm, C, sc_ref)
        n = xh * gv
        dhn = dh * n
        zero = jnp.zeros_like(dh)
        dsh_ref[0:1, :] += jnp.sum(jnp.where(isc, dh, zero), axis=0, keepdims=True)
        dsh_ref[1:2, :] += jnp.sum(jnp.where(isc, zero, dh), axis=0, keepdims=True)
        dsc_ref[0:1, :] += jnp.sum(jnp.where(isc, dhn, zero), axis=0, keepdims=True)
        dsc_ref[1:2, :] += jnp.sum(jnp.where(isc, zero, dhn), axis=0, keepdims=True)
        dn = dh * (1.0 + sc)
        dg_ref[...] += jnp.sum(dn * xh, axis=0, keepdims=True)
        dxh = dn * gv
        dx_ref[...] = dres_ref[...] + r * (dxh - xh * jnp.mean(dxh * xh, axis=-1, keepdims=True))

    row = pl.BlockSpec((tm, D), lambda i: (i, 0))
    vec = pl.BlockSpec((1, D), lambda i: (0, 0))
    two = pl.BlockSpec((2, D), lambda i: (0, 0))
    return pl.pallas_call(
        body, name=name, grid=(T // tm,),
        in_specs=[row, vec, two, row, row], out_specs=[row, vec, two, two],
        out_shape=[jax.ShapeDtypeStruct((T, D), F32), jax.ShapeDtypeStruct((1, D), F32),
                   jax.ShapeDtypeStruct((2, D), F32), jax.ShapeDtypeStruct((2, D), F32)],
        compiler_params=_cparams(("arbitrary",)),
    )(X, g, scale, dh, dres)


def _gate_bwd(dxo, mix, gate, C, name):
    T, D = dxo.shape
    tm = _tile(T, 512, 16)

    def body(dx_ref, mix_ref, gate_ref, dmix_ref, dgate_ref):
        i = pl.program_id(0)

        @pl.when(i == 0)
        def _():
            dgate_ref[...] = jnp.zeros_like(dgate_ref)

        dx = dx_ref[...]
        isc, gt = _seg_select(i, tm, C, gate_ref)
        dmix_ref[...] = (dx * gt).astype(BF16)
        p = dx * mix_ref[...]
        zero = jnp.zeros_like(p)
        dgate_ref[0:1, :] += jnp.sum(jnp.where(isc, p, zero), axis=0, keepdims=True)
        dgate_ref[1:2, :] += jnp.sum(jnp.where(isc, zero, p), axis=0, keepdims=True)

    row = pl.BlockSpec((tm, D), lambda i: (i, 0))
    two = pl.BlockSpec((2, D), lambda i: (0, 0))
    return pl.pallas_call(
        body, name=name, grid=(T // tm,),
        in_specs=[row, row, two], out_specs=[row, two],
        out_shape=[jax.ShapeDtypeStruct((T, D), BF16), jax.ShapeDtypeStruct((2, D), F32)],
        compiler_params=_cparams(("arbitrary",)),
    )(dxo, mix, gate)


def _conv_masks(T, C):
    row = lax.broadcasted_iota(jnp.int32, (T, 1), 0)
    first = (row == 0) | (row == C)
    last = (row == C - 1) | (row == T - 1)
    return first, last


def _shift_rows(x, first, last):
    T = x.shape[0]
    prev = jnp.where(first, 0.0, pltpu.roll(x, 1, 0))
    nxt = jnp.where(last, 0.0, pltpu.roll(x, T - 1, 0))
    return prev, nxt


def _convact(up, cw, cb, C, name):
    _, T, F = up.shape
    tc = LANES

    def body(u_ref, w_ref, b_ref, o_ref):
        gp = u_ref[0].astype(F32)
        first, last = _conv_masks(T, C)
        prev, nxt = _shift_rows(gp, first, last)
        gc = prev * w_ref[0:1, :] + gp * w_ref[1:2, :] + nxt * w_ref[2:3, :] + b_ref[...]
        o_ref[...] = (gc * _sigmoid(gc) * u_ref[1].astype(F32)).astype(BF16)

    col = pl.BlockSpec((T, tc), lambda j: (0, j))
    return pl.pallas_call(
        body, name=name, grid=(F // tc,),
        in_specs=[pl.BlockSpec((2, T, tc), lambda j: (0, 0, j)), pl.BlockSpec((3, tc), lambda j: (0, j)),
                  pl.BlockSpec((1, tc), lambda j: (0, j))],
        out_specs=col, out_shape=jax.ShapeDtypeStruct((T, F), BF16),
        compiler_params=_cparams(("parallel",)),
    )(up, cw, cb)


def _convact_bwd(up, cw, cb, dact, C, name):
    _, T, F = up.shape
    tc = LANES

    def body(u_ref, w_ref, b_ref, da_ref, du_ref, dw_ref, db_ref):
        gp = u_ref[0].astype(F32)
        val = u_ref[1].astype(F32)
        da = da_ref[...].astype(F32)
        first, last = _conv_masks(T, C)
        prev, nxt = _shift_rows(gp, first, last)
        w0, w1, w2 = w_ref[0:1, :], w_ref[1:2, :], w_ref[2:3, :]
        gc = prev * w0 + gp * w1 + nxt * w2 + b_ref[...]
        s = _sigmoid(gc)
        du_ref[1] = (da * gc * s).astype(BF16)
        dgc = da * val * (s + gc * s * (1.0 - s))
        db_ref[...] = jnp.sum(dgc, axis=0, keepdims=True)
        dw_ref[0:1, :] = jnp.sum(dgc * prev, axis=0, keepdims=True)
        dw_ref[1:2, :] = jnp.sum(dgc * gp, axis=0, keepdims=True)
        dw_ref[2:3, :] = jnp.sum(dgc * nxt, axis=0, keepdims=True)
        dprev, dnxt = _shift_rows(dgc, first, last)
        du_ref[0] = (dnxt * w0 + dgc * w1 + dprev * w2).astype(BF16)

    col = pl.BlockSpec((T, tc), lambda j: (0, j))
    two = pl.BlockSpec((2, T, tc), lambda j: (0, 0, j))
    w3 = pl.BlockSpec((3, tc), lambda j: (0, j))
    w1 = pl.BlockSpec((1, tc), lambda j: (0, j))
    return pl.pallas_call(
        body, name=name, grid=(F // tc,),
        in_specs=[two, w3, w1, col], out_specs=[two, w3, w1],
        out_shape=[jax.ShapeDtypeStruct((2, T, F), BF16),
                   jax.ShapeDtypeStruct((3, F), F32), jax.ShapeDtypeStruct((1, F), F32)],
        compiler_params=_cparams(("parallel",)),
    )(up, cw, cb, dact)


def _rot_half(x):
    W = x.shape[-1]
    lane = lax.broadcasted_iota(jnp.int32, x.shape, x.ndim - 1)
    lo = (lane % 32) < 16
    return jnp.where(lo, -pltpu.roll(x, W - 16, x.ndim - 1), pltpu.roll(x, 16, x.ndim - 1))


def _swap_halves(x):
    return pltpu.roll(x, 64, x.ndim - 1)


def _rope_tables(dm):
    t = jnp.arange(dm.L)
    n_freq = HEAD_DIM // 4
    inv_freq = ROPE_BASE ** (-jnp.arange(n_freq, dtype=F32) / n_freq)
    ang_r = (t // GRID_W).astype(F32)[:, None] * inv_freq
    ang_c = (t % GRID_W).astype(F32)[:, None] * inv_freq
    ang = jnp.concatenate([ang_r, ang_r, ang_c, ang_c], axis=-1)
    cos = jnp.concatenate([jnp.ones((dm.C, HEAD_DIM), F32), jnp.cos(ang)], axis=0)
    sin = jnp.concatenate([jnp.zeros((dm.C, HEAD_DIM), F32), jnp.sin(ang)], axis=0)
    return jnp.tile(cos, (1, 2)), jnp.tile(sin, (1, 2))


def _attn_prep(z, cos, sin, dm, name):
    T = dm.T
    AW = dm.ATTN_W
    tm = _tile(T, 256, 16)
    rep = AW // LANES

    def body(z_ref, cos_ref, sin_ref, q_ref, k_ref, v_ref):
        cs, sn = cos_ref[...], sin_ref[...]
        q = z_ref[:, 0:AW]
        csq, snq = jnp.tile(cs, (1, rep)), jnp.tile(sn, (1, rep))
        q_ref[...] = (q * csq + _rot_half(q) * snq).astype(BF16)
        k = z_ref[:, dm.OFF_K:dm.OFF_V]
        k = k * cs + _rot_half(k) * sn
        v = z_ref[:, dm.OFF_V:dm.OFF_S]
        lo = lax.broadcasted_iota(jnp.int32, k.shape, 1) < HEAD_DIM
        ks, vs = _swap_halves(k), _swap_halves(v)
        k_ref[0] = jnp.where(lo, k, ks).astype(BF16)
        k_ref[1] = jnp.where(lo, ks, k).astype(BF16)
        v_ref[0] = jnp.where(lo, v, vs).astype(BF16)
        v_ref[1] = jnp.where(lo, vs, v).astype(BF16)

    tab = pl.BlockSpec((tm, LANES), lambda i: (i, 0))
    kv = pl.BlockSpec((2, tm, LANES), lambda i: (0, i, 0))
    return pl.pallas_call(
        body, name=name, grid=(T // tm,),
        in_specs=[pl.BlockSpec((tm, dm.OFF_S), lambda i: (i, 0)), tab, tab],
        out_specs=[pl.BlockSpec((tm, AW), lambda i: (i, 0)), kv, kv],
        out_shape=[jax.ShapeDtypeStruct((T, AW), BF16), jax.ShapeDtypeStruct((2, T, LANES), BF16),
                   jax.ShapeDtypeStruct((2, T, LANES), BF16)],
        compiler_params=_cparams(("parallel",)),
    )(z, cos, sin)


def _attn_prep_bwd(dq, dk2, dv2, cos, sin, dm, name):
    T = dm.T
    AW = dm.ATTN_W
    tm = _tile(T, 256, 16)
    rep = AW // LANES

    def body(dq_ref, dk_ref, dv_ref, cos_ref, sin_ref, o_ref):
        cs, sn = cos_ref[...], sin_ref[...]
        csq, snq = jnp.tile(cs, (1, rep)), jnp.tile(sn, (1, rep))
        dq = dq_ref[...]
        o_ref[:, 0:AW] = dq * csq - _rot_half(dq * snq)
        lo = lax.broadcasted_iota(jnp.int32, (tm, LANES), 1) < HEAD_DIM
        dk = jnp.where(lo, dk_ref[0], dk_ref[1])
        o_ref[:, dm.OFF_K:dm.OFF_V] = dk * cs - _rot_half(dk * sn)
        o_ref[:, dm.OFF_V:dm.OFF_S] = jnp.where(lo, dv_ref[0], dv_ref[1])

    tab = pl.BlockSpec((tm, LANES), lambda i: (i, 0))
    kv = pl.BlockSpec((2, tm, LANES), lambda i: (0, i, 0))
    return pl.pallas_call(
        body, name=name, grid=(T // tm,),
        in_specs=[pl.BlockSpec((tm, AW), lambda i: (i, 0)), kv, kv, tab, tab],
        out_specs=pl.BlockSpec((tm, dm.OFF_S), lambda i: (i, 0)),
        out_shape=jax.ShapeDtypeStruct((T, dm.OFF_S), F32),
        compiler_params=_cparams(("parallel",)),
    )(dq, dk2, dv2, cos, sin)


def _attn_specs(dm):
    B = ATTN_BLOCK
    nblk = dm.T // B
    q_spec = pl.BlockSpec((B, dm.ATTN_W), lambda i: (i, 0))
    prev = pl.BlockSpec((2, B, LANES), lambda i: (0, jnp.maximum(i - 1, 0), 0))
    own = pl.BlockSpec((2, B, LANES), lambda i: (0, i, 0))
    nxt = pl.BlockSpec((2, B, LANES), lambda i: (0, jnp.minimum(i + 1, nblk - 1), 0))
    ctx = pl.BlockSpec((2, dm.C, LANES), lambda i: (0, 0, 0))
    sink = pl.BlockSpec((2, GQA_RATIO * B, 1), lambda i: (0, 0, 0))
    return nblk, q_spec, prev, own, nxt, ctx, sink


def _attn_scores(i, q_ref, kp_ref, ko_ref, kn_ref, kc_ref, sink_ref, h, dm):
    B = ATTN_BLOCK
    nctx = dm.C // B
    nb = dm.L // B
    npair = GQA_RATIO // 2
    lo = lax.broadcasted_iota(jnp.int32, (B, LANES), 1) < HEAD_DIM
    zero = jnp.zeros((B, LANES), BF16)
    parts = []
    for p in range(npair):
        q2 = q_ref[:, (h * npair + p) * LANES:(h * npair + p + 1) * LANES]
        parts.append(jnp.where(lo, q2, zero))
        parts.append(jnp.where(lo, zero, q2))
    q8 = jnp.concatenate(parts, axis=0)
    kcat = jnp.concatenate([kp_ref[h], ko_ref[h], kn_ref[h], kc_ref[h]], axis=0)
    s = _dot(q8, kcat, _NT) * (HEAD_DIM ** -0.5)
    R, NK = s.shape
    iq = lax.broadcasted_iota(jnp.int32, (R, NK), 0) % B
    col = lax.broadcasted_iota(jnp.int32, (R, NK), 1)
    jk = col % B
    qb = i - nctx
    lat = qb >= 0
    m_prev = (col < B) & lat & (qb >= 1) & (jk >= iq)
    m_own = (col >= B) & (col < 2 * B) & lat
    m_next = (col >= 2 * B) & (col < 3 * B) & lat & (qb <= nb - 2) & (jk <= iq)
    mask = m_prev | m_own | m_next | (col >= 3 * B)
    s = jnp.where(mask, s, -jnp.inf)
    sk = sink_ref[h]
    m = jnp.maximum(jnp.max(s, axis=-1, keepdims=True), sk)
    e = jnp.exp(s - m)
    es = jnp.exp(sk - m)
    inv = 1.0 / (jnp.sum(e, axis=-1, keepdims=True) + es)
    return q8, kcat, e * inv, es * inv, lo


def _unstack_heads(x8, lo):
    B = ATTN_BLOCK
    out = []
    for p in range(GQA_RATIO // 2):
        a = x8[(2 * p) * B:(2 * p + 1) * B]
        b = x8[(2 * p + 1) * B:(2 * p + 2) * B]
        out.append(jnp.where(lo, a, b))
    return out


def _attention(q, k2, v2, sinkcol, dm, name):
    B = ATTN_BLOCK
    nblk, q_spec, prev, own, nxt, ctx, sink = _attn_specs(dm)
    npair = GQA_RATIO // 2

    def body(q_ref, kp_ref, ko_ref, kn_ref, kc_ref, vp_ref, vo_ref, vn_ref, vc_ref, sink_ref, o_ref):
        i = pl.program_id(0)
        for h in range(2):
            _, _, p, _, lo = _attn_scores(i, q_ref, kp_ref, ko_ref, kn_ref, kc_ref, sink_ref, h, dm)
            vcat = jnp.concatenate([vp_ref[h], vo_ref[h], vn_ref[h], vc_ref[h]], axis=0)
            o8 = _dot(p, vcat, _NN)
            for pi, o2 in enumerate(_unstack_heads(o8, lo)):
                c0 = (h * npair + pi) * LANES
                o_ref[:, c0:c0 + LANES] = o2.astype(BF16)

    return pl.pallas_call(
        body, name=name, grid=(nblk,),
        in_specs=[q_spec, prev, own, nxt, ctx, prev, own, nxt, ctx, sink],
        out_specs=q_spec, out_shape=jax.ShapeDtypeStruct((dm.T, dm.ATTN_W), BF16),
        compiler_params=_cparams(("parallel",)),
    )(q, k2, k2, k2, k2, v2, v2, v2, v2, sinkcol)


def _attention_bwd(q, k2, v2, sinkcol, do, dm, name):
    B = ATTN_BLOCK
    nblk, q_spec, prev, own, nxt, ctx, sink = _attn_specs(dm)
    npair = GQA_RATIO // 2

    def body(q_ref, kp_ref, ko_ref, kn_ref, kc_ref, vp_ref, vo_ref, vn_ref, vc_ref, sink_ref, do_ref,
             dq_ref, pk_ref, pv_ref, dkc_ref, dvc_ref, dsk_ref):
        i = pl.program_id(0)

        @pl.when(i == 0)
        def _():
            dkc_ref[...] = jnp.zeros_like(dkc_ref)
            dvc_ref[...] = jnp.zeros_like(dvc_ref)
            dsk_ref[...] = jnp.zeros_like(dsk_ref)

        for h in range(2):
            q8, kcat, p, ps, lo = _attn_scores(i, q_ref, kp_ref, ko_ref, kn_ref, kc_ref, sink_ref, h, dm)
            vcat = jnp.concatenate([vp_ref[h], vo_ref[h], vn_ref[h], vc_ref[h]], axis=0)
            zero = jnp.zeros((B, LANES), F32)
            parts = []
            for pi in range(npair):
                d2 = do_ref[:, (h * npair + pi) * LANES:(h * npair + pi + 1) * LANES]
                parts.append(jnp.where(lo, d2, zero))
                parts.append(jnp.where(lo, zero, d2))
            do8 = jnp.concatenate(parts, axis=0)
            dp = _dot(do8, vcat, _NT)
            delta = jnp.sum(p * dp, axis=-1, keepdims=True)
            ds = p * (dp - delta) * (HEAD_DIM ** -0.5)
            dsr = -ps * delta
            rows = [jnp.broadcast_to(jnp.sum(dsr[r * B:(r + 1) * B], axis=0, keepdims=True), (1, LANES))
                    for r in range(GQA_RATIO)]
            dsk_ref[h] += jnp.concatenate(rows, axis=0)
            dq8 = _dot(ds, kcat, _NN)
            for pi, d2 in enumerate(_unstack_heads(dq8, lo)):
                c0 = (h * npair + pi) * LANES
                dq_ref[:, c0:c0 + LANES] = d2
            dk = _dot(ds, q8, _TN)
            dv = _dot(p, do8, _TN)
            dk = dk + _swap_halves(dk)
            dv = dv + _swap_halves(dv)
            for w in range(3):
                pk_ref[0, h, w] = dk[w * B:(w + 1) * B]
                pv_ref[0, h, w] = dv[w * B:(w + 1) * B]
            dkc_ref[h] += dk[3 * B:]
            dvc_ref[h] += dv[3 * B:]

    part = pl.BlockSpec((1, 2, 3, B, LANES), lambda i: (i, 0, 0, 0, 0))
    acc_c = pl.BlockSpec((2, dm.C, LANES), lambda i: (0, 0, 0))
    acc_s = pl.BlockSpec((2, GQA_RATIO, LANES), lambda i: (0, 0, 0))
    return pl.pallas_call(
        body, name=name, grid=(nblk,),
        in_specs=[q_spec, prev, own, nxt, ctx, prev, own, nxt, ctx, sink, q_spec],
        out_specs=[q_spec, part, part, acc_c, acc_c, acc_s],
        out_shape=[jax.ShapeDtypeStruct((dm.T, dm.ATTN_W), F32),
                   jax.ShapeDtypeStruct((nblk, 2, 3, B, LANES), F32),
                   jax.ShapeDtypeStruct((nblk, 2, 3, B, LANES), F32),
                   jax.ShapeDtypeStruct((2, dm.C, LANES), F32), jax.ShapeDtypeStruct((2, dm.C, LANES), F32),
                   jax.ShapeDtypeStruct((2, GQA_RATIO, LANES), F32)],
        compiler_params=_cparams(("arbitrary",)),
    )(q, k2, k2, k2, k2, v2, v2, v2, v2, sinkcol, do)


def _kv_reduce(part, ctxacc, dm, name):
    B = ATTN_BLOCK
    nblk = dm.T // B
    nctx = dm.C // B

    def body(own_ref, nxt_ref, prv_ref, c_ref, o_ref):
        j = pl.program_id(0)
        acc = own_ref[0, :, 0]
        acc = acc + jnp.where(j < nblk - 1, nxt_ref[0, :, 0], 0.0)
        acc = acc + jnp.where(j > 0, prv_ref[0, :, 0], 0.0)
        acc = acc + jnp.where(j < nctx, c_ref[...], 0.0)
        o_ref[...] = acc

    def spec(w, f):
        return pl.BlockSpec((1, 2, 1, B, LANES), lambda j: (f(j), 0, w, 0, 0))

    return pl.pallas_call(
        body, name=name, grid=(nblk,),
        in_specs=[spec(1, lambda j: j), spec(0, lambda j: jnp.minimum(j + 1, nblk - 1)),
                  spec(2, lambda j: jnp.maximum(j - 1, 0)),
                  pl.BlockSpec((2, B, LANES), lambda j: (0, jnp.minimum(j, nctx - 1), 0))],
        out_specs=pl.BlockSpec((2, B, LANES), lambda j: (0, j, 0)),
        out_shape=jax.ShapeDtypeStruct((2, dm.T, LANES), F32),
        compiler_params=_cparams(("parallel",)),
    )(part, part, part, ctxacc)


def _gmlp_core(z_ref, lg_ref, lb_ref, ws_ref, bs_ref, dm):
    GW = dm.GMLP_W
    gu = z_ref[:, dm.OFF_GU:dm.OFF_GV]
    gv = z_ref[:, dm.OFF_GV:dm.N_IN]
    u = _gelu(gu)
    ge = _gelu(gv)
    mu = jnp.mean(ge, axis=-1, keepdims=True)
    xc = ge - mu
    r = lax.rsqrt(jnp.mean(xc * xc, axis=-1, keepdims=True) + NORM_EPS)
    xh = xc * r
    v = xh * lg_ref[...] + lb_ref[...]
    mixed = []
    for g in range(dm.GG):
        vg = v[:, g * GMLP_GROUP_W:(g + 1) * GMLP_GROUP_W]
        mixed.append(_dot(ws_ref[g], vg, _NN) + bs_ref[g])
    return gu, gv, u, xh, r, v, mixed


def _gmlp(z, ln_g, ln_b, w_s, b_s, dm, name):
    T, GW, CH = dm.T, dm.GMLP_W, GMLP_CHUNK

    def body(z_ref, lg_ref, lb_ref, ws_ref, bs_ref, o_ref):
        _, _, u, _, _, _, mixed = _gmlp_core(z_ref, lg_ref, lb_ref, ws_ref, bs_ref, dm)
        for g in range(dm.GG):
            sl = slice(g * GMLP_GROUP_W, (g + 1) * GMLP_GROUP_W)
            o_ref[:, sl] = (u[:, sl] * mixed[g]).astype(BF16)

    vec = pl.BlockSpec((1, GW), lambda i: (0, 0))
    return pl.pallas_call(
        body, name=name, grid=(T // CH,),
        in_specs=[pl.BlockSpec((CH, dm.N_IN), lambda i: (i, 0)), vec, vec,
                  pl.BlockSpec((dm.GG, CH, CH), lambda i: (0, 0, 0)),
                  pl.BlockSpec((dm.GG, CH, 1), lambda i: (0, 0, 0))],
        out_specs=pl.BlockSpec((CH, GW), lambda i: (i, 0)),
        out_shape=jax.ShapeDtypeStruct((T, GW), BF16),
        compiler_params=_cparams(("parallel",)),
    )(z, ln_g, ln_b, w_s, b_s)


def _gmlp_bwd(z, ln_g, ln_b, w_s, b_s, do, dm, name):
    T, GW, CH = dm.T, dm.GMLP_W, GMLP_CHUNK

    def body(z_ref, lg_ref, lb_ref, ws_ref, bs_ref, do_ref, dz_ref, dlg_ref, dlb_ref, dws_ref, dbs_ref):
        i = pl.program_id(0)

        @pl.when(i == 0)
        def _():
            dlg_ref[...] = jnp.zeros_like(dlg_ref)
            dlb_ref[...] = jnp.zeros_like(dlb_ref)
            dws_ref[...] = jnp.zeros_like(dws_ref)
            dbs_ref[...] = jnp.zeros_like(dbs_ref)

        gu, gv, u, xh, r, v, mixed = _gmlp_core(z_ref, lg_ref, lb_ref, ws_ref, bs_ref, dm)
        do = do_ref[...]
        dv_parts = []
        for g in range(dm.GG):
            sl = slice(g * GMLP_GROUP_W, (g + 1) * GMLP_GROUP_W)
            dog = do[:, sl]
            dz_ref[:, sl] = dog * mixed[g] * _gelu_grad(gu[:, sl])
            dmx = dog * u[:, sl]
            dbs_ref[g] += jnp.sum(dmx, axis=-1, keepdims=True)
            dws_ref[g] += _dot(dmx, v[:, sl], _NT)
            dv_parts.append(_dot(ws_ref[g], dmx, _TN))
        dv = jnp.concatenate(dv_parts, axis=-1) if dm.GG > 1 else dv_parts[0]
        dlg_ref[...] += jnp.sum(dv * xh, axis=0, keepdims=True)
        dlb_ref[...] += jnp.sum(dv, axis=0, keepdims=True)
        dxh = dv * lg_ref[...]
        dge = r * (dxh - jnp.mean(dxh, axis=-1, keepdims=True) - xh * jnp.mean(dxh * xh, axis=-1, keepdims=True))
        dz_ref[:, GW:2 * GW] = dge * _gelu_grad(gv)

    vec = pl.BlockSpec((1, GW), lambda i: (0, 0))
    wsp = pl.BlockSpec((dm.GG, CH, CH), lambda i: (0, 0, 0))
    bsp = pl.BlockSpec((dm.GG, CH, 1), lambda i: (0, 0, 0))
    return pl.pallas_call(
        body, name=name, grid=(T // CH,),
        in_specs=[pl.BlockSpec((CH, dm.N_IN), lambda i: (i, 0)), vec, vec, wsp, bsp,
                  pl.BlockSpec((CH, GW), lambda i: (i, (dm.ATTN_W + dm.SSM_W) // GW))],
        out_specs=[pl.BlockSpec((CH, 2 * GW), lambda i: (i, 0)), vec, vec, wsp, bsp],
        out_shape=[jax.ShapeDtypeStruct((T, 2 * GW), F32), jax.ShapeDtypeStruct((1, GW), F32),
                   jax.ShapeDtypeStruct((1, GW), F32), jax.ShapeDtypeStruct((dm.GG, CH, CH), F32),
                   jax.ShapeDtypeStruct((dm.GG, CH, 1), F32)],
        compiler_params=_cparams(("arbitrary",)),
    )(z, ln_g, ln_b, w_s, b_s, do)


def _disc_fn(lam_re, lam_im, ldt, b_re, b_im):
    dt = jnp.exp(ldt)
    mag = jnp.exp(lam_re * dt)
    a_re = mag * jnp.cos(lam_im * dt)
    a_im = mag * jnp.sin(lam_im * dt)
    den = lam_re * lam_re + lam_im * lam_im
    n_re = a_re - 1.0
    f_re = (n_re * lam_re + a_im * lam_im) / den
    f_im = (a_im * lam_re - n_re * lam_im) / den
    return a_re, a_im, f_re * b_re - f_im * b_im, f_re * b_im + f_im * b_re


def _s5_disc(lam_re, lam_im, ldt, bT_re, bT_im, name):
    nd, H, GP = bT_re.shape

    def body(lr_ref, li_ref, dt_ref, br_ref, bi_ref, ar_ref, ai_ref, bbr_ref, bbi_ref):
        for d in range(nd):
            a_re, a_im, bb_re, bb_im = _disc_fn(lr_ref[d], li_ref[d], dt_ref[d], br_ref[d], bi_ref[d])
            ar_ref[d] = a_re
            ai_ref[d] = a_im
            bbr_ref[d] = bb_re
            bbi_ref[d] = bb_im

    vs = jax.ShapeDtypeStruct((nd, 1, GP), F32)
    ms = jax.ShapeDtypeStruct((nd, H, GP), F32)
    return pl.pallas_call(body, name=name, out_shape=[vs, vs, ms, ms], compiler_params=_cparams())(
        lam_re, lam_im, ldt, bT_re, bT_im)


def _s5_disc_bwd(lam_re, lam_im, ldt, bT_re, bT_im, da_re, da_im, dbb_re, dbb_im, name):
    nd, H, GP = bT_re.shape

    def body(lr_ref, li_ref, dt_ref, br_ref, bi_ref, dar_ref, dai_ref, dbr_ref, dbi_ref,
             olr_ref, oli_ref, odt_ref, obr_ref, obi_ref):
        for d in range(nd):
            _, vjp = jax.vjp(_disc_fn, lr_ref[d], li_ref[d], dt_ref[d], br_ref[d], bi_ref[d])
            g = vjp((dar_ref[d], dai_ref[d], dbr_ref[d], dbi_ref[d]))
            olr_ref[d], oli_ref[d], odt_ref[d], obr_ref[d], obi_ref[d] = g

    vs = jax.ShapeDtypeStruct((nd, 1, GP), F32)
    ms = jax.ShapeDtypeStruct((nd, H, GP), F32)
    return pl.pallas_call(body, name=name, out_shape=[vs, vs, vs, ms, ms], compiler_params=_cparams())(
        lam_re, lam_im, ldt, bT_re, bT_im, da_re, da_im, dbb_re, dbb_im)


def _scan_pass(re_ref, im_ref, nsteps, ar, ai, ir, ii, reverse, store):
    def step(n, carry):
        sr, si = carry
        t = (nsteps - 1 - n) if reverse else n
        off = pl.multiple_of(t * 8, 8)
        nr = ar * sr - ai * si + re_ref[pl.ds(off, 8), :]
        ni = ar * si + ai * sr + im_ref[pl.ds(off, 8), :]
        if store:
            re_ref[pl.ds(off, 8), :] = nr
            im_ref[pl.ds(off, 8), :] = ni
        return nr, ni

    return lax.fori_loop(0, nsteps, step, (ir, ii), unroll=4)


def _cpow(ar, ai, n):
    rr = ri = None
    br, bi = ar, ai
    while n:
        if n & 1:
            rr, ri = (br, bi) if rr is None else (rr * br - ri * bi, rr * bi + ri * br)
        n >>= 1
        if n:
            br, bi = br * br - bi * bi, 2.0 * br * bi
    return rr, ri


def _row_of(x, k):
    rows = lax.broadcasted_iota(jnp.int32, x.shape, 0)
    return jnp.sum(jnp.where(rows == k, x, 0.0), axis=0, keepdims=True)


def _full_scan(re_ref, im_ref, nsteps, ar1, ai1, h0r, h0i, reverse):
    P = ar1.shape[1]
    ar, ai = jnp.broadcast_to(ar1, (8, P)), jnp.broadcast_to(ai1, (8, P))
    z = jnp.zeros((8, P), F32)
    er, ei = _scan_pass(re_ref, im_ref, nsteps, ar, ai, z, z, reverse, False)
    pr, pi = _cpow(ar1, ai1, nsteps)
    rows = lax.broadcasted_iota(jnp.int32, (8, P), 0)
    initr, initi = z, z
    cr, ci = h0r, h0i
    for k in (range(SCAN_SEGMENTS - 1, -1, -1) if reverse else range(SCAN_SEGMENTS)):
        initr = jnp.where(rows == k, cr, initr)
        initi = jnp.where(rows == k, ci, initi)
        ekr, eki = _row_of(er, k), _row_of(ei, k)
        cr, ci = ekr + pr * cr - pi * ci, eki + pr * ci + pi * cr
    _scan_pass(re_ref, im_ref, nsteps, ar, ai, initr, initi, reverse, True)
    return initr, initi, cr, ci


def _rows_loop(n, chunk, fn):
    def step(c, carry):
        fn(pl.multiple_of(c * chunk, chunk))
        return carry
    lax.fori_loop(0, n // chunk, step, 0)


def _s5_specs(dm, PB):
    nsb = dm.GP // PB
    per = (LANES * SSM_STATE // SSM_GROUP) // PB
    ul = pl.BlockSpec((dm.L, LANES), lambda j: (0, j // per))
    uc = pl.BlockSpec((dm.C, LANES), lambda j: (0, j // per))
    av = pl.BlockSpec((2, 1, PB), lambda j: (0, 0, j))
    bd = pl.BlockSpec((2, LANES, PB), lambda j: (0, j // per, j))
    cd = pl.BlockSpec((2, PB, LANES), lambda j: (0, j, j // per))
    dv = pl.BlockSpec((1, LANES), lambda j: (0, j // per))
    return PB, nsb, per, ul, uc, av, bd, cd, dv


def _s5_scan(ul, uc, a_re, a_im, bd_re, bd_im, cd_re, cd_im, dskip, dm, name):
    PB, nsb, per, ul_s, uc_s, av, bd, cd, dv = _s5_specs(dm, STATE_BLOCK_FWD)
    L, C = dm.L, dm.C
    RC = _tile(L, 512, 8)
    RCC = _tile(C, 512, 8)

    def body(ul_ref, uc_ref, ar_ref, ai_ref, br_ref, bi_ref, cr_ref, ci_ref, d_ref, yl_ref, yc_ref,
             lre, lim, cre, cim):
        j = pl.program_id(0)

        @pl.when(j % per == 0)
        def _():
            yl_ref[...] = ul_ref[...] * d_ref[...]
            yc_ref[...] = uc_ref[...] * d_ref[...]

        for d in range(2):
            rev = d == 1
            ar1, ai1 = ar_ref[d], ai_ref[d]

            def proj(u_ref, re, im, chunk, n):
                def f(r0):
                    u = u_ref[pl.ds(r0, chunk), :]
                    re[pl.ds(r0, chunk), :] = _dot(u, br_ref[d], _NN)
                    im[pl.ds(r0, chunk), :] = _dot(u, bi_ref[d], _NN)
                _rows_loop(n, chunk, f)

            def readout(y_ref, re, im, chunk, n):
                def f(r0):
                    y_ref[pl.ds(r0, chunk), :] += (_dot(re[pl.ds(r0, chunk), :], cr_ref[d], _NN)
                                                   - _dot(im[pl.ds(r0, chunk), :], ci_ref[d], _NN))
                _rows_loop(n, chunk, f)

            z1 = jnp.zeros((1, PB), F32)
            proj(uc_ref, cre, cim, RCC, C)
            _, _, fr, fi = _full_scan(cre, cim, C // 8, ar1, ai1, z1, z1, rev)
            readout(yc_ref, cre, cim, RCC, C)
            proj(ul_ref, lre, lim, RC, L)
            _full_scan(lre, lim, L // 8, ar1, ai1, fr, fi, rev)
            readout(yl_ref, lre, lim, RC, L)

    return pl.pallas_call(
        body, name=name, grid=(nsb,),
        in_specs=[ul_s, uc_s, av, av, bd, bd, cd, cd, dv],
        out_specs=[ul_s, uc_s],
        out_shape=[jax.ShapeDtypeStruct((L, dm.SSM_W), F32), jax.ShapeDtypeStruct((C, dm.SSM_W), F32)],
        scratch_shapes=[pltpu.VMEM((L, PB), F32), pltpu.VMEM((L, PB), F32),
                        pltpu.VMEM((C, PB), F32), pltpu.VMEM((C, PB), F32)],
        compiler_params=_cparams(("arbitrary",)),
    )(ul, uc, a_re, a_im, bd_re, bd_im, cd_re, cd_im, dskip)


def _s5_scan_bwd(ul, uc, dyl, dyc, a_re, a_im, bd_re, bd_im, cd_re, cd_im, bdT_re, bdT_im, cdT_re, cdT_im,
                 dskip, dm, name):
    PB, nsb, per, ul_s, uc_s, av, bd, cd, dv = _s5_specs(dm, STATE_BLOCK)
    L, C = dm.L, dm.C
    RC = _tile(L, 512, 8)
    RCC = _tile(C, 512, 8)
    bdT = pl.BlockSpec((2, PB, LANES), lambda j: (0, j, j // per))
    cdT = pl.BlockSpec((2, LANES, PB), lambda j: (0, j // per, j))
    dbd = pl.BlockSpec((2, 1, LANES, PB), lambda j: (0, j, 0, 0))
    dcd = pl.BlockSpec((2, 1, PB, LANES), lambda j: (0, j, 0, 0))

    def body(ul_ref, uc_ref, dyl_ref, dyc_ref, ar_ref, ai_ref, br_ref, bi_ref, cr_ref, ci_ref,
             brT_ref, biT_ref, crT_ref, ciT_ref, d_ref,
             dul_ref, duc_ref, dar_ref, dai_ref, dbr_ref, dbi_ref, dcr_ref, dci_ref, dd_ref,
             lre, lim, cre, cim, gre, gim, hre, him):
        j = pl.program_id(0)

        @pl.when(j % per == 0)
        def _():
            dul_ref[...] = dyl_ref[...] * d_ref[...]
            duc_ref[...] = dyc_ref[...] * d_ref[...]
            dd_ref[...] = (jnp.sum(dyl_ref[...] * ul_ref[...], axis=0, keepdims=True)
                           + jnp.sum(dyc_ref[...] * uc_ref[...], axis=0, keepdims=True))

        for d in range(2):
            rev = d == 1
            ar1, ai1 = ar_ref[d], ai_ref[d]
            z1 = jnp.zeros((1, PB), F32)

            def proj(u_ref, w_re, w_im, sign, re, im, chunk, n):
                def f(r0):
                    u = u_ref[pl.ds(r0, chunk), :]
                    re[pl.ds(r0, chunk), :] = _dot(u, w_re, _NN)
                    im[pl.ds(r0, chunk), :] = sign * _dot(u, w_im, _NN)
                _rows_loop(n, chunk, f)

            proj(uc_ref, br_ref[d], bi_ref[d], 1.0, cre, cim, RCC, C)
            icr, ici, fr, fi = _full_scan(cre, cim, C // 8, ar1, ai1, z1, z1, rev)
            proj(ul_ref, br_ref[d], bi_ref[d], 1.0, lre, lim, RC, L)
            ilr, ili, _, _ = _full_scan(lre, lim, L // 8, ar1, ai1, fr, fi, rev)
            proj(dyl_ref, crT_ref[d], ciT_ref[d], -1.0, gre, gim, RC, L)
            _, _, gfr, gfi = _full_scan(gre, gim, L // 8, ar1, -ai1, z1, z1, not rev)
            proj(dyc_ref, crT_ref[d], ciT_ref[d], -1.0, hre, him, RCC, C)
            _full_scan(hre, him, C // 8, ar1, -ai1, gfr, gfi, not rev)

            dar = jnp.zeros((8, PB), F32)
            dai = jnp.zeros((8, PB), F32)
            dbr = jnp.zeros((LANES, PB), F32)
            dbi = jnp.zeros((LANES, PB), F32)
            dcr = jnp.zeros((PB, LANES), F32)
            dci = jnp.zeros((PB, LANES), F32)
            for (u_ref, dy_ref, du_ref, sre, sim, are, aim, inr, ini, n, chunk) in (
                    (ul_ref, dyl_ref, dul_ref, lre, lim, gre, gim, ilr, ili, L, RC),
                    (uc_ref, dyc_ref, duc_ref, cre, cim, hre, him, icr, ici, C, RCC)):
                nst = n // 8

                def da_step(t, carry, sre=sre, sim=sim, are=are, aim=aim, nst=nst):
                    xr, xi = carry
                    tp = (t + 1) if rev else (t - 1)
                    o = pl.multiple_of(t * 8, 8)
                    op = pl.multiple_of(tp * 8, 8)
                    g_r, g_i = are[pl.ds(o, 8), :], aim[pl.ds(o, 8), :]
                    p_r, p_i = sre[pl.ds(op, 8), :], sim[pl.ds(op, 8), :]
                    return xr + g_r * p_r + g_i * p_i, xi + g_i * p_r - g_r * p_i

                lo_t, hi_t = (0, nst - 1) if rev else (1, nst)
                dar, dai = lax.fori_loop(lo_t, hi_t, da_step, (dar, dai))
                e0 = (nst - 1) * 8 if rev else 0
                g_r, g_i = are[pl.ds(e0, 8), :], aim[pl.ds(e0, 8), :]
                dar = dar + g_r * inr + g_i * ini
                dai = dai + g_i * inr - g_r * ini

                def mats(c, carry, u_ref=u_ref, dy_ref=dy_ref, du_ref=du_ref, sre=sre, sim=sim, are=are, aim=aim,
                         chunk=chunk):
                    xbr, xbi, xcr, xci = carry
                    r0 = pl.multiple_of(c * chunk, chunk)
                    u = u_ref[pl.ds(r0, chunk), :]
                    dy = dy_ref[pl.ds(r0, chunk), :]
                    g_r, g_i = are[pl.ds(r0, chunk), :], aim[pl.ds(r0, chunk), :]
                    du_ref[pl.ds(r0, chunk), :] += _dot(g_r, brT_ref[d], _NN) + _dot(g_i, biT_ref[d], _NN)
                    xbr = xbr + _dot(u, g_r, _TN)
                    xbi = xbi + _dot(u, g_i, _TN)
                    xcr = xcr + _dot(sre[pl.ds(r0, chunk), :], dy, _TN)
                    xci = xci - _dot(sim[pl.ds(r0, chunk), :], dy, _TN)
                    return xbr, xbi, xcr, xci

                dbr, dbi, dcr, dci = lax.fori_loop(0, n // chunk, mats, (dbr, dbi, dcr, dci))
            dar_ref[d] = jnp.sum(dar, axis=0, keepdims=True)
            dai_ref[d] = jnp.sum(dai, axis=0, keepdims=True)
            dbr_ref[d, 0] = dbr
            dbi_ref[d, 0] = dbi
            dcr_ref[d, 0] = dcr
            dci_ref[d, 0] = dci

    W, GP = dm.SSM_W, dm.GP
    return pl.pallas_call(
        body, name=name, grid=(nsb,),
        in_specs=[ul_s, uc_s, ul_s, uc_s, av, av, bd, bd, cd, cd, bdT, bdT, cdT, cdT, dv],
        out_specs=[ul_s, uc_s, av, av, dbd, dbd, dcd, dcd, dv],
        out_shape=[jax.ShapeDtypeStruct((L, W), F32), jax.ShapeDtypeStruct((C, W), F32),
                   jax.ShapeDtypeStruct((2, 1, GP), F32), jax.ShapeDtypeStruct((2, 1, GP), F32),
                   jax.ShapeDtypeStruct((2, nsb, LANES, PB), F32), jax.ShapeDtypeStruct((2, nsb, LANES, PB), F32),
                   jax.ShapeDtypeStruct((2, nsb, PB, LANES), F32), jax.ShapeDtypeStruct((2, nsb, PB, LANES), F32),
                   jax.ShapeDtypeStruct((1, W), F32)],
        scratch_shapes=[pltpu.VMEM((L, PB), F32), pltpu.VMEM((L, PB), F32),
                        pltpu.VMEM((C, PB), F32), pltpu.VMEM((C, PB), F32),
                        pltpu.VMEM((L, PB), F32), pltpu.VMEM((L, PB), F32),
                        pltpu.VMEM((C, PB), F32), pltpu.VMEM((C, PB), F32)],
        compiler_params=_cparams(("arbitrary",)),
    )(ul, uc, dyl, dyc, a_re, a_im, bd_re, bd_im, cd_re, cd_im, bdT_re, bdT_im, cdT_re, cdT_im, dskip)


def _glu(y, w, b, name):
    T, W = y.shape
    tm = _tile(T, 544, 16)

    def body(y_ref, w_ref, b_ref, o_ref):
        g = _gelu(y_ref[...])
        o_ref[...] = (g * _sigmoid(_dot(g, w_ref[...], _NN) + b_ref[...])).astype(BF16)

    return pl.pallas_call(
        body, name=name, grid=(T // tm,),
        in_specs=[pl.BlockSpec((tm, W), lambda i: (i, 0)), pl.BlockSpec((W, W), lambda i: (0, 0)),
                  pl.BlockSpec((1, W), lambda i: (0, 0))],
        out_specs=pl.BlockSpec((tm, W), lambda i: (i, 0)),
        out_shape=jax.ShapeDtypeStruct((T, W), BF16),
        compiler_params=_cparams(("parallel",)),
    )(y, w, b)


def _glu_bwd(y, w, b, do, do_col, name):
    T, W = y.shape
    tm = _tile(T, 544, 16)

    def body(y_ref, w_ref, b_ref, do_ref, dy_ref, dw_ref, db_ref):
        i = pl.program_id(0)

        @pl.when(i == 0)
        def _():
            dw_ref[...] = jnp.zeros_like(dw_ref)
            db_ref[...] = jnp.zeros_like(db_ref)

        y = y_ref[...]
        do = do_ref[...]
        g = _gelu(y)
        s = _sigmoid(_dot(g, w_ref[...], _NN) + b_ref[...])
        dpre = do * g * s * (1.0 - s)
        db_ref[...] += jnp.sum(dpre, axis=0, keepdims=True)
        dw_ref[...] += _dot(g, dpre, _TN)
        dg = do * s + _dot(dpre, w_ref[...], _NT)
        dy_ref[...] = dg * _gelu_grad(y)

    row = pl.BlockSpec((tm, W), lambda i: (i, 0))
    wsp = pl.BlockSpec((W, W), lambda i: (0, 0))
    vec = pl.BlockSpec((1, W), lambda i: (0, 0))
    return pl.pallas_call(
        body, name=name, grid=(T // tm,),
        in_specs=[row, wsp, vec, pl.BlockSpec((tm, W), lambda i: (i, do_col))], out_specs=[row, wsp, vec],
        out_shape=[jax.ShapeDtypeStruct((T, W), F32), jax.ShapeDtypeStruct((W, W), F32),
                   jax.ShapeDtypeStruct((1, W), F32)],
        compiler_params=_cparams(("arbitrary",)),
    )(y, w, b, do)


def _loss_head(X, g, target, C, name):
    T, D = X.shape
    tm = _tile(math.gcd(C, T - C), 512, 16)
    nc = C // tm

    def body(x_ref, g_ref, t_ref, loss_ref, dx_ref, dg_ref):
        i = pl.program_id(0)

        @pl.when(i == 0)
        def _():
            loss_ref[...] = jnp.zeros_like(loss_ref)
            dg_ref[...] = jnp.zeros_like(dg_ref)

        @pl.when(i < nc)
        def _():
            dx_ref[...] = jnp.zeros_like(dx_ref)

        @pl.when(i >= nc)
        def _():
            x = x_ref[...]
            gv = g_ref[...]
            r = lax.rsqrt(jnp.mean(x * x, axis=-1, keepdims=True) + NORM_EPS)
            xh = x * r
            err = xh * gv - t_ref[...]
            loss_ref[...] += 0.5 * jnp.sum(jnp.mean(err * err, axis=-1, keepdims=True), axis=0, keepdims=True)
            dy = err * (1.0 / D)
            dg_ref[...] += jnp.sum(dy * xh, axis=0, keepdims=True)
            dxh = dy * gv
            dx_ref[...] = r * (dxh - xh * jnp.mean(dxh * xh, axis=-1, keepdims=True))

    row = pl.BlockSpec((tm, D), lambda i: (i, 0))
    return pl.pallas_call(
        body, name=name, grid=(T // tm,),
        in_specs=[row, pl.BlockSpec((1, D), lambda i: (0, 0)),
                  pl.BlockSpec((tm, D), lambda i: (jnp.maximum(i - nc, 0), 0))],
        out_specs=[pl.BlockSpec((1, 1), lambda i: (0, 0)), row, pl.BlockSpec((1, D), lambda i: (0, 0))],
        out_shape=[jax.ShapeDtypeStruct((1, 1), F32), jax.ShapeDtypeStruct((T, D), F32),
                   jax.ShapeDtypeStruct((1, D), F32)],
        compiler_params=_cparams(("arbitrary",)),
    )(X, g, target)


def _ada_fwd(cond, w, b, name):
    nl, D, Ns = w.shape
    tn = _tile(Ns, 1024, LANES)

    def body(c_ref, w_ref, b_ref, o_ref):
        c = c_ref[...]
        o_ref[0] = _dot(c * _sigmoid(c), w_ref[0], _NN) + b_ref[0]

    return pl.pallas_call(
        body, name=name, grid=(nl, Ns // tn),
        in_specs=[pl.BlockSpec((16, D), lambda l, j: (0, 0)), pl.BlockSpec((1, D, tn), lambda l, j: (l, 0, j)),
                  pl.BlockSpec((1, 1, tn), lambda l, j: (l, 0, j))],
        out_specs=pl.BlockSpec((1, 16, tn), lambda l, j: (l, 0, j)),
        out_shape=jax.ShapeDtypeStruct((nl, 16, Ns), F32),
        compiler_params=_cparams(("parallel", "parallel")),
    )(cond, w, b)


def _ada_bwd(cond, w, dmod, name):
    nl, D, Ns = w.shape
    tn = _tile(Ns, 1024, LANES)

    def body(c_ref, w_ref, dm_ref, dw_ref, da_ref):
        j = pl.program_id(1)

        @pl.when(j == 0)
        def _():
            da_ref[...] = jnp.zeros_like(da_ref)

        c = c_ref[...]
        dw_ref[0] = _dot(c * _sigmoid(c), dm_ref[0], _TN)
        da_ref[0] += _dot(dm_ref[0], w_ref[0], _NT)

    return pl.pallas_call(
        body, name=name, grid=(nl, Ns // tn),
        in_specs=[pl.BlockSpec((16, D), lambda l, j: (0, 0)), pl.BlockSpec((1, D, tn), lambda l, j: (l, 0, j)),
                  pl.BlockSpec((1, 16, tn), lambda l, j: (l, 0, j))],
        out_specs=[pl.BlockSpec((1, D, tn), lambda l, j: (l, 0, j)), pl.BlockSpec((1, 16, D), lambda l, j: (l, 0, 0))],
        out_shape=[jax.ShapeDtypeStruct((nl, D, Ns), F32), jax.ShapeDtypeStruct((nl, 16, D), F32)],
        compiler_params=_cparams(("parallel", "arbitrary")),
    )(cond, w, dmod)


def _cctx_grad(c_ctx, dact4, name):
    nch, nl, _, D = dact4.shape

    def body(c_ref, da_ref, o_ref):
        acc = jnp.zeros((1, D), F32)
        for ch in range(nch):
            for l in range(nl):
                acc = acc + da_ref[ch, l, 8:9, :]
        c = c_ref[...]
        s = _sigmoid(c)
        o_ref[...] = acc * (s + c * s * (1.0 - s))

    return pl.pallas_call(body, name=name, out_shape=jax.ShapeDtypeStruct((1, D), F32),
                          compiler_params=_cparams())(c_ctx, dact4)


def _adamw(w, g, m, v, name):
    R, W = w.shape
    tr = _tile(R, max(16, (1 << 19) // W // 16 * 16), 16 if g.dtype == BF16 else 8)
    c1 = 1.0 - ADAM_B1 ** ADAM_STEP
    c2 = 1.0 - ADAM_B2 ** ADAM_STEP

    def body(w_ref, g_ref, m_ref, v_ref, d_ref, nm_ref, nv_ref, g32_ref):
        g = g_ref[...].astype(F32)
        g32_ref[...] = g
        m = ADAM_B1 * m_ref[...] + (1.0 - ADAM_B1) * g
        v = ADAM_B2 * v_ref[...] + (1.0 - ADAM_B2) * (g * g)
        nm_ref[...] = m
        nv_ref[...] = v
        d_ref[...] = -ADAM_LR * ((m / c1) / (jnp.sqrt(v / c2) + ADAM_EPS) + ADAM_WD * w_ref[...])

    blk = pl.BlockSpec((tr, W), lambda i: (i, 0))
    s = jax.ShapeDtypeStruct((R, W), F32)
    return pl.pallas_call(
        body, name=name, grid=(R // tr,), in_specs=[blk] * 4, out_specs=[blk] * 4, out_shape=[s, s, s, s],
        compiler_params=_cparams(("parallel",)),
    )(w, g, m, v)


def _adamw_nd(w, g, m, v, name):
    shp = w.shape
    two = (math.prod(shp[:-1]), shp[-1])
    outs = _adamw(w.reshape(two), g.reshape(two), m.reshape(two), v.reshape(two), name)
    return [o.reshape(shp) for o in outs]


def _interleave(a, n):
    return a.reshape(SCAN_SEGMENTS, n // SCAN_SEGMENTS, -1).transpose(1, 0, 2).reshape(n, -1)


def _deinterleave(a, n):
    return a.reshape(n // SCAN_SEGMENTS, SCAN_SEGMENTS, -1).transpose(1, 0, 2).reshape(n, -1)


def _blockdiag_in(bbT, dm):
    eye = jnp.eye(dm.SG, dtype=F32)
    b4 = bbT.reshape(2, SSM_GROUP, dm.SG, SSM_STATE)
    return jnp.einsum("dhgp,gk->dghkp", b4, eye).reshape(2, dm.SSM_W, dm.GP)


def _blockdiag_in_T(dbd, dm):
    gpb = STATE_BLOCK // SSM_STATE
    per = (LANES // SSM_GROUP) // gpb
    d8 = dbd.reshape(2, dm.SSM_W // LANES, per, per, gpb, SSM_GROUP, gpb, SSM_STATE)
    x = jnp.einsum("duaabhbp->duabhp", d8).reshape(2, dm.SG, SSM_GROUP, SSM_STATE)
    return x.transpose(0, 2, 1, 3).reshape(2, SSM_GROUP, dm.GP)


def _blockdiag_out(c, dm):
    eye = jnp.eye(dm.SG, dtype=F32)
    return jnp.einsum("dghp,gk->dgpkh", c, eye).reshape(2, dm.GP, dm.SSM_W)


def _blockdiag_out_T(dcd, dm):
    gpb = STATE_BLOCK // SSM_STATE
    per = (LANES // SSM_GROUP) // gpb
    d8 = dcd.reshape(2, dm.SSM_W // LANES, per, gpb, SSM_STATE, per, gpb, SSM_GROUP)
    return jnp.einsum("duabpabh->duabhp", d8).reshape(2, dm.SG, SSM_GROUP, SSM_STATE)


_BIG = ("w_in", "w_out", "ssm_w_glu", "ffn_w_up", "ffn_w_down")
_SHARD_AXIS = {"w_in": 2, "w_out": 1, "ssm_w_glu": 1, "ffn_w_up": 2, "ffn_w_down": 1}
_SMALL = ("g_mix", "g_ffn", "attn_sink", "ssm_lambda_re", "ssm_lambda_im", "ssm_log_dt", "ssm_b_re", "ssm_b_im",
          "ssm_c_re", "ssm_c_im", "ssm_d", "ssm_b_glu", "gmlp_ln_g", "gmlp_ln_b", "gmlp_w_s", "gmlp_b_s",
          "ffn_conv_b", "g_final", "ffn_conv_w")
_WEIGHTS = ("c_ctx", "w_ada", "b_ada", "g_mix", "g_ffn", "w_in", "w_out", "attn_sink", "ssm_lambda_re",
            "ssm_lambda_im", "ssm_log_dt", "ssm_b_re", "ssm_b_im", "ssm_c_re", "ssm_c_im", "ssm_d", "ssm_w_glu",
            "ssm_b_glu", "gmlp_ln_g", "gmlp_ln_b", "gmlp_w_s", "gmlp_b_s", "ffn_w_up", "ffn_conv_w", "ffn_conv_b",
            "ffn_w_down", "g_final")


def _step(P, M, V, x, c, ctx, loss_target):
    dm = _Dims()
    D, T, C, L = dm.D, dm.T, dm.C, dm.L
    mx, my, mc = lax.axis_index("x"), lax.axis_index("y"), lax.axis_index("c")
    chip = 2 * mx + my
    dev = 2 * chip + mc

    conv_sh = P["ffn_conv_w"].shape
    small_f = _pack([c, P["ffn_conv_w"]], F32, 8)
    small_all = _allgather8(small_f, "ag_small_fwd")
    c_all, conv_all = [], []
    for d in range(N_DEV):
        cd_, cw_ = _unpack(small_all[d], [(1, D), conv_sh])
        c_all.append(cd_)
        conv_all.append(cw_)
    conv_w = jnp.concatenate([conv_all[2 * j] for j in range(N_CHIPS)], axis=2)
    cond = jnp.concatenate(c_all + [P["c_ctx"].reshape(1, D), jnp.zeros((16 - N_DEV - 1, D), F32)], axis=0)

    n_sh = P["w_ada"].shape[2]
    b_sh = lax.dynamic_slice_in_dim(P["b_ada"], chip * n_sh, n_sh, axis=1)[:, None, :]
    mods_sh = _ada_fwd(cond, P["w_ada"], b_sh, "ada_fwd")
    mods4 = _comm(mods_sh, group="chips", by_peer=False, name="ag_mods")
    mods = jnp.concatenate([mods4[j] for j in range(N_CHIPS)], axis=-1)
    mod_lat = lax.dynamic_index_in_dim(mods, dev, axis=1, keepdims=False)
    mod_ctx = mods[:, N_DEV]
    modp = jnp.stack([mod_ctx, mod_lat], axis=1).reshape(DEPTH, 2, N_MOD, D)

    wb = {n: P[n].astype(BF16) for n in _BIG}

    def layer_weights(g):
        return dict(w_in=jnp.concatenate([g["w_in"][j] for j in range(N_CHIPS)], axis=1),
                    w_glu=g["ssm_w_glu"].reshape(dm.SSM_W, dm.SSM_W),
                    w_out=_Mat(g["w_out"], axis=0), w_dn=_Mat(g["ffn_w_down"], axis=0),
                    w_up=_Mat(g["ffn_w_up"], axis=1))

    g0 = _comm_multi([wb[n][0] for n in _BIG], group="chips", by_peer=False, name="ag_w_chips")
    Wl = [layer_weights(dict(zip(_BIG, g0)))]
    cos, sin = _rope_tables(dm)

    X = jnp.concatenate([ctx.reshape(C, D), x.reshape(L, D)], axis=0)
    saved = []
    for l in range(DEPTH):
        mp = modp[l]
        sh_a, sc_a, gt_a, sh_f, sc_f, gt_f = [mp[:, k] for k in range(N_MOD)]
        w_in, w_out, w_glu, w_up, w_dn = [Wl[l][k] for k in ("w_in", "w_out", "w_glu", "w_up", "w_dn")]
        g_mix, g_ffn = P["g_mix"][l][None], P["g_ffn"][l][None]

        h = _normmod(X, g_mix, sh_a, sc_a, C, "normmod_a")
        nxt = {}

        def riding(*names):
            return [wb[n][l + 1] for n in names] if l + 1 < DEPTH else []

        def landed(names, gathered):
            nxt.update(zip(names, gathered) if l + 1 < DEPTH else ())

        z = _matmul(h, w_in, mode="nn", name="mm_in", carry=riding("w_in"))
        if l + 1 < DEPTH:
            z, got = z
            landed(("w_in",), got)
        q, k2, v2 = _attn_prep(z, cos, sin, dm, "attn_prep")
        sinkcol = jnp.repeat(P["attn_sink"][l].reshape(2, GQA_RATIO), ATTN_BLOCK, axis=1)[..., None]
        o_attn = _attention(q, k2, v2, sinkcol, dm, "attention")
        u = z[:, dm.OFF_S:dm.OFF_GU]
        ul, uc = _interleave(u[C:], L), _interleave(u[:C], C)
        lam_re = P["ssm_lambda_re"][l].reshape(2, 1, dm.GP)
        lam_im = P["ssm_lambda_im"][l].reshape(2, 1, dm.GP)
        ldt = jnp.repeat(P["ssm_log_dt"][l], SSM_STATE, axis=-1).reshape(2, 1, dm.GP)
        bT_re = P["ssm_b_re"][l].reshape(2, dm.GP, SSM_GROUP).transpose(0, 2, 1)
        bT_im = P["ssm_b_im"][l].reshape(2, dm.GP, SSM_GROUP).transpose(0, 2, 1)
        a_re, a_im, bbT_re, bbT_im = _s5_disc(lam_re, lam_im, ldt, bT_re, bT_im, "s5_disc")
        bd_re, bd_im = _blockdiag_in(bbT_re, dm).astype(BF16), _blockdiag_in(bbT_im, dm).astype(BF16)
        cd_re = _blockdiag_out(P["ssm_c_re"][l], dm).astype(BF16)
        cd_im = _blockdiag_out(P["ssm_c_im"][l], dm).astype(BF16)
        dskip = P["ssm_d"][l][None]
        yl, yc = _s5_scan(ul, uc, a_re, a_im, bd_re, bd_im, cd_re, cd_im, dskip, dm, "s5_scan")
        y = jnp.concatenate([_deinterleave(yc, C), _deinterleave(yl, L)], axis=0)
        b_glu = P["ssm_b_glu"][l][None]
        o_ssm = _glu(y, w_glu, b_glu, "glu")
        ln_g, ln_b = P["gmlp_ln_g"][l][None], P["gmlp_ln_b"][l][None]
        w_s = P["gmlp_w_s"][l].astype(BF16)
        b_s = P["gmlp_b_s"][l][..., None]
        o_gmlp = _gmlp(z, ln_g, ln_b, w_s, b_s, dm, "gmlp")
        o_cat = jnp.concatenate([o_attn, o_ssm, o_gmlp], axis=-1)
        mix, X1, *got = _matmul_residual(o_cat, w_out, X, gt_a, C, "mm_out", carry=riding("w_out", "ssm_w_glu"))
        landed(("w_out", "ssm_w_glu"), got[0] if got else ())
        h2 = _normmod(X1, g_ffn, sh_f, sc_f, C, "normmod_f")
        up = _matmul(h2, w_up, mode="nn", name="mm_up", out_dtype=BF16, out_split=(1, 2),
                     carry=riding("ffn_w_up"))
        if l + 1 < DEPTH:
            up, got = up
            landed(("ffn_w_up",), got)
        cw, cb = conv_w[l], P["ffn_conv_b"][l][None]
        act = _convact(up, cw, cb, C, "convact")
        ffn, X2, *got = _matmul_residual(act, w_dn, X1, gt_f, C, "mm_down", carry=riding("ffn_w_down"))
        landed(("ffn_w_down",), got[0] if got else ())
        if l + 1 < DEPTH:
            Wl.append(layer_weights(nxt))
        saved.append(dict(X=X, h=h, z=z, q=q, k2=k2, v2=v2, sinkcol=sinkcol, ul=ul, uc=uc, a_re=a_re, a_im=a_im,
                          bd_re=bd_re, bd_im=bd_im, cd_re=cd_re, cd_im=cd_im, y=y, o_cat=o_cat, mix=mix, X1=X1, h2=h2,
                          up=up, act=act, ffn=ffn, lam_re=lam_re, lam_im=lam_im, ldt=ldt, bT_re=bT_re,
                          bT_im=bT_im))
        X = X2

    loss_l, dX, dg_final = _loss_head(X, P["g_final"][None], loss_target.reshape(L, D), C, "loss_head")
    loss = lax.psum(loss_l[0, 0], ("x", "y", "c"))

    G = {n: [None] * DEPTH for n in _WEIGHTS if n not in ("c_ctx", "w_ada", "b_ada", "g_final")}
    dmod = [None] * DEPTH
    for l in reversed(range(DEPTH)):
        S = saved[l]
        mp = modp[l]
        sh_a, sc_a, gt_a, sh_f, sc_f, gt_f = [mp[:, k] for k in range(N_MOD)]
        w_in, w_out, w_glu, w_up, w_dn = [Wl[l][k] for k in ("w_in", "w_out", "w_glu", "w_up", "w_dn")]
        g_mix, g_ffn = P["g_mix"][l][None], P["g_ffn"][l][None]
        cw, cb = conv_w[l], P["ffn_conv_b"][l][None]

        dffn, dgt_f = _gate_bwd(dX, S["ffn"], gt_f, C, "gate_bwd")
        dact = _matmul(dffn, w_dn, mode="nt", name="mm_down_dx", out_dtype=BF16)
        G["ffn_w_down"][l] = _matmul(S["act"], dffn, mode="tn", name="mm_down_dw", out_dtype=BF16,
                                     out_split=(0, N_CHIPS), tm_cap=1408, tk_cap=2176)
        dup, dcw, dcb = _convact_bwd(S["up"], cw, cb, dact, C, "convact_bwd")
        G["ffn_conv_w"][l], G["ffn_conv_b"][l] = dcw, dcb[0]
        dh2 = _matmul(_Mat(dup, axis=1), w_up, mode="nt", name="mm_up_dx", tk_cap=2816)
        G["ffn_w_up"][l] = _matmul(S["h2"], _Mat(dup, axis=1), mode="tn", name="mm_up_dw", out_dtype=BF16,
                                   out_split=(1, N_CHIPS), tk_cap=2176)
        dX1, dg_f, dsh_f, dsc_f = _normmod_bwd(S["X1"], g_ffn, sc_f, dh2, dX, C, "normmod_bwd")
        G["g_ffn"][l] = dg_f[0]
        dmix, dgt_a = _gate_bwd(dX1, S["mix"], gt_a, C, "gate_bwd")
        do = _matmul(dmix, w_out, mode="nt", name="mm_out_dx")
        G["w_out"][l] = _matmul(S["o_cat"], dmix, mode="tn", name="mm_out_dw", out_dtype=BF16,
                                out_split=(0, N_CHIPS), tk_cap=2176)
        do_attn = do_ssm = do_gmlp = do
        ln_g, ln_b = P["gmlp_ln_g"][l][None], P["gmlp_ln_b"][l][None]
        w_s = P["gmlp_w_s"][l].astype(BF16)
        b_s = P["gmlp_b_s"][l][..., None]
        dz_g, dlg, dlb, dws, dbs = _gmlp_bwd(S["z"], ln_g, ln_b, w_s, b_s, do_gmlp, dm, "gmlp_bwd")
        G["gmlp_ln_g"][l], G["gmlp_ln_b"][l], G["gmlp_w_s"][l], G["gmlp_b_s"][l] = dlg[0], dlb[0], dws, dbs[..., 0]
        b_glu = P["ssm_b_glu"][l][None]
        dy, dwglu, dbglu = _glu_bwd(S["y"], w_glu, b_glu, do_ssm, dm.ATTN_W // dm.SSM_W, "glu_bwd")
        G["ssm_w_glu"][l], G["ssm_b_glu"][l] = dwglu.astype(BF16).reshape(N_CHIPS, -1, dm.SSM_W), dbglu[0]
        dyl, dyc = _interleave(dy[C:], L), _interleave(dy[:C], C)
        tr = lambda a: jnp.swapaxes(a, 1, 2)
        dskip = P["ssm_d"][l][None]
        (dul, duc, da_re, da_im, dbd_re, dbd_im, dcd_re, dcd_im, dd) = _s5_scan_bwd(
            S["ul"], S["uc"], dyl, dyc, S["a_re"], S["a_im"], S["bd_re"], S["bd_im"], S["cd_re"], S["cd_im"],
            tr(S["bd_re"]), tr(S["bd_im"]), tr(S["cd_re"]), tr(S["cd_im"]), dskip, dm, "s5_scan_bwd")
        dz_s = jnp.concatenate([_deinterleave(duc, C), _deinterleave(dul, L)], axis=0)
        dlr, dli, dldt, dbTr, dbTi = _s5_disc_bwd(S["lam_re"], S["lam_im"], S["ldt"], S["bT_re"], S["bT_im"],
                                                  da_re, da_im, _blockdiag_in_T(dbd_re, dm),
                                                  _blockdiag_in_T(dbd_im, dm), "s5_disc_bwd")
        G["ssm_lambda_re"][l] = dlr.reshape(2, dm.SG, SSM_STATE)
        G["ssm_lambda_im"][l] = dli.reshape(2, dm.SG, SSM_STATE)
        G["ssm_log_dt"][l] = jnp.sum(dldt.reshape(2, dm.SG, SSM_STATE), axis=-1)
        G["ssm_b_re"][l] = dbTr.transpose(0, 2, 1).reshape(2, dm.SG, SSM_STATE, SSM_GROUP)
        G["ssm_b_im"][l] = dbTi.transpose(0, 2, 1).reshape(2, dm.SG, SSM_STATE, SSM_GROUP)
        G["ssm_c_re"][l] = _blockdiag_out_T(dcd_re, dm)
        G["ssm_c_im"][l] = _blockdiag_out_T(dcd_im, dm)
        G["ssm_d"][l] = dd[0]
        dq, pk, pv, dkc, dvc, dsk = _attention_bwd(S["q"], S["k2"], S["v2"], S["sinkcol"], do_attn, dm, "attention_bwd")
        G["attn_sink"][l] = dsk[:, :, 0].reshape(dm.NH)
        dk2 = _kv_reduce(pk, dkc, dm, "kv_reduce")
        dv2 = _kv_reduce(pv, dvc, dm, "kv_reduce")
        dz_a = _attn_prep_bwd(dq, dk2, dv2, cos, sin, dm, "attn_prep_bwd")
        dz = jnp.concatenate([dz_a, dz_s, dz_g], axis=-1).astype(BF16)
        dh = _matmul(dz, w_in, mode="nt", name="mm_in_dx")
        dw_in = _matmul(S["h"], dz, mode="tn", name="mm_in_dw", out_dtype=BF16, tk_cap=2176)
        n_in = dm.N_IN // N_CHIPS
        G["w_in"][l] = jnp.stack([dw_in[:, j * n_in:(j + 1) * n_in] for j in range(N_CHIPS)])
        dX, dg_a, dsh_a, dsc_a = _normmod_bwd(S["X"], g_mix, sc_a, dh, dX1, C, "normmod_bwd")
        G["g_mix"][l] = dg_a[0]
        dmod[l] = jnp.stack([dsh_a, dsc_a, dgt_a, dsh_f, dsc_f, dgt_f], axis=1)

    grad_x = dX[C:].reshape(1, L, D)
    Gbig = {n: G.pop(n) for n in _BIG}
    G = {n: jnp.stack(v) for n, v in G.items()}
    G["g_final"] = dg_final[0]
    dmod = jnp.stack(dmod)

    small_shapes = [G[n].shape for n in _SMALL]
    gs_pack = _pack([G[n] for n in _SMALL], F32, 16 * N_DEV)
    r8 = gs_pack.shape[0] // N_DEV
    gs_parts = _comm(gs_pack.reshape(N_DEV, r8, PACK_W), group="all", by_peer=True, name="rs_small")
    gs_sum = _allgather8(_sum_slots(gs_parts, F32, "sum_small"), "ag_small_red").reshape(-1, PACK_W)
    for n, a in zip(_SMALL, _unpack(gs_sum, small_shapes)):
        G[n] = a
    allp = _allgather8(_pack([dmod], F32, 16), "ag_dmod")
    dmod_all = allp.reshape(N_DEV, -1)[:, :DEPTH * 2 * N_MOD * D]
    dmod_all = dmod_all.reshape(N_DEV, DEPTH, 2, N_MOD * D)
    dmod_ctx = _sum_slots(dmod_all[:, :, 0].reshape(N_DEV, DEPTH, N_MOD * D), F32, "sum_dmod_ctx")
    dmod16 = jnp.concatenate([jnp.swapaxes(dmod_all[:, :, 1], 0, 1), dmod_ctx[:, None],
                              jnp.zeros((DEPTH, 16 - N_DEV - 1, N_MOD * D), F32)], axis=1)
    G["b_ada"] = _sum_slots(jnp.swapaxes(dmod16, 0, 1)[:N_DEV + 1], F32, "sum_b_ada")
    dmod16_sh = lax.dynamic_slice_in_dim(dmod16, chip * n_sh, n_sh, axis=2)
    G["w_ada"], dact = _ada_bwd(cond, P["w_ada"], dmod16_sh, "ada_bwd")
    dact4 = _comm(dact, group="chips", by_peer=False, name="ag_dact")
    G["c_ctx"] = _cctx_grad(P["c_ctx"].reshape(1, D), dact4, "cctx_grad")[0]
    conv_cols = conv_sh[2]
    G["ffn_conv_w"] = lax.dynamic_slice_in_dim(G["ffn_conv_w"], chip * conv_cols, conv_cols, axis=2)

    LH = DEPTH // 2
    xs = [jnp.stack([jnp.stack([jnp.concatenate([Gbig[n][hh * LH + li][j] for li in range(LH)], axis=0)
                                for j in range(N_CHIPS)]) for hh in range(2)]) for n in _BIG]
    parts = _comm_multi(xs, group="all", by_peer=True, name="rs_all")
    for n, part in zip(_BIG, parts):
        red = _sum_slots(part, BF16, "sum_all")
        both = _comm(red, group="sib", by_peer=False, name="rs_share")
        G[n] = both.reshape(P[n].shape)

    delta, new_m, new_v = {}, {}, {}
    for n in _BIG + ("w_ada",):
        delta[n], new_m[n], new_v[n], G[n] = _adamw_nd(P[n], G[n], M[n], V[n], "adamw_" + n)
    rest = [n for n in _WEIGHTS if n not in _BIG + ("w_ada",)]
    shapes = [P[n].shape for n in rest]
    packed = [_pack([d[n] for n in rest], F32, 8) for d in (P, G, M, V)]
    outs = _adamw(*packed, "adamw_small")[:3]
    for dst, buf in zip((delta, new_m, new_v), outs):
        for n, a in zip(rest, _unpack(buf, shapes)):
            dst[n] = a
    return loss, grad_x, G, delta, new_m, new_v


def kernel(x, c, ctx, c_ctx, w_ada, b_ada, g_mix, g_ffn, w_in, w_out, attn_sink, ssm_lambda_re, ssm_lambda_im, ssm_log_dt, ssm_b_re, ssm_b_im, ssm_c_re, ssm_c_im, ssm_d, ssm_w_glu, ssm_b_glu, gmlp_ln_g, gmlp_ln_b, gmlp_w_s, gmlp_b_s, ffn_w_up, ffn_conv_w, ffn_conv_b, ffn_w_down, g_final, loss_target, m_c_ctx, m_w_ada, m_b_ada, m_g_mix, m_g_ffn, m_w_in, m_w_out, m_attn_sink, m_ssm_lambda_re, m_ssm_lambda_im, m_ssm_log_dt, m_ssm_b_re, m_ssm_b_im, m_ssm_c_re, m_ssm_c_im, m_ssm_d, m_ssm_w_glu, m_ssm_b_glu, m_gmlp_ln_g, m_gmlp_ln_b, m_gmlp_w_s, m_gmlp_b_s, m_ffn_w_up, m_ffn_conv_w, m_ffn_conv_b, m_ffn_w_down, m_g_final, v_c_ctx, v_w_ada, v_b_ada, v_g_mix, v_g_ffn, v_w_in, v_w_out, v_attn_sink, v_ssm_lambda_re, v_ssm_lambda_im, v_ssm_log_dt, v_ssm_b_re, v_ssm_b_im, v_ssm_c_re, v_ssm_c_im, v_ssm_d, v_ssm_w_glu, v_ssm_b_glu, v_gmlp_ln_g, v_gmlp_ln_b, v_gmlp_w_s, v_gmlp_b_s, v_ffn_w_up, v_ffn_conv_w, v_ffn_conv_b, v_ffn_w_down, v_g_final):
    env = locals()
    P = {n: env[n] for n in _WEIGHTS}
    M = {n: env["m_" + n] for n in _WEIGHTS}
    V = {n: env["v_" + n] for n in _WEIGHTS}
    loss, grad_x, G, delta, new_m, new_v = _step(P, M, V, x, c, ctx, loss_target)
    return (loss, grad_x, *[G[n] for n in _WEIGHTS], *[delta[n] for n in _WEIGHTS],
            *[new_m[n] for n in _WEIGHTS], *[new_v[n] for n in _WEIGHTS])
```

```python
import functools
import math

import jax
import jax.numpy as jnp
from jax import lax
from jax.experimental import pallas as pl
from jax.experimental.pallas import tpu as pltpu

F32 = jnp.float32
BF16 = jnp.bfloat16

D_MODEL = 2048
SEQ = 4096
DEPTH = 4
CTX_LEN = 256
GRID_W = 64
HEAD_DIM = 64
GQA_RATIO = 8
WINDOW = 128
ATTN_BLOCK = 128
ROPE_BASE = 10000.0
SSM_GROUP = 16
SSM_STATE = 64
GMLP_CHUNK = 128
GMLP_GROUP_W = 128
N_MOD = 6
NORM_EPS = 1e-6
ADAM_LR = 0.001
ADAM_B1 = 0.9
ADAM_B2 = 0.999
ADAM_EPS = 1e-08
ADAM_WD = 0.01
ADAM_STEP = 10

N_CHIPS = 4
N_DEV = 8
SCAN_SEGMENTS = 8
STATE_BLOCK = 256
STATE_BLOCK_FWD = 512
LANES = 128
PACK_W = 1024
COMM_CHUNKS = (16, 8, 4, 2)
SUM_BLOCK_BYTES = 8 << 20
VMEM_LIMIT_MB = 56
_GELU_K = math.sqrt(2.0 / math.pi)
_GELU_C = 0.044715


class _Dims:
    def __init__(self):
        self.D = D_MODEL
        self.L = SEQ
        self.C = CTX_LEN
        self.T = SEQ + CTX_LEN
        self.ATTN_W = D_MODEL // 2
        self.SSM_W = D_MODEL // 4
        self.GMLP_W = D_MODEL - self.ATTN_W - self.SSM_W
        self.NH = self.ATTN_W // HEAD_DIM
        self.NKV = self.NH // GQA_RATIO
        self.KV_W = self.NKV * HEAD_DIM
        self.SG = self.SSM_W // SSM_GROUP
        self.GP = self.SG * SSM_STATE
        self.GG = self.GMLP_W // GMLP_GROUP_W
        self.DFF = ((8 * D_MODEL // 3 + 255) // 256) * 256
        self.OFF_K = self.ATTN_W
        self.OFF_V = self.OFF_K + self.KV_W
        self.OFF_S = self.OFF_V + self.KV_W
        self.OFF_GU = self.OFF_S + self.SSM_W
        self.OFF_GV = self.OFF_GU + self.GMLP_W
        self.N_IN = self.OFF_GV + self.GMLP_W
        assert self.NKV == 2 and self.KV_W == LANES
        assert self.C % (SCAN_SEGMENTS * 8) == 0 and self.L % (SCAN_SEGMENTS * 8) == 0
        assert self.C % ATTN_BLOCK == 0 and self.L % ATTN_BLOCK == 0


def _tile(n, cap, mult):
    best = None
    for t in range(mult, min(n, cap) + 1, mult):
        if n % t == 0:
            best = t
    return best if best is not None else n


def _cparams(sem=None):
    kw = dict(vmem_limit_bytes=VMEM_LIMIT_MB << 20)
    if sem is not None:
        kw["dimension_semantics"] = sem
    return pltpu.CompilerParams(**kw)


def _gelu(x):
    return 0.5 * x * (1.0 + jnp.tanh(_GELU_K * (x + _GELU_C * x * x * x)))


def _gelu_grad(x):
    t = jnp.tanh(_GELU_K * (x + _GELU_C * x * x * x))
    return 0.5 * (1.0 + t) + 0.5 * x * (1.0 - t * t) * _GELU_K * (1.0 + 3.0 * _GELU_C * x * x)


def _sigmoid(x):
    return 1.0 / (1.0 + jnp.exp(-x))


def _dot(a, b, dims):
    return lax.dot_general(a.astype(BF16), b.astype(BF16), (dims, ((), ())), preferred_element_type=F32)


_NN = ((1,), (0,))
_NT = ((1,), (1,))
_TN = ((0,), (0,))


def _comm(x, *, group, by_peer, name):
    n = {"chips": N_CHIPS, "all": N_DEV}[group]
    blk = x.shape[1:] if by_peer else x.shape
    npeer = n - 1
    W = blk[-1]
    rows = math.prod(blk[:-1])
    x2 = x.reshape((n, rows, W) if by_peer else (rows, W))
    nchunk = next((k for k in COMM_CHUNKS if rows % (k * 16) == 0), 1)
    cr = rows // nchunk

    def body(x_ref, o_ref, send_sems, recv_sems, local_sem):
        mx, my, mc = lax.axis_index("x"), lax.axis_index("y"), lax.axis_index("c")
        if group == "chips":
            me = 2 * mx + my
            peers = [((mx ^ fx, my ^ fy, mc), 2 * (mx ^ fx) + (my ^ fy)) for fx, fy in ((1, 0), (0, 1), (1, 1))]
        else:
            me = 4 * mx + 2 * my + mc
            peers = [((mx ^ fx, my ^ fy, mc ^ fc), 4 * (mx ^ fx) + 2 * (my ^ fy) + (mc ^ fc))
                     for fx in (0, 1) for fy in (0, 1) for fc in (0, 1) if fx + fy + fc]

        def src_of(pid):
            return x_ref.at[pid] if by_peer else x_ref

        def chunk(ref, ch):
            return ref.at[pl.ds(ch * cr, cr), :]

        def remote(k, dev, pid, ch=None):
            s, d = src_of(pid), o_ref.at[me]
            if ch is not None:
                s, d = chunk(s, ch), chunk(d, ch)
            return pltpu.make_async_remote_copy(src_ref=s, dst_ref=d, send_sem=send_sems.at[k],
                                                recv_sem=recv_sems.at[k], device_id=dev,
                                                device_id_type=pl.DeviceIdType.MESH)

        for ch in range(nchunk):
            pltpu.make_async_copy(chunk(src_of(me), ch), chunk(o_ref.at[me], ch), local_sem).start()
            for k, (dev, pid) in enumerate(peers):
                remote(k, dev, pid, ch).start()
        for k, (dev, pid) in enumerate(peers):
            remote(k, dev, pid).wait_recv()
        for k, (dev, pid) in enumerate(peers):
            remote(k, dev, pid).wait_send()
        pltpu.make_async_copy(src_of(me), o_ref.at[me], local_sem).wait()

    out = pl.pallas_call(
        body, name=name,
        out_shape=jax.ShapeDtypeStruct((n, rows, W), x.dtype),
        in_specs=[pl.BlockSpec(memory_space=pl.ANY)],
        out_specs=pl.BlockSpec(memory_space=pl.ANY),
        scratch_shapes=[pltpu.SemaphoreType.DMA((npeer,)), pltpu.SemaphoreType.DMA((npeer,)),
                        pltpu.SemaphoreType.DMA],
    )(x2)
    return out.reshape((n,) + tuple(blk))


def _carried_comm(kind, xs):
    nt = len(xs)
    masks = {"chips_ag": [(1, 0, 0), (0, 1, 0), (1, 1, 0)], "sib_ag": [(0, 0, 1)],
             "all_rs": [(fx, fy, fc) for fx in (0, 1) for fy in (0, 1) for fc in (0, 1) if fx + fy + fc]}[kind]
    npeer = len(masks)

    def copies(x_refs, o_refs, sems):
        send_sems, recv_sems, local_sems = sems
        mx, my, mc = lax.axis_index("x"), lax.axis_index("y"), lax.axis_index("c")
        me = {"chips_ag": 2 * mx + my, "sib_ag": mc, "all_rs": 4 * mx + 2 * my + mc}[kind]

        def src(t, px, py, pc):
            if kind != "all_rs":
                return x_refs[t]
            half = xs[t].shape[1] // 2
            return x_refs[t].at[2 * px + py, pl.ds(pl.multiple_of(pc * half, 16), half), :]

        local = [pltpu.make_async_copy(src(t, mx, my, mc), o_refs[t].at[me], local_sems.at[t]) for t in range(nt)]
        remote = [pltpu.make_async_remote_copy(
            src_ref=src(t, mx ^ fx, my ^ fy, mc ^ fc), dst_ref=o_refs[t].at[me],
            send_sem=send_sems.at[t * npeer + k], recv_sem=recv_sems.at[t * npeer + k],
            device_id=(mx ^ fx, my ^ fy, mc ^ fc), device_id_type=pl.DeviceIdType.MESH)
            for t in range(nt) for k, (fx, fy, fc) in enumerate(masks)]
        return local, remote

    def start(x_refs, o_refs, sems):
        local, remote = copies(x_refs, o_refs, sems)
        for cp in local + remote:
            cp.start()

    def wait(x_refs, o_refs, sems):
        local, remote = copies(x_refs, o_refs, sems)
        for cp in remote:
            cp.wait_recv()
        for cp in remote:
            cp.wait_send()
        for cp in local:
            cp.wait()

    def out_shape(x):
        if kind == "all_rs":
            return (N_DEV, x.shape[1] // 2, x.shape[2])
        return ((N_CHIPS if kind == "chips_ag" else 2),) + tuple(x.shape)

    out_shapes = [jax.ShapeDtypeStruct(out_shape(x), x.dtype) for x in xs]
    scratch = [pltpu.SemaphoreType.DMA((max(nt * npeer, 1),)), pltpu.SemaphoreType.DMA((max(nt * npeer, 1),)),
               pltpu.SemaphoreType.DMA((max(nt, 1),))]
    return out_shapes, scratch, start, wait


def _comm_alone(kind, xs, name):
    nt = len(xs)
    out_shapes, scratch, start, wait = _carried_comm(kind, xs)

    def body(*refs):
        x_refs, o_refs, sems = refs[:nt], refs[nt:2 * nt], refs[2 * nt:]
        start(x_refs, o_refs, sems)
        wait(x_refs, o_refs, sems)

    any_spec = pl.BlockSpec(memory_space=pl.ANY)
    return pl.pallas_call(body, name=name, out_shape=out_shapes, in_specs=[any_spec] * nt,
                          out_specs=[any_spec] * nt, scratch_shapes=scratch)(*xs)


def _allgather8(x, name):
    return _comm(x, group="all", by_peer=False, name=name)


def _sum_slots(x, out_dtype, name):
    n, R, W = x.shape
    tr = _tile(R, max(16, min(512, SUM_BLOCK_BYTES // (n * W * x.dtype.itemsize) // 16 * 16)), 16)

    def body(x_ref, o_ref):
        acc = x_ref[0].astype(F32)
        for s in range(1, n):
            acc = acc + x_ref[s].astype(F32)
        o_ref[...] = acc.astype(o_ref.dtype)

    return pl.pallas_call(
        body, name=name, grid=(R // tr,),
        in_specs=[pl.BlockSpec((n, tr, W), lambda i: (0, i, 0))],
        out_specs=pl.BlockSpec((tr, W), lambda i: (i, 0)),
        out_shape=jax.ShapeDtypeStruct((R, W), out_dtype),
        compiler_params=_cparams(("parallel",)),
    )(x)


def _pack(arrs, dtype, row_mult):
    flat = jnp.concatenate([a.reshape(-1).astype(dtype) for a in arrs])
    n = flat.shape[0]
    quant = row_mult * PACK_W
    total = -(-n // quant) * quant
    if total != n:
        flat = jnp.concatenate([flat, jnp.zeros((total - n,), dtype)])
    return flat.reshape(total // PACK_W, PACK_W)


def _unpack(buf, shapes):
    flat = buf.reshape(-1)
    out, off = [], 0
    for s in shapes:
        n = math.prod(s)
        out.append(flat[off:off + n].reshape(s))
        off += n
    return out


class _Mat:
    def __init__(self, arr, axis=None, layer=None, part0=0, nparts=None):
        self.arr, self.axis, self.layer, self.part0 = arr, axis, layer, part0
        self.r, self.c = arr.shape[-2:]
        self.nparts = 1 if axis is None else (arr.shape[0] if nparts is None else nparts)

    @property
    def shape(self):
        return (self.r * (self.nparts if self.axis == 0 else 1), self.c * (self.nparts if self.axis == 1 else 1))

    def unit(self, ax):
        return self.r if ax == 0 else self.c

    def spec(self, tr, tc, fn):
        def imap(*g):
            rb, cb = fn(*g)
            lead = []
            if self.axis == 0:
                per = self.r // tr
                lead.append(self.part0 + rb // per)
                rb = rb % per
            elif self.axis == 1:
                per = self.c // tc
                lead.append(self.part0 + cb // per)
                cb = cb % per
            if self.layer is not None:
                lead.append(self.layer)
            return (*lead, rb, cb)

        return pl.BlockSpec((None,) * (self.arr.ndim - 2) + (tr, tc), imap)


def _as_mat(x):
    return x if isinstance(x, _Mat) else _Mat(x)


def _matmul(a, b, *, mode, name, out_dtype=F32, add=None, out_split=None, carry=None, tm_cap=1088, tn_cap=1536,
            tk_cap=2048):
    a, b = _as_mat(a), _as_mat(b)
    am, ak = (0, 1) if mode != "tn" else (1, 0)
    bk, bn = (0, 1) if mode != "nt" else (1, 0)
    M, K, N = a.shape[am], a.shape[ak], b.shape[bn]
    assert b.shape[bk] == K
    if out_split is None:
        o = _Mat(jax.ShapeDtypeStruct((M, N), out_dtype))
    else:
        ax, parts = out_split
        o = _Mat(jax.ShapeDtypeStruct((parts, M // parts if ax == 0 else M, N // parts if ax == 1 else N), out_dtype),
                 axis=ax)
    tm = _tile(math.gcd(a.unit(am), o.unit(0)), tm_cap, 16 if mode != "tn" else LANES)
    tn = _tile(math.gcd(b.unit(bn), o.unit(1)), tn_cap, LANES)
    tk = _tile(math.gcd(a.unit(ak), b.unit(bk)), tk_cap, LANES if mode != "tn" else 16)
    nk = K // tk
    dims = {"nn": _NN, "nt": _NT, "tn": _TN}[mode]

    kind, carry = carry if carry else ("chips_ag", [])
    carry = list(carry)
    nc = len(carry)
    c_out, c_scratch, c_start, c_wait = _carried_comm(kind, carry)
    grid = (M // tm, N // tn, nk)

    def body(*refs):
        a_ref, b_ref = refs[:2]
        add_ref = refs[2] if add is not None else None
        n_in = 2 + (add is not None)
        cx_refs, o_ref, co_refs = refs[n_in:n_in + nc], refs[n_in + nc], refs[n_in + nc + 1:n_in + 2 * nc + 1]
        acc_ref, sems = refs[n_in + 2 * nc + 1], refs[n_in + 2 * nc + 2:]
        i, j, k = pl.program_id(0), pl.program_id(1), pl.program_id(2)
        if nc:
            @pl.when((i == 0) & (j == 0) & (k == 0))
            def _():
                c_start(cx_refs, co_refs, sems)

        @pl.when(k == 0)
        def _():
            acc_ref[...] = jnp.zeros_like(acc_ref)

        acc_ref[...] += _dot(a_ref[...], b_ref[...], dims)

        @pl.when(k == nk - 1)
        def _():
            r = acc_ref[...]
            if add_ref is not None:
                r = r + add_ref[...].astype(F32)
            o_ref[...] = r.astype(o_ref.dtype)

        if nc:
            @pl.when((i == grid[0] - 1) & (j == grid[1] - 1) & (k == nk - 1))
            def _():
                c_wait(cx_refs, co_refs, sems)

    if mode == "tn":
        a_spec = a.spec(tk, tm, lambda i, j, k: (k, i))
    else:
        a_spec = a.spec(tm, tk, lambda i, j, k: (i, k))
    if mode == "nt":
        b_spec = b.spec(tn, tk, lambda i, j, k: (j, k))
    else:
        b_spec = b.spec(tk, tn, lambda i, j, k: (k, j))
    o_spec = o.spec(tm, tn, lambda i, j, k: (i, j))
    any_spec = pl.BlockSpec(memory_space=pl.ANY)
    in_specs = [a_spec, b_spec] + ([pl.BlockSpec((tm, tn), lambda i, j, k: (i, j))] if add is not None else [])
    args = (a.arr, b.arr) + ((add,) if add is not None else ())
    outs = pl.pallas_call(
        body, name=name, grid=grid,
        in_specs=in_specs + [any_spec] * nc, out_specs=[o_spec] + [any_spec] * nc,
        out_shape=[o.arr] + c_out,
        scratch_shapes=[pltpu.VMEM((tm, tn), F32)] + (c_scratch if nc else []),
        compiler_params=_cparams(("arbitrary",) * 3 if nc else ("parallel", "parallel", "arbitrary")),
    )(*args, *carry)
    return (outs[0], outs[1:]) if nc else outs[0]


def _matmul_residual(a, w, res, gate, C, name, carry=None):
    w = _as_mat(w)
    M, K = a.shape
    N = w.shape[1]
    tm = _tile(M, 1088, 16)
    tn = _tile(w.unit(1), 1024, LANES)
    tk = _tile(w.unit(0), 2048, LANES)
    nk = K // tk
    kind, carry = carry if carry else ("chips_ag", [])
    carry = list(carry)
    nc = len(carry)
    c_out, c_scratch, c_start, c_wait = _carried_comm(kind, carry)
    grid = (M // tm, N // tn, nk)

    def body(*refs):
        a_ref, w_ref, res_ref, gate_ref = refs[:4]
        cx_refs, (mix_ref, o_ref), co_refs = refs[4:4 + nc], refs[4 + nc:6 + nc], refs[6 + nc:6 + 2 * nc]
        acc_ref, sems = refs[6 + 2 * nc], refs[7 + 2 * nc:]
        i, j, k = pl.program_id(0), pl.program_id(1), pl.program_id(2)
        if nc:
            @pl.when((i == 0) & (j == 0) & (k == 0))
            def _():
                c_start(cx_refs, co_refs, sems)

        @pl.when(k == 0)
        def _():
            acc_ref[...] = jnp.zeros_like(acc_ref)

        acc_ref[...] += _dot(a_ref[...], w_ref[...], _NN)

        @pl.when(k == nk - 1)
        def _():
            r = acc_ref[...]
            row = i * tm + lax.broadcasted_iota(jnp.int32, (tm, 1), 0)
            g = jnp.where(row < C, gate_ref[0:1, :], gate_ref[1:2, :])
            mix_ref[...] = r
            o_ref[...] = res_ref[...] + g * r

        if nc:
            @pl.when((i == grid[0] - 1) & (j == grid[1] - 1) & (k == nk - 1))
            def _():
                c_wait(cx_refs, co_refs, sems)

    o_spec = pl.BlockSpec((tm, tn), lambda i, j, k: (i, j))
    any_spec = pl.BlockSpec(memory_space=pl.ANY)
    outs = pl.pallas_call(
        body, name=name, grid=grid,
        in_specs=[pl.BlockSpec((tm, tk), lambda i, j, k: (i, k)),
                  w.spec(tk, tn, lambda i, j, k: (k, j)),
                  o_spec,
                  pl.BlockSpec((2, tn), lambda i, j, k: (0, j))] + [any_spec] * nc,
        out_specs=[o_spec, o_spec] + [any_spec] * nc,
        out_shape=[jax.ShapeDtypeStruct((M, N), F32), jax.ShapeDtypeStruct((M, N), F32)] + c_out,
        scratch_shapes=[pltpu.VMEM((tm, tn), F32)] + (c_scratch if nc else []),
        compiler_params=_cparams(("arbitrary",) * 3 if nc else ("parallel", "parallel", "arbitrary")),
    )(a, w.arr, res, gate, *carry)
    return (outs[0], outs[1], outs[2:]) if nc else (outs[0], outs[1])


def _seg_select(i, tm, C, ref):
    row = i * tm + lax.broadcasted_iota(jnp.int32, (tm, 1), 0)
    return row < C, jnp.where(row < C, ref[0:1, :], ref[1:2, :])


def _normmod(X, g, shift, scale, C, name):
    T, D = X.shape
    tm = _tile(T, 512, 16)

    def body(x_ref, g_ref, sh_ref, sc_ref, h_ref):
        i = pl.program_id(0)
        x = x_ref[...]
        r = lax.rsqrt(jnp.mean(x * x, axis=-1, keepdims=True) + NORM_EPS)
        _, sh = _seg_select(i, tm, C, sh_ref)
        _, sc = _seg_select(i, tm, C, sc_ref)
        h_ref[...] = ((x * r) * g_ref[...] * (1.0 + sc) + sh).astype(BF16)

    row = pl.BlockSpec((tm, D), lambda i: (i, 0))
    vec = pl.BlockSpec((1, D), lambda i: (0, 0))
    two = pl.BlockSpec((2, D), lambda i: (0, 0))
    return pl.pallas_call(
        body, name=name, grid=(T // tm,),
        in_specs=[row, vec, two, two], out_specs=row,
        out_shape=jax.ShapeDtypeStruct((T, D), BF16),
        compiler_params=_cparams(("parallel",)),
    )(X, g, shift, scale)


def _normmod_bwd(X, g, scale, dh, dres, C, name):
    T, D = X.shape
    tm = _tile(T, 512, 16)

    def body(x_ref, g_ref, sc_ref, dh_ref, dres_ref, dx_ref, dg_ref, dsh_ref, dsc_ref):
        i = pl.program_id(0)

        @pl.when(i == 0)
        def _():
            dg_ref[...] = jnp.zeros_like(dg_ref)
            dsh_ref[...] = jnp.zeros_like(dsh_ref)
            dsc_ref[...] = jnp.zeros_like(dsc_ref)

        x = x_ref[...]
        dh = dh_ref[...]
        gv = g_ref[...]
        r = lax.rsqrt(jnp.mean(x * x, axis=-1, keepdims=True) + NORM_EPS)
        xh = x * r
        isc, sc = _seg_select(i, tm, C, sc_ref)
        n = xh * gv
        dhn = dh * n
        zero = jnp.zeros_like(dh)
        dsh_ref[0:1, :] += jnp.sum(jnp.where(isc, dh, zero), axis=0, keepdims=True)
        dsh_ref[1:2, :] += jnp.sum(jnp.where(isc, zero, dh), axis=0, keepdims=True)
        dsc_ref[0:1, :] += jnp.sum(jnp.where(isc, dhn, zero), axis=0, keepdims=True)
        dsc_ref[1:2, :] += jnp.sum(jnp.where(isc, zero, dhn), axis=0, keepdims=True)
        dn = dh * (1.0 + sc)
        dg_ref[...] += jnp.sum(dn * xh, axis=0, keepdims=True)
        dxh = dn * gv
        dx_ref[...] = dres_ref[...] + r * (dxh - xh * jnp.mean(dxh * xh, axis=-1, keepdims=True))

    row = pl.BlockSpec((tm, D), lambda i: (i, 0))
    vec = pl.BlockSpec((1, D), lambda i: (0, 0))
    two = pl.BlockSpec((2, D), lambda i: (0, 0))
    return pl.pallas_call(
        body, name=name, grid=(T // tm,),
        in_specs=[row, vec, two, row, row], out_specs=[row, vec, two, two],
        out_shape=[jax.ShapeDtypeStruct((T, D), F32), jax.ShapeDtypeStruct((1, D), F32),
                   jax.ShapeDtypeStruct((2, D), F32), jax.ShapeDtypeStruct((2, D), F32)],
        compiler_params=_cparams(("arbitrary",)),
    )(X, g, scale, dh, dres)


def _gate_bwd(dxo, mix, gate, C, name):
    T, D = dxo.shape
    tm = _tile(T, 512, 16)

    def body(dx_ref, mix_ref, gate_ref, dmix_ref, dgate_ref):
        i = pl.program_id(0)

        @pl.when(i == 0)
        def _():
            dgate_ref[...] = jnp.zeros_like(dgate_ref)

        dx = dx_ref[...]
        isc, gt = _seg_select(i, tm, C, gate_ref)
        dmix_ref[...] = (dx * gt).astype(BF16)
        p = dx * mix_ref[...]
        zero = jnp.zeros_like(p)
        dgate_ref[0:1, :] += jnp.sum(jnp.where(isc, p, zero), axis=0, keepdims=True)
        dgate_ref[1:2, :] += jnp.sum(jnp.where(isc, zero, p), axis=0, keepdims=True)

    row = pl.BlockSpec((tm, D), lambda i: (i, 0))
    two = pl.BlockSpec((2, D), lambda i: (0, 0))
    return pl.pallas_call(
        body, name=name, grid=(T // tm,),
        in_specs=[row, row, two], out_specs=[row, two],
        out_shape=[jax.ShapeDtypeStruct((T, D), BF16), jax.ShapeDtypeStruct((2, D), F32)],
        compiler_params=_cparams(("arbitrary",)),
    )(dxo, mix, gate)


def _conv_masks(T, C):
    row = lax.broadcasted_iota(jnp.int32, (T, 1), 0)
    first = (row == 0) | (row == C)
    last = (row == C - 1) | (row == T - 1)
    return first, last


def _shift_rows(x, first, last):
    T = x.shape[0]
    prev = jnp.where(first, 0.0, pltpu.roll(x, 1, 0))
    nxt = jnp.where(last, 0.0, pltpu.roll(x, T - 1, 0))
    return prev, nxt


def _convact(up, cw, cb, C, name):
    _, T, F = up.shape
    tc = LANES

    def body(u_ref, w_ref, b_ref, o_ref):
        gp = u_ref[0].astype(F32)
        first, last = _conv_masks(T, C)
        prev, nxt = _shift_rows(gp, first, last)
        gc = prev * w_ref[0:1, :] + gp * w_ref[1:2, :] + nxt * w_ref[2:3, :] + b_ref[...]
        o_ref[...] = (gc * _sigmoid(gc) * u_ref[1].astype(F32)).astype(BF16)

    col = pl.BlockSpec((T, tc), lambda j: (0, j))
    return pl.pallas_call(
        body, name=name, grid=(F // tc,),
        in_specs=[pl.BlockSpec((2, T, tc), lambda j: (0, 0, j)), pl.BlockSpec((3, tc), lambda j: (0, j)),
                  pl.BlockSpec((1, tc), lambda j: (0, j))],
        out_specs=col, out_shape=jax.ShapeDtypeStruct((T, F), BF16),
        compiler_params=_cparams(("parallel",)),
    )(up, cw, cb)


def _convact_bwd(up, cw, cb, dact, C, name):
    _, T, F = up.shape
    tc = LANES

    def body(u_ref, w_ref, b_ref, da_ref, du_ref, dw_ref, db_ref):
        gp = u_ref[0].astype(F32)
        val = u_ref[1].astype(F32)
        da = da_ref[...].astype(F32)
        first, last = _conv_masks(T, C)
        prev, nxt = _shift_rows(gp, first, last)
        w0, w1, w2 = w_ref[0:1, :], w_ref[1:2, :], w_ref[2:3, :]
        gc = prev * w0 + gp * w1 + nxt * w2 + b_ref[...]
        s = _sigmoid(gc)
        du_ref[1] = (da * gc * s).astype(BF16)
        dgc = da * val * (s + gc * s * (1.0 - s))
        db_ref[...] = jnp.sum(dgc, axis=0, keepdims=True)
        dw_ref[0:1, :] = jnp.sum(dgc * prev, axis=0, keepdims=True)
        dw_ref[1:2, :] = jnp.sum(dgc * gp, axis=0, keepdims=True)
        dw_ref[2:3, :] = jnp.sum(dgc * nxt, axis=0, keepdims=True)
        dprev, dnxt = _shift_rows(dgc, first, last)
        du_ref[0] = (dnxt * w0 + dgc * w1 + dprev * w2).astype(BF16)

    col = pl.BlockSpec((T, tc), lambda j: (0, j))
    two = pl.BlockSpec((2, T, tc), lambda j: (0, 0, j))
    w3 = pl.BlockSpec((3, tc), lambda j: (0, j))
    w1 = pl.BlockSpec((1, tc), lambda j: (0, j))
    return pl.pallas_call(
        body, name=name, grid=(F // tc,),
        in_specs=[two, w3, w1, col], out_specs=[two, w3, w1],
        out_shape=[jax.ShapeDtypeStruct((2, T, F), BF16),
                   jax.ShapeDtypeStruct((3, F), F32), jax.ShapeDtypeStruct((1, F), F32)],
        compiler_params=_cparams(("parallel",)),
    )(up, cw, cb, dact)


def _rot_half(x):
    W = x.shape[-1]
    lane = lax.broadcasted_iota(jnp.int32, x.shape, x.ndim - 1)
    lo = (lane % 32) < 16
    return jnp.where(lo, -pltpu.roll(x, W - 16, x.ndim - 1), pltpu.roll(x, 16, x.ndim - 1))


def _swap_halves(x):
    return pltpu.roll(x, 64, x.ndim - 1)


def _rope_tables(dm):
    t = jnp.arange(dm.L)
    n_freq = HEAD_DIM // 4
    inv_freq = ROPE_BASE ** (-jnp.arange(n_freq, dtype=F32) / n_freq)
    ang_r = (t // GRID_W).astype(F32)[:, None] * inv_freq
    ang_c = (t % GRID_W).astype(F32)[:, None] * inv_freq
    ang = jnp.concatenate([ang_r, ang_r, ang_c, ang_c], axis=-1)
    cos = jnp.concatenate([jnp.ones((dm.C, HEAD_DIM), F32), jnp.cos(ang)], axis=0)
    sin = jnp.concatenate([jnp.zeros((dm.C, HEAD_DIM), F32), jnp.sin(ang)], axis=0)
    return jnp.tile(cos, (1, 2)), jnp.tile(sin, (1, 2))


def _attn_prep(z, cos, sin, dm, name):
    T = dm.T
    AW = dm.ATTN_W
    tm = _tile(T, 256, 16)
    rep = AW // LANES

    def body(z_ref, cos_ref, sin_ref, q_ref, k_ref, v_ref):
        cs, sn = cos_ref[...], sin_ref[...]
        q = z_ref[:, 0:AW]
        csq, snq = jnp.tile(cs, (1, rep)), jnp.tile(sn, (1, rep))
        q_ref[...] = (q * csq + _rot_half(q) * snq).astype(BF16)
        k = z_ref[:, dm.OFF_K:dm.OFF_V]
        k = k * cs + _rot_half(k) * sn
        v = z_ref[:, dm.OFF_V:dm.OFF_S]
        lo = lax.broadcasted_iota(jnp.int32, k.shape, 1) < HEAD_DIM
        ks, vs = _swap_halves(k), _swap_halves(v)
        k_ref[0] = jnp.where(lo, k, ks).astype(BF16)
        k_ref[1] = jnp.where(lo, ks, k).astype(BF16)
        v_ref[0] = jnp.where(lo, v, vs).astype(BF16)
        v_ref[1] = jnp.where(lo, vs, v).astype(BF16)

    tab = pl.BlockSpec((tm, LANES), lambda i: (i, 0))
    kv = pl.BlockSpec((2, tm, LANES), lambda i: (0, i, 0))
    return pl.pallas_call(
        body, name=name, grid=(T // tm,),
        in_specs=[pl.BlockSpec((tm, dm.OFF_S), lambda i: (i, 0)), tab, tab],
        out_specs=[pl.BlockSpec((tm, AW), lambda i: (i, 0)), kv, kv],
        out_shape=[jax.ShapeDtypeStruct((T, AW), BF16), jax.ShapeDtypeStruct((2, T, LANES), BF16),
                   jax.ShapeDtypeStruct((2, T, LANES), BF16)],
        compiler_params=_cparams(("parallel",)),
    )(z, cos, sin)


def _attn_prep_bwd(dq, dk2, dv2, cos, sin, dm, name):
    T = dm.T
    AW = dm.ATTN_W
    tm = _tile(T, 256, 16)
    rep = AW // LANES

    def body(dq_ref, dk_ref, dv_ref, cos_ref, sin_ref, o_ref):
        cs, sn = cos_ref[...], sin_ref[...]
        csq, snq = jnp.tile(cs, (1, rep)), jnp.tile(sn, (1, rep))
        dq = dq_ref[...]
        o_ref[:, 0:AW] = dq * csq - _rot_half(dq * snq)
        lo = lax.broadcasted_iota(jnp.int32, (tm, LANES), 1) < HEAD_DIM
        dk = jnp.where(lo, dk_ref[0], dk_ref[1])
        o_ref[:, dm.OFF_K:dm.OFF_V] = dk * cs - _rot_half(dk * sn)
        o_ref[:, dm.OFF_V:dm.OFF_S] = jnp.where(lo, dv_ref[0], dv_ref[1])

    tab = pl.BlockSpec((tm, LANES), lambda i: (i, 0))
    kv = pl.BlockSpec((2, tm, LANES), lambda i: (0, i, 0))
    return pl.pallas_call(
        body, name=name, grid=(T // tm,),
        in_specs=[pl.BlockSpec((tm, AW), lambda i: (i, 0)), kv, kv, tab, tab],
        out_specs=pl.BlockSpec((tm, dm.OFF_S), lambda i: (i, 0)),
        out_shape=jax.ShapeDtypeStruct((T, dm.OFF_S), F32),
        compiler_params=_cparams(("parallel",)),
    )(dq, dk2, dv2, cos, sin)


def _attn_specs(dm):
    B = ATTN_BLOCK
    nblk = dm.T // B
    q_spec = pl.BlockSpec((B, dm.ATTN_W), lambda i: (i, 0))
    prev = pl.BlockSpec((2, B, LANES), lambda i: (0, jnp.maximum(i - 1, 0), 0))
    own = pl.BlockSpec((2, B, LANES), lambda i: (0, i, 0))
    nxt = pl.BlockSpec((2, B, LANES), lambda i: (0, jnp.minimum(i + 1, nblk - 1), 0))
    ctx = pl.BlockSpec((2, dm.C, LANES), lambda i: (0, 0, 0))
    sink = pl.BlockSpec((2, GQA_RATIO * B, 1), lambda i: (0, 0, 0))
    return nblk, q_spec, prev, own, nxt, ctx, sink


def _attn_scores(i, q_ref, kp_ref, ko_ref, kn_ref, kc_ref, sink_ref, h, dm):
    B = ATTN_BLOCK
    nctx = dm.C // B
    nb = dm.L // B
    npair = GQA_RATIO // 2
    lo = lax.broadcasted_iota(jnp.int32, (B, LANES), 1) < HEAD_DIM
    zero = jnp.zeros((B, LANES), BF16)
    parts = []
    for p in range(npair):
        q2 = q_ref[:, (h * npair + p) * LANES:(h * npair + p + 1) * LANES]
        parts.append(jnp.where(lo, q2, zero))
        parts.append(jnp.where(lo, zero, q2))
    q8 = jnp.concatenate(parts, axis=0)
    kcat = jnp.concatenate([kp_ref[h], ko_ref[h], kn_ref[h], kc_ref[h]], axis=0)
    s = _dot(q8, kcat, _NT) * (HEAD_DIM ** -0.5)
    R, NK = s.shape
    iq = lax.broadcasted_iota(jnp.int32, (R, NK), 0) % B
    col = lax.broadcasted_iota(jnp.int32, (R, NK), 1)
    jk = col % B
    qb = i - nctx
    lat = qb >= 0
    m_prev = (col < B) & lat & (qb >= 1) & (jk >= iq)
    m_own = (col >= B) & (col < 2 * B) & lat
    m_next = (col >= 2 * B) & (col < 3 * B) & lat & (qb <= nb - 2) & (jk <= iq)
    mask = m_prev | m_own | m_next | (col >= 3 * B)
    s = jnp.where(mask, s, -jnp.inf)
    sk = sink_ref[h]
    m = jnp.maximum(jnp.max(s, axis=-1, keepdims=True), sk)
    e = jnp.exp(s - m)
    es = jnp.exp(sk - m)
    inv = 1.0 / (jnp.sum(e, axis=-1, keepdims=True) + es)
    return q8, kcat, e * inv, es * inv, lo


def _unstack_heads(x8, lo):
    B = ATTN_BLOCK
    out = []
    for p in range(GQA_RATIO // 2):
        a = x8[(2 * p) * B:(2 * p + 1) * B]
        b = x8[(2 * p + 1) * B:(2 * p + 2) * B]
        out.append(jnp.where(lo, a, b))
    return out


def _attention(q, k2, v2, sinkcol, dm, name):
    B = ATTN_BLOCK
    nblk, q_spec, prev, own, nxt, ctx, sink = _attn_specs(dm)
    npair = GQA_RATIO // 2

    def body(q_ref, kp_ref, ko_ref, kn_ref, kc_ref, vp_ref, vo_ref, vn_ref, vc_ref, sink_ref, o_ref):
        i = pl.program_id(0)
        for h in range(2):
            _, _, p, _, lo = _attn_scores(i, q_ref, kp_ref, ko_ref, kn_ref, kc_ref, sink_ref, h, dm)
            vcat = jnp.concatenate([vp_ref[h], vo_ref[h], vn_ref[h], vc_ref[h]], axis=0)
            o8 = _dot(p, vcat, _NN)
            for pi, o2 in enumerate(_unstack_heads(o8, lo)):
                c0 = (h * npair + pi) * LANES
                o_ref[:, c0:c0 + LANES] = o2.astype(BF16)

    return pl.pallas_call(
        body, name=name, grid=(nblk,),
        in_specs=[q_spec, prev, own, nxt, ctx, prev, own, nxt, ctx, sink],
        out_specs=q_spec, out_shape=jax.ShapeDtypeStruct((dm.T, dm.ATTN_W), BF16),
        compiler_params=_cparams(("parallel",)),
    )(q, k2, k2, k2, k2, v2, v2, v2, v2, sinkcol)


def _attention_bwd(q, k2, v2, sinkcol, do, dm, name):
    B = ATTN_BLOCK
    nblk, q_spec, prev, own, nxt, ctx, sink = _attn_specs(dm)
    npair = GQA_RATIO // 2

    def body(q_ref, kp_ref, ko_ref, kn_ref, kc_ref, vp_ref, vo_ref, vn_ref, vc_ref, sink_ref, do_ref,
             dq_ref, pk_ref, pv_ref, dkc_ref, dvc_ref, dsk_ref):
        i = pl.program_id(0)

        @pl.when(i == 0)
        def _():
            dkc_ref[...] = jnp.zeros_like(dkc_ref)
            dvc_ref[...] = jnp.zeros_like(dvc_ref)
            dsk_ref[...] = jnp.zeros_like(dsk_ref)

        for h in range(2):
            q8, kcat, p, ps, lo = _attn_scores(i, q_ref, kp_ref, ko_ref, kn_ref, kc_ref, sink_ref, h, dm)
            vcat = jnp.concatenate([vp_ref[h], vo_ref[h], vn_ref[h], vc_ref[h]], axis=0)
            zero = jnp.zeros((B, LANES), F32)
            parts = []
            for pi in range(npair):
                d2 = do_ref[:, (h * npair + pi) * LANES:(h * npair + pi + 1) * LANES]
                parts.append(jnp.where(lo, d2, zero))
                parts.append(jnp.where(lo, zero, d2))
            do8 = jnp.concatenate(parts, axis=0)
            dp = _dot(do8, vcat, _NT)
            delta = jnp.sum(p * dp, axis=-1, keepdims=True)
            ds = p * (dp - delta) * (HEAD_DIM ** -0.5)
            dsr = -ps * delta
            rows = [jnp.broadcast_to(jnp.sum(dsr[r * B:(r + 1) * B], axis=0, keepdims=True), (1, LANES))
                    for r in range(GQA_RATIO)]
            dsk_ref[h] += jnp.concatenate(rows, axis=0)
            dq8 = _dot(ds, kcat, _NN)
            for pi, d2 in enumerate(_unstack_heads(dq8, lo)):
                c0 = (h * npair + pi) * LANES
                dq_ref[:, c0:c0 + LANES] = d2
            dk = _dot(ds, q8, _TN)
            dv = _dot(p, do8, _TN)
            dk = dk + _swap_halves(dk)
            dv = dv + _swap_halves(dv)
            for w in range(3):
                pk_ref[0, h, w] = dk[w * B:(w + 1) * B]
                pv_ref[0, h, w] = dv[w * B:(w + 1) * B]
            dkc_ref[h] += dk[3 * B:]
            dvc_ref[h] += dv[3 * B:]

    part = pl.BlockSpec((1, 2, 3, B, LANES), lambda i: (i, 0, 0, 0, 0))
    acc_c = pl.BlockSpec((2, dm.C, LANES), lambda i: (0, 0, 0))
    acc_s = pl.BlockSpec((2, GQA_RATIO, LANES), lambda i: (0, 0, 0))
    return pl.pallas_call(
        body, name=name, grid=(nblk,),
        in_specs=[q_spec, prev, own, nxt, ctx, prev, own, nxt, ctx, sink, q_spec],
        out_specs=[q_spec, part, part, acc_c, acc_c, acc_s],
        out_shape=[jax.ShapeDtypeStruct((dm.T, dm.ATTN_W), F32),
                   jax.ShapeDtypeStruct((nblk, 2, 3, B, LANES), F32),
                   jax.ShapeDtypeStruct((nblk, 2, 3, B, LANES), F32),
                   jax.ShapeDtypeStruct((2, dm.C, LANES), F32), jax.ShapeDtypeStruct((2, dm.C, LANES), F32),
                   jax.ShapeDtypeStruct((2, GQA_RATIO, LANES), F32)],
        compiler_params=_cparams(("arbitrary",)),
    )(q, k2, k2, k2, k2, v2, v2, v2, v2, sinkcol, do)


def _kv_reduce(part, ctxacc, dm, name):
    B = ATTN_BLOCK
    nblk = dm.T // B
    nctx = dm.C // B

    def body(own_ref, nxt_ref, prv_ref, c_ref, o_ref):
        j = pl.program_id(0)
        acc = own_ref[0, :, 0]
        acc = acc + jnp.where(j < nblk - 1, nxt_ref[0, :, 0], 0.0)
        acc = acc + jnp.where(j > 0, prv_ref[0, :, 0], 0.0)
        acc = acc + jnp.where(j < nctx, c_ref[...], 0.0)
        o_ref[...] = acc

    def spec(w, f):
        return pl.BlockSpec((1, 2, 1, B, LANES), lambda j: (f(j), 0, w, 0, 0))

    return pl.pallas_call(
        body, name=name, grid=(nblk,),
        in_specs=[spec(1, lambda j: j), spec(0, lambda j: jnp.minimum(j + 1, nblk - 1)),
                  spec(2, lambda j: jnp.maximum(j - 1, 0)),
                  pl.BlockSpec((2, B, LANES), lambda j: (0, jnp.minimum(j, nctx - 1), 0))],
        out_specs=pl.BlockSpec((2, B, LANES), lambda j: (0, j, 0)),
        out_shape=jax.ShapeDtypeStruct((2, dm.T, LANES), F32),
        compiler_params=_cparams(("parallel",)),
    )(part, part, part, ctxacc)


def _gmlp_core(z_ref, lg_ref, lb_ref, ws_ref, bs_ref, dm):
    GW = dm.GMLP_W
    gu = z_ref[:, dm.OFF_GU:dm.OFF_GV]
    gv = z_ref[:, dm.OFF_GV:dm.N_IN]
    u = _gelu(gu)
    ge = _gelu(gv)
    mu = jnp.mean(ge, axis=-1, keepdims=True)
    xc = ge - mu
    r = lax.rsqrt(jnp.mean(xc * xc, axis=-1, keepdims=True) + NORM_EPS)
    xh = xc * r
    v = xh * lg_ref[...] + lb_ref[...]
    mixed = []
    for g in range(dm.GG):
        vg = v[:, g * GMLP_GROUP_W:(g + 1) * GMLP_GROUP_W]
        mixed.append(_dot(ws_ref[g], vg, _NN) + bs_ref[g])
    return gu, gv, u, xh, r, v, mixed


def _gmlp(z, ln_g, ln_b, w_s, b_s, dm, name):
    T, GW, CH = dm.T, dm.GMLP_W, GMLP_CHUNK

    def body(z_ref, lg_ref, lb_ref, ws_ref, bs_ref, o_ref):
        _, _, u, _, _, _, mixed = _gmlp_core(z_ref, lg_ref, lb_ref, ws_ref, bs_ref, dm)
        for g in range(dm.GG):
            sl = slice(g * GMLP_GROUP_W, (g + 1) * GMLP_GROUP_W)
            o_ref[:, sl] = (u[:, sl] * mixed[g]).astype(BF16)

    vec = pl.BlockSpec((1, GW), lambda i: (0, 0))
    return pl.pallas_call(
        body, name=name, grid=(T // CH,),
        in_specs=[pl.BlockSpec((CH, dm.N_IN), lambda i: (i, 0)), vec, vec,
                  pl.BlockSpec((dm.GG, CH, CH), lambda i: (0, 0, 0)),
                  pl.BlockSpec((dm.GG, CH, 1), lambda i: (0, 0, 0))],
        out_specs=pl.BlockSpec((CH, GW), lambda i: (i, 0)),
        out_shape=jax.ShapeDtypeStruct((T, GW), BF16),
        compiler_params=_cparams(("parallel",)),
    )(z, ln_g, ln_b, w_s, b_s)


def _gmlp_bwd(z, ln_g, ln_b, w_s, b_s, do, dm, name):
    T, GW, CH = dm.T, dm.GMLP_W, GMLP_CHUNK

    def body(z_ref, lg_ref, lb_ref, ws_ref, bs_ref, do_ref, dz_ref, dlg_ref, dlb_ref, dws_ref, dbs_ref):
        i = pl.program_id(0)

        @pl.when(i == 0)
        def _():
            dlg_ref[...] = jnp.zeros_like(dlg_ref)
            dlb_ref[...] = jnp.zeros_like(dlb_ref)
            dws_ref[...] = jnp.zeros_like(dws_ref)
            dbs_ref[...] = jnp.zeros_like(dbs_ref)

        gu, gv, u, xh, r, v, mixed = _gmlp_core(z_ref, lg_ref, lb_ref, ws_ref, bs_ref, dm)
        do = do_ref[...]
        dv_parts = []
        for g in range(dm.GG):
            sl = slice(g * GMLP_GROUP_W, (g + 1) * GMLP_GROUP_W)
            dog = do[:, sl]
            dz_ref[:, sl] = dog * mixed[g] * _gelu_grad(gu[:, sl])
            dmx = dog * u[:, sl]
            dbs_ref[g] += jnp.sum(dmx, axis=-1, keepdims=True)
            dws_ref[g] += _dot(dmx, v[:, sl], _NT)
            dv_parts.append(_dot(ws_ref[g], dmx, _TN))
        dv = jnp.concatenate(dv_parts, axis=-1) if dm.GG > 1 else dv_parts[0]
        dlg_ref[...] += jnp.sum(dv * xh, axis=0, keepdims=True)
        dlb_ref[...] += jnp.sum(dv, axis=0, keepdims=True)
        dxh = dv * lg_ref[...]
        dge = r * (dxh - jnp.mean(dxh, axis=-1, keepdims=True) - xh * jnp.mean(dxh * xh, axis=-1, keepdims=True))
        dz_ref[:, GW:2 * GW] = dge * _gelu_grad(gv)

    vec = pl.BlockSpec((1, GW), lambda i: (0, 0))
    wsp = pl.BlockSpec((dm.GG, CH, CH), lambda i: (0, 0, 0))
    bsp = pl.BlockSpec((dm.GG, CH, 1), lambda i: (0, 0, 0))
    return pl.pallas_call(
        body, name=name, grid=(T // CH,),
        in_specs=[pl.BlockSpec((CH, dm.N_IN), lambda i: (i, 0)), vec, vec, wsp, bsp,
                  pl.BlockSpec((CH, GW), lambda i: (i, (dm.ATTN_W + dm.SSM_W) // GW))],
        out_specs=[pl.BlockSpec((CH, 2 * GW), lambda i: (i, 0)), vec, vec, wsp, bsp],
        out_shape=[jax.ShapeDtypeStruct((T, 2 * GW), F32), jax.ShapeDtypeStruct((1, GW), F32),
                   jax.ShapeDtypeStruct((1, GW), F32), jax.ShapeDtypeStruct((dm.GG, CH, CH), F32),
                   jax.ShapeDtypeStruct((dm.GG, CH, 1), F32)],
        compiler_params=_cparams(("arbitrary",)),
    )(z, ln_g, ln_b, w_s, b_s, do)


def _disc_fn(lam_re, lam_im, ldt, b_re, b_im):
    dt = jnp.exp(ldt)
    mag = jnp.exp(lam_re * dt)
    a_re = mag * jnp.cos(lam_im * dt)
    a_im = mag * jnp.sin(lam_im * dt)
    den = lam_re * lam_re + lam_im * lam_im
    n_re = a_re - 1.0
    f_re = (n_re * lam_re + a_im * lam_im) / den
    f_im = (a_im * lam_re - n_re * lam_im) / den
    return a_re, a_im, f_re * b_re - f_im * b_im, f_re * b_im + f_im * b_re


def _s5_disc(lam_re, lam_im, ldt, bT_re, bT_im, name):
    nd, H, GP = bT_re.shape

    def body(lr_ref, li_ref, dt_ref, br_ref, bi_ref, ar_ref, ai_ref, bbr_ref, bbi_ref):
        for d in range(nd):
            a_re, a_im, bb_re, bb_im = _disc_fn(lr_ref[d], li_ref[d], dt_ref[d], br_ref[d], bi_ref[d])
            ar_ref[d] = a_re
            ai_ref[d] = a_im
            bbr_ref[d] = bb_re
            bbi_ref[d] = bb_im

    vs = jax.ShapeDtypeStruct((nd, 1, GP), F32)
    ms = jax.ShapeDtypeStruct((nd, H, GP), F32)
    return pl.pallas_call(body, name=name, out_shape=[vs, vs, ms, ms], compiler_params=_cparams())(
        lam_re, lam_im, ldt, bT_re, bT_im)


def _s5_disc_bwd(lam_re, lam_im, ldt, bT_re, bT_im, da_re, da_im, dbb_re, dbb_im, name):
    nd, H, GP = bT_re.shape

    def body(lr_ref, li_ref, dt_ref, br_ref, bi_ref, dar_ref, dai_ref, dbr_ref, dbi_ref,
             olr_ref, oli_ref, odt_ref, obr_ref, obi_ref):
        for d in range(nd):
            _, vjp = jax.vjp(_disc_fn, lr_ref[d], li_ref[d], dt_ref[d], br_ref[d], bi_ref[d])
            g = vjp((dar_ref[d], dai_ref[d], dbr_ref[d], dbi_ref[d]))
            olr_ref[d], oli_ref[d], odt_ref[d], obr_ref[d], obi_ref[d] = g

    vs = jax.ShapeDtypeStruct((nd, 1, GP), F32)
    ms = jax.ShapeDtypeStruct((nd, H, GP), F32)
    return pl.pallas_call(body, name=name, out_shape=[vs, vs, vs, ms, ms], compiler_params=_cparams())(
        lam_re, lam_im, ldt, bT_re, bT_im, da_re, da_im, dbb_re, dbb_im)


def _scan_pass(re_ref, im_ref, nsteps, ar, ai, ir, ii, reverse, store):
    def step(n, carry):
        sr, si = carry
        t = (nsteps - 1 - n) if reverse else n
        off = pl.multiple_of(t * 8, 8)
        nr = ar * sr - ai * si + re_ref[pl.ds(off, 8), :]
        ni = ar * si + ai * sr + im_ref[pl.ds(off, 8), :]
        if store:
            re_ref[pl.ds(off, 8), :] = nr
            im_ref[pl.ds(off, 8), :] = ni
        return nr, ni

    return lax.fori_loop(0, nsteps, step, (ir, ii), unroll=4)


def _cpow(ar, ai, n):
    rr = ri = None
    br, bi = ar, ai
    while n:
        if n & 1:
            rr, ri = (br, bi) if rr is None else (rr * br - ri * bi, rr * bi + ri * br)
        n >>= 1
        if n:
            br, bi = br * br - bi * bi, 2.0 * br * bi
    return rr, ri


def _row_of(x, k):
    rows = lax.broadcasted_iota(jnp.int32, x.shape, 0)
    return jnp.sum(jnp.where(rows == k, x, 0.0), axis=0, keepdims=True)


def _full_scan(re_ref, im_ref, nsteps, ar1, ai1, h0r, h0i, reverse):
    P = ar1.shape[1]
    ar, ai = jnp.broadcast_to(ar1, (8, P)), jnp.broadcast_to(ai1, (8, P))
    z = jnp.zeros((8, P), F32)
    er, ei = _scan_pass(re_ref, im_ref, nsteps, ar, ai, z, z, reverse, False)
    pr, pi = _cpow(ar1, ai1, nsteps)
    rows = lax.broadcasted_iota(jnp.int32, (8, P), 0)
    initr, initi = z, z
    cr, ci = h0r, h0i
    for k in (range(SCAN_SEGMENTS - 1, -1, -1) if reverse else range(SCAN_SEGMENTS)):
        initr = jnp.where(rows == k, cr, initr)
        initi = jnp.where(rows == k, ci, initi)
        ekr, eki = _row_of(er, k), _row_of(ei, k)
        cr, ci = ekr + pr * cr - pi * ci, eki + pr * ci + pi * cr
    _scan_pass(re_ref, im_ref, nsteps, ar, ai, initr, initi, reverse, True)
    return initr, initi, cr, ci


def _rows_loop(n, chunk, fn):
    def step(c, carry):
        fn(pl.multiple_of(c * chunk, chunk))
        return carry
    lax.fori_loop(0, n // chunk, step, 0)


def _s5_specs(dm, PB):
    nsb = dm.GP // PB
    per = (LANES * SSM_STATE // SSM_GROUP) // PB
    ul = pl.BlockSpec((dm.L, LANES), lambda j: (0, j // per))
    uc = pl.BlockSpec((dm.C, LANES), lambda j: (0, j // per))
    av = pl.BlockSpec((2, 1, PB), lambda j: (0, 0, j))
    bd = pl.BlockSpec((2, LANES, PB), lambda j: (0, j // per, j))
    cd = pl.BlockSpec((2, PB, LANES), lambda j: (0, j, j // per))
    dv = pl.BlockSpec((1, LANES), lambda j: (0, j // per))
    return PB, nsb, per, ul, uc, av, bd, cd, dv


def _s5_scan(ul, uc, a_re, a_im, bd_re, bd_im, cd_re, cd_im, dskip, dm, name):
    PB, nsb, per, ul_s, uc_s, av, bd, cd, dv = _s5_specs(dm, STATE_BLOCK_FWD)
    L, C = dm.L, dm.C
    RC = _tile(L, 512, 8)
    RCC = _tile(C, 512, 8)

    def body(ul_ref, uc_ref, ar_ref, ai_ref, br_ref, bi_ref, cr_ref, ci_ref, d_ref, yl_ref, yc_ref,
             lre, lim, cre, cim):
        j = pl.program_id(0)

        @pl.when(j % per == 0)
        def _():
            yl_ref[...] = ul_ref[...] * d_ref[...]
            yc_ref[...] = uc_ref[...] * d_ref[...]

        for d in range(2):
            rev = d == 1
            ar1, ai1 = ar_ref[d], ai_ref[d]

            def proj(u_ref, re, im, chunk, n):
                def f(r0):
                    u = u_ref[pl.ds(r0, chunk), :]
                    re[pl.ds(r0, chunk), :] = _dot(u, br_ref[d], _NN)
                    im[pl.ds(r0, chunk), :] = _dot(u, bi_ref[d], _NN)
                _rows_loop(n, chunk, f)

            def readout(y_ref, re, im, chunk, n):
                def f(r0):
                    y_ref[pl.ds(r0, chunk), :] += (_dot(re[pl.ds(r0, chunk), :], cr_ref[d], _NN)
                                                   - _dot(im[pl.ds(r0, chunk), :], ci_ref[d], _NN))
                _rows_loop(n, chunk, f)

            z1 = jnp.zeros((1, PB), F32)
            proj(uc_ref, cre, cim, RCC, C)
            _, _, fr, fi = _full_scan(cre, cim, C // 8, ar1, ai1, z1, z1, rev)
            readout(yc_ref, cre, cim, RCC, C)
            proj(ul_ref, lre, lim, RC, L)
            _full_scan(lre, lim, L // 8, ar1, ai1, fr, fi, rev)
            readout(yl_ref, lre, lim, RC, L)

    return pl.pallas_call(
        body, name=name, grid=(nsb,),
        in_specs=[ul_s, uc_s, av, av, bd, bd, cd, cd, dv],
        out_specs=[ul_s, uc_s],
        out_shape=[jax.ShapeDtypeStruct((L, dm.SSM_W), F32), jax.ShapeDtypeStruct((C, dm.SSM_W), F32)],
        scratch_shapes=[pltpu.VMEM((L, PB), F32), pltpu.VMEM((L, PB), F32),
                        pltpu.VMEM((C, PB), F32), pltpu.VMEM((C, PB), F32)],
        compiler_params=_cparams(("arbitrary",)),
    )(ul, uc, a_re, a_im, bd_re, bd_im, cd_re, cd_im, dskip)


def _s5_scan_bwd(ul, uc, dyl, dyc, a_re, a_im, bd_re, bd_im, cd_re, cd_im, bdT_re, bdT_im, cdT_re, cdT_im,
                 dskip, dm, name):
    PB, nsb, per, ul_s, uc_s, av, bd, cd, dv = _s5_specs(dm, STATE_BLOCK)
    L, C = dm.L, dm.C
    RC = _tile(L, 512, 8)
    RCC = _tile(C, 512, 8)
    bdT = pl.BlockSpec((2, PB, LANES), lambda j: (0, j, j // per))
    cdT = pl.BlockSpec((2, LANES, PB), lambda j: (0, j // per, j))
    dbd = pl.BlockSpec((2, 1, LANES, PB), lambda j: (0, j, 0, 0))
    dcd = pl.BlockSpec((2, 1, PB, LANES), lambda j: (0, j, 0, 0))

    def body(ul_ref, uc_ref, dyl_ref, dyc_ref, ar_ref, ai_ref, br_ref, bi_ref, cr_ref, ci_ref,
             brT_ref, biT_ref, crT_ref, ciT_ref, d_ref,
             dul_ref, duc_ref, dar_ref, dai_ref, dbr_ref, dbi_ref, dcr_ref, dci_ref, dd_ref,
             lre, lim, cre, cim, gre, gim, hre, him):
        j = pl.program_id(0)

        @pl.when(j % per == 0)
        def _():
            dul_ref[...] = dyl_ref[...] * d_ref[...]
            duc_ref[...] = dyc_ref[...] * d_ref[...]
            dd_ref[...] = (jnp.sum(dyl_ref[...] * ul_ref[...], axis=0, keepdims=True)
                           + jnp.sum(dyc_ref[...] * uc_ref[...], axis=0, keepdims=True))

        for d in range(2):
            rev = d == 1
            ar1, ai1 = ar_ref[d], ai_ref[d]
            z1 = jnp.zeros((1, PB), F32)

            def proj(u_ref, w_re, w_im, sign, re, im, chunk, n):
                def f(r0):
                    u = u_ref[pl.ds(r0, chunk), :]
                    re[pl.ds(r0, chunk), :] = _dot(u, w_re, _NN)
                    im[pl.ds(r0, chunk), :] = sign * _dot(u, w_im, _NN)
                _rows_loop(n, chunk, f)

            proj(uc_ref, br_ref[d], bi_ref[d], 1.0, cre, cim, RCC, C)
            icr, ici, fr, fi = _full_scan(cre, cim, C // 8, ar1, ai1, z1, z1, rev)
            proj(ul_ref, br_ref[d], bi_ref[d], 1.0, lre, lim, RC, L)
            ilr, ili, _, _ = _full_scan(lre, lim, L // 8, ar1, ai1, fr, fi, rev)
            proj(dyl_ref, crT_ref[d], ciT_ref[d], -1.0, gre, gim, RC, L)
            _, _, gfr, gfi = _full_scan(gre, gim, L // 8, ar1, -ai1, z1, z1, not rev)
            proj(dyc_ref, crT_ref[d], ciT_ref[d], -1.0, hre, him, RCC, C)
            _full_scan(hre, him, C // 8, ar1, -ai1, gfr, gfi, not rev)

            dar = jnp.zeros((8, PB), F32)
            dai = jnp.zeros((8, PB), F32)
            dbr = jnp.zeros((LANES, PB), F32)
            dbi = jnp.zeros((LANES, PB), F32)
            dcr = jnp.zeros((PB, LANES), F32)
            dci = jnp.zeros((PB, LANES), F32)
            for (u_ref, dy_ref, du_ref, sre, sim, are, aim, inr, ini, n, chunk) in (
                    (ul_ref, dyl_ref, dul_ref, lre, lim, gre, gim, ilr, ili, L, RC),
                    (uc_ref, dyc_ref, duc_ref, cre, cim, hre, him, icr, ici, C, RCC)):
                nst = n // 8

                def da_step(t, carry, sre=sre, sim=sim, are=are, aim=aim, nst=nst):
                    xr, xi = carry
                    tp = (t + 1) if rev else (t - 1)
                    o = pl.multiple_of(t * 8, 8)
                    op = pl.multiple_of(tp * 8, 8)
                    g_r, g_i = are[pl.ds(o, 8), :], aim[pl.ds(o, 8), :]
                    p_r, p_i = sre[pl.ds(op, 8), :], sim[pl.ds(op, 8), :]
                    return xr + g_r * p_r + g_i * p_i, xi + g_i * p_r - g_r * p_i

                lo_t, hi_t = (0, nst - 1) if rev else (1, nst)
                dar, dai = lax.fori_loop(lo_t, hi_t, da_step, (dar, dai))
                e0 = (nst - 1) * 8 if rev else 0
                g_r, g_i = are[pl.ds(e0, 8), :], aim[pl.ds(e0, 8), :]
                dar = dar + g_r * inr + g_i * ini
                dai = dai + g_i * inr - g_r * ini

                def mats(c, carry, u_ref=u_ref, dy_ref=dy_ref, du_ref=du_ref, sre=sre, sim=sim, are=are, aim=aim,
                         chunk=chunk):
                    xbr, xbi, xcr, xci = carry
                    r0 = pl.multiple_of(c * chunk, chunk)
                    u = u_ref[pl.ds(r0, chunk), :]
                    dy = dy_ref[pl.ds(r0, chunk), :]
                    g_r, g_i = are[pl.ds(r0, chunk), :], aim[pl.ds(r0, chunk), :]
                    du_ref[pl.ds(r0, chunk), :] += _dot(g_r, brT_ref[d], _NN) + _dot(g_i, biT_ref[d], _NN)
                    xbr = xbr + _dot(u, g_r, _TN)
                    xbi = xbi + _dot(u, g_i, _TN)
                    xcr = xcr + _dot(sre[pl.ds(r0, chunk), :], dy, _TN)
                    xci = xci - _dot(sim[pl.ds(r0, chunk), :], dy, _TN)
                    return xbr, xbi, xcr, xci

                dbr, dbi, dcr, dci = lax.fori_loop(0, n // chunk, mats, (dbr, dbi, dcr, dci))
            dar_ref[d] = jnp.sum(dar, axis=0, keepdims=True)
            dai_ref[d] = jnp.sum(dai, axis=0, keepdims=True)
            dbr_ref[d, 0] = dbr
            dbi_ref[d, 0] = dbi
            dcr_ref[d, 0] = dcr
            dci_ref[d, 0] = dci

    W, GP = dm.SSM_W, dm.GP
    return pl.pallas_call(
        body, name=name, grid=(nsb,),
        in_specs=[ul_s, uc_s, ul_s, uc_s, av, av, bd, bd, cd, cd, bdT, bdT, cdT, cdT, dv],
        out_specs=[ul_s, uc_s, av, av, dbd, dbd, dcd, dcd, dv],
        out_shape=[jax.ShapeDtypeStruct((L, W), F32), jax.ShapeDtypeStruct((C, W), F32),
                   jax.ShapeDtypeStruct((2, 1, GP), F32), jax.ShapeDtypeStruct((2, 1, GP), F32),
                   jax.ShapeDtypeStruct((2, nsb, LANES, PB), F32), jax.ShapeDtypeStruct((2, nsb, LANES, PB), F32),
                   jax.ShapeDtypeStruct((2, nsb, PB, LANES), F32), jax.ShapeDtypeStruct((2, nsb, PB, LANES), F32),
                   jax.ShapeDtypeStruct((1, W), F32)],
        scratch_shapes=[pltpu.VMEM((L, PB), F32), pltpu.VMEM((L, PB), F32),
                        pltpu.VMEM((C, PB), F32), pltpu.VMEM((C, PB), F32),
                        pltpu.VMEM((L, PB), F32), pltpu.VMEM((L, PB), F32),
                        pltpu.VMEM((C, PB), F32), pltpu.VMEM((C, PB), F32)],
        compiler_params=_cparams(("arbitrary",)),
    )(ul, uc, dyl, dyc, a_re, a_im, bd_re, bd_im, cd_re, cd_im, bdT_re, bdT_im, cdT_re, cdT_im, dskip)


def _glu(y, w, b, name):
    T, W = y.shape
    tm = _tile(T, 544, 16)

    def body(y_ref, w_ref, b_ref, o_ref):
        g = _gelu(y_ref[...])
        o_ref[...] = (g * _sigmoid(_dot(g, w_ref[...], _NN) + b_ref[...])).astype(BF16)

    return pl.pallas_call(
        body, name=name, grid=(T // tm,),
        in_specs=[pl.BlockSpec((tm, W), lambda i: (i, 0)), pl.BlockSpec((W, W), lambda i: (0, 0)),
                  pl.BlockSpec((1, W), lambda i: (0, 0))],
        out_specs=pl.BlockSpec((tm, W), lambda i: (i, 0)),
        out_shape=jax.ShapeDtypeStruct((T, W), BF16),
        compiler_params=_cparams(("parallel",)),
    )(y, w, b)


def _glu_bwd(y, w, b, do, do_col, name):
    T, W = y.shape
    tm = _tile(T, 544, 16)

    def body(y_ref, w_ref, b_ref, do_ref, dy_ref, dw_ref, db_ref):
        i = pl.program_id(0)

        @pl.when(i == 0)
        def _():
            dw_ref[...] = jnp.zeros_like(dw_ref)
            db_ref[...] = jnp.zeros_like(db_ref)

        y = y_ref[...]
        do = do_ref[...]
        g = _gelu(y)
        s = _sigmoid(_dot(g, w_ref[...], _NN) + b_ref[...])
        dpre = do * g * s * (1.0 - s)
        db_ref[...] += jnp.sum(dpre, axis=0, keepdims=True)
        dw_ref[...] += _dot(g, dpre, _TN)
        dg = do * s + _dot(dpre, w_ref[...], _NT)
        dy_ref[...] = dg * _gelu_grad(y)

    row = pl.BlockSpec((tm, W), lambda i: (i, 0))
    wsp = pl.BlockSpec((W, W), lambda i: (0, 0))
    vec = pl.BlockSpec((1, W), lambda i: (0, 0))
    return pl.pallas_call(
        body, name=name, grid=(T // tm,),
        in_specs=[row, wsp, vec, pl.BlockSpec((tm, W), lambda i: (i, do_col))], out_specs=[row, wsp, vec],
        out_shape=[jax.ShapeDtypeStruct((T, W), F32), jax.ShapeDtypeStruct((W, W), F32),
                   jax.ShapeDtypeStruct((1, W), F32)],
        compiler_params=_cparams(("arbitrary",)),
    )(y, w, b, do)


def _loss_head(X, g, target, C, name):
    T, D = X.shape
    tm = _tile(math.gcd(C, T - C), 512, 16)
    nc = C // tm

    def body(x_ref, g_ref, t_ref, loss_ref, dx_ref, dg_ref):
        i = pl.program_id(0)

        @pl.when(i == 0)
        def _():
            loss_ref[...] = jnp.zeros_like(loss_ref)
            dg_ref[...] = jnp.zeros_like(dg_ref)

        @pl.when(i < nc)
        def _():
            dx_ref[...] = jnp.zeros_like(dx_ref)

        @pl.when(i >= nc)
        def _():
            x = x_ref[...]
            gv = g_ref[...]
            r = lax.rsqrt(jnp.mean(x * x, axis=-1, keepdims=True) + NORM_EPS)
            xh = x * r
            err = xh * gv - t_ref[...]
            loss_ref[...] += 0.5 * jnp.sum(jnp.mean(err * err, axis=-1, keepdims=True), axis=0, keepdims=True)
            dy = err * (1.0 / D)
            dg_ref[...] += jnp.sum(dy * xh, axis=0, keepdims=True)
            dxh = dy * gv
            dx_ref[...] = r * (dxh - xh * jnp.mean(dxh * xh, axis=-1, keepdims=True))

    row = pl.BlockSpec((tm, D), lambda i: (i, 0))
    return pl.pallas_call(
        body, name=name, grid=(T // tm,),
        in_specs=[row, pl.BlockSpec((1, D), lambda i: (0, 0)),
                  pl.BlockSpec((tm, D), lambda i: (jnp.maximum(i - nc, 0), 0))],
        out_specs=[pl.BlockSpec((1, 1), lambda i: (0, 0)), row, pl.BlockSpec((1, D), lambda i: (0, 0))],
        out_shape=[jax.ShapeDtypeStruct((1, 1), F32), jax.ShapeDtypeStruct((T, D), F32),
                   jax.ShapeDtypeStruct((1, D), F32)],
        compiler_params=_cparams(("arbitrary",)),
    )(X, g, target)


def _ada_fwd(cond, w, b, name):
    nl, D, Ns = w.shape
    tn = _tile(Ns, 1024, LANES)

    def body(c_ref, w_ref, b_ref, o_ref):
        c = c_ref[...]
        o_ref[0] = _dot(c * _sigmoid(c), w_ref[0], _NN) + b_ref[0]

    return pl.pallas_call(
        body, name=name, grid=(nl, Ns // tn),
        in_specs=[pl.BlockSpec((16, D), lambda l, j: (0, 0)), pl.BlockSpec((1, D, tn), lambda l, j: (l, 0, j)),
                  pl.BlockSpec((1, 1, tn), lambda l, j: (l, 0, j))],
        out_specs=pl.BlockSpec((1, 16, tn), lambda l, j: (l, 0, j)),
        out_shape=jax.ShapeDtypeStruct((nl, 16, Ns), F32),
        compiler_params=_cparams(("parallel", "parallel")),
    )(cond, w, b)


def _ada_bwd(cond, w, dmod, name):
    nl, D, Ns = w.shape
    tn = _tile(Ns, 1024, LANES)

    def body(c_ref, w_ref, dm_ref, dw_ref, da_ref):
        j = pl.program_id(1)

        @pl.when(j == 0)
        def _():
            da_ref[...] = jnp.zeros_like(da_ref)

        c = c_ref[...]
        dw_ref[0] = _dot(c * _sigmoid(c), dm_ref[0], _TN)
        da_ref[0] += _dot(dm_ref[0], w_ref[0], _NT)

    return pl.pallas_call(
        body, name=name, grid=(nl, Ns // tn),
        in_specs=[pl.BlockSpec((16, D), lambda l, j: (0, 0)), pl.BlockSpec((1, D, tn), lambda l, j: (l, 0, j)),
                  pl.BlockSpec((1, 16, tn), lambda l, j: (l, 0, j))],
        out_specs=[pl.BlockSpec((1, D, tn), lambda l, j: (l, 0, j)), pl.BlockSpec((1, 16, D), lambda l, j: (l, 0, 0))],
        out_shape=[jax.ShapeDtypeStruct((nl, D, Ns), F32), jax.ShapeDtypeStruct((nl, 16, D), F32)],
        compiler_params=_cparams(("parallel", "arbitrary")),
    )(cond, w, dmod)


def _cctx_grad(c_ctx, dact4, name):
    nch, nl, _, D = dact4.shape

    def body(c_ref, da_ref, o_ref):
        acc = jnp.zeros((1, D), F32)
        for ch in range(nch):
            for l in range(nl):
                acc = acc + da_ref[ch, l, 8:9, :]
        c = c_ref[...]
        s = _sigmoid(c)
        o_ref[...] = acc * (s + c * s * (1.0 - s))

    return pl.pallas_call(body, name=name, out_shape=jax.ShapeDtypeStruct((1, D), F32),
                          compiler_params=_cparams())(c_ctx, dact4)


def _adamw(w, g, m, v, name):
    R, W = w.shape
    tr = _tile(R, max(16, (1 << 19) // W // 16 * 16), 16 if g.dtype == BF16 else 8)
    c1 = 1.0 - ADAM_B1 ** ADAM_STEP
    c2 = 1.0 - ADAM_B2 ** ADAM_STEP

    def body(w_ref, g_ref, m_ref, v_ref, d_ref, nm_ref, nv_ref, g32_ref):
        g = g_ref[...].astype(F32)
        g32_ref[...] = g
        m = ADAM_B1 * m_ref[...] + (1.0 - ADAM_B1) * g
        v = ADAM_B2 * v_ref[...] + (1.0 - ADAM_B2) * (g * g)
        nm_ref[...] = m
        nv_ref[...] = v
        d_ref[...] = -ADAM_LR * ((m / c1) / (jnp.sqrt(v / c2) + ADAM_EPS) + ADAM_WD * w_ref[...])

    blk = pl.BlockSpec((tr, W), lambda i: (i, 0))
    s = jax.ShapeDtypeStruct((R, W), F32)
    return pl.pallas_call(
        body, name=name, grid=(R // tr,), in_specs=[blk] * 4, out_specs=[blk] * 4, out_shape=[s, s, s, s],
        compiler_params=_cparams(("parallel",)),
    )(w, g, m, v)


def _adamw_nd(w, g, m, v, name):
    shp = w.shape
    two = (math.prod(shp[:-1]), shp[-1])
    outs = _adamw(w.reshape(two), g.reshape(two), m.reshape(two), v.reshape(two), name)
    return [o.reshape(shp) for o in outs]


def _interleave(a, n):
    return a.reshape(SCAN_SEGMENTS, n // SCAN_SEGMENTS, -1).transpose(1, 0, 2).reshape(n, -1)


def _deinterleave(a, n):
    return a.reshape(n // SCAN_SEGMENTS, SCAN_SEGMENTS, -1).transpose(1, 0, 2).reshape(n, -1)


def _blockdiag_in(bbT, dm):
    eye = jnp.eye(dm.SG, dtype=F32)
    b4 = bbT.reshape(2, SSM_GROUP, dm.SG, SSM_STATE)
    return jnp.einsum("dhgp,gk->dghkp", b4, eye).reshape(2, dm.SSM_W, dm.GP)


def _blockdiag_in_T(dbd, dm):
    gpb = STATE_BLOCK // SSM_STATE
    per = (LANES // SSM_GROUP) // gpb
    d8 = dbd.reshape(2, dm.SSM_W // LANES, per, per, gpb, SSM_GROUP, gpb, SSM_STATE)
    x = jnp.einsum("duaabhbp->duabhp", d8).reshape(2, dm.SG, SSM_GROUP, SSM_STATE)
    return x.transpose(0, 2, 1, 3).reshape(2, SSM_GROUP, dm.GP)


def _blockdiag_out(c, dm):
    eye = jnp.eye(dm.SG, dtype=F32)
    return jnp.einsum("dghp,gk->dgpkh", c, eye).reshape(2, dm.GP, dm.SSM_W)


def _blockdiag_out_T(dcd, dm):
    gpb = STATE_BLOCK // SSM_STATE
    per = (LANES // SSM_GROUP) // gpb
    d8 = dcd.reshape(2, dm.SSM_W // LANES, per, gpb, SSM_STATE, per, gpb, SSM_GROUP)
    return jnp.einsum("duabpabh->duabhp", d8).reshape(2, dm.SG, SSM_GROUP, SSM_STATE)


_BIG = ("w_in", "w_out", "ssm_w_glu", "ffn_w_up", "ffn_w_down")
_SHARD_AXIS = {"w_in": 2, "w_out": 1, "ssm_w_glu": 1, "ffn_w_up": 2, "ffn_w_down": 1}
_SMALL = ("g_mix", "g_ffn", "attn_sink", "ssm_lambda_re", "ssm_lambda_im", "ssm_log_dt", "ssm_b_re", "ssm_b_im",
          "ssm_c_re", "ssm_c_im", "ssm_d", "ssm_b_glu", "gmlp_ln_g", "gmlp_ln_b", "gmlp_w_s", "gmlp_b_s",
          "ffn_conv_b", "g_final", "ffn_conv_w")
_WEIGHTS = ("c_ctx", "w_ada", "b_ada", "g_mix", "g_ffn", "w_in", "w_out", "attn_sink", "ssm_lambda_re",
            "ssm_lambda_im", "ssm_log_dt", "ssm_b_re", "ssm_b_im", "ssm_c_re", "ssm_c_im", "ssm_d", "ssm_w_glu",
            "ssm_b_glu", "gmlp_ln_g", "gmlp_ln_b", "gmlp_w_s", "gmlp_b_s", "ffn_w_up", "ffn_conv_w", "ffn_conv_b",
            "ffn_w_down", "g_final")


def _step(P, M, V, x, c, ctx, loss_target):
    dm = _Dims()
    D, T, C, L = dm.D, dm.T, dm.C, dm.L
    mx, my, mc = lax.axis_index("x"), lax.axis_index("y"), lax.axis_index("c")
    chip = 2 * mx + my
    dev = 2 * chip + mc

    conv_sh = P["ffn_conv_w"].shape
    small_f = _pack([c, P["ffn_conv_w"]], F32, 8)
    small_all = _allgather8(small_f, "ag_small_fwd")
    c_all, conv_all = [], []
    for d in range(N_DEV):
        cd_, cw_ = _unpack(small_all[d], [(1, D), conv_sh])
        c_all.append(cd_)
        conv_all.append(cw_)
    conv_w = jnp.concatenate([conv_all[2 * j] for j in range(N_CHIPS)], axis=2)
    cond = jnp.concatenate(c_all + [P["c_ctx"].reshape(1, D), jnp.zeros((16 - N_DEV - 1, D), F32)], axis=0)

    n_sh = P["w_ada"].shape[2]
    b_sh = lax.dynamic_slice_in_dim(P["b_ada"], chip * n_sh, n_sh, axis=1)[:, None, :]
    mods_sh = _ada_fwd(cond, P["w_ada"], b_sh, "ada_fwd")
    mods4 = _comm(mods_sh, group="chips", by_peer=False, name="ag_mods")
    mods = jnp.concatenate([mods4[j] for j in range(N_CHIPS)], axis=-1)
    mod_lat = lax.dynamic_index_in_dim(mods, dev, axis=1, keepdims=False)
    mod_ctx = mods[:, N_DEV]
    modp = jnp.stack([mod_ctx, mod_lat], axis=1).reshape(DEPTH, 2, N_MOD, D)

    wb = {n: P[n].astype(BF16) for n in _BIG}

    def layer_weights(g):
        return dict(w_in=jnp.concatenate([g["w_in"][j] for j in range(N_CHIPS)], axis=1),
                    w_glu=g["ssm_w_glu"].reshape(dm.SSM_W, dm.SSM_W),
                    w_out=_Mat(g["w_out"], axis=0), w_dn=_Mat(g["ffn_w_down"], axis=0),
                    w_up=_Mat(g["ffn_w_up"], axis=1))

    g0 = _comm_alone("chips_ag", [wb[n][0] for n in _BIG], "ag_w_chips")
    Wl = [layer_weights(dict(zip(_BIG, g0)))]
    cos, sin = _rope_tables(dm)

    X = jnp.concatenate([ctx.reshape(C, D), x.reshape(L, D)], axis=0)
    saved = []
    for l in range(DEPTH):
        mp = modp[l]
        sh_a, sc_a, gt_a, sh_f, sc_f, gt_f = [mp[:, k] for k in range(N_MOD)]
        w_in, w_out, w_glu, w_up, w_dn = [Wl[l][k] for k in ("w_in", "w_out", "w_glu", "w_up", "w_dn")]
        g_mix, g_ffn = P["g_mix"][l][None], P["g_ffn"][l][None]

        h = _normmod(X, g_mix, sh_a, sc_a, C, "normmod_a")
        nxt = {}

        def riding(*names):
            return ("chips_ag", [wb[n][l + 1] for n in names]) if l + 1 < DEPTH else None

        def landed(names, gathered):
            nxt.update(zip(names, gathered) if l + 1 < DEPTH else ())

        z = _matmul(h, w_in, mode="nn", name="mm_in", carry=riding("w_in"))
        if l + 1 < DEPTH:
            z, got = z
            landed(("w_in",), got)
        q, k2, v2 = _attn_prep(z, cos, sin, dm, "attn_prep")
        sinkcol = jnp.repeat(P["attn_sink"][l].reshape(2, GQA_RATIO), ATTN_BLOCK, axis=1)[..., None]
        o_attn = _attention(q, k2, v2, sinkcol, dm, "attention")
        u = z[:, dm.OFF_S:dm.OFF_GU]
        ul, uc = _interleave(u[C:], L), _interleave(u[:C], C)
        lam_re = P["ssm_lambda_re"][l].reshape(2, 1, dm.GP)
        lam_im = P["ssm_lambda_im"][l].reshape(2, 1, dm.GP)
        ldt = jnp.repeat(P["ssm_log_dt"][l], SSM_STATE, axis=-1).reshape(2, 1, dm.GP)
        bT_re = P["ssm_b_re"][l].reshape(2, dm.GP, SSM_GROUP).transpose(0, 2, 1)
        bT_im = P["ssm_b_im"][l].reshape(2, dm.GP, SSM_GROUP).transpose(0, 2, 1)
        a_re, a_im, bbT_re, bbT_im = _s5_disc(lam_re, lam_im, ldt, bT_re, bT_im, "s5_disc")
        bd_re, bd_im = _blockdiag_in(bbT_re, dm).astype(BF16), _blockdiag_in(bbT_im, dm).astype(BF16)
        cd_re = _blockdiag_out(P["ssm_c_re"][l], dm).astype(BF16)
        cd_im = _blockdiag_out(P["ssm_c_im"][l], dm).astype(BF16)
        dskip = P["ssm_d"][l][None]
        yl, yc = _s5_scan(ul, uc, a_re, a_im, bd_re, bd_im, cd_re, cd_im, dskip, dm, "s5_scan")
        y = jnp.concatenate([_deinterleave(yc, C), _deinterleave(yl, L)], axis=0)
        b_glu = P["ssm_b_glu"][l][None]
        o_ssm = _glu(y, w_glu, b_glu, "glu")
        ln_g, ln_b = P["gmlp_ln_g"][l][None], P["gmlp_ln_b"][l][None]
        w_s = P["gmlp_w_s"][l].astype(BF16)
        b_s = P["gmlp_b_s"][l][..., None]
        o_gmlp = _gmlp(z, ln_g, ln_b, w_s, b_s, dm, "gmlp")
        o_cat = jnp.concatenate([o_attn, o_ssm, o_gmlp], axis=-1)
        mix, X1, *got = _matmul_residual(o_cat, w_out, X, gt_a, C, "mm_out", carry=riding("w_out", "ssm_w_glu"))
        landed(("w_out", "ssm_w_glu"), got[0] if got else ())
        h2 = _normmod(X1, g_ffn, sh_f, sc_f, C, "normmod_f")
        up = _matmul(h2, w_up, mode="nn", name="mm_up", out_dtype=BF16, out_split=(1, 2),
                     carry=riding("ffn_w_up"))
        if l + 1 < DEPTH:
            up, got = up
            landed(("ffn_w_up",), got)
        cw, cb = conv_w[l], P["ffn_conv_b"][l][None]
        act = _convact(up, cw, cb, C, "convact")
        ffn, X2, *got = _matmul_residual(act, w_dn, X1, gt_f, C, "mm_down", carry=riding("ffn_w_down"))
        landed(("ffn_w_down",), got[0] if got else ())
        if l + 1 < DEPTH:
            Wl.append(layer_weights(nxt))
        saved.append(dict(X=X, h=h, z=z, q=q, k2=k2, v2=v2, sinkcol=sinkcol, ul=ul, uc=uc, a_re=a_re, a_im=a_im,
                          bd_re=bd_re, bd_im=bd_im, cd_re=cd_re, cd_im=cd_im, y=y, o_cat=o_cat, mix=mix, X1=X1, h2=h2,
                          up=up, act=act, ffn=ffn, lam_re=lam_re, lam_im=lam_im, ldt=ldt, bT_re=bT_re,
                          bT_im=bT_im))
        X = X2

    loss_l, dX, dg_final = _loss_head(X, P["g_final"][None], loss_target.reshape(L, D), C, "loss_head")
    loss = lax.psum(loss_l[0, 0], ("x", "y", "c"))

    G = {n: [None] * DEPTH for n in _WEIGHTS if n not in ("c_ctx", "w_ada", "b_ada", "g_final")}
    dmod = [None] * DEPTH
    Gred = {n: [None] * DEPTH for n in _BIG}
    ready = None

    def hosted(fn, store, kind, src, names):
        if src is None:
            return fn(None)
        out, got = fn((kind, [src[n] for n in names]))
        store.update(zip(names, got))
        return out

    def reduce_parts(parts):
        return {n: _sum_slots(parts[n], BF16, "sum_all") for n in _BIG}

    def file_shared(shared, layer):
        for n in _BIG:
            Gred[n][layer] = shared[n].reshape(P[n].shape[1:])

    for l in reversed(range(DEPTH)):
        S = saved[l]
        parts, shared = {}, {}
        mp = modp[l]
        sh_a, sc_a, gt_a, sh_f, sc_f, gt_f = [mp[:, k] for k in range(N_MOD)]
        w_in, w_out, w_glu, w_up, w_dn = [Wl[l][k] for k in ("w_in", "w_out", "w_glu", "w_up", "w_dn")]
        g_mix, g_ffn = P["g_mix"][l][None], P["g_ffn"][l][None]
        cw, cb = conv_w[l], P["ffn_conv_b"][l][None]

        dffn, dgt_f = _gate_bwd(dX, S["ffn"], gt_f, C, "gate_bwd")
        dact = hosted(lambda c: _matmul(dffn, w_dn, mode="nt", name="mm_down_dx", out_dtype=BF16, carry=c),
                      parts, "all_rs", ready, ("w_in",))
        G["ffn_w_down"][l] = hosted(
            lambda c: _matmul(S["act"], dffn, mode="tn", name="mm_down_dw", out_dtype=BF16, out_split=(0, N_CHIPS),
                              tm_cap=1408, tk_cap=2176, carry=c), parts, "all_rs", ready, ("w_out", "ssm_w_glu"))
        dup, dcw, dcb = _convact_bwd(S["up"], cw, cb, dact, C, "convact_bwd")
        G["ffn_conv_w"][l], G["ffn_conv_b"][l] = dcw, dcb[0]
        dh2 = hosted(lambda c: _matmul(_Mat(dup, axis=1), w_up, mode="nt", name="mm_up_dx", tk_cap=2816, carry=c),
                     parts, "all_rs", ready, ("ffn_w_down",))
        G["ffn_w_up"][l] = hosted(
            lambda c: _matmul(S["h2"], _Mat(dup, axis=1), mode="tn", name="mm_up_dw", out_dtype=BF16,
                              out_split=(1, N_CHIPS), tk_cap=2176, carry=c), parts, "all_rs", ready, ("ffn_w_up",))
        red = reduce_parts(parts) if ready is not None else None
        dX1, dg_f, dsh_f, dsc_f = _normmod_bwd(S["X1"], g_ffn, sc_f, dh2, dX, C, "normmod_bwd")
        G["g_ffn"][l] = dg_f[0]
        dmix, dgt_a = _gate_bwd(dX1, S["mix"], gt_a, C, "gate_bwd")
        do = hosted(lambda c: _matmul(dmix, w_out, mode="nt", name="mm_out_dx", carry=c),
                    shared, "sib_ag", red, ("w_in", "w_out", "ssm_w_glu"))
        G["w_out"][l] = hosted(
            lambda c: _matmul(S["o_cat"], dmix, mode="tn", name="mm_out_dw", out_dtype=BF16, out_split=(0, N_CHIPS),
                              tk_cap=2176, carry=c), shared, "sib_ag", red, ("ffn_w_down",))
        do_attn = do_ssm = do_gmlp = do
        ln_g, ln_b = P["gmlp_ln_g"][l][None], P["gmlp_ln_b"][l][None]
        w_s = P["gmlp_w_s"][l].astype(BF16)
        b_s = P["gmlp_b_s"][l][..., None]
        dz_g, dlg, dlb, dws, dbs = _gmlp_bwd(S["z"], ln_g, ln_b, w_s, b_s, do_gmlp, dm, "gmlp_bwd")
        G["gmlp_ln_g"][l], G["gmlp_ln_b"][l], G["gmlp_w_s"][l], G["gmlp_b_s"][l] = dlg[0], dlb[0], dws, dbs[..., 0]
        b_glu = P["ssm_b_glu"][l][None]
        dy, dwglu, dbglu = _glu_bwd(S["y"], w_glu, b_glu, do_ssm, dm.ATTN_W // dm.SSM_W, "glu_bwd")
        G["ssm_w_glu"][l], G["ssm_b_glu"][l] = dwglu.astype(BF16).reshape(N_CHIPS, -1, dm.SSM_W), dbglu[0]
        dyl, dyc = _interleave(dy[C:], L), _interleave(dy[:C], C)
        tr = lambda a: jnp.swapaxes(a, 1, 2)
        dskip = P["ssm_d"][l][None]
        (dul, duc, da_re, da_im, dbd_re, dbd_im, dcd_re, dcd_im, dd) = _s5_scan_bwd(
            S["ul"], S["uc"], dyl, dyc, S["a_re"], S["a_im"], S["bd_re"], S["bd_im"], S["cd_re"], S["cd_im"],
            tr(S["bd_re"]), tr(S["bd_im"]), tr(S["cd_re"]), tr(S["cd_im"]), dskip, dm, "s5_scan_bwd")
        dz_s = jnp.concatenate([_deinterleave(duc, C), _deinterleave(dul, L)], axis=0)
        dlr, dli, dldt, dbTr, dbTi = _s5_disc_bwd(S["lam_re"], S["lam_im"], S["ldt"], S["bT_re"], S["bT_im"],
                                                  da_re, da_im, _blockdiag_in_T(dbd_re, dm),
                                                  _blockdiag_in_T(dbd_im, dm), "s5_disc_bwd")
        G["ssm_lambda_re"][l] = dlr.reshape(2, dm.SG, SSM_STATE)
        G["ssm_lambda_im"][l] = dli.reshape(2, dm.SG, SSM_STATE)
        G["ssm_log_dt"][l] = jnp.sum(dldt.reshape(2, dm.SG, SSM_STATE), axis=-1)
        G["ssm_b_re"][l] = dbTr.transpose(0, 2, 1).reshape(2, dm.SG, SSM_STATE, SSM_GROUP)
        G["ssm_b_im"][l] = dbTi.transpose(0, 2, 1).reshape(2, dm.SG, SSM_STATE, SSM_GROUP)
        G["ssm_c_re"][l] = _blockdiag_out_T(dcd_re, dm)
        G["ssm_c_im"][l] = _blockdiag_out_T(dcd_im, dm)
        G["ssm_d"][l] = dd[0]
        dq, pk, pv, dkc, dvc, dsk = _attention_bwd(S["q"], S["k2"], S["v2"], S["sinkcol"], do_attn, dm, "attention_bwd")
        G["attn_sink"][l] = dsk[:, :, 0].reshape(dm.NH)
        dk2 = _kv_reduce(pk, dkc, dm, "kv_reduce")
        dv2 = _kv_reduce(pv, dvc, dm, "kv_reduce")
        dz_a = _attn_prep_bwd(dq, dk2, dv2, cos, sin, dm, "attn_prep_bwd")
        dz = jnp.concatenate([dz_a, dz_s, dz_g], axis=-1).astype(BF16)
        dh = hosted(lambda c: _matmul(dz, w_in, mode="nt", name="mm_in_dx", carry=c),
                    shared, "sib_ag", red, ("ffn_w_up",))
        dw_in = _matmul(S["h"], dz, mode="tn", name="mm_in_dw", out_dtype=BF16, tk_cap=2176)
        n_in = dm.N_IN // N_CHIPS
        G["w_in"][l] = jnp.stack([dw_in[:, j * n_in:(j + 1) * n_in] for j in range(N_CHIPS)])
        if ready is not None:
            file_shared(shared, l + 1)
        ready = {n: G[n][l] for n in _BIG}
        dX, dg_a, dsh_a, dsc_a = _normmod_bwd(S["X"], g_mix, sc_a, dh, dX1, C, "normmod_bwd")
        G["g_mix"][l] = dg_a[0]
        dmod[l] = jnp.stack([dsh_a, dsc_a, dgt_a, dsh_f, dsc_f, dgt_f], axis=1)

    grad_x = dX[C:].reshape(1, L, D)
    for n in _BIG:
        G.pop(n)
    parts = dict(zip(_BIG, _comm_alone("all_rs", [ready[n] for n in _BIG], "rs_all")))
    red = reduce_parts(parts)
    file_shared(dict(zip(_BIG, _comm_alone("sib_ag", [red[n] for n in _BIG], "rs_share"))), 0)
    G = {n: jnp.stack(v) for n, v in G.items()}
    G["g_final"] = dg_final[0]
    dmod = jnp.stack(dmod)

    small_shapes = [G[n].shape for n in _SMALL]
    gs_pack = _pack([G[n] for n in _SMALL], F32, 16 * N_DEV)
    r8 = gs_pack.shape[0] // N_DEV
    gs_parts = _comm(gs_pack.reshape(N_DEV, r8, PACK_W), group="all", by_peer=True, name="rs_small")
    gs_sum = _allgather8(_sum_slots(gs_parts, F32, "sum_small"), "ag_small_red").reshape(-1, PACK_W)
    for n, a in zip(_SMALL, _unpack(gs_sum, small_shapes)):
        G[n] = a
    allp = _allgather8(_pack([dmod], F32, 16), "ag_dmod")
    dmod_all = allp.reshape(N_DEV, -1)[:, :DEPTH * 2 * N_MOD * D]
    dmod_all = dmod_all.reshape(N_DEV, DEPTH, 2, N_MOD * D)
    dmod_ctx = _sum_slots(dmod_all[:, :, 0].reshape(N_DEV, DEPTH, N_MOD * D), F32, "sum_dmod_ctx")
    dmod16 = jnp.concatenate([jnp.swapaxes(dmod_all[:, :, 1], 0, 1), dmod_ctx[:, None],
                              jnp.zeros((DEPTH, 16 - N_DEV - 1, N_MOD * D), F32)], axis=1)
    G["b_ada"] = _sum_slots(jnp.swapaxes(dmod16, 0, 1)[:N_DEV + 1], F32, "sum_b_ada")
    dmod16_sh = lax.dynamic_slice_in_dim(dmod16, chip * n_sh, n_sh, axis=2)
    G["w_ada"], dact = _ada_bwd(cond, P["w_ada"], dmod16_sh, "ada_bwd")
    dact4 = _comm(dact, group="chips", by_peer=False, name="ag_dact")
    G["c_ctx"] = _cctx_grad(P["c_ctx"].reshape(1, D), dact4, "cctx_grad")[0]
    conv_cols = conv_sh[2]
    G["ffn_conv_w"] = lax.dynamic_slice_in_dim(G["ffn_conv_w"], chip * conv_cols, conv_cols, axis=2)

    for n in _BIG:
        G[n] = jnp.stack(Gred[n])

    delta, new_m, new_v = {}, {}, {}
    for n in _BIG + ("w_ada",):
        delta[n], new_m[n], new_v[n], G[n] = _adamw_nd(P[n], G[n], M[n], V[n], "adamw_" + n)
    rest = [n for n in _WEIGHTS if n not in _BIG + ("w_ada",)]
    shapes = [P[n].shape for n in rest]
    packed = [_pack([d[n] for n in rest], F32, 8) for d in (P, G, M, V)]
    outs = _adamw(*packed, "adamw_small")[:3]
    for dst, buf in zip((delta, new_m, new_v), outs):
        for n, a in zip(rest, _unpack(buf, shapes)):
            dst[n] = a
    return loss, grad_x, G, delta, new_m, new_v


def kernel(x, c, ctx, c_ctx, w_ada, b_ada, g_mix, g_ffn, w_in, w_out, attn_sink, ssm_lambda_re, ssm_lambda_im, ssm_log_dt, ssm_b_re, ssm_b_im, ssm_c_re, ssm_c_im, ssm_d, ssm_w_glu, ssm_b_glu, gmlp_ln_g, gmlp_ln_b, gmlp_w_s, gmlp_b_s, ffn_w_up, ffn_conv_w, ffn_conv_b, ffn_w_down, g_final, loss_target, m_c_ctx, m_w_ada, m_b_ada, m_g_mix, m_g_ffn, m_w_in, m_w_out, m_attn_sink, m_ssm_lambda_re, m_ssm_lambda_im, m_ssm_log_dt, m_ssm_b_re, m_ssm_b_im, m_ssm_c_re, m_ssm_c_im, m_ssm_d, m_ssm_w_glu, m_ssm_b_glu, m_gmlp_ln_g, m_gmlp_ln_b, m_gmlp_w_s, m_gmlp_b_s, m_ffn_w_up, m_ffn_conv_w, m_ffn_conv_b, m_ffn_w_down, m_g_final, v_c_ctx, v_w_ada, v_b_ada, v_g_mix, v_g_ffn, v_w_in, v_w_out, v_attn_sink, v_ssm_lambda_re, v_ssm_lambda_im, v_ssm_log_dt, v_ssm_b_re, v_ssm_b_im, v_ssm_c_re, v_ssm_c_im, v_ssm_d, v_ssm_w_glu, v_ssm_b_glu, v_gmlp_ln_g, v_gmlp_ln_b, v_gmlp_w_s, v_gmlp_b_s, v_ffn_w_up, v_ffn_conv_w, v_ffn_conv_b, v_ffn_w_down, v_g_final):
    env = locals()
    P = {n: env[n] for n in _WEIGHTS}
    M = {n: env["m_" + n] for n in _WEIGHTS}
    V = {n: env["v_" + n] for n in _WEIGHTS}
    loss, grad_x, G, delta, new_m, new_v = _step(P, M, V, x, c, ctx, loss_target)
    return (loss, grad_x, *[G[n] for n in _WEIGHTS], *[delta[n] for n in _WEIGHTS],
            *[new_m[n] for n in _WEIGHTS], *[new_v[n] for n in _WEIGHTS])
```

```python
import functools
import math

import jax
import jax.numpy as jnp
from jax import lax
from jax.experimental import pallas as pl
from jax.experimental.pallas import tpu as pltpu

F32 = jnp.float32
BF16 = jnp.bfloat16

D_MODEL = 2048
SEQ = 4096
DEPTH = 4
CTX_LEN = 256
GRID_W = 64
HEAD_DIM = 64
GQA_RATIO = 8
WINDOW = 128
ATTN_BLOCK = 128
ROPE_BASE = 10000.0
SSM_GROUP = 16
SSM_STATE = 64
GMLP_CHUNK = 128
GMLP_GROUP_W = 128
N_MOD = 6
NORM_EPS = 1e-6
ADAM_LR = 0.001
ADAM_B1 = 0.9
ADAM_B2 = 0.999
ADAM_EPS = 1e-08
ADAM_WD = 0.01
ADAM_STEP = 10

N_CHIPS = 4
N_DEV = 8
SCAN_SEGMENTS = 8
STATE_BLOCK = 256
STATE_BLOCK_FWD = 512
LANES = 128
PACK_W = 1024
COMM_CHUNKS = (16, 8, 4, 2)
SUM_BLOCK_BYTES = 8 << 20
VMEM_LIMIT_MB = 56
_GELU_K = math.sqrt(2.0 / math.pi)
_GELU_C = 0.044715


class _Dims:
    def __init__(self):
        self.D = D_MODEL
        self.L = SEQ
        self.C = CTX_LEN
        self.T = SEQ + CTX_LEN
        self.ATTN_W = D_MODEL // 2
        self.SSM_W = D_MODEL // 4
        self.GMLP_W = D_MODEL - self.ATTN_W - self.SSM_W
        self.NH = self.ATTN_W // HEAD_DIM
        self.NKV = self.NH // GQA_RATIO
        self.KV_W = self.NKV * HEAD_DIM
        self.SG = self.SSM_W // SSM_GROUP
        self.GP = self.SG * SSM_STATE
        self.GG = self.GMLP_W // GMLP_GROUP_W
        self.DFF = ((8 * D_MODEL // 3 + 255) // 256) * 256
        self.OFF_K = self.ATTN_W
        self.OFF_V = self.OFF_K + self.KV_W
        self.OFF_S = self.OFF_V + self.KV_W
        self.OFF_GU = self.OFF_S + self.SSM_W
        self.OFF_GV = self.OFF_GU + self.GMLP_W
        self.N_IN = self.OFF_GV + self.GMLP_W
        assert self.NKV == 2 and self.KV_W == LANES
        assert self.C % (SCAN_SEGMENTS * 8) == 0 and self.L % (SCAN_SEGMENTS * 8) == 0
        assert self.C % ATTN_BLOCK == 0 and self.L % ATTN_BLOCK == 0


def _tile(n, cap, mult):
    best = None
    for t in range(mult, min(n, cap) + 1, mult):
        if n % t == 0:
            best = t
    return best if best is not None else n


def _cparams(sem=None):
    kw = dict(vmem_limit_bytes=VMEM_LIMIT_MB << 20)
    if sem is not None:
        kw["dimension_semantics"] = sem
    return pltpu.CompilerParams(**kw)


def _gelu(x):
    return 0.5 * x * (1.0 + jnp.tanh(_GELU_K * (x + _GELU_C * x * x * x)))


def _gelu_grad(x):
    t = jnp.tanh(_GELU_K * (x + _GELU_C * x * x * x))
    return 0.5 * (1.0 + t) + 0.5 * x * (1.0 - t * t) * _GELU_K * (1.0 + 3.0 * _GELU_C * x * x)


def _sigmoid(x):
    return 1.0 / (1.0 + jnp.exp(-x))


def _dot(a, b, dims):
    return lax.dot_general(a.astype(BF16), b.astype(BF16), (dims, ((), ())), preferred_element_type=F32)


_NN = ((1,), (0,))
_NT = ((1,), (1,))
_TN = ((0,), (0,))


def _comm(x, *, group, by_peer, name):
    n = {"chips": N_CHIPS, "all": N_DEV}[group]
    blk = x.shape[1:] if by_peer else x.shape
    npeer = n - 1
    W = blk[-1]
    rows = math.prod(blk[:-1])
    x2 = x.reshape((n, rows, W) if by_peer else (rows, W))
    nchunk = next((k for k in COMM_CHUNKS if rows % (k * 16) == 0), 1)
    cr = rows // nchunk

    def body(x_ref, o_ref, send_sems, recv_sems, local_sem):
        mx, my, mc = lax.axis_index("x"), lax.axis_index("y"), lax.axis_index("c")
        if group == "chips":
            me = 2 * mx + my
            peers = [((mx ^ fx, my ^ fy, mc), 2 * (mx ^ fx) + (my ^ fy)) for fx, fy in ((1, 0), (0, 1), (1, 1))]
        else:
            me = 4 * mx + 2 * my + mc
            peers = [((mx ^ fx, my ^ fy, mc ^ fc), 4 * (mx ^ fx) + 2 * (my ^ fy) + (mc ^ fc))
                     for fx in (0, 1) for fy in (0, 1) for fc in (0, 1) if fx + fy + fc]

        def src_of(pid):
            return x_ref.at[pid] if by_peer else x_ref

        def chunk(ref, ch):
            return ref.at[pl.ds(ch * cr, cr), :]

        def remote(k, dev, pid, ch=None):
            s, d = src_of(pid), o_ref.at[me]
            if ch is not None:
                s, d = chunk(s, ch), chunk(d, ch)
            return pltpu.make_async_remote_copy(src_ref=s, dst_ref=d, send_sem=send_sems.at[k],
                                                recv_sem=recv_sems.at[k], device_id=dev,
                                                device_id_type=pl.DeviceIdType.MESH)

        for ch in range(nchunk):
            pltpu.make_async_copy(chunk(src_of(me), ch), chunk(o_ref.at[me], ch), local_sem).start()
            for k, (dev, pid) in enumerate(peers):
                remote(k, dev, pid, ch).start()
        for k, (dev, pid) in enumerate(peers):
            remote(k, dev, pid).wait_recv()
        for k, (dev, pid) in enumerate(peers):
            remote(k, dev, pid).wait_send()
        pltpu.make_async_copy(src_of(me), o_ref.at[me], local_sem).wait()

    out = pl.pallas_call(
        body, name=name,
        out_shape=jax.ShapeDtypeStruct((n, rows, W), x.dtype),
        in_specs=[pl.BlockSpec(memory_space=pl.ANY)],
        out_specs=pl.BlockSpec(memory_space=pl.ANY),
        scratch_shapes=[pltpu.SemaphoreType.DMA((npeer,)), pltpu.SemaphoreType.DMA((npeer,)),
                        pltpu.SemaphoreType.DMA],
    )(x2)
    return out.reshape((n,) + tuple(blk))


def _carried_comm(kind, xs):
    nt = len(xs)
    masks = {"chips_ag": [(1, 0, 0), (0, 1, 0), (1, 1, 0)], "sib_ag": [(0, 0, 1)],
             "all_rs": [(fx, fy, fc) for fx in (0, 1) for fy in (0, 1) for fc in (0, 1) if fx + fy + fc]}[kind]
    npeer = len(masks)

    def copies(x_refs, o_refs, sems):
        send_sems, recv_sems, local_sems = sems
        mx, my, mc = lax.axis_index("x"), lax.axis_index("y"), lax.axis_index("c")
        me = {"chips_ag": 2 * mx + my, "sib_ag": mc, "all_rs": 4 * mx + 2 * my + mc}[kind]

        def src(t, px, py, pc):
            if kind != "all_rs":
                return x_refs[t]
            half = xs[t].shape[1] // 2
            return x_refs[t].at[2 * px + py, pl.ds(pl.multiple_of(pc * half, 16), half), :]

        local = [pltpu.make_async_copy(src(t, mx, my, mc), o_refs[t].at[me], local_sems.at[t]) for t in range(nt)]
        remote = [pltpu.make_async_remote_copy(
            src_ref=src(t, mx ^ fx, my ^ fy, mc ^ fc), dst_ref=o_refs[t].at[me],
            send_sem=send_sems.at[t * npeer + k], recv_sem=recv_sems.at[t * npeer + k],
            device_id=(mx ^ fx, my ^ fy, mc ^ fc), device_id_type=pl.DeviceIdType.MESH)
            for t in range(nt) for k, (fx, fy, fc) in enumerate(masks)]
        return local, remote

    def start(x_refs, o_refs, sems):
        local, remote = copies(x_refs, o_refs, sems)
        for cp in local + remote:
            cp.start()

    def wait(x_refs, o_refs, sems):
        local, remote = copies(x_refs, o_refs, sems)
        for cp in remote:
            cp.wait_recv()
        for cp in remote:
            cp.wait_send()
        for cp in local:
            cp.wait()

    def out_shape(x):
        if kind == "all_rs":
            return (N_DEV, x.shape[1] // 2, x.shape[2])
        return ((N_CHIPS if kind == "chips_ag" else 2),) + tuple(x.shape)

    out_shapes = [jax.ShapeDtypeStruct(out_shape(x), x.dtype) for x in xs]
    scratch = [pltpu.SemaphoreType.DMA((max(nt * npeer, 1),)), pltpu.SemaphoreType.DMA((max(nt * npeer, 1),)),
               pltpu.SemaphoreType.DMA((max(nt, 1),))]
    return out_shapes, scratch, start, wait


def _ride(body, n_in, n_out, n_scratch, carry, nsteps):
    if not carry:
        return body, [], [], []
    kind, xs = carry
    nc = len(xs)
    c_out, c_scratch, c_start, c_wait = _carried_comm(kind, xs)

    def wrapped(*refs):
        ins, cx = refs[:n_in], refs[n_in:n_in + nc]
        outs, co = refs[n_in + nc:n_in + nc + n_out], refs[n_in + nc + n_out:n_in + 2 * nc + n_out]
        base = n_in + 2 * nc + n_out
        scr, sems = refs[base:base + n_scratch], refs[base + n_scratch:]
        i = pl.program_id(0)

        @pl.when(i == 0)
        def _():
            c_start(cx, co, sems)

        body(*ins, *outs, *scr)

        @pl.when(i == nsteps - 1)
        def _():
            c_wait(cx, co, sems)

    return wrapped, list(xs), c_out, c_scratch


def _comm_alone(kind, xs, name):
    nt = len(xs)
    out_shapes, scratch, start, wait = _carried_comm(kind, xs)

    def body(*refs):
        x_refs, o_refs, sems = refs[:nt], refs[nt:2 * nt], refs[2 * nt:]
        start(x_refs, o_refs, sems)
        wait(x_refs, o_refs, sems)

    any_spec = pl.BlockSpec(memory_space=pl.ANY)
    return pl.pallas_call(body, name=name, out_shape=out_shapes, in_specs=[any_spec] * nt,
                          out_specs=[any_spec] * nt, scratch_shapes=scratch)(*xs)


def _allgather8(x, name):
    return _comm(x, group="all", by_peer=False, name=name)


def _sum_slots(x, out_dtype, name):
    n, R, W = x.shape
    tr = _tile(R, max(16, min(512, SUM_BLOCK_BYTES // (n * W * x.dtype.itemsize) // 16 * 16)), 16)

    def body(x_ref, o_ref):
        acc = x_ref[0].astype(F32)
        for s in range(1, n):
            acc = acc + x_ref[s].astype(F32)
        o_ref[...] = acc.astype(o_ref.dtype)

    return pl.pallas_call(
        body, name=name, grid=(R // tr,),
        in_specs=[pl.BlockSpec((n, tr, W), lambda i: (0, i, 0))],
        out_specs=pl.BlockSpec((tr, W), lambda i: (i, 0)),
        out_shape=jax.ShapeDtypeStruct((R, W), out_dtype),
        compiler_params=_cparams(("parallel",)),
    )(x)


def _pack(arrs, dtype, row_mult):
    flat = jnp.concatenate([a.reshape(-1).astype(dtype) for a in arrs])
    n = flat.shape[0]
    quant = row_mult * PACK_W
    total = -(-n // quant) * quant
    if total != n:
        flat = jnp.concatenate([flat, jnp.zeros((total - n,), dtype)])
    return flat.reshape(total // PACK_W, PACK_W)


def _unpack(buf, shapes):
    flat = buf.reshape(-1)
    out, off = [], 0
    for s in shapes:
        n = math.prod(s)
        out.append(flat[off:off + n].reshape(s))
        off += n
    return out


class _Mat:
    def __init__(self, arr, axis=None, layer=None, part0=0, nparts=None):
        self.arr, self.axis, self.layer, self.part0 = arr, axis, layer, part0
        self.r, self.c = arr.shape[-2:]
        self.nparts = 1 if axis is None else (arr.shape[0] if nparts is None else nparts)

    @property
    def shape(self):
        return (self.r * (self.nparts if self.axis == 0 else 1), self.c * (self.nparts if self.axis == 1 else 1))

    def unit(self, ax):
        return self.r if ax == 0 else self.c

    def spec(self, tr, tc, fn):
        def imap(*g):
            rb, cb = fn(*g)
            lead = []
            if self.axis == 0:
                per = self.r // tr
                lead.append(self.part0 + rb // per)
                rb = rb % per
            elif self.axis == 1:
                per = self.c // tc
                lead.append(self.part0 + cb // per)
                cb = cb % per
            if self.layer is not None:
                lead.append(self.layer)
            return (*lead, rb, cb)

        return pl.BlockSpec((None,) * (self.arr.ndim - 2) + (tr, tc), imap)


def _as_mat(x):
    return x if isinstance(x, _Mat) else _Mat(x)


def _matmul(a, b, *, mode, name, out_dtype=F32, add=None, out_split=None, carry=None, tm_cap=1088, tn_cap=1536,
            tk_cap=2048):
    a, b = _as_mat(a), _as_mat(b)
    am, ak = (0, 1) if mode != "tn" else (1, 0)
    bk, bn = (0, 1) if mode != "nt" else (1, 0)
    M, K, N = a.shape[am], a.shape[ak], b.shape[bn]
    assert b.shape[bk] == K
    if out_split is None:
        o = _Mat(jax.ShapeDtypeStruct((M, N), out_dtype))
    else:
        ax, parts = out_split
        o = _Mat(jax.ShapeDtypeStruct((parts, M // parts if ax == 0 else M, N // parts if ax == 1 else N), out_dtype),
                 axis=ax)
    tm = _tile(math.gcd(a.unit(am), o.unit(0)), tm_cap, 16 if mode != "tn" else LANES)
    tn = _tile(math.gcd(b.unit(bn), o.unit(1)), tn_cap, LANES)
    tk = _tile(math.gcd(a.unit(ak), b.unit(bk)), tk_cap, LANES if mode != "tn" else 16)
    nk = K // tk
    dims = {"nn": _NN, "nt": _NT, "tn": _TN}[mode]

    kind, carry = carry if carry else ("chips_ag", [])
    carry = list(carry)
    nc = len(carry)
    c_out, c_scratch, c_start, c_wait = _carried_comm(kind, carry)
    grid = (M // tm, N // tn, nk)

    def body(*refs):
        a_ref, b_ref = refs[:2]
        add_ref = refs[2] if add is not None else None
        n_in = 2 + (add is not None)
        cx_refs, o_ref, co_refs = refs[n_in:n_in + nc], refs[n_in + nc], refs[n_in + nc + 1:n_in + 2 * nc + 1]
        acc_ref, sems = refs[n_in + 2 * nc + 1], refs[n_in + 2 * nc + 2:]
        i, j, k = pl.program_id(0), pl.program_id(1), pl.program_id(2)
        if nc:
            @pl.when((i == 0) & (j == 0) & (k == 0))
            def _():
                c_start(cx_refs, co_refs, sems)

        @pl.when(k == 0)
        def _():
            acc_ref[...] = jnp.zeros_like(acc_ref)

        acc_ref[...] += _dot(a_ref[...], b_ref[...], dims)

        @pl.when(k == nk - 1)
        def _():
            r = acc_ref[...]
            if add_ref is not None:
                r = r + add_ref[...].astype(F32)
            o_ref[...] = r.astype(o_ref.dtype)

        if nc:
            @pl.when((i == grid[0] - 1) & (j == grid[1] - 1) & (k == nk - 1))
            def _():
                c_wait(cx_refs, co_refs, sems)

    if mode == "tn":
        a_spec = a.spec(tk, tm, lambda i, j, k: (k, i))
    else:
        a_spec = a.spec(tm, tk, lambda i, j, k: (i, k))
    if mode == "nt":
        b_spec = b.spec(tn, tk, lambda i, j, k: (j, k))
    else:
        b_spec = b.spec(tk, tn, lambda i, j, k: (k, j))
    o_spec = o.spec(tm, tn, lambda i, j, k: (i, j))
    any_spec = pl.BlockSpec(memory_space=pl.ANY)
    in_specs = [a_spec, b_spec] + ([pl.BlockSpec((tm, tn), lambda i, j, k: (i, j))] if add is not None else [])
    args = (a.arr, b.arr) + ((add,) if add is not None else ())
    outs = pl.pallas_call(
        body, name=name, grid=grid,
        in_specs=in_specs + [any_spec] * nc, out_specs=[o_spec] + [any_spec] * nc,
        out_shape=[o.arr] + c_out,
        scratch_shapes=[pltpu.VMEM((tm, tn), F32)] + (c_scratch if nc else []),
        compiler_params=_cparams(("arbitrary",) * 3 if nc else ("parallel", "parallel", "arbitrary")),
    )(*args, *carry)
    return (outs[0], outs[1:]) if nc else outs[0]


def _matmul_residual(a, w, res, gate, C, name, carry=None):
    w = _as_mat(w)
    M, K = a.shape
    N = w.shape[1]
    tm = _tile(M, 1088, 16)
    tn = _tile(w.unit(1), 1024, LANES)
    tk = _tile(w.unit(0), 2048, LANES)
    nk = K // tk
    kind, carry = carry if carry else ("chips_ag", [])
    carry = list(carry)
    nc = len(carry)
    c_out, c_scratch, c_start, c_wait = _carried_comm(kind, carry)
    grid = (M // tm, N // tn, nk)

    def body(*refs):
        a_ref, w_ref, res_ref, gate_ref = refs[:4]
        cx_refs, (mix_ref, o_ref), co_refs = refs[4:4 + nc], refs[4 + nc:6 + nc], refs[6 + nc:6 + 2 * nc]
        acc_ref, sems = refs[6 + 2 * nc], refs[7 + 2 * nc:]
        i, j, k = pl.program_id(0), pl.program_id(1), pl.program_id(2)
        if nc:
            @pl.when((i == 0) & (j == 0) & (k == 0))
            def _():
                c_start(cx_refs, co_refs, sems)

        @pl.when(k == 0)
        def _():
            acc_ref[...] = jnp.zeros_like(acc_ref)

        acc_ref[...] += _dot(a_ref[...], w_ref[...], _NN)

        @pl.when(k == nk - 1)
        def _():
            r = acc_ref[...]
            row = i * tm + lax.broadcasted_iota(jnp.int32, (tm, 1), 0)
            g = jnp.where(row < C, gate_ref[0:1, :], gate_ref[1:2, :])
            mix_ref[...] = r
            o_ref[...] = res_ref[...] + g * r

        if nc:
            @pl.when((i == grid[0] - 1) & (j == grid[1] - 1) & (k == nk - 1))
            def _():
                c_wait(cx_refs, co_refs, sems)

    o_spec = pl.BlockSpec((tm, tn), lambda i, j, k: (i, j))
    any_spec = pl.BlockSpec(memory_space=pl.ANY)
    outs = pl.pallas_call(
        body, name=name, grid=grid,
        in_specs=[pl.BlockSpec((tm, tk), lambda i, j, k: (i, k)),
                  w.spec(tk, tn, lambda i, j, k: (k, j)),
                  o_spec,
                  pl.BlockSpec((2, tn), lambda i, j, k: (0, j))] + [any_spec] * nc,
        out_specs=[o_spec, o_spec] + [any_spec] * nc,
        out_shape=[jax.ShapeDtypeStruct((M, N), F32), jax.ShapeDtypeStruct((M, N), F32)] + c_out,
        scratch_shapes=[pltpu.VMEM((tm, tn), F32)] + (c_scratch if nc else []),
        compiler_params=_cparams(("arbitrary",) * 3 if nc else ("parallel", "parallel", "arbitrary")),
    )(a, w.arr, res, gate, *carry)
    return (outs[0], outs[1], outs[2:]) if nc else (outs[0], outs[1])


def _seg_select(i, tm, C, ref):
    row = i * tm + lax.broadcasted_iota(jnp.int32, (tm, 1), 0)
    return row < C, jnp.where(row < C, ref[0:1, :], ref[1:2, :])


def _normmod(X, g, shift, scale, C, name):
    T, D = X.shape
    tm = _tile(T, 512, 16)

    def body(x_ref, g_ref, sh_ref, sc_ref, h_ref):
        i = pl.program_id(0)
        x = x_ref[...]
        r = lax.rsqrt(jnp.mean(x * x, axis=-1, keepdims=True) + NORM_EPS)
        _, sh = _seg_select(i, tm, C, sh_ref)
        _, sc = _seg_select(i, tm, C, sc_ref)
        h_ref[...] = ((x * r) * g_ref[...] * (1.0 + sc) + sh).astype(BF16)

    row = pl.BlockSpec((tm, D), lambda i: (i, 0))
    vec = pl.BlockSpec((1, D), lambda i: (0, 0))
    two = pl.BlockSpec((2, D), lambda i: (0, 0))
    return pl.pallas_call(
        body, name=name, grid=(T // tm,),
        in_specs=[row, vec, two, two], out_specs=row,
        out_shape=jax.ShapeDtypeStruct((T, D), BF16),
        compiler_params=_cparams(("parallel",)),
    )(X, g, shift, scale)


def _normmod_bwd(X, g, scale, dh, dres, C, name):
    T, D = X.shape
    tm = _tile(T, 512, 16)

    def body(x_ref, g_ref, sc_ref, dh_ref, dres_ref, dx_ref, dg_ref, dsh_ref, dsc_ref):
        i = pl.program_id(0)

        @pl.when(i == 0)
        def _():
            dg_ref[...] = jnp.zeros_like(dg_ref)
            dsh_ref[...] = jnp.zeros_like(dsh_ref)
            dsc_ref[...] = jnp.zeros_like(dsc_ref)

        x = x_ref[...]
        dh = dh_ref[...]
        gv = g_ref[...]
        r = lax.rsqrt(jnp.mean(x * x, axis=-1, keepdims=True) + NORM_EPS)
        xh = x * r
        isc, sc = _seg_select(i, tm, C, sc_ref)
        n = xh * gv
        dhn = dh * n
        zero = jnp.zeros_like(dh)
        dsh_ref[0:1, :] += jnp.sum(jnp.where(isc, dh, zero), axis=0, keepdims=True)
        dsh_ref[1:2, :] += jnp.sum(jnp.where(isc, zero, dh), axis=0, keepdims=True)
        dsc_ref[0:1, :] += jnp.sum(jnp.where(isc, dhn, zero), axis=0, keepdims=True)
        dsc_ref[1:2, :] += jnp.sum(jnp.where(isc, zero, dhn), axis=0, keepdims=True)
        dn = dh * (1.0 + sc)
        dg_ref[...] += jnp.sum(dn * xh, axis=0, keepdims=True)
        dxh = dn * gv
        dx_ref[...] = dres_ref[...] + r * (dxh - xh * jnp.mean(dxh * xh, axis=-1, keepdims=True))

    row = pl.BlockSpec((tm, D), lambda i: (i, 0))
    vec = pl.BlockSpec((1, D), lambda i: (0, 0))
    two = pl.BlockSpec((2, D), lambda i: (0, 0))
    return pl.pallas_call(
        body, name=name, grid=(T // tm,),
        in_specs=[row, vec, two, row, row], out_specs=[row, vec, two, two],
        out_shape=[jax.ShapeDtypeStruct((T, D), F32), jax.ShapeDtypeStruct((1, D), F32),
                   jax.ShapeDtypeStruct((2, D), F32), jax.ShapeDtypeStruct((2, D), F32)],
        compiler_params=_cparams(("arbitrary",)),
    )(X, g, scale, dh, dres)


def _gate_bwd(dxo, mix, gate, C, name):
    T, D = dxo.shape
    tm = _tile(T, 512, 16)

    def body(dx_ref, mix_ref, gate_ref, dmix_ref, dgate_ref):
        i = pl.program_id(0)

        @pl.when(i == 0)
        def _():
            dgate_ref[...] = jnp.zeros_like(dgate_ref)

        dx = dx_ref[...]
        isc, gt = _seg_select(i, tm, C, gate_ref)
        dmix_ref[...] = (dx * gt).astype(BF16)
        p = dx * mix_ref[...]
        zero = jnp.zeros_like(p)
        dgate_ref[0:1, :] += jnp.sum(jnp.where(isc, p, zero), axis=0, keepdims=True)
        dgate_ref[1:2, :] += jnp.sum(jnp.where(isc, zero, p), axis=0, keepdims=True)

    row = pl.BlockSpec((tm, D), lambda i: (i, 0))
    two = pl.BlockSpec((2, D), lambda i: (0, 0))
    return pl.pallas_call(
        body, name=name, grid=(T // tm,),
        in_specs=[row, row, two], out_specs=[row, two],
        out_shape=[jax.ShapeDtypeStruct((T, D), BF16), jax.ShapeDtypeStruct((2, D), F32)],
        compiler_params=_cparams(("arbitrary",)),
    )(dxo, mix, gate)


def _conv_masks(T, C):
    row = lax.broadcasted_iota(jnp.int32, (T, 1), 0)
    first = (row == 0) | (row == C)
    last = (row == C - 1) | (row == T - 1)
    return first, last


def _shift_rows(x, first, last):
    T = x.shape[0]
    prev = jnp.where(first, 0.0, pltpu.roll(x, 1, 0))
    nxt = jnp.where(last, 0.0, pltpu.roll(x, T - 1, 0))
    return prev, nxt


def _convact(up, cw, cb, C, name):
    _, T, F = up.shape
    tc = LANES

    def body(u_ref, w_ref, b_ref, o_ref):
        gp = u_ref[0].astype(F32)
        first, last = _conv_masks(T, C)
        prev, nxt = _shift_rows(gp, first, last)
        gc = prev * w_ref[0:1, :] + gp * w_ref[1:2, :] + nxt * w_ref[2:3, :] + b_ref[...]
        o_ref[...] = (gc * _sigmoid(gc) * u_ref[1].astype(F32)).astype(BF16)

    col = pl.BlockSpec((T, tc), lambda j: (0, j))
    return pl.pallas_call(
        body, name=name, grid=(F // tc,),
        in_specs=[pl.BlockSpec((2, T, tc), lambda j: (0, 0, j)), pl.BlockSpec((3, tc), lambda j: (0, j)),
                  pl.BlockSpec((1, tc), lambda j: (0, j))],
        out_specs=col, out_shape=jax.ShapeDtypeStruct((T, F), BF16),
        compiler_params=_cparams(("parallel",)),
    )(up, cw, cb)


def _convact_bwd(up, cw, cb, dact, C, name):
    _, T, F = up.shape
    tc = LANES

    def body(u_ref, w_ref, b_ref, da_ref, du_ref, dw_ref, db_ref):
        gp = u_ref[0].astype(F32)
        val = u_ref[1].astype(F32)
        da = da_ref[...].astype(F32)
        first, last = _conv_masks(T, C)
        prev, nxt = _shift_rows(gp, first, last)
        w0, w1, w2 = w_ref[0:1, :], w_ref[1:2, :], w_ref[2:3, :]
        gc = prev * w0 + gp * w1 + nxt * w2 + b_ref[...]
        s = _sigmoid(gc)
        du_ref[1] = (da * gc * s).astype(BF16)
        dgc = da * val * (s + gc * s * (1.0 - s))
        db_ref[...] = jnp.sum(dgc, axis=0, keepdims=True)
        dw_ref[0:1, :] = jnp.sum(dgc * prev, axis=0, keepdims=True)
        dw_ref[1:2, :] = jnp.sum(dgc * gp, axis=0, keepdims=True)
        dw_ref[2:3, :] = jnp.sum(dgc * nxt, axis=0, keepdims=True)
        dprev, dnxt = _shift_rows(dgc, first, last)
        du_ref[0] = (dnxt * w0 + dgc * w1 + dprev * w2).astype(BF16)

    col = pl.BlockSpec((T, tc), lambda j: (0, j))
    two = pl.BlockSpec((2, T, tc), lambda j: (0, 0, j))
    w3 = pl.BlockSpec((3, tc), lambda j: (0, j))
    w1 = pl.BlockSpec((1, tc), lambda j: (0, j))
    return pl.pallas_call(
        body, name=name, grid=(F // tc,),
        in_specs=[two, w3, w1, col], out_specs=[two, w3, w1],
        out_shape=[jax.ShapeDtypeStruct((2, T, F), BF16),
                   jax.ShapeDtypeStruct((3, F), F32), jax.ShapeDtypeStruct((1, F), F32)],
        compiler_params=_cparams(("parallel",)),
    )(up, cw, cb, dact)


def _rot_half(x):
    W = x.shape[-1]
    lane = lax.broadcasted_iota(jnp.int32, x.shape, x.ndim - 1)
    lo = (lane % 32) < 16
    return jnp.where(lo, -pltpu.roll(x, W - 16, x.ndim - 1), pltpu.roll(x, 16, x.ndim - 1))


def _swap_halves(x):
    return pltpu.roll(x, 64, x.ndim - 1)


def _rope_tables(dm):
    t = jnp.arange(dm.L)
    n_freq = HEAD_DIM // 4
    inv_freq = ROPE_BASE ** (-jnp.arange(n_freq, dtype=F32) / n_freq)
    ang_r = (t // GRID_W).astype(F32)[:, None] * inv_freq
    ang_c = (t % GRID_W).astype(F32)[:, None] * inv_freq
    ang = jnp.concatenate([ang_r, ang_r, ang_c, ang_c], axis=-1)
    cos = jnp.concatenate([jnp.ones((dm.C, HEAD_DIM), F32), jnp.cos(ang)], axis=0)
    sin = jnp.concatenate([jnp.zeros((dm.C, HEAD_DIM), F32), jnp.sin(ang)], axis=0)
    return jnp.tile(cos, (1, 2)), jnp.tile(sin, (1, 2))


def _attn_prep(z, cos, sin, dm, name):
    T = dm.T
    AW = dm.ATTN_W
    tm = _tile(T, 256, 16)
    rep = AW // LANES

    def body(z_ref, cos_ref, sin_ref, q_ref, k_ref, v_ref):
        cs, sn = cos_ref[...], sin_ref[...]
        q = z_ref[:, 0:AW]
        csq, snq = jnp.tile(cs, (1, rep)), jnp.tile(sn, (1, rep))
        q_ref[...] = (q * csq + _rot_half(q) * snq).astype(BF16)
        k = z_ref[:, dm.OFF_K:dm.OFF_V]
        k = k * cs + _rot_half(k) * sn
        v = z_ref[:, dm.OFF_V:dm.OFF_S]
        lo = lax.broadcasted_iota(jnp.int32, k.shape, 1) < HEAD_DIM
        ks, vs = _swap_halves(k), _swap_halves(v)
        k_ref[0] = jnp.where(lo, k, ks).astype(BF16)
        k_ref[1] = jnp.where(lo, ks, k).astype(BF16)
        v_ref[0] = jnp.where(lo, v, vs).astype(BF16)
        v_ref[1] = jnp.where(lo, vs, v).astype(BF16)

    tab = pl.BlockSpec((tm, LANES), lambda i: (i, 0))
    kv = pl.BlockSpec((2, tm, LANES), lambda i: (0, i, 0))
    return pl.pallas_call(
        body, name=name, grid=(T // tm,),
        in_specs=[pl.BlockSpec((tm, dm.OFF_S), lambda i: (i, 0)), tab, tab],
        out_specs=[pl.BlockSpec((tm, AW), lambda i: (i, 0)), kv, kv],
        out_shape=[jax.ShapeDtypeStruct((T, AW), BF16), jax.ShapeDtypeStruct((2, T, LANES), BF16),
                   jax.ShapeDtypeStruct((2, T, LANES), BF16)],
        compiler_params=_cparams(("parallel",)),
    )(z, cos, sin)


def _attn_prep_bwd(dq, dk2, dv2, cos, sin, dm, name):
    T = dm.T
    AW = dm.ATTN_W
    tm = _tile(T, 256, 16)
    rep = AW // LANES

    def body(dq_ref, dk_ref, dv_ref, cos_ref, sin_ref, o_ref):
        cs, sn = cos_ref[...], sin_ref[...]
        csq, snq = jnp.tile(cs, (1, rep)), jnp.tile(sn, (1, rep))
        dq = dq_ref[...]
        o_ref[:, 0:AW] = dq * csq - _rot_half(dq * snq)
        lo = lax.broadcasted_iota(jnp.int32, (tm, LANES), 1) < HEAD_DIM
        dk = jnp.where(lo, dk_ref[0], dk_ref[1])
        o_ref[:, dm.OFF_K:dm.OFF_V] = dk * cs - _rot_half(dk * sn)
        o_ref[:, dm.OFF_V:dm.OFF_S] = jnp.where(lo, dv_ref[0], dv_ref[1])

    tab = pl.BlockSpec((tm, LANES), lambda i: (i, 0))
    kv = pl.BlockSpec((2, tm, LANES), lambda i: (0, i, 0))
    return pl.pallas_call(
        body, name=name, grid=(T // tm,),
        in_specs=[pl.BlockSpec((tm, AW), lambda i: (i, 0)), kv, kv, tab, tab],
        out_specs=pl.BlockSpec((tm, dm.OFF_S), lambda i: (i, 0)),
        out_shape=jax.ShapeDtypeStruct((T, dm.OFF_S), F32),
        compiler_params=_cparams(("parallel",)),
    )(dq, dk2, dv2, cos, sin)


def _attn_specs(dm):
    B = ATTN_BLOCK
    nblk = dm.T // B
    q_spec = pl.BlockSpec((B, dm.ATTN_W), lambda i: (i, 0))
    prev = pl.BlockSpec((2, B, LANES), lambda i: (0, jnp.maximum(i - 1, 0), 0))
    own = pl.BlockSpec((2, B, LANES), lambda i: (0, i, 0))
    nxt = pl.BlockSpec((2, B, LANES), lambda i: (0, jnp.minimum(i + 1, nblk - 1), 0))
    ctx = pl.BlockSpec((2, dm.C, LANES), lambda i: (0, 0, 0))
    sink = pl.BlockSpec((2, GQA_RATIO * B, 1), lambda i: (0, 0, 0))
    return nblk, q_spec, prev, own, nxt, ctx, sink


def _attn_scores(i, q_ref, kp_ref, ko_ref, kn_ref, kc_ref, sink_ref, h, dm):
    B = ATTN_BLOCK
    nctx = dm.C // B
    nb = dm.L // B
    npair = GQA_RATIO // 2
    lo = lax.broadcasted_iota(jnp.int32, (B, LANES), 1) < HEAD_DIM
    zero = jnp.zeros((B, LANES), BF16)
    parts = []
    for p in range(npair):
        q2 = q_ref[:, (h * npair + p) * LANES:(h * npair + p + 1) * LANES]
        parts.append(jnp.where(lo, q2, zero))
        parts.append(jnp.where(lo, zero, q2))
    q8 = jnp.concatenate(parts, axis=0)
    kcat = jnp.concatenate([kp_ref[h], ko_ref[h], kn_ref[h], kc_ref[h]], axis=0)
    s = _dot(q8, kcat, _NT) * (HEAD_DIM ** -0.5)
    R, NK = s.shape
    iq = lax.broadcasted_iota(jnp.int32, (R, NK), 0) % B
    col = lax.broadcasted_iota(jnp.int32, (R, NK), 1)
    jk = col % B
    qb = i - nctx
    lat = qb >= 0
    m_prev = (col < B) & lat & (qb >= 1) & (jk >= iq)
    m_own = (col >= B) & (col < 2 * B) & lat
    m_next = (col >= 2 * B) & (col < 3 * B) & lat & (qb <= nb - 2) & (jk <= iq)
    mask = m_prev | m_own | m_next | (col >= 3 * B)
    s = jnp.where(mask, s, -jnp.inf)
    sk = sink_ref[h]
    m = jnp.maximum(jnp.max(s, axis=-1, keepdims=True), sk)
    e = jnp.exp(s - m)
    es = jnp.exp(sk - m)
    inv = 1.0 / (jnp.sum(e, axis=-1, keepdims=True) + es)
    return q8, kcat, e * inv, es * inv, lo


def _unstack_heads(x8, lo):
    B = ATTN_BLOCK
    out = []
    for p in range(GQA_RATIO // 2):
        a = x8[(2 * p) * B:(2 * p + 1) * B]
        b = x8[(2 * p + 1) * B:(2 * p + 2) * B]
        out.append(jnp.where(lo, a, b))
    return out


def _attention(q, k2, v2, sinkcol, dm, name):
    B = ATTN_BLOCK
    nblk, q_spec, prev, own, nxt, ctx, sink = _attn_specs(dm)
    npair = GQA_RATIO // 2

    def body(q_ref, kp_ref, ko_ref, kn_ref, kc_ref, vp_ref, vo_ref, vn_ref, vc_ref, sink_ref, o_ref):
        i = pl.program_id(0)
        for h in range(2):
            _, _, p, _, lo = _attn_scores(i, q_ref, kp_ref, ko_ref, kn_ref, kc_ref, sink_ref, h, dm)
            vcat = jnp.concatenate([vp_ref[h], vo_ref[h], vn_ref[h], vc_ref[h]], axis=0)
            o8 = _dot(p, vcat, _NN)
            for pi, o2 in enumerate(_unstack_heads(o8, lo)):
                c0 = (h * npair + pi) * LANES
                o_ref[:, c0:c0 + LANES] = o2.astype(BF16)

    return pl.pallas_call(
        body, name=name, grid=(nblk,),
        in_specs=[q_spec, prev, own, nxt, ctx, prev, own, nxt, ctx, sink],
        out_specs=q_spec, out_shape=jax.ShapeDtypeStruct((dm.T, dm.ATTN_W), BF16),
        compiler_params=_cparams(("parallel",)),
    )(q, k2, k2, k2, k2, v2, v2, v2, v2, sinkcol)


def _attention_bwd(q, k2, v2, sinkcol, do, dm, name, carry=None):
    B = ATTN_BLOCK
    nblk, q_spec, prev, own, nxt, ctx, sink = _attn_specs(dm)
    npair = GQA_RATIO // 2

    def body(q_ref, kp_ref, ko_ref, kn_ref, kc_ref, vp_ref, vo_ref, vn_ref, vc_ref, sink_ref, do_ref,
             dq_ref, pk_ref, pv_ref, dkc_ref, dvc_ref, dsk_ref):
        i = pl.program_id(0)

        @pl.when(i == 0)
        def _():
            dkc_ref[...] = jnp.zeros_like(dkc_ref)
            dvc_ref[...] = jnp.zeros_like(dvc_ref)
            dsk_ref[...] = jnp.zeros_like(dsk_ref)

        for h in range(2):
            q8, kcat, p, ps, lo = _attn_scores(i, q_ref, kp_ref, ko_ref, kn_ref, kc_ref, sink_ref, h, dm)
            vcat = jnp.concatenate([vp_ref[h], vo_ref[h], vn_ref[h], vc_ref[h]], axis=0)
            zero = jnp.zeros((B, LANES), F32)
            parts = []
            for pi in range(npair):
                d2 = do_ref[:, (h * npair + pi) * LANES:(h * npair + pi + 1) * LANES]
                parts.append(jnp.where(lo, d2, zero))
                parts.append(jnp.where(lo, zero, d2))
            do8 = jnp.concatenate(parts, axis=0)
            dp = _dot(do8, vcat, _NT)
            delta = jnp.sum(p * dp, axis=-1, keepdims=True)
            ds = p * (dp - delta) * (HEAD_DIM ** -0.5)
            dsr = -ps * delta
            rows = [jnp.broadcast_to(jnp.sum(dsr[r * B:(r + 1) * B], axis=0, keepdims=True), (1, LANES))
                    for r in range(GQA_RATIO)]
            dsk_ref[h] += jnp.concatenate(rows, axis=0)
            dq8 = _dot(ds, kcat, _NN)
            for pi, d2 in enumerate(_unstack_heads(dq8, lo)):
                c0 = (h * npair + pi) * LANES
                dq_ref[:, c0:c0 + LANES] = d2
            dk = _dot(ds, q8, _TN)
            dv = _dot(p, do8, _TN)
            dk = dk + _swap_halves(dk)
            dv = dv + _swap_halves(dv)
            for w in range(3):
                pk_ref[0, h, w] = dk[w * B:(w + 1) * B]
                pv_ref[0, h, w] = dv[w * B:(w + 1) * B]
            dkc_ref[h] += dk[3 * B:]
            dvc_ref[h] += dv[3 * B:]

    part = pl.BlockSpec((1, 2, 3, B, LANES), lambda i: (i, 0, 0, 0, 0))
    acc_c = pl.BlockSpec((2, dm.C, LANES), lambda i: (0, 0, 0))
    acc_s = pl.BlockSpec((2, GQA_RATIO, LANES), lambda i: (0, 0, 0))
    body, cxs, c_out, c_scratch = _ride(body, 11, 6, 0, carry, nblk)
    any_spec = pl.BlockSpec(memory_space=pl.ANY)
    outs = pl.pallas_call(
        body, name=name, grid=(nblk,),
        in_specs=[q_spec, prev, own, nxt, ctx, prev, own, nxt, ctx, sink, q_spec] + [any_spec] * len(cxs),
        out_specs=[q_spec, part, part, acc_c, acc_c, acc_s] + [any_spec] * len(cxs),
        out_shape=[jax.ShapeDtypeStruct((dm.T, dm.ATTN_W), F32),
                   jax.ShapeDtypeStruct((nblk, 2, 3, B, LANES), F32),
                   jax.ShapeDtypeStruct((nblk, 2, 3, B, LANES), F32),
                   jax.ShapeDtypeStruct((2, dm.C, LANES), F32), jax.ShapeDtypeStruct((2, dm.C, LANES), F32),
                   jax.ShapeDtypeStruct((2, GQA_RATIO, LANES), F32)] + c_out,
        scratch_shapes=c_scratch,
        compiler_params=_cparams(("arbitrary",)),
    )(q, k2, k2, k2, k2, v2, v2, v2, v2, sinkcol, do, *cxs)
    return (outs[:6], outs[6:]) if carry else outs


def _kv_reduce(part, ctxacc, dm, name):
    B = ATTN_BLOCK
    nblk = dm.T // B
    nctx = dm.C // B

    def body(own_ref, nxt_ref, prv_ref, c_ref, o_ref):
        j = pl.program_id(0)
        acc = own_ref[0, :, 0]
        acc = acc + jnp.where(j < nblk - 1, nxt_ref[0, :, 0], 0.0)
        acc = acc + jnp.where(j > 0, prv_ref[0, :, 0], 0.0)
        acc = acc + jnp.where(j < nctx, c_ref[...], 0.0)
        o_ref[...] = acc

    def spec(w, f):
        return pl.BlockSpec((1, 2, 1, B, LANES), lambda j: (f(j), 0, w, 0, 0))

    return pl.pallas_call(
        body, name=name, grid=(nblk,),
        in_specs=[spec(1, lambda j: j), spec(0, lambda j: jnp.minimum(j + 1, nblk - 1)),
                  spec(2, lambda j: jnp.maximum(j - 1, 0)),
                  pl.BlockSpec((2, B, LANES), lambda j: (0, jnp.minimum(j, nctx - 1), 0))],
        out_specs=pl.BlockSpec((2, B, LANES), lambda j: (0, j, 0)),
        out_shape=jax.ShapeDtypeStruct((2, dm.T, LANES), F32),
        compiler_params=_cparams(("parallel",)),
    )(part, part, part, ctxacc)


def _gmlp_core(z_ref, lg_ref, lb_ref, ws_ref, bs_ref, dm):
    GW = dm.GMLP_W
    gu = z_ref[:, dm.OFF_GU:dm.OFF_GV]
    gv = z_ref[:, dm.OFF_GV:dm.N_IN]
    u = _gelu(gu)
    ge = _gelu(gv)
    mu = jnp.mean(ge, axis=-1, keepdims=True)
    xc = ge - mu
    r = lax.rsqrt(jnp.mean(xc * xc, axis=-1, keepdims=True) + NORM_EPS)
    xh = xc * r
    v = xh * lg_ref[...] + lb_ref[...]
    mixed = []
    for g in range(dm.GG):
        vg = v[:, g * GMLP_GROUP_W:(g + 1) * GMLP_GROUP_W]
        mixed.append(_dot(ws_ref[g], vg, _NN) + bs_ref[g])
    return gu, gv, u, xh, r, v, mixed


def _gmlp(z, ln_g, ln_b, w_s, b_s, dm, name):
    T, GW, CH = dm.T, dm.GMLP_W, GMLP_CHUNK

    def body(z_ref, lg_ref, lb_ref, ws_ref, bs_ref, o_ref):
        _, _, u, _, _, _, mixed = _gmlp_core(z_ref, lg_ref, lb_ref, ws_ref, bs_ref, dm)
        for g in range(dm.GG):
            sl = slice(g * GMLP_GROUP_W, (g + 1) * GMLP_GROUP_W)
            o_ref[:, sl] = (u[:, sl] * mixed[g]).astype(BF16)

    vec = pl.BlockSpec((1, GW), lambda i: (0, 0))
    return pl.pallas_call(
        body, name=name, grid=(T // CH,),
        in_specs=[pl.BlockSpec((CH, dm.N_IN), lambda i: (i, 0)), vec, vec,
                  pl.BlockSpec((dm.GG, CH, CH), lambda i: (0, 0, 0)),
                  pl.BlockSpec((dm.GG, CH, 1), lambda i: (0, 0, 0))],
        out_specs=pl.BlockSpec((CH, GW), lambda i: (i, 0)),
        out_shape=jax.ShapeDtypeStruct((T, GW), BF16),
        compiler_params=_cparams(("parallel",)),
    )(z, ln_g, ln_b, w_s, b_s)


def _gmlp_bwd(z, ln_g, ln_b, w_s, b_s, do, dm, name):
    T, GW, CH = dm.T, dm.GMLP_W, GMLP_CHUNK

    def body(z_ref, lg_ref, lb_ref, ws_ref, bs_ref, do_ref, dz_ref, dlg_ref, dlb_ref, dws_ref, dbs_ref):
        i = pl.program_id(0)

        @pl.when(i == 0)
        def _():
            dlg_ref[...] = jnp.zeros_like(dlg_ref)
            dlb_ref[...] = jnp.zeros_like(dlb_ref)
            dws_ref[...] = jnp.zeros_like(dws_ref)
            dbs_ref[...] = jnp.zeros_like(dbs_ref)

        gu, gv, u, xh, r, v, mixed = _gmlp_core(z_ref, lg_ref, lb_ref, ws_ref, bs_ref, dm)
        do = do_ref[...]
        dv_parts = []
        for g in range(dm.GG):
            sl = slice(g * GMLP_GROUP_W, (g + 1) * GMLP_GROUP_W)
            dog = do[:, sl]
            dz_ref[:, sl] = dog * mixed[g] * _gelu_grad(gu[:, sl])
            dmx = dog * u[:, sl]
            dbs_ref[g] += jnp.sum(dmx, axis=-1, keepdims=True)
            dws_ref[g] += _dot(dmx, v[:, sl], _NT)
            dv_parts.append(_dot(ws_ref[g], dmx, _TN))
        dv = jnp.concatenate(dv_parts, axis=-1) if dm.GG > 1 else dv_parts[0]
        dlg_ref[...] += jnp.sum(dv * xh, axis=0, keepdims=True)
        dlb_ref[...] += jnp.sum(dv, axis=0, keepdims=True)
        dxh = dv * lg_ref[...]
        dge = r * (dxh - jnp.mean(dxh, axis=-1, keepdims=True) - xh * jnp.mean(dxh * xh, axis=-1, keepdims=True))
        dz_ref[:, GW:2 * GW] = dge * _gelu_grad(gv)

    vec = pl.BlockSpec((1, GW), lambda i: (0, 0))
    wsp = pl.BlockSpec((dm.GG, CH, CH), lambda i: (0, 0, 0))
    bsp = pl.BlockSpec((dm.GG, CH, 1), lambda i: (0, 0, 0))
    return pl.pallas_call(
        body, name=name, grid=(T // CH,),
        in_specs=[pl.BlockSpec((CH, dm.N_IN), lambda i: (i, 0)), vec, vec, wsp, bsp,
                  pl.BlockSpec((CH, GW), lambda i: (i, (dm.ATTN_W + dm.SSM_W) // GW))],
        out_specs=[pl.BlockSpec((CH, 2 * GW), lambda i: (i, 0)), vec, vec, wsp, bsp],
        out_shape=[jax.ShapeDtypeStruct((T, 2 * GW), F32), jax.ShapeDtypeStruct((1, GW), F32),
                   jax.ShapeDtypeStruct((1, GW), F32), jax.ShapeDtypeStruct((dm.GG, CH, CH), F32),
                   jax.ShapeDtypeStruct((dm.GG, CH, 1), F32)],
        compiler_params=_cparams(("arbitrary",)),
    )(z, ln_g, ln_b, w_s, b_s, do)


def _disc_fn(lam_re, lam_im, ldt, b_re, b_im):
    dt = jnp.exp(ldt)
    mag = jnp.exp(lam_re * dt)
    a_re = mag * jnp.cos(lam_im * dt)
    a_im = mag * jnp.sin(lam_im * dt)
    den = lam_re * lam_re + lam_im * lam_im
    n_re = a_re - 1.0
    f_re = (n_re * lam_re + a_im * lam_im) / den
    f_im = (a_im * lam_re - n_re * lam_im) / den
    return a_re, a_im, f_re * b_re - f_im * b_im, f_re * b_im + f_im * b_re


def _s5_disc(lam_re, lam_im, ldt, bT_re, bT_im, name):
    nd, H, GP = bT_re.shape

    def body(lr_ref, li_ref, dt_ref, br_ref, bi_ref, ar_ref, ai_ref, bbr_ref, bbi_ref):
        for d in range(nd):
            a_re, a_im, bb_re, bb_im = _disc_fn(lr_ref[d], li_ref[d], dt_ref[d], br_ref[d], bi_ref[d])
            ar_ref[d] = a_re
            ai_ref[d] = a_im
            bbr_ref[d] = bb_re
            bbi_ref[d] = bb_im

    vs = jax.ShapeDtypeStruct((nd, 1, GP), F32)
    ms = jax.ShapeDtypeStruct((nd, H, GP), F32)
    return pl.pallas_call(body, name=name, out_shape=[vs, vs, ms, ms], compiler_params=_cparams())(
        lam_re, lam_im, ldt, bT_re, bT_im)


def _s5_disc_bwd(lam_re, lam_im, ldt, bT_re, bT_im, da_re, da_im, dbb_re, dbb_im, name):
    nd, H, GP = bT_re.shape

    def body(lr_ref, li_ref, dt_ref, br_ref, bi_ref, dar_ref, dai_ref, dbr_ref, dbi_ref,
             olr_ref, oli_ref, odt_ref, obr_ref, obi_ref):
        for d in range(nd):
            _, vjp = jax.vjp(_disc_fn, lr_ref[d], li_ref[d], dt_ref[d], br_ref[d], bi_ref[d])
            g = vjp((dar_ref[d], dai_ref[d], dbr_ref[d], dbi_ref[d]))
            olr_ref[d], oli_ref[d], odt_ref[d], obr_ref[d], obi_ref[d] = g

    vs = jax.ShapeDtypeStruct((nd, 1, GP), F32)
    ms = jax.ShapeDtypeStruct((nd, H, GP), F32)
    return pl.pallas_call(body, name=name, out_shape=[vs, vs, vs, ms, ms], compiler_params=_cparams())(
        lam_re, lam_im, ldt, bT_re, bT_im, da_re, da_im, dbb_re, dbb_im)


def _scan_pass(re_ref, im_ref, nsteps, ar, ai, ir, ii, reverse, store):
    def step(n, carry):
        sr, si = carry
        t = (nsteps - 1 - n) if reverse else n
        off = pl.multiple_of(t * 8, 8)
        nr = ar * sr - ai * si + re_ref[pl.ds(off, 8), :]
        ni = ar * si + ai * sr + im_ref[pl.ds(off, 8), :]
        if store:
            re_ref[pl.ds(off, 8), :] = nr
            im_ref[pl.ds(off, 8), :] = ni
        return nr, ni

    return lax.fori_loop(0, nsteps, step, (ir, ii), unroll=4)


def _cpow(ar, ai, n):
    rr = ri = None
    br, bi = ar, ai
    while n:
        if n & 1:
            rr, ri = (br, bi) if rr is None else (rr * br - ri * bi, rr * bi + ri * br)
        n >>= 1
        if n:
            br, bi = br * br - bi * bi, 2.0 * br * bi
    return rr, ri


def _row_of(x, k):
    rows = lax.broadcasted_iota(jnp.int32, x.shape, 0)
    return jnp.sum(jnp.where(rows == k, x, 0.0), axis=0, keepdims=True)


def _full_scan(re_ref, im_ref, nsteps, ar1, ai1, h0r, h0i, reverse):
    P = ar1.shape[1]
    ar, ai = jnp.broadcast_to(ar1, (8, P)), jnp.broadcast_to(ai1, (8, P))
    z = jnp.zeros((8, P), F32)
    er, ei = _scan_pass(re_ref, im_ref, nsteps, ar, ai, z, z, reverse, False)
    pr, pi = _cpow(ar1, ai1, nsteps)
    rows = lax.broadcasted_iota(jnp.int32, (8, P), 0)
    initr, initi = z, z
    cr, ci = h0r, h0i
    for k in (range(SCAN_SEGMENTS - 1, -1, -1) if reverse else range(SCAN_SEGMENTS)):
        initr = jnp.where(rows == k, cr, initr)
        initi = jnp.where(rows == k, ci, initi)
        ekr, eki = _row_of(er, k), _row_of(ei, k)
        cr, ci = ekr + pr * cr - pi * ci, eki + pr * ci + pi * cr
    _scan_pass(re_ref, im_ref, nsteps, ar, ai, initr, initi, reverse, True)
    return initr, initi, cr, ci


def _rows_loop(n, chunk, fn):
    def step(c, carry):
        fn(pl.multiple_of(c * chunk, chunk))
        return carry
    lax.fori_loop(0, n // chunk, step, 0)


def _s5_specs(dm, PB):
    nsb = dm.GP // PB
    per = (LANES * SSM_STATE // SSM_GROUP) // PB
    ul = pl.BlockSpec((dm.L, LANES), lambda j: (0, j // per))
    uc = pl.BlockSpec((dm.C, LANES), lambda j: (0, j // per))
    av = pl.BlockSpec((2, 1, PB), lambda j: (0, 0, j))
    bd = pl.BlockSpec((2, LANES, PB), lambda j: (0, j // per, j))
    cd = pl.BlockSpec((2, PB, LANES), lambda j: (0, j, j // per))
    dv = pl.BlockSpec((1, LANES), lambda j: (0, j // per))
    return PB, nsb, per, ul, uc, av, bd, cd, dv


def _s5_scan(ul, uc, a_re, a_im, bd_re, bd_im, cd_re, cd_im, dskip, dm, name):
    PB, nsb, per, ul_s, uc_s, av, bd, cd, dv = _s5_specs(dm, STATE_BLOCK_FWD)
    L, C = dm.L, dm.C
    RC = _tile(L, 512, 8)
    RCC = _tile(C, 512, 8)

    def body(ul_ref, uc_ref, ar_ref, ai_ref, br_ref, bi_ref, cr_ref, ci_ref, d_ref, yl_ref, yc_ref,
             lre, lim, cre, cim):
        j = pl.program_id(0)

        @pl.when(j % per == 0)
        def _():
            yl_ref[...] = ul_ref[...] * d_ref[...]
            yc_ref[...] = uc_ref[...] * d_ref[...]

        for d in range(2):
            rev = d == 1
            ar1, ai1 = ar_ref[d], ai_ref[d]

            def proj(u_ref, re, im, chunk, n):
                def f(r0):
                    u = u_ref[pl.ds(r0, chunk), :]
                    re[pl.ds(r0, chunk), :] = _dot(u, br_ref[d], _NN)
                    im[pl.ds(r0, chunk), :] = _dot(u, bi_ref[d], _NN)
                _rows_loop(n, chunk, f)

            def readout(y_ref, re, im, chunk, n):
                def f(r0):
                    y_ref[pl.ds(r0, chunk), :] += (_dot(re[pl.ds(r0, chunk), :], cr_ref[d], _NN)
                                                   - _dot(im[pl.ds(r0, chunk), :], ci_ref[d], _NN))
                _rows_loop(n, chunk, f)

            z1 = jnp.zeros((1, PB), F32)
            proj(uc_ref, cre, cim, RCC, C)
            _, _, fr, fi = _full_scan(cre, cim, C // 8, ar1, ai1, z1, z1, rev)
            readout(yc_ref, cre, cim, RCC, C)
            proj(ul_ref, lre, lim, RC, L)
            _full_scan(lre, lim, L // 8, ar1, ai1, fr, fi, rev)
            readout(yl_ref, lre, lim, RC, L)

    return pl.pallas_call(
        body, name=name, grid=(nsb,),
        in_specs=[ul_s, uc_s, av, av, bd, bd, cd, cd, dv],
        out_specs=[ul_s, uc_s],
        out_shape=[jax.ShapeDtypeStruct((L, dm.SSM_W), F32), jax.ShapeDtypeStruct((C, dm.SSM_W), F32)],
        scratch_shapes=[pltpu.VMEM((L, PB), F32), pltpu.VMEM((L, PB), F32),
                        pltpu.VMEM((C, PB), F32), pltpu.VMEM((C, PB), F32)],
        compiler_params=_cparams(("arbitrary",)),
    )(ul, uc, a_re, a_im, bd_re, bd_im, cd_re, cd_im, dskip)


def _s5_scan_bwd(ul, uc, dyl, dyc, a_re, a_im, bd_re, bd_im, cd_re, cd_im, bdT_re, bdT_im, cdT_re, cdT_im,
                 dskip, dm, name, carry=None):
    PB, nsb, per, ul_s, uc_s, av, bd, cd, dv = _s5_specs(dm, STATE_BLOCK)
    L, C = dm.L, dm.C
    RC = _tile(L, 512, 8)
    RCC = _tile(C, 512, 8)
    bdT = pl.BlockSpec((2, PB, LANES), lambda j: (0, j, j // per))
    cdT = pl.BlockSpec((2, LANES, PB), lambda j: (0, j // per, j))
    dbd = pl.BlockSpec((2, 1, LANES, PB), lambda j: (0, j, 0, 0))
    dcd = pl.BlockSpec((2, 1, PB, LANES), lambda j: (0, j, 0, 0))

    def body(ul_ref, uc_ref, dyl_ref, dyc_ref, ar_ref, ai_ref, br_ref, bi_ref, cr_ref, ci_ref,
             brT_ref, biT_ref, crT_ref, ciT_ref, d_ref,
             dul_ref, duc_ref, dar_ref, dai_ref, dbr_ref, dbi_ref, dcr_ref, dci_ref, dd_ref,
             lre, lim, cre, cim, gre, gim, hre, him):
        j = pl.program_id(0)

        @pl.when(j % per == 0)
        def _():
            dul_ref[...] = dyl_ref[...] * d_ref[...]
            duc_ref[...] = dyc_ref[...] * d_ref[...]
            dd_ref[...] = (jnp.sum(dyl_ref[...] * ul_ref[...], axis=0, keepdims=True)
                           + jnp.sum(dyc_ref[...] * uc_ref[...], axis=0, keepdims=True))

        for d in range(2):
            rev = d == 1
            ar1, ai1 = ar_ref[d], ai_ref[d]
            z1 = jnp.zeros((1, PB), F32)

            def proj(u_ref, w_re, w_im, sign, re, im, chunk, n):
                def f(r0):
                    u = u_ref[pl.ds(r0, chunk), :]
                    re[pl.ds(r0, chunk), :] = _dot(u, w_re, _NN)
                    im[pl.ds(r0, chunk), :] = sign * _dot(u, w_im, _NN)
                _rows_loop(n, chunk, f)

            proj(uc_ref, br_ref[d], bi_ref[d], 1.0, cre, cim, RCC, C)
            icr, ici, fr, fi = _full_scan(cre, cim, C // 8, ar1, ai1, z1, z1, rev)
            proj(ul_ref, br_ref[d], bi_ref[d], 1.0, lre, lim, RC, L)
            ilr, ili, _, _ = _full_scan(lre, lim, L // 8, ar1, ai1, fr, fi, rev)
            proj(dyl_ref, crT_ref[d], ciT_ref[d], -1.0, gre, gim, RC, L)
            _, _, gfr, gfi = _full_scan(gre, gim, L // 8, ar1, -ai1, z1, z1, not rev)
            proj(dyc_ref, crT_ref[d], ciT_ref[d], -1.0, hre, him, RCC, C)
            _full_scan(hre, him, C // 8, ar1, -ai1, gfr, gfi, not rev)

            dar = jnp.zeros((8, PB), F32)
            dai = jnp.zeros((8, PB), F32)
            dbr = jnp.zeros((LANES, PB), F32)
            dbi = jnp.zeros((LANES, PB), F32)
            dcr = jnp.zeros((PB, LANES), F32)
            dci = jnp.zeros((PB, LANES), F32)
            for (u_ref, dy_ref, du_ref, sre, sim, are, aim, inr, ini, n, chunk) in (
                    (ul_ref, dyl_ref, dul_ref, lre, lim, gre, gim, ilr, ili, L, RC),
                    (uc_ref, dyc_ref, duc_ref, cre, cim, hre, him, icr, ici, C, RCC)):
                nst = n // 8

                def da_step(t, carry, sre=sre, sim=sim, are=are, aim=aim, nst=nst):
                    xr, xi = carry
                    tp = (t + 1) if rev else (t - 1)
                    o = pl.multiple_of(t * 8, 8)
                    op = pl.multiple_of(tp * 8, 8)
                    g_r, g_i = are[pl.ds(o, 8), :], aim[pl.ds(o, 8), :]
                    p_r, p_i = sre[pl.ds(op, 8), :], sim[pl.ds(op, 8), :]
                    return xr + g_r * p_r + g_i * p_i, xi + g_i * p_r - g_r * p_i

                lo_t, hi_t = (0, nst - 1) if rev else (1, nst)
                dar, dai = lax.fori_loop(lo_t, hi_t, da_step, (dar, dai))
                e0 = (nst - 1) * 8 if rev else 0
                g_r, g_i = are[pl.ds(e0, 8), :], aim[pl.ds(e0, 8), :]
                dar = dar + g_r * inr + g_i * ini
                dai = dai + g_i * inr - g_r * ini

                def mats(c, carry, u_ref=u_ref, dy_ref=dy_ref, du_ref=du_ref, sre=sre, sim=sim, are=are, aim=aim,
                         chunk=chunk):
                    xbr, xbi, xcr, xci = carry
                    r0 = pl.multiple_of(c * chunk, chunk)
                    u = u_ref[pl.ds(r0, chunk), :]
                    dy = dy_ref[pl.ds(r0, chunk), :]
                    g_r, g_i = are[pl.ds(r0, chunk), :], aim[pl.ds(r0, chunk), :]
                    du_ref[pl.ds(r0, chunk), :] += _dot(g_r, brT_ref[d], _NN) + _dot(g_i, biT_ref[d], _NN)
                    xbr = xbr + _dot(u, g_r, _TN)
                    xbi = xbi + _dot(u, g_i, _TN)
                    xcr = xcr + _dot(sre[pl.ds(r0, chunk), :], dy, _TN)
                    xci = xci - _dot(sim[pl.ds(r0, chunk), :], dy, _TN)
                    return xbr, xbi, xcr, xci

                dbr, dbi, dcr, dci = lax.fori_loop(0, n // chunk, mats, (dbr, dbi, dcr, dci))
            dar_ref[d] = jnp.sum(dar, axis=0, keepdims=True)
            dai_ref[d] = jnp.sum(dai, axis=0, keepdims=True)
            dbr_ref[d, 0] = dbr
            dbi_ref[d, 0] = dbi
            dcr_ref[d, 0] = dcr
            dci_ref[d, 0] = dci

    W, GP = dm.SSM_W, dm.GP
    body, cxs, c_out, c_scratch = _ride(body, 15, 9, 8, carry, nsb)
    any_spec = pl.BlockSpec(memory_space=pl.ANY)
    outs = pl.pallas_call(
        body, name=name, grid=(nsb,),
        in_specs=[ul_s, uc_s, ul_s, uc_s, av, av, bd, bd, cd, cd, bdT, bdT, cdT, cdT, dv] + [any_spec] * len(cxs),
        out_specs=[ul_s, uc_s, av, av, dbd, dbd, dcd, dcd, dv] + [any_spec] * len(cxs),
        out_shape=[jax.ShapeDtypeStruct((L, W), F32), jax.ShapeDtypeStruct((C, W), F32),
                   jax.ShapeDtypeStruct((2, 1, GP), F32), jax.ShapeDtypeStruct((2, 1, GP), F32),
                   jax.ShapeDtypeStruct((2, nsb, LANES, PB), F32), jax.ShapeDtypeStruct((2, nsb, LANES, PB), F32),
                   jax.ShapeDtypeStruct((2, nsb, PB, LANES), F32), jax.ShapeDtypeStruct((2, nsb, PB, LANES), F32),
                   jax.ShapeDtypeStruct((1, W), F32)] + c_out,
        scratch_shapes=[pltpu.VMEM((L, PB), F32), pltpu.VMEM((L, PB), F32),
                        pltpu.VMEM((C, PB), F32), pltpu.VMEM((C, PB), F32),
                        pltpu.VMEM((L, PB), F32), pltpu.VMEM((L, PB), F32),
                        pltpu.VMEM((C, PB), F32), pltpu.VMEM((C, PB), F32)] + c_scratch,
        compiler_params=_cparams(("arbitrary",)),
    )(ul, uc, dyl, dyc, a_re, a_im, bd_re, bd_im, cd_re, cd_im, bdT_re, bdT_im, cdT_re, cdT_im, dskip, *cxs)
    return (outs[:9], outs[9:]) if carry else outs


def _glu(y, w, b, name):
    T, W = y.shape
    tm = _tile(T, 544, 16)

    def body(y_ref, w_ref, b_ref, o_ref):
        g = _gelu(y_ref[...])
        o_ref[...] = (g * _sigmoid(_dot(g, w_ref[...], _NN) + b_ref[...])).astype(BF16)

    return pl.pallas_call(
        body, name=name, grid=(T // tm,),
        in_specs=[pl.BlockSpec((tm, W), lambda i: (i, 0)), pl.BlockSpec((W, W), lambda i: (0, 0)),
                  pl.BlockSpec((1, W), lambda i: (0, 0))],
        out_specs=pl.BlockSpec((tm, W), lambda i: (i, 0)),
        out_shape=jax.ShapeDtypeStruct((T, W), BF16),
        compiler_params=_cparams(("parallel",)),
    )(y, w, b)


def _glu_bwd(y, w, b, do, do_col, name):
    T, W = y.shape
    tm = _tile(T, 544, 16)

    def body(y_ref, w_ref, b_ref, do_ref, dy_ref, dw_ref, db_ref):
        i = pl.program_id(0)

        @pl.when(i == 0)
        def _():
            dw_ref[...] = jnp.zeros_like(dw_ref)
            db_ref[...] = jnp.zeros_like(db_ref)

        y = y_ref[...]
        do = do_ref[...]
        g = _gelu(y)
        s = _sigmoid(_dot(g, w_ref[...], _NN) + b_ref[...])
        dpre = do * g * s * (1.0 - s)
        db_ref[...] += jnp.sum(dpre, axis=0, keepdims=True)
        dw_ref[...] += _dot(g, dpre, _TN)
        dg = do * s + _dot(dpre, w_ref[...], _NT)
        dy_ref[...] = dg * _gelu_grad(y)

    row = pl.BlockSpec((tm, W), lambda i: (i, 0))
    wsp = pl.BlockSpec((W, W), lambda i: (0, 0))
    vec = pl.BlockSpec((1, W), lambda i: (0, 0))
    return pl.pallas_call(
        body, name=name, grid=(T // tm,),
        in_specs=[row, wsp, vec, pl.BlockSpec((tm, W), lambda i: (i, do_col))], out_specs=[row, wsp, vec],
        out_shape=[jax.ShapeDtypeStruct((T, W), F32), jax.ShapeDtypeStruct((W, W), F32),
                   jax.ShapeDtypeStruct((1, W), F32)],
        compiler_params=_cparams(("arbitrary",)),
    )(y, w, b, do)


def _loss_head(X, g, target, C, name):
    T, D = X.shape
    tm = _tile(math.gcd(C, T - C), 512, 16)
    nc = C // tm

    def body(x_ref, g_ref, t_ref, loss_ref, dx_ref, dg_ref):
        i = pl.program_id(0)

        @pl.when(i == 0)
        def _():
            loss_ref[...] = jnp.zeros_like(loss_ref)
            dg_ref[...] = jnp.zeros_like(dg_ref)

        @pl.when(i < nc)
        def _():
            dx_ref[...] = jnp.zeros_like(dx_ref)

        @pl.when(i >= nc)
        def _():
            x = x_ref[...]
            gv = g_ref[...]
            r = lax.rsqrt(jnp.mean(x * x, axis=-1, keepdims=True) + NORM_EPS)
            xh = x * r
            err = xh * gv - t_ref[...]
            loss_ref[...] += 0.5 * jnp.sum(jnp.mean(err * err, axis=-1, keepdims=True), axis=0, keepdims=True)
            dy = err * (1.0 / D)
            dg_ref[...] += jnp.sum(dy * xh, axis=0, keepdims=True)
            dxh = dy * gv
            dx_ref[...] = r * (dxh - xh * jnp.mean(dxh * xh, axis=-1, keepdims=True))

    row = pl.BlockSpec((tm, D), lambda i: (i, 0))
    return pl.pallas_call(
        body, name=name, grid=(T // tm,),
        in_specs=[row, pl.BlockSpec((1, D), lambda i: (0, 0)),
                  pl.BlockSpec((tm, D), lambda i: (jnp.maximum(i - nc, 0), 0))],
        out_specs=[pl.BlockSpec((1, 1), lambda i: (0, 0)), row, pl.BlockSpec((1, D), lambda i: (0, 0))],
        out_shape=[jax.ShapeDtypeStruct((1, 1), F32), jax.ShapeDtypeStruct((T, D), F32),
                   jax.ShapeDtypeStruct((1, D), F32)],
        compiler_params=_cparams(("arbitrary",)),
    )(X, g, target)


def _ada_fwd(cond, w, b, name):
    nl, D, Ns = w.shape
    tn = _tile(Ns, 1024, LANES)

    def body(c_ref, w_ref, b_ref, o_ref):
        c = c_ref[...]
        o_ref[0] = _dot(c * _sigmoid(c), w_ref[0], _NN) + b_ref[0]

    return pl.pallas_call(
        body, name=name, grid=(nl, Ns // tn),
        in_specs=[pl.BlockSpec((16, D), lambda l, j: (0, 0)), pl.BlockSpec((1, D, tn), lambda l, j: (l, 0, j)),
                  pl.BlockSpec((1, 1, tn), lambda l, j: (l, 0, j))],
        out_specs=pl.BlockSpec((1, 16, tn), lambda l, j: (l, 0, j)),
        out_shape=jax.ShapeDtypeStruct((nl, 16, Ns), F32),
        compiler_params=_cparams(("parallel", "parallel")),
    )(cond, w, b)


def _ada_bwd(cond, w, dmod, name):
    nl, D, Ns = w.shape
    tn = _tile(Ns, 1024, LANES)

    def body(c_ref, w_ref, dm_ref, dw_ref, da_ref):
        j = pl.program_id(1)

        @pl.when(j == 0)
        def _():
            da_ref[...] = jnp.zeros_like(da_ref)

        c = c_ref[...]
        dw_ref[0] = _dot(c * _sigmoid(c), dm_ref[0], _TN)
        da_ref[0] += _dot(dm_ref[0], w_ref[0], _NT)

    return pl.pallas_call(
        body, name=name, grid=(nl, Ns // tn),
        in_specs=[pl.BlockSpec((16, D), lambda l, j: (0, 0)), pl.BlockSpec((1, D, tn), lambda l, j: (l, 0, j)),
                  pl.BlockSpec((1, 16, tn), lambda l, j: (l, 0, j))],
        out_specs=[pl.BlockSpec((1, D, tn), lambda l, j: (l, 0, j)), pl.BlockSpec((1, 16, D), lambda l, j: (l, 0, 0))],
        out_shape=[jax.ShapeDtypeStruct((nl, D, Ns), F32), jax.ShapeDtypeStruct((nl, 16, D), F32)],
        compiler_params=_cparams(("parallel", "arbitrary")),
    )(cond, w, dmod)


def _cctx_grad(c_ctx, dact4, name):
    nch, nl, _, D = dact4.shape

    def body(c_ref, da_ref, o_ref):
        acc = jnp.zeros((1, D), F32)
        for ch in range(nch):
            for l in range(nl):
                acc = acc + da_ref[ch, l, 8:9, :]
        c = c_ref[...]
        s = _sigmoid(c)
        o_ref[...] = acc * (s + c * s * (1.0 - s))

    return pl.pallas_call(body, name=name, out_shape=jax.ShapeDtypeStruct((1, D), F32),
                          compiler_params=_cparams())(c_ctx, dact4)


def _adamw(w, g, m, v, name):
    R, W = w.shape
    tr = _tile(R, max(16, (1 << 19) // W // 16 * 16), 16 if g.dtype == BF16 else 8)
    c1 = 1.0 - ADAM_B1 ** ADAM_STEP
    c2 = 1.0 - ADAM_B2 ** ADAM_STEP

    def body(w_ref, g_ref, m_ref, v_ref, d_ref, nm_ref, nv_ref, g32_ref):
        g = g_ref[...].astype(F32)
        g32_ref[...] = g
        m = ADAM_B1 * m_ref[...] + (1.0 - ADAM_B1) * g
        v = ADAM_B2 * v_ref[...] + (1.0 - ADAM_B2) * (g * g)
        nm_ref[...] = m
        nv_ref[...] = v
        d_ref[...] = -ADAM_LR * ((m / c1) / (jnp.sqrt(v / c2) + ADAM_EPS) + ADAM_WD * w_ref[...])

    blk = pl.BlockSpec((tr, W), lambda i: (i, 0))
    s = jax.ShapeDtypeStruct((R, W), F32)
    return pl.pallas_call(
        body, name=name, grid=(R // tr,), in_specs=[blk] * 4, out_specs=[blk] * 4, out_shape=[s, s, s, s],
        compiler_params=_cparams(("parallel",)),
    )(w, g, m, v)


def _adamw_nd(w, g, m, v, name):
    shp = w.shape
    two = (math.prod(shp[:-1]), shp[-1])
    outs = _adamw(w.reshape(two), g.reshape(two), m.reshape(two), v.reshape(two), name)
    return [o.reshape(shp) for o in outs]


def _interleave(a, n):
    return a.reshape(SCAN_SEGMENTS, n // SCAN_SEGMENTS, -1).transpose(1, 0, 2).reshape(n, -1)


def _deinterleave(a, n):
    return a.reshape(n // SCAN_SEGMENTS, SCAN_SEGMENTS, -1).transpose(1, 0, 2).reshape(n, -1)


def _blockdiag_in(bbT, dm):
    eye = jnp.eye(dm.SG, dtype=F32)
    b4 = bbT.reshape(2, SSM_GROUP, dm.SG, SSM_STATE)
    return jnp.einsum("dhgp,gk->dghkp", b4, eye).reshape(2, dm.SSM_W, dm.GP)


def _blockdiag_in_T(dbd, dm):
    gpb = STATE_BLOCK // SSM_STATE
    per = (LANES // SSM_GROUP) // gpb
    d8 = dbd.reshape(2, dm.SSM_W // LANES, per, per, gpb, SSM_GROUP, gpb, SSM_STATE)
    x = jnp.einsum("duaabhbp->duabhp", d8).reshape(2, dm.SG, SSM_GROUP, SSM_STATE)
    return x.transpose(0, 2, 1, 3).reshape(2, SSM_GROUP, dm.GP)


def _blockdiag_out(c, dm):
    eye = jnp.eye(dm.SG, dtype=F32)
    return jnp.einsum("dghp,gk->dgpkh", c, eye).reshape(2, dm.GP, dm.SSM_W)


def _blockdiag_out_T(dcd, dm):
    gpb = STATE_BLOCK // SSM_STATE
    per = (LANES // SSM_GROUP) // gpb
    d8 = dcd.reshape(2, dm.SSM_W // LANES, per, gpb, SSM_STATE, per, gpb, SSM_GROUP)
    return jnp.einsum("duabpabh->duabhp", d8).reshape(2, dm.SG, SSM_GROUP, SSM_STATE)


_BIG = ("w_in", "w_out", "ssm_w_glu", "ffn_w_up", "ffn_w_down")
_SHARD_AXIS = {"w_in": 2, "w_out": 1, "ssm_w_glu": 1, "ffn_w_up": 2, "ffn_w_down": 1}
_SMALL = ("g_mix", "g_ffn", "attn_sink", "ssm_lambda_re", "ssm_lambda_im", "ssm_log_dt", "ssm_b_re", "ssm_b_im",
          "ssm_c_re", "ssm_c_im", "ssm_d", "ssm_b_glu", "gmlp_ln_g", "gmlp_ln_b", "gmlp_w_s", "gmlp_b_s",
          "ffn_conv_b", "g_final", "ffn_conv_w")
_WEIGHTS = ("c_ctx", "w_ada", "b_ada", "g_mix", "g_ffn", "w_in", "w_out", "attn_sink", "ssm_lambda_re",
            "ssm_lambda_im", "ssm_log_dt", "ssm_b_re", "ssm_b_im", "ssm_c_re", "ssm_c_im", "ssm_d", "ssm_w_glu",
            "ssm_b_glu", "gmlp_ln_g", "gmlp_ln_b", "gmlp_w_s", "gmlp_b_s", "ffn_w_up", "ffn_conv_w", "ffn_conv_b",
            "ffn_w_down", "g_final")


def _step(P, M, V, x, c, ctx, loss_target):
    dm = _Dims()
    D, T, C, L = dm.D, dm.T, dm.C, dm.L
    mx, my, mc = lax.axis_index("x"), lax.axis_index("y"), lax.axis_index("c")
    chip = 2 * mx + my
    dev = 2 * chip + mc

    conv_sh = P["ffn_conv_w"].shape
    small_f = _pack([c, P["ffn_conv_w"]], F32, 8)
    small_all = _allgather8(small_f, "ag_small_fwd")
    c_all, conv_all = [], []
    for d in range(N_DEV):
        cd_, cw_ = _unpack(small_all[d], [(1, D), conv_sh])
        c_all.append(cd_)
        conv_all.append(cw_)
    conv_w = jnp.concatenate([conv_all[2 * j] for j in range(N_CHIPS)], axis=2)
    cond = jnp.concatenate(c_all + [P["c_ctx"].reshape(1, D), jnp.zeros((16 - N_DEV - 1, D), F32)], axis=0)

    n_sh = P["w_ada"].shape[2]
    b_sh = lax.dynamic_slice_in_dim(P["b_ada"], chip * n_sh, n_sh, axis=1)[:, None, :]
    mods_sh = _ada_fwd(cond, P["w_ada"], b_sh, "ada_fwd")
    mods4 = _comm(mods_sh, group="chips", by_peer=False, name="ag_mods")
    mods = jnp.concatenate([mods4[j] for j in range(N_CHIPS)], axis=-1)
    mod_lat = lax.dynamic_index_in_dim(mods, dev, axis=1, keepdims=False)
    mod_ctx = mods[:, N_DEV]
    modp = jnp.stack([mod_ctx, mod_lat], axis=1).reshape(DEPTH, 2, N_MOD, D)

    wb = {n: P[n].astype(BF16) for n in _BIG}

    def layer_weights(g):
        return dict(w_in=jnp.concatenate([g["w_in"][j] for j in range(N_CHIPS)], axis=1),
                    w_glu=g["ssm_w_glu"].reshape(dm.SSM_W, dm.SSM_W),
                    w_out=_Mat(g["w_out"], axis=0), w_dn=_Mat(g["ffn_w_down"], axis=0),
                    w_up=_Mat(g["ffn_w_up"], axis=1))

    g0 = _comm_alone("chips_ag", [wb[n][0] for n in _BIG], "ag_w_chips")
    Wl = [layer_weights(dict(zip(_BIG, g0)))]
    cos, sin = _rope_tables(dm)

    X = jnp.concatenate([ctx.reshape(C, D), x.reshape(L, D)], axis=0)
    saved = []
    for l in range(DEPTH):
        mp = modp[l]
        sh_a, sc_a, gt_a, sh_f, sc_f, gt_f = [mp[:, k] for k in range(N_MOD)]
        w_in, w_out, w_glu, w_up, w_dn = [Wl[l][k] for k in ("w_in", "w_out", "w_glu", "w_up", "w_dn")]
        g_mix, g_ffn = P["g_mix"][l][None], P["g_ffn"][l][None]

        h = _normmod(X, g_mix, sh_a, sc_a, C, "normmod_a")
        nxt = {}

        def riding(*names):
            return ("chips_ag", [wb[n][l + 1] for n in names]) if l + 1 < DEPTH else None

        def landed(names, gathered):
            nxt.update(zip(names, gathered) if l + 1 < DEPTH else ())

        z = _matmul(h, w_in, mode="nn", name="mm_in", carry=riding("w_in"))
        if l + 1 < DEPTH:
            z, got = z
            landed(("w_in",), got)
        q, k2, v2 = _attn_prep(z, cos, sin, dm, "attn_prep")
        sinkcol = jnp.repeat(P["attn_sink"][l].reshape(2, GQA_RATIO), ATTN_BLOCK, axis=1)[..., None]
        o_attn = _attention(q, k2, v2, sinkcol, dm, "attention")
        u = z[:, dm.OFF_S:dm.OFF_GU]
        ul, uc = _interleave(u[C:], L), _interleave(u[:C], C)
        lam_re = P["ssm_lambda_re"][l].reshape(2, 1, dm.GP)
        lam_im = P["ssm_lambda_im"][l].reshape(2, 1, dm.GP)
        ldt = jnp.repeat(P["ssm_log_dt"][l], SSM_STATE, axis=-1).reshape(2, 1, dm.GP)
        bT_re = P["ssm_b_re"][l].reshape(2, dm.GP, SSM_GROUP).transpose(0, 2, 1)
        bT_im = P["ssm_b_im"][l].reshape(2, dm.GP, SSM_GROUP).transpose(0, 2, 1)
        a_re, a_im, bbT_re, bbT_im = _s5_disc(lam_re, lam_im, ldt, bT_re, bT_im, "s5_disc")
        bd_re, bd_im = _blockdiag_in(bbT_re, dm).astype(BF16), _blockdiag_in(bbT_im, dm).astype(BF16)
        cd_re = _blockdiag_out(P["ssm_c_re"][l], dm).astype(BF16)
        cd_im = _blockdiag_out(P["ssm_c_im"][l], dm).astype(BF16)
        dskip = P["ssm_d"][l][None]
        yl, yc = _s5_scan(ul, uc, a_re, a_im, bd_re, bd_im, cd_re, cd_im, dskip, dm, "s5_scan")
        y = jnp.concatenate([_deinterleave(yc, C), _deinterleave(yl, L)], axis=0)
        b_glu = P["ssm_b_glu"][l][None]
        o_ssm = _glu(y, w_glu, b_glu, "glu")
        ln_g, ln_b = P["gmlp_ln_g"][l][None], P["gmlp_ln_b"][l][None]
        w_s = P["gmlp_w_s"][l].astype(BF16)
        b_s = P["gmlp_b_s"][l][..., None]
        o_gmlp = _gmlp(z, ln_g, ln_b, w_s, b_s, dm, "gmlp")
        o_cat = jnp.concatenate([o_attn, o_ssm, o_gmlp], axis=-1)
        mix, X1, *got = _matmul_residual(o_cat, w_out, X, gt_a, C, "mm_out", carry=riding("w_out", "ssm_w_glu"))
        landed(("w_out", "ssm_w_glu"), got[0] if got else ())
        h2 = _normmod(X1, g_ffn, sh_f, sc_f, C, "normmod_f")
        up = _matmul(h2, w_up, mode="nn", name="mm_up", out_dtype=BF16, out_split=(1, 2),
                     carry=riding("ffn_w_up"))
        if l + 1 < DEPTH:
            up, got = up
            landed(("ffn_w_up",), got)
        cw, cb = conv_w[l], P["ffn_conv_b"][l][None]
        act = _convact(up, cw, cb, C, "convact")
        ffn, X2, *got = _matmul_residual(act, w_dn, X1, gt_f, C, "mm_down", carry=riding("ffn_w_down"))
        landed(("ffn_w_down",), got[0] if got else ())
        if l + 1 < DEPTH:
            Wl.append(layer_weights(nxt))
        saved.append(dict(X=X, h=h, z=z, q=q, k2=k2, v2=v2, sinkcol=sinkcol, ul=ul, uc=uc, a_re=a_re, a_im=a_im,
                          bd_re=bd_re, bd_im=bd_im, cd_re=cd_re, cd_im=cd_im, y=y, o_cat=o_cat, mix=mix, X1=X1, h2=h2,
                          up=up, act=act, ffn=ffn, lam_re=lam_re, lam_im=lam_im, ldt=ldt, bT_re=bT_re,
                          bT_im=bT_im))
        X = X2

    loss_l, dX, dg_final = _loss_head(X, P["g_final"][None], loss_target.reshape(L, D), C, "loss_head")
    loss = lax.psum(loss_l[0, 0], ("x", "y", "c"))

    G = {n: [None] * DEPTH for n in _WEIGHTS if n not in ("c_ctx", "w_ada", "b_ada", "g_final")}
    dmod = [None] * DEPTH
    Gred = {n: [None] * DEPTH for n in _BIG}
    ready = None

    def hosted(fn, store, kind, src, names):
        if src is None:
            return fn(None)
        out, got = fn((kind, [src[n] for n in names]))
        store.update(zip(names, got))
        return out

    def reduce_parts(parts, names=_BIG):
        return {n: _sum_slots(parts[n], BF16, "sum_all") for n in names}

    def file_shared(shared, layer):
        for n in _BIG:
            Gred[n][layer] = shared[n].reshape(P[n].shape[1:])

    for l in reversed(range(DEPTH)):
        S = saved[l]
        parts, shared = {}, {}
        mp = modp[l]
        sh_a, sc_a, gt_a, sh_f, sc_f, gt_f = [mp[:, k] for k in range(N_MOD)]
        w_in, w_out, w_glu, w_up, w_dn = [Wl[l][k] for k in ("w_in", "w_out", "w_glu", "w_up", "w_dn")]
        g_mix, g_ffn = P["g_mix"][l][None], P["g_ffn"][l][None]
        cw, cb = conv_w[l], P["ffn_conv_b"][l][None]

        dffn, dgt_f = _gate_bwd(dX, S["ffn"], gt_f, C, "gate_bwd")
        dact = hosted(lambda c: _matmul(dffn, w_dn, mode="nt", name="mm_down_dx", out_dtype=BF16, carry=c),
                      parts, "all_rs", ready, ("w_in",))
        G["ffn_w_down"][l] = hosted(
            lambda c: _matmul(S["act"], dffn, mode="tn", name="mm_down_dw", out_dtype=BF16, out_split=(0, N_CHIPS),
                              tm_cap=1408, tk_cap=2176, carry=c), parts, "all_rs", ready, ("w_out", "ssm_w_glu"))
        dup, dcw, dcb = _convact_bwd(S["up"], cw, cb, dact, C, "convact_bwd")
        G["ffn_conv_w"][l], G["ffn_conv_b"][l] = dcw, dcb[0]
        dh2 = hosted(lambda c: _matmul(_Mat(dup, axis=1), w_up, mode="nt", name="mm_up_dx", tk_cap=2816, carry=c),
                     parts, "all_rs", ready, ("ffn_w_down",))
        G["ffn_w_up"][l] = _matmul(S["h2"], _Mat(dup, axis=1), mode="tn", name="mm_up_dw", out_dtype=BF16,
                                   out_split=(1, N_CHIPS), tk_cap=2176)
        red = reduce_parts(parts, ("w_in", "w_out", "ssm_w_glu", "ffn_w_down")) if ready is not None else None
        dX1, dg_f, dsh_f, dsc_f = _normmod_bwd(S["X1"], g_ffn, sc_f, dh2, dX, C, "normmod_bwd")
        G["g_ffn"][l] = dg_f[0]
        dmix, dgt_a = _gate_bwd(dX1, S["mix"], gt_a, C, "gate_bwd")
        do = hosted(lambda c: _matmul(dmix, w_out, mode="nt", name="mm_out_dx", carry=c),
                    shared, "sib_ag", red, ("w_in", "w_out", "ssm_w_glu"))
        G["w_out"][l] = hosted(
            lambda c: _matmul(S["o_cat"], dmix, mode="tn", name="mm_out_dw", out_dtype=BF16, out_split=(0, N_CHIPS),
                              tk_cap=2176, carry=c), shared, "sib_ag", red, ("ffn_w_down",))
        do_attn = do_ssm = do_gmlp = do
        ln_g, ln_b = P["gmlp_ln_g"][l][None], P["gmlp_ln_b"][l][None]
        w_s = P["gmlp_w_s"][l].astype(BF16)
        b_s = P["gmlp_b_s"][l][..., None]
        dz_g, dlg, dlb, dws, dbs = _gmlp_bwd(S["z"], ln_g, ln_b, w_s, b_s, do_gmlp, dm, "gmlp_bwd")
        G["gmlp_ln_g"][l], G["gmlp_ln_b"][l], G["gmlp_w_s"][l], G["gmlp_b_s"][l] = dlg[0], dlb[0], dws, dbs[..., 0]
        b_glu = P["ssm_b_glu"][l][None]
        dy, dwglu, dbglu = _glu_bwd(S["y"], w_glu, b_glu, do_ssm, dm.ATTN_W // dm.SSM_W, "glu_bwd")
        G["ssm_w_glu"][l], G["ssm_b_glu"][l] = dwglu.astype(BF16).reshape(N_CHIPS, -1, dm.SSM_W), dbglu[0]
        dyl, dyc = _interleave(dy[C:], L), _interleave(dy[:C], C)
        tr = lambda a: jnp.swapaxes(a, 1, 2)
        dskip = P["ssm_d"][l][None]
        (dul, duc, da_re, da_im, dbd_re, dbd_im, dcd_re, dcd_im, dd) = hosted(
            lambda c: _s5_scan_bwd(
                S["ul"], S["uc"], dyl, dyc, S["a_re"], S["a_im"], S["bd_re"], S["bd_im"], S["cd_re"], S["cd_im"],
                tr(S["bd_re"]), tr(S["bd_im"]), tr(S["cd_re"]), tr(S["cd_im"]), dskip, dm, "s5_scan_bwd", carry=c),
            parts, "all_rs", ready, ("ffn_w_up",))
        red_up = reduce_parts(parts, ("ffn_w_up",)) if ready is not None else None
        dz_s = jnp.concatenate([_deinterleave(duc, C), _deinterleave(dul, L)], axis=0)
        dlr, dli, dldt, dbTr, dbTi = _s5_disc_bwd(S["lam_re"], S["lam_im"], S["ldt"], S["bT_re"], S["bT_im"],
                                                  da_re, da_im, _blockdiag_in_T(dbd_re, dm),
                                                  _blockdiag_in_T(dbd_im, dm), "s5_disc_bwd")
        G["ssm_lambda_re"][l] = dlr.reshape(2, dm.SG, SSM_STATE)
        G["ssm_lambda_im"][l] = dli.reshape(2, dm.SG, SSM_STATE)
        G["ssm_log_dt"][l] = jnp.sum(dldt.reshape(2, dm.SG, SSM_STATE), axis=-1)
        G["ssm_b_re"][l] = dbTr.transpose(0, 2, 1).reshape(2, dm.SG, SSM_STATE, SSM_GROUP)
        G["ssm_b_im"][l] = dbTi.transpose(0, 2, 1).reshape(2, dm.SG, SSM_STATE, SSM_GROUP)
        G["ssm_c_re"][l] = _blockdiag_out_T(dcd_re, dm)
        G["ssm_c_im"][l] = _blockdiag_out_T(dcd_im, dm)
        G["ssm_d"][l] = dd[0]
        dq, pk, pv, dkc, dvc, dsk = hosted(
            lambda c: _attention_bwd(S["q"], S["k2"], S["v2"], S["sinkcol"], do_attn, dm, "attention_bwd", carry=c),
            shared, "sib_ag", red_up, ("ffn_w_up",))
        G["attn_sink"][l] = dsk[:, :, 0].reshape(dm.NH)
        dk2 = _kv_reduce(pk, dkc, dm, "kv_reduce")
        dv2 = _kv_reduce(pv, dvc, dm, "kv_reduce")
        dz_a = _attn_prep_bwd(dq, dk2, dv2, cos, sin, dm, "attn_prep_bwd")
        dz = jnp.concatenate([dz_a, dz_s, dz_g], axis=-1).astype(BF16)
        dh = _matmul(dz, w_in, mode="nt", name="mm_in_dx")
        dw_in = _matmul(S["h"], dz, mode="tn", name="mm_in_dw", out_dtype=BF16, tk_cap=2176)
        n_in = dm.N_IN // N_CHIPS
        G["w_in"][l] = jnp.stack([dw_in[:, j * n_in:(j + 1) * n_in] for j in range(N_CHIPS)])
        if ready is not None:
            file_shared(shared, l + 1)
        ready = {n: G[n][l] for n in _BIG}
        dX, dg_a, dsh_a, dsc_a = _normmod_bwd(S["X"], g_mix, sc_a, dh, dX1, C, "normmod_bwd")
        G["g_mix"][l] = dg_a[0]
        dmod[l] = jnp.stack([dsh_a, dsc_a, dgt_a, dsh_f, dsc_f, dgt_f], axis=1)

    grad_x = dX[C:].reshape(1, L, D)
    for n in _BIG:
        G.pop(n)
    parts = dict(zip(_BIG, _comm_alone("all_rs", [ready[n] for n in _BIG], "rs_all")))
    red = reduce_parts(parts)
    file_shared(dict(zip(_BIG, _comm_alone("sib_ag", [red[n] for n in _BIG], "rs_share"))), 0)
    G = {n: jnp.stack(v) for n, v in G.items()}
    G["g_final"] = dg_final[0]
    dmod = jnp.stack(dmod)

    small_shapes = [G[n].shape for n in _SMALL]
    gs_pack = _pack([G[n] for n in _SMALL], F32, 16 * N_DEV)
    r8 = gs_pack.shape[0] // N_DEV
    gs_parts = _comm(gs_pack.reshape(N_DEV, r8, PACK_W), group="all", by_peer=True, name="rs_small")
    gs_sum = _allgather8(_sum_slots(gs_parts, F32, "sum_small"), "ag_small_red").reshape(-1, PACK_W)
    for n, a in zip(_SMALL, _unpack(gs_sum, small_shapes)):
        G[n] = a
    allp = _allgather8(_pack([dmod], F32, 16), "ag_dmod")
    dmod_all = allp.reshape(N_DEV, -1)[:, :DEPTH * 2 * N_MOD * D]
    dmod_all = dmod_all.reshape(N_DEV, DEPTH, 2, N_MOD * D)
    dmod_ctx = _sum_slots(dmod_all[:, :, 0].reshape(N_DEV, DEPTH, N_MOD * D), F32, "sum_dmod_ctx")
    dmod16 = jnp.concatenate([jnp.swapaxes(dmod_all[:, :, 1], 0, 1), dmod_ctx[:, None],
                              jnp.zeros((DEPTH, 16 - N_DEV - 1, N_MOD * D), F32)], axis=1)
    G["b_ada"] = _sum_slots(jnp.swapaxes(dmod16, 0, 1)[:N_DEV + 1], F32, "sum_b_ada")
    dmod16_sh = lax.dynamic_slice_in_dim(dmod16, chip * n_sh, n_sh, axis=2)
    G["w_ada"], dact = _ada_bwd(cond, P["w_ada"], dmod16_sh, "ada_bwd")
    dact4 = _comm(dact, group="chips", by_peer=False, name="ag_dact")
    G["c_ctx"] = _cctx_grad(P["c_ctx"].reshape(1, D), dact4, "cctx_grad")[0]
    conv_cols = conv_sh[2]
    G["ffn_conv_w"] = lax.dynamic_slice_in_dim(G["ffn_conv_w"], chip * conv_cols, conv_cols, axis=2)

    for n in _BIG:
        G[n] = jnp.stack(Gred[n])

    delta, new_m, new_v = {}, {}, {}
    for n in _BIG + ("w_ada",):
        delta[n], new_m[n], new_v[n], G[n] = _adamw_nd(P[n], G[n], M[n], V[n], "adamw_" + n)
    rest = [n for n in _WEIGHTS if n not in _BIG + ("w_ada",)]
    shapes = [P[n].shape for n in rest]
    packed = [_pack([d[n] for n in rest], F32, 8) for d in (P, G, M, V)]
    outs = _adamw(*packed, "adamw_small")[:3]
    for dst, buf in zip((delta, new_m, new_v), outs):
        for n, a in zip(rest, _unpack(buf, shapes)):
            dst[n] = a
    return loss, grad_x, G, delta, new_m, new_v


def kernel(x, c, ctx, c_ctx, w_ada, b_ada, g_mix, g_ffn, w_in, w_out, attn_sink, ssm_lambda_re, ssm_lambda_im, ssm_log_dt, ssm_b_re, ssm_b_im, ssm_c_re, ssm_c_im, ssm_d, ssm_w_glu, ssm_b_glu, gmlp_ln_g, gmlp_ln_b, gmlp_w_s, gmlp_b_s, ffn_w_up, ffn_conv_w, ffn_conv_b, ffn_w_down, g_final, loss_target, m_c_ctx, m_w_ada, m_b_ada, m_g_mix, m_g_ffn, m_w_in, m_w_out, m_attn_sink, m_ssm_lambda_re, m_ssm_lambda_im, m_ssm_log_dt, m_ssm_b_re, m_ssm_b_im, m_ssm_c_re, m_ssm_c_im, m_ssm_d, m_ssm_w_glu, m_ssm_b_glu, m_gmlp_ln_g, m_gmlp_ln_b, m_gmlp_w_s, m_gmlp_b_s, m_ffn_w_up, m_ffn_conv_w, m_ffn_conv_b, m_ffn_w_down, m_g_final, v_c_ctx, v_w_ada, v_b_ada, v_g_mix, v_g_ffn, v_w_in, v_w_out, v_attn_sink, v_ssm_lambda_re, v_ssm_lambda_im, v_ssm_log_dt, v_ssm_b_re, v_ssm_b_im, v_ssm_c_re, v_ssm_c_im, v_ssm_d, v_ssm_w_glu, v_ssm_b_glu, v_gmlp_ln_g, v_gmlp_ln_b, v_gmlp_w_s, v_gmlp_b_s, v_ffn_w_up, v_ffn_conv_w, v_ffn_conv_b, v_ffn_w_down, v_g_final):
    env = locals()
    P = {n: env[n] for n in _WEIGHTS}
    M = {n: env["m_" + n] for n in _WEIGHTS}
    V = {n: env["v_" + n] for n in _WEIGHTS}
    loss, grad_x, G, delta, new_m, new_v = _step(P, M, V, x, c, ctx, loss_target)
    return (loss, grad_x, *[G[n] for n in _WEIGHTS], *[delta[n] for n in _WEIGHTS],
            *[new_m[n] for n in _WEIGHTS], *[new_v[n] for n in _WEIGHTS])
```

```python
import functools
import math

import jax
import jax.numpy as jnp
from jax import lax
from jax.experimental import pallas as pl
from jax.experimental.pallas import tpu as pltpu

F32 = jnp.float32
BF16 = jnp.bfloat16

D_MODEL = 2048
SEQ = 4096
DEPTH = 4
CTX_LEN = 256
GRID_W = 64
HEAD_DIM = 64
GQA_RATIO = 8
WINDOW = 128
ATTN_BLOCK = 128
ROPE_BASE = 10000.0
SSM_GROUP = 16
SSM_STATE = 64
GMLP_CHUNK = 128
GMLP_GROUP_W = 128
N_MOD = 6
NORM_EPS = 1e-6
ADAM_LR = 0.001
ADAM_B1 = 0.9
ADAM_B2 = 0.999
ADAM_EPS = 1e-08
ADAM_WD = 0.01
ADAM_STEP = 10

N_CHIPS = 4
N_DEV = 8
SCAN_SEGMENTS = 8
STATE_BLOCK = 256
STATE_BLOCK_FWD = 512
LANES = 128
PACK_W = 1024
COMM_CHUNKS = (16, 8, 4, 2)
SUM_BLOCK_BYTES = 8 << 20
VMEM_LIMIT_MB = 56
_GELU_K = math.sqrt(2.0 / math.pi)
_GELU_C = 0.044715


class _Dims:
    def __init__(self):
        self.D = D_MODEL
        self.L = SEQ
        self.C = CTX_LEN
        self.T = SEQ + CTX_LEN
        self.ATTN_W = D_MODEL // 2
        self.SSM_W = D_MODEL // 4
        self.GMLP_W = D_MODEL - self.ATTN_W - self.SSM_W
        self.NH = self.ATTN_W // HEAD_DIM
        self.NKV = self.NH // GQA_RATIO
        self.KV_W = self.NKV * HEAD_DIM
        self.SG = self.SSM_W // SSM_GROUP
        self.GP = self.SG * SSM_STATE
        self.GG = self.GMLP_W // GMLP_GROUP_W
        self.DFF = ((8 * D_MODEL // 3 + 255) // 256) * 256
        self.OFF_K = self.ATTN_W
        self.OFF_V = self.OFF_K + self.KV_W
        self.OFF_S = self.OFF_V + self.KV_W
        self.OFF_GU = self.OFF_S + self.SSM_W
        self.OFF_GV = self.OFF_GU + self.GMLP_W
        self.N_IN = self.OFF_GV + self.GMLP_W
        assert self.NKV == 2 and self.KV_W == LANES
        assert self.C % (SCAN_SEGMENTS * 8) == 0 and self.L % (SCAN_SEGMENTS * 8) == 0
        assert self.C % ATTN_BLOCK == 0 and self.L % ATTN_BLOCK == 0


def _tile(n, cap, mult):
    best = None
    for t in range(mult, min(n, cap) + 1, mult):
        if n % t == 0:
            best = t
    return best if best is not None else n


def _cparams(sem=None):
    kw = dict(vmem_limit_bytes=VMEM_LIMIT_MB << 20)
    if sem is not None:
        kw["dimension_semantics"] = sem
    return pltpu.CompilerParams(**kw)


def _gelu(x):
    return 0.5 * x * (1.0 + jnp.tanh(_GELU_K * (x + _GELU_C * x * x * x)))


def _gelu_grad(x):
    t = jnp.tanh(_GELU_K * (x + _GELU_C * x * x * x))
    return 0.5 * (1.0 + t) + 0.5 * x * (1.0 - t * t) * _GELU_K * (1.0 + 3.0 * _GELU_C * x * x)


def _sigmoid(x):
    return 1.0 / (1.0 + jnp.exp(-x))


def _dot(a, b, dims):
    return lax.dot_general(a.astype(BF16), b.astype(BF16), (dims, ((), ())), preferred_element_type=F32)


_NN = ((1,), (0,))
_NT = ((1,), (1,))
_TN = ((0,), (0,))


def _comm(x, *, group, by_peer, name):
    n = {"chips": N_CHIPS, "all": N_DEV}[group]
    blk = x.shape[1:] if by_peer else x.shape
    npeer = n - 1
    W = blk[-1]
    rows = math.prod(blk[:-1])
    x2 = x.reshape((n, rows, W) if by_peer else (rows, W))
    nchunk = next((k for k in COMM_CHUNKS if rows % (k * 16) == 0), 1)
    cr = rows // nchunk

    def body(x_ref, o_ref, send_sems, recv_sems, local_sem):
        mx, my, mc = lax.axis_index("x"), lax.axis_index("y"), lax.axis_index("c")
        if group == "chips":
            me = 2 * mx + my
            peers = [((mx ^ fx, my ^ fy, mc), 2 * (mx ^ fx) + (my ^ fy)) for fx, fy in ((1, 0), (0, 1), (1, 1))]
        else:
            me = 4 * mx + 2 * my + mc
            peers = [((mx ^ fx, my ^ fy, mc ^ fc), 4 * (mx ^ fx) + 2 * (my ^ fy) + (mc ^ fc))
                     for fx in (0, 1) for fy in (0, 1) for fc in (0, 1) if fx + fy + fc]

        def src_of(pid):
            return x_ref.at[pid] if by_peer else x_ref

        def chunk(ref, ch):
            return ref.at[pl.ds(ch * cr, cr), :]

        def remote(k, dev, pid, ch=None):
            s, d = src_of(pid), o_ref.at[me]
            if ch is not None:
                s, d = chunk(s, ch), chunk(d, ch)
            return pltpu.make_async_remote_copy(src_ref=s, dst_ref=d, send_sem=send_sems.at[k],
                                                recv_sem=recv_sems.at[k], device_id=dev,
                                                device_id_type=pl.DeviceIdType.MESH)

        for ch in range(nchunk):
            pltpu.make_async_copy(chunk(src_of(me), ch), chunk(o_ref.at[me], ch), local_sem).start()
            for k, (dev, pid) in enumerate(peers):
                remote(k, dev, pid, ch).start()
        for k, (dev, pid) in enumerate(peers):
            remote(k, dev, pid).wait_recv()
        for k, (dev, pid) in enumerate(peers):
            remote(k, dev, pid).wait_send()
        pltpu.make_async_copy(src_of(me), o_ref.at[me], local_sem).wait()

    out = pl.pallas_call(
        body, name=name,
        out_shape=jax.ShapeDtypeStruct((n, rows, W), x.dtype),
        in_specs=[pl.BlockSpec(memory_space=pl.ANY)],
        out_specs=pl.BlockSpec(memory_space=pl.ANY),
        scratch_shapes=[pltpu.SemaphoreType.DMA((npeer,)), pltpu.SemaphoreType.DMA((npeer,)),
                        pltpu.SemaphoreType.DMA],
    )(x2)
    return out.reshape((n,) + tuple(blk))


def _carried_comm(kind, xs):
    nt = len(xs)
    masks = {"chips_ag": [(1, 0, 0), (0, 1, 0), (1, 1, 0)], "sib_ag": [(0, 0, 1)],
             "all_rs": [(fx, fy, fc) for fx in (0, 1) for fy in (0, 1) for fc in (0, 1) if fx + fy + fc]}[kind]
    npeer = len(masks)

    def copies(x_refs, o_refs, sems):
        send_sems, recv_sems, local_sems = sems
        mx, my, mc = lax.axis_index("x"), lax.axis_index("y"), lax.axis_index("c")
        me = {"chips_ag": 2 * mx + my, "sib_ag": mc, "all_rs": 4 * mx + 2 * my + mc}[kind]

        def src(t, px, py, pc):
            if kind != "all_rs":
                return x_refs[t]
            half = xs[t].shape[1] // 2
            return x_refs[t].at[2 * px + py, pl.ds(pl.multiple_of(pc * half, 16), half), :]

        local = [pltpu.make_async_copy(src(t, mx, my, mc), o_refs[t].at[me], local_sems.at[t]) for t in range(nt)]
        remote = [pltpu.make_async_remote_copy(
            src_ref=src(t, mx ^ fx, my ^ fy, mc ^ fc), dst_ref=o_refs[t].at[me],
            send_sem=send_sems.at[t * npeer + k], recv_sem=recv_sems.at[t * npeer + k],
            device_id=(mx ^ fx, my ^ fy, mc ^ fc), device_id_type=pl.DeviceIdType.MESH)
            for t in range(nt) for k, (fx, fy, fc) in enumerate(masks)]
        return local, remote

    def start(x_refs, o_refs, sems):
        local, remote = copies(x_refs, o_refs, sems)
        for cp in local + remote:
            cp.start()

    def wait(x_refs, o_refs, sems):
        local, remote = copies(x_refs, o_refs, sems)
        for cp in remote:
            cp.wait_recv()
        for cp in remote:
            cp.wait_send()
        for cp in local:
            cp.wait()

    def out_shape(x):
        if kind == "all_rs":
            return (N_DEV, x.shape[1] // 2, x.shape[2])
        return ((N_CHIPS if kind == "chips_ag" else 2),) + tuple(x.shape)

    out_shapes = [jax.ShapeDtypeStruct(out_shape(x), x.dtype) for x in xs]
    scratch = [pltpu.SemaphoreType.DMA((max(nt * npeer, 1),)), pltpu.SemaphoreType.DMA((max(nt * npeer, 1),)),
               pltpu.SemaphoreType.DMA((max(nt, 1),))]
    return out_shapes, scratch, start, wait


def _ride(body, n_in, n_out, n_scratch, carry, nsteps):
    if not carry:
        return body, [], [], []
    kind, xs = carry
    nc = len(xs)
    c_out, c_scratch, c_start, c_wait = _carried_comm(kind, xs)

    def wrapped(*refs):
        ins, cx = refs[:n_in], refs[n_in:n_in + nc]
        outs, co = refs[n_in + nc:n_in + nc + n_out], refs[n_in + nc + n_out:n_in + 2 * nc + n_out]
        base = n_in + 2 * nc + n_out
        scr, sems = refs[base:base + n_scratch], refs[base + n_scratch:]
        i = pl.program_id(0)

        @pl.when(i == 0)
        def _():
            c_start(cx, co, sems)

        body(*ins, *outs, *scr)

        @pl.when(i == nsteps - 1)
        def _():
            c_wait(cx, co, sems)

    return wrapped, list(xs), c_out, c_scratch


def _comm_alone(kind, xs, name):
    nt = len(xs)
    out_shapes, scratch, start, wait = _carried_comm(kind, xs)

    def body(*refs):
        x_refs, o_refs, sems = refs[:nt], refs[nt:2 * nt], refs[2 * nt:]
        start(x_refs, o_refs, sems)
        wait(x_refs, o_refs, sems)

    any_spec = pl.BlockSpec(memory_space=pl.ANY)
    return pl.pallas_call(body, name=name, out_shape=out_shapes, in_specs=[any_spec] * nt,
                          out_specs=[any_spec] * nt, scratch_shapes=scratch)(*xs)


def _allgather8(x, name):
    return _comm(x, group="all", by_peer=False, name=name)


def _sum_slots(x, out_dtype, name):
    n, R, W = x.shape
    tr = _tile(R, max(16, min(512, SUM_BLOCK_BYTES // (n * W * x.dtype.itemsize) // 16 * 16)), 16)

    def body(x_ref, o_ref):
        acc = x_ref[0].astype(F32)
        for s in range(1, n):
            acc = acc + x_ref[s].astype(F32)
        o_ref[...] = acc.astype(o_ref.dtype)

    return pl.pallas_call(
        body, name=name, grid=(R // tr,),
        in_specs=[pl.BlockSpec((n, tr, W), lambda i: (0, i, 0))],
        out_specs=pl.BlockSpec((tr, W), lambda i: (i, 0)),
        out_shape=jax.ShapeDtypeStruct((R, W), out_dtype),
        compiler_params=_cparams(("parallel",)),
    )(x)


def _pack(arrs, dtype, row_mult):
    flat = jnp.concatenate([a.reshape(-1).astype(dtype) for a in arrs])
    n = flat.shape[0]
    quant = row_mult * PACK_W
    total = -(-n // quant) * quant
    if total != n:
        flat = jnp.concatenate([flat, jnp.zeros((total - n,), dtype)])
    return flat.reshape(total // PACK_W, PACK_W)


def _unpack(buf, shapes):
    flat = buf.reshape(-1)
    out, off = [], 0
    for s in shapes:
        n = math.prod(s)
        out.append(flat[off:off + n].reshape(s))
        off += n
    return out


class _Mat:
    def __init__(self, arr, axis=None, layer=None, part0=0, nparts=None):
        self.arr, self.axis, self.layer, self.part0 = arr, axis, layer, part0
        self.r, self.c = arr.shape[-2:]
        self.nparts = 1 if axis is None else (arr.shape[0] if nparts is None else nparts)

    @property
    def shape(self):
        return (self.r * (self.nparts if self.axis == 0 else 1), self.c * (self.nparts if self.axis == 1 else 1))

    def unit(self, ax):
        return self.r if ax == 0 else self.c

    def spec(self, tr, tc, fn):
        def imap(*g):
            rb, cb = fn(*g)
            lead = []
            if self.axis == 0:
                per = self.r // tr
                lead.append(self.part0 + rb // per)
                rb = rb % per
            elif self.axis == 1:
                per = self.c // tc
                lead.append(self.part0 + cb // per)
                cb = cb % per
            if self.layer is not None:
                lead.append(self.layer)
            return (*lead, rb, cb)

        return pl.BlockSpec((None,) * (self.arr.ndim - 2) + (tr, tc), imap)


def _as_mat(x):
    return x if isinstance(x, _Mat) else _Mat(x)


def _matmul(a, b, *, mode, name, out_dtype=F32, add=None, out_split=None, carry=None, tm_cap=1088, tn_cap=1536,
            tk_cap=2048):
    a, b = _as_mat(a), _as_mat(b)
    am, ak = (0, 1) if mode != "tn" else (1, 0)
    bk, bn = (0, 1) if mode != "nt" else (1, 0)
    M, K, N = a.shape[am], a.shape[ak], b.shape[bn]
    assert b.shape[bk] == K
    if out_split is None:
        o = _Mat(jax.ShapeDtypeStruct((M, N), out_dtype))
    else:
        ax, parts = out_split
        o = _Mat(jax.ShapeDtypeStruct((parts, M // parts if ax == 0 else M, N // parts if ax == 1 else N), out_dtype),
                 axis=ax)
    tm = _tile(math.gcd(a.unit(am), o.unit(0)), tm_cap, 16 if mode != "tn" else LANES)
    tn = _tile(math.gcd(b.unit(bn), o.unit(1)), tn_cap, LANES)
    tk = _tile(math.gcd(a.unit(ak), b.unit(bk)), tk_cap, LANES if mode != "tn" else 16)
    nk = K // tk
    dims = {"nn": _NN, "nt": _NT, "tn": _TN}[mode]

    kind, carry = carry if carry else ("chips_ag", [])
    carry = list(carry)
    nc = len(carry)
    c_out, c_scratch, c_start, c_wait = _carried_comm(kind, carry)
    grid = (M // tm, N // tn, nk)

    def body(*refs):
        a_ref, b_ref = refs[:2]
        add_ref = refs[2] if add is not None else None
        n_in = 2 + (add is not None)
        cx_refs, o_ref, co_refs = refs[n_in:n_in + nc], refs[n_in + nc], refs[n_in + nc + 1:n_in + 2 * nc + 1]
        acc_ref, sems = refs[n_in + 2 * nc + 1], refs[n_in + 2 * nc + 2:]
        i, j, k = pl.program_id(0), pl.program_id(1), pl.program_id(2)
        if nc:
            @pl.when((i == 0) & (j == 0) & (k == 0))
            def _():
                c_start(cx_refs, co_refs, sems)

        @pl.when(k == 0)
        def _():
            acc_ref[...] = jnp.zeros_like(acc_ref)

        acc_ref[...] += _dot(a_ref[...], b_ref[...], dims)

        @pl.when(k == nk - 1)
        def _():
            r = acc_ref[...]
            if add_ref is not None:
                r = r + add_ref[...].astype(F32)
            o_ref[...] = r.astype(o_ref.dtype)

        if nc:
            @pl.when((i == grid[0] - 1) & (j == grid[1] - 1) & (k == nk - 1))
            def _():
                c_wait(cx_refs, co_refs, sems)

    if mode == "tn":
        a_spec = a.spec(tk, tm, lambda i, j, k: (k, i))
    else:
        a_spec = a.spec(tm, tk, lambda i, j, k: (i, k))
    if mode == "nt":
        b_spec = b.spec(tn, tk, lambda i, j, k: (j, k))
    else:
        b_spec = b.spec(tk, tn, lambda i, j, k: (k, j))
    o_spec = o.spec(tm, tn, lambda i, j, k: (i, j))
    any_spec = pl.BlockSpec(memory_space=pl.ANY)
    in_specs = [a_spec, b_spec] + ([pl.BlockSpec((tm, tn), lambda i, j, k: (i, j))] if add is not None else [])
    args = (a.arr, b.arr) + ((add,) if add is not None else ())
    outs = pl.pallas_call(
        body, name=name, grid=grid,
        in_specs=in_specs + [any_spec] * nc, out_specs=[o_spec] + [any_spec] * nc,
        out_shape=[o.arr] + c_out,
        scratch_shapes=[pltpu.VMEM((tm, tn), F32)] + (c_scratch if nc else []),
        compiler_params=_cparams(("arbitrary",) * 3 if nc else ("parallel", "parallel", "arbitrary")),
    )(*args, *carry)
    return (outs[0], outs[1:]) if nc else outs[0]


def _matmul_residual(a, w, res, gate, C, name, carry=None):
    w = _as_mat(w)
    M, K = a.shape
    N = w.shape[1]
    tm = _tile(M, 1088, 16)
    tn = _tile(w.unit(1), 1024, LANES)
    tk = _tile(w.unit(0), 2048, LANES)
    nk = K // tk
    kind, carry = carry if carry else ("chips_ag", [])
    carry = list(carry)
    nc = len(carry)
    c_out, c_scratch, c_start, c_wait = _carried_comm(kind, carry)
    grid = (M // tm, N // tn, nk)

    def body(*refs):
        a_ref, w_ref, res_ref, gate_ref = refs[:4]
        cx_refs, (mix_ref, o_ref), co_refs = refs[4:4 + nc], refs[4 + nc:6 + nc], refs[6 + nc:6 + 2 * nc]
        acc_ref, sems = refs[6 + 2 * nc], refs[7 + 2 * nc:]
        i, j, k = pl.program_id(0), pl.program_id(1), pl.program_id(2)
        if nc:
            @pl.when((i == 0) & (j == 0) & (k == 0))
            def _():
                c_start(cx_refs, co_refs, sems)

        @pl.when(k == 0)
        def _():
            acc_ref[...] = jnp.zeros_like(acc_ref)

        acc_ref[...] += _dot(a_ref[...], w_ref[...], _NN)

        @pl.when(k == nk - 1)
        def _():
            r = acc_ref[...]
            row = i * tm + lax.broadcasted_iota(jnp.int32, (tm, 1), 0)
            g = jnp.where(row < C, gate_ref[0:1, :], gate_ref[1:2, :])
            mix_ref[...] = r
            o_ref[...] = res_ref[...] + g * r

        if nc:
            @pl.when((i == grid[0] - 1) & (j == grid[1] - 1) & (k == nk - 1))
            def _():
                c_wait(cx_refs, co_refs, sems)

    o_spec = pl.BlockSpec((tm, tn), lambda i, j, k: (i, j))
    any_spec = pl.BlockSpec(memory_space=pl.ANY)
    outs = pl.pallas_call(
        body, name=name, grid=grid,
        in_specs=[pl.BlockSpec((tm, tk), lambda i, j, k: (i, k)),
                  w.spec(tk, tn, lambda i, j, k: (k, j)),
                  o_spec,
                  pl.BlockSpec((2, tn), lambda i, j, k: (0, j))] + [any_spec] * nc,
        out_specs=[o_spec, o_spec] + [any_spec] * nc,
        out_shape=[jax.ShapeDtypeStruct((M, N), F32), jax.ShapeDtypeStruct((M, N), F32)] + c_out,
        scratch_shapes=[pltpu.VMEM((tm, tn), F32)] + (c_scratch if nc else []),
        compiler_params=_cparams(("arbitrary",) * 3 if nc else ("parallel", "parallel", "arbitrary")),
    )(a, w.arr, res, gate, *carry)
    return (outs[0], outs[1], outs[2:]) if nc else (outs[0], outs[1])


def _seg_select(i, tm, C, ref):
    row = i * tm + lax.broadcasted_iota(jnp.int32, (tm, 1), 0)
    return row < C, jnp.where(row < C, ref[0:1, :], ref[1:2, :])


def _normmod(X, g, shift, scale, C, name):
    T, D = X.shape
    tm = _tile(T, 512, 16)

    def body(x_ref, g_ref, sh_ref, sc_ref, h_ref):
        i = pl.program_id(0)
        x = x_ref[...]
        r = lax.rsqrt(jnp.mean(x * x, axis=-1, keepdims=True) + NORM_EPS)
        _, sh = _seg_select(i, tm, C, sh_ref)
        _, sc = _seg_select(i, tm, C, sc_ref)
        h_ref[...] = ((x * r) * g_ref[...] * (1.0 + sc) + sh).astype(BF16)

    row = pl.BlockSpec((tm, D), lambda i: (i, 0))
    vec = pl.BlockSpec((1, D), lambda i: (0, 0))
    two = pl.BlockSpec((2, D), lambda i: (0, 0))
    return pl.pallas_call(
        body, name=name, grid=(T // tm,),
        in_specs=[row, vec, two, two], out_specs=row,
        out_shape=jax.ShapeDtypeStruct((T, D), BF16),
        compiler_params=_cparams(("parallel",)),
    )(X, g, shift, scale)


def _normmod_bwd(X, g, scale, dh, dres, C, name):
    T, D = X.shape
    tm = _tile(T, 512, 16)

    def body(x_ref, g_ref, sc_ref, dh_ref, dres_ref, dx_ref, dg_ref, dsh_ref, dsc_ref):
        i = pl.program_id(0)

        @pl.when(i == 0)
        def _():
            dg_ref[...] = jnp.zeros_like(dg_ref)
            dsh_ref[...] = jnp.zeros_like(dsh_ref)
            dsc_ref[...] = jnp.zeros_like(dsc_ref)

        x = x_ref[...]
        dh = dh_ref[...]
        gv = g_ref[...]
        r = lax.rsqrt(jnp.mean(x * x, axis=-1, keepdims=True) + NORM_EPS)
        xh = x * r
        isc, sc = _seg_select(i, tm, C, sc_ref)
        n = xh * gv
        dhn = dh * n
        zero = jnp.zeros_like(dh)
        dsh_ref[0:1, :] += jnp.sum(jnp.where(isc, dh, zero), axis=0, keepdims=True)
        dsh_ref[1:2, :] += jnp.sum(jnp.where(isc, zero, dh), axis=0, keepdims=True)
        dsc_ref[0:1, :] += jnp.sum(jnp.where(isc, dhn, zero), axis=0, keepdims=True)
        dsc_ref[1:2, :] += jnp.sum(jnp.where(isc, zero, dhn), axis=0, keepdims=True)
        dn = dh * (1.0 + sc)
        dg_ref[...] += jnp.sum(dn * xh, axis=0, keepdims=True)
        dxh = dn * gv
        dx_ref[...] = dres_ref[...] + r * (dxh - xh * jnp.mean(dxh * xh, axis=-1, keepdims=True))

    row = pl.BlockSpec((tm, D), lambda i: (i, 0))
    vec = pl.BlockSpec((1, D), lambda i: (0, 0))
    two = pl.BlockSpec((2, D), lambda i: (0, 0))
    return pl.pallas_call(
        body, name=name, grid=(T // tm,),
        in_specs=[row, vec, two, row, row], out_specs=[row, vec, two, two],
        out_shape=[jax.ShapeDtypeStruct((T, D), F32), jax.ShapeDtypeStruct((1, D), F32),
                   jax.ShapeDtypeStruct((2, D), F32), jax.ShapeDtypeStruct((2, D), F32)],
        compiler_params=_cparams(("arbitrary",)),
    )(X, g, scale, dh, dres)


def _gate_bwd(dxo, mix, gate, C, name):
    T, D = dxo.shape
    tm = _tile(T, 512, 16)

    def body(dx_ref, mix_ref, gate_ref, dmix_ref, dgate_ref):
        i = pl.program_id(0)

        @pl.when(i == 0)
        def _():
            dgate_ref[...] = jnp.zeros_like(dgate_ref)

        dx = dx_ref[...]
        isc, gt = _seg_select(i, tm, C, gate_ref)
        dmix_ref[...] = (dx * gt).astype(BF16)
        p = dx * mix_ref[...]
        zero = jnp.zeros_like(p)
        dgate_ref[0:1, :] += jnp.sum(jnp.where(isc, p, zero), axis=0, keepdims=True)
        dgate_ref[1:2, :] += jnp.sum(jnp.where(isc, zero, p), axis=0, keepdims=True)

    row = pl.BlockSpec((tm, D), lambda i: (i, 0))
    two = pl.BlockSpec((2, D), lambda i: (0, 0))
    return pl.pallas_call(
        body, name=name, grid=(T // tm,),
        in_specs=[row, row, two], out_specs=[row, two],
        out_shape=[jax.ShapeDtypeStruct((T, D), BF16), jax.ShapeDtypeStruct((2, D), F32)],
        compiler_params=_cparams(("arbitrary",)),
    )(dxo, mix, gate)


def _conv_masks(T, C):
    row = lax.broadcasted_iota(jnp.int32, (T, 1), 0)
    first = (row == 0) | (row == C)
    last = (row == C - 1) | (row == T - 1)
    return first, last


def _shift_rows(x, first, last):
    T = x.shape[0]
    prev = jnp.where(first, 0.0, pltpu.roll(x, 1, 0))
    nxt = jnp.where(last, 0.0, pltpu.roll(x, T - 1, 0))
    return prev, nxt


def _convact(up, cw, cb, C, name):
    _, T, F = up.shape
    tc = LANES

    def body(u_ref, w_ref, b_ref, o_ref):
        gp = u_ref[0].astype(F32)
        first, last = _conv_masks(T, C)
        prev, nxt = _shift_rows(gp, first, last)
        gc = prev * w_ref[0:1, :] + gp * w_ref[1:2, :] + nxt * w_ref[2:3, :] + b_ref[...]
        o_ref[...] = (gc * _sigmoid(gc) * u_ref[1].astype(F32)).astype(BF16)

    col = pl.BlockSpec((T, tc), lambda j: (0, j))
    return pl.pallas_call(
        body, name=name, grid=(F // tc,),
        in_specs=[pl.BlockSpec((2, T, tc), lambda j: (0, 0, j)), pl.BlockSpec((3, tc), lambda j: (0, j)),
                  pl.BlockSpec((1, tc), lambda j: (0, j))],
        out_specs=col, out_shape=jax.ShapeDtypeStruct((T, F), BF16),
        compiler_params=_cparams(("parallel",)),
    )(up, cw, cb)


def _convact_bwd(up, cw, cb, dact, C, name):
    _, T, F = up.shape
    tc = LANES

    def body(u_ref, w_ref, b_ref, da_ref, du_ref, dw_ref, db_ref):
        gp = u_ref[0].astype(F32)
        val = u_ref[1].astype(F32)
        da = da_ref[...].astype(F32)
        first, last = _conv_masks(T, C)
        prev, nxt = _shift_rows(gp, first, last)
        w0, w1, w2 = w_ref[0:1, :], w_ref[1:2, :], w_ref[2:3, :]
        gc = prev * w0 + gp * w1 + nxt * w2 + b_ref[...]
        s = _sigmoid(gc)
        du_ref[1] = (da * gc * s).astype(BF16)
        dgc = da * val * (s + gc * s * (1.0 - s))
        db_ref[...] = jnp.sum(dgc, axis=0, keepdims=True)
        dw_ref[0:1, :] = jnp.sum(dgc * prev, axis=0, keepdims=True)
        dw_ref[1:2, :] = jnp.sum(dgc * gp, axis=0, keepdims=True)
        dw_ref[2:3, :] = jnp.sum(dgc * nxt, axis=0, keepdims=True)
        dprev, dnxt = _shift_rows(dgc, first, last)
        du_ref[0] = (dnxt * w0 + dgc * w1 + dprev * w2).astype(BF16)

    col = pl.BlockSpec((T, tc), lambda j: (0, j))
    two = pl.BlockSpec((2, T, tc), lambda j: (0, 0, j))
    w3 = pl.BlockSpec((3, tc), lambda j: (0, j))
    w1 = pl.BlockSpec((1, tc), lambda j: (0, j))
    return pl.pallas_call(
        body, name=name, grid=(F // tc,),
        in_specs=[two, w3, w1, col], out_specs=[two, w3, w1],
        out_shape=[jax.ShapeDtypeStruct((2, T, F), BF16),
                   jax.ShapeDtypeStruct((3, F), F32), jax.ShapeDtypeStruct((1, F), F32)],
        compiler_params=_cparams(("parallel",)),
    )(up, cw, cb, dact)


def _rot_half(x):
    W = x.shape[-1]
    lane = lax.broadcasted_iota(jnp.int32, x.shape, x.ndim - 1)
    lo = (lane % 32) < 16
    return jnp.where(lo, -pltpu.roll(x, W - 16, x.ndim - 1), pltpu.roll(x, 16, x.ndim - 1))


def _swap_halves(x):
    return pltpu.roll(x, 64, x.ndim - 1)


def _rope_tables(dm):
    t = jnp.arange(dm.L)
    n_freq = HEAD_DIM // 4
    inv_freq = ROPE_BASE ** (-jnp.arange(n_freq, dtype=F32) / n_freq)
    ang_r = (t // GRID_W).astype(F32)[:, None] * inv_freq
    ang_c = (t % GRID_W).astype(F32)[:, None] * inv_freq
    ang = jnp.concatenate([ang_r, ang_r, ang_c, ang_c], axis=-1)
    cos = jnp.concatenate([jnp.ones((dm.C, HEAD_DIM), F32), jnp.cos(ang)], axis=0)
    sin = jnp.concatenate([jnp.zeros((dm.C, HEAD_DIM), F32), jnp.sin(ang)], axis=0)
    return jnp.tile(cos, (1, 2)), jnp.tile(sin, (1, 2))


def _attn_prep(z, cos, sin, dm, name):
    T = dm.T
    AW = dm.ATTN_W
    tm = _tile(T, 256, 16)
    rep = AW // LANES

    def body(z_ref, cos_ref, sin_ref, q_ref, k_ref, v_ref):
        cs, sn = cos_ref[...], sin_ref[...]
        q = z_ref[:, 0:AW]
        csq, snq = jnp.tile(cs, (1, rep)), jnp.tile(sn, (1, rep))
        q_ref[...] = (q * csq + _rot_half(q) * snq).astype(BF16)
        k = z_ref[:, dm.OFF_K:dm.OFF_V]
        k = k * cs + _rot_half(k) * sn
        v = z_ref[:, dm.OFF_V:dm.OFF_S]
        lo = lax.broadcasted_iota(jnp.int32, k.shape, 1) < HEAD_DIM
        ks, vs = _swap_halves(k), _swap_halves(v)
        k_ref[0] = jnp.where(lo, k, ks).astype(BF16)
        k_ref[1] = jnp.where(lo, ks, k).astype(BF16)
        v_ref[0] = jnp.where(lo, v, vs).astype(BF16)
        v_ref[1] = jnp.where(lo, vs, v).astype(BF16)

    tab = pl.BlockSpec((tm, LANES), lambda i: (i, 0))
    kv = pl.BlockSpec((2, tm, LANES), lambda i: (0, i, 0))
    return pl.pallas_call(
        body, name=name, grid=(T // tm,),
        in_specs=[pl.BlockSpec((tm, dm.OFF_S), lambda i: (i, 0)), tab, tab],
        out_specs=[pl.BlockSpec((tm, AW), lambda i: (i, 0)), kv, kv],
        out_shape=[jax.ShapeDtypeStruct((T, AW), BF16), jax.ShapeDtypeStruct((2, T, LANES), BF16),
                   jax.ShapeDtypeStruct((2, T, LANES), BF16)],
        compiler_params=_cparams(("parallel",)),
    )(z, cos, sin)


def _attn_prep_bwd(dq, dk2, dv2, cos, sin, dm, name):
    T = dm.T
    AW = dm.ATTN_W
    tm = _tile(T, 256, 16)
    rep = AW // LANES

    def body(dq_ref, dk_ref, dv_ref, cos_ref, sin_ref, o_ref):
        cs, sn = cos_ref[...], sin_ref[...]
        csq, snq = jnp.tile(cs, (1, rep)), jnp.tile(sn, (1, rep))
        dq = dq_ref[...]
        o_ref[:, 0:AW] = dq * csq - _rot_half(dq * snq)
        lo = lax.broadcasted_iota(jnp.int32, (tm, LANES), 1) < HEAD_DIM
        dk = jnp.where(lo, dk_ref[0], dk_ref[1])
        o_ref[:, dm.OFF_K:dm.OFF_V] = dk * cs - _rot_half(dk * sn)
        o_ref[:, dm.OFF_V:dm.OFF_S] = jnp.where(lo, dv_ref[0], dv_ref[1])

    tab = pl.BlockSpec((tm, LANES), lambda i: (i, 0))
    kv = pl.BlockSpec((2, tm, LANES), lambda i: (0, i, 0))
    return pl.pallas_call(
        body, name=name, grid=(T // tm,),
        in_specs=[pl.BlockSpec((tm, AW), lambda i: (i, 0)), kv, kv, tab, tab],
        out_specs=pl.BlockSpec((tm, dm.OFF_S), lambda i: (i, 0)),
        out_shape=jax.ShapeDtypeStruct((T, dm.OFF_S), F32),
        compiler_params=_cparams(("parallel",)),
    )(dq, dk2, dv2, cos, sin)


def _attn_specs(dm):
    B = ATTN_BLOCK
    nblk = dm.T // B
    q_spec = pl.BlockSpec((B, dm.ATTN_W), lambda i: (i, 0))
    prev = pl.BlockSpec((2, B, LANES), lambda i: (0, jnp.maximum(i - 1, 0), 0))
    own = pl.BlockSpec((2, B, LANES), lambda i: (0, i, 0))
    nxt = pl.BlockSpec((2, B, LANES), lambda i: (0, jnp.minimum(i + 1, nblk - 1), 0))
    ctx = pl.BlockSpec((2, dm.C, LANES), lambda i: (0, 0, 0))
    sink = pl.BlockSpec((2, GQA_RATIO * B, 1), lambda i: (0, 0, 0))
    return nblk, q_spec, prev, own, nxt, ctx, sink


def _attn_scores(i, q_ref, kp_ref, ko_ref, kn_ref, kc_ref, sink_ref, h, dm):
    B = ATTN_BLOCK
    nctx = dm.C // B
    nb = dm.L // B
    npair = GQA_RATIO // 2
    lo = lax.broadcasted_iota(jnp.int32, (B, LANES), 1) < HEAD_DIM
    zero = jnp.zeros((B, LANES), BF16)
    parts = []
    for p in range(npair):
        q2 = q_ref[:, (h * npair + p) * LANES:(h * npair + p + 1) * LANES]
        parts.append(jnp.where(lo, q2, zero))
        parts.append(jnp.where(lo, zero, q2))
    q8 = jnp.concatenate(parts, axis=0)
    kcat = jnp.concatenate([kp_ref[h], ko_ref[h], kn_ref[h], kc_ref[h]], axis=0)
    s = _dot(q8, kcat, _NT) * (HEAD_DIM ** -0.5)
    R, NK = s.shape
    iq = lax.broadcasted_iota(jnp.int32, (R, NK), 0) % B
    col = lax.broadcasted_iota(jnp.int32, (R, NK), 1)
    jk = col % B
    qb = i - nctx
    lat = qb >= 0
    m_prev = (col < B) & lat & (qb >= 1) & (jk >= iq)
    m_own = (col >= B) & (col < 2 * B) & lat
    m_next = (col >= 2 * B) & (col < 3 * B) & lat & (qb <= nb - 2) & (jk <= iq)
    mask = m_prev | m_own | m_next | (col >= 3 * B)
    s = jnp.where(mask, s, -jnp.inf)
    sk = sink_ref[h]
    m = jnp.maximum(jnp.max(s, axis=-1, keepdims=True), sk)
    e = jnp.exp(s - m)
    es = jnp.exp(sk - m)
    inv = 1.0 / (jnp.sum(e, axis=-1, keepdims=True) + es)
    return q8, kcat, e * inv, es * inv, lo


def _unstack_heads(x8, lo):
    B = ATTN_BLOCK
    out = []
    for p in range(GQA_RATIO // 2):
        a = x8[(2 * p) * B:(2 * p + 1) * B]
        b = x8[(2 * p + 1) * B:(2 * p + 2) * B]
        out.append(jnp.where(lo, a, b))
    return out


def _attention(q, k2, v2, sinkcol, dm, name, carry=None):
    B = ATTN_BLOCK
    nblk, q_spec, prev, own, nxt, ctx, sink = _attn_specs(dm)
    npair = GQA_RATIO // 2

    def body(q_ref, kp_ref, ko_ref, kn_ref, kc_ref, vp_ref, vo_ref, vn_ref, vc_ref, sink_ref, o_ref):
        i = pl.program_id(0)
        for h in range(2):
            _, _, p, _, lo = _attn_scores(i, q_ref, kp_ref, ko_ref, kn_ref, kc_ref, sink_ref, h, dm)
            vcat = jnp.concatenate([vp_ref[h], vo_ref[h], vn_ref[h], vc_ref[h]], axis=0)
            o8 = _dot(p, vcat, _NN)
            for pi, o2 in enumerate(_unstack_heads(o8, lo)):
                c0 = (h * npair + pi) * LANES
                o_ref[:, c0:c0 + LANES] = o2.astype(BF16)

    body, cxs, c_out, c_scratch = _ride(body, 10, 1, 0, carry, nblk)
    any_spec = pl.BlockSpec(memory_space=pl.ANY)
    outs = pl.pallas_call(
        body, name=name, grid=(nblk,),
        in_specs=[q_spec, prev, own, nxt, ctx, prev, own, nxt, ctx, sink] + [any_spec] * len(cxs),
        out_specs=[q_spec] + [any_spec] * len(cxs),
        out_shape=[jax.ShapeDtypeStruct((dm.T, dm.ATTN_W), BF16)] + c_out,
        scratch_shapes=c_scratch,
        compiler_params=_cparams(("arbitrary",) if carry else ("parallel",)),
    )(q, k2, k2, k2, k2, v2, v2, v2, v2, sinkcol, *cxs)
    return (outs[0], outs[1:]) if carry else outs[0]


def _attention_bwd(q, k2, v2, sinkcol, do, dm, name, carry=None):
    B = ATTN_BLOCK
    nblk, q_spec, prev, own, nxt, ctx, sink = _attn_specs(dm)
    npair = GQA_RATIO // 2

    def body(q_ref, kp_ref, ko_ref, kn_ref, kc_ref, vp_ref, vo_ref, vn_ref, vc_ref, sink_ref, do_ref,
             dq_ref, pk_ref, pv_ref, dkc_ref, dvc_ref, dsk_ref):
        i = pl.program_id(0)

        @pl.when(i == 0)
        def _():
            dkc_ref[...] = jnp.zeros_like(dkc_ref)
            dvc_ref[...] = jnp.zeros_like(dvc_ref)
            dsk_ref[...] = jnp.zeros_like(dsk_ref)

        for h in range(2):
            q8, kcat, p, ps, lo = _attn_scores(i, q_ref, kp_ref, ko_ref, kn_ref, kc_ref, sink_ref, h, dm)
            vcat = jnp.concatenate([vp_ref[h], vo_ref[h], vn_ref[h], vc_ref[h]], axis=0)
            zero = jnp.zeros((B, LANES), F32)
            parts = []
            for pi in range(npair):
                d2 = do_ref[:, (h * npair + pi) * LANES:(h * npair + pi + 1) * LANES]
                parts.append(jnp.where(lo, d2, zero))
                parts.append(jnp.where(lo, zero, d2))
            do8 = jnp.concatenate(parts, axis=0)
            dp = _dot(do8, vcat, _NT)
            delta = jnp.sum(p * dp, axis=-1, keepdims=True)
            ds = p * (dp - delta) * (HEAD_DIM ** -0.5)
            dsr = -ps * delta
            rows = [jnp.broadcast_to(jnp.sum(dsr[r * B:(r + 1) * B], axis=0, keepdims=True), (1, LANES))
                    for r in range(GQA_RATIO)]
            dsk_ref[h] += jnp.concatenate(rows, axis=0)
            dq8 = _dot(ds, kcat, _NN)
            for pi, d2 in enumerate(_unstack_heads(dq8, lo)):
                c0 = (h * npair + pi) * LANES
                dq_ref[:, c0:c0 + LANES] = d2
            dk = _dot(ds, q8, _TN)
            dv = _dot(p, do8, _TN)
            dk = dk + _swap_halves(dk)
            dv = dv + _swap_halves(dv)
            for w in range(3):
                pk_ref[0, h, w] = dk[w * B:(w + 1) * B]
                pv_ref[0, h, w] = dv[w * B:(w + 1) * B]
            dkc_ref[h] += dk[3 * B:]
            dvc_ref[h] += dv[3 * B:]

    part = pl.BlockSpec((1, 2, 3, B, LANES), lambda i: (i, 0, 0, 0, 0))
    acc_c = pl.BlockSpec((2, dm.C, LANES), lambda i: (0, 0, 0))
    acc_s = pl.BlockSpec((2, GQA_RATIO, LANES), lambda i: (0, 0, 0))
    body, cxs, c_out, c_scratch = _ride(body, 11, 6, 0, carry, nblk)
    any_spec = pl.BlockSpec(memory_space=pl.ANY)
    outs = pl.pallas_call(
        body, name=name, grid=(nblk,),
        in_specs=[q_spec, prev, own, nxt, ctx, prev, own, nxt, ctx, sink, q_spec] + [any_spec] * len(cxs),
        out_specs=[q_spec, part, part, acc_c, acc_c, acc_s] + [any_spec] * len(cxs),
        out_shape=[jax.ShapeDtypeStruct((dm.T, dm.ATTN_W), F32),
                   jax.ShapeDtypeStruct((nblk, 2, 3, B, LANES), F32),
                   jax.ShapeDtypeStruct((nblk, 2, 3, B, LANES), F32),
                   jax.ShapeDtypeStruct((2, dm.C, LANES), F32), jax.ShapeDtypeStruct((2, dm.C, LANES), F32),
                   jax.ShapeDtypeStruct((2, GQA_RATIO, LANES), F32)] + c_out,
        scratch_shapes=c_scratch,
        compiler_params=_cparams(("arbitrary",)),
    )(q, k2, k2, k2, k2, v2, v2, v2, v2, sinkcol, do, *cxs)
    return (outs[:6], outs[6:]) if carry else outs


def _kv_reduce(part, ctxacc, dm, name):
    B = ATTN_BLOCK
    nblk = dm.T // B
    nctx = dm.C // B

    def body(own_ref, nxt_ref, prv_ref, c_ref, o_ref):
        j = pl.program_id(0)
        acc = own_ref[0, :, 0]
        acc = acc + jnp.where(j < nblk - 1, nxt_ref[0, :, 0], 0.0)
        acc = acc + jnp.where(j > 0, prv_ref[0, :, 0], 0.0)
        acc = acc + jnp.where(j < nctx, c_ref[...], 0.0)
        o_ref[...] = acc

    def spec(w, f):
        return pl.BlockSpec((1, 2, 1, B, LANES), lambda j: (f(j), 0, w, 0, 0))

    return pl.pallas_call(
        body, name=name, grid=(nblk,),
        in_specs=[spec(1, lambda j: j), spec(0, lambda j: jnp.minimum(j + 1, nblk - 1)),
                  spec(2, lambda j: jnp.maximum(j - 1, 0)),
                  pl.BlockSpec((2, B, LANES), lambda j: (0, jnp.minimum(j, nctx - 1), 0))],
        out_specs=pl.BlockSpec((2, B, LANES), lambda j: (0, j, 0)),
        out_shape=jax.ShapeDtypeStruct((2, dm.T, LANES), F32),
        compiler_params=_cparams(("parallel",)),
    )(part, part, part, ctxacc)


def _gmlp_core(z_ref, lg_ref, lb_ref, ws_ref, bs_ref, dm):
    GW = dm.GMLP_W
    gu = z_ref[:, dm.OFF_GU:dm.OFF_GV]
    gv = z_ref[:, dm.OFF_GV:dm.N_IN]
    u = _gelu(gu)
    ge = _gelu(gv)
    mu = jnp.mean(ge, axis=-1, keepdims=True)
    xc = ge - mu
    r = lax.rsqrt(jnp.mean(xc * xc, axis=-1, keepdims=True) + NORM_EPS)
    xh = xc * r
    v = xh * lg_ref[...] + lb_ref[...]
    mixed = []
    for g in range(dm.GG):
        vg = v[:, g * GMLP_GROUP_W:(g + 1) * GMLP_GROUP_W]
        mixed.append(_dot(ws_ref[g], vg, _NN) + bs_ref[g])
    return gu, gv, u, xh, r, v, mixed


def _gmlp(z, ln_g, ln_b, w_s, b_s, dm, name):
    T, GW, CH = dm.T, dm.GMLP_W, GMLP_CHUNK

    def body(z_ref, lg_ref, lb_ref, ws_ref, bs_ref, o_ref):
        _, _, u, _, _, _, mixed = _gmlp_core(z_ref, lg_ref, lb_ref, ws_ref, bs_ref, dm)
        for g in range(dm.GG):
            sl = slice(g * GMLP_GROUP_W, (g + 1) * GMLP_GROUP_W)
            o_ref[:, sl] = (u[:, sl] * mixed[g]).astype(BF16)

    vec = pl.BlockSpec((1, GW), lambda i: (0, 0))
    return pl.pallas_call(
        body, name=name, grid=(T // CH,),
        in_specs=[pl.BlockSpec((CH, dm.N_IN), lambda i: (i, 0)), vec, vec,
                  pl.BlockSpec((dm.GG, CH, CH), lambda i: (0, 0, 0)),
                  pl.BlockSpec((dm.GG, CH, 1), lambda i: (0, 0, 0))],
        out_specs=pl.BlockSpec((CH, GW), lambda i: (i, 0)),
        out_shape=jax.ShapeDtypeStruct((T, GW), BF16),
        compiler_params=_cparams(("parallel",)),
    )(z, ln_g, ln_b, w_s, b_s)


def _gmlp_bwd(z, ln_g, ln_b, w_s, b_s, do, dm, name):
    T, GW, CH = dm.T, dm.GMLP_W, GMLP_CHUNK

    def body(z_ref, lg_ref, lb_ref, ws_ref, bs_ref, do_ref, dz_ref, dlg_ref, dlb_ref, dws_ref, dbs_ref):
        i = pl.program_id(0)

        @pl.when(i == 0)
        def _():
            dlg_ref[...] = jnp.zeros_like(dlg_ref)
            dlb_ref[...] = jnp.zeros_like(dlb_ref)
            dws_ref[...] = jnp.zeros_like(dws_ref)
            dbs_ref[...] = jnp.zeros_like(dbs_ref)

        gu, gv, u, xh, r, v, mixed = _gmlp_core(z_ref, lg_ref, lb_ref, ws_ref, bs_ref, dm)
        do = do_ref[...]
        dv_parts = []
        for g in range(dm.GG):
            sl = slice(g * GMLP_GROUP_W, (g + 1) * GMLP_GROUP_W)
            dog = do[:, sl]
            dz_ref[:, sl] = dog * mixed[g] * _gelu_grad(gu[:, sl])
            dmx = dog * u[:, sl]
            dbs_ref[g] += jnp.sum(dmx, axis=-1, keepdims=True)
            dws_ref[g] += _dot(dmx, v[:, sl], _NT)
            dv_parts.append(_dot(ws_ref[g], dmx, _TN))
        dv = jnp.concatenate(dv_parts, axis=-1) if dm.GG > 1 else dv_parts[0]
        dlg_ref[...] += jnp.sum(dv * xh, axis=0, keepdims=True)
        dlb_ref[...] += jnp.sum(dv, axis=0, keepdims=True)
        dxh = dv * lg_ref[...]
        dge = r * (dxh - jnp.mean(dxh, axis=-1, keepdims=True) - xh * jnp.mean(dxh * xh, axis=-1, keepdims=True))
        dz_ref[:, GW:2 * GW] = dge * _gelu_grad(gv)

    vec = pl.BlockSpec((1, GW), lambda i: (0, 0))
    wsp = pl.BlockSpec((dm.GG, CH, CH), lambda i: (0, 0, 0))
    bsp = pl.BlockSpec((dm.GG, CH, 1), lambda i: (0, 0, 0))
    return pl.pallas_call(
        body, name=name, grid=(T // CH,),
        in_specs=[pl.BlockSpec((CH, dm.N_IN), lambda i: (i, 0)), vec, vec, wsp, bsp,
                  pl.BlockSpec((CH, GW), lambda i: (i, (dm.ATTN_W + dm.SSM_W) // GW))],
        out_specs=[pl.BlockSpec((CH, 2 * GW), lambda i: (i, 0)), vec, vec, wsp, bsp],
        out_shape=[jax.ShapeDtypeStruct((T, 2 * GW), F32), jax.ShapeDtypeStruct((1, GW), F32),
                   jax.ShapeDtypeStruct((1, GW), F32), jax.ShapeDtypeStruct((dm.GG, CH, CH), F32),
                   jax.ShapeDtypeStruct((dm.GG, CH, 1), F32)],
        compiler_params=_cparams(("arbitrary",)),
    )(z, ln_g, ln_b, w_s, b_s, do)


def _disc_fn(lam_re, lam_im, ldt, b_re, b_im):
    dt = jnp.exp(ldt)
    mag = jnp.exp(lam_re * dt)
    a_re = mag * jnp.cos(lam_im * dt)
    a_im = mag * jnp.sin(lam_im * dt)
    den = lam_re * lam_re + lam_im * lam_im
    n_re = a_re - 1.0
    f_re = (n_re * lam_re + a_im * lam_im) / den
    f_im = (a_im * lam_re - n_re * lam_im) / den
    return a_re, a_im, f_re * b_re - f_im * b_im, f_re * b_im + f_im * b_re


def _s5_disc(lam_re, lam_im, ldt, bT_re, bT_im, name):
    nd, H, GP = bT_re.shape

    def body(lr_ref, li_ref, dt_ref, br_ref, bi_ref, ar_ref, ai_ref, bbr_ref, bbi_ref):
        for d in range(nd):
            a_re, a_im, bb_re, bb_im = _disc_fn(lr_ref[d], li_ref[d], dt_ref[d], br_ref[d], bi_ref[d])
            ar_ref[d] = a_re
            ai_ref[d] = a_im
            bbr_ref[d] = bb_re
            bbi_ref[d] = bb_im

    vs = jax.ShapeDtypeStruct((nd, 1, GP), F32)
    ms = jax.ShapeDtypeStruct((nd, H, GP), F32)
    return pl.pallas_call(body, name=name, out_shape=[vs, vs, ms, ms], compiler_params=_cparams())(
        lam_re, lam_im, ldt, bT_re, bT_im)


def _s5_disc_bwd(lam_re, lam_im, ldt, bT_re, bT_im, da_re, da_im, dbb_re, dbb_im, name):
    nd, H, GP = bT_re.shape

    def body(lr_ref, li_ref, dt_ref, br_ref, bi_ref, dar_ref, dai_ref, dbr_ref, dbi_ref,
             olr_ref, oli_ref, odt_ref, obr_ref, obi_ref):
        for d in range(nd):
            _, vjp = jax.vjp(_disc_fn, lr_ref[d], li_ref[d], dt_ref[d], br_ref[d], bi_ref[d])
            g = vjp((dar_ref[d], dai_ref[d], dbr_ref[d], dbi_ref[d]))
            olr_ref[d], oli_ref[d], odt_ref[d], obr_ref[d], obi_ref[d] = g

    vs = jax.ShapeDtypeStruct((nd, 1, GP), F32)
    ms = jax.ShapeDtypeStruct((nd, H, GP), F32)
    return pl.pallas_call(body, name=name, out_shape=[vs, vs, vs, ms, ms], compiler_params=_cparams())(
        lam_re, lam_im, ldt, bT_re, bT_im, da_re, da_im, dbb_re, dbb_im)


def _scan_pass(re_ref, im_ref, nsteps, ar, ai, ir, ii, reverse, store):
    def step(n, carry):
        sr, si = carry
        t = (nsteps - 1 - n) if reverse else n
        off = pl.multiple_of(t * 8, 8)
        nr = ar * sr - ai * si + re_ref[pl.ds(off, 8), :]
        ni = ar * si + ai * sr + im_ref[pl.ds(off, 8), :]
        if store:
            re_ref[pl.ds(off, 8), :] = nr
            im_ref[pl.ds(off, 8), :] = ni
        return nr, ni

    return lax.fori_loop(0, nsteps, step, (ir, ii), unroll=4)


def _cpow(ar, ai, n):
    rr = ri = None
    br, bi = ar, ai
    while n:
        if n & 1:
            rr, ri = (br, bi) if rr is None else (rr * br - ri * bi, rr * bi + ri * br)
        n >>= 1
        if n:
            br, bi = br * br - bi * bi, 2.0 * br * bi
    return rr, ri


def _row_of(x, k):
    rows = lax.broadcasted_iota(jnp.int32, x.shape, 0)
    return jnp.sum(jnp.where(rows == k, x, 0.0), axis=0, keepdims=True)


def _full_scan(re_ref, im_ref, nsteps, ar1, ai1, h0r, h0i, reverse):
    P = ar1.shape[1]
    ar, ai = jnp.broadcast_to(ar1, (8, P)), jnp.broadcast_to(ai1, (8, P))
    z = jnp.zeros((8, P), F32)
    er, ei = _scan_pass(re_ref, im_ref, nsteps, ar, ai, z, z, reverse, False)
    pr, pi = _cpow(ar1, ai1, nsteps)
    rows = lax.broadcasted_iota(jnp.int32, (8, P), 0)
    initr, initi = z, z
    cr, ci = h0r, h0i
    for k in (range(SCAN_SEGMENTS - 1, -1, -1) if reverse else range(SCAN_SEGMENTS)):
        initr = jnp.where(rows == k, cr, initr)
        initi = jnp.where(rows == k, ci, initi)
        ekr, eki = _row_of(er, k), _row_of(ei, k)
        cr, ci = ekr + pr * cr - pi * ci, eki + pr * ci + pi * cr
    _scan_pass(re_ref, im_ref, nsteps, ar, ai, initr, initi, reverse, True)
    return initr, initi, cr, ci


def _rows_loop(n, chunk, fn):
    def step(c, carry):
        fn(pl.multiple_of(c * chunk, chunk))
        return carry
    lax.fori_loop(0, n // chunk, step, 0)


def _s5_specs(dm, PB):
    nsb = dm.GP // PB
    per = (LANES * SSM_STATE // SSM_GROUP) // PB
    ul = pl.BlockSpec((dm.L, LANES), lambda j: (0, j // per))
    uc = pl.BlockSpec((dm.C, LANES), lambda j: (0, j // per))
    av = pl.BlockSpec((2, 1, PB), lambda j: (0, 0, j))
    bd = pl.BlockSpec((2, LANES, PB), lambda j: (0, j // per, j))
    cd = pl.BlockSpec((2, PB, LANES), lambda j: (0, j, j // per))
    dv = pl.BlockSpec((1, LANES), lambda j: (0, j // per))
    return PB, nsb, per, ul, uc, av, bd, cd, dv


def _s5_scan(ul, uc, a_re, a_im, bd_re, bd_im, cd_re, cd_im, dskip, dm, name, carry=None):
    PB, nsb, per, ul_s, uc_s, av, bd, cd, dv = _s5_specs(dm, STATE_BLOCK_FWD)
    L, C = dm.L, dm.C
    RC = _tile(L, 512, 8)
    RCC = _tile(C, 512, 8)

    def body(ul_ref, uc_ref, ar_ref, ai_ref, br_ref, bi_ref, cr_ref, ci_ref, d_ref, yl_ref, yc_ref,
             lre, lim, cre, cim):
        j = pl.program_id(0)

        @pl.when(j % per == 0)
        def _():
            yl_ref[...] = ul_ref[...] * d_ref[...]
            yc_ref[...] = uc_ref[...] * d_ref[...]

        for d in range(2):
            rev = d == 1
            ar1, ai1 = ar_ref[d], ai_ref[d]

            def proj(u_ref, re, im, chunk, n):
                def f(r0):
                    u = u_ref[pl.ds(r0, chunk), :]
                    re[pl.ds(r0, chunk), :] = _dot(u, br_ref[d], _NN)
                    im[pl.ds(r0, chunk), :] = _dot(u, bi_ref[d], _NN)
                _rows_loop(n, chunk, f)

            def readout(y_ref, re, im, chunk, n):
                def f(r0):
                    y_ref[pl.ds(r0, chunk), :] += (_dot(re[pl.ds(r0, chunk), :], cr_ref[d], _NN)
                                                   - _dot(im[pl.ds(r0, chunk), :], ci_ref[d], _NN))
                _rows_loop(n, chunk, f)

            z1 = jnp.zeros((1, PB), F32)
            proj(uc_ref, cre, cim, RCC, C)
            _, _, fr, fi = _full_scan(cre, cim, C // 8, ar1, ai1, z1, z1, rev)
            readout(yc_ref, cre, cim, RCC, C)
            proj(ul_ref, lre, lim, RC, L)
            _full_scan(lre, lim, L // 8, ar1, ai1, fr, fi, rev)
            readout(yl_ref, lre, lim, RC, L)

    body, cxs, c_out, c_scratch = _ride(body, 9, 2, 4, carry, nsb)
    any_spec = pl.BlockSpec(memory_space=pl.ANY)
    outs = pl.pallas_call(
        body, name=name, grid=(nsb,),
        in_specs=[ul_s, uc_s, av, av, bd, bd, cd, cd, dv] + [any_spec] * len(cxs),
        out_specs=[ul_s, uc_s] + [any_spec] * len(cxs),
        out_shape=[jax.ShapeDtypeStruct((L, dm.SSM_W), F32), jax.ShapeDtypeStruct((C, dm.SSM_W), F32)] + c_out,
        scratch_shapes=[pltpu.VMEM((L, PB), F32), pltpu.VMEM((L, PB), F32),
                        pltpu.VMEM((C, PB), F32), pltpu.VMEM((C, PB), F32)] + c_scratch,
        compiler_params=_cparams(("arbitrary",)),
    )(ul, uc, a_re, a_im, bd_re, bd_im, cd_re, cd_im, dskip, *cxs)
    return (outs[:2], outs[2:]) if carry else outs


def _s5_scan_bwd(ul, uc, dyl, dyc, a_re, a_im, bd_re, bd_im, cd_re, cd_im, bdT_re, bdT_im, cdT_re, cdT_im,
                 dskip, dm, name, carry=None):
    PB, nsb, per, ul_s, uc_s, av, bd, cd, dv = _s5_specs(dm, STATE_BLOCK)
    L, C = dm.L, dm.C
    RC = _tile(L, 512, 8)
    RCC = _tile(C, 512, 8)
    bdT = pl.BlockSpec((2, PB, LANES), lambda j: (0, j, j // per))
    cdT = pl.BlockSpec((2, LANES, PB), lambda j: (0, j // per, j))
    dbd = pl.BlockSpec((2, 1, LANES, PB), lambda j: (0, j, 0, 0))
    dcd = pl.BlockSpec((2, 1, PB, LANES), lambda j: (0, j, 0, 0))

    def body(ul_ref, uc_ref, dyl_ref, dyc_ref, ar_ref, ai_ref, br_ref, bi_ref, cr_ref, ci_ref,
             brT_ref, biT_ref, crT_ref, ciT_ref, d_ref,
             dul_ref, duc_ref, dar_ref, dai_ref, dbr_ref, dbi_ref, dcr_ref, dci_ref, dd_ref,
             lre, lim, cre, cim, gre, gim, hre, him):
        j = pl.program_id(0)

        @pl.when(j % per == 0)
        def _():
            dul_ref[...] = dyl_ref[...] * d_ref[...]
            duc_ref[...] = dyc_ref[...] * d_ref[...]
            dd_ref[...] = (jnp.sum(dyl_ref[...] * ul_ref[...], axis=0, keepdims=True)
                           + jnp.sum(dyc_ref[...] * uc_ref[...], axis=0, keepdims=True))

        for d in range(2):
            rev = d == 1
            ar1, ai1 = ar_ref[d], ai_ref[d]
            z1 = jnp.zeros((1, PB), F32)

            def proj(u_ref, w_re, w_im, sign, re, im, chunk, n):
                def f(r0):
                    u = u_ref[pl.ds(r0, chunk), :]
                    re[pl.ds(r0, chunk), :] = _dot(u, w_re, _NN)
                    im[pl.ds(r0, chunk), :] = sign * _dot(u, w_im, _NN)
                _rows_loop(n, chunk, f)

            proj(uc_ref, br_ref[d], bi_ref[d], 1.0, cre, cim, RCC, C)
            icr, ici, fr, fi = _full_scan(cre, cim, C // 8, ar1, ai1, z1, z1, rev)
            proj(ul_ref, br_ref[d], bi_ref[d], 1.0, lre, lim, RC, L)
            ilr, ili, _, _ = _full_scan(lre, lim, L // 8, ar1, ai1, fr, fi, rev)
            proj(dyl_ref, crT_ref[d], ciT_ref[d], -1.0, gre, gim, RC, L)
            _, _, gfr, gfi = _full_scan(gre, gim, L // 8, ar1, -ai1, z1, z1, not rev)
            proj(dyc_ref, crT_ref[d], ciT_ref[d], -1.0, hre, him, RCC, C)
            _full_scan(hre, him, C // 8, ar1, -ai1, gfr, gfi, not rev)

            dar = jnp.zeros((8, PB), F32)
            dai = jnp.zeros((8, PB), F32)
            dbr = jnp.zeros((LANES, PB), F32)
            dbi = jnp.zeros((LANES, PB), F32)
            dcr = jnp.zeros((PB, LANES), F32)
            dci = jnp.zeros((PB, LANES), F32)
            for (u_ref, dy_ref, du_ref, sre, sim, are, aim, inr, ini, n, chunk) in (
                    (ul_ref, dyl_ref, dul_ref, lre, lim, gre, gim, ilr, ili, L, RC),
                    (uc_ref, dyc_ref, duc_ref, cre, cim, hre, him, icr, ici, C, RCC)):
                nst = n // 8

                def da_step(t, carry, sre=sre, sim=sim, are=are, aim=aim, nst=nst):
                    xr, xi = carry
                    tp = (t + 1) if rev else (t - 1)
                    o = pl.multiple_of(t * 8, 8)
                    op = pl.multiple_of(tp * 8, 8)
                    g_r, g_i = are[pl.ds(o, 8), :], aim[pl.ds(o, 8), :]
                    p_r, p_i = sre[pl.ds(op, 8), :], sim[pl.ds(op, 8), :]
                    return xr + g_r * p_r + g_i * p_i, xi + g_i * p_r - g_r * p_i

                lo_t, hi_t = (0, nst - 1) if rev else (1, nst)
                dar, dai = lax.fori_loop(lo_t, hi_t, da_step, (dar, dai))
                e0 = (nst - 1) * 8 if rev else 0
                g_r, g_i = are[pl.ds(e0, 8), :], aim[pl.ds(e0, 8), :]
                dar = dar + g_r * inr + g_i * ini
                dai = dai + g_i * inr - g_r * ini

                def mats(c, carry, u_ref=u_ref, dy_ref=dy_ref, du_ref=du_ref, sre=sre, sim=sim, are=are, aim=aim,
                         chunk=chunk):
                    xbr, xbi, xcr, xci = carry
                    r0 = pl.multiple_of(c * chunk, chunk)
                    u = u_ref[pl.ds(r0, chunk), :]
                    dy = dy_ref[pl.ds(r0, chunk), :]
                    g_r, g_i = are[pl.ds(r0, chunk), :], aim[pl.ds(r0, chunk), :]
                    du_ref[pl.ds(r0, chunk), :] += _dot(g_r, brT_ref[d], _NN) + _dot(g_i, biT_ref[d], _NN)
                    xbr = xbr + _dot(u, g_r, _TN)
                    xbi = xbi + _dot(u, g_i, _TN)
                    xcr = xcr + _dot(sre[pl.ds(r0, chunk), :], dy, _TN)
                    xci = xci - _dot(sim[pl.ds(r0, chunk), :], dy, _TN)
                    return xbr, xbi, xcr, xci

                dbr, dbi, dcr, dci = lax.fori_loop(0, n // chunk, mats, (dbr, dbi, dcr, dci))
            dar_ref[d] = jnp.sum(dar, axis=0, keepdims=True)
            dai_ref[d] = jnp.sum(dai, axis=0, keepdims=True)
            dbr_ref[d, 0] = dbr
            dbi_ref[d, 0] = dbi
            dcr_ref[d, 0] = dcr
            dci_ref[d, 0] = dci

    W, GP = dm.SSM_W, dm.GP
    body, cxs, c_out, c_scratch = _ride(body, 15, 9, 8, carry, nsb)
    any_spec = pl.BlockSpec(memory_space=pl.ANY)
    outs = pl.pallas_call(
        body, name=name, grid=(nsb,),
        in_specs=[ul_s, uc_s, ul_s, uc_s, av, av, bd, bd, cd, cd, bdT, bdT, cdT, cdT, dv] + [any_spec] * len(cxs),
        out_specs=[ul_s, uc_s, av, av, dbd, dbd, dcd, dcd, dv] + [any_spec] * len(cxs),
        out_shape=[jax.ShapeDtypeStruct((L, W), F32), jax.ShapeDtypeStruct((C, W), F32),
                   jax.ShapeDtypeStruct((2, 1, GP), F32), jax.ShapeDtypeStruct((2, 1, GP), F32),
                   jax.ShapeDtypeStruct((2, nsb, LANES, PB), F32), jax.ShapeDtypeStruct((2, nsb, LANES, PB), F32),
                   jax.ShapeDtypeStruct((2, nsb, PB, LANES), F32), jax.ShapeDtypeStruct((2, nsb, PB, LANES), F32),
                   jax.ShapeDtypeStruct((1, W), F32)] + c_out,
        scratch_shapes=[pltpu.VMEM((L, PB), F32), pltpu.VMEM((L, PB), F32),
                        pltpu.VMEM((C, PB), F32), pltpu.VMEM((C, PB), F32),
                        pltpu.VMEM((L, PB), F32), pltpu.VMEM((L, PB), F32),
                        pltpu.VMEM((C, PB), F32), pltpu.VMEM((C, PB), F32)] + c_scratch,
        compiler_params=_cparams(("arbitrary",)),
    )(ul, uc, dyl, dyc, a_re, a_im, bd_re, bd_im, cd_re, cd_im, bdT_re, bdT_im, cdT_re, cdT_im, dskip, *cxs)
    return (outs[:9], outs[9:]) if carry else outs


def _glu(y, w, b, name):
    T, W = y.shape
    tm = _tile(T, 544, 16)

    def body(y_ref, w_ref, b_ref, o_ref):
        g = _gelu(y_ref[...])
        o_ref[...] = (g * _sigmoid(_dot(g, w_ref[...], _NN) + b_ref[...])).astype(BF16)

    return pl.pallas_call(
        body, name=name, grid=(T // tm,),
        in_specs=[pl.BlockSpec((tm, W), lambda i: (i, 0)), pl.BlockSpec((W, W), lambda i: (0, 0)),
                  pl.BlockSpec((1, W), lambda i: (0, 0))],
        out_specs=pl.BlockSpec((tm, W), lambda i: (i, 0)),
        out_shape=jax.ShapeDtypeStruct((T, W), BF16),
        compiler_params=_cparams(("parallel",)),
    )(y, w, b)


def _glu_bwd(y, w, b, do, do_col, name):
    T, W = y.shape
    tm = _tile(T, 544, 16)

    def body(y_ref, w_ref, b_ref, do_ref, dy_ref, dw_ref, db_ref):
        i = pl.program_id(0)

        @pl.when(i == 0)
        def _():
            dw_ref[...] = jnp.zeros_like(dw_ref)
            db_ref[...] = jnp.zeros_like(db_ref)

        y = y_ref[...]
        do = do_ref[...]
        g = _gelu(y)
        s = _sigmoid(_dot(g, w_ref[...], _NN) + b_ref[...])
        dpre = do * g * s * (1.0 - s)
        db_ref[...] += jnp.sum(dpre, axis=0, keepdims=True)
        dw_ref[...] += _dot(g, dpre, _TN)
        dg = do * s + _dot(dpre, w_ref[...], _NT)
        dy_ref[...] = dg * _gelu_grad(y)

    row = pl.BlockSpec((tm, W), lambda i: (i, 0))
    wsp = pl.BlockSpec((W, W), lambda i: (0, 0))
    vec = pl.BlockSpec((1, W), lambda i: (0, 0))
    return pl.pallas_call(
        body, name=name, grid=(T // tm,),
        in_specs=[row, wsp, vec, pl.BlockSpec((tm, W), lambda i: (i, do_col))], out_specs=[row, wsp, vec],
        out_shape=[jax.ShapeDtypeStruct((T, W), F32), jax.ShapeDtypeStruct((W, W), F32),
                   jax.ShapeDtypeStruct((1, W), F32)],
        compiler_params=_cparams(("arbitrary",)),
    )(y, w, b, do)


def _loss_head(X, g, target, C, name):
    T, D = X.shape
    tm = _tile(math.gcd(C, T - C), 512, 16)
    nc = C // tm

    def body(x_ref, g_ref, t_ref, loss_ref, dx_ref, dg_ref):
        i = pl.program_id(0)

        @pl.when(i == 0)
        def _():
            loss_ref[...] = jnp.zeros_like(loss_ref)
            dg_ref[...] = jnp.zeros_like(dg_ref)

        @pl.when(i < nc)
        def _():
            dx_ref[...] = jnp.zeros_like(dx_ref)

        @pl.when(i >= nc)
        def _():
            x = x_ref[...]
            gv = g_ref[...]
            r = lax.rsqrt(jnp.mean(x * x, axis=-1, keepdims=True) + NORM_EPS)
            xh = x * r
            err = xh * gv - t_ref[...]
            loss_ref[...] += 0.5 * jnp.sum(jnp.mean(err * err, axis=-1, keepdims=True), axis=0, keepdims=True)
            dy = err * (1.0 / D)
            dg_ref[...] += jnp.sum(dy * xh, axis=0, keepdims=True)
            dxh = dy * gv
            dx_ref[...] = r * (dxh - xh * jnp.mean(dxh * xh, axis=-1, keepdims=True))

    row = pl.BlockSpec((tm, D), lambda i: (i, 0))
    return pl.pallas_call(
        body, name=name, grid=(T // tm,),
        in_specs=[row, pl.BlockSpec((1, D), lambda i: (0, 0)),
                  pl.BlockSpec((tm, D), lambda i: (jnp.maximum(i - nc, 0), 0))],
        out_specs=[pl.BlockSpec((1, 1), lambda i: (0, 0)), row, pl.BlockSpec((1, D), lambda i: (0, 0))],
        out_shape=[jax.ShapeDtypeStruct((1, 1), F32), jax.ShapeDtypeStruct((T, D), F32),
                   jax.ShapeDtypeStruct((1, D), F32)],
        compiler_params=_cparams(("arbitrary",)),
    )(X, g, target)


def _ada_fwd(cond, w, b, name):
    nl, D, Ns = w.shape
    tn = _tile(Ns, 1024, LANES)

    def body(c_ref, w_ref, b_ref, o_ref):
        c = c_ref[...]
        o_ref[0] = _dot(c * _sigmoid(c), w_ref[0], _NN) + b_ref[0]

    return pl.pallas_call(
        body, name=name, grid=(nl, Ns // tn),
        in_specs=[pl.BlockSpec((16, D), lambda l, j: (0, 0)), pl.BlockSpec((1, D, tn), lambda l, j: (l, 0, j)),
                  pl.BlockSpec((1, 1, tn), lambda l, j: (l, 0, j))],
        out_specs=pl.BlockSpec((1, 16, tn), lambda l, j: (l, 0, j)),
        out_shape=jax.ShapeDtypeStruct((nl, 16, Ns), F32),
        compiler_params=_cparams(("parallel", "parallel")),
    )(cond, w, b)


def _ada_bwd(cond, w, dmod, name):
    nl, D, Ns = w.shape
    tn = _tile(Ns, 1024, LANES)

    def body(c_ref, w_ref, dm_ref, dw_ref, da_ref):
        j = pl.program_id(1)

        @pl.when(j == 0)
        def _():
            da_ref[...] = jnp.zeros_like(da_ref)

        c = c_ref[...]
        dw_ref[0] = _dot(c * _sigmoid(c), dm_ref[0], _TN)
        da_ref[0] += _dot(dm_ref[0], w_ref[0], _NT)

    return pl.pallas_call(
        body, name=name, grid=(nl, Ns // tn),
        in_specs=[pl.BlockSpec((16, D), lambda l, j: (0, 0)), pl.BlockSpec((1, D, tn), lambda l, j: (l, 0, j)),
                  pl.BlockSpec((1, 16, tn), lambda l, j: (l, 0, j))],
        out_specs=[pl.BlockSpec((1, D, tn), lambda l, j: (l, 0, j)), pl.BlockSpec((1, 16, D), lambda l, j: (l, 0, 0))],
        out_shape=[jax.ShapeDtypeStruct((nl, D, Ns), F32), jax.ShapeDtypeStruct((nl, 16, D), F32)],
        compiler_params=_cparams(("parallel", "arbitrary")),
    )(cond, w, dmod)


def _cctx_grad(c_ctx, dact4, name):
    nch, nl, _, D = dact4.shape

    def body(c_ref, da_ref, o_ref):
        acc = jnp.zeros((1, D), F32)
        for ch in range(nch):
            for l in range(nl):
                acc = acc + da_ref[ch, l, 8:9, :]
        c = c_ref[...]
        s = _sigmoid(c)
        o_ref[...] = acc * (s + c * s * (1.0 - s))

    return pl.pallas_call(body, name=name, out_shape=jax.ShapeDtypeStruct((1, D), F32),
                          compiler_params=_cparams())(c_ctx, dact4)


def _adamw(w, g, m, v, name):
    R, W = w.shape
    tr = _tile(R, max(16, (1 << 19) // W // 16 * 16), 16 if g.dtype == BF16 else 8)
    c1 = 1.0 - ADAM_B1 ** ADAM_STEP
    c2 = 1.0 - ADAM_B2 ** ADAM_STEP

    def body(w_ref, g_ref, m_ref, v_ref, d_ref, nm_ref, nv_ref, g32_ref):
        g = g_ref[...].astype(F32)
        g32_ref[...] = g
        m = ADAM_B1 * m_ref[...] + (1.0 - ADAM_B1) * g
        v = ADAM_B2 * v_ref[...] + (1.0 - ADAM_B2) * (g * g)
        nm_ref[...] = m
        nv_ref[...] = v
        d_ref[...] = -ADAM_LR * ((m / c1) / (jnp.sqrt(v / c2) + ADAM_EPS) + ADAM_WD * w_ref[...])

    blk = pl.BlockSpec((tr, W), lambda i: (i, 0))
    s = jax.ShapeDtypeStruct((R, W), F32)
    return pl.pallas_call(
        body, name=name, grid=(R // tr,), in_specs=[blk] * 4, out_specs=[blk] * 4, out_shape=[s, s, s, s],
        compiler_params=_cparams(("parallel",)),
    )(w, g, m, v)


def _adamw_nd(w, g, m, v, name):
    shp = w.shape
    two = (math.prod(shp[:-1]), shp[-1])
    outs = _adamw(w.reshape(two), g.reshape(two), m.reshape(two), v.reshape(two), name)
    return [o.reshape(shp) for o in outs]


def _interleave(a, n):
    return a.reshape(SCAN_SEGMENTS, n // SCAN_SEGMENTS, -1).transpose(1, 0, 2).reshape(n, -1)


def _deinterleave(a, n):
    return a.reshape(n // SCAN_SEGMENTS, SCAN_SEGMENTS, -1).transpose(1, 0, 2).reshape(n, -1)


def _blockdiag_in(bbT, dm):
    eye = jnp.eye(dm.SG, dtype=F32)
    b4 = bbT.reshape(2, SSM_GROUP, dm.SG, SSM_STATE)
    return jnp.einsum("dhgp,gk->dghkp", b4, eye).reshape(2, dm.SSM_W, dm.GP)


def _blockdiag_in_T(dbd, dm):
    gpb = STATE_BLOCK // SSM_STATE
    per = (LANES // SSM_GROUP) // gpb
    d8 = dbd.reshape(2, dm.SSM_W // LANES, per, per, gpb, SSM_GROUP, gpb, SSM_STATE)
    x = jnp.einsum("duaabhbp->duabhp", d8).reshape(2, dm.SG, SSM_GROUP, SSM_STATE)
    return x.transpose(0, 2, 1, 3).reshape(2, SSM_GROUP, dm.GP)


def _blockdiag_out(c, dm):
    eye = jnp.eye(dm.SG, dtype=F32)
    return jnp.einsum("dghp,gk->dgpkh", c, eye).reshape(2, dm.GP, dm.SSM_W)


def _blockdiag_out_T(dcd, dm):
    gpb = STATE_BLOCK // SSM_STATE
    per = (LANES // SSM_GROUP) // gpb
    d8 = dcd.reshape(2, dm.SSM_W // LANES, per, gpb, SSM_STATE, per, gpb, SSM_GROUP)
    return jnp.einsum("duabpabh->duabhp", d8).reshape(2, dm.SG, SSM_GROUP, SSM_STATE)


_BIG = ("w_in", "w_out", "ssm_w_glu", "ffn_w_up", "ffn_w_down")
_SHARD_AXIS = {"w_in": 2, "w_out": 1, "ssm_w_glu": 1, "ffn_w_up": 2, "ffn_w_down": 1}
_SMALL = ("g_mix", "g_ffn", "attn_sink", "ssm_lambda_re", "ssm_lambda_im", "ssm_log_dt", "ssm_b_re", "ssm_b_im",
          "ssm_c_re", "ssm_c_im", "ssm_d", "ssm_b_glu", "gmlp_ln_g", "gmlp_ln_b", "gmlp_w_s", "gmlp_b_s",
          "ffn_conv_b", "g_final", "ffn_conv_w")
_WEIGHTS = ("c_ctx", "w_ada", "b_ada", "g_mix", "g_ffn", "w_in", "w_out", "attn_sink", "ssm_lambda_re",
            "ssm_lambda_im", "ssm_log_dt", "ssm_b_re", "ssm_b_im", "ssm_c_re", "ssm_c_im", "ssm_d", "ssm_w_glu",
            "ssm_b_glu", "gmlp_ln_g", "gmlp_ln_b", "gmlp_w_s", "gmlp_b_s", "ffn_w_up", "ffn_conv_w", "ffn_conv_b",
            "ffn_w_down", "g_final")


def _step(P, M, V, x, c, ctx, loss_target):
    dm = _Dims()
    D, T, C, L = dm.D, dm.T, dm.C, dm.L
    mx, my, mc = lax.axis_index("x"), lax.axis_index("y"), lax.axis_index("c")
    chip = 2 * mx + my
    dev = 2 * chip + mc

    conv_sh = P["ffn_conv_w"].shape
    small_f = _pack([c, P["ffn_conv_w"]], F32, 8)
    small_all = _allgather8(small_f, "ag_small_fwd")
    c_all, conv_all = [], []
    for d in range(N_DEV):
        cd_, cw_ = _unpack(small_all[d], [(1, D), conv_sh])
        c_all.append(cd_)
        conv_all.append(cw_)
    conv_w = jnp.concatenate([conv_all[2 * j] for j in range(N_CHIPS)], axis=2)
    cond = jnp.concatenate(c_all + [P["c_ctx"].reshape(1, D), jnp.zeros((16 - N_DEV - 1, D), F32)], axis=0)

    n_sh = P["w_ada"].shape[2]
    b_sh = lax.dynamic_slice_in_dim(P["b_ada"], chip * n_sh, n_sh, axis=1)[:, None, :]
    mods_sh = _ada_fwd(cond, P["w_ada"], b_sh, "ada_fwd")
    mods4 = _comm(mods_sh, group="chips", by_peer=False, name="ag_mods")
    mods = jnp.concatenate([mods4[j] for j in range(N_CHIPS)], axis=-1)
    mod_lat = lax.dynamic_index_in_dim(mods, dev, axis=1, keepdims=False)
    mod_ctx = mods[:, N_DEV]
    modp = jnp.stack([mod_ctx, mod_lat], axis=1).reshape(DEPTH, 2, N_MOD, D)

    wb = {n: P[n].astype(BF16) for n in _BIG}

    def layer_weights(g):
        return dict(w_in=jnp.concatenate([g["w_in"][j] for j in range(N_CHIPS)], axis=1),
                    w_glu=g["ssm_w_glu"].reshape(dm.SSM_W, dm.SSM_W),
                    w_out=_Mat(g["w_out"], axis=0), w_dn=_Mat(g["ffn_w_down"], axis=0),
                    w_up=_Mat(g["ffn_w_up"], axis=1))

    g0 = _comm_alone("chips_ag", [wb[n][0] for n in _BIG], "ag_w_chips")
    Wl = [layer_weights(dict(zip(_BIG, g0)))]
    cos, sin = _rope_tables(dm)

    X = jnp.concatenate([ctx.reshape(C, D), x.reshape(L, D)], axis=0)
    saved = []
    for l in range(DEPTH):
        mp = modp[l]
        sh_a, sc_a, gt_a, sh_f, sc_f, gt_f = [mp[:, k] for k in range(N_MOD)]
        w_in, w_out, w_glu, w_up, w_dn = [Wl[l][k] for k in ("w_in", "w_out", "w_glu", "w_up", "w_dn")]
        g_mix, g_ffn = P["g_mix"][l][None], P["g_ffn"][l][None]

        h = _normmod(X, g_mix, sh_a, sc_a, C, "normmod_a")
        nxt = {}

        def riding(*names):
            return ("chips_ag", [wb[n][l + 1] for n in names]) if l + 1 < DEPTH else None

        def landed(names, gathered):
            nxt.update(zip(names, gathered) if l + 1 < DEPTH else ())

        z = _matmul(h, w_in, mode="nn", name="mm_in")
        q, k2, v2 = _attn_prep(z, cos, sin, dm, "attn_prep")
        sinkcol = jnp.repeat(P["attn_sink"][l].reshape(2, GQA_RATIO), ATTN_BLOCK, axis=1)[..., None]
        o_attn = _attention(q, k2, v2, sinkcol, dm, "attention", carry=riding("w_in"))
        if l + 1 < DEPTH:
            o_attn, got = o_attn
            landed(("w_in",), got)
        u = z[:, dm.OFF_S:dm.OFF_GU]
        ul, uc = _interleave(u[C:], L), _interleave(u[:C], C)
        lam_re = P["ssm_lambda_re"][l].reshape(2, 1, dm.GP)
        lam_im = P["ssm_lambda_im"][l].reshape(2, 1, dm.GP)
        ldt = jnp.repeat(P["ssm_log_dt"][l], SSM_STATE, axis=-1).reshape(2, 1, dm.GP)
        bT_re = P["ssm_b_re"][l].reshape(2, dm.GP, SSM_GROUP).transpose(0, 2, 1)
        bT_im = P["ssm_b_im"][l].reshape(2, dm.GP, SSM_GROUP).transpose(0, 2, 1)
        a_re, a_im, bbT_re, bbT_im = _s5_disc(lam_re, lam_im, ldt, bT_re, bT_im, "s5_disc")
        bd_re, bd_im = _blockdiag_in(bbT_re, dm).astype(BF16), _blockdiag_in(bbT_im, dm).astype(BF16)
        cd_re = _blockdiag_out(P["ssm_c_re"][l], dm).astype(BF16)
        cd_im = _blockdiag_out(P["ssm_c_im"][l], dm).astype(BF16)
        dskip = P["ssm_d"][l][None]
        ys = _s5_scan(ul, uc, a_re, a_im, bd_re, bd_im, cd_re, cd_im, dskip, dm, "s5_scan",
                      carry=riding("w_out", "ssm_w_glu"))
        if l + 1 < DEPTH:
            ys, got = ys
            landed(("w_out", "ssm_w_glu"), got)
        yl, yc = ys
        y = jnp.concatenate([_deinterleave(yc, C), _deinterleave(yl, L)], axis=0)
        b_glu = P["ssm_b_glu"][l][None]
        o_ssm = _glu(y, w_glu, b_glu, "glu")
        ln_g, ln_b = P["gmlp_ln_g"][l][None], P["gmlp_ln_b"][l][None]
        w_s = P["gmlp_w_s"][l].astype(BF16)
        b_s = P["gmlp_b_s"][l][..., None]
        o_gmlp = _gmlp(z, ln_g, ln_b, w_s, b_s, dm, "gmlp")
        o_cat = jnp.concatenate([o_attn, o_ssm, o_gmlp], axis=-1)
        mix, X1 = _matmul_residual(o_cat, w_out, X, gt_a, C, "mm_out")
        h2 = _normmod(X1, g_ffn, sh_f, sc_f, C, "normmod_f")
        up = _matmul(h2, w_up, mode="nn", name="mm_up", out_dtype=BF16, out_split=(1, 2),
                     carry=riding("ffn_w_up"))
        if l + 1 < DEPTH:
            up, got = up
            landed(("ffn_w_up",), got)
        cw, cb = conv_w[l], P["ffn_conv_b"][l][None]
        act = _convact(up, cw, cb, C, "convact")
        ffn, X2, *got = _matmul_residual(act, w_dn, X1, gt_f, C, "mm_down", carry=riding("ffn_w_down"))
        landed(("ffn_w_down",), got[0] if got else ())
        if l + 1 < DEPTH:
            Wl.append(layer_weights(nxt))
        saved.append(dict(X=X, h=h, z=z, q=q, k2=k2, v2=v2, sinkcol=sinkcol, ul=ul, uc=uc, a_re=a_re, a_im=a_im,
                          bd_re=bd_re, bd_im=bd_im, cd_re=cd_re, cd_im=cd_im, y=y, o_cat=o_cat, mix=mix, X1=X1, h2=h2,
                          up=up, act=act, ffn=ffn, lam_re=lam_re, lam_im=lam_im, ldt=ldt, bT_re=bT_re,
                          bT_im=bT_im))
        X = X2

    loss_l, dX, dg_final = _loss_head(X, P["g_final"][None], loss_target.reshape(L, D), C, "loss_head")
    loss = lax.psum(loss_l[0, 0], ("x", "y", "c"))

    G = {n: [None] * DEPTH for n in _WEIGHTS if n not in ("c_ctx", "w_ada", "b_ada", "g_final")}
    dmod = [None] * DEPTH
    Gred = {n: [None] * DEPTH for n in _BIG}
    ready = None

    def hosted(fn, store, kind, src, names):
        if src is None:
            return fn(None)
        out, got = fn((kind, [src[n] for n in names]))
        store.update(zip(names, got))
        return out

    def reduce_parts(parts, names=_BIG):
        return {n: _sum_slots(parts[n], BF16, "sum_all") for n in names}

    def file_shared(shared, layer):
        for n in _BIG:
            Gred[n][layer] = shared[n].reshape(P[n].shape[1:])

    for l in reversed(range(DEPTH)):
        S = saved[l]
        parts, shared = {}, {}
        mp = modp[l]
        sh_a, sc_a, gt_a, sh_f, sc_f, gt_f = [mp[:, k] for k in range(N_MOD)]
        w_in, w_out, w_glu, w_up, w_dn = [Wl[l][k] for k in ("w_in", "w_out", "w_glu", "w_up", "w_dn")]
        g_mix, g_ffn = P["g_mix"][l][None], P["g_ffn"][l][None]
        cw, cb = conv_w[l], P["ffn_conv_b"][l][None]

        dffn, dgt_f = _gate_bwd(dX, S["ffn"], gt_f, C, "gate_bwd")
        dact = hosted(lambda c: _matmul(dffn, w_dn, mode="nt", name="mm_down_dx", out_dtype=BF16, carry=c),
                      parts, "all_rs", ready, ("w_in",))
        G["ffn_w_down"][l] = hosted(
            lambda c: _matmul(S["act"], dffn, mode="tn", name="mm_down_dw", out_dtype=BF16, out_split=(0, N_CHIPS),
                              tm_cap=1408, tk_cap=2176, carry=c), parts, "all_rs", ready, ("w_out", "ssm_w_glu"))
        dup, dcw, dcb = _convact_bwd(S["up"], cw, cb, dact, C, "convact_bwd")
        G["ffn_conv_w"][l], G["ffn_conv_b"][l] = dcw, dcb[0]
        dh2 = hosted(lambda c: _matmul(_Mat(dup, axis=1), w_up, mode="nt", name="mm_up_dx", tk_cap=2816, carry=c),
                     parts, "all_rs", ready, ("ffn_w_down",))
        G["ffn_w_up"][l] = _matmul(S["h2"], _Mat(dup, axis=1), mode="tn", name="mm_up_dw", out_dtype=BF16,
                                   out_split=(1, N_CHIPS), tk_cap=2176)
        red = reduce_parts(parts, ("w_in", "w_out", "ssm_w_glu", "ffn_w_down")) if ready is not None else None
        dX1, dg_f, dsh_f, dsc_f = _normmod_bwd(S["X1"], g_ffn, sc_f, dh2, dX, C, "normmod_bwd")
        G["g_ffn"][l] = dg_f[0]
        dmix, dgt_a = _gate_bwd(dX1, S["mix"], gt_a, C, "gate_bwd")
        do = hosted(lambda c: _matmul(dmix, w_out, mode="nt", name="mm_out_dx", carry=c),
                    shared, "sib_ag", red, ("w_in", "w_out", "ssm_w_glu"))
        G["w_out"][l] = hosted(
            lambda c: _matmul(S["o_cat"], dmix, mode="tn", name="mm_out_dw", out_dtype=BF16, out_split=(0, N_CHIPS),
                              tk_cap=2176, carry=c), shared, "sib_ag", red, ("ffn_w_down",))
        do_attn = do_ssm = do_gmlp = do
        ln_g, ln_b = P["gmlp_ln_g"][l][None], P["gmlp_ln_b"][l][None]
        w_s = P["gmlp_w_s"][l].astype(BF16)
        b_s = P["gmlp_b_s"][l][..., None]
        dz_g, dlg, dlb, dws, dbs = _gmlp_bwd(S["z"], ln_g, ln_b, w_s, b_s, do_gmlp, dm, "gmlp_bwd")
        G["gmlp_ln_g"][l], G["gmlp_ln_b"][l], G["gmlp_w_s"][l], G["gmlp_b_s"][l] = dlg[0], dlb[0], dws, dbs[..., 0]
        b_glu = P["ssm_b_glu"][l][None]
        dy, dwglu, dbglu = _glu_bwd(S["y"], w_glu, b_glu, do_ssm, dm.ATTN_W // dm.SSM_W, "glu_bwd")
        G["ssm_w_glu"][l], G["ssm_b_glu"][l] = dwglu.astype(BF16).reshape(N_CHIPS, -1, dm.SSM_W), dbglu[0]
        dyl, dyc = _interleave(dy[C:], L), _interleave(dy[:C], C)
        tr = lambda a: jnp.swapaxes(a, 1, 2)
        dskip = P["ssm_d"][l][None]
        (dul, duc, da_re, da_im, dbd_re, dbd_im, dcd_re, dcd_im, dd) = hosted(
            lambda c: _s5_scan_bwd(
                S["ul"], S["uc"], dyl, dyc, S["a_re"], S["a_im"], S["bd_re"], S["bd_im"], S["cd_re"], S["cd_im"],
                tr(S["bd_re"]), tr(S["bd_im"]), tr(S["cd_re"]), tr(S["cd_im"]), dskip, dm, "s5_scan_bwd", carry=c),
            parts, "all_rs", ready, ("ffn_w_up",))
        red_up = reduce_parts(parts, ("ffn_w_up",)) if ready is not None else None
        dz_s = jnp.concatenate([_deinterleave(duc, C), _deinterleave(dul, L)], axis=0)
        dlr, dli, dldt, dbTr, dbTi = _s5_disc_bwd(S["lam_re"], S["lam_im"], S["ldt"], S["bT_re"], S["bT_im"],
                                                  da_re, da_im, _blockdiag_in_T(dbd_re, dm),
                                                  _blockdiag_in_T(dbd_im, dm), "s5_disc_bwd")
        G["ssm_lambda_re"][l] = dlr.reshape(2, dm.SG, SSM_STATE)
        G["ssm_lambda_im"][l] = dli.reshape(2, dm.SG, SSM_STATE)
        G["ssm_log_dt"][l] = jnp.sum(dldt.reshape(2, dm.SG, SSM_STATE), axis=-1)
        G["ssm_b_re"][l] = dbTr.transpose(0, 2, 1).reshape(2, dm.SG, SSM_STATE, SSM_GROUP)
        G["ssm_b_im"][l] = dbTi.transpose(0, 2, 1).reshape(2, dm.SG, SSM_STATE, SSM_GROUP)
        G["ssm_c_re"][l] = _blockdiag_out_T(dcd_re, dm)
        G["ssm_c_im"][l] = _blockdiag_out_T(dcd_im, dm)
        G["ssm_d"][l] = dd[0]
        dq, pk, pv, dkc, dvc, dsk = hosted(
            lambda c: _attention_bwd(S["q"], S["k2"], S["v2"], S["sinkcol"], do_attn, dm, "attention_bwd", carry=c),
            shared, "sib_ag", red_up, ("ffn_w_up",))
        G["attn_sink"][l] = dsk[:, :, 0].reshape(dm.NH)
        dk2 = _kv_reduce(pk, dkc, dm, "kv_reduce")
        dv2 = _kv_reduce(pv, dvc, dm, "kv_reduce")
        dz_a = _attn_prep_bwd(dq, dk2, dv2, cos, sin, dm, "attn_prep_bwd")
        dz = jnp.concatenate([dz_a, dz_s, dz_g], axis=-1).astype(BF16)
        dh = _matmul(dz, w_in, mode="nt", name="mm_in_dx")
        dw_in = _matmul(S["h"], dz, mode="tn", name="mm_in_dw", out_dtype=BF16, tk_cap=2176)
        n_in = dm.N_IN // N_CHIPS
        G["w_in"][l] = jnp.stack([dw_in[:, j * n_in:(j + 1) * n_in] for j in range(N_CHIPS)])
        if ready is not None:
            file_shared(shared, l + 1)
        ready = {n: G[n][l] for n in _BIG}
        dX, dg_a, dsh_a, dsc_a = _normmod_bwd(S["X"], g_mix, sc_a, dh, dX1, C, "normmod_bwd")
        G["g_mix"][l] = dg_a[0]
        dmod[l] = jnp.stack([dsh_a, dsc_a, dgt_a, dsh_f, dsc_f, dgt_f], axis=1)

    grad_x = dX[C:].reshape(1, L, D)
    for n in _BIG:
        G.pop(n)
    parts = dict(zip(_BIG, _comm_alone("all_rs", [ready[n] for n in _BIG], "rs_all")))
    red = reduce_parts(parts)
    file_shared(dict(zip(_BIG, _comm_alone("sib_ag", [red[n] for n in _BIG], "rs_share"))), 0)
    G = {n: jnp.stack(v) for n, v in G.items()}
    G["g_final"] = dg_final[0]
    dmod = jnp.stack(dmod)

    small_shapes = [G[n].shape for n in _SMALL]
    gs_pack = _pack([G[n] for n in _SMALL], F32, 16 * N_DEV)
    r8 = gs_pack.shape[0] // N_DEV
    gs_parts = _comm(gs_pack.reshape(N_DEV, r8, PACK_W), group="all", by_peer=True, name="rs_small")
    gs_sum = _allgather8(_sum_slots(gs_parts, F32, "sum_small"), "ag_small_red").reshape(-1, PACK_W)
    for n, a in zip(_SMALL, _unpack(gs_sum, small_shapes)):
        G[n] = a
    allp = _allgather8(_pack([dmod], F32, 16), "ag_dmod")
    dmod_all = allp.reshape(N_DEV, -1)[:, :DEPTH * 2 * N_MOD * D]
    dmod_all = dmod_all.reshape(N_DEV, DEPTH, 2, N_MOD * D)
    dmod_ctx = _sum_slots(dmod_all[:, :, 0].reshape(N_DEV, DEPTH, N_MOD * D), F32, "sum_dmod_ctx")
    dmod16 = jnp.concatenate([jnp.swapaxes(dmod_all[:, :, 1], 0, 1), dmod_ctx[:, None],
                              jnp.zeros((DEPTH, 16 - N_DEV - 1, N_MOD * D), F32)], axis=1)
    G["b_ada"] = _sum_slots(jnp.swapaxes(dmod16, 0, 1)[:N_DEV + 1], F32, "sum_b_ada")
    dmod16_sh = lax.dynamic_slice_in_dim(dmod16, chip * n_sh, n_sh, axis=2)
    G["w_ada"], dact = _ada_bwd(cond, P["w_ada"], dmod16_sh, "ada_bwd")
    dact4 = _comm(dact, group="chips", by_peer=False, name="ag_dact")
    G["c_ctx"] = _cctx_grad(P["c_ctx"].reshape(1, D), dact4, "cctx_grad")[0]
    conv_cols = conv_sh[2]
    G["ffn_conv_w"] = lax.dynamic_slice_in_dim(G["ffn_conv_w"], chip * conv_cols, conv_cols, axis=2)

    for n in _BIG:
        G[n] = jnp.stack(Gred[n])

    delta, new_m, new_v = {}, {}, {}
    for n in _BIG + ("w_ada",):
        delta[n], new_m[n], new_v[n], G[n] = _adamw_nd(P[n], G[n], M[n], V[n], "adamw_" + n)
    rest = [n for n in _WEIGHTS if n not in _BIG + ("w_ada",)]
    shapes = [P[n].shape for n in rest]
    packed = [_pack([d[n] for n in rest], F32, 8) for d in (P, G, M, V)]
    outs = _adamw(*packed, "adamw_small")[:3]
    for dst, buf in zip((delta, new_m, new_v), outs):
        for n, a in zip(rest, _unpack(buf, shapes)):
            dst[n] = a
    return loss, grad_x, G, delta, new_m, new_v


def kernel(x, c, ctx, c_ctx, w_ada, b_ada, g_mix, g_ffn, w_in, w_out, attn_sink, ssm_lambda_re, ssm_lambda_im, ssm_log_dt, ssm_b_re, ssm_b_im, ssm_c_re, ssm_c_im, ssm_d, ssm_w_glu, ssm_b_glu, gmlp_ln_g, gmlp_ln_b, gmlp_w_s, gmlp_b_s, ffn_w_up, ffn_conv_w, ffn_conv_b, ffn_w_down, g_final, loss_target, m_c_ctx, m_w_ada, m_b_ada, m_g_mix, m_g_ffn, m_w_in, m_w_out, m_attn_sink, m_ssm_lambda_re, m_ssm_lambda_im, m_ssm_log_dt, m_ssm_b_re, m_ssm_b_im, m_ssm_c_re, m_ssm_c_im, m_ssm_d, m_ssm_w_glu, m_ssm_b_glu, m_gmlp_ln_g, m_gmlp_ln_b, m_gmlp_w_s, m_gmlp_b_s, m_ffn_w_up, m_ffn_conv_w, m_ffn_conv_b, m_ffn_w_down, m_g_final, v_c_ctx, v_w_ada, v_b_ada, v_g_mix, v_g_ffn, v_w_in, v_w_out, v_attn_sink, v_ssm_lambda_re, v_ssm_lambda_im, v_ssm_log_dt, v_ssm_b_re, v_ssm_b_im, v_ssm_c_re, v_ssm_c_im, v_ssm_d, v_ssm_w_glu, v_ssm_b_glu, v_gmlp_ln_g, v_gmlp_ln_b, v_gmlp_w_s, v_gmlp_b_s, v_ffn_w_up, v_ffn_conv_w, v_ffn_conv_b, v_ffn_w_down, v_g_final):
    env = locals()
    P = {n: env[n] for n in _WEIGHTS}
    M = {n: env["m_" + n] for n in _WEIGHTS}
    V = {n: env["v_" + n] for n in _WEIGHTS}
    loss, grad_x, G, delta, new_m, new_v = _step(P, M, V, x, c, ctx, loss_target)
    return (loss, grad_x, *[G[n] for n in _WEIGHTS], *[delta[n] for n in _WEIGHTS],
            *[new_m[n] for n in _WEIGHTS], *[new_v[n] for n in _WEIGHTS])
```

```python
import functools
import math

import jax
import jax.numpy as jnp
from jax import lax
from jax.experimental import pallas as pl
from jax.experimental.pallas import tpu as pltpu

F32 = jnp.float32
BF16 = jnp.bfloat16

D_MODEL = 2048
SEQ = 4096
DEPTH = 4
CTX_LEN = 256
GRID_W = 64
HEAD_DIM = 64
GQA_RATIO = 8
WINDOW = 128
ATTN_BLOCK = 128
ROPE_BASE = 10000.0
SSM_GROUP = 16
SSM_STATE = 64
GMLP_CHUNK = 128
GMLP_GROUP_W = 128
N_MOD = 6
NORM_EPS = 1e-6
ADAM_LR = 0.001
ADAM_B1 = 0.9
ADAM_B2 = 0.999
ADAM_EPS = 1e-08
ADAM_WD = 0.01
ADAM_STEP = 10

N_CHIPS = 4
N_DEV = 8
SCAN_SEGMENTS = 8
STATE_BLOCK = 256
STATE_BLOCK_FWD = 512
LANES = 128
PACK_W = 1024
COMM_CHUNKS = (16, 8, 4, 2)
SUM_BLOCK_BYTES = 8 << 20
VMEM_LIMIT_MB = 56
_GELU_K = math.sqrt(2.0 / math.pi)
_GELU_C = 0.044715


class _Dims:
    def __init__(self):
        self.D = D_MODEL
        self.L = SEQ
        self.C = CTX_LEN
        self.T = SEQ + CTX_LEN
        self.ATTN_W = D_MODEL // 2
        self.SSM_W = D_MODEL // 4
        self.GMLP_W = D_MODEL - self.ATTN_W - self.SSM_W
        self.NH = self.ATTN_W // HEAD_DIM
        self.NKV = self.NH // GQA_RATIO
        self.KV_W = self.NKV * HEAD_DIM
        self.SG = self.SSM_W // SSM_GROUP
        self.GP = self.SG * SSM_STATE
        self.GG = self.GMLP_W // GMLP_GROUP_W
        self.DFF = ((8 * D_MODEL // 3 + 255) // 256) * 256
        self.OFF_K = self.ATTN_W
        self.OFF_V = self.OFF_K + self.KV_W
        self.OFF_S = self.OFF_V + self.KV_W
        self.OFF_GU = self.OFF_S + self.SSM_W
        self.OFF_GV = self.OFF_GU + self.GMLP_W
        self.N_IN = self.OFF_GV + self.GMLP_W
        assert self.NKV == 2 and self.KV_W == LANES
        assert self.C % (SCAN_SEGMENTS * 8) == 0 and self.L % (SCAN_SEGMENTS * 8) == 0
        assert self.C % ATTN_BLOCK == 0 and self.L % ATTN_BLOCK == 0


def _tile(n, cap, mult):
    best = None
    for t in range(mult, min(n, cap) + 1, mult):
        if n % t == 0:
            best = t
    return best if best is not None else n


def _cparams(sem=None):
    kw = dict(vmem_limit_bytes=VMEM_LIMIT_MB << 20)
    if sem is not None:
        kw["dimension_semantics"] = sem
    return pltpu.CompilerParams(**kw)


def _gelu(x):
    return 0.5 * x * (1.0 + jnp.tanh(_GELU_K * (x + _GELU_C * x * x * x)))


def _gelu_grad(x):
    t = jnp.tanh(_GELU_K * (x + _GELU_C * x * x * x))
    return 0.5 * (1.0 + t) + 0.5 * x * (1.0 - t * t) * _GELU_K * (1.0 + 3.0 * _GELU_C * x * x)


def _sigmoid(x):
    return 1.0 / (1.0 + jnp.exp(-x))


def _dot(a, b, dims):
    return lax.dot_general(a.astype(BF16), b.astype(BF16), (dims, ((), ())), preferred_element_type=F32)


_NN = ((1,), (0,))
_NT = ((1,), (1,))
_TN = ((0,), (0,))


def _comm(x, *, group, by_peer, name):
    n = {"chips": N_CHIPS, "all": N_DEV}[group]
    blk = x.shape[1:] if by_peer else x.shape
    npeer = n - 1
    W = blk[-1]
    rows = math.prod(blk[:-1])
    x2 = x.reshape((n, rows, W) if by_peer else (rows, W))
    nchunk = next((k for k in COMM_CHUNKS if rows % (k * 16) == 0), 1)
    cr = rows // nchunk

    def body(x_ref, o_ref, send_sems, recv_sems, local_sem):
        mx, my, mc = lax.axis_index("x"), lax.axis_index("y"), lax.axis_index("c")
        if group == "chips":
            me = 2 * mx + my
            peers = [((mx ^ fx, my ^ fy, mc), 2 * (mx ^ fx) + (my ^ fy)) for fx, fy in ((1, 0), (0, 1), (1, 1))]
        else:
            me = 4 * mx + 2 * my + mc
            peers = [((mx ^ fx, my ^ fy, mc ^ fc), 4 * (mx ^ fx) + 2 * (my ^ fy) + (mc ^ fc))
                     for fx in (0, 1) for fy in (0, 1) for fc in (0, 1) if fx + fy + fc]

        def src_of(pid):
            return x_ref.at[pid] if by_peer else x_ref

        def chunk(ref, ch):
            return ref.at[pl.ds(ch * cr, cr), :]

        def remote(k, dev, pid, ch=None):
            s, d = src_of(pid), o_ref.at[me]
            if ch is not None:
                s, d = chunk(s, ch), chunk(d, ch)
            return pltpu.make_async_remote_copy(src_ref=s, dst_ref=d, send_sem=send_sems.at[k],
                                                recv_sem=recv_sems.at[k], device_id=dev,
                                                device_id_type=pl.DeviceIdType.MESH)

        for ch in range(nchunk):
            pltpu.make_async_copy(chunk(src_of(me), ch), chunk(o_ref.at[me], ch), local_sem).start()
            for k, (dev, pid) in enumerate(peers):
                remote(k, dev, pid, ch).start()
        for k, (dev, pid) in enumerate(peers):
            remote(k, dev, pid).wait_recv()
        for k, (dev, pid) in enumerate(peers):
            remote(k, dev, pid).wait_send()
        pltpu.make_async_copy(src_of(me), o_ref.at[me], local_sem).wait()

    out = pl.pallas_call(
        body, name=name,
        out_shape=jax.ShapeDtypeStruct((n, rows, W), x.dtype),
        in_specs=[pl.BlockSpec(memory_space=pl.ANY)],
        out_specs=pl.BlockSpec(memory_space=pl.ANY),
        scratch_shapes=[pltpu.SemaphoreType.DMA((npeer,)), pltpu.SemaphoreType.DMA((npeer,)),
                        pltpu.SemaphoreType.DMA],
    )(x2)
    return out.reshape((n,) + tuple(blk))


def _carried_comm(kind, xs):
    nt = len(xs)
    masks = {"chips_ag": [(1, 0, 0), (0, 1, 0), (1, 1, 0)], "sib_ag": [(0, 0, 1)],
             "all_rs": [(fx, fy, fc) for fx in (0, 1) for fy in (0, 1) for fc in (0, 1) if fx + fy + fc]}[kind]
    npeer = len(masks)

    def copies(x_refs, o_refs, sems):
        send_sems, recv_sems, local_sems = sems
        mx, my, mc = lax.axis_index("x"), lax.axis_index("y"), lax.axis_index("c")
        me = {"chips_ag": 2 * mx + my, "sib_ag": mc, "all_rs": 4 * mx + 2 * my + mc}[kind]

        def src(t, px, py, pc):
            if kind != "all_rs":
                return x_refs[t]
            half = xs[t].shape[1] // 2
            return x_refs[t].at[2 * px + py, pl.ds(pl.multiple_of(pc * half, 16), half), :]

        local = [pltpu.make_async_copy(src(t, mx, my, mc), o_refs[t].at[me], local_sems.at[t]) for t in range(nt)]
        remote = [pltpu.make_async_remote_copy(
            src_ref=src(t, mx ^ fx, my ^ fy, mc ^ fc), dst_ref=o_refs[t].at[me],
            send_sem=send_sems.at[t * npeer + k], recv_sem=recv_sems.at[t * npeer + k],
            device_id=(mx ^ fx, my ^ fy, mc ^ fc), device_id_type=pl.DeviceIdType.MESH)
            for t in range(nt) for k, (fx, fy, fc) in enumerate(masks)]
        return local, remote

    def start(x_refs, o_refs, sems):
        local, remote = copies(x_refs, o_refs, sems)
        for cp in local + remote:
            cp.start()

    def wait(x_refs, o_refs, sems):
        local, remote = copies(x_refs, o_refs, sems)
        for cp in remote:
            cp.wait_recv()
        for cp in remote:
            cp.wait_send()
        for cp in local:
            cp.wait()

    def out_shape(x):
        if kind == "all_rs":
            return (N_DEV, x.shape[1] // 2, x.shape[2])
        return ((N_CHIPS if kind == "chips_ag" else 2),) + tuple(x.shape)

    out_shapes = [jax.ShapeDtypeStruct(out_shape(x), x.dtype) for x in xs]
    scratch = [pltpu.SemaphoreType.DMA((max(nt * npeer, 1),)), pltpu.SemaphoreType.DMA((max(nt * npeer, 1),)),
               pltpu.SemaphoreType.DMA((max(nt, 1),))]
    return out_shapes, scratch, start, wait


def _ride(body, n_in, n_out, n_scratch, carry, nsteps):
    if not carry:
        return body, [], [], []
    kind, xs = carry
    nc = len(xs)
    c_out, c_scratch, c_start, c_wait = _carried_comm(kind, xs)

    def wrapped(*refs):
        ins, cx = refs[:n_in], refs[n_in:n_in + nc]
        outs, co = refs[n_in + nc:n_in + nc + n_out], refs[n_in + nc + n_out:n_in + 2 * nc + n_out]
        base = n_in + 2 * nc + n_out
        scr, sems = refs[base:base + n_scratch], refs[base + n_scratch:]
        i = pl.program_id(0)

        @pl.when(i == 0)
        def _():
            c_start(cx, co, sems)

        body(*ins, *outs, *scr)

        @pl.when(i == nsteps - 1)
        def _():
            c_wait(cx, co, sems)

    return wrapped, list(xs), c_out, c_scratch


def _comm_alone(kind, xs, name):
    nt = len(xs)
    out_shapes, scratch, start, wait = _carried_comm(kind, xs)

    def body(*refs):
        x_refs, o_refs, sems = refs[:nt], refs[nt:2 * nt], refs[2 * nt:]
        start(x_refs, o_refs, sems)
        wait(x_refs, o_refs, sems)

    any_spec = pl.BlockSpec(memory_space=pl.ANY)
    return pl.pallas_call(body, name=name, out_shape=out_shapes, in_specs=[any_spec] * nt,
                          out_specs=[any_spec] * nt, scratch_shapes=scratch)(*xs)


def _allgather8(x, name):
    return _comm(x, group="all", by_peer=False, name=name)


def _sum_slots(x, out_dtype, name):
    n, R, W = x.shape
    tr = _tile(R, max(16, min(512, SUM_BLOCK_BYTES // (n * W * x.dtype.itemsize) // 16 * 16)), 16)

    def body(x_ref, o_ref):
        acc = x_ref[0].astype(F32)
        for s in range(1, n):
            acc = acc + x_ref[s].astype(F32)
        o_ref[...] = acc.astype(o_ref.dtype)

    return pl.pallas_call(
        body, name=name, grid=(R // tr,),
        in_specs=[pl.BlockSpec((n, tr, W), lambda i: (0, i, 0))],
        out_specs=pl.BlockSpec((tr, W), lambda i: (i, 0)),
        out_shape=jax.ShapeDtypeStruct((R, W), out_dtype),
        compiler_params=_cparams(("parallel",)),
    )(x)


def _pack(arrs, dtype, row_mult):
    flat = jnp.concatenate([a.reshape(-1).astype(dtype) for a in arrs])
    n = flat.shape[0]
    quant = row_mult * PACK_W
    total = -(-n // quant) * quant
    if total != n:
        flat = jnp.concatenate([flat, jnp.zeros((total - n,), dtype)])
    return flat.reshape(total // PACK_W, PACK_W)


def _unpack(buf, shapes):
    flat = buf.reshape(-1)
    out, off = [], 0
    for s in shapes:
        n = math.prod(s)
        out.append(flat[off:off + n].reshape(s))
        off += n
    return out


class _Mat:
    def __init__(self, arr, axis=None, layer=None, part0=0, nparts=None):
        self.arr, self.axis, self.layer, self.part0 = arr, axis, layer, part0
        self.r, self.c = arr.shape[-2:]
        self.nparts = 1 if axis is None else (arr.shape[0] if nparts is None else nparts)

    @property
    def shape(self):
        return (self.r * (self.nparts if self.axis == 0 else 1), self.c * (self.nparts if self.axis == 1 else 1))

    def unit(self, ax):
        return self.r if ax == 0 else self.c

    def spec(self, tr, tc, fn):
        def imap(*g):
            rb, cb = fn(*g)
            lead = []
            if self.axis == 0:
                per = self.r // tr
                lead.append(self.part0 + rb // per)
                rb = rb % per
            elif self.axis == 1:
                per = self.c // tc
                lead.append(self.part0 + cb // per)
                cb = cb % per
            if self.layer is not None:
                lead.append(self.layer)
            return (*lead, rb, cb)

        return pl.BlockSpec((None,) * (self.arr.ndim - 2) + (tr, tc), imap)


def _as_mat(x):
    return x if isinstance(x, _Mat) else _Mat(x)


def _matmul(a, b, *, mode, name, out_dtype=F32, add=None, out_split=None, carry=None, tm_cap=1088, tn_cap=1536,
            tk_cap=2048):
    a, b = _as_mat(a), _as_mat(b)
    am, ak = (0, 1) if mode != "tn" else (1, 0)
    bk, bn = (0, 1) if mode != "nt" else (1, 0)
    M, K, N = a.shape[am], a.shape[ak], b.shape[bn]
    assert b.shape[bk] == K
    if out_split is None:
        o = _Mat(jax.ShapeDtypeStruct((M, N), out_dtype))
    else:
        ax, parts = out_split
        o = _Mat(jax.ShapeDtypeStruct((parts, M // parts if ax == 0 else M, N // parts if ax == 1 else N), out_dtype),
                 axis=ax)
    tm = _tile(math.gcd(a.unit(am), o.unit(0)), tm_cap, 16 if mode != "tn" else LANES)
    tn = _tile(math.gcd(b.unit(bn), o.unit(1)), tn_cap, LANES)
    tk = _tile(math.gcd(a.unit(ak), b.unit(bk)), tk_cap, LANES if mode != "tn" else 16)
    nk = K // tk
    dims = {"nn": _NN, "nt": _NT, "tn": _TN}[mode]

    kind, carry = carry if carry else ("chips_ag", [])
    carry = list(carry)
    nc = len(carry)
    c_out, c_scratch, c_start, c_wait = _carried_comm(kind, carry)
    grid = (M // tm, N // tn, nk)

    def body(*refs):
        a_ref, b_ref = refs[:2]
        add_ref = refs[2] if add is not None else None
        n_in = 2 + (add is not None)
        cx_refs, o_ref, co_refs = refs[n_in:n_in + nc], refs[n_in + nc], refs[n_in + nc + 1:n_in + 2 * nc + 1]
        acc_ref, sems = refs[n_in + 2 * nc + 1], refs[n_in + 2 * nc + 2:]
        i, j, k = pl.program_id(0), pl.program_id(1), pl.program_id(2)
        if nc:
            @pl.when((i == 0) & (j == 0) & (k == 0))
            def _():
                c_start(cx_refs, co_refs, sems)

        @pl.when(k == 0)
        def _():
            acc_ref[...] = jnp.zeros_like(acc_ref)

        acc_ref[...] += _dot(a_ref[...], b_ref[...], dims)

        @pl.when(k == nk - 1)
        def _():
            r = acc_ref[...]
            if add_ref is not None:
                r = r + add_ref[...].astype(F32)
            o_ref[...] = r.astype(o_ref.dtype)

        if nc:
            @pl.when((i == grid[0] - 1) & (j == grid[1] - 1) & (k == nk - 1))
            def _():
                c_wait(cx_refs, co_refs, sems)

    if mode == "tn":
        a_spec = a.spec(tk, tm, lambda i, j, k: (k, i))
    else:
        a_spec = a.spec(tm, tk, lambda i, j, k: (i, k))
    if mode == "nt":
        b_spec = b.spec(tn, tk, lambda i, j, k: (j, k))
    else:
        b_spec = b.spec(tk, tn, lambda i, j, k: (k, j))
    o_spec = o.spec(tm, tn, lambda i, j, k: (i, j))
    any_spec = pl.BlockSpec(memory_space=pl.ANY)
    in_specs = [a_spec, b_spec] + ([pl.BlockSpec((tm, tn), lambda i, j, k: (i, j))] if add is not None else [])
    args = (a.arr, b.arr) + ((add,) if add is not None else ())
    outs = pl.pallas_call(
        body, name=name, grid=grid,
        in_specs=in_specs + [any_spec] * nc, out_specs=[o_spec] + [any_spec] * nc,
        out_shape=[o.arr] + c_out,
        scratch_shapes=[pltpu.VMEM((tm, tn), F32)] + (c_scratch if nc else []),
        compiler_params=_cparams(("arbitrary",) * 3 if nc else ("parallel", "parallel", "arbitrary")),
    )(*args, *carry)
    return (outs[0], outs[1:]) if nc else outs[0]


def _matmul_residual(a, w, res, gate, C, name, carry=None):
    w = _as_mat(w)
    M, K = a.shape
    N = w.shape[1]
    tm = _tile(M, 1088, 16)
    tn = _tile(w.unit(1), 1024, LANES)
    tk = _tile(w.unit(0), 2048, LANES)
    nk = K // tk
    kind, carry = carry if carry else ("chips_ag", [])
    carry = list(carry)
    nc = len(carry)
    c_out, c_scratch, c_start, c_wait = _carried_comm(kind, carry)
    grid = (M // tm, N // tn, nk)

    def body(*refs):
        a_ref, w_ref, res_ref, gate_ref = refs[:4]
        cx_refs, (mix_ref, o_ref), co_refs = refs[4:4 + nc], refs[4 + nc:6 + nc], refs[6 + nc:6 + 2 * nc]
        acc_ref, sems = refs[6 + 2 * nc], refs[7 + 2 * nc:]
        i, j, k = pl.program_id(0), pl.program_id(1), pl.program_id(2)
        if nc:
            @pl.when((i == 0) & (j == 0) & (k == 0))
            def _():
                c_start(cx_refs, co_refs, sems)

        @pl.when(k == 0)
        def _():
            acc_ref[...] = jnp.zeros_like(acc_ref)

        acc_ref[...] += _dot(a_ref[...], w_ref[...], _NN)

        @pl.when(k == nk - 1)
        def _():
            r = acc_ref[...]
            row = i * tm + lax.broadcasted_iota(jnp.int32, (tm, 1), 0)
            g = jnp.where(row < C, gate_ref[0:1, :], gate_ref[1:2, :])
            mix_ref[...] = r
            o_ref[...] = res_ref[...] + g * r

        if nc:
            @pl.when((i == grid[0] - 1) & (j == grid[1] - 1) & (k == nk - 1))
            def _():
                c_wait(cx_refs, co_refs, sems)

    o_spec = pl.BlockSpec((tm, tn), lambda i, j, k: (i, j))
    any_spec = pl.BlockSpec(memory_space=pl.ANY)
    outs = pl.pallas_call(
        body, name=name, grid=grid,
        in_specs=[pl.BlockSpec((tm, tk), lambda i, j, k: (i, k)),
                  w.spec(tk, tn, lambda i, j, k: (k, j)),
                  o_spec,
                  pl.BlockSpec((2, tn), lambda i, j, k: (0, j))] + [any_spec] * nc,
        out_specs=[o_spec, o_spec] + [any_spec] * nc,
        out_shape=[jax.ShapeDtypeStruct((M, N), F32), jax.ShapeDtypeStruct((M, N), F32)] + c_out,
        scratch_shapes=[pltpu.VMEM((tm, tn), F32)] + (c_scratch if nc else []),
        compiler_params=_cparams(("arbitrary",) * 3 if nc else ("parallel", "parallel", "arbitrary")),
    )(a, w.arr, res, gate, *carry)
    return (outs[0], outs[1], outs[2:]) if nc else (outs[0], outs[1])


def _seg_select(i, tm, C, ref):
    row = i * tm + lax.broadcasted_iota(jnp.int32, (tm, 1), 0)
    return row < C, jnp.where(row < C, ref[0:1, :], ref[1:2, :])


def _normmod(X, g, shift, scale, C, name):
    T, D = X.shape
    tm = _tile(T, 512, 16)

    def body(x_ref, g_ref, sh_ref, sc_ref, h_ref):
        i = pl.program_id(0)
        x = x_ref[...]
        r = lax.rsqrt(jnp.mean(x * x, axis=-1, keepdims=True) + NORM_EPS)
        _, sh = _seg_select(i, tm, C, sh_ref)
        _, sc = _seg_select(i, tm, C, sc_ref)
        h_ref[...] = ((x * r) * g_ref[...] * (1.0 + sc) + sh).astype(BF16)

    row = pl.BlockSpec((tm, D), lambda i: (i, 0))
    vec = pl.BlockSpec((1, D), lambda i: (0, 0))
    two = pl.BlockSpec((2, D), lambda i: (0, 0))
    return pl.pallas_call(
        body, name=name, grid=(T // tm,),
        in_specs=[row, vec, two, two], out_specs=row,
        out_shape=jax.ShapeDtypeStruct((T, D), BF16),
        compiler_params=_cparams(("parallel",)),
    )(X, g, shift, scale)


def _normmod_bwd(X, g, scale, dh, dres, C, name):
    T, D = X.shape
    tm = _tile(T, 512, 16)

    def body(x_ref, g_ref, sc_ref, dh_ref, dres_ref, dx_ref, dg_ref, dsh_ref, dsc_ref):
        i = pl.program_id(0)

        @pl.when(i == 0)
        def _():
            dg_ref[...] = jnp.zeros_like(dg_ref)
            dsh_ref[...] = jnp.zeros_like(dsh_ref)
            dsc_ref[...] = jnp.zeros_like(dsc_ref)

        x = x_ref[...]
        dh = dh_ref[...]
        gv = g_ref[...]
        r = lax.rsqrt(jnp.mean(x * x, axis=-1, keepdims=True) + NORM_EPS)
        xh = x * r
        isc, sc = _seg_select(i, tm, C, sc_ref)
        n = xh * gv
        dhn = dh * n
        zero = jnp.zeros_like(dh)
        dsh_ref[0:1, :] += jnp.sum(jnp.where(isc, dh, zero), axis=0, keepdims=True)
        dsh_ref[1:2, :] += jnp.sum(jnp.where(isc, zero, dh), axis=0, keepdims=True)
        dsc_ref[0:1, :] += jnp.sum(jnp.where(isc, dhn, zero), axis=0, keepdims=True)
        dsc_ref[1:2, :] += jnp.sum(jnp.where(isc, zero, dhn), axis=0, keepdims=True)
        dn = dh * (1.0 + sc)
        dg_ref[...] += jnp.sum(dn * xh, axis=0, keepdims=True)
        dxh = dn * gv
        dx_ref[...] = dres_ref[...] + r * (dxh - xh * jnp.mean(dxh * xh, axis=-1, keepdims=True))

    row = pl.BlockSpec((tm, D), lambda i: (i, 0))
    vec = pl.BlockSpec((1, D), lambda i: (0, 0))
    two = pl.BlockSpec((2, D), lambda i: (0, 0))
    return pl.pallas_call(
        body, name=name, grid=(T // tm,),
        in_specs=[row, vec, two, row, row], out_specs=[row, vec, two, two],
        out_shape=[jax.ShapeDtypeStruct((T, D), F32), jax.ShapeDtypeStruct((1, D), F32),
                   jax.ShapeDtypeStruct((2, D), F32), jax.ShapeDtypeStruct((2, D), F32)],
        compiler_params=_cparams(("arbitrary",)),
    )(X, g, scale, dh, dres)


def _gate_bwd(dxo, mix, gate, C, name):
    T, D = dxo.shape
    tm = _tile(T, 512, 16)

    def body(dx_ref, mix_ref, gate_ref, dmix_ref, dgate_ref):
        i = pl.program_id(0)

        @pl.when(i == 0)
        def _():
            dgate_ref[...] = jnp.zeros_like(dgate_ref)

        dx = dx_ref[...]
        isc, gt = _seg_select(i, tm, C, gate_ref)
        dmix_ref[...] = (dx * gt).astype(BF16)
        p = dx * mix_ref[...]
        zero = jnp.zeros_like(p)
        dgate_ref[0:1, :] += jnp.sum(jnp.where(isc, p, zero), axis=0, keepdims=True)
        dgate_ref[1:2, :] += jnp.sum(jnp.where(isc, zero, p), axis=0, keepdims=True)

    row = pl.BlockSpec((tm, D), lambda i: (i, 0))
    two = pl.BlockSpec((2, D), lambda i: (0, 0))
    return pl.pallas_call(
        body, name=name, grid=(T // tm,),
        in_specs=[row, row, two], out_specs=[row, two],
        out_shape=[jax.ShapeDtypeStruct((T, D), BF16), jax.ShapeDtypeStruct((2, D), F32)],
        compiler_params=_cparams(("arbitrary",)),
    )(dxo, mix, gate)


def _conv_masks(T, C):
    row = lax.broadcasted_iota(jnp.int32, (T, 1), 0)
    first = (row == 0) | (row == C)
    last = (row == C - 1) | (row == T - 1)
    return first, last


def _shift_rows(x, first, last):
    T = x.shape[0]
    prev = jnp.where(first, 0.0, pltpu.roll(x, 1, 0))
    nxt = jnp.where(last, 0.0, pltpu.roll(x, T - 1, 0))
    return prev, nxt


def _convact(up, cw, cb, C, name):
    _, T, F = up.shape
    tc = LANES

    def body(u_ref, w_ref, b_ref, o_ref):
        gp = u_ref[0].astype(F32)
        first, last = _conv_masks(T, C)
        prev, nxt = _shift_rows(gp, first, last)
        gc = prev * w_ref[0:1, :] + gp * w_ref[1:2, :] + nxt * w_ref[2:3, :] + b_ref[...]
        o_ref[...] = (gc * _sigmoid(gc) * u_ref[1].astype(F32)).astype(BF16)

    col = pl.BlockSpec((T, tc), lambda j: (0, j))
    return pl.pallas_call(
        body, name=name, grid=(F // tc,),
        in_specs=[pl.BlockSpec((2, T, tc), lambda j: (0, 0, j)), pl.BlockSpec((3, tc), lambda j: (0, j)),
                  pl.BlockSpec((1, tc), lambda j: (0, j))],
        out_specs=col, out_shape=jax.ShapeDtypeStruct((T, F), BF16),
        compiler_params=_cparams(("parallel",)),
    )(up, cw, cb)


def _convact_bwd(up, cw, cb, dact, C, name):
    _, T, F = up.shape
    tc = LANES

    def body(u_ref, w_ref, b_ref, da_ref, du_ref, dw_ref, db_ref):
        gp = u_ref[0].astype(F32)
        val = u_ref[1].astype(F32)
        da = da_ref[...].astype(F32)
        first, last = _conv_masks(T, C)
        prev, nxt = _shift_rows(gp, first, last)
        w0, w1, w2 = w_ref[0:1, :], w_ref[1:2, :], w_ref[2:3, :]
        gc = prev * w0 + gp * w1 + nxt * w2 + b_ref[...]
        s = _sigmoid(gc)
        du_ref[1] = (da * gc * s).astype(BF16)
        dgc = da * val * (s + gc * s * (1.0 - s))
        db_ref[...] = jnp.sum(dgc, axis=0, keepdims=True)
        dw_ref[0:1, :] = jnp.sum(dgc * prev, axis=0, keepdims=True)
        dw_ref[1:2, :] = jnp.sum(dgc * gp, axis=0, keepdims=True)
        dw_ref[2:3, :] = jnp.sum(dgc * nxt, axis=0, keepdims=True)
        dprev, dnxt = _shift_rows(dgc, first, last)
        du_ref[0] = (dnxt * w0 + dgc * w1 + dprev * w2).astype(BF16)

    col = pl.BlockSpec((T, tc), lambda j: (0, j))
    two = pl.BlockSpec((2, T, tc), lambda j: (0, 0, j))
    w3 = pl.BlockSpec((3, tc), lambda j: (0, j))
    w1 = pl.BlockSpec((1, tc), lambda j: (0, j))
    return pl.pallas_call(
        body, name=name, grid=(F // tc,),
        in_specs=[two, w3, w1, col], out_specs=[two, w3, w1],
        out_shape=[jax.ShapeDtypeStruct((2, T, F), BF16),
                   jax.ShapeDtypeStruct((3, F), F32), jax.ShapeDtypeStruct((1, F), F32)],
        compiler_params=_cparams(("parallel",)),
    )(up, cw, cb, dact)


def _rot_half(x):
    W = x.shape[-1]
    lane = lax.broadcasted_iota(jnp.int32, x.shape, x.ndim - 1)
    lo = (lane % 32) < 16
    return jnp.where(lo, -pltpu.roll(x, W - 16, x.ndim - 1), pltpu.roll(x, 16, x.ndim - 1))


def _swap_halves(x):
    return pltpu.roll(x, 64, x.ndim - 1)


def _rope_tables(dm):
    t = jnp.arange(dm.L)
    n_freq = HEAD_DIM // 4
    inv_freq = ROPE_BASE ** (-jnp.arange(n_freq, dtype=F32) / n_freq)
    ang_r = (t // GRID_W).astype(F32)[:, None] * inv_freq
    ang_c = (t % GRID_W).astype(F32)[:, None] * inv_freq
    ang = jnp.concatenate([ang_r, ang_r, ang_c, ang_c], axis=-1)
    cos = jnp.concatenate([jnp.ones((dm.C, HEAD_DIM), F32), jnp.cos(ang)], axis=0)
    sin = jnp.concatenate([jnp.zeros((dm.C, HEAD_DIM), F32), jnp.sin(ang)], axis=0)
    return jnp.tile(cos, (1, 2)), jnp.tile(sin, (1, 2))


def _attn_prep(z, cos, sin, dm, name):
    T = dm.T
    AW = dm.ATTN_W
    tm = _tile(T, 256, 16)
    rep = AW // LANES

    def body(z_ref, cos_ref, sin_ref, q_ref, k_ref, v_ref):
        cs, sn = cos_ref[...], sin_ref[...]
        q = z_ref[:, 0:AW]
        csq, snq = jnp.tile(cs, (1, rep)), jnp.tile(sn, (1, rep))
        q_ref[...] = (q * csq + _rot_half(q) * snq).astype(BF16)
        k = z_ref[:, dm.OFF_K:dm.OFF_V]
        k = k * cs + _rot_half(k) * sn
        v = z_ref[:, dm.OFF_V:dm.OFF_S]
        lo = lax.broadcasted_iota(jnp.int32, k.shape, 1) < HEAD_DIM
        ks, vs = _swap_halves(k), _swap_halves(v)
        k_ref[0] = jnp.where(lo, k, ks).astype(BF16)
        k_ref[1] = jnp.where(lo, ks, k).astype(BF16)
        v_ref[0] = jnp.where(lo, v, vs).astype(BF16)
        v_ref[1] = jnp.where(lo, vs, v).astype(BF16)

    tab = pl.BlockSpec((tm, LANES), lambda i: (i, 0))
    kv = pl.BlockSpec((2, tm, LANES), lambda i: (0, i, 0))
    return pl.pallas_call(
        body, name=name, grid=(T // tm,),
        in_specs=[pl.BlockSpec((tm, dm.OFF_S), lambda i: (i, 0)), tab, tab],
        out_specs=[pl.BlockSpec((tm, AW), lambda i: (i, 0)), kv, kv],
        out_shape=[jax.ShapeDtypeStruct((T, AW), BF16), jax.ShapeDtypeStruct((2, T, LANES), BF16),
                   jax.ShapeDtypeStruct((2, T, LANES), BF16)],
        compiler_params=_cparams(("parallel",)),
    )(z, cos, sin)


def _attn_prep_bwd(dq, dk2, dv2, cos, sin, dm, name):
    T = dm.T
    AW = dm.ATTN_W
    tm = _tile(T, 256, 16)
    rep = AW // LANES

    def body(dq_ref, dk_ref, dv_ref, cos_ref, sin_ref, o_ref):
        cs, sn = cos_ref[...], sin_ref[...]
        csq, snq = jnp.tile(cs, (1, rep)), jnp.tile(sn, (1, rep))
        dq = dq_ref[...]
        o_ref[:, 0:AW] = dq * csq - _rot_half(dq * snq)
        lo = lax.broadcasted_iota(jnp.int32, (tm, LANES), 1) < HEAD_DIM
        dk = jnp.where(lo, dk_ref[0], dk_ref[1])
        o_ref[:, dm.OFF_K:dm.OFF_V] = dk * cs - _rot_half(dk * sn)
        o_ref[:, dm.OFF_V:dm.OFF_S] = jnp.where(lo, dv_ref[0], dv_ref[1])

    tab = pl.BlockSpec((tm, LANES), lambda i: (i, 0))
    kv = pl.BlockSpec((2, tm, LANES), lambda i: (0, i, 0))
    return pl.pallas_call(
        body, name=name, grid=(T // tm,),
        in_specs=[pl.BlockSpec((tm, AW), lambda i: (i, 0)), kv, kv, tab, tab],
        out_specs=pl.BlockSpec((tm, dm.OFF_S), lambda i: (i, 0)),
        out_shape=jax.ShapeDtypeStruct((T, dm.OFF_S), F32),
        compiler_params=_cparams(("parallel",)),
    )(dq, dk2, dv2, cos, sin)


def _attn_specs(dm):
    B = ATTN_BLOCK
    nblk = dm.T // B
    q_spec = pl.BlockSpec((B, dm.ATTN_W), lambda i: (i, 0))
    prev = pl.BlockSpec((2, B, LANES), lambda i: (0, jnp.maximum(i - 1, 0), 0))
    own = pl.BlockSpec((2, B, LANES), lambda i: (0, i, 0))
    nxt = pl.BlockSpec((2, B, LANES), lambda i: (0, jnp.minimum(i + 1, nblk - 1), 0))
    ctx = pl.BlockSpec((2, dm.C, LANES), lambda i: (0, 0, 0))
    sink = pl.BlockSpec((2, GQA_RATIO * B, 1), lambda i: (0, 0, 0))
    return nblk, q_spec, prev, own, nxt, ctx, sink


def _attn_scores(i, q_ref, kp_ref, ko_ref, kn_ref, kc_ref, sink_ref, h, dm):
    B = ATTN_BLOCK
    nctx = dm.C // B
    nb = dm.L // B
    npair = GQA_RATIO // 2
    lo = lax.broadcasted_iota(jnp.int32, (B, LANES), 1) < HEAD_DIM
    zero = jnp.zeros((B, LANES), BF16)
    parts = []
    for p in range(npair):
        q2 = q_ref[:, (h * npair + p) * LANES:(h * npair + p + 1) * LANES]
        parts.append(jnp.where(lo, q2, zero))
        parts.append(jnp.where(lo, zero, q2))
    q8 = jnp.concatenate(parts, axis=0)
    kcat = jnp.concatenate([kp_ref[h], ko_ref[h], kn_ref[h], kc_ref[h]], axis=0)
    s = _dot(q8, kcat, _NT) * (HEAD_DIM ** -0.5)
    R, NK = s.shape
    iq = lax.broadcasted_iota(jnp.int32, (R, NK), 0) % B
    col = lax.broadcasted_iota(jnp.int32, (R, NK), 1)
    jk = col % B
    qb = i - nctx
    lat = qb >= 0
    m_prev = (col < B) & lat & (qb >= 1) & (jk >= iq)
    m_own = (col >= B) & (col < 2 * B) & lat
    m_next = (col >= 2 * B) & (col < 3 * B) & lat & (qb <= nb - 2) & (jk <= iq)
    mask = m_prev | m_own | m_next | (col >= 3 * B)
    s = jnp.where(mask, s, -jnp.inf)
    sk = sink_ref[h]
    m = jnp.maximum(jnp.max(s, axis=-1, keepdims=True), sk)
    e = jnp.exp(s - m)
    es = jnp.exp(sk - m)
    inv = 1.0 / (jnp.sum(e, axis=-1, keepdims=True) + es)
    return q8, kcat, e * inv, es * inv, lo


def _unstack_heads(x8, lo):
    B = ATTN_BLOCK
    out = []
    for p in range(GQA_RATIO // 2):
        a = x8[(2 * p) * B:(2 * p + 1) * B]
        b = x8[(2 * p + 1) * B:(2 * p + 2) * B]
        out.append(jnp.where(lo, a, b))
    return out


def _attention(q, k2, v2, sinkcol, dm, name, carry=None):
    B = ATTN_BLOCK
    nblk, q_spec, prev, own, nxt, ctx, sink = _attn_specs(dm)
    npair = GQA_RATIO // 2

    def body(q_ref, kp_ref, ko_ref, kn_ref, kc_ref, vp_ref, vo_ref, vn_ref, vc_ref, sink_ref, o_ref):
        i = pl.program_id(0)
        for h in range(2):
            _, _, p, _, lo = _attn_scores(i, q_ref, kp_ref, ko_ref, kn_ref, kc_ref, sink_ref, h, dm)
            vcat = jnp.concatenate([vp_ref[h], vo_ref[h], vn_ref[h], vc_ref[h]], axis=0)
            o8 = _dot(p, vcat, _NN)
            for pi, o2 in enumerate(_unstack_heads(o8, lo)):
                c0 = (h * npair + pi) * LANES
                o_ref[:, c0:c0 + LANES] = o2.astype(BF16)

    body, cxs, c_out, c_scratch = _ride(body, 10, 1, 0, carry, nblk)
    any_spec = pl.BlockSpec(memory_space=pl.ANY)
    outs = pl.pallas_call(
        body, name=name, grid=(nblk,),
        in_specs=[q_spec, prev, own, nxt, ctx, prev, own, nxt, ctx, sink] + [any_spec] * len(cxs),
        out_specs=[q_spec] + [any_spec] * len(cxs),
        out_shape=[jax.ShapeDtypeStruct((dm.T, dm.ATTN_W), BF16)] + c_out,
        scratch_shapes=c_scratch,
        compiler_params=_cparams(("arbitrary",) if carry else ("parallel",)),
    )(q, k2, k2, k2, k2, v2, v2, v2, v2, sinkcol, *cxs)
    return (outs[0], outs[1:]) if carry else outs[0]


def _attention_bwd(q, k2, v2, sinkcol, do, dm, name, carry=None):
    B = ATTN_BLOCK
    nblk, q_spec, prev, own, nxt, ctx, sink = _attn_specs(dm)
    npair = GQA_RATIO // 2

    def body(q_ref, kp_ref, ko_ref, kn_ref, kc_ref, vp_ref, vo_ref, vn_ref, vc_ref, sink_ref, do_ref,
             dq_ref, pk_ref, pv_ref, dkc_ref, dvc_ref, dsk_ref):
        i = pl.program_id(0)

        @pl.when(i == 0)
        def _():
            dkc_ref[...] = jnp.zeros_like(dkc_ref)
            dvc_ref[...] = jnp.zeros_like(dvc_ref)
            dsk_ref[...] = jnp.zeros_like(dsk_ref)

        for h in range(2):
            q8, kcat, p, ps, lo = _attn_scores(i, q_ref, kp_ref, ko_ref, kn_ref, kc_ref, sink_ref, h, dm)
            vcat = jnp.concatenate([vp_ref[h], vo_ref[h], vn_ref[h], vc_ref[h]], axis=0)
            zero = jnp.zeros((B, LANES), F32)
            parts = []
            for pi in range(npair):
                d2 = do_ref[:, (h * npair + pi) * LANES:(h * npair + pi + 1) * LANES]
                parts.append(jnp.where(lo, d2, zero))
                parts.append(jnp.where(lo, zero, d2))
            do8 = jnp.concatenate(parts, axis=0)
            dp = _dot(do8, vcat, _NT)
            delta = jnp.sum(p * dp, axis=-1, keepdims=True)
            ds = p * (dp - delta) * (HEAD_DIM ** -0.5)
            dsr = -ps * delta
            rows = [jnp.broadcast_to(jnp.sum(dsr[r * B:(r + 1) * B], axis=0, keepdims=True), (1, LANES))
                    for r in range(GQA_RATIO)]
            dsk_ref[h] += jnp.concatenate(rows, axis=0)
            dq8 = _dot(ds, kcat, _NN)
            for pi, d2 in enumerate(_unstack_heads(dq8, lo)):
                c0 = (h * npair + pi) * LANES
                dq_ref[:, c0:c0 + LANES] = d2
            dk = _dot(ds, q8, _TN)
            dv = _dot(p, do8, _TN)
            dk = dk + _swap_halves(dk)
            dv = dv + _swap_halves(dv)
            for w in range(3):
                pk_ref[0, h, w] = dk[w * B:(w + 1) * B]
                pv_ref[0, h, w] = dv[w * B:(w + 1) * B]
            dkc_ref[h] += dk[3 * B:]
            dvc_ref[h] += dv[3 * B:]

    part = pl.BlockSpec((1, 2, 3, B, LANES), lambda i: (i, 0, 0, 0, 0))
    acc_c = pl.BlockSpec((2, dm.C, LANES), lambda i: (0, 0, 0))
    acc_s = pl.BlockSpec((2, GQA_RATIO, LANES), lambda i: (0, 0, 0))
    body, cxs, c_out, c_scratch = _ride(body, 11, 6, 0, carry, nblk)
    any_spec = pl.BlockSpec(memory_space=pl.ANY)
    outs = pl.pallas_call(
        body, name=name, grid=(nblk,),
        in_specs=[q_spec, prev, own, nxt, ctx, prev, own, nxt, ctx, sink, q_spec] + [any_spec] * len(cxs),
        out_specs=[q_spec, part, part, acc_c, acc_c, acc_s] + [any_spec] * len(cxs),
        out_shape=[jax.ShapeDtypeStruct((dm.T, dm.ATTN_W), F32),
                   jax.ShapeDtypeStruct((nblk, 2, 3, B, LANES), F32),
                   jax.ShapeDtypeStruct((nblk, 2, 3, B, LANES), F32),
                   jax.ShapeDtypeStruct((2, dm.C, LANES), F32), jax.ShapeDtypeStruct((2, dm.C, LANES), F32),
                   jax.ShapeDtypeStruct((2, GQA_RATIO, LANES), F32)] + c_out,
        scratch_shapes=c_scratch,
        compiler_params=_cparams(("arbitrary",)),
    )(q, k2, k2, k2, k2, v2, v2, v2, v2, sinkcol, do, *cxs)
    return (outs[:6], outs[6:]) if carry else outs


def _kv_reduce(part, ctxacc, dm, name):
    B = ATTN_BLOCK
    nblk = dm.T // B
    nctx = dm.C // B

    def body(own_ref, nxt_ref, prv_ref, c_ref, o_ref):
        j = pl.program_id(0)
        acc = own_ref[0, :, 0]
        acc = acc + jnp.where(j < nblk - 1, nxt_ref[0, :, 0], 0.0)
        acc = acc + jnp.where(j > 0, prv_ref[0, :, 0], 0.0)
        acc = acc + jnp.where(j < nctx, c_ref[...], 0.0)
        o_ref[...] = acc

    def spec(w, f):
        return pl.BlockSpec((1, 2, 1, B, LANES), lambda j: (f(j), 0, w, 0, 0))

    return pl.pallas_call(
        body, name=name, grid=(nblk,),
        in_specs=[spec(1, lambda j: j), spec(0, lambda j: jnp.minimum(j + 1, nblk - 1)),
                  spec(2, lambda j: jnp.maximum(j - 1, 0)),
                  pl.BlockSpec((2, B, LANES), lambda j: (0, jnp.minimum(j, nctx - 1), 0))],
        out_specs=pl.BlockSpec((2, B, LANES), lambda j: (0, j, 0)),
        out_shape=jax.ShapeDtypeStruct((2, dm.T, LANES), F32),
        compiler_params=_cparams(("parallel",)),
    )(part, part, part, ctxacc)


def _gmlp_core(z_ref, lg_ref, lb_ref, ws_ref, bs_ref, dm):
    GW = dm.GMLP_W
    gu = z_ref[:, dm.OFF_GU:dm.OFF_GV]
    gv = z_ref[:, dm.OFF_GV:dm.N_IN]
    u = _gelu(gu)
    ge = _gelu(gv)
    mu = jnp.mean(ge, axis=-1, keepdims=True)
    xc = ge - mu
    r = lax.rsqrt(jnp.mean(xc * xc, axis=-1, keepdims=True) + NORM_EPS)
    xh = xc * r
    v = xh * lg_ref[...] + lb_ref[...]
    mixed = []
    for g in range(dm.GG):
        vg = v[:, g * GMLP_GROUP_W:(g + 1) * GMLP_GROUP_W]
        mixed.append(_dot(ws_ref[g], vg, _NN) + bs_ref[g])
    return gu, gv, u, xh, r, v, mixed


def _gmlp(z, ln_g, ln_b, w_s, b_s, dm, name):
    T, GW, CH = dm.T, dm.GMLP_W, GMLP_CHUNK

    def body(z_ref, lg_ref, lb_ref, ws_ref, bs_ref, o_ref):
        _, _, u, _, _, _, mixed = _gmlp_core(z_ref, lg_ref, lb_ref, ws_ref, bs_ref, dm)
        for g in range(dm.GG):
            sl = slice(g * GMLP_GROUP_W, (g + 1) * GMLP_GROUP_W)
            o_ref[:, sl] = (u[:, sl] * mixed[g]).astype(BF16)

    vec = pl.BlockSpec((1, GW), lambda i: (0, 0))
    return pl.pallas_call(
        body, name=name, grid=(T // CH,),
        in_specs=[pl.BlockSpec((CH, dm.N_IN), lambda i: (i, 0)), vec, vec,
                  pl.BlockSpec((dm.GG, CH, CH), lambda i: (0, 0, 0)),
                  pl.BlockSpec((dm.GG, CH, 1), lambda i: (0, 0, 0))],
        out_specs=pl.BlockSpec((CH, GW), lambda i: (i, 0)),
        out_shape=jax.ShapeDtypeStruct((T, GW), BF16),
        compiler_params=_cparams(("parallel",)),
    )(z, ln_g, ln_b, w_s, b_s)


def _gmlp_bwd(z, ln_g, ln_b, w_s, b_s, do, dm, name):
    T, GW, CH = dm.T, dm.GMLP_W, GMLP_CHUNK

    def body(z_ref, lg_ref, lb_ref, ws_ref, bs_ref, do_ref, dz_ref, dlg_ref, dlb_ref, dws_ref, dbs_ref):
        i = pl.program_id(0)

        @pl.when(i == 0)
        def _():
            dlg_ref[...] = jnp.zeros_like(dlg_ref)
            dlb_ref[...] = jnp.zeros_like(dlb_ref)
            dws_ref[...] = jnp.zeros_like(dws_ref)
            dbs_ref[...] = jnp.zeros_like(dbs_ref)

        gu, gv, u, xh, r, v, mixed = _gmlp_core(z_ref, lg_ref, lb_ref, ws_ref, bs_ref, dm)
        do = do_ref[...]
        dv_parts = []
        for g in range(dm.GG):
            sl = slice(g * GMLP_GROUP_W, (g + 1) * GMLP_GROUP_W)
            dog = do[:, sl]
            dz_ref[:, sl] = dog * mixed[g] * _gelu_grad(gu[:, sl])
            dmx = dog * u[:, sl]
            dbs_ref[g] += jnp.sum(dmx, axis=-1, keepdims=True)
            dws_ref[g] += _dot(dmx, v[:, sl], _NT)
            dv_parts.append(_dot(ws_ref[g], dmx, _TN))
        dv = jnp.concatenate(dv_parts, axis=-1) if dm.GG > 1 else dv_parts[0]
        dlg_ref[...] += jnp.sum(dv * xh, axis=0, keepdims=True)
        dlb_ref[...] += jnp.sum(dv, axis=0, keepdims=True)
        dxh = dv * lg_ref[...]
        dge = r * (dxh - jnp.mean(dxh, axis=-1, keepdims=True) - xh * jnp.mean(dxh * xh, axis=-1, keepdims=True))
        dz_ref[:, GW:2 * GW] = dge * _gelu_grad(gv)

    vec = pl.BlockSpec((1, GW), lambda i: (0, 0))
    wsp = pl.BlockSpec((dm.GG, CH, CH), lambda i: (0, 0, 0))
    bsp = pl.BlockSpec((dm.GG, CH, 1), lambda i: (0, 0, 0))
    return pl.pallas_call(
        body, name=name, grid=(T // CH,),
        in_specs=[pl.BlockSpec((CH, dm.N_IN), lambda i: (i, 0)), vec, vec, wsp, bsp,
                  pl.BlockSpec((CH, GW), lambda i: (i, (dm.ATTN_W + dm.SSM_W) // GW))],
        out_specs=[pl.BlockSpec((CH, 2 * GW), lambda i: (i, 0)), vec, vec, wsp, bsp],
        out_shape=[jax.ShapeDtypeStruct((T, 2 * GW), F32), jax.ShapeDtypeStruct((1, GW), F32),
                   jax.ShapeDtypeStruct((1, GW), F32), jax.ShapeDtypeStruct((dm.GG, CH, CH), F32),
                   jax.ShapeDtypeStruct((dm.GG, CH, 1), F32)],
        compiler_params=_cparams(("arbitrary",)),
    )(z, ln_g, ln_b, w_s, b_s, do)


def _disc_fn(lam_re, lam_im, ldt, b_re, b_im):
    dt = jnp.exp(ldt)
    mag = jnp.exp(lam_re * dt)
    a_re = mag * jnp.cos(lam_im * dt)
    a_im = mag * jnp.sin(lam_im * dt)
    den = lam_re * lam_re + lam_im * lam_im
    n_re = a_re - 1.0
    f_re = (n_re * lam_re + a_im * lam_im) / den
    f_im = (a_im * lam_re - n_re * lam_im) / den
    return a_re, a_im, f_re * b_re - f_im * b_im, f_re * b_im + f_im * b_re


def _s5_disc(lam_re, lam_im, ldt, bT_re, bT_im, name):
    nd, H, GP = bT_re.shape

    def body(lr_ref, li_ref, dt_ref, br_ref, bi_ref, ar_ref, ai_ref, bbr_ref, bbi_ref):
        for d in range(nd):
            a_re, a_im, bb_re, bb_im = _disc_fn(lr_ref[d], li_ref[d], dt_ref[d], br_ref[d], bi_ref[d])
            ar_ref[d] = a_re
            ai_ref[d] = a_im
            bbr_ref[d] = bb_re
            bbi_ref[d] = bb_im

    vs = jax.ShapeDtypeStruct((nd, 1, GP), F32)
    ms = jax.ShapeDtypeStruct((nd, H, GP), F32)
    return pl.pallas_call(body, name=name, out_shape=[vs, vs, ms, ms], compiler_params=_cparams())(
        lam_re, lam_im, ldt, bT_re, bT_im)


def _s5_disc_bwd(lam_re, lam_im, ldt, bT_re, bT_im, da_re, da_im, dbb_re, dbb_im, name):
    nd, H, GP = bT_re.shape

    def body(lr_ref, li_ref, dt_ref, br_ref, bi_ref, dar_ref, dai_ref, dbr_ref, dbi_ref,
             olr_ref, oli_ref, odt_ref, obr_ref, obi_ref):
        for d in range(nd):
            _, vjp = jax.vjp(_disc_fn, lr_ref[d], li_ref[d], dt_ref[d], br_ref[d], bi_ref[d])
            g = vjp((dar_ref[d], dai_ref[d], dbr_ref[d], dbi_ref[d]))
            olr_ref[d], oli_ref[d], odt_ref[d], obr_ref[d], obi_ref[d] = g

    vs = jax.ShapeDtypeStruct((nd, 1, GP), F32)
    ms = jax.ShapeDtypeStruct((nd, H, GP), F32)
    return pl.pallas_call(body, name=name, out_shape=[vs, vs, vs, ms, ms], compiler_params=_cparams())(
        lam_re, lam_im, ldt, bT_re, bT_im, da_re, da_im, dbb_re, dbb_im)


def _scan_pass(re_ref, im_ref, nsteps, ar, ai, ir, ii, reverse, store):
    def step(n, carry):
        sr, si = carry
        t = (nsteps - 1 - n) if reverse else n
        off = pl.multiple_of(t * 8, 8)
        nr = ar * sr - ai * si + re_ref[pl.ds(off, 8), :]
        ni = ar * si + ai * sr + im_ref[pl.ds(off, 8), :]
        if store:
            re_ref[pl.ds(off, 8), :] = nr
            im_ref[pl.ds(off, 8), :] = ni
        return nr, ni

    return lax.fori_loop(0, nsteps, step, (ir, ii), unroll=4)


def _cpow(ar, ai, n):
    rr = ri = None
    br, bi = ar, ai
    while n:
        if n & 1:
            rr, ri = (br, bi) if rr is None else (rr * br - ri * bi, rr * bi + ri * br)
        n >>= 1
        if n:
            br, bi = br * br - bi * bi, 2.0 * br * bi
    return rr, ri


def _row_of(x, k):
    rows = lax.broadcasted_iota(jnp.int32, x.shape, 0)
    return jnp.sum(jnp.where(rows == k, x, 0.0), axis=0, keepdims=True)


def _full_scan(re_ref, im_ref, nsteps, ar1, ai1, h0r, h0i, reverse):
    P = ar1.shape[1]
    ar, ai = jnp.broadcast_to(ar1, (8, P)), jnp.broadcast_to(ai1, (8, P))
    z = jnp.zeros((8, P), F32)
    er, ei = _scan_pass(re_ref, im_ref, nsteps, ar, ai, z, z, reverse, False)
    pr, pi = _cpow(ar1, ai1, nsteps)
    rows = lax.broadcasted_iota(jnp.int32, (8, P), 0)
    initr, initi = z, z
    cr, ci = h0r, h0i
    for k in (range(SCAN_SEGMENTS - 1, -1, -1) if reverse else range(SCAN_SEGMENTS)):
        initr = jnp.where(rows == k, cr, initr)
        initi = jnp.where(rows == k, ci, initi)
        ekr, eki = _row_of(er, k), _row_of(ei, k)
        cr, ci = ekr + pr * cr - pi * ci, eki + pr * ci + pi * cr
    _scan_pass(re_ref, im_ref, nsteps, ar, ai, initr, initi, reverse, True)
    return initr, initi, cr, ci


def _rows_loop(n, chunk, fn):
    def step(c, carry):
        fn(pl.multiple_of(c * chunk, chunk))
        return carry
    lax.fori_loop(0, n // chunk, step, 0)


def _s5_specs(dm, PB):
    nsb = dm.GP // PB
    per = (LANES * SSM_STATE // SSM_GROUP) // PB
    ul = pl.BlockSpec((dm.L, LANES), lambda j: (0, j // per))
    uc = pl.BlockSpec((dm.C, LANES), lambda j: (0, j // per))
    av = pl.BlockSpec((2, 1, PB), lambda j: (0, 0, j))
    bd = pl.BlockSpec((2, LANES, PB), lambda j: (0, j // per, j))
    cd = pl.BlockSpec((2, PB, LANES), lambda j: (0, j, j // per))
    dv = pl.BlockSpec((1, LANES), lambda j: (0, j // per))
    return PB, nsb, per, ul, uc, av, bd, cd, dv


def _s5_scan(ul, uc, a_re, a_im, bd_re, bd_im, cd_re, cd_im, dskip, dm, name, carry=None):
    PB, nsb, per, ul_s, uc_s, av, bd, cd, dv = _s5_specs(dm, STATE_BLOCK_FWD)
    L, C = dm.L, dm.C
    RC = _tile(L, 512, 8)
    RCC = _tile(C, 512, 8)

    def body(ul_ref, uc_ref, ar_ref, ai_ref, br_ref, bi_ref, cr_ref, ci_ref, d_ref, yl_ref, yc_ref,
             lre, lim, cre, cim):
        j = pl.program_id(0)

        @pl.when(j % per == 0)
        def _():
            yl_ref[...] = ul_ref[...] * d_ref[...]
            yc_ref[...] = uc_ref[...] * d_ref[...]

        for d in range(2):
            rev = d == 1
            ar1, ai1 = ar_ref[d], ai_ref[d]

            def proj(u_ref, re, im, chunk, n):
                def f(r0):
                    u = u_ref[pl.ds(r0, chunk), :]
                    re[pl.ds(r0, chunk), :] = _dot(u, br_ref[d], _NN)
                    im[pl.ds(r0, chunk), :] = _dot(u, bi_ref[d], _NN)
                _rows_loop(n, chunk, f)

            def readout(y_ref, re, im, chunk, n):
                def f(r0):
                    y_ref[pl.ds(r0, chunk), :] += (_dot(re[pl.ds(r0, chunk), :], cr_ref[d], _NN)
                                                   - _dot(im[pl.ds(r0, chunk), :], ci_ref[d], _NN))
                _rows_loop(n, chunk, f)

            z1 = jnp.zeros((1, PB), F32)
            proj(uc_ref, cre, cim, RCC, C)
            _, _, fr, fi = _full_scan(cre, cim, C // 8, ar1, ai1, z1, z1, rev)
            readout(yc_ref, cre, cim, RCC, C)
            proj(ul_ref, lre, lim, RC, L)
            _full_scan(lre, lim, L // 8, ar1, ai1, fr, fi, rev)
            readout(yl_ref, lre, lim, RC, L)

    body, cxs, c_out, c_scratch = _ride(body, 9, 2, 4, carry, nsb)
    any_spec = pl.BlockSpec(memory_space=pl.ANY)
    outs = pl.pallas_call(
        body, name=name, grid=(nsb,),
        in_specs=[ul_s, uc_s, av, av, bd, bd, cd, cd, dv] + [any_spec] * len(cxs),
        out_specs=[ul_s, uc_s] + [any_spec] * len(cxs),
        out_shape=[jax.ShapeDtypeStruct((L, dm.SSM_W), F32), jax.ShapeDtypeStruct((C, dm.SSM_W), F32)] + c_out,
        scratch_shapes=[pltpu.VMEM((L, PB), F32), pltpu.VMEM((L, PB), F32),
                        pltpu.VMEM((C, PB), F32), pltpu.VMEM((C, PB), F32)] + c_scratch,
        compiler_params=_cparams(("arbitrary",)),
    )(ul, uc, a_re, a_im, bd_re, bd_im, cd_re, cd_im, dskip, *cxs)
    return (outs[:2], outs[2:]) if carry else outs


def _s5_scan_bwd(ul, uc, dyl, dyc, a_re, a_im, bd_re, bd_im, cd_re, cd_im, bdT_re, bdT_im, cdT_re, cdT_im,
                 dskip, dm, name, carry=None):
    PB, nsb, per, ul_s, uc_s, av, bd, cd, dv = _s5_specs(dm, STATE_BLOCK)
    L, C = dm.L, dm.C
    RC = _tile(L, 512, 8)
    RCC = _tile(C, 512, 8)
    bdT = pl.BlockSpec((2, PB, LANES), lambda j: (0, j, j // per))
    cdT = pl.BlockSpec((2, LANES, PB), lambda j: (0, j // per, j))
    dbd = pl.BlockSpec((2, 1, LANES, PB), lambda j: (0, j, 0, 0))
    dcd = pl.BlockSpec((2, 1, PB, LANES), lambda j: (0, j, 0, 0))

    def body(ul_ref, uc_ref, dyl_ref, dyc_ref, ar_ref, ai_ref, br_ref, bi_ref, cr_ref, ci_ref,
             brT_ref, biT_ref, crT_ref, ciT_ref, d_ref,
             dul_ref, duc_ref, dar_ref, dai_ref, dbr_ref, dbi_ref, dcr_ref, dci_ref, dd_ref,
             lre, lim, cre, cim, gre, gim, hre, him):
        j = pl.program_id(0)

        @pl.when(j % per == 0)
        def _():
            dul_ref[...] = dyl_ref[...] * d_ref[...]
            duc_ref[...] = dyc_ref[...] * d_ref[...]
            dd_ref[...] = (jnp.sum(dyl_ref[...] * ul_ref[...], axis=0, keepdims=True)
                           + jnp.sum(dyc_ref[...] * uc_ref[...], axis=0, keepdims=True))

        for d in range(2):
            rev = d == 1
            ar1, ai1 = ar_ref[d], ai_ref[d]
            z1 = jnp.zeros((1, PB), F32)

            def proj(u_ref, w_re, w_im, sign, re, im, chunk, n):
                def f(r0):
                    u = u_ref[pl.ds(r0, chunk), :]
                    re[pl.ds(r0, chunk), :] = _dot(u, w_re, _NN)
                    im[pl.ds(r0, chunk), :] = sign * _dot(u, w_im, _NN)
                _rows_loop(n, chunk, f)

            proj(uc_ref, br_ref[d], bi_ref[d], 1.0, cre, cim, RCC, C)
            icr, ici, fr, fi = _full_scan(cre, cim, C // 8, ar1, ai1, z1, z1, rev)
            proj(ul_ref, br_ref[d], bi_ref[d], 1.0, lre, lim, RC, L)
            ilr, ili, _, _ = _full_scan(lre, lim, L // 8, ar1, ai1, fr, fi, rev)
            proj(dyl_ref, crT_ref[d], ciT_ref[d], -1.0, gre, gim, RC, L)
            _, _, gfr, gfi = _full_scan(gre, gim, L // 8, ar1, -ai1, z1, z1, not rev)
            proj(dyc_ref, crT_ref[d], ciT_ref[d], -1.0, hre, him, RCC, C)
            _full_scan(hre, him, C // 8, ar1, -ai1, gfr, gfi, not rev)

            dar = jnp.zeros((8, PB), F32)
            dai = jnp.zeros((8, PB), F32)
            dbr = jnp.zeros((LANES, PB), F32)
            dbi = jnp.zeros((LANES, PB), F32)
            dcr = jnp.zeros((PB, LANES), F32)
            dci = jnp.zeros((PB, LANES), F32)
            for (u_ref, dy_ref, du_ref, sre, sim, are, aim, inr, ini, n, chunk) in (
                    (ul_ref, dyl_ref, dul_ref, lre, lim, gre, gim, ilr, ili, L, RC),
                    (uc_ref, dyc_ref, duc_ref, cre, cim, hre, him, icr, ici, C, RCC)):
                nst = n // 8

                def da_step(t, carry, sre=sre, sim=sim, are=are, aim=aim, nst=nst):
                    xr, xi = carry
                    tp = (t + 1) if rev else (t - 1)
                    o = pl.multiple_of(t * 8, 8)
                    op = pl.multiple_of(tp * 8, 8)
                    g_r, g_i = are[pl.ds(o, 8), :], aim[pl.ds(o, 8), :]
                    p_r, p_i = sre[pl.ds(op, 8), :], sim[pl.ds(op, 8), :]
                    return xr + g_r * p_r + g_i * p_i, xi + g_i * p_r - g_r * p_i

                lo_t, hi_t = (0, nst - 1) if rev else (1, nst)
                dar, dai = lax.fori_loop(lo_t, hi_t, da_step, (dar, dai))
                e0 = (nst - 1) * 8 if rev else 0
                g_r, g_i = are[pl.ds(e0, 8), :], aim[pl.ds(e0, 8), :]
                dar = dar + g_r * inr + g_i * ini
                dai = dai + g_i * inr - g_r * ini

                def mats(c, carry, u_ref=u_ref, dy_ref=dy_ref, du_ref=du_ref, sre=sre, sim=sim, are=are, aim=aim,
                         chunk=chunk):
                    xbr, xbi, xcr, xci = carry
                    r0 = pl.multiple_of(c * chunk, chunk)
                    u = u_ref[pl.ds(r0, chunk), :]
                    dy = dy_ref[pl.ds(r0, chunk), :]
                    g_r, g_i = are[pl.ds(r0, chunk), :], aim[pl.ds(r0, chunk), :]
                    du_ref[pl.ds(r0, chunk), :] += _dot(g_r, brT_ref[d], _NN) + _dot(g_i, biT_ref[d], _NN)
                    xbr = xbr + _dot(u, g_r, _TN)
                    xbi = xbi + _dot(u, g_i, _TN)
                    xcr = xcr + _dot(sre[pl.ds(r0, chunk), :], dy, _TN)
                    xci = xci - _dot(sim[pl.ds(r0, chunk), :], dy, _TN)
                    return xbr, xbi, xcr, xci

                dbr, dbi, dcr, dci = lax.fori_loop(0, n // chunk, mats, (dbr, dbi, dcr, dci))
            dar_ref[d] = jnp.sum(dar, axis=0, keepdims=True)
            dai_ref[d] = jnp.sum(dai, axis=0, keepdims=True)
            dbr_ref[d, 0] = dbr
            dbi_ref[d, 0] = dbi
            dcr_ref[d, 0] = dcr
            dci_ref[d, 0] = dci

    W, GP = dm.SSM_W, dm.GP
    body, cxs, c_out, c_scratch = _ride(body, 15, 9, 8, carry, nsb)
    any_spec = pl.BlockSpec(memory_space=pl.ANY)
    outs = pl.pallas_call(
        body, name=name, grid=(nsb,),
        in_specs=[ul_s, uc_s, ul_s, uc_s, av, av, bd, bd, cd, cd, bdT, bdT, cdT, cdT, dv] + [any_spec] * len(cxs),
        out_specs=[ul_s, uc_s, av, av, dbd, dbd, dcd, dcd, dv] + [any_spec] * len(cxs),
        out_shape=[jax.ShapeDtypeStruct((L, W), F32), jax.ShapeDtypeStruct((C, W), F32),
                   jax.ShapeDtypeStruct((2, 1, GP), F32), jax.ShapeDtypeStruct((2, 1, GP), F32),
                   jax.ShapeDtypeStruct((2, nsb, LANES, PB), F32), jax.ShapeDtypeStruct((2, nsb, LANES, PB), F32),
                   jax.ShapeDtypeStruct((2, nsb, PB, LANES), F32), jax.ShapeDtypeStruct((2, nsb, PB, LANES), F32),
                   jax.ShapeDtypeStruct((1, W), F32)] + c_out,
        scratch_shapes=[pltpu.VMEM((L, PB), F32), pltpu.VMEM((L, PB), F32),
                        pltpu.VMEM((C, PB), F32), pltpu.VMEM((C, PB), F32),
                        pltpu.VMEM((L, PB), F32), pltpu.VMEM((L, PB), F32),
                        pltpu.VMEM((C, PB), F32), pltpu.VMEM((C, PB), F32)] + c_scratch,
        compiler_params=_cparams(("arbitrary",)),
    )(ul, uc, dyl, dyc, a_re, a_im, bd_re, bd_im, cd_re, cd_im, bdT_re, bdT_im, cdT_re, cdT_im, dskip, *cxs)
    return (outs[:9], outs[9:]) if carry else outs


def _glu(y, w, b, name):
    T, W = y.shape
    tm = _tile(T, 544, 16)

    def body(y_ref, w_ref, b_ref, o_ref):
        g = _gelu(y_ref[...])
        o_ref[...] = (g * _sigmoid(_dot(g, w_ref[...], _NN) + b_ref[...])).astype(BF16)

    return pl.pallas_call(
        body, name=name, grid=(T // tm,),
        in_specs=[pl.BlockSpec((tm, W), lambda i: (i, 0)), pl.BlockSpec((W, W), lambda i: (0, 0)),
                  pl.BlockSpec((1, W), lambda i: (0, 0))],
        out_specs=pl.BlockSpec((tm, W), lambda i: (i, 0)),
        out_shape=jax.ShapeDtypeStruct((T, W), BF16),
        compiler_params=_cparams(("parallel",)),
    )(y, w, b)


def _glu_bwd(y, w, b, do, do_col, name):
    T, W = y.shape
    tm = _tile(T, 544, 16)

    def body(y_ref, w_ref, b_ref, do_ref, dy_ref, dw_ref, db_ref):
        i = pl.program_id(0)

        @pl.when(i == 0)
        def _():
            dw_ref[...] = jnp.zeros_like(dw_ref)
            db_ref[...] = jnp.zeros_like(db_ref)

        y = y_ref[...]
        do = do_ref[...]
        g = _gelu(y)
        s = _sigmoid(_dot(g, w_ref[...], _NN) + b_ref[...])
        dpre = do * g * s * (1.0 - s)
        db_ref[...] += jnp.sum(dpre, axis=0, keepdims=True)
        dw_ref[...] += _dot(g, dpre, _TN)
        dg = do * s + _dot(dpre, w_ref[...], _NT)
        dy_ref[...] = dg * _gelu_grad(y)

    row = pl.BlockSpec((tm, W), lambda i: (i, 0))
    wsp = pl.BlockSpec((W, W), lambda i: (0, 0))
    vec = pl.BlockSpec((1, W), lambda i: (0, 0))
    return pl.pallas_call(
        body, name=name, grid=(T // tm,),
        in_specs=[row, wsp, vec, pl.BlockSpec((tm, W), lambda i: (i, do_col))], out_specs=[row, wsp, vec],
        out_shape=[jax.ShapeDtypeStruct((T, W), F32), jax.ShapeDtypeStruct((W, W), F32),
                   jax.ShapeDtypeStruct((1, W), F32)],
        compiler_params=_cparams(("arbitrary",)),
    )(y, w, b, do)


def _loss_head(X, g, target, C, name):
    T, D = X.shape
    tm = _tile(math.gcd(C, T - C), 512, 16)
    nc = C // tm

    def body(x_ref, g_ref, t_ref, loss_ref, dx_ref, dg_ref):
        i = pl.program_id(0)

        @pl.when(i == 0)
        def _():
            loss_ref[...] = jnp.zeros_like(loss_ref)
            dg_ref[...] = jnp.zeros_like(dg_ref)

        @pl.when(i < nc)
        def _():
            dx_ref[...] = jnp.zeros_like(dx_ref)

        @pl.when(i >= nc)
        def _():
            x = x_ref[...]
            gv = g_ref[...]
            r = lax.rsqrt(jnp.mean(x * x, axis=-1, keepdims=True) + NORM_EPS)
            xh = x * r
            err = xh * gv - t_ref[...]
            loss_ref[...] += 0.5 * jnp.sum(jnp.mean(err * err, axis=-1, keepdims=True), axis=0, keepdims=True)
            dy = err * (1.0 / D)
            dg_ref[...] += jnp.sum(dy * xh, axis=0, keepdims=True)
            dxh = dy * gv
            dx_ref[...] = r * (dxh - xh * jnp.mean(dxh * xh, axis=-1, keepdims=True))

    row = pl.BlockSpec((tm, D), lambda i: (i, 0))
    return pl.pallas_call(
        body, name=name, grid=(T // tm,),
        in_specs=[row, pl.BlockSpec((1, D), lambda i: (0, 0)),
                  pl.BlockSpec((tm, D), lambda i: (jnp.maximum(i - nc, 0), 0))],
        out_specs=[pl.BlockSpec((1, 1), lambda i: (0, 0)), row, pl.BlockSpec((1, D), lambda i: (0, 0))],
        out_shape=[jax.ShapeDtypeStruct((1, 1), F32), jax.ShapeDtypeStruct((T, D), F32),
                   jax.ShapeDtypeStruct((1, D), F32)],
        compiler_params=_cparams(("arbitrary",)),
    )(X, g, target)


def _ada_fwd(cond, w, b, name):
    nl, D, Ns = w.shape
    tn = _tile(Ns, 1024, LANES)

    def body(c_ref, w_ref, b_ref, o_ref):
        c = c_ref[...]
        o_ref[0] = _dot(c * _sigmoid(c), w_ref[0], _NN) + b_ref[0]

    return pl.pallas_call(
        body, name=name, grid=(nl, Ns // tn),
        in_specs=[pl.BlockSpec((16, D), lambda l, j: (0, 0)), pl.BlockSpec((1, D, tn), lambda l, j: (l, 0, j)),
                  pl.BlockSpec((1, 1, tn), lambda l, j: (l, 0, j))],
        out_specs=pl.BlockSpec((1, 16, tn), lambda l, j: (l, 0, j)),
        out_shape=jax.ShapeDtypeStruct((nl, 16, Ns), F32),
        compiler_params=_cparams(("parallel", "parallel")),
    )(cond, w, b)


def _ada_bwd(cond, w, dmod, name):
    nl, D, Ns = w.shape
    tn = _tile(Ns, 1024, LANES)

    def body(c_ref, w_ref, dm_ref, dw_ref, da_ref):
        j = pl.program_id(1)

        @pl.when(j == 0)
        def _():
            da_ref[...] = jnp.zeros_like(da_ref)

        c = c_ref[...]
        dw_ref[0] = _dot(c * _sigmoid(c), dm_ref[0], _TN)
        da_ref[0] += _dot(dm_ref[0], w_ref[0], _NT)

    return pl.pallas_call(
        body, name=name, grid=(nl, Ns // tn),
        in_specs=[pl.BlockSpec((16, D), lambda l, j: (0, 0)), pl.BlockSpec((1, D, tn), lambda l, j: (l, 0, j)),
                  pl.BlockSpec((1, 16, tn), lambda l, j: (l, 0, j))],
        out_specs=[pl.BlockSpec((1, D, tn), lambda l, j: (l, 0, j)), pl.BlockSpec((1, 16, D), lambda l, j: (l, 0, 0))],
        out_shape=[jax.ShapeDtypeStruct((nl, D, Ns), F32), jax.ShapeDtypeStruct((nl, 16, D), F32)],
        compiler_params=_cparams(("parallel", "arbitrary")),
    )(cond, w, dmod)


def _cctx_grad(c_ctx, dact4, name):
    nch, nl, _, D = dact4.shape

    def body(c_ref, da_ref, o_ref):
        acc = jnp.zeros((1, D), F32)
        for ch in range(nch):
            for l in range(nl):
                acc = acc + da_ref[ch, l, 8:9, :]
        c = c_ref[...]
        s = _sigmoid(c)
        o_ref[...] = acc * (s + c * s * (1.0 - s))

    return pl.pallas_call(body, name=name, out_shape=jax.ShapeDtypeStruct((1, D), F32),
                          compiler_params=_cparams())(c_ctx, dact4)


def _adamw(w, g, m, v, name):
    R, W = w.shape
    tr = _tile(R, max(16, (1 << 19) // W // 16 * 16), 16 if g.dtype == BF16 else 8)
    c1 = 1.0 - ADAM_B1 ** ADAM_STEP
    c2 = 1.0 - ADAM_B2 ** ADAM_STEP

    def body(w_ref, g_ref, m_ref, v_ref, d_ref, nm_ref, nv_ref, g32_ref):
        g = g_ref[...].astype(F32)
        g32_ref[...] = g
        m = ADAM_B1 * m_ref[...] + (1.0 - ADAM_B1) * g
        v = ADAM_B2 * v_ref[...] + (1.0 - ADAM_B2) * (g * g)
        nm_ref[...] = m
        nv_ref[...] = v
        d_ref[...] = -ADAM_LR * ((m / c1) / (jnp.sqrt(v / c2) + ADAM_EPS) + ADAM_WD * w_ref[...])

    blk = pl.BlockSpec((tr, W), lambda i: (i, 0))
    s = jax.ShapeDtypeStruct((R, W), F32)
    return pl.pallas_call(
        body, name=name, grid=(R // tr,), in_specs=[blk] * 4, out_specs=[blk] * 4, out_shape=[s, s, s, s],
        compiler_params=_cparams(("parallel",)),
    )(w, g, m, v)


def _adamw_nd(w, g, m, v, name):
    shp = w.shape
    two = (math.prod(shp[:-1]), shp[-1])
    outs = _adamw(w.reshape(two), g.reshape(two), m.reshape(two), v.reshape(two), name)
    return [o.reshape(shp) for o in outs]


def _interleave(a, n):
    return a.reshape(SCAN_SEGMENTS, n // SCAN_SEGMENTS, -1).transpose(1, 0, 2).reshape(n, -1)


def _deinterleave(a, n):
    return a.reshape(n // SCAN_SEGMENTS, SCAN_SEGMENTS, -1).transpose(1, 0, 2).reshape(n, -1)


def _blockdiag_in(bbT, dm):
    eye = jnp.eye(dm.SG, dtype=F32)
    b4 = bbT.reshape(2, SSM_GROUP, dm.SG, SSM_STATE)
    return jnp.einsum("dhgp,gk->dghkp", b4, eye).reshape(2, dm.SSM_W, dm.GP)


def _blockdiag_in_T(dbd, dm):
    gpb = STATE_BLOCK // SSM_STATE
    per = (LANES // SSM_GROUP) // gpb
    d8 = dbd.reshape(2, dm.SSM_W // LANES, per, per, gpb, SSM_GROUP, gpb, SSM_STATE)
    x = jnp.einsum("duaabhbp->duabhp", d8).reshape(2, dm.SG, SSM_GROUP, SSM_STATE)
    return x.transpose(0, 2, 1, 3).reshape(2, SSM_GROUP, dm.GP)


def _blockdiag_out(c, dm):
    eye = jnp.eye(dm.SG, dtype=F32)
    return jnp.einsum("dghp,gk->dgpkh", c, eye).reshape(2, dm.GP, dm.SSM_W)


def _blockdiag_out_T(dcd, dm):
    gpb = STATE_BLOCK // SSM_STATE
    per = (LANES // SSM_GROUP) // gpb
    d8 = dcd.reshape(2, dm.SSM_W // LANES, per, gpb, SSM_STATE, per, gpb, SSM_GROUP)
    return jnp.einsum("duabpabh->duabhp", d8).reshape(2, dm.SG, SSM_GROUP, SSM_STATE)


_BIG = ("w_in", "w_out", "ssm_w_glu", "ffn_w_up", "ffn_w_down")
_SHARD_AXIS = {"w_in": 2, "w_out": 1, "ssm_w_glu": 1, "ffn_w_up": 2, "ffn_w_down": 1}
_SMALL = ("g_mix", "g_ffn", "attn_sink", "ssm_lambda_re", "ssm_lambda_im", "ssm_log_dt", "ssm_b_re", "ssm_b_im",
          "ssm_c_re", "ssm_c_im", "ssm_d", "ssm_b_glu", "gmlp_ln_g", "gmlp_ln_b", "gmlp_w_s", "gmlp_b_s",
          "ffn_conv_b", "g_final", "ffn_conv_w")
_WEIGHTS = ("c_ctx", "w_ada", "b_ada", "g_mix", "g_ffn", "w_in", "w_out", "attn_sink", "ssm_lambda_re",
            "ssm_lambda_im", "ssm_log_dt", "ssm_b_re", "ssm_b_im", "ssm_c_re", "ssm_c_im", "ssm_d", "ssm_w_glu",
            "ssm_b_glu", "gmlp_ln_g", "gmlp_ln_b", "gmlp_w_s", "gmlp_b_s", "ffn_w_up", "ffn_conv_w", "ffn_conv_b",
            "ffn_w_down", "g_final")


def _step(P, M, V, x, c, ctx, loss_target):
    dm = _Dims()
    D, T, C, L = dm.D, dm.T, dm.C, dm.L
    mx, my, mc = lax.axis_index("x"), lax.axis_index("y"), lax.axis_index("c")
    chip = 2 * mx + my
    dev = 2 * chip + mc

    conv_sh = P["ffn_conv_w"].shape
    small_f = _pack([c, P["ffn_conv_w"]], F32, 8)
    small_all = _allgather8(small_f, "ag_small_fwd")
    c_all, conv_all = [], []
    for d in range(N_DEV):
        cd_, cw_ = _unpack(small_all[d], [(1, D), conv_sh])
        c_all.append(cd_)
        conv_all.append(cw_)
    conv_w = jnp.concatenate([conv_all[2 * j] for j in range(N_CHIPS)], axis=2)
    cond = jnp.concatenate(c_all + [P["c_ctx"].reshape(1, D), jnp.zeros((16 - N_DEV - 1, D), F32)], axis=0)

    n_sh = P["w_ada"].shape[2]
    b_sh = lax.dynamic_slice_in_dim(P["b_ada"], chip * n_sh, n_sh, axis=1)[:, None, :]
    mods_sh = _ada_fwd(cond, P["w_ada"], b_sh, "ada_fwd")
    mods4 = _comm(mods_sh, group="chips", by_peer=False, name="ag_mods")
    mods = jnp.concatenate([mods4[j] for j in range(N_CHIPS)], axis=-1)
    mod_lat = lax.dynamic_index_in_dim(mods, dev, axis=1, keepdims=False)
    mod_ctx = mods[:, N_DEV]
    modp = jnp.stack([mod_ctx, mod_lat], axis=1).reshape(DEPTH, 2, N_MOD, D)

    wb = {n: P[n].astype(BF16) for n in _BIG}

    def layer_weights(g):
        return dict(w_in=jnp.concatenate([g["w_in"][j] for j in range(N_CHIPS)], axis=1),
                    w_glu=g["ssm_w_glu"].reshape(dm.SSM_W, dm.SSM_W),
                    w_out=_Mat(g["w_out"], axis=0), w_dn=_Mat(g["ffn_w_down"], axis=0),
                    w_up=_Mat(g["ffn_w_up"], axis=1))

    g0 = _comm_alone("chips_ag", [wb[n][0] for n in _BIG], "ag_w_chips")
    Wl = [layer_weights(dict(zip(_BIG, g0)))]
    cos, sin = _rope_tables(dm)

    X = jnp.concatenate([ctx.reshape(C, D), x.reshape(L, D)], axis=0)
    saved = []
    for l in range(DEPTH):
        mp = modp[l]
        sh_a, sc_a, gt_a, sh_f, sc_f, gt_f = [mp[:, k] for k in range(N_MOD)]
        w_in, w_out, w_glu, w_up, w_dn = [Wl[l][k] for k in ("w_in", "w_out", "w_glu", "w_up", "w_dn")]
        g_mix, g_ffn = P["g_mix"][l][None], P["g_ffn"][l][None]

        h = _normmod(X, g_mix, sh_a, sc_a, C, "normmod_a")
        nxt = {}

        def riding(*names):
            return ("chips_ag", [wb[n][l + 1] for n in names]) if l + 1 < DEPTH else None

        def landed(names, gathered):
            nxt.update(zip(names, gathered) if l + 1 < DEPTH else ())

        z = _matmul(h, w_in, mode="nn", name="mm_in")
        q, k2, v2 = _attn_prep(z, cos, sin, dm, "attn_prep")
        sinkcol = jnp.repeat(P["attn_sink"][l].reshape(2, GQA_RATIO), ATTN_BLOCK, axis=1)[..., None]
        o_attn = _attention(q, k2, v2, sinkcol, dm, "attention", carry=riding("w_in"))
        if l + 1 < DEPTH:
            o_attn, got = o_attn
            landed(("w_in",), got)
        u = z[:, dm.OFF_S:dm.OFF_GU]
        ul, uc = _interleave(u[C:], L), _interleave(u[:C], C)
        lam_re = P["ssm_lambda_re"][l].reshape(2, 1, dm.GP)
        lam_im = P["ssm_lambda_im"][l].reshape(2, 1, dm.GP)
        ldt = jnp.repeat(P["ssm_log_dt"][l], SSM_STATE, axis=-1).reshape(2, 1, dm.GP)
        bT_re = P["ssm_b_re"][l].reshape(2, dm.GP, SSM_GROUP).transpose(0, 2, 1)
        bT_im = P["ssm_b_im"][l].reshape(2, dm.GP, SSM_GROUP).transpose(0, 2, 1)
        a_re, a_im, bbT_re, bbT_im = _s5_disc(lam_re, lam_im, ldt, bT_re, bT_im, "s5_disc")
        bd_re, bd_im = _blockdiag_in(bbT_re, dm).astype(BF16), _blockdiag_in(bbT_im, dm).astype(BF16)
        cd_re = _blockdiag_out(P["ssm_c_re"][l], dm).astype(BF16)
        cd_im = _blockdiag_out(P["ssm_c_im"][l], dm).astype(BF16)
        dskip = P["ssm_d"][l][None]
        ys = _s5_scan(ul, uc, a_re, a_im, bd_re, bd_im, cd_re, cd_im, dskip, dm, "s5_scan",
                      carry=riding("w_out", "ssm_w_glu"))
        if l + 1 < DEPTH:
            ys, got = ys
            landed(("w_out", "ssm_w_glu"), got)
        yl, yc = ys
        y = jnp.concatenate([_deinterleave(yc, C), _deinterleave(yl, L)], axis=0)
        b_glu = P["ssm_b_glu"][l][None]
        o_ssm = _glu(y, w_glu, b_glu, "glu")
        ln_g, ln_b = P["gmlp_ln_g"][l][None], P["gmlp_ln_b"][l][None]
        w_s = P["gmlp_w_s"][l].astype(BF16)
        b_s = P["gmlp_b_s"][l][..., None]
        o_gmlp = _gmlp(z, ln_g, ln_b, w_s, b_s, dm, "gmlp")
        o_cat = jnp.concatenate([o_attn, o_ssm, o_gmlp], axis=-1)
        mix, X1 = _matmul_residual(o_cat, w_out, X, gt_a, C, "mm_out")
        h2 = _normmod(X1, g_ffn, sh_f, sc_f, C, "normmod_f")
        up = _matmul(h2, w_up, mode="nn", name="mm_up", out_dtype=BF16, out_split=(1, 2),
                     carry=riding("ffn_w_up"))
        if l + 1 < DEPTH:
            up, got = up
            landed(("ffn_w_up",), got)
        cw, cb = conv_w[l], P["ffn_conv_b"][l][None]
        act = _convact(up, cw, cb, C, "convact")
        ffn, X2, *got = _matmul_residual(act, w_dn, X1, gt_f, C, "mm_down", carry=riding("ffn_w_down"))
        landed(("ffn_w_down",), got[0] if got else ())
        if l + 1 < DEPTH:
            Wl.append(layer_weights(nxt))
        saved.append(dict(X=X, h=h, z=z, q=q, k2=k2, v2=v2, sinkcol=sinkcol, ul=ul, uc=uc, a_re=a_re, a_im=a_im,
                          bd_re=bd_re, bd_im=bd_im, cd_re=cd_re, cd_im=cd_im, y=y, o_cat=o_cat, mix=mix, X1=X1, h2=h2,
                          up=up, act=act, ffn=ffn, lam_re=lam_re, lam_im=lam_im, ldt=ldt, bT_re=bT_re,
                          bT_im=bT_im))
        X = X2

    loss_l, dX, dg_final = _loss_head(X, P["g_final"][None], loss_target.reshape(L, D), C, "loss_head")
    loss = lax.psum(loss_l[0, 0], ("x", "y", "c"))

    G = {n: [None] * DEPTH for n in _WEIGHTS if n not in ("c_ctx", "w_ada", "b_ada", "g_final")}
    dmod = [None] * DEPTH
    Gred = {n: [None] * DEPTH for n in _BIG}
    ready = None
    sent = {}

    def hosted(fn, store, kind, src, names):
        if src is None:
            return fn(None)
        out, got = fn((kind, [src[n] for n in names]))
        store.update(zip(names, got))
        return out

    def reduce_parts(parts, names=_BIG):
        return {n: _sum_slots(parts[n], BF16, "sum_all") for n in names}

    def file_shared(shared, layer):
        for n in _BIG:
            Gred[n][layer] = shared[n].reshape(P[n].shape[1:])

    for l in reversed(range(DEPTH)):
        S = saved[l]
        parts, shared = dict(sent), {}
        mp = modp[l]
        sh_a, sc_a, gt_a, sh_f, sc_f, gt_f = [mp[:, k] for k in range(N_MOD)]
        w_in, w_out, w_glu, w_up, w_dn = [Wl[l][k] for k in ("w_in", "w_out", "w_glu", "w_up", "w_dn")]
        g_mix, g_ffn = P["g_mix"][l][None], P["g_ffn"][l][None]
        cw, cb = conv_w[l], P["ffn_conv_b"][l][None]

        dffn, dgt_f = _gate_bwd(dX, S["ffn"], gt_f, C, "gate_bwd")
        dact = hosted(lambda c: _matmul(dffn, w_dn, mode="nt", name="mm_down_dx", out_dtype=BF16, carry=c),
                      parts, "all_rs", ready, ("w_in",))
        G["ffn_w_down"][l] = _matmul(S["act"], dffn, mode="tn", name="mm_down_dw", out_dtype=BF16,
                                     out_split=(0, N_CHIPS), tm_cap=1408, tk_cap=2176)
        dup, dcw, dcb = _convact_bwd(S["up"], cw, cb, dact, C, "convact_bwd")
        G["ffn_conv_w"][l], G["ffn_conv_b"][l] = dcw, dcb[0]
        dh2 = _matmul(_Mat(dup, axis=1), w_up, mode="nt", name="mm_up_dx", tk_cap=2816)
        own = {}
        G["ffn_w_up"][l] = hosted(
            lambda c: _matmul(S["h2"], _Mat(dup, axis=1), mode="tn", name="mm_up_dw", out_dtype=BF16,
                              out_split=(1, N_CHIPS), tk_cap=2176, carry=c),
            own, "all_rs", {"ffn_w_down": G["ffn_w_down"][l]}, ("ffn_w_down",))
        red = reduce_parts(parts, ("w_in", "w_out", "ssm_w_glu", "ffn_w_down")) if ready is not None else None
        dX1, dg_f, dsh_f, dsc_f = _normmod_bwd(S["X1"], g_ffn, sc_f, dh2, dX, C, "normmod_bwd")
        G["g_ffn"][l] = dg_f[0]
        dmix, dgt_a = _gate_bwd(dX1, S["mix"], gt_a, C, "gate_bwd")
        do = hosted(lambda c: _matmul(dmix, w_out, mode="nt", name="mm_out_dx", carry=c),
                    shared, "sib_ag", red, ("w_in", "w_out", "ssm_w_glu"))
        G["w_out"][l] = hosted(
            lambda c: _matmul(S["o_cat"], dmix, mode="tn", name="mm_out_dw", out_dtype=BF16, out_split=(0, N_CHIPS),
                              tk_cap=2176, carry=c), shared, "sib_ag", red, ("ffn_w_down",))
        do_attn = do_ssm = do_gmlp = do
        ln_g, ln_b = P["gmlp_ln_g"][l][None], P["gmlp_ln_b"][l][None]
        w_s = P["gmlp_w_s"][l].astype(BF16)
        b_s = P["gmlp_b_s"][l][..., None]
        dz_g, dlg, dlb, dws, dbs = _gmlp_bwd(S["z"], ln_g, ln_b, w_s, b_s, do_gmlp, dm, "gmlp_bwd")
        G["gmlp_ln_g"][l], G["gmlp_ln_b"][l], G["gmlp_w_s"][l], G["gmlp_b_s"][l] = dlg[0], dlb[0], dws, dbs[..., 0]
        b_glu = P["ssm_b_glu"][l][None]
        dy, dwglu, dbglu = _glu_bwd(S["y"], w_glu, b_glu, do_ssm, dm.ATTN_W // dm.SSM_W, "glu_bwd")
        G["ssm_w_glu"][l], G["ssm_b_glu"][l] = dwglu.astype(BF16).reshape(N_CHIPS, -1, dm.SSM_W), dbglu[0]
        dyl, dyc = _interleave(dy[C:], L), _interleave(dy[:C], C)
        tr = lambda a: jnp.swapaxes(a, 1, 2)
        dskip = P["ssm_d"][l][None]
        (dul, duc, da_re, da_im, dbd_re, dbd_im, dcd_re, dcd_im, dd) = hosted(
            lambda c: _s5_scan_bwd(
                S["ul"], S["uc"], dyl, dyc, S["a_re"], S["a_im"], S["bd_re"], S["bd_im"], S["cd_re"], S["cd_im"],
                tr(S["bd_re"]), tr(S["bd_im"]), tr(S["cd_re"]), tr(S["cd_im"]), dskip, dm, "s5_scan_bwd", carry=c),
            parts, "all_rs", ready, ("ffn_w_up",))
        red_up = reduce_parts(parts, ("ffn_w_up",)) if ready is not None else None
        dz_s = jnp.concatenate([_deinterleave(duc, C), _deinterleave(dul, L)], axis=0)
        dlr, dli, dldt, dbTr, dbTi = _s5_disc_bwd(S["lam_re"], S["lam_im"], S["ldt"], S["bT_re"], S["bT_im"],
                                                  da_re, da_im, _blockdiag_in_T(dbd_re, dm),
                                                  _blockdiag_in_T(dbd_im, dm), "s5_disc_bwd")
        G["ssm_lambda_re"][l] = dlr.reshape(2, dm.SG, SSM_STATE)
        G["ssm_lambda_im"][l] = dli.reshape(2, dm.SG, SSM_STATE)
        G["ssm_log_dt"][l] = jnp.sum(dldt.reshape(2, dm.SG, SSM_STATE), axis=-1)
        G["ssm_b_re"][l] = dbTr.transpose(0, 2, 1).reshape(2, dm.SG, SSM_STATE, SSM_GROUP)
        G["ssm_b_im"][l] = dbTi.transpose(0, 2, 1).reshape(2, dm.SG, SSM_STATE, SSM_GROUP)
        G["ssm_c_re"][l] = _blockdiag_out_T(dcd_re, dm)
        G["ssm_c_im"][l] = _blockdiag_out_T(dcd_im, dm)
        G["ssm_d"][l] = dd[0]
        dq, pk, pv, dkc, dvc, dsk = hosted(
            lambda c: _attention_bwd(S["q"], S["k2"], S["v2"], S["sinkcol"], do_attn, dm, "attention_bwd", carry=c),
            shared, "sib_ag", red_up, ("ffn_w_up",))
        G["attn_sink"][l] = dsk[:, :, 0].reshape(dm.NH)
        dk2 = _kv_reduce(pk, dkc, dm, "kv_reduce")
        dv2 = _kv_reduce(pv, dvc, dm, "kv_reduce")
        dz_a = _attn_prep_bwd(dq, dk2, dv2, cos, sin, dm, "attn_prep_bwd")
        dz = jnp.concatenate([dz_a, dz_s, dz_g], axis=-1).astype(BF16)
        dh = hosted(lambda c: _matmul(dz, w_in, mode="nt", name="mm_in_dx", carry=c), own, "all_rs",
                    {"w_out": G["w_out"][l], "ssm_w_glu": G["ssm_w_glu"][l]}, ("w_out", "ssm_w_glu"))
        dw_in = _matmul(S["h"], dz, mode="tn", name="mm_in_dw", out_dtype=BF16, tk_cap=2176)
        n_in = dm.N_IN // N_CHIPS
        G["w_in"][l] = jnp.stack([dw_in[:, j * n_in:(j + 1) * n_in] for j in range(N_CHIPS)])
        if ready is not None:
            file_shared(shared, l + 1)
        ready = {n: G[n][l] for n in _BIG}
        sent = own
        dX, dg_a, dsh_a, dsc_a = _normmod_bwd(S["X"], g_mix, sc_a, dh, dX1, C, "normmod_bwd")
        G["g_mix"][l] = dg_a[0]
        dmod[l] = jnp.stack([dsh_a, dsc_a, dgt_a, dsh_f, dsc_f, dgt_f], axis=1)

    grad_x = dX[C:].reshape(1, L, D)
    for n in _BIG:
        G.pop(n)
    late = ("w_in", "ffn_w_up")
    parts = dict(sent, **dict(zip(late, _comm_alone("all_rs", [ready[n] for n in late], "rs_all"))))
    red = reduce_parts(parts)
    file_shared(dict(zip(_BIG, _comm_alone("sib_ag", [red[n] for n in _BIG], "rs_share"))), 0)
    G = {n: jnp.stack(v) for n, v in G.items()}
    G["g_final"] = dg_final[0]
    dmod = jnp.stack(dmod)

    small_shapes = [G[n].shape for n in _SMALL]
    gs_pack = _pack([G[n] for n in _SMALL], F32, 16 * N_DEV)
    r8 = gs_pack.shape[0] // N_DEV
    gs_parts = _comm(gs_pack.reshape(N_DEV, r8, PACK_W), group="all", by_peer=True, name="rs_small")
    gs_sum = _allgather8(_sum_slots(gs_parts, F32, "sum_small"), "ag_small_red").reshape(-1, PACK_W)
    for n, a in zip(_SMALL, _unpack(gs_sum, small_shapes)):
        G[n] = a
    allp = _allgather8(_pack([dmod], F32, 16), "ag_dmod")
    dmod_all = allp.reshape(N_DEV, -1)[:, :DEPTH * 2 * N_MOD * D]
    dmod_all = dmod_all.reshape(N_DEV, DEPTH, 2, N_MOD * D)
    dmod_ctx = _sum_slots(dmod_all[:, :, 0].reshape(N_DEV, DEPTH, N_MOD * D), F32, "sum_dmod_ctx")
    dmod16 = jnp.concatenate([jnp.swapaxes(dmod_all[:, :, 1], 0, 1), dmod_ctx[:, None],
                              jnp.zeros((DEPTH, 16 - N_DEV - 1, N_MOD * D), F32)], axis=1)
    G["b_ada"] = _sum_slots(jnp.swapaxes(dmod16, 0, 1)[:N_DEV + 1], F32, "sum_b_ada")
    dmod16_sh = lax.dynamic_slice_in_dim(dmod16, chip * n_sh, n_sh, axis=2)
    G["w_ada"], dact = _ada_bwd(cond, P["w_ada"], dmod16_sh, "ada_bwd")
    dact4 = _comm(dact, group="chips", by_peer=False, name="ag_dact")
    G["c_ctx"] = _cctx_grad(P["c_ctx"].reshape(1, D), dact4, "cctx_grad")[0]
    conv_cols = conv_sh[2]
    G["ffn_conv_w"] = lax.dynamic_slice_in_dim(G["ffn_conv_w"], chip * conv_cols, conv_cols, axis=2)

    for n in _BIG:
        G[n] = jnp.stack(Gred[n])

    delta, new_m, new_v = {}, {}, {}
    for n in _BIG + ("w_ada",):
        delta[n], new_m[n], new_v[n], G[n] = _adamw_nd(P[n], G[n], M[n], V[n], "adamw_" + n)
    rest = [n for n in _WEIGHTS if n not in _BIG + ("w_ada",)]
    shapes = [P[n].shape for n in rest]
    packed = [_pack([d[n] for n in rest], F32, 8) for d in (P, G, M, V)]
    outs = _adamw(*packed, "adamw_small")[:3]
    for dst, buf in zip((delta, new_m, new_v), outs):
        for n, a in zip(rest, _unpack(buf, shapes)):
            dst[n] = a
    return loss, grad_x, G, delta, new_m, new_v


def kernel(x, c, ctx, c_ctx, w_ada, b_ada, g_mix, g_ffn, w_in, w_out, attn_sink, ssm_lambda_re, ssm_lambda_im, ssm_log_dt, ssm_b_re, ssm_b_im, ssm_c_re, ssm_c_im, ssm_d, ssm_w_glu, ssm_b_glu, gmlp_ln_g, gmlp_ln_b, gmlp_w_s, gmlp_b_s, ffn_w_up, ffn_conv_w, ffn_conv_b, ffn_w_down, g_final, loss_target, m_c_ctx, m_w_ada, m_b_ada, m_g_mix, m_g_ffn, m_w_in, m_w_out, m_attn_sink, m_ssm_lambda_re, m_ssm_lambda_im, m_ssm_log_dt, m_ssm_b_re, m_ssm_b_im, m_ssm_c_re, m_ssm_c_im, m_ssm_d, m_ssm_w_glu, m_ssm_b_glu, m_gmlp_ln_g, m_gmlp_ln_b, m_gmlp_w_s, m_gmlp_b_s, m_ffn_w_up, m_ffn_conv_w, m_ffn_conv_b, m_ffn_w_down, m_g_final, v_c_ctx, v_w_ada, v_b_ada, v_g_mix, v_g_ffn, v_w_in, v_w_out, v_attn_sink, v_ssm_lambda_re, v_ssm_lambda_im, v_ssm_log_dt, v_ssm_b_re, v_ssm_b_im, v_ssm_c_re, v_ssm_c_im, v_ssm_d, v_ssm_w_glu, v_ssm_b_glu, v_gmlp_ln_g, v_gmlp_ln_b, v_gmlp_w_s, v_gmlp_b_s, v_ffn_w_up, v_ffn_conv_w, v_ffn_conv_b, v_ffn_w_down, v_g_final):
    env = locals()
    P = {n: env[n] for n in _WEIGHTS}
    M = {n: env["m_" + n] for n in _WEIGHTS}
    V = {n: env["v_" + n] for n in _WEIGHTS}
    loss, grad_x, G, delta, new_m, new_v = _step(P, M, V, x, c, ctx, loss_target)
    return (loss, grad_x, *[G[n] for n in _WEIGHTS], *[delta[n] for n in _WEIGHTS],
            *[new_m[n] for n in _WEIGHTS], *[new_v[n] for n in _WEIGHTS])
```
